```python
import math
import jax, jax.numpy as jnp
from jax import lax
import numpy as np

D_MODEL = 1024
BATCH = 8
SEQ = 8192
DEPTH = 2

SSD_EXPAND = 2
D_INNER = SSD_EXPAND * D_MODEL
SSD_HEAD_DIM = 64
SSD_HEADS = D_INNER // SSD_HEAD_DIM
SSD_GROUPS = 8
SSD_STATE = 128
SSD_CONV = 5
SSD_CHUNK = 128
CONV_DIM = D_INNER + 2 * SSD_GROUPS * SSD_STATE
ATTN_HEAD_DIM = 64
ATTN_Q_HEADS = D_MODEL // ATTN_HEAD_DIM
ATTN_KV_HEADS = 4
ATTN_WIDTH = ATTN_Q_HEADS * ATTN_HEAD_DIM
KV_WIDTH = ATTN_KV_HEADS * ATTN_HEAD_DIM
WINDOW = 128
BLOCK = 128
N_BUCKETS = 32
MAX_DISTANCE = 128
D_FF = 4 * D_MODEL
EPS = 1e-6
IN_SPLITS = (D_INNER, CONV_DIM, 2 * SSD_HEADS, ATTN_WIDTH, KV_WIDTH, KV_WIDTH, 2 * D_MODEL)
N_IN = sum(IN_SPLITS)

kernel_name = 'hybrid_ssd_swa_encoder'


def rmsnorm(x, g):
    xf = x.astype(jnp.float32)
    y = xf * lax.rsqrt(jnp.mean(xf * xf, axis=-1, keepdims=True) + EPS)
    return (y * g.astype(jnp.float32)).astype(x.dtype)


def split_cols(t, sizes):
    idx, acc = [], 0
    for s in sizes[:-1]:
        acc += s
        idx.append(acc)
    return jnp.split(t, idx, axis=-1)


def depthwise_conv(u, w, b):
    pad = (SSD_CONV - 1) // 2
    y = lax.conv_general_dilated(u, w.astype(u.dtype), window_strides=(1,), padding=[(pad, pad)],
                                 dimension_numbers=('NWC', 'WIO', 'NWC'),
                                 feature_group_count=u.shape[-1])
    return y + b.astype(u.dtype)


def segsum_exp(a):
    cs = jnp.cumsum(a, axis=-1)
    diff = cs[..., :, None] - cs[..., None, :]
    L = a.shape[-1]
    mask = jnp.tril(jnp.ones((L, L), dtype=bool))
    return jnp.exp(jnp.where(mask, diff, -jnp.inf))


def ssd_scan(x, dt, a, b, c):
    Bsz, S, G, R, P = x.shape
    N = b.shape[-1]
    nc, L = S // SSD_CHUNK, SSD_CHUNK
    x = x.reshape(Bsz, nc, L, G, R, P)
    dt = dt.reshape(Bsz, nc, L, G, R)
    b = b.reshape(Bsz, nc, L, G, N)
    c = c.reshape(Bsz, nc, L, G, N)
    a_dt = jnp.moveaxis(dt * a, 2, -1)
    a_cs = jnp.cumsum(a_dt, axis=-1)
    xdt = x * dt[..., None]
    decay = segsum_exp(a_dt)
    cb = jnp.einsum('bclgn,bcsgn->bcgls', c, b)
    y_diag = jnp.einsum('bcgls,bcgrls,bcsgrp->bclgrp', cb, decay, xdt)
    decay_states = jnp.exp(a_cs[..., -1:] - a_cs)
    states = jnp.einsum('bclgn,bcgrl,bclgrp->bcgrpn', b, decay_states, xdt)
    chunk_decay = jnp.exp(a_cs[..., -1])

    def step(h, inp):
        st, dec = inp
        return h * dec[..., None, None] + st, h

    h0 = jnp.zeros((Bsz, G, R, P, N), x.dtype)
    _, prev = lax.scan(step, h0, (jnp.moveaxis(states, 1, 0), jnp.moveaxis(chunk_decay, 1, 0)))
    prev = jnp.moveaxis(prev, 0, 1)
    y_off = jnp.einsum('bclgn,bcgrpn,bcgrl->bclgrp', c, prev, jnp.exp(a_cs))
    return (y_diag + y_off).reshape(Bsz, S, G, R, P)


def gated_rmsnorm(y, z, w):
    u = y * jax.nn.silu(z.astype(jnp.float32))
    ug = u.reshape(u.shape[:-1] + (SSD_GROUPS, D_INNER // SSD_GROUPS))
    ug = ug * lax.rsqrt(jnp.mean(ug * ug, axis=-1, keepdims=True) + EPS)
    return ug.reshape(u.shape) * w.astype(jnp.float32)


def ssd_branch(z, xbc, dt_raw, conv_w, conv_b, dt_bias, a_log, d_skip, norm_w, w_out):
    Bsz, S = z.shape[:2]
    G, R, P, N = SSD_GROUPS, SSD_HEADS // SSD_GROUPS, SSD_HEAD_DIM, SSD_STATE
    xbc = jax.nn.silu(depthwise_conv(xbc, conv_w, conv_b))
    xs, bs, cs = split_cols(xbc, (D_INNER, G * N, G * N))
    xs = xs.astype(jnp.float32).reshape(Bsz, S, G, R, P)
    bs = bs.astype(jnp.float32).reshape(Bsz, S, G, N)
    cs = cs.astype(jnp.float32).reshape(Bsz, S, G, N)
    dt = jax.nn.softplus(dt_raw.astype(jnp.float32).reshape(Bsz, S, 2, G, R)
                         + dt_bias.astype(jnp.float32).reshape(2, G, R))
    a = -jnp.exp(a_log.astype(jnp.float32)).reshape(2, G, R)
    y_fwd = ssd_scan(xs, dt[:, :, 0], a[0], bs, cs)
    fl = lambda t: jnp.flip(t, axis=1)
    y_bwd = fl(ssd_scan(fl(xs), fl(dt[:, :, 1]), a[1], fl(bs), fl(cs)))
    y = y_fwd + y_bwd + xs * d_skip.astype(jnp.float32).reshape(G, R, 1)
    y = gated_rmsnorm(y.reshape(Bsz, S, D_INNER), z, norm_w)
    return y.astype(z.dtype) @ w_out


def t5_bucket(rel):
    nb = N_BUCKETS // 2
    max_exact = nb // 2
    ret = jnp.where(rel > 0, nb, 0)
    n = jnp.abs(rel)
    nf = jnp.maximum(n, 1).astype(jnp.float32)
    large = max_exact + (jnp.log(nf / max_exact) / math.log(MAX_DISTANCE / max_exact)
                         * (nb - max_exact)).astype(jnp.int32)
    large = jnp.minimum(large, nb - 1)
    return ret + jnp.where(n < max_exact, n, large)


def window_attention(q, k, v, sink, rel_table):
    Bsz, S = q.shape[:2]
    nb = S // BLOCK
    rep = ATTN_Q_HEADS // ATTN_KV_HEADS
    qb = q.reshape(Bsz, nb, BLOCK, ATTN_KV_HEADS, rep, ATTN_HEAD_DIM)

    def band(t):
        t = t.reshape(Bsz, S, ATTN_KV_HEADS, ATTN_HEAD_DIM)
        t = jnp.pad(t, ((0, 0), (BLOCK, BLOCK), (0, 0), (0, 0)))
        t = t.reshape(Bsz, nb + 2, BLOCK, ATTN_KV_HEADS, ATTN_HEAD_DIM)
        return jnp.concatenate([t[:, :-2], t[:, 1:-1], t[:, 2:]], axis=2)

    kb, vb = band(k), band(v)
    logits = jnp.einsum('bnqgrd,bnkgd->bngrqk', qb, kb).astype(jnp.float32) * (ATTN_HEAD_DIM ** -0.5)
    i = jnp.arange(BLOCK)[:, None]
    j = jnp.arange(3 * BLOCK)[None, :]
    rel = j - BLOCK - i
    bias = rel_table[t5_bucket(rel)].astype(jnp.float32)
    bias = jnp.transpose(bias, (2, 0, 1)).reshape(ATTN_KV_HEADS, rep, BLOCK, 3 * BLOCK)
    kpos = jnp.arange(nb)[:, None] * BLOCK + j - BLOCK
    valid = (jnp.abs(rel) <= WINDOW)[None] & ((kpos >= 0) & (kpos < S))[:, None, :]
    logits = jnp.where(valid[None, :, None, None], logits + bias, -jnp.inf)
    sink_l = sink.astype(jnp.float32).reshape(1, 1, ATTN_KV_HEADS, rep, 1, 1)
    m = jnp.maximum(jnp.max(logits, axis=-1, keepdims=True), sink_l)
    p = jnp.exp(logits - m)
    p = p / (jnp.sum(p, axis=-1, keepdims=True) + jnp.exp(sink_l - m))
    out = jnp.einsum('bngrqk,bnkgd->bnqgrd', p.astype(v.dtype), vb)
    return out.reshape(Bsz, S, ATTN_WIDTH)


def _fwd_setup_inputs(seed: int = 0) -> dict:
    key = jax.random.key(seed)
    ks = jax.random.split(key, 24)
    f32 = jnp.float32
    nrm = lambda k, shape, s: jax.random.normal(k, shape, f32) * s
    gain = lambda k, shape: 1.0 + 0.05 * jax.random.normal(k, shape, f32)
    dt0 = jnp.exp(jax.random.uniform(ks[10], (DEPTH, 2, SSD_HEADS), f32, math.log(1e-3), math.log(1e-1)))
    return {
        'x': jax.random.normal(ks[0], (BATCH, SEQ, D_MODEL), f32),
        'pre_mix_norm': gain(ks[1], (DEPTH, D_MODEL)),
        'w_in': nrm(ks[2], (DEPTH, D_MODEL, N_IN), D_MODEL ** -0.5),
        'b_gate': nrm(ks[3], (DEPTH, 2 * D_MODEL), 0.1),
        'conv_w': nrm(ks[4], (DEPTH, SSD_CONV, 1, CONV_DIM), SSD_CONV ** -0.5),
        'conv_b': nrm(ks[5], (DEPTH, CONV_DIM), 0.02),
        'dt_bias': dt0 + jnp.log(-jnp.expm1(-dt0)),
        'a_log': jnp.log(jax.random.uniform(ks[6], (DEPTH, 2, SSD_HEADS), f32, 1.0, 16.0)),
        'd_skip': gain(ks[7], (DEPTH, SSD_HEADS)),
        'ssd_norm': gain(ks[8], (DEPTH, D_INNER)),
        'w_ssd_out': nrm(ks[9], (DEPTH, D_INNER, D_MODEL), D_INNER ** -0.5),
        'attn_sink': nrm(ks[11], (DEPTH, ATTN_Q_HEADS), 0.5),
        'rel_bias_table': nrm(ks[12], (N_BUCKETS, ATTN_Q_HEADS), 0.5),
        'w_attn_out': nrm(ks[13], (DEPTH, ATTN_WIDTH, D_MODEL), ATTN_WIDTH ** -0.5),
        'w_o': nrm(ks[14], (DEPTH, D_MODEL, D_MODEL), D_MODEL ** -0.5),
        'post_mix_norm': gain(ks[15], (DEPTH, D_MODEL)),
        'pre_mlp_norm': gain(ks[16], (DEPTH, D_MODEL)),
        'w_mlp_in': nrm(ks[17], (DEPTH, D_MODEL, D_FF), D_MODEL ** -0.5),
        'w_mlp_out': nrm(ks[18], (DEPTH, D_FF, D_MODEL), D_FF ** -0.5),
        'post_mlp_norm': gain(ks[19], (DEPTH, D_MODEL)),
    }


def _fwd_reference(x, pre_mix_norm, w_in, b_gate, conv_w, conv_b, dt_bias, a_log, d_skip, ssd_norm,
              w_ssd_out, attn_sink, rel_bias_table, w_attn_out, w_o, post_mix_norm,
              pre_mlp_norm, w_mlp_in, w_mlp_out, post_mlp_norm):
    for l in range(DEPTH):
        h = rmsnorm(x, pre_mix_norm[l])
        proj = h @ w_in[l]
        z, xbc, dt_raw, q, k, v, gates = split_cols(proj, IN_SPLITS)
        y_ssd = ssd_branch(z, xbc, dt_raw, conv_w[l], conv_b[l], dt_bias[l], a_log[l],
                           d_skip[l], ssd_norm[l], w_ssd_out[l])
        y_attn = window_attention(q, k, v, attn_sink[l], rel_bias_table) @ w_attn_out[l]
        g = jax.nn.sigmoid((gates + b_gate[l]).astype(jnp.float32)).astype(x.dtype)
        g_ssd, g_attn = g[..., :D_MODEL], g[..., D_MODEL:]
        mixed = (g_ssd * y_ssd + g_attn * y_attn) @ w_o[l]
        x = x + rmsnorm(mixed, post_mix_norm[l])
        h = rmsnorm(x, pre_mlp_norm[l])
        f = jnp.square(jax.nn.relu(h @ w_mlp_in[l])) @ w_mlp_out[l]
        x = x + rmsnorm(f, post_mlp_norm[l])
    return x


import jax as _jax
import jax.numpy as _jnp

TWIN_FORMAT = 'train_step'
FWD_PARAMS = ['x', 'pre_mix_norm', 'w_in', 'b_gate', 'conv_w', 'conv_b', 'dt_bias', 'a_log', 'd_skip', 'ssd_norm', 'w_ssd_out', 'attn_sink', 'rel_bias_table', 'w_attn_out', 'w_o', 'post_mix_norm', 'pre_mlp_norm', 'w_mlp_in', 'w_mlp_out', 'post_mlp_norm']
TWIN_WEIGHTS = ['pre_mix_norm', 'w_in', 'b_gate', 'conv_w', 'conv_b', 'dt_bias', 'a_log', 'd_skip', 'ssd_norm', 'w_ssd_out', 'attn_sink', 'rel_bias_table', 'w_attn_out', 'w_o', 'post_mix_norm', 'pre_mlp_norm', 'w_mlp_in', 'w_mlp_out', 'post_mlp_norm']
TWIN_DIFF_INPUT = 'x'
TWIN_INPUTS = ['x', 'pre_mix_norm', 'w_in', 'b_gate', 'conv_w', 'conv_b', 'dt_bias', 'a_log', 'd_skip', 'ssd_norm', 'w_ssd_out', 'attn_sink', 'rel_bias_table', 'w_attn_out', 'w_o', 'post_mix_norm', 'pre_mlp_norm', 'w_mlp_in', 'w_mlp_out', 'post_mlp_norm', 'loss_target', 'm_pre_mix_norm', 'm_w_in', 'm_b_gate', 'm_conv_w', 'm_conv_b', 'm_dt_bias', 'm_a_log', 'm_d_skip', 'm_ssd_norm', 'm_w_ssd_out', 'm_attn_sink', 'm_rel_bias_table', 'm_w_attn_out', 'm_w_o', 'm_post_mix_norm', 'm_pre_mlp_norm', 'm_w_mlp_in', 'm_w_mlp_out', 'm_post_mlp_norm', 'v_pre_mix_norm', 'v_w_in', 'v_b_gate', 'v_conv_w', 'v_conv_b', 'v_dt_bias', 'v_a_log', 'v_d_skip', 'v_ssd_norm', 'v_w_ssd_out', 'v_attn_sink', 'v_rel_bias_table', 'v_w_attn_out', 'v_w_o', 'v_post_mix_norm', 'v_pre_mlp_norm', 'v_w_mlp_in', 'v_w_mlp_out', 'v_post_mlp_norm']
TWIN_OUTPUTS = ['loss', 'grad_x', 'grad_pre_mix_norm', 'grad_w_in', 'grad_b_gate', 'grad_conv_w', 'grad_conv_b', 'grad_dt_bias', 'grad_a_log', 'grad_d_skip', 'grad_ssd_norm', 'grad_w_ssd_out', 'grad_attn_sink', 'grad_rel_bias_table', 'grad_w_attn_out', 'grad_w_o', 'grad_post_mix_norm', 'grad_pre_mlp_norm', 'grad_w_mlp_in', 'grad_w_mlp_out', 'grad_post_mlp_norm', 'delta_pre_mix_norm', 'delta_w_in', 'delta_b_gate', 'delta_conv_w', 'delta_conv_b', 'delta_dt_bias', 'delta_a_log', 'delta_d_skip', 'delta_ssd_norm', 'delta_w_ssd_out', 'delta_attn_sink', 'delta_rel_bias_table', 'delta_w_attn_out', 'delta_w_o', 'delta_post_mix_norm', 'delta_pre_mlp_norm', 'delta_w_mlp_in', 'delta_w_mlp_out', 'delta_post_mlp_norm', 'new_m_pre_mix_norm', 'new_m_w_in', 'new_m_b_gate', 'new_m_conv_w', 'new_m_conv_b', 'new_m_dt_bias', 'new_m_a_log', 'new_m_d_skip', 'new_m_ssd_norm', 'new_m_w_ssd_out', 'new_m_attn_sink', 'new_m_rel_bias_table', 'new_m_w_attn_out', 'new_m_w_o', 'new_m_post_mix_norm', 'new_m_pre_mlp_norm', 'new_m_w_mlp_in', 'new_m_w_mlp_out', 'new_m_post_mlp_norm', 'new_v_pre_mix_norm', 'new_v_w_in', 'new_v_b_gate', 'new_v_conv_w', 'new_v_conv_b', 'new_v_dt_bias', 'new_v_a_log', 'new_v_d_skip', 'new_v_ssd_norm', 'new_v_w_ssd_out', 'new_v_attn_sink', 'new_v_rel_bias_table', 'new_v_w_attn_out', 'new_v_w_o', 'new_v_post_mix_norm', 'new_v_pre_mlp_norm', 'new_v_w_mlp_in', 'new_v_w_mlp_out', 'new_v_post_mlp_norm']
TWIN_LEAF_KINDS = {'loss': 'loss', 'grad_x': 'grad_x', 'grad_pre_mix_norm': 'grad_w', 'grad_w_in': 'grad_w', 'grad_b_gate': 'grad_w', 'grad_conv_w': 'grad_w', 'grad_conv_b': 'grad_w', 'grad_dt_bias': 'grad_w', 'grad_a_log': 'grad_w', 'grad_d_skip': 'grad_w', 'grad_ssd_norm': 'grad_w', 'grad_w_ssd_out': 'grad_w', 'grad_attn_sink': 'grad_w', 'grad_rel_bias_table': 'grad_w', 'grad_w_attn_out': 'grad_w', 'grad_w_o': 'grad_w', 'grad_post_mix_norm': 'grad_w', 'grad_pre_mlp_norm': 'grad_w', 'grad_w_mlp_in': 'grad_w', 'grad_w_mlp_out': 'grad_w', 'grad_post_mlp_norm': 'grad_w', 'delta_pre_mix_norm': 'delta_w', 'delta_w_in': 'delta_w', 'delta_b_gate': 'delta_w', 'delta_conv_w': 'delta_w', 'delta_conv_b': 'delta_w', 'delta_dt_bias': 'delta_w', 'delta_a_log': 'delta_w', 'delta_d_skip': 'delta_w', 'delta_ssd_norm': 'delta_w', 'delta_w_ssd_out': 'delta_w', 'delta_attn_sink': 'delta_w', 'delta_rel_bias_table': 'delta_w', 'delta_w_attn_out': 'delta_w', 'delta_w_o': 'delta_w', 'delta_post_mix_norm': 'delta_w', 'delta_pre_mlp_norm': 'delta_w', 'delta_w_mlp_in': 'delta_w', 'delta_w_mlp_out': 'delta_w', 'delta_post_mlp_norm': 'delta_w', 'new_m_pre_mix_norm': 'new_m', 'new_m_w_in': 'new_m', 'new_m_b_gate': 'new_m', 'new_m_conv_w': 'new_m', 'new_m_conv_b': 'new_m', 'new_m_dt_bias': 'new_m', 'new_m_a_log': 'new_m', 'new_m_d_skip': 'new_m', 'new_m_ssd_norm': 'new_m', 'new_m_w_ssd_out': 'new_m', 'new_m_attn_sink': 'new_m', 'new_m_rel_bias_table': 'new_m', 'new_m_w_attn_out': 'new_m', 'new_m_w_o': 'new_m', 'new_m_post_mix_norm': 'new_m', 'new_m_pre_mlp_norm': 'new_m', 'new_m_w_mlp_in': 'new_m', 'new_m_w_mlp_out': 'new_m', 'new_m_post_mlp_norm': 'new_m', 'new_v_pre_mix_norm': 'new_v', 'new_v_w_in': 'new_v', 'new_v_b_gate': 'new_v', 'new_v_conv_w': 'new_v', 'new_v_conv_b': 'new_v', 'new_v_dt_bias': 'new_v', 'new_v_a_log': 'new_v', 'new_v_d_skip': 'new_v', 'new_v_ssd_norm': 'new_v', 'new_v_w_ssd_out': 'new_v', 'new_v_attn_sink': 'new_v', 'new_v_rel_bias_table': 'new_v', 'new_v_w_attn_out': 'new_v', 'new_v_w_o': 'new_v', 'new_v_post_mix_norm': 'new_v', 'new_v_pre_mlp_norm': 'new_v', 'new_v_w_mlp_in': 'new_v', 'new_v_w_mlp_out': 'new_v', 'new_v_post_mlp_norm': 'new_v'}


def _forward(args):
    return _fwd_reference(*[args[k] for k in FWD_PARAMS])


def _output_shape():
    def fwd():
        inp = _fwd_setup_inputs(0)
        return _fwd_reference(*[inp[k] for k in FWD_PARAMS])
    out = _jax.eval_shape(fwd)
    return out.shape, out.dtype

N_MICROBATCH = 1
ADAM_LR = 0.001
ADAM_B1 = 0.9
ADAM_B2 = 0.999
ADAM_EPS = 1e-08
ADAM_WD = 0.01
ADAM_STEP = 10
PER_EXAMPLE_BATCH_AXIS = {'x': 0, 'loss_target': 0}
SHARED_INPUTS = []
_WEIGHT_DTYPES = {'pre_mix_norm': _jnp.float32, 'w_in': _jnp.float32, 'b_gate': _jnp.float32, 'conv_w': _jnp.float32, 'conv_b': _jnp.float32, 'dt_bias': _jnp.float32, 'a_log': _jnp.float32, 'd_skip': _jnp.float32, 'ssd_norm': _jnp.float32, 'w_ssd_out': _jnp.float32, 'attn_sink': _jnp.float32, 'rel_bias_table': _jnp.float32, 'w_attn_out': _jnp.float32, 'w_o': _jnp.float32, 'post_mix_norm': _jnp.float32, 'pre_mlp_norm': _jnp.float32, 'w_mlp_in': _jnp.float32, 'w_mlp_out': _jnp.float32, 'post_mlp_norm': _jnp.float32}
MOMENT_SCALE = {'pre_mix_norm': 8.542546e+00, 'w_in': 3.030213e+00, 'b_gate': 4.151385e+00, 'conv_w': 4.186011e+00, 'conv_b': 1.315965e+01, 'dt_bias': 3.689204e+00, 'a_log': 1.729484e+01, 'd_skip': 1.887461e+01, 'ssd_norm': 9.390970e+00, 'w_ssd_out': 1.244019e+01, 'attn_sink': 1.540505e-01, 'rel_bias_table': 2.014839e-01, 'w_attn_out': 7.268085e+00, 'w_o': 1.470683e+01, 'post_mix_norm': 6.672516e+01, 'pre_mlp_norm': 6.912993e+00, 'w_mlp_in': 3.446107e+00, 'w_mlp_out': 1.990666e+01, 'post_mlp_norm': 7.065559e+01}


def _to_microbatches(a, axis):
    t = _jnp.moveaxis(a, axis, 0)
    t = t.reshape((N_MICROBATCH, t.shape[0] // N_MICROBATCH) + t.shape[1:])
    return _jnp.moveaxis(t, 1, axis + 1)


def setup_inputs(seed: int = 0) -> dict:
    inp = _fwd_setup_inputs(seed)
    key = _jax.random.fold_in(_jax.random.key(seed), 7919)
    shape, _ = _output_shape()
    out = dict(inp)
    out["loss_target"] = _jax.random.normal(_jax.random.fold_in(key, 0), shape, _jnp.float32)
    for i, name in enumerate(TWIN_WEIGHTS):
        w = inp[name].astype(_jnp.float32)
        if MOMENT_SCALE is None:
            s = _jnp.sqrt(_jnp.mean(_jnp.square(w)) + 1e-30)
        else:
            s = MOMENT_SCALE[name]
        km, kv = _jax.random.split(_jax.random.fold_in(key, i + 1))
        out[name] = w
        out["m_" + name] = s * _jax.random.normal(km, w.shape, _jnp.float32)
        out["v_" + name] = (s * s) * _jax.random.uniform(kv, w.shape, _jnp.float32, 0.5, 1.5)
    if N_MICROBATCH > 1:
        for name, axis in PER_EXAMPLE_BATCH_AXIS.items():
            out[name] = _to_microbatches(out[name], axis)
    return {'x': out['x'], 'pre_mix_norm': out['pre_mix_norm'], 'w_in': out['w_in'], 'b_gate': out['b_gate'], 'conv_w': out['conv_w'], 'conv_b': out['conv_b'], 'dt_bias': out['dt_bias'], 'a_log': out['a_log'], 'd_skip': out['d_skip'], 'ssd_norm': out['ssd_norm'], 'w_ssd_out': out['w_ssd_out'], 'attn_sink': out['attn_sink'], 'rel_bias_table': out['rel_bias_table'], 'w_attn_out': out['w_attn_out'], 'w_o': out['w_o'], 'post_mix_norm': out['post_mix_norm'], 'pre_mlp_norm': out['pre_mlp_norm'], 'w_mlp_in': out['w_mlp_in'], 'w_mlp_out': out['w_mlp_out'], 'post_mlp_norm': out['post_mlp_norm'], 'loss_target': out['loss_target'], 'm_pre_mix_norm': out['m_pre_mix_norm'], 'm_w_in': out['m_w_in'], 'm_b_gate': out['m_b_gate'], 'm_conv_w': out['m_conv_w'], 'm_conv_b': out['m_conv_b'], 'm_dt_bias': out['m_dt_bias'], 'm_a_log': out['m_a_log'], 'm_d_skip': out['m_d_skip'], 'm_ssd_norm': out['m_ssd_norm'], 'm_w_ssd_out': out['m_w_ssd_out'], 'm_attn_sink': out['m_attn_sink'], 'm_rel_bias_table': out['m_rel_bias_table'], 'm_w_attn_out': out['m_w_attn_out'], 'm_w_o': out['m_w_o'], 'm_post_mix_norm': out['m_post_mix_norm'], 'm_pre_mlp_norm': out['m_pre_mlp_norm'], 'm_w_mlp_in': out['m_w_mlp_in'], 'm_w_mlp_out': out['m_w_mlp_out'], 'm_post_mlp_norm': out['m_post_mlp_norm'], 'v_pre_mix_norm': out['v_pre_mix_norm'], 'v_w_in': out['v_w_in'], 'v_b_gate': out['v_b_gate'], 'v_conv_w': out['v_conv_w'], 'v_conv_b': out['v_conv_b'], 'v_dt_bias': out['v_dt_bias'], 'v_a_log': out['v_a_log'], 'v_d_skip': out['v_d_skip'], 'v_ssd_norm': out['v_ssd_norm'], 'v_w_ssd_out': out['v_w_ssd_out'], 'v_attn_sink': out['v_attn_sink'], 'v_rel_bias_table': out['v_rel_bias_table'], 'v_w_attn_out': out['v_w_attn_out'], 'v_w_o': out['v_w_o'], 'v_post_mix_norm': out['v_post_mix_norm'], 'v_pre_mlp_norm': out['v_pre_mlp_norm'], 'v_w_mlp_in': out['v_w_mlp_in'], 'v_w_mlp_out': out['v_w_mlp_out'], 'v_post_mlp_norm': out['v_post_mlp_norm']}


def _loss(weights, diff, rest, loss_target):
    with _jax.named_scope("forward"):
        args = {**rest, TWIN_DIFF_INPUT: diff, **{k: w.astype(_WEIGHT_DTYPES[k]) for k, w in weights.items()}}
        y = _forward(args)
    with _jax.named_scope("loss_head"):
        err = _jnp.square(y.astype(_jnp.float32) - loss_target)
        return 0.5 * _jnp.sum(_jnp.mean(err, axis=-1)) if err.ndim else 0.5 * err


def _adamw(w, g, m, v):
    m = ADAM_B1 * m + (1.0 - ADAM_B1) * g
    v = ADAM_B2 * v + (1.0 - ADAM_B2) * _jnp.square(g)
    m_hat = m / (1.0 - ADAM_B1 ** ADAM_STEP)
    v_hat = v / (1.0 - ADAM_B2 ** ADAM_STEP)
    delta = -ADAM_LR * (m_hat / (_jnp.sqrt(v_hat) + ADAM_EPS) + ADAM_WD * w)
    return delta, m, v


def reference(x, pre_mix_norm, w_in, b_gate, conv_w, conv_b, dt_bias, a_log, d_skip, ssd_norm, w_ssd_out, attn_sink, rel_bias_table, w_attn_out, w_o, post_mix_norm, pre_mlp_norm, w_mlp_in, w_mlp_out, post_mlp_norm, loss_target, m_pre_mix_norm, m_w_in, m_b_gate, m_conv_w, m_conv_b, m_dt_bias, m_a_log, m_d_skip, m_ssd_norm, m_w_ssd_out, m_attn_sink, m_rel_bias_table, m_w_attn_out, m_w_o, m_post_mix_norm, m_pre_mlp_norm, m_w_mlp_in, m_w_mlp_out, m_post_mlp_norm, v_pre_mix_norm, v_w_in, v_b_gate, v_conv_w, v_conv_b, v_dt_bias, v_a_log, v_d_skip, v_ssd_norm, v_w_ssd_out, v_attn_sink, v_rel_bias_table, v_w_attn_out, v_w_o, v_post_mix_norm, v_pre_mlp_norm, v_w_mlp_in, v_w_mlp_out, v_post_mlp_norm):
    given = dict(x=x, pre_mix_norm=pre_mix_norm, w_in=w_in, b_gate=b_gate, conv_w=conv_w, conv_b=conv_b, dt_bias=dt_bias, a_log=a_log, d_skip=d_skip, ssd_norm=ssd_norm, w_ssd_out=w_ssd_out, attn_sink=attn_sink, rel_bias_table=rel_bias_table, w_attn_out=w_attn_out, w_o=w_o, post_mix_norm=post_mix_norm, pre_mlp_norm=pre_mlp_norm, w_mlp_in=w_mlp_in, w_mlp_out=w_mlp_out, post_mlp_norm=post_mlp_norm, loss_target=loss_target, m_pre_mix_norm=m_pre_mix_norm, m_w_in=m_w_in, m_b_gate=m_b_gate, m_conv_w=m_conv_w, m_conv_b=m_conv_b, m_dt_bias=m_dt_bias, m_a_log=m_a_log, m_d_skip=m_d_skip, m_ssd_norm=m_ssd_norm, m_w_ssd_out=m_w_ssd_out, m_attn_sink=m_attn_sink, m_rel_bias_table=m_rel_bias_table, m_w_attn_out=m_w_attn_out, m_w_o=m_w_o, m_post_mix_norm=m_post_mix_norm, m_pre_mlp_norm=m_pre_mlp_norm, m_w_mlp_in=m_w_mlp_in, m_w_mlp_out=m_w_mlp_out, m_post_mlp_norm=m_post_mlp_norm, v_pre_mix_norm=v_pre_mix_norm, v_w_in=v_w_in, v_b_gate=v_b_gate, v_conv_w=v_conv_w, v_conv_b=v_conv_b, v_dt_bias=v_dt_bias, v_a_log=v_a_log, v_d_skip=v_d_skip, v_ssd_norm=v_ssd_norm, v_w_ssd_out=v_w_ssd_out, v_attn_sink=v_attn_sink, v_rel_bias_table=v_rel_bias_table, v_w_attn_out=v_w_attn_out, v_w_o=v_w_o, v_post_mix_norm=v_post_mix_norm, v_pre_mlp_norm=v_pre_mlp_norm, v_w_mlp_in=v_w_mlp_in, v_w_mlp_out=v_w_mlp_out, v_post_mlp_norm=v_post_mlp_norm)
    weights = {n: given[n] for n in TWIN_WEIGHTS}
    shared = {n: given[n] for n in SHARED_INPUTS}
    per_example = {n: given[n] for n in ['x']}
    grad_fn = _jax.value_and_grad(_loss, argnums=(0, 1))

    def one_microbatch(ex, loss_target):
        ex = dict(ex)
        diff = ex.pop(TWIN_DIFF_INPUT)
        return grad_fn(weights, diff, {**shared, **ex}, loss_target)

    if N_MICROBATCH == 1:
        loss, (grad_w, grad_x) = one_microbatch(per_example, given["loss_target"])
    else:
        def body(carry, xs):
            loss_sum, grad_sum = carry
            l_k, (gw_k, gx_k) = one_microbatch(xs[0], xs[1])
            with _jax.named_scope("update"):
                return (loss_sum + l_k, _jax.tree.map(_jnp.add, grad_sum, gw_k)), gx_k

        init = (_jnp.zeros((), _jnp.float32), _jax.tree.map(_jnp.zeros_like, weights))
        (loss, grad_w), grad_x = _jax.lax.scan(body, init, (per_example, given["loss_target"]))
    with _jax.named_scope("update"):
        delta_w, new_m, new_v = {}, {}, {}
        for n in TWIN_WEIGHTS:
            delta_w[n], new_m[n], new_v[n] = _adamw(weights[n], grad_w[n], given["m_" + n], given["v_" + n])
    return (loss, grad_x, *[grad_w[n] for n in TWIN_WEIGHTS], *[delta_w[n] for n in TWIN_WEIGHTS],
            *[new_m[n] for n in TWIN_WEIGHTS], *[new_v[n] for n in TWIN_WEIGHTS])
```

```python
import math

import jax
import jax.numpy as jnp
from jax import lax
from jax.experimental import pallas as pl
from jax.experimental.pallas import tpu as pltpu

F32 = jnp.float32
BF16 = jnp.bfloat16
MESH = pl.DeviceIdType.MESH

VMEM_LIMIT_BYTES = 52 * 1024 * 1024
LANES = 128
SUBLANES = 8

EPS = 1e-6
D_MODEL = 1024
D_INNER = 2048
SSD_HEADS = 32
SSD_HEAD_DIM = 64
SSD_GROUPS = 8
SSD_STATE = 128
SSD_CONV = 5
CHUNK = 128
CONV_DIM = 4096
ATTN_HEADS = 16
ATTN_KV = 4
ATTN_DIM = 64
BLOCK = 128
N_BUCKETS = 32
MAX_DISTANCE = 128
D_FF = 4096
N_IN = 9792
NEG = -1e30

OFF_Z, OFF_G, OFF_XBC, OFF_Q, OFF_K, OFF_V, OFF_DT, N_PROJ = 0, 2048, 4096, 8192, 9216, 9472, 9728, 9856

ADAM_LR, ADAM_B1, ADAM_B2, ADAM_EPS, ADAM_WD, ADAM_STEP = 0.001, 0.9, 0.999, 1e-08, 0.01, 10

BIG_ROWS = 11264
SMALL_ROWS = 32


def _cparams(*sem):
    return pltpu.CompilerParams(dimension_semantics=sem, vmem_limit_bytes=VMEM_LIMIT_BYTES)


def _dot(a, b, precision=None):
    return lax.dot_general(a, b, (((1,), (0,)), ((), ())), preferred_element_type=F32, precision=precision)


def _dot_nt(a, b, precision=None):
    return lax.dot_general(a, b, (((1,), (1,)), ((), ())), preferred_element_type=F32, precision=precision)


def _dot_tn(a, b):
    return lax.dot_general(a, b, (((0,), (0,)), ((), ())), preferred_element_type=F32)


def _pick(n, prefs):
    for p in prefs:
        if n % p == 0:
            return p
    return n


def _matmul(name, a, b, mode, out_dtype=F32):
    if mode == "nn":
        (m, k), (_, n) = a.shape, b.shape
    elif mode == "nt":
        (m, k), (n, _) = a.shape, b.shape
    else:
        (k, m), (_, n) = a.shape, b.shape
    tm = _pick(m, (512, 256, 128))
    tn = _pick(n, (1024, 896, 512, 256, 128))
    tk = _pick(k, (1024, 896, 512, 256, 128)) if mode != "tn" else _pick(k, (512, 256, 128))
    nk = k // tk
    if mode == "nn":
        a_spec = pl.BlockSpec((tm, tk), lambda i, j, q: (i, q))
        b_spec = pl.BlockSpec((tk, tn), lambda i, j, q: (q, j))
        fn = _dot
    elif mode == "nt":
        a_spec = pl.BlockSpec((tm, tk), lambda i, j, q: (i, q))
        b_spec = pl.BlockSpec((tn, tk), lambda i, j, q: (j, q))
        fn = _dot_nt
    else:
        a_spec = pl.BlockSpec((tk, tm), lambda i, j, q: (q, i))
        b_spec = pl.BlockSpec((tk, tn), lambda i, j, q: (q, j))
        fn = _dot_tn

    def body(a_ref, b_ref, o_ref, acc_ref):
        p = fn(a_ref[...].astype(BF16), b_ref[...].astype(BF16))
        if nk == 1:
            o_ref[...] = p.astype(o_ref.dtype)
        else:
            q = pl.program_id(2)

            @pl.when(q == 0)
            def _():
                acc_ref[...] = p

            @pl.when(q > 0)
            def _():
                acc_ref[...] += p

            @pl.when(q == nk - 1)
            def _():
                o_ref[...] = acc_ref[...].astype(o_ref.dtype)

    return pl.pallas_call(
        body, name=name, grid=(m // tm, n // tn, nk),
        in_specs=[a_spec, b_spec], out_specs=pl.BlockSpec((tm, tn), lambda i, j, q: (i, j)),
        out_shape=jax.ShapeDtypeStruct((m, n), out_dtype),
        scratch_shapes=[pltpu.VMEM((tm, tn), F32)],
        compiler_params=_cparams("parallel", "parallel", "arbitrary"),
    )(a, b)


def _row(arr, width=None, cb=0, lead=None):
    return (arr, width, cb, lead)


def _rowwise(name, fn, rows, vecs, outs, accs=(), tb=256):
    t = rows[0][0].shape[-2]
    tb = min(tb, t)
    in_specs, args = [], []
    for arr, width, cb, lead in rows:
        w = arr.shape[-1] if width is None else width
        if lead is None:
            in_specs.append(pl.BlockSpec((tb, w), lambda i, cb=cb: (i, cb)))
        else:
            in_specs.append(pl.BlockSpec((None, tb, w), lambda i, cb=cb, lead=lead: (lead, i, cb)))
        args.append(arr)
    for v in vecs:
        in_specs.append(pl.BlockSpec(v.shape, lambda i, nd=v.ndim: (0,) * nd))
        args.append(v)
    out_shape = [jax.ShapeDtypeStruct((t, c), dt) for c, dt in outs] + [jax.ShapeDtypeStruct(s, F32) for s in accs]
    out_specs = [pl.BlockSpec((tb, c), lambda i: (i, 0)) for c, _ in outs] + [pl.BlockSpec(s, lambda i: (0, 0)) for s in accs]
    n_in, n_out = len(args), len(outs)

    def body(*refs):
        vals = [r[...] for r in refs[:n_in]]
        o_vals, a_vals = fn(*vals)
        for r, v in zip(refs[n_in:n_in + n_out], o_vals):
            r[...] = v.astype(r.dtype)
        first = pl.program_id(0) == 0
        for r, v in zip(refs[n_in + n_out:], a_vals):
            @pl.when(first)
            def _(r=r, v=v):
                r[...] = v

            @pl.when(jnp.logical_not(first))
            def _(r=r, v=v):
                r[...] += v

    res = pl.pallas_call(
        body, name=name, grid=(t // tb,), in_specs=in_specs, out_specs=out_specs, out_shape=out_shape,
        compiler_params=_cparams("arbitrary"),
    )(*args)
    return res


def _rms_fwd(x, g):
    r = lax.rsqrt(jnp.mean(x * x, axis=-1, keepdims=True) + EPS)
    return x * r * g


def _rms_bwd(x, g, dy):
    r = lax.rsqrt(jnp.mean(x * x, axis=-1, keepdims=True) + EPS)
    xh = x * r
    dxh = dy * g
    dx = r * (dxh - xh * jnp.mean(dxh * xh, axis=-1, keepdims=True))
    return dx, jnp.sum(dy * xh, axis=0, keepdims=True)


def _silu(x):
    return x * jax.nn.sigmoid(x)


def _silu_grad(x):
    s = jax.nn.sigmoid(x)
    return s * (1.0 + x * (1.0 - s))


GROUP_W = D_INNER // SSD_GROUPS


def _gated_norm_fwd(y, z, w):
    u = y * _silu(z)
    parts = []
    for j in range(SSD_GROUPS):
        ug = u[:, j * GROUP_W:(j + 1) * GROUP_W]
        parts.append(ug * lax.rsqrt(jnp.mean(ug * ug, axis=-1, keepdims=True) + EPS))
    return jnp.concatenate(parts, axis=-1) * w


def _gated_norm_bwd(y, z, w, dyn):
    sz = _silu(z)
    u = y * sz
    duh = dyn * w
    du_parts, uh_parts = [], []
    for j in range(SSD_GROUPS):
        sl = slice(j * GROUP_W, (j + 1) * GROUP_W)
        ug = u[:, sl]
        r = lax.rsqrt(jnp.mean(ug * ug, axis=-1, keepdims=True) + EPS)
        uh = ug * r
        dg = duh[:, sl]
        du_parts.append(r * (dg - uh * jnp.mean(dg * uh, axis=-1, keepdims=True)))
        uh_parts.append(uh)
    du = jnp.concatenate(du_parts, axis=-1)
    uh = jnp.concatenate(uh_parts, axis=-1)
    dw = jnp.sum(dyn * uh, axis=0, keepdims=True)
    return du * sz, du * y * _silu_grad(z), dw


HALO = SUBLANES


def _halo_specs(tb, cb, col0, t):
    nblk8 = t // HALO
    per = tb // HALO
    main = pl.BlockSpec((tb, cb), lambda j, i: (i, col0 + j))
    prev = pl.BlockSpec((HALO, cb), lambda j, i: (jnp.maximum(i * per - 1, 0), col0 + j))
    nxt = pl.BlockSpec((HALO, cb), lambda j, i: (jnp.minimum((i + 1) * per, nblk8 - 1), col0 + j))
    return main, prev, nxt


def _fill_ext(ext_ref, cur_ref, prev_ref, next_ref, tb, ni):
    i = pl.program_id(1)
    ext_ref[0:HALO, :] = jnp.where(i > 0, prev_ref[...], 0.0)
    ext_ref[HALO:HALO + tb, :] = cur_ref[...]
    ext_ref[HALO + tb:HALO + tb + HALO, :] = jnp.where(i < ni - 1, next_ref[...], 0.0)


def _conv_fwd(proj, w, b):
    t = proj.shape[0]
    tb, cb = min(512, t), 512
    ni, nj = t // tb, CONV_DIM // cb
    main, prev, nxt = _halo_specs(tb, cb, OFF_XBC // cb, t)
    pad = (SSD_CONV - 1) // 2

    def body(u_ref, up_ref, un_ref, w_ref, b_ref, pre_ref, act_ref, ext_ref):
        _fill_ext(ext_ref, u_ref, up_ref, un_ref, tb, ni)
        acc = jnp.broadcast_to(b_ref[...], (tb, cb))
        for k in range(SSD_CONV):
            acc = acc + w_ref[k:k + 1, :] * ext_ref[pl.ds(HALO + k - pad, tb), :]
        pre_ref[...] = acc
        act_ref[...] = _silu(acc)

    out = pl.BlockSpec((tb, cb), lambda j, i: (i, j))
    return pl.pallas_call(
        body, name="conv_fwd", grid=(nj, ni),
        in_specs=[main, prev, nxt, pl.BlockSpec((SSD_CONV, cb), lambda j, i: (0, j)), pl.BlockSpec((1, cb), lambda j, i: (0, j))],
        out_specs=[out, out],
        out_shape=[jax.ShapeDtypeStruct((t, CONV_DIM), F32)] * 2,
        scratch_shapes=[pltpu.VMEM((tb + 2 * HALO, cb), F32)],
        compiler_params=_cparams("parallel", "arbitrary"),
    )(proj, proj, proj, w, b)


def _conv_bwd(dpre, proj, w):
    t = proj.shape[0]
    tb, cb = min(512, t), 512
    ni, nj = t // tb, CONV_DIM // cb
    umain, uprev, unext = _halo_specs(tb, cb, OFF_XBC // cb, t)
    dmain, dprev, dnext = _halo_specs(tb, cb, 0, t)
    pad = (SSD_CONV - 1) // 2

    def body(d_ref, dp_ref, dn_ref, u_ref, up_ref, un_ref, w_ref, du_ref, dw_ref, db_ref, extd_ref, extu_ref):
        _fill_ext(extd_ref, d_ref, dp_ref, dn_ref, tb, ni)
        _fill_ext(extu_ref, u_ref, up_ref, un_ref, tb, ni)
        d = d_ref[...]
        du = jnp.zeros((tb, cb), F32)
        @pl.when(pl.program_id(1) == 0)
        def _():
            dw_ref[...] = jnp.zeros_like(dw_ref)
            db_ref[...] = jnp.zeros_like(db_ref)

        for k in range(SSD_CONV):
            du = du + w_ref[k:k + 1, :] * extd_ref[pl.ds(HALO - k + pad, tb), :]
            dw_ref[k:k + 1, :] += jnp.sum(d * extu_ref[pl.ds(HALO + k - pad, tb), :], axis=0, keepdims=True)
        du_ref[...] = du.astype(du_ref.dtype)
        db_ref[...] += jnp.sum(d, axis=0, keepdims=True)

    return pl.pallas_call(
        body, name="conv_bwd", grid=(nj, ni),
        in_specs=[dmain, dprev, dnext, umain, uprev, unext, pl.BlockSpec((SSD_CONV, cb), lambda j, i: (0, j))],
        out_specs=[pl.BlockSpec((tb, cb), lambda j, i: (i, j)), pl.BlockSpec((SSD_CONV, cb), lambda j, i: (0, j)),
                   pl.BlockSpec((1, cb), lambda j, i: (0, j))],
        out_shape=[jax.ShapeDtypeStruct((t, CONV_DIM), BF16), jax.ShapeDtypeStruct((SSD_CONV, CONV_DIM), F32),
                   jax.ShapeDtypeStruct((1, CONV_DIM), F32)],
        scratch_shapes=[pltpu.VMEM((tb + 2 * HALO, cb), F32)] * 2,
        compiler_params=_cparams("parallel", "arbitrary"),
    )(dpre, dpre, dpre, proj, proj, proj, w)


PAIR = 2 * SSD_HEAD_DIM
HI = lax.Precision.HIGHEST


def _ssd_prelude(d, dt_ref, dtt_ref, ar_ref, ac_ref):
    li = lax.broadcasted_iota(jnp.int32, (CHUNK, CHUNK), 0)
    si = lax.broadcasted_iota(jnp.int32, (CHUNK, CHUNK), 1)
    fwd = d == 0
    hi, lo = jnp.where(fwd, li, si), jnp.where(fwd, si, li)
    tri = hi >= lo
    trif = tri.astype(F32)
    trit = (hi <= lo).astype(F32)
    dt = dt_ref[...]
    adt = dt * ar_ref[...]
    adtt = dtt_ref[...] * ac_ref[...]
    p = _dot(trif, adt, HI)
    pt = _dot_nt(adtt, trif, HI)
    tot = jnp.sum(adt, axis=0, keepdims=True)
    return tri, trit, dt, p, pt, tot


def _ssd_specs(nc, rev):
    def cidx(d, c):
        up = (d == 1) if rev else (d == 0)
        return jnp.where(up, c, nc - 1 - c)

    specs = [
        pl.BlockSpec((CHUNK, D_INNER), lambda d, c: (cidx(d, c), 0)),
        pl.BlockSpec((CHUNK, 1024), lambda d, c: (cidx(d, c), 2)),
        pl.BlockSpec((CHUNK, 1024), lambda d, c: (cidx(d, c), 3)),
        pl.BlockSpec((None, CHUNK, SSD_HEADS), lambda d, c: (d, cidx(d, c), 0)),
        pl.BlockSpec((None, SSD_HEADS, CHUNK), lambda d, c: (d, 0, cidx(d, c))),
        pl.BlockSpec((None, 1, SSD_HEADS), lambda d, c: (d, 0, 0)),
        pl.BlockSpec((None, SSD_HEADS, 1), lambda d, c: (d, 0, 0)),
    ]
    return cidx, specs


def _ssd_fwd(act, dt2, dt2t, a_row, a_col):
    t = act.shape[0]
    nc = t // CHUNK
    cidx, specs = _ssd_specs(nc, rev=False)

    def body(xs_ref, bs_ref, cs_ref, dt_ref, dtt_ref, ar_ref, ac_ref, y_ref, st_ref, h_ref):
        d, c = pl.program_id(0), pl.program_id(1)

        @pl.when(c == 0)
        def _():
            h_ref[...] = jnp.zeros_like(h_ref)

        st_ref[...] = h_ref[...]
        tri, _, dt, p, pt, tot = _ssd_prelude(d, dt_ref, dtt_ref, ar_ref, ac_ref)
        ep = jnp.exp(p)
        w = jnp.exp(tot - p)
        etot = jnp.exp(tot)
        lane = lax.broadcasted_iota(jnp.int32, (CHUNK, PAIR), 1) < SSD_HEAD_DIM
        rowh = lax.broadcasted_iota(jnp.int32, (PAIR, SSD_STATE), 0) < SSD_HEAD_DIM
        for g in range(SSD_GROUPS):
            gs = slice(g * SSD_STATE, (g + 1) * SSD_STATE)
            bg = bs_ref[:, gs]
            bb = bg.astype(BF16)
            cb = cs_ref[:, gs].astype(BF16)
            cbm = _dot_nt(cb, bb)
            for pr in range(2):
                h0 = g * 4 + pr * 2
                h1 = h0 + 1
                sl = slice(h0 * SSD_HEAD_DIM, h0 * SSD_HEAD_DIM + PAIR)
                dtp = jnp.where(lane, dt[:, h0:h0 + 1], dt[:, h1:h1 + 1])
                xdt = (xs_ref[:, sl] * dtp).astype(BF16)
                yd = []
                st = []
                for h in (h0, h1):
                    dec = jnp.exp(jnp.where(tri, p[:, h:h + 1] - pt[h:h + 1, :], NEG))
                    yd.append(_dot((cbm * dec).astype(BF16), xdt))
                    st.append(_dot_tn(xdt, (bg * w[:, h:h + 1]).astype(BF16)))
                hin = h_ref[sl, :]
                epp = jnp.where(lane, ep[:, h0:h0 + 1], ep[:, h1:h1 + 1])
                yo = _dot_nt(cb, hin.astype(BF16)) * epp
                y_ref[:, sl] = jnp.where(lane, yd[0], yd[1]) + yo
                et = jnp.where(rowh, etot[:, h0:h0 + 1], etot[:, h1:h1 + 1])
                h_ref[sl, :] = hin * et + jnp.where(rowh, st[0], st[1])

    return pl.pallas_call(
        body, name="ssd_fwd", grid=(2, nc), in_specs=specs,
        out_specs=[pl.BlockSpec((None, CHUNK, D_INNER), lambda d, c: (d, cidx(d, c), 0)),
                   pl.BlockSpec((None, None, D_INNER, SSD_STATE), lambda d, c: (d, cidx(d, c), 0, 0))],
        out_shape=[jax.ShapeDtypeStruct((2, t, D_INNER), F32), jax.ShapeDtypeStruct((2, nc, D_INNER, SSD_STATE), F32)],
        scratch_shapes=[pltpu.VMEM((D_INNER, SSD_STATE), F32)],
        compiler_params=_cparams("arbitrary", "arbitrary"),
    )(act, act, act, dt2, dt2t, a_row, a_col)


def _put_col(acc, col, h):
    lane = lax.broadcasted_iota(jnp.int32, acc.shape, 1)
    return jnp.where(lane == h, col, acc)


def _put_row(acc, row, h):
    sub = lax.broadcasted_iota(jnp.int32, acc.shape, 0)
    return jnp.where(sub == h, row, acc)


def _ssd_bwd(act, dt2, dt2t, a_row, a_col, dy, states):
    t = act.shape[0]
    nc = t // CHUNK
    cidx, specs = _ssd_specs(nc, rev=True)
    specs = specs + [
        pl.BlockSpec((CHUNK, D_INNER), lambda d, c: (cidx(d, c), 0)),
        pl.BlockSpec((None, None, D_INNER, SSD_STATE), lambda d, c: (d, cidx(d, c), 0, 0)),
    ]

    def body(xs_ref, bs_ref, cs_ref, dt_ref, dtt_ref, ar_ref, ac_ref, dy_ref, st_ref,
             dxs_ref, dbs_ref, dcs_ref, ddt_ref, da_ref, dh_ref):
        d, c = pl.program_id(0), pl.program_id(1)

        @pl.when(c == 0)
        def _():
            dh_ref[...] = jnp.zeros_like(dh_ref)
            da_ref[...] = jnp.zeros_like(da_ref)

        tri, trit, dt, p, pt, tot = _ssd_prelude(d, dt_ref, dtt_ref, ar_ref, ac_ref)
        ep = jnp.exp(p)
        w = jnp.exp(tot - p)
        etot = jnp.exp(tot)
        lane = lax.broadcasted_iota(jnp.int32, (CHUNK, PAIR), 1) < SSD_HEAD_DIM
        rowh = lax.broadcasted_iota(jnp.int32, (PAIR, SSD_STATE), 0) < SSD_HEAD_DIM
        dp_col = jnp.zeros((CHUNK, SSD_HEADS), F32)
        dp_row = jnp.zeros((SSD_HEADS, CHUNK), F32)
        dtot = jnp.zeros((1, SSD_HEADS), F32)
        ddtx = jnp.zeros((CHUNK, SSD_HEADS), F32)
        for g in range(SSD_GROUPS):
            gs = slice(g * SSD_STATE, (g + 1) * SSD_STATE)
            bg = bs_ref[:, gs]
            bb = bg.astype(BF16)
            cb = cs_ref[:, gs].astype(BF16)
            cbm = _dot_nt(cb, bb)
            dcb = jnp.zeros((CHUNK, CHUNK), F32)
            dc_acc = jnp.zeros((CHUNK, SSD_STATE), F32)
            db_acc = jnp.zeros((CHUNK, SSD_STATE), F32)
            for pr in range(2):
                h0 = g * 4 + pr * 2
                h1 = h0 + 1
                sl = slice(h0 * SSD_HEAD_DIM, h0 * SSD_HEAD_DIM + PAIR)
                xp = xs_ref[:, sl]
                dtp = jnp.where(lane, dt[:, h0:h0 + 1], dt[:, h1:h1 + 1])
                xdt_f = xp * dtp
                xdt = xdt_f.astype(BF16)
                dyp = dy_ref[:, sl]
                dyb = dyp.astype(BF16)
                hin = st_ref[sl, :]
                dh = dh_ref[sl, :]
                hb = hin.astype(BF16)
                dhb = dh.astype(BF16)
                epp = jnp.where(lane, ep[:, h0:h0 + 1], ep[:, h1:h1 + 1])
                dye = dyp * epp
                dyeb = dye.astype(BF16)
                gy = _dot_nt(cb, hb) * dye
                dc_acc = dc_acc + _dot(dyeb, hb)
                dhin = _dot_tn(dyeb, cb)
                hh = dh * hin
                dxdt = jnp.zeros((CHUNK, PAIR), F32)
                for idx, h in enumerate((h0, h1)):
                    hm = lane if idx == 0 else jnp.logical_not(lane)
                    rm = rowh if idx == 0 else jnp.logical_not(rowh)
                    dec = jnp.exp(jnp.where(tri, p[:, h:h + 1] - pt[h:h + 1, :], NEG))
                    mf = cbm * dec
                    wc = w[:, h:h + 1]
                    t1 = _dot_tn(mf.astype(BF16), dyb)
                    t2 = _dot_nt((bg * wc).astype(BF16), dhb)
                    dxdt = jnp.where(hm, t1 + t2, dxdt)
                    dm = _dot_nt(jnp.where(hm, dyp, 0.0).astype(BF16), xdt)
                    dcb = dcb + dm * dec
                    e = dm * mf
                    q = _dot(jnp.where(hm, xdt_f, 0.0).astype(BF16), dhb)
                    db_acc = db_acc + q * wc
                    wdw = wc * jnp.sum(q * bg, axis=1, keepdims=True)
                    col = jnp.sum(e, axis=1, keepdims=True) + jnp.sum(jnp.where(hm, gy, 0.0), axis=1, keepdims=True) - wdw
                    dp_col = _put_col(dp_col, col, h)
                    dp_row = _put_row(dp_row, -jnp.sum(e, axis=0, keepdims=True), h)
                    dtot_h = jnp.sum(wdw, axis=0, keepdims=True) + etot[:, h:h + 1] * jnp.sum(
                        jnp.sum(jnp.where(rm, hh, 0.0), axis=1, keepdims=True), axis=0, keepdims=True)
                    dtot = _put_col(dtot, dtot_h, h)
                dxs_ref[:, sl] = dxdt * dtp
                ddx = dxdt * xp
                ddtx = _put_col(ddtx, jnp.sum(jnp.where(lane, ddx, 0.0), axis=1, keepdims=True), h0)
                ddtx = _put_col(ddtx, jnp.sum(jnp.where(lane, 0.0, ddx), axis=1, keepdims=True), h1)
                et = jnp.where(rowh, etot[:, h0:h0 + 1], etot[:, h1:h1 + 1])
                dh_ref[sl, :] = dh * et + dhin
            dcbb = dcb.astype(BF16)
            dcs_ref[:, gs] = _dot(dcbb, bb) + dc_acc
            dbs_ref[:, gs] = _dot_tn(dcbb, cb) + db_acc
        d_adt = _dot(trit, dp_col, HI) + _dot_nt(trit, dp_row, HI) + dtot
        ddt_ref[...] = ddtx + ar_ref[...] * d_adt
        da_ref[...] += jnp.sum(dt * d_adt, axis=0, keepdims=True)

    return pl.pallas_call(
        body, name="ssd_bwd", grid=(2, nc), in_specs=specs,
        out_specs=[pl.BlockSpec((None, CHUNK, D_INNER), lambda d, c: (d, cidx(d, c), 0)),
                   pl.BlockSpec((None, CHUNK, 1024), lambda d, c: (d, cidx(d, c), 0)),
                   pl.BlockSpec((None, CHUNK, 1024), lambda d, c: (d, cidx(d, c), 0)),
                   pl.BlockSpec((None, CHUNK, SSD_HEADS), lambda d, c: (d, cidx(d, c), 0)),
                   pl.BlockSpec((None, 1, SSD_HEADS), lambda d, c: (d, 0, 0))],
        out_shape=[jax.ShapeDtypeStruct((2, t, D_INNER), F32), jax.ShapeDtypeStruct((2, t, 1024), F32),
                   jax.ShapeDtypeStruct((2, t, 1024), F32), jax.ShapeDtypeStruct((2, t, SSD_HEADS), F32),
                   jax.ShapeDtypeStruct((2, 1, SSD_HEADS), F32)],
        scratch_shapes=[pltpu.VMEM((D_INNER, SSD_STATE), F32)],
        compiler_params=_cparams("arbitrary", "arbitrary"),
    )(act, act, act, dt2, dt2t, a_row, a_col, dy, states)


REP = ATTN_HEADS // ATTN_KV
SCALE = ATTN_DIM ** -0.5


def _attn_specs(nb):
    q = pl.BlockSpec((None, REP, BLOCK, ATTN_DIM), lambda g, n: (g, 0, n, 0))
    kp = pl.BlockSpec((None, BLOCK, ATTN_DIM), lambda g, n: (g, jnp.maximum(n - 1, 0), 0))
    kc = pl.BlockSpec((None, BLOCK, ATTN_DIM), lambda g, n: (g, n, 0))
    kn = pl.BlockSpec((None, BLOCK, ATTN_DIM), lambda g, n: (g, jnp.minimum(n + 1, nb - 1), 0))
    bias = pl.BlockSpec((None, REP, BLOCK, 3 * BLOCK), lambda g, n: (g, 0, 0, 0))
    sink = pl.BlockSpec((None, REP, 1, LANES), lambda g, n: (g, 0, 0, 0))
    return q, kp, kc, kn, bias, sink


def _attn_valid(n, nb):
    ii = lax.broadcasted_iota(jnp.int32, (BLOCK, 3 * BLOCK), 0)
    jj = lax.broadcasted_iota(jnp.int32, (BLOCK, 3 * BLOCK), 1)
    mid = (jj >= BLOCK) & (jj < 2 * BLOCK)
    lo = (jj < BLOCK) & (jj >= ii) & (n > 0)
    hi = (jj >= 2 * BLOCK) & (jj - 2 * BLOCK <= ii) & (n < nb - 1)
    return mid | lo | hi


def _attn_probs(q, kcat, bias, snk, valid):
    s = _dot_nt(q, kcat) * SCALE + bias
    s = jnp.where(valid, s, NEG)
    m = jnp.maximum(jnp.max(s, axis=1, keepdims=True), snk)
    p = jnp.where(valid, jnp.exp(s - m), 0.0)
    es = jnp.exp(snk - m)
    den = jnp.sum(p, axis=1, keepdims=True) + es
    return p / den, es / den


def _attn_fwd(q4, k3, v3, bias4, sink4):
    t = q4.shape[2]
    nb = t // BLOCK
    qs, kp, kc, kn, bs, ss = _attn_specs(nb)

    def body(q_ref, kp_ref, kc_ref, kn_ref, vp_ref, vc_ref, vn_ref, b_ref, s_ref, o_ref):
        n = pl.program_id(1)
        kcat = jnp.concatenate([kp_ref[...], kc_ref[...], kn_ref[...]], axis=0)
        vcat = jnp.concatenate([vp_ref[...], vc_ref[...], vn_ref[...]], axis=0)
        valid = _attn_valid(n, nb)
        for r in range(REP):
            pn, _ = _attn_probs(q_ref[r], kcat, b_ref[r], s_ref[r][:, 0:1], valid)
            o_ref[r] = _dot(pn.astype(BF16), vcat).astype(o_ref.dtype)

    return pl.pallas_call(
        body, name="attn_fwd", grid=(ATTN_KV, nb), in_specs=[qs, kp, kc, kn, kp, kc, kn, bs, ss],
        out_specs=qs, out_shape=jax.ShapeDtypeStruct(q4.shape, BF16),
        compiler_params=_cparams("parallel", "arbitrary"),
    )(q4, k3, k3, k3, v3, v3, v3, bias4, sink4)


def _attn_bwd(q4, k3, v3, bias4, sink4, do4):
    t = q4.shape[2]
    nb = t // BLOCK
    qs, kp, kc, kn, bs, ss = _attn_specs(nb)
    part = pl.BlockSpec((3, None, BLOCK, ATTN_DIM), lambda g, n: (0, g, n, 0))

    def body(q_ref, kp_ref, kc_ref, kn_ref, vp_ref, vc_ref, vn_ref, b_ref, s_ref, do_ref,
             dq_ref, dk_ref, dv_ref, db_ref, ds_ref):
        n = pl.program_id(1)

        @pl.when(n == 0)
        def _():
            db_ref[...] = jnp.zeros_like(db_ref)
            ds_ref[...] = jnp.zeros_like(ds_ref)

        kcat = jnp.concatenate([kp_ref[...], kc_ref[...], kn_ref[...]], axis=0)
        vcat = jnp.concatenate([vp_ref[...], vc_ref[...], vn_ref[...]], axis=0)
        valid = _attn_valid(n, nb)
        dk = jnp.zeros((3 * BLOCK, ATTN_DIM), F32)
        dv = jnp.zeros((3 * BLOCK, ATTN_DIM), F32)
        for r in range(REP):
            q = q_ref[r]
            do = do_ref[r]
            pn, psink = _attn_probs(q, kcat, b_ref[r], s_ref[r][:, 0:1], valid)
            dp = _dot_nt(do, vcat)
            delta = jnp.sum(pn * dp, axis=1, keepdims=True)
            dsc = pn * (dp - delta)
            db_ref[r] += dsc
            ds_ref[r] += jnp.broadcast_to(-jnp.sum(psink * delta, axis=0, keepdims=True), (1, LANES))
            dsb = (dsc * SCALE).astype(BF16)
            dq_ref[r] = _dot(dsb, kcat).astype(dq_ref.dtype)
            dk = dk + _dot_tn(dsb, q)
            dv = dv + _dot_tn(pn.astype(BF16), do)
        for j in range(3):
            dk_ref[j] = dk[j * BLOCK:(j + 1) * BLOCK]
            dv_ref[j] = dv[j * BLOCK:(j + 1) * BLOCK]

    return pl.pallas_call(
        body, name="attn_bwd", grid=(ATTN_KV, nb), in_specs=[qs, kp, kc, kn, kp, kc, kn, bs, ss, qs],
        out_specs=[qs, part, part, bs, ss],
        out_shape=[jax.ShapeDtypeStruct(q4.shape, BF16), jax.ShapeDtypeStruct((3,) + k3.shape, F32),
                   jax.ShapeDtypeStruct((3,) + k3.shape, F32), jax.ShapeDtypeStruct(bias4.shape, F32),
                   jax.ShapeDtypeStruct(sink4.shape, F32)],
        compiler_params=_cparams("parallel", "arbitrary"),
    )(q4, k3, k3, k3, v3, v3, v3, bias4, sink4, do4)


def _kv_combine(name, parts):
    _, kv, t, dd = parts.shape
    nb = t // BLOCK

    def body(a_ref, b_ref, c_ref, o_ref):
        n = pl.program_id(1)
        a = jnp.where(n < nb - 1, a_ref[...], 0.0)
        c = jnp.where(n > 0, c_ref[...], 0.0)
        o_ref[...] = (a + b_ref[...] + c).astype(o_ref.dtype)

    return pl.pallas_call(
        body, name=name, grid=(kv, nb),
        in_specs=[pl.BlockSpec((None, None, BLOCK, dd), lambda g, n: (0, g, jnp.minimum(n + 1, nb - 1), 0)),
                  pl.BlockSpec((None, None, BLOCK, dd), lambda g, n: (1, g, n, 0)),
                  pl.BlockSpec((None, None, BLOCK, dd), lambda g, n: (2, g, jnp.maximum(n - 1, 0), 0))],
        out_specs=pl.BlockSpec((None, BLOCK, dd), lambda g, n: (g, n, 0)),
        out_shape=jax.ShapeDtypeStruct((kv, t, dd), BF16),
        compiler_params=_cparams("parallel", "arbitrary"),
    )(parts, parts, parts)


def _t5_bucket(rel):
    nb = N_BUCKETS // 2
    max_exact = nb // 2
    ret = jnp.where(rel > 0, nb, 0)
    n = jnp.abs(rel)
    nf = jnp.maximum(n, 1).astype(jnp.float32)
    large = max_exact + (jnp.log(nf / max_exact) / math.log(MAX_DISTANCE / max_exact) * (nb - max_exact)).astype(jnp.int32)
    large = jnp.minimum(large, nb - 1)
    return ret + jnp.where(n < max_exact, n, large)


def _bucket_map():
    i = jnp.arange(BLOCK)[:, None]
    j = jnp.arange(3 * BLOCK)[None, :]
    return _t5_bucket(j - BLOCK - i)


def _bias_table_grad(dbias, bucket):
    d2 = dbias.reshape(ATTN_HEADS, BLOCK * 3 * BLOCK)
    onehot_t = (bucket.reshape(1, -1) == jnp.arange(N_BUCKETS)[:, None]).astype(F32)

    def body(d_ref, o_ref, out_ref):
        out_ref[...] = _dot_nt(d_ref[...], o_ref[...], HI)

    return pl.pallas_call(body, name="bias_table_grad", out_shape=jax.ShapeDtypeStruct((ATTN_HEADS, N_BUCKETS), F32),
                          compiler_params=pltpu.CompilerParams(vmem_limit_bytes=VMEM_LIMIT_BYTES))(d2, onehot_t)


HBM_SPEC = pl.BlockSpec(memory_space=pl.ANY)


def _allgather_chips(name, x):
    def body(x_ref, out_ref, send_sems, recv_sems, local_sem):
        mx, my, mc = lax.axis_index("x"), lax.axis_index("y"), lax.axis_index("c")
        me = 2 * mx + my
        chips = [(1 - mx, my), (mx, 1 - my), (1 - mx, 1 - my)]
        mine = pltpu.make_async_copy(x_ref, out_ref.at[me], local_sem)
        mine.start()
        sends = [pltpu.make_async_remote_copy(src_ref=x_ref, dst_ref=out_ref.at[me], send_sem=send_sems.at[k],
                                              recv_sem=recv_sems.at[k], device_id=(px, py, mc), device_id_type=MESH)
                 for k, (px, py) in enumerate(chips)]
        for cp in sends:
            cp.start()
        for k, (px, py) in enumerate(chips):
            pltpu.make_async_remote_copy(src_ref=x_ref, dst_ref=out_ref.at[2 * px + py], send_sem=send_sems.at[k],
                                         recv_sem=recv_sems.at[k], device_id=(px, py, mc), device_id_type=MESH).wait_recv()
        for cp in sends:
            cp.wait_send()
        mine.wait()

    return pl.pallas_call(
        body, name=name, in_specs=[HBM_SPEC], out_specs=HBM_SPEC,
        out_shape=jax.ShapeDtypeStruct((4,) + x.shape, x.dtype),
        scratch_shapes=[pltpu.SemaphoreType.DMA((3,)), pltpu.SemaphoreType.DMA((3,)), pltpu.SemaphoreType.DMA],
    )(x)


def _allgather_all(name, x):
    def body(x_ref, out_ref, send_sems, recv_sems, local_sem):
        mx, my, mc = lax.axis_index("x"), lax.axis_index("y"), lax.axis_index("c")
        me = 4 * mx + 2 * my + mc
        flips = [(fx, fy, fc) for fx in (0, 1) for fy in (0, 1) for fc in (0, 1)][1:]
        peers = [(mx ^ fx, my ^ fy, mc ^ fc) for fx, fy, fc in flips]
        mine = pltpu.make_async_copy(x_ref, out_ref.at[me], local_sem)
        mine.start()
        sends = [pltpu.make_async_remote_copy(src_ref=x_ref, dst_ref=out_ref.at[me], send_sem=send_sems.at[k],
                                              recv_sem=recv_sems.at[k], device_id=peer, device_id_type=MESH)
                 for k, peer in enumerate(peers)]
        for cp in sends:
            cp.start()
        for k, (px, py, pc) in enumerate(peers):
            pltpu.make_async_remote_copy(src_ref=x_ref, dst_ref=out_ref.at[4 * px + 2 * py + pc], send_sem=send_sems.at[k],
                                         recv_sem=recv_sems.at[k], device_id=(px, py, pc), device_id_type=MESH).wait_recv()
        for cp in sends:
            cp.wait_send()
        mine.wait()

    return pl.pallas_call(
        body, name=name, in_specs=[HBM_SPEC], out_specs=HBM_SPEC,
        out_shape=jax.ShapeDtypeStruct((8,) + x.shape, x.dtype),
        scratch_shapes=[pltpu.SemaphoreType.DMA((7,)), pltpu.SemaphoreType.DMA((7,)), pltpu.SemaphoreType.DMA],
    )(x)


def _scatter_halves(name, g):
    _, _, r2, c = g.shape

    def body(g_ref, out_ref, send_sems, recv_sems, local_sem):
        mx, my, mc = lax.axis_index("x"), lax.axis_index("y"), lax.axis_index("c")
        me = 4 * mx + 2 * my + mc
        flips = [(fx, fy, fc) for fx in (0, 1) for fy in (0, 1) for fc in (0, 1)][1:]
        peers = [(mx ^ fx, my ^ fy, mc ^ fc) for fx, fy, fc in flips]
        mine = pltpu.make_async_copy(g_ref.at[2 * mx + my, mc], out_ref.at[me], local_sem)
        mine.start()
        sends = [pltpu.make_async_remote_copy(src_ref=g_ref.at[2 * px + py, pc], dst_ref=out_ref.at[me],
                                              send_sem=send_sems.at[k], recv_sem=recv_sems.at[k],
                                              device_id=(px, py, pc), device_id_type=MESH)
                 for k, (px, py, pc) in enumerate(peers)]
        for cp in sends:
            cp.start()
        for k, (px, py, pc) in enumerate(peers):
            pltpu.make_async_remote_copy(src_ref=g_ref.at[0, 0], dst_ref=out_ref.at[4 * px + 2 * py + pc],
                                         send_sem=send_sems.at[k], recv_sem=recv_sems.at[k],
                                         device_id=(px, py, pc), device_id_type=MESH).wait_recv()
        for cp in sends:
            cp.wait_send()
        mine.wait()

    return pl.pallas_call(
        body, name=name, in_specs=[HBM_SPEC], out_specs=HBM_SPEC,
        out_shape=jax.ShapeDtypeStruct((8, r2, c), g.dtype),
        scratch_shapes=[pltpu.SemaphoreType.DMA((7,)), pltpu.SemaphoreType.DMA((7,)), pltpu.SemaphoreType.DMA],
    )(g)


def _allgather_cores(name, x):
    def body(x_ref, out_ref, send_sem, recv_sem, local_sem):
        mx, my, mc = lax.axis_index("x"), lax.axis_index("y"), lax.axis_index("c")
        mine = pltpu.make_async_copy(x_ref, out_ref.at[mc], local_sem)
        mine.start()
        send = pltpu.make_async_remote_copy(src_ref=x_ref, dst_ref=out_ref.at[mc], send_sem=send_sem, recv_sem=recv_sem,
                                            device_id=(mx, my, 1 - mc), device_id_type=MESH)
        send.start()
        pltpu.make_async_remote_copy(src_ref=x_ref, dst_ref=out_ref.at[1 - mc], send_sem=send_sem, recv_sem=recv_sem,
                                     device_id=(mx, my, 1 - mc), device_id_type=MESH).wait_recv()
        send.wait_send()
        mine.wait()

    return pl.pallas_call(
        body, name=name, in_specs=[HBM_SPEC], out_specs=HBM_SPEC,
        out_shape=jax.ShapeDtypeStruct((2,) + x.shape, x.dtype),
        scratch_shapes=[pltpu.SemaphoreType.DMA, pltpu.SemaphoreType.DMA, pltpu.SemaphoreType.DMA],
    )(x)


def _sum_slots(name, st, tb=256):
    n, r, c = st.shape
    tb = _pick(r, (tb, 32))

    def body(s_ref, o_ref):
        acc = s_ref[0].astype(F32)
        for k in range(1, n):
            acc = acc + s_ref[k].astype(F32)
        o_ref[...] = acc

    return pl.pallas_call(
        body, name=name, grid=(r // tb,), in_specs=[pl.BlockSpec((n, tb, c), lambda i: (0, i, 0))],
        out_specs=pl.BlockSpec((tb, c), lambda i: (i, 0)), out_shape=jax.ShapeDtypeStruct((r, c), F32),
        compiler_params=_cparams("parallel"),
    )(st)


def _adamw(name, w, g, m, v):
    def fn(w, g, m, v):
        m2 = ADAM_B1 * m + (1.0 - ADAM_B1) * g
        v2 = ADAM_B2 * v + (1.0 - ADAM_B2) * jnp.square(g)
        m_hat = m2 / (1.0 - ADAM_B1 ** ADAM_STEP)
        v_hat = v2 / (1.0 - ADAM_B2 ** ADAM_STEP)
        delta = -ADAM_LR * (m_hat / (jnp.sqrt(v_hat) + ADAM_EPS) + ADAM_WD * w)
        return [delta, m2, v2], []

    tb = _pick(w.shape[0], (256, 32))
    return _rowwise(name, fn, [_row(w), _row(g), _row(m), _row(v)], [], [(w.shape[1], F32)] * 3, tb=tb)


BIG = ("w_in", "w_ssd_out", "w_attn_out", "w_o", "w_mlp_in", "w_mlp_out", "conv_w")
SMALL = ("pre_mix_norm", "b_gate", "conv_b", "dt_bias", "a_log", "d_skip", "ssd_norm", "attn_sink", "rel_bias_table",
         "post_mix_norm", "pre_mlp_norm", "post_mlp_norm")
ALL_W = ("pre_mix_norm", "w_in", "b_gate", "conv_w", "conv_b", "dt_bias", "a_log", "d_skip", "ssd_norm", "w_ssd_out",
         "attn_sink", "rel_bias_table", "w_attn_out", "w_o", "post_mix_norm", "pre_mlp_norm", "w_mlp_in", "w_mlp_out",
         "post_mlp_norm")


def _pack_rows(parts, rows, dtype):
    flat = jnp.concatenate([p.reshape(-1, D_MODEL).astype(dtype) for p in parts], axis=0)
    return jnp.pad(flat, ((0, rows - flat.shape[0]), (0, 0)))


def _pack_big(shards, dtype):
    return _pack_rows([shards[n] for n in BIG], BIG_ROWS, dtype)


def _unpack_big(flat, like):
    out, r = {}, 0
    for n in BIG:
        shp = like[n].shape
        nr = math.prod(shp) // D_MODEL
        out[n] = flat[r:r + nr].reshape(shp)
        r += nr
    return out


def _pack_small(parts, extra=None):
    flat = jnp.concatenate([parts[n].reshape(-1).astype(F32) for n in SMALL] + ([extra.reshape(-1)] if extra is not None else []))
    return jnp.pad(flat, (0, SMALL_ROWS * D_MODEL - flat.shape[0])).reshape(SMALL_ROWS, D_MODEL)


def _unpack_small(flat2, like):
    flat = flat2.reshape(-1)
    out, r = {}, 0
    for n in SMALL:
        shp = like[n].shape
        k = math.prod(shp)
        out[n] = flat[r:r + k].reshape(shp)
        r += k
    return out, flat[r]


def _shard_of_full(name, full, s):
    if name in ("w_in", "w_mlp_in"):
        w = full.shape[2] // 4
        return full[:, :, s * w:(s + 1) * w]
    if name == "conv_w":
        w = full.shape[3] // 4
        return full[:, :, :, s * w:(s + 1) * w]
    w = full.shape[1] // 4
    return full[:, s * w:(s + 1) * w, :]


def _full_of_shards(name, shards):
    axis = {"w_in": 2, "w_mlp_in": 2, "conv_w": 3}.get(name, 1)
    return jnp.concatenate(shards, axis=axis)


def _to_proj_layout(w):
    z, xbc, dt, q, k, v, gates = (w[..., 0:2048], w[..., 2048:6144], w[..., 6144:6208], w[..., 6208:7232],
                                  w[..., 7232:7488], w[..., 7488:7744], w[..., 7744:9792])
    pad = jnp.zeros(w.shape[:-1] + (N_PROJ - N_IN,), w.dtype)
    return jnp.concatenate([z, gates, xbc, q, k, v, dt, pad], axis=-1)


def _from_proj_layout(w):
    z, gates, xbc, q, k, v, dt = (w[..., 0:2048], w[..., 2048:4096], w[..., 4096:8192], w[..., 8192:9216],
                                  w[..., 9216:9472], w[..., 9472:9728], w[..., 9728:9792])
    return jnp.concatenate([z, xbc, dt, q, k, v, gates], axis=-1)


def _heads_major(a, nh):
    t = a.shape[0]
    return a.reshape(t, nh, ATTN_DIM).transpose(1, 0, 2)


def _tokens_major(a):
    nh, t, dd = a.shape
    return a.transpose(1, 0, 2).reshape(t, nh * dd)


def _layer_fwd(h1, x, W, P, l, bias4):
    t = x.shape[0]
    S = {"x": x, "h1": h1}
    proj = _matmul("proj", h1, W["w_in"][l], "nn")
    S["proj"] = proj
    pre, act = _conv_fwd(proj, W["conv_w"][l], P["conv_b"][l].reshape(1, CONV_DIM))
    S["pre"], S["act"] = pre, act

    dtb = jnp.pad(P["dt_bias"][l].reshape(1, 2 * SSD_HEADS), ((0, 0), (0, LANES - 2 * SSD_HEADS)))

    def dt_fn(raw, b):
        v = raw + b
        return [jnp.maximum(v, 0.0) + jnp.log1p(jnp.exp(-jnp.abs(v)))], []

    (dt,) = _rowwise("dt_fwd", dt_fn, [_row(proj, LANES, OFF_DT // LANES)], [dtb], [(LANES, F32)], tb=512)
    dt2 = jnp.stack([dt[:, 0:SSD_HEADS], dt[:, SSD_HEADS:2 * SSD_HEADS]])
    dt2t = dt2.transpose(0, 2, 1)
    a = -jnp.exp(P["a_log"][l])
    a_row, a_col = a.reshape(2, 1, SSD_HEADS), a.reshape(2, SSD_HEADS, 1)
    S["dt2"], S["dt2t"], S["a_row"], S["a_col"] = dt2, dt2t, a_row, a_col
    y2, states = _ssd_fwd(act, dt2, dt2t, a_row, a_col)
    S["states"] = states

    dsk = jnp.repeat(P["d_skip"][l], SSD_HEAD_DIM).reshape(1, D_INNER)
    nw = P["ssd_norm"][l].reshape(1, D_INNER)
    S["dsk"], S["nw"] = dsk, nw

    def gn_fn(yf, yb, xs, z, dsk, nw):
        y = yf + yb + xs * dsk
        return [y, _gated_norm_fwd(y, z, nw)], []

    y, yn = _rowwise("gated_norm_fwd", gn_fn,
                     [_row(y2, lead=0), _row(y2, lead=1), _row(act, D_INNER, 0), _row(proj, D_INNER, OFF_Z // D_INNER)],
                     [dsk, nw], [(D_INNER, F32), (D_INNER, BF16)])
    S["y"], S["yn"] = y, yn
    y_ssd = _matmul("ssd_out", yn, W["w_ssd_out"][l], "nn")
    S["y_ssd"] = y_ssd

    q4 = _heads_major(proj[:, OFF_Q:OFF_Q + 1024].astype(BF16), ATTN_HEADS).reshape(ATTN_KV, REP, t, ATTN_DIM)
    k3 = _heads_major(proj[:, OFF_K:OFF_K + 256].astype(BF16), ATTN_KV)
    v3 = _heads_major(proj[:, OFF_V:OFF_V + 256].astype(BF16), ATTN_KV)
    sink4 = jnp.broadcast_to(P["attn_sink"][l].reshape(ATTN_KV, REP, 1, 1), (ATTN_KV, REP, 1, LANES))
    S["q4"], S["k3"], S["v3"], S["sink4"] = q4, k3, v3, sink4
    o4 = _attn_fwd(q4, k3, v3, bias4, sink4)
    o = _tokens_major(o4.reshape(ATTN_HEADS, t, ATTN_DIM))
    S["o"] = o
    y_attn = _matmul("attn_out", o, W["w_attn_out"][l], "nn")
    S["y_attn"] = y_attn

    bg = P["b_gate"][l].reshape(1, 2 * D_MODEL)
    S["bg"] = bg

    def merge_fn(gates, ys, ya, b):
        g = jax.nn.sigmoid(gates + b)
        return [g[:, :D_MODEL] * ys + g[:, D_MODEL:] * ya], []

    (mix_in,) = _rowwise("merge_fwd", merge_fn, [_row(proj, 2 * D_MODEL, OFF_G // (2 * D_MODEL)), _row(y_ssd), _row(y_attn)],
                         [bg], [(D_MODEL, BF16)])
    S["mix_in"] = mix_in
    mixed = _matmul("w_o", mix_in, W["w_o"][l], "nn")
    S["mixed"] = mixed

    g_pm = P["post_mix_norm"][l].reshape(1, D_MODEL)
    g_pl = P["pre_mlp_norm"][l].reshape(1, D_MODEL)

    def postmix_fn(x, mixed, g1, g2):
        x2 = x + _rms_fwd(mixed, g1)
        return [x2, _rms_fwd(x2, g2)], []

    x2, h2 = _rowwise("post_mix_fwd", postmix_fn, [_row(x), _row(mixed)], [g_pm, g_pl], [(D_MODEL, F32), (D_MODEL, BF16)])
    S["x2"], S["h2"] = x2, h2
    f1 = _matmul("mlp_in", h2, W["w_mlp_in"][l], "nn")
    S["f1"] = f1

    def relu2_fn(f):
        return [jnp.square(jnp.maximum(f, 0.0))], []

    (a1,) = _rowwise("relu2_fwd", relu2_fn, [_row(f1)], [], [(D_FF, BF16)])
    S["a1"] = a1
    f2 = _matmul("mlp_out", a1, W["w_mlp_out"][l], "nn")
    S["f2"] = f2
    return S


def _layer_bwd(S, dx3, W, P, l, bias4, bucket):
    t = dx3.shape[0]
    G = {}
    g_pmlp = P["post_mlp_norm"][l].reshape(1, D_MODEL)

    def b1_fn(f2, dx3, g):
        df2, dg = _rms_bwd(f2, g, dx3)
        return [df2], [dg]

    df2, G["post_mlp_norm"] = _rowwise("post_mlp_bwd", b1_fn, [_row(S["f2"]), _row(dx3)], [g_pmlp], [(D_MODEL, BF16)], [(1, D_MODEL)])
    da1 = _matmul("d_a1", df2, W["w_mlp_out"][l], "nt")
    G["w_mlp_out"] = _matmul("dw_mlp_out", S["a1"], df2, "tn")

    def b2_fn(da1, f1):
        return [da1 * 2.0 * jnp.maximum(f1, 0.0)], []

    (df1,) = _rowwise("relu2_bwd", b2_fn, [_row(da1), _row(S["f1"])], [], [(D_FF, BF16)])
    dh2 = _matmul("d_h2", df1, W["w_mlp_in"][l], "nt")
    G["w_mlp_in"] = _matmul("dw_mlp_in", S["h2"], df1, "tn")

    g_pm = P["post_mix_norm"][l].reshape(1, D_MODEL)
    g_pl = P["pre_mlp_norm"][l].reshape(1, D_MODEL)

    def b3_fn(x2, dh2, dx3, mixed, g_pl, g_pm):
        d1, dgl = _rms_bwd(x2, g_pl, dh2)
        dx2 = dx3 + d1
        dmixed, dgm = _rms_bwd(mixed, g_pm, dx2)
        return [dx2, dmixed], [dgl, dgm]

    dx2, dmixed, G["pre_mlp_norm"], G["post_mix_norm"] = _rowwise(
        "post_mix_bwd", b3_fn, [_row(S["x2"]), _row(dh2), _row(dx3), _row(S["mixed"])], [g_pl, g_pm],
        [(D_MODEL, F32), (D_MODEL, BF16)], [(1, D_MODEL), (1, D_MODEL)])
    dmix_in = _matmul("d_mix_in", dmixed, W["w_o"][l], "nt")
    G["w_o"] = _matmul("dw_o", S["mix_in"], dmixed, "tn")

    proj = S["proj"]

    def b4_fn(gates, ys, ya, dmix, b):
        g = jax.nn.sigmoid(gates + b)
        gs, ga = g[:, :D_MODEL], g[:, D_MODEL:]
        dg = jnp.concatenate([ys * dmix, ya * dmix], axis=-1) * g * (1.0 - g)
        return [gs * dmix, ga * dmix, dg], [jnp.sum(dg, axis=0, keepdims=True)]

    dy_ssd, dy_attn, dgates, G["b_gate"] = _rowwise(
        "merge_bwd", b4_fn, [_row(proj, 2 * D_MODEL, OFF_G // (2 * D_MODEL)), _row(S["y_ssd"]), _row(S["y_attn"]), _row(dmix_in)],
        [S["bg"]], [(D_MODEL, BF16), (D_MODEL, BF16), (2 * D_MODEL, BF16)], [(1, 2 * D_MODEL)])

    dyn = _matmul("d_yn", dy_ssd, W["w_ssd_out"][l], "nt")
    G["w_ssd_out"] = _matmul("dw_ssd_out", S["yn"], dy_ssd, "tn")
    do = _matmul("d_o", dy_attn, W["w_attn_out"][l], "nt", out_dtype=BF16)
    G["w_attn_out"] = _matmul("dw_attn_out", S["o"], dy_attn, "tn")

    do4 = _heads_major(do, ATTN_HEADS).reshape(ATTN_KV, REP, t, ATTN_DIM)
    dq4, dkp, dvp, dbias4, dsink4 = _attn_bwd(S["q4"], S["k3"], S["v3"], bias4, S["sink4"], do4)
    dq = _tokens_major(dq4.reshape(ATTN_HEADS, t, ATTN_DIM))
    dk = _tokens_major(_kv_combine("dk_combine", dkp))
    dv = _tokens_major(_kv_combine("dv_combine", dvp))
    G["attn_sink"] = dsink4[:, :, 0, 0].reshape(ATTN_HEADS)
    G["rel_bias_table"] = _bias_table_grad(dbias4.reshape(ATTN_HEADS, BLOCK, 3 * BLOCK), bucket).T

    act = S["act"]

    def b5_fn(y, z, xs, dyn, nw, dsk):
        dy, dz, dnw = _gated_norm_bwd(y, z, nw, dyn)
        return [dy, dz], [dnw, jnp.sum(dy * xs, axis=0, keepdims=True)]

    dy, dz, G["ssd_norm"], dskip_cols = _rowwise(
        "gated_norm_bwd", b5_fn, [_row(S["y"]), _row(proj, D_INNER, OFF_Z // D_INNER), _row(act, D_INNER, 0), _row(dyn)],
        [S["nw"], S["dsk"]], [(D_INNER, F32), (D_INNER, BF16)], [(1, D_INNER), (1, D_INNER)])
    G["d_skip"] = dskip_cols.reshape(SSD_HEADS, SSD_HEAD_DIM).sum(axis=-1)

    dxs2, dbs2, dcs2, ddt2, da2 = _ssd_bwd(act, S["dt2"], S["dt2t"], S["a_row"], S["a_col"], dy, S["states"])
    G["a_log"] = da2.reshape(2, SSD_HEADS) * S["a_row"].reshape(2, SSD_HEADS)

    def b6_fn(dxf, dxb, dy, dbf, dbb, dcf, dcb, pre, dsk):
        dact = jnp.concatenate([dxf + dxb + dy * dsk, dbf + dbb, dcf + dcb], axis=-1)
        return [dact * _silu_grad(pre)], []

    (dpre,) = _rowwise("silu_bwd", b6_fn,
                       [_row(dxs2, lead=0), _row(dxs2, lead=1), _row(dy), _row(dbs2, lead=0), _row(dbs2, lead=1),
                        _row(dcs2, lead=0), _row(dcs2, lead=1), _row(S["pre"])], [S["dsk"]], [(CONV_DIM, F32)], tb=128)
    du, dconv_w, dconv_b = _conv_bwd(dpre, proj, W["conv_w"][l])
    G["conv_w"] = dconv_w.reshape(SSD_CONV, 1, CONV_DIM)
    G["conv_b"] = dconv_b.reshape(CONV_DIM)

    dtb = jnp.pad(P["dt_bias"][l].reshape(1, 2 * SSD_HEADS), ((0, 0), (0, LANES - 2 * SSD_HEADS)))
    ddt = jnp.pad(jnp.concatenate([ddt2[0], ddt2[1]], axis=-1), ((0, 0), (0, LANES - 2 * SSD_HEADS)))

    def b7_fn(raw, ddt, b):
        draw = ddt * jax.nn.sigmoid(raw + b)
        return [draw], [jnp.sum(draw, axis=0, keepdims=True)]

    draw, ddtb = _rowwise("dt_bwd", b7_fn, [_row(proj, LANES, OFF_DT // LANES), _row(ddt)], [dtb], [(LANES, BF16)], [(1, LANES)], tb=512)
    G["dt_bias"] = ddtb[0, :2 * SSD_HEADS].reshape(2, SSD_HEADS)

    dproj = jnp.concatenate([dz, dgates, du, dq, dk, dv, draw], axis=-1)
    G["w_in"] = _from_proj_layout(_matmul("dw_in", S["h1"], dproj, "tn"))
    dh1 = _matmul("d_h1", dproj, W["w_in"][l], "nt")

    g_pre = P["pre_mix_norm"][l].reshape(1, D_MODEL)

    def b8_fn(x, dh1, dx2, g):
        d1, dg = _rms_bwd(x, g, dh1)
        return [dx2 + d1], [dg]

    dx, G["pre_mix_norm"] = _rowwise("pre_mix_bwd", b8_fn, [_row(S["x"]), _row(dh1), _row(dx2)], [g_pre], [(D_MODEL, F32)], [(1, D_MODEL)])
    return dx, G


def _step(x, target, shard_w, shard_m, shard_v):
    depth = shard_w["w_in"].shape[0]
    gathered = _allgather_chips("gather_weights", _pack_big(shard_w, BF16))
    conv_rows = shard_w["conv_w"].reshape(-1, D_MODEL)
    conv_g = _allgather_chips("gather_conv", jnp.pad(conv_rows, ((0, 16 - conv_rows.shape[0]), (0, 0))))
    per_chip = [_unpack_big(gathered[s], shard_w) for s in range(4)]
    W = {n: _full_of_shards(n, [per_chip[s][n] for s in range(4)]) for n in BIG if n != "conv_w"}
    W["w_in"] = _to_proj_layout(W["w_in"])
    W["conv_w"] = jnp.concatenate([conv_g[s][:conv_rows.shape[0]].reshape(shard_w["conv_w"].shape) for s in range(4)],
                                  axis=3).reshape(depth, SSD_CONV, CONV_DIM)
    P = {n: shard_w[n] for n in SMALL}
    loss_part, grad_x, full = _local_step(x, target, W, P)

    pieces = jnp.stack([_pack_big({n: _shard_of_full(n, full[n], s) for n in BIG}, BF16) for s in range(4)])
    staged = _scatter_halves("scatter_grads", pieces.reshape(4, 2, BIG_ROWS // 2, D_MODEL))
    half = _sum_slots("sum_grads", staged)
    g_big = _allgather_cores("share_grads", half).reshape(BIG_ROWS, D_MODEL)
    d_big, m_big, v_big = _adamw("adamw_big", _pack_big(shard_w, F32), g_big, _pack_big(shard_m, F32), _pack_big(shard_v, F32))

    small = _allgather_all("gather_small", _pack_small(full, loss_part))
    g_small = _sum_slots("sum_small", small, tb=SMALL_ROWS)
    d_small, m_small, v_small = _adamw("adamw_small", _pack_small(shard_w, jnp.zeros((), F32)), g_small,
                                       _pack_small(shard_m, jnp.zeros((), F32)), _pack_small(shard_v, jnp.zeros((), F32)))

    outs = {}
    for tag, big, sm in (("grad", g_big, g_small), ("delta", d_big, d_small), ("new_m", m_big, m_small), ("new_v", v_big, v_small)):
        ub = _unpack_big(big, shard_w)
        us, extra = _unpack_small(sm, shard_w)
        outs[tag] = {**ub, **us}
        if tag == "grad":
            loss = extra
    return loss, grad_x, outs


def _local_step(x, target, W, P):
    depth = W["w_in"].shape[0]
    bucket = _bucket_map()
    bias = jnp.transpose(P["rel_bias_table"][bucket], (2, 0, 1))
    bias4 = bias.reshape(ATTN_KV, REP, BLOCK, 3 * BLOCK)

    def pre_fn(x, g):
        return [_rms_fwd(x, g)], []

    (h1,) = _rowwise("pre_mix_fwd", pre_fn, [_row(x)], [P["pre_mix_norm"][0].reshape(1, D_MODEL)], [(D_MODEL, BF16)])
    saved = []
    loss_cols = dxl = None
    for l in range(depth):
        S = _layer_fwd(h1, x, W, P, l, bias4)
        saved.append(S)
        g_pmlp = P["post_mlp_norm"][l].reshape(1, D_MODEL)
        if l + 1 < depth:
            def post_fn(x2, f2, g1, g2):
                x3 = x2 + _rms_fwd(f2, g1)
                return [x3, _rms_fwd(x3, g2)], []

            x, h1 = _rowwise("post_mlp_fwd", post_fn, [_row(S["x2"]), _row(S["f2"])],
                             [g_pmlp, P["pre_mix_norm"][l + 1].reshape(1, D_MODEL)], [(D_MODEL, F32), (D_MODEL, BF16)])
        else:
            def loss_fn(x2, f2, tgt, g1):
                diff = x2 + _rms_fwd(f2, g1) - tgt
                return [diff * (1.0 / D_MODEL)], [jnp.sum(diff * diff, axis=0, keepdims=True)]

            dxl, loss_cols = _rowwise("loss", loss_fn, [_row(S["x2"]), _row(S["f2"]), _row(target)], [g_pmlp],
                                      [(D_MODEL, F32)], [(1, D_MODEL)])
    loss_part = 0.5 * jnp.sum(loss_cols) / D_MODEL

    grads = [None] * depth
    dx = dxl
    for l in reversed(range(depth)):
        dx, grads[l] = _layer_bwd(saved[l], dx, W, P, l, bias4, bucket)
    grad_x = dx

    full = {n: jnp.stack([grads[l][n] for l in range(depth)]) for n in ALL_W if n != "rel_bias_table"}
    full["rel_bias_table"] = sum(grads[l]["rel_bias_table"] for l in range(depth))
    return loss_part, grad_x, full


def kernel(x, pre_mix_norm, w_in, b_gate, conv_w, conv_b, dt_bias, a_log, d_skip, ssd_norm, w_ssd_out, attn_sink, rel_bias_table, w_attn_out, w_o, post_mix_norm, pre_mlp_norm, w_mlp_in, w_mlp_out, post_mlp_norm, loss_target, m_pre_mix_norm, m_w_in, m_b_gate, m_conv_w, m_conv_b, m_dt_bias, m_a_log, m_d_skip, m_ssd_norm, m_w_ssd_out, m_attn_sink, m_rel_bias_table, m_w_attn_out, m_w_o, m_post_mix_norm, m_pre_mlp_norm, m_w_mlp_in, m_w_mlp_out, m_post_mlp_norm, v_pre_mix_norm, v_w_in, v_b_gate, v_conv_w, v_conv_b, v_dt_bias, v_a_log, v_d_skip, v_ssd_norm, v_w_ssd_out, v_attn_sink, v_rel_bias_table, v_w_attn_out, v_w_o, v_post_mix_norm, v_pre_mlp_norm, v_w_mlp_in, v_w_mlp_out, v_post_mlp_norm):
    a = locals()
    shard_w = {n: a[n] for n in ALL_W}
    shard_m = {n: a["m_" + n] for n in ALL_W}
    shard_v = {n: a["v_" + n] for n in ALL_W}
    loss, grad_x, outs = _step(x[0], loss_target[0], shard_w, shard_m, shard_v)
    return (loss, grad_x[None], *[outs["grad"][n] for n in ALL_W], *[outs["delta"][n] for n in ALL_W],
            *[outs["new_m"][n] for n in ALL_W], *[outs["new_v"][n] for n in ALL_W])
```

```python
import math

import jax
import jax.numpy as jnp
from jax import lax
from jax.experimental import pallas as pl
from jax.experimental.pallas import tpu as pltpu

F32 = jnp.float32
BF16 = jnp.bfloat16
MESH = pl.DeviceIdType.MESH

VMEM_LIMIT_BYTES = 52 * 1024 * 1024
LANES = 128
SUBLANES = 8

EPS = 1e-6
D_MODEL = 1024
D_INNER = 2048
SSD_HEADS = 32
SSD_HEAD_DIM = 64
SSD_GROUPS = 8
SSD_STATE = 128
SSD_CONV = 5
CHUNK = 128
CONV_DIM = 4096
ATTN_HEADS = 16
ATTN_KV = 4
ATTN_DIM = 64
BLOCK = 128
N_BUCKETS = 32
MAX_DISTANCE = 128
D_FF = 4096
N_IN = 9792
NEG = -1e30

OFF_Z, OFF_G, OFF_XBC, OFF_DT, N_MAIN, N_PROJ = 0, 2048, 4096, 8192, 8320, 9856
N_QKV = N_PROJ - N_MAIN

ADAM_LR, ADAM_B1, ADAM_B2, ADAM_EPS, ADAM_WD, ADAM_STEP = 0.001, 0.9, 0.999, 1e-08, 0.01, 10

BIG_ROWS = 11264
SMALL_ROWS = 32


def _cparams(*sem):
    return pltpu.CompilerParams(dimension_semantics=sem, vmem_limit_bytes=VMEM_LIMIT_BYTES)


def _dot(a, b, precision=None):
    return lax.dot_general(a, b, (((1,), (0,)), ((), ())), preferred_element_type=F32, precision=precision)


def _dot_nt(a, b, precision=None):
    return lax.dot_general(a, b, (((1,), (1,)), ((), ())), preferred_element_type=F32, precision=precision)


def _dot_tn(a, b):
    return lax.dot_general(a, b, (((0,), (0,)), ((), ())), preferred_element_type=F32)


def _pick(n, prefs):
    for p in prefs:
        if n % p == 0:
            return p
    return n


def _matmul(name, a, b, mode, out_dtype=F32):
    if mode == "nn":
        (m, k), (_, n) = a.shape, b.shape
    elif mode == "nt":
        (m, k), (n, _) = a.shape, b.shape
    else:
        (k, m), (_, n) = a.shape, b.shape
    tm = _pick(m, (512, 256, 128))
    tn = _pick(n, (1024, 896, 640, 512, 256, 128))
    tk = _pick(k, (1024, 896, 512, 256, 128)) if mode != "tn" else _pick(k, (512, 256, 128))
    nk = k // tk
    if mode == "nn":
        a_spec = pl.BlockSpec((tm, tk), lambda i, j, q: (i, q))
        b_spec = pl.BlockSpec((tk, tn), lambda i, j, q: (q, j))
        fn = _dot
    elif mode == "nt":
        a_spec = pl.BlockSpec((tm, tk), lambda i, j, q: (i, q))
        b_spec = pl.BlockSpec((tn, tk), lambda i, j, q: (j, q))
        fn = _dot_nt
    else:
        a_spec = pl.BlockSpec((tk, tm), lambda i, j, q: (q, i))
        b_spec = pl.BlockSpec((tk, tn), lambda i, j, q: (q, j))
        fn = _dot_tn

    def body(a_ref, b_ref, o_ref, acc_ref):
        p = fn(a_ref[...].astype(BF16), b_ref[...].astype(BF16))
        if nk == 1:
            o_ref[...] = p.astype(o_ref.dtype)
        else:
            q = pl.program_id(2)

            @pl.when(q == 0)
            def _():
                acc_ref[...] = p

            @pl.when(q > 0)
            def _():
                acc_ref[...] += p

            @pl.when(q == nk - 1)
            def _():
                o_ref[...] = acc_ref[...].astype(o_ref.dtype)

    return pl.pallas_call(
        body, name=name, grid=(m // tm, n // tn, nk),
        in_specs=[a_spec, b_spec], out_specs=pl.BlockSpec((tm, tn), lambda i, j, q: (i, j)),
        out_shape=jax.ShapeDtypeStruct((m, n), out_dtype),
        scratch_shapes=[pltpu.VMEM((tm, tn), F32)],
        compiler_params=_cparams("parallel", "parallel", "arbitrary"),
    )(a, b)


def _row(arr, width=None, cb=0, lead=None):
    return (arr, width, cb, lead)


def _rowwise(name, fn, rows, vecs, outs, accs=(), tb=256):
    t = rows[0][0].shape[-2]
    tb = min(tb, t)
    in_specs, args = [], []
    for arr, width, cb, lead in rows:
        w = arr.shape[-1] if width is None else width
        if lead is None:
            in_specs.append(pl.BlockSpec((tb, w), lambda i, cb=cb: (i, cb)))
        else:
            in_specs.append(pl.BlockSpec((None, tb, w), lambda i, cb=cb, lead=lead: (lead, i, cb)))
        args.append(arr)
    for v in vecs:
        in_specs.append(pl.BlockSpec(v.shape, lambda i, nd=v.ndim: (0,) * nd))
        args.append(v)
    out_shape = [jax.ShapeDtypeStruct((t, c), dt) for c, dt in outs] + [jax.ShapeDtypeStruct(s, F32) for s in accs]
    out_specs = [pl.BlockSpec((tb, c), lambda i: (i, 0)) for c, _ in outs] + [pl.BlockSpec(s, lambda i: (0, 0)) for s in accs]
    n_in, n_out = len(args), len(outs)

    def body(*refs):
        vals = [r[...] for r in refs[:n_in]]
        o_vals, a_vals = fn(*vals)
        for r, v in zip(refs[n_in:n_in + n_out], o_vals):
            r[...] = v.astype(r.dtype)
        first = pl.program_id(0) == 0
        for r, v in zip(refs[n_in + n_out:], a_vals):
            @pl.when(first)
            def _(r=r, v=v):
                r[...] = v

            @pl.when(jnp.logical_not(first))
            def _(r=r, v=v):
                r[...] += v

    res = pl.pallas_call(
        body, name=name, grid=(t // tb,), in_specs=in_specs, out_specs=out_specs, out_shape=out_shape,
        compiler_params=_cparams("arbitrary"),
    )(*args)
    return res


def _rms_fwd(x, g):
    r = lax.rsqrt(jnp.mean(x * x, axis=-1, keepdims=True) + EPS)
    return x * r * g


def _rms_bwd(x, g, dy):
    r = lax.rsqrt(jnp.mean(x * x, axis=-1, keepdims=True) + EPS)
    xh = x * r
    dxh = dy * g
    dx = r * (dxh - xh * jnp.mean(dxh * xh, axis=-1, keepdims=True))
    return dx, jnp.sum(dy * xh, axis=0, keepdims=True)


def _silu(x):
    return x * jax.nn.sigmoid(x)


def _silu_grad(x):
    s = jax.nn.sigmoid(x)
    return s * (1.0 + x * (1.0 - s))


GROUP_W = D_INNER // SSD_GROUPS


def _gated_norm_fwd(y, z, w):
    u = y * _silu(z)
    parts = []
    for j in range(SSD_GROUPS):
        ug = u[:, j * GROUP_W:(j + 1) * GROUP_W]
        parts.append(ug * lax.rsqrt(jnp.mean(ug * ug, axis=-1, keepdims=True) + EPS))
    return jnp.concatenate(parts, axis=-1) * w


def _gated_norm_bwd(y, z, w, dyn):
    sz = _silu(z)
    u = y * sz
    duh = dyn * w
    du_parts, uh_parts = [], []
    for j in range(SSD_GROUPS):
        sl = slice(j * GROUP_W, (j + 1) * GROUP_W)
        ug = u[:, sl]
        r = lax.rsqrt(jnp.mean(ug * ug, axis=-1, keepdims=True) + EPS)
        uh = ug * r
        dg = duh[:, sl]
        du_parts.append(r * (dg - uh * jnp.mean(dg * uh, axis=-1, keepdims=True)))
        uh_parts.append(uh)
    du = jnp.concatenate(du_parts, axis=-1)
    uh = jnp.concatenate(uh_parts, axis=-1)
    dw = jnp.sum(dyn * uh, axis=0, keepdims=True)
    return du * sz, du * y * _silu_grad(z), dw


HALO = SUBLANES


def _halo_specs(tb, cb, col0, t):
    nblk8 = t // HALO
    per = tb // HALO
    main = pl.BlockSpec((tb, cb), lambda j, i: (i, col0 + j))
    prev = pl.BlockSpec((HALO, cb), lambda j, i: (jnp.maximum(i * per - 1, 0), col0 + j))
    nxt = pl.BlockSpec((HALO, cb), lambda j, i: (jnp.minimum((i + 1) * per, nblk8 - 1), col0 + j))
    return main, prev, nxt


def _fill_ext(ext_ref, cur_ref, prev_ref, next_ref, tb, ni):
    i = pl.program_id(1)
    ext_ref[0:HALO, :] = jnp.where(i > 0, prev_ref[...], 0.0)
    ext_ref[HALO:HALO + tb, :] = cur_ref[...]
    ext_ref[HALO + tb:HALO + tb + HALO, :] = jnp.where(i < ni - 1, next_ref[...], 0.0)


def _conv_fwd(proj, w, b):
    t = proj.shape[0]
    tb, cb = min(512, t), 512
    ni, nj = t // tb, CONV_DIM // cb
    main, prev, nxt = _halo_specs(tb, cb, OFF_XBC // cb, t)
    pad = (SSD_CONV - 1) // 2

    def body(u_ref, up_ref, un_ref, w_ref, b_ref, pre_ref, act_ref, ext_ref):
        _fill_ext(ext_ref, u_ref, up_ref, un_ref, tb, ni)
        acc = jnp.broadcast_to(b_ref[...], (tb, cb))
        for k in range(SSD_CONV):
            acc = acc + w_ref[k:k + 1, :] * ext_ref[pl.ds(HALO + k - pad, tb), :]
        pre_ref[...] = acc
        act_ref[...] = _silu(acc)

    out = pl.BlockSpec((tb, cb), lambda j, i: (i, j))
    return pl.pallas_call(
        body, name="conv_fwd", grid=(nj, ni),
        in_specs=[main, prev, nxt, pl.BlockSpec((SSD_CONV, cb), lambda j, i: (0, j)), pl.BlockSpec((1, cb), lambda j, i: (0, j))],
        out_specs=[out, out],
        out_shape=[jax.ShapeDtypeStruct((t, CONV_DIM), F32)] * 2,
        scratch_shapes=[pltpu.VMEM((tb + 2 * HALO, cb), F32)],
        compiler_params=_cparams("parallel", "arbitrary"),
    )(proj, proj, proj, w, b)


def _conv_bwd(dpre, proj, w):
    t = proj.shape[0]
    tb, cb = min(512, t), 512
    ni, nj = t // tb, CONV_DIM // cb
    umain, uprev, unext = _halo_specs(tb, cb, OFF_XBC // cb, t)
    dmain, dprev, dnext = _halo_specs(tb, cb, 0, t)
    pad = (SSD_CONV - 1) // 2

    def body(d_ref, dp_ref, dn_ref, u_ref, up_ref, un_ref, w_ref, du_ref, dw_ref, db_ref, extd_ref, extu_ref):
        _fill_ext(extd_ref, d_ref, dp_ref, dn_ref, tb, ni)
        _fill_ext(extu_ref, u_ref, up_ref, un_ref, tb, ni)
        d = d_ref[...]
        du = jnp.zeros((tb, cb), F32)
        @pl.when(pl.program_id(1) == 0)
        def _():
            dw_ref[...] = jnp.zeros_like(dw_ref)
            db_ref[...] = jnp.zeros_like(db_ref)

        for k in range(SSD_CONV):
            du = du + w_ref[k:k + 1, :] * extd_ref[pl.ds(HALO - k + pad, tb), :]
            dw_ref[k:k + 1, :] += jnp.sum(d * extu_ref[pl.ds(HALO + k - pad, tb), :], axis=0, keepdims=True)
        du_ref[...] = du.astype(du_ref.dtype)
        db_ref[...] += jnp.sum(d, axis=0, keepdims=True)

    return pl.pallas_call(
        body, name="conv_bwd", grid=(nj, ni),
        in_specs=[dmain, dprev, dnext, umain, uprev, unext, pl.BlockSpec((SSD_CONV, cb), lambda j, i: (0, j))],
        out_specs=[pl.BlockSpec((tb, cb), lambda j, i: (i, j)), pl.BlockSpec((SSD_CONV, cb), lambda j, i: (0, j)),
                   pl.BlockSpec((1, cb), lambda j, i: (0, j))],
        out_shape=[jax.ShapeDtypeStruct((t, CONV_DIM), BF16), jax.ShapeDtypeStruct((SSD_CONV, CONV_DIM), F32),
                   jax.ShapeDtypeStruct((1, CONV_DIM), F32)],
        scratch_shapes=[pltpu.VMEM((tb + 2 * HALO, cb), F32)] * 2,
        compiler_params=_cparams("parallel", "arbitrary"),
    )(dpre, dpre, dpre, proj, proj, proj, w)


PAIR = 2 * SSD_HEAD_DIM
HI = lax.Precision.HIGHEST


def _ssd_prelude(d, dt_ref, dtt_ref, ar_ref, ac_ref):
    li = lax.broadcasted_iota(jnp.int32, (CHUNK, CHUNK), 0)
    si = lax.broadcasted_iota(jnp.int32, (CHUNK, CHUNK), 1)
    fwd = d == 0
    hi, lo = jnp.where(fwd, li, si), jnp.where(fwd, si, li)
    tri = hi >= lo
    trif = tri.astype(F32)
    trit = (hi <= lo).astype(F32)
    dt = dt_ref[...]
    adt = dt * ar_ref[...]
    adtt = dtt_ref[...] * ac_ref[...]
    p = _dot(trif, adt, HI)
    pt = _dot_nt(adtt, trif, HI)
    tot = jnp.sum(adt, axis=0, keepdims=True)
    return tri, trit, dt, p, pt, tot


def _ssd_specs(nc, rev):
    def cidx(d, c):
        up = (d == 1) if rev else (d == 0)
        return jnp.where(up, c, nc - 1 - c)

    specs = [
        pl.BlockSpec((CHUNK, D_INNER), lambda d, c: (cidx(d, c), 0)),
        pl.BlockSpec((CHUNK, 1024), lambda d, c: (cidx(d, c), 2)),
        pl.BlockSpec((CHUNK, 1024), lambda d, c: (cidx(d, c), 3)),
        pl.BlockSpec((None, CHUNK, SSD_HEADS), lambda d, c: (d, cidx(d, c), 0)),
        pl.BlockSpec((None, SSD_HEADS, CHUNK), lambda d, c: (d, 0, cidx(d, c))),
        pl.BlockSpec((None, 1, SSD_HEADS), lambda d, c: (d, 0, 0)),
        pl.BlockSpec((None, SSD_HEADS, 1), lambda d, c: (d, 0, 0)),
        pl.BlockSpec((None, CHUNK, D_INNER), lambda d, c: (d, cidx(d, c), 0)),
    ]
    return cidx, specs


def _head_decay(tri, p, pt, tot, h):
    pb = jnp.broadcast_to(p[:, h:h + 1], (CHUNK, CHUNK))
    dec = jnp.exp(jnp.where(tri, pb - pt[h:h + 1, :], NEG))
    return dec, jnp.exp(tot[:, h:h + 1] - pb), jnp.exp(pb)


def _ssd_fwd(act, dt2, dt2t, a_row, a_col, dte):
    t = act.shape[0]
    nc = t // CHUNK
    cidx, specs = _ssd_specs(nc, rev=False)

    def body(xs_ref, bs_ref, cs_ref, dt_ref, dtt_ref, ar_ref, ac_ref, dte_ref, y_ref, st_ref, h_ref):
        d, c = pl.program_id(0), pl.program_id(1)

        @pl.when(c == 0)
        def _():
            h_ref[...] = jnp.zeros_like(h_ref)

        st_ref[...] = h_ref[...]
        tri, _, _, p, pt, tot = _ssd_prelude(d, dt_ref, dtt_ref, ar_ref, ac_ref)
        etot = jnp.exp(tot)
        lane = lax.broadcasted_iota(jnp.int32, (CHUNK, PAIR), 1) < SSD_HEAD_DIM
        rowh = lax.broadcasted_iota(jnp.int32, (PAIR, SSD_STATE), 0) < SSD_HEAD_DIM
        for g in range(SSD_GROUPS):
            gs = slice(g * SSD_STATE, (g + 1) * SSD_STATE)
            bg = bs_ref[:, gs]
            cb = cs_ref[:, gs].astype(BF16)
            cbm = _dot_nt(cb, bg.astype(BF16))
            for pr in range(2):
                h0 = g * 4 + pr * 2
                h1 = h0 + 1
                sl = slice(h0 * SSD_HEAD_DIM, h0 * SSD_HEAD_DIM + PAIR)
                xdt = (xs_ref[:, sl] * dte_ref[:, sl]).astype(BF16)
                yd, st, epb = [], [], []
                for h in (h0, h1):
                    dec, wb, eb = _head_decay(tri, p, pt, tot, h)
                    yd.append(_dot((cbm * dec).astype(BF16), xdt))
                    st.append(_dot_tn(xdt, (bg * wb).astype(BF16)))
                    epb.append(eb)
                hin = h_ref[sl, :]
                yo = _dot_nt(cb, hin.astype(BF16)) * jnp.where(lane, epb[0], epb[1])
                y_ref[:, sl] = jnp.where(lane, yd[0], yd[1]) + yo
                et = jnp.where(rowh, etot[:, h0:h0 + 1], etot[:, h1:h1 + 1])
                h_ref[sl, :] = hin * et + jnp.where(rowh, st[0], st[1])

    return pl.pallas_call(
        body, name="ssd_fwd", grid=(2, nc), in_specs=specs,
        out_specs=[pl.BlockSpec((None, CHUNK, D_INNER), lambda d, c: (d, cidx(d, c), 0)),
                   pl.BlockSpec((None, None, D_INNER, SSD_STATE), lambda d, c: (d, cidx(d, c), 0, 0))],
        out_shape=[jax.ShapeDtypeStruct((2, t, D_INNER), F32), jax.ShapeDtypeStruct((2, nc, D_INNER, SSD_STATE), F32)],
        scratch_shapes=[pltpu.VMEM((D_INNER, SSD_STATE), F32)],
        compiler_params=_cparams("arbitrary", "arbitrary"),
    )(act, act, act, dt2, dt2t, a_row, a_col, dte)


def _put_col(acc, col, h):
    lane = lax.broadcasted_iota(jnp.int32, acc.shape, 1)
    return jnp.where(lane == h, col, acc)


def _put_row(acc, row, h):
    sub = lax.broadcasted_iota(jnp.int32, acc.shape, 0)
    return jnp.where(sub == h, row, acc)


def _sum_all(x):
    return jnp.sum(jnp.sum(x, axis=0, keepdims=True), axis=1, keepdims=True)


def _ssd_bwd(act, dt2, dt2t, a_row, a_col, dte, dy, states):
    t = act.shape[0]
    nc = t // CHUNK
    cidx, specs = _ssd_specs(nc, rev=True)
    specs = specs + [
        pl.BlockSpec((CHUNK, D_INNER), lambda d, c: (cidx(d, c), 0)),
        pl.BlockSpec((None, None, D_INNER, SSD_STATE), lambda d, c: (d, cidx(d, c), 0, 0)),
    ]

    def body(xs_ref, bs_ref, cs_ref, dt_ref, dtt_ref, ar_ref, ac_ref, dte_ref, dy_ref, st_ref,
             dxs_ref, dbs_ref, dcs_ref, ddt_ref, ddx_ref, da_ref, dh_ref):
        d, c = pl.program_id(0), pl.program_id(1)

        @pl.when(c == 0)
        def _():
            dh_ref[...] = jnp.zeros_like(dh_ref)
            da_ref[...] = jnp.zeros_like(da_ref)

        tri, trit, dt, p, pt, tot = _ssd_prelude(d, dt_ref, dtt_ref, ar_ref, ac_ref)
        etot = jnp.exp(tot)
        lane = lax.broadcasted_iota(jnp.int32, (CHUNK, PAIR), 1) < SSD_HEAD_DIM
        rowh = lax.broadcasted_iota(jnp.int32, (PAIR, SSD_STATE), 0) < SSD_HEAD_DIM
        same_head = (jnp.right_shift(lax.broadcasted_iota(jnp.int32, (PAIR, PAIR), 0), 6)
                     == jnp.right_shift(lax.broadcasted_iota(jnp.int32, (PAIR, PAIR), 1), 6)).astype(BF16)
        lane32 = lax.broadcasted_iota(jnp.int32, (CHUNK, SSD_HEADS), 1)
        dp_col = jnp.zeros((CHUNK, SSD_HEADS), F32)
        dp_row = jnp.zeros((SSD_HEADS, CHUNK), F32)
        dtot = jnp.zeros((1, SSD_HEADS), F32)
        for g in range(SSD_GROUPS):
            gs = slice(g * SSD_STATE, (g + 1) * SSD_STATE)
            bg = bs_ref[:, gs]
            bb = bg.astype(BF16)
            cb = cs_ref[:, gs].astype(BF16)
            cbm = _dot_nt(cb, bb)
            dcb = jnp.zeros((CHUNK, CHUNK), F32)
            dc_acc = jnp.zeros((CHUNK, SSD_STATE), F32)
            db_acc = jnp.zeros((CHUNK, SSD_STATE), F32)
            for pr in range(2):
                h0 = g * 4 + pr * 2
                h1 = h0 + 1
                sl = slice(h0 * SSD_HEAD_DIM, h0 * SSD_HEAD_DIM + PAIR)
                xp = xs_ref[:, sl]
                dtp = dte_ref[:, sl]
                xdt_f = xp * dtp
                xdt = xdt_f.astype(BF16)
                dyp = dy_ref[:, sl]
                dyb = dyp.astype(BF16)
                hin = st_ref[sl, :]
                dh = dh_ref[sl, :]
                hb = hin.astype(BF16)
                dhb = dh.astype(BF16)
                heads = [_head_decay(tri, p, pt, tot, h) for h in (h0, h1)]
                dye = dyp * jnp.where(lane, heads[0][2], heads[1][2])
                dyeb = dye.astype(BF16)
                gy = _dot_nt(cb, hb) * dye
                dc_acc = dc_acc + _dot(dyeb, hb)
                dhin = _dot_tn(dyeb, cb)
                hh = dh * hin
                dxdt = jnp.zeros((CHUNK, PAIR), F32)
                for idx, h in enumerate((h0, h1)):
                    hm = lane if idx == 0 else jnp.logical_not(lane)
                    rm = rowh if idx == 0 else jnp.logical_not(rowh)
                    dec, wb, _ = heads[idx]
                    mf = cbm * dec
                    t1 = _dot_tn(mf.astype(BF16), dyb)
                    t2 = _dot_nt((bg * wb).astype(BF16), dhb)
                    dxdt = jnp.where(hm, t1 + t2, dxdt)
                    dm = _dot_nt(jnp.where(hm, dyp, 0.0).astype(BF16), xdt)
                    dcb = dcb + dm * dec
                    e = dm * mf
                    qw = _dot(jnp.where(hm, xdt_f, 0.0).astype(BF16), dhb) * wb
                    db_acc = db_acc + qw
                    qwb = qw * bg
                    col = jnp.sum(e + jnp.where(hm, gy, 0.0) - qwb, axis=1, keepdims=True)
                    dp_col = jnp.where(lane32 == h, col, dp_col)
                    dp_row = _put_row(dp_row, -jnp.sum(e, axis=0, keepdims=True), h)
                    dtot_h = _sum_all(qwb) + etot[:, h:h + 1] * _sum_all(jnp.where(rm, hh, 0.0))
                    dtot = _put_col(dtot, dtot_h, h)
                dxs_ref[:, sl] = dxdt * dtp
                ddx = dxdt * xp
                ddx_hi = ddx.astype(BF16)
                ddx_lo = (ddx - ddx_hi.astype(F32)).astype(BF16)
                ddx_ref[:, sl] = _dot(ddx_hi, same_head) + _dot(ddx_lo, same_head)
                et = jnp.where(rowh, etot[:, h0:h0 + 1], etot[:, h1:h1 + 1])
                dh_ref[sl, :] = dh * et + dhin
            dcbb = dcb.astype(BF16)
            dcs_ref[:, gs] = _dot(dcbb, bb) + dc_acc
            dbs_ref[:, gs] = _dot_tn(dcbb, cb) + db_acc
        d_adt = _dot(trit, dp_col, HI) + _dot_nt(trit, dp_row, HI) + dtot
        ddt_ref[...] = ar_ref[...] * d_adt
        da_ref[...] += jnp.sum(dt * d_adt, axis=0, keepdims=True)

    return pl.pallas_call(
        body, name="ssd_bwd", grid=(2, nc), in_specs=specs,
        out_specs=[pl.BlockSpec((None, CHUNK, D_INNER), lambda d, c: (d, cidx(d, c), 0)),
                   pl.BlockSpec((None, CHUNK, 1024), lambda d, c: (d, cidx(d, c), 0)),
                   pl.BlockSpec((None, CHUNK, 1024), lambda d, c: (d, cidx(d, c), 0)),
                   pl.BlockSpec((None, CHUNK, SSD_HEADS), lambda d, c: (d, cidx(d, c), 0)),
                   pl.BlockSpec((None, CHUNK, D_INNER), lambda d, c: (d, cidx(d, c), 0)),
                   pl.BlockSpec((None, 1, SSD_HEADS), lambda d, c: (d, 0, 0))],
        out_shape=[jax.ShapeDtypeStruct((2, t, D_INNER), F32), jax.ShapeDtypeStruct((2, t, 1024), F32),
                   jax.ShapeDtypeStruct((2, t, 1024), F32), jax.ShapeDtypeStruct((2, t, SSD_HEADS), F32),
                   jax.ShapeDtypeStruct((2, t, D_INNER), F32), jax.ShapeDtypeStruct((2, 1, SSD_HEADS), F32)],
        scratch_shapes=[pltpu.VMEM((D_INNER, SSD_STATE), F32)],
        compiler_params=_cparams("arbitrary", "arbitrary"),
    )(act, act, act, dt2, dt2t, a_row, a_col, dte, dy, states)


REP = ATTN_HEADS // ATTN_KV
SCALE = ATTN_DIM ** -0.5


def _attn_specs(nb):
    q = pl.BlockSpec((None, REP, BLOCK, ATTN_DIM), lambda g, n: (g, 0, n, 0))
    kp = pl.BlockSpec((None, BLOCK, ATTN_DIM), lambda g, n: (g, jnp.maximum(n - 1, 0), 0))
    kc = pl.BlockSpec((None, BLOCK, ATTN_DIM), lambda g, n: (g, n, 0))
    kn = pl.BlockSpec((None, BLOCK, ATTN_DIM), lambda g, n: (g, jnp.minimum(n + 1, nb - 1), 0))
    bias = pl.BlockSpec((None, REP, BLOCK, 3 * BLOCK), lambda g, n: (g, 0, 0, 0))
    sink = pl.BlockSpec((None, REP, 1, LANES), lambda g, n: (g, 0, 0, 0))
    return q, kp, kc, kn, bias, sink


def _attn_valid(n, nb):
    ii = lax.broadcasted_iota(jnp.int32, (BLOCK, 3 * BLOCK), 0)
    jj = lax.broadcasted_iota(jnp.int32, (BLOCK, 3 * BLOCK), 1)
    mid = (jj >= BLOCK) & (jj < 2 * BLOCK)
    lo = (jj < BLOCK) & (jj >= ii) & (n > 0)
    hi = (jj >= 2 * BLOCK) & (jj - 2 * BLOCK <= ii) & (n < nb - 1)
    return mid | lo | hi


def _attn_probs(q, kcat, bias, snk, valid):
    s = _dot_nt(q, kcat) * SCALE + bias
    s = jnp.where(valid, s, NEG)
    m = jnp.maximum(jnp.max(s, axis=1, keepdims=True), snk)
    p = jnp.where(valid, jnp.exp(s - m), 0.0)
    es = jnp.exp(snk - m)
    den = jnp.sum(p, axis=1, keepdims=True) + es
    return p / den, es / den


def _attn_fwd(q4, k3, v3, bias4, sink4):
    t = q4.shape[2]
    nb = t // BLOCK
    qs, kp, kc, kn, bs, ss = _attn_specs(nb)

    def body(q_ref, kp_ref, kc_ref, kn_ref, vp_ref, vc_ref, vn_ref, b_ref, s_ref, o_ref):
        n = pl.program_id(1)
        kcat = jnp.concatenate([kp_ref[...], kc_ref[...], kn_ref[...]], axis=0)
        vcat = jnp.concatenate([vp_ref[...], vc_ref[...], vn_ref[...]], axis=0)
        valid = _attn_valid(n, nb)
        for r in range(REP):
            pn, _ = _attn_probs(q_ref[r], kcat, b_ref[r], s_ref[r][:, 0:1], valid)
            o_ref[r] = _dot(pn.astype(BF16), vcat).astype(o_ref.dtype)

    return pl.pallas_call(
        body, name="attn_fwd", grid=(ATTN_KV, nb), in_specs=[qs, kp, kc, kn, kp, kc, kn, bs, ss],
        out_specs=qs, out_shape=jax.ShapeDtypeStruct(q4.shape, BF16),
        compiler_params=_cparams("parallel", "arbitrary"),
    )(q4, k3, k3, k3, v3, v3, v3, bias4, sink4)


def _attn_bwd(q4, k3, v3, bias4, sink4, do4):
    t = q4.shape[2]
    nb = t // BLOCK
    qs, kp, kc, kn, bs, ss = _attn_specs(nb)
    part = pl.BlockSpec((3, None, BLOCK, ATTN_DIM), lambda g, n: (0, g, n, 0))

    def body(q_ref, kp_ref, kc_ref, kn_ref, vp_ref, vc_ref, vn_ref, b_ref, s_ref, do_ref,
             dq_ref, dk_ref, dv_ref, db_ref, ds_ref):
        n = pl.program_id(1)

        @pl.when(n == 0)
        def _():
            db_ref[...] = jnp.zeros_like(db_ref)
            ds_ref[...] = jnp.zeros_like(ds_ref)

        kcat = jnp.concatenate([kp_ref[...], kc_ref[...], kn_ref[...]], axis=0)
        vcat = jnp.concatenate([vp_ref[...], vc_ref[...], vn_ref[...]], axis=0)
        valid = _attn_valid(n, nb)
        dk = jnp.zeros((3 * BLOCK, ATTN_DIM), F32)
        dv = jnp.zeros((3 * BLOCK, ATTN_DIM), F32)
        for r in range(REP):
            q = q_ref[r]
            do = do_ref[r]
            pn, psink = _attn_probs(q, kcat, b_ref[r], s_ref[r][:, 0:1], valid)
            dp = _dot_nt(do, vcat)
            delta = jnp.sum(pn * dp, axis=1, keepdims=True)
            dsc = pn * (dp - delta)
            db_ref[r] += dsc
            ds_ref[r] += jnp.broadcast_to(-jnp.sum(psink * delta, axis=0, keepdims=True), (1, LANES))
            dsb = (dsc * SCALE).astype(BF16)
            dq_ref[r] = _dot(dsb, kcat).astype(dq_ref.dtype)
            dk = dk + _dot_tn(dsb, q)
            dv = dv + _dot_tn(pn.astype(BF16), do)
        for j in range(3):
            dk_ref[j] = dk[j * BLOCK:(j + 1) * BLOCK]
            dv_ref[j] = dv[j * BLOCK:(j + 1) * BLOCK]

    return pl.pallas_call(
        body, name="attn_bwd", grid=(ATTN_KV, nb), in_specs=[qs, kp, kc, kn, kp, kc, kn, bs, ss, qs],
        out_specs=[qs, part, part, bs, ss],
        out_shape=[jax.ShapeDtypeStruct(q4.shape, BF16), jax.ShapeDtypeStruct((3,) + k3.shape, F32),
                   jax.ShapeDtypeStruct((3,) + k3.shape, F32), jax.ShapeDtypeStruct(bias4.shape, F32),
                   jax.ShapeDtypeStruct(sink4.shape, F32)],
        compiler_params=_cparams("parallel", "arbitrary"),
    )(q4, k3, k3, k3, v3, v3, v3, bias4, sink4, do4)


def _kv_combine(name, parts):
    _, kv, t, dd = parts.shape

    def body(p_ref, o_ref):
        z = jnp.zeros((BLOCK, dd), F32)
        from_next = jnp.concatenate([p_ref[0, BLOCK:, :], z], axis=0)
        from_prev = jnp.concatenate([z, p_ref[2, :t - BLOCK, :]], axis=0)
        o_ref[...] = (from_next + p_ref[1] + from_prev).astype(o_ref.dtype)

    return pl.pallas_call(
        body, name=name, grid=(kv,),
        in_specs=[pl.BlockSpec((3, None, t, dd), lambda g: (0, g, 0, 0))],
        out_specs=pl.BlockSpec((None, t, dd), lambda g: (g, 0, 0)),
        out_shape=jax.ShapeDtypeStruct((kv, t, dd), BF16),
        compiler_params=_cparams("parallel"),
    )(parts)


def _t5_bucket(rel):
    nb = N_BUCKETS // 2
    max_exact = nb // 2
    ret = jnp.where(rel > 0, nb, 0)
    n = jnp.abs(rel)
    nf = jnp.maximum(n, 1).astype(jnp.float32)
    large = max_exact + (jnp.log(nf / max_exact) / math.log(MAX_DISTANCE / max_exact) * (nb - max_exact)).astype(jnp.int32)
    large = jnp.minimum(large, nb - 1)
    return ret + jnp.where(n < max_exact, n, large)


def _bucket_map():
    i = jnp.arange(BLOCK)[:, None]
    j = jnp.arange(3 * BLOCK)[None, :]
    return _t5_bucket(j - BLOCK - i)


def _bias_from_table(table, onehot_t):
    def body(t_ref, o_ref, out_ref):
        out_ref[...] = _dot(t_ref[...], o_ref[...], HI)

    return pl.pallas_call(body, name="bias_from_table", out_shape=jax.ShapeDtypeStruct((ATTN_HEADS, onehot_t.shape[1]), F32),
                          compiler_params=pltpu.CompilerParams(vmem_limit_bytes=VMEM_LIMIT_BYTES))(table.T, onehot_t)


def _bias_table_grad(dbias, onehot_t):
    def body(d_ref, o_ref, out_ref):
        out_ref[...] = _dot_nt(d_ref[...], o_ref[...], HI)

    return pl.pallas_call(body, name="bias_table_grad", out_shape=jax.ShapeDtypeStruct((ATTN_HEADS, N_BUCKETS), F32),
                          compiler_params=pltpu.CompilerParams(vmem_limit_bytes=VMEM_LIMIT_BYTES))(dbias, onehot_t)


HBM_SPEC = pl.BlockSpec(memory_space=pl.ANY)


def _allgather_chips(name, x):
    def body(x_ref, out_ref, send_sems, recv_sems, local_sem):
        mx, my, mc = lax.axis_index("x"), lax.axis_index("y"), lax.axis_index("c")
        me = 2 * mx + my
        chips = [(1 - mx, my), (mx, 1 - my), (1 - mx, 1 - my)]
        mine = pltpu.make_async_copy(x_ref, out_ref.at[me], local_sem)
        mine.start()
        sends = [pltpu.make_async_remote_copy(src_ref=x_ref, dst_ref=out_ref.at[me], send_sem=send_sems.at[k],
                                              recv_sem=recv_sems.at[k], device_id=(px, py, mc), device_id_type=MESH)
                 for k, (px, py) in enumerate(chips)]
        for cp in sends:
            cp.start()
        for k, (px, py) in enumerate(chips):
            pltpu.make_async_remote_copy(src_ref=x_ref, dst_ref=out_ref.at[2 * px + py], send_sem=send_sems.at[k],
                                         recv_sem=recv_sems.at[k], device_id=(px, py, mc), device_id_type=MESH).wait_recv()
        for cp in sends:
            cp.wait_send()
        mine.wait()

    return pl.pallas_call(
        body, name=name, in_specs=[HBM_SPEC], out_specs=HBM_SPEC,
        out_shape=jax.ShapeDtypeStruct((4,) + x.shape, x.dtype),
        scratch_shapes=[pltpu.SemaphoreType.DMA((3,)), pltpu.SemaphoreType.DMA((3,)), pltpu.SemaphoreType.DMA],
    )(x)


def _allgather_all(name, x):
    def body(x_ref, out_ref, send_sems, recv_sems, local_sem):
        mx, my, mc = lax.axis_index("x"), lax.axis_index("y"), lax.axis_index("c")
        me = 4 * mx + 2 * my + mc
        flips = [(fx, fy, fc) for fx in (0, 1) for fy in (0, 1) for fc in (0, 1)][1:]
        peers = [(mx ^ fx, my ^ fy, mc ^ fc) for fx, fy, fc in flips]
        mine = pltpu.make_async_copy(x_ref, out_ref.at[me], local_sem)
        mine.start()
        sends = [pltpu.make_async_remote_copy(src_ref=x_ref, dst_ref=out_ref.at[me], send_sem=send_sems.at[k],
                                              recv_sem=recv_sems.at[k], device_id=peer, device_id_type=MESH)
                 for k, peer in enumerate(peers)]
        for cp in sends:
            cp.start()
        for k, (px, py, pc) in enumerate(peers):
            pltpu.make_async_remote_copy(src_ref=x_ref, dst_ref=out_ref.at[4 * px + 2 * py + pc], send_sem=send_sems.at[k],
                                         recv_sem=recv_sems.at[k], device_id=(px, py, pc), device_id_type=MESH).wait_recv()
        for cp in sends:
            cp.wait_send()
        mine.wait()

    return pl.pallas_call(
        body, name=name, in_specs=[HBM_SPEC], out_specs=HBM_SPEC,
        out_shape=jax.ShapeDtypeStruct((8,) + x.shape, x.dtype),
        scratch_shapes=[pltpu.SemaphoreType.DMA((7,)), pltpu.SemaphoreType.DMA((7,)), pltpu.SemaphoreType.DMA],
    )(x)


def _scatter_halves(name, g):
    _, _, r2, c = g.shape

    def body(g_ref, out_ref, send_sems, recv_sems, local_sem):
        mx, my, mc = lax.axis_index("x"), lax.axis_index("y"), lax.axis_index("c")
        me = 4 * mx + 2 * my + mc
        flips = [(fx, fy, fc) for fx in (0, 1) for fy in (0, 1) for fc in (0, 1)][1:]
        peers = [(mx ^ fx, my ^ fy, mc ^ fc) for fx, fy, fc in flips]
        mine = pltpu.make_async_copy(g_ref.at[2 * mx + my, mc], out_ref.at[me], local_sem)
        mine.start()
        sends = [pltpu.make_async_remote_copy(src_ref=g_ref.at[2 * px + py, pc], dst_ref=out_ref.at[me],
                                              send_sem=send_sems.at[k], recv_sem=recv_sems.at[k],
                                              device_id=(px, py, pc), device_id_type=MESH)
                 for k, (px, py, pc) in enumerate(peers)]
        for cp in sends:
            cp.start()
        for k, (px, py, pc) in enumerate(peers):
            pltpu.make_async_remote_copy(src_ref=g_ref.at[0, 0], dst_ref=out_ref.at[4 * px + 2 * py + pc],
                                         send_sem=send_sems.at[k], recv_sem=recv_sems.at[k],
                                         device_id=(px, py, pc), device_id_type=MESH).wait_recv()
        for cp in sends:
            cp.wait_send()
        mine.wait()

    return pl.pallas_call(
        body, name=name, in_specs=[HBM_SPEC], out_specs=HBM_SPEC,
        out_shape=jax.ShapeDtypeStruct((8, r2, c), g.dtype),
        scratch_shapes=[pltpu.SemaphoreType.DMA((7,)), pltpu.SemaphoreType.DMA((7,)), pltpu.SemaphoreType.DMA],
    )(g)


def _allgather_cores(name, x):
    def body(x_ref, out_ref, send_sem, recv_sem, local_sem):
        mx, my, mc = lax.axis_index("x"), lax.axis_index("y"), lax.axis_index("c")
        mine = pltpu.make_async_copy(x_ref, out_ref.at[mc], local_sem)
        mine.start()
        send = pltpu.make_async_remote_copy(src_ref=x_ref, dst_ref=out_ref.at[mc], send_sem=send_sem, recv_sem=recv_sem,
                                            device_id=(mx, my, 1 - mc), device_id_type=MESH)
        send.start()
        pltpu.make_async_remote_copy(src_ref=x_ref, dst_ref=out_ref.at[1 - mc], send_sem=send_sem, recv_sem=recv_sem,
                                     device_id=(mx, my, 1 - mc), device_id_type=MESH).wait_recv()
        send.wait_send()
        mine.wait()

    return pl.pallas_call(
        body, name=name, in_specs=[HBM_SPEC], out_specs=HBM_SPEC,
        out_shape=jax.ShapeDtypeStruct((2,) + x.shape, x.dtype),
        scratch_shapes=[pltpu.SemaphoreType.DMA, pltpu.SemaphoreType.DMA, pltpu.SemaphoreType.DMA],
    )(x)


def _sum_slots(name, st, tb=256):
    n, r, c = st.shape
    tb = _pick(r, (tb, 32))

    def body(s_ref, o_ref):
        acc = s_ref[0].astype(F32)
        for k in range(1, n):
            acc = acc + s_ref[k].astype(F32)
        o_ref[...] = acc

    return pl.pallas_call(
        body, name=name, grid=(r // tb,), in_specs=[pl.BlockSpec((n, tb, c), lambda i: (0, i, 0))],
        out_specs=pl.BlockSpec((tb, c), lambda i: (i, 0)), out_shape=jax.ShapeDtypeStruct((r, c), F32),
        compiler_params=_cparams("parallel"),
    )(st)


def _adamw(name, w, g, m, v):
    def fn(w, g, m, v):
        m2 = ADAM_B1 * m + (1.0 - ADAM_B1) * g
        v2 = ADAM_B2 * v + (1.0 - ADAM_B2) * jnp.square(g)
        m_hat = m2 / (1.0 - ADAM_B1 ** ADAM_STEP)
        v_hat = v2 / (1.0 - ADAM_B2 ** ADAM_STEP)
        delta = -ADAM_LR * (m_hat / (jnp.sqrt(v_hat) + ADAM_EPS) + ADAM_WD * w)
        return [delta, m2, v2], []

    tb = _pick(w.shape[0], (256, 32))
    return _rowwise(name, fn, [_row(w), _row(g), _row(m), _row(v)], [], [(w.shape[1], F32)] * 3, tb=tb)


BIG = ("w_in", "w_ssd_out", "w_attn_out", "w_o", "w_mlp_in", "w_mlp_out", "conv_w")
SMALL = ("pre_mix_norm", "b_gate", "conv_b", "dt_bias", "a_log", "d_skip", "ssd_norm", "attn_sink", "rel_bias_table",
         "post_mix_norm", "pre_mlp_norm", "post_mlp_norm")
ALL_W = ("pre_mix_norm", "w_in", "b_gate", "conv_w", "conv_b", "dt_bias", "a_log", "d_skip", "ssd_norm", "w_ssd_out",
         "attn_sink", "rel_bias_table", "w_attn_out", "w_o", "post_mix_norm", "pre_mlp_norm", "w_mlp_in", "w_mlp_out",
         "post_mlp_norm")


def _pack_rows(parts, rows, dtype):
    flat = jnp.concatenate([p.reshape(-1, D_MODEL).astype(dtype) for p in parts], axis=0)
    return jnp.pad(flat, ((0, rows - flat.shape[0]), (0, 0)))


def _pack_big(shards, dtype):
    return _pack_rows([shards[n] for n in BIG], BIG_ROWS, dtype)


def _unpack_big(flat, like):
    out, r = {}, 0
    for n in BIG:
        shp = like[n].shape
        nr = math.prod(shp) // D_MODEL
        out[n] = flat[r:r + nr].reshape(shp)
        r += nr
    return out


def _pack_small(parts, extra=None):
    flat = jnp.concatenate([parts[n].reshape(-1).astype(F32) for n in SMALL] + ([extra.reshape(-1)] if extra is not None else []))
    return jnp.pad(flat, (0, SMALL_ROWS * D_MODEL - flat.shape[0])).reshape(SMALL_ROWS, D_MODEL)


def _unpack_small(flat2, like):
    flat = flat2.reshape(-1)
    out, r = {}, 0
    for n in SMALL:
        shp = like[n].shape
        k = math.prod(shp)
        out[n] = flat[r:r + k].reshape(shp)
        r += k
    return out, flat[r]


def _shard_of_full(name, full, s):
    if name in ("w_in", "w_mlp_in"):
        w = full.shape[2] // 4
        return full[:, :, s * w:(s + 1) * w]
    if name == "conv_w":
        w = full.shape[3] // 4
        return full[:, :, :, s * w:(s + 1) * w]
    w = full.shape[1] // 4
    return full[:, s * w:(s + 1) * w, :]


def _full_of_shards(name, shards):
    axis = {"w_in": 2, "w_mlp_in": 2, "conv_w": 3}.get(name, 1)
    return jnp.concatenate(shards, axis=axis)


def _to_proj_layout(w):
    z, xbc, dt, q, k, v, gates = (w[..., 0:2048], w[..., 2048:6144], w[..., 6144:6208], w[..., 6208:7232],
                                  w[..., 7232:7488], w[..., 7488:7744], w[..., 7744:9792])
    pad = jnp.zeros(w.shape[:-1] + (N_PROJ - N_IN,), w.dtype)
    return jnp.concatenate([z, gates, xbc, dt, pad, q, k, v], axis=-1)


def _from_proj_layout(w):
    z, gates, xbc, dt, q, k, v = (w[..., 0:2048], w[..., 2048:4096], w[..., 4096:8192], w[..., 8192:8256],
                                  w[..., 8320:9344], w[..., 9344:9600], w[..., 9600:9856])
    return jnp.concatenate([z, xbc, dt, q, k, v, gates], axis=-1)


def _heads_major(a, nh):
    t = a.shape[0]
    return a.reshape(t, nh, ATTN_DIM).transpose(1, 0, 2)


def _tokens_major(a):
    nh, t, dd = a.shape
    return a.transpose(1, 0, 2).reshape(t, nh * dd)


def _layer_fwd(h1, x, W, P, l, bias4):
    t = x.shape[0]
    S = {"x": x, "h1": h1}
    proj = _matmul("proj", h1, W["w_in_main"][l], "nn")
    qkv = _matmul("proj_qkv", h1, W["w_in_qkv"][l], "nn", out_dtype=BF16)
    S["proj"] = proj
    pre, act = _conv_fwd(proj, W["conv_w"][l], P["conv_b"][l].reshape(1, CONV_DIM))
    S["pre"], S["act"] = pre, act

    dtb = jnp.pad(P["dt_bias"][l].reshape(1, 2 * SSD_HEADS), ((0, 0), (0, LANES - 2 * SSD_HEADS)))

    def dt_fn(raw, b):
        v = raw + b
        return [jnp.maximum(v, 0.0) + jnp.log1p(jnp.exp(-jnp.abs(v)))], []

    (dt,) = _rowwise("dt_fwd", dt_fn, [_row(proj, LANES, OFF_DT // LANES)], [dtb], [(LANES, F32)], tb=512)
    dt2 = jnp.stack([dt[:, 0:SSD_HEADS], dt[:, SSD_HEADS:2 * SSD_HEADS]])
    dt2t = dt2.transpose(0, 2, 1)
    a = -jnp.exp(P["a_log"][l])
    a_row, a_col = a.reshape(2, 1, SSD_HEADS), a.reshape(2, SSD_HEADS, 1)
    dte = jnp.repeat(dt2, SSD_HEAD_DIM, axis=-1)
    S["dt2"], S["dt2t"], S["a_row"], S["a_col"], S["dte"] = dt2, dt2t, a_row, a_col, dte
    y2, states = _ssd_fwd(act, dt2, dt2t, a_row, a_col, dte)
    S["states"] = states

    dsk = jnp.repeat(P["d_skip"][l], SSD_HEAD_DIM).reshape(1, D_INNER)
    nw = P["ssd_norm"][l].reshape(1, D_INNER)
    S["dsk"], S["nw"] = dsk, nw

    def gn_fn(yf, yb, xs, z, dsk, nw):
        y = yf + yb + xs * dsk
        return [y, _gated_norm_fwd(y, z, nw)], []

    y, yn = _rowwise("gated_norm_fwd", gn_fn,
                     [_row(y2, lead=0), _row(y2, lead=1), _row(act, D_INNER, 0), _row(proj, D_INNER, OFF_Z // D_INNER)],
                     [dsk, nw], [(D_INNER, F32), (D_INNER, BF16)])
    S["y"], S["yn"] = y, yn
    y_ssd = _matmul("ssd_out", yn, W["w_ssd_out"][l], "nn")
    S["y_ssd"] = y_ssd

    q4 = _heads_major(qkv[:, 0:1024], ATTN_HEADS).reshape(ATTN_KV, REP, t, ATTN_DIM)
    k3 = _heads_major(qkv[:, 1024:1280], ATTN_KV)
    v3 = _heads_major(qkv[:, 1280:1536], ATTN_KV)
    sink4 = jnp.broadcast_to(P["attn_sink"][l].reshape(ATTN_KV, REP, 1, 1), (ATTN_KV, REP, 1, LANES))
    S["q4"], S["k3"], S["v3"], S["sink4"] = q4, k3, v3, sink4
    o4 = _attn_fwd(q4, k3, v3, bias4, sink4)
    o = _tokens_major(o4.reshape(ATTN_HEADS, t, ATTN_DIM))
    S["o"] = o
    y_attn = _matmul("attn_out", o, W["w_attn_out"][l], "nn")
    S["y_attn"] = y_attn

    bg = P["b_gate"][l].reshape(1, 2 * D_MODEL)
    S["bg"] = bg

    def merge_fn(gates, ys, ya, b):
        g = jax.nn.sigmoid(gates + b)
        return [g[:, :D_MODEL] * ys + g[:, D_MODEL:] * ya], []

    (mix_in,) = _rowwise("merge_fwd", merge_fn, [_row(proj, 2 * D_MODEL, OFF_G // (2 * D_MODEL)), _row(y_ssd), _row(y_attn)],
                         [bg], [(D_MODEL, BF16)])
    S["mix_in"] = mix_in
    mixed = _matmul("w_o", mix_in, W["w_o"][l], "nn")
    S["mixed"] = mixed

    g_pm = P["post_mix_norm"][l].reshape(1, D_MODEL)
    g_pl = P["pre_mlp_norm"][l].reshape(1, D_MODEL)

    def postmix_fn(x, mixed, g1, g2):
        x2 = x + _rms_fwd(mixed, g1)
        return [x2, _rms_fwd(x2, g2)], []

    x2, h2 = _rowwise("post_mix_fwd", postmix_fn, [_row(x), _row(mixed)], [g_pm, g_pl], [(D_MODEL, F32), (D_MODEL, BF16)])
    S["x2"], S["h2"] = x2, h2
    f1 = _matmul("mlp_in", h2, W["w_mlp_in"][l], "nn")
    S["f1"] = f1

    def relu2_fn(f):
        return [jnp.square(jnp.maximum(f, 0.0))], []

    (a1,) = _rowwise("relu2_fwd", relu2_fn, [_row(f1)], [], [(D_FF, BF16)])
    S["a1"] = a1
    f2 = _matmul("mlp_out", a1, W["w_mlp_out"][l], "nn")
    S["f2"] = f2
    return S


def _layer_bwd(S, dx3, W, P, l, bias4, onehot_t):
    t = dx3.shape[0]
    G = {}
    g_pmlp = P["post_mlp_norm"][l].reshape(1, D_MODEL)

    def b1_fn(f2, dx3, g):
        df2, dg = _rms_bwd(f2, g, dx3)
        return [df2], [dg]

    df2, G["post_mlp_norm"] = _rowwise("post_mlp_bwd", b1_fn, [_row(S["f2"]), _row(dx3)], [g_pmlp], [(D_MODEL, BF16)], [(1, D_MODEL)])
    da1 = _matmul("d_a1", df2, W["w_mlp_out"][l], "nt")
    G["w_mlp_out"] = _matmul("dw_mlp_out", S["a1"], df2, "tn")

    def b2_fn(da1, f1):
        return [da1 * 2.0 * jnp.maximum(f1, 0.0)], []

    (df1,) = _rowwise("relu2_bwd", b2_fn, [_row(da1), _row(S["f1"])], [], [(D_FF, BF16)])
    dh2 = _matmul("d_h2", df1, W["w_mlp_in"][l], "nt")
    G["w_mlp_in"] = _matmul("dw_mlp_in", S["h2"], df1, "tn")

    g_pm = P["post_mix_norm"][l].reshape(1, D_MODEL)
    g_pl = P["pre_mlp_norm"][l].reshape(1, D_MODEL)

    def b3_fn(x2, dh2, dx3, mixed, g_pl, g_pm):
        d1, dgl = _rms_bwd(x2, g_pl, dh2)
        dx2 = dx3 + d1
        dmixed, dgm = _rms_bwd(mixed, g_pm, dx2)
        return [dx2, dmixed], [dgl, dgm]

    dx2, dmixed, G["pre_mlp_norm"], G["post_mix_norm"] = _rowwise(
        "post_mix_bwd", b3_fn, [_row(S["x2"]), _row(dh2), _row(dx3), _row(S["mixed"])], [g_pl, g_pm],
        [(D_MODEL, F32), (D_MODEL, BF16)], [(1, D_MODEL), (1, D_MODEL)])
    dmix_in = _matmul("d_mix_in", dmixed, W["w_o"][l], "nt")
    G["w_o"] = _matmul("dw_o", S["mix_in"], dmixed, "tn")

    proj = S["proj"]

    def b4_fn(gates, ys, ya, dmix, b):
        g = jax.nn.sigmoid(gates + b)
        gs, ga = g[:, :D_MODEL], g[:, D_MODEL:]
        dg = jnp.concatenate([ys * dmix, ya * dmix], axis=-1) * g * (1.0 - g)
        return [gs * dmix, ga * dmix, dg], [jnp.sum(dg, axis=0, keepdims=True)]

    dy_ssd, dy_attn, dgates, G["b_gate"] = _rowwise(
        "merge_bwd", b4_fn, [_row(proj, 2 * D_MODEL, OFF_G // (2 * D_MODEL)), _row(S["y_ssd"]), _row(S["y_attn"]), _row(dmix_in)],
        [S["bg"]], [(D_MODEL, BF16), (D_MODEL, BF16), (2 * D_MODEL, BF16)], [(1, 2 * D_MODEL)])

    dyn = _matmul("d_yn", dy_ssd, W["w_ssd_out"][l], "nt")
    G["w_ssd_out"] = _matmul("dw_ssd_out", S["yn"], dy_ssd, "tn")
    do = _matmul("d_o", dy_attn, W["w_attn_out"][l], "nt", out_dtype=BF16)
    G["w_attn_out"] = _matmul("dw_attn_out", S["o"], dy_attn, "tn")

    do4 = _heads_major(do, ATTN_HEADS).reshape(ATTN_KV, REP, t, ATTN_DIM)
    dq4, dkp, dvp, dbias4, dsink4 = _attn_bwd(S["q4"], S["k3"], S["v3"], bias4, S["sink4"], do4)
    dq = _tokens_major(dq4.reshape(ATTN_HEADS, t, ATTN_DIM))
    dk = _tokens_major(_kv_combine("dk_combine", dkp))
    dv = _tokens_major(_kv_combine("dv_combine", dvp))
    G["attn_sink"] = dsink4[:, :, 0, 0].reshape(ATTN_HEADS)
    G["rel_bias_table"] = _bias_table_grad(dbias4.reshape(ATTN_HEADS, BLOCK * 3 * BLOCK), onehot_t).T

    act = S["act"]

    def b5_fn(y, z, xs, dyn, nw, dsk):
        dy, dz, dnw = _gated_norm_bwd(y, z, nw, dyn)
        return [dy, dz], [dnw, jnp.sum(dy * xs, axis=0, keepdims=True)]

    dy, dz, G["ssd_norm"], dskip_cols = _rowwise(
        "gated_norm_bwd", b5_fn, [_row(S["y"]), _row(proj, D_INNER, OFF_Z // D_INNER), _row(act, D_INNER, 0), _row(dyn)],
        [S["nw"], S["dsk"]], [(D_INNER, F32), (D_INNER, BF16)], [(1, D_INNER), (1, D_INNER)])
    G["d_skip"] = dskip_cols.reshape(SSD_HEADS, SSD_HEAD_DIM).sum(axis=-1)

    dxs2, dbs2, dcs2, ddt_decay, ddt_x, da2 = _ssd_bwd(act, S["dt2"], S["dt2t"], S["a_row"], S["a_col"], S["dte"], dy, S["states"])
    ddt2 = ddt_decay + ddt_x[:, :, ::SSD_HEAD_DIM]
    G["a_log"] = da2.reshape(2, SSD_HEADS) * S["a_row"].reshape(2, SSD_HEADS)

    def b6_fn(dxf, dxb, dy, dbf, dbb, dcf, dcb, pre, dsk):
        dact = jnp.concatenate([dxf + dxb + dy * dsk, dbf + dbb, dcf + dcb], axis=-1)
        return [dact * _silu_grad(pre)], []

    (dpre,) = _rowwise("silu_bwd", b6_fn,
                       [_row(dxs2, lead=0), _row(dxs2, lead=1), _row(dy), _row(dbs2, lead=0), _row(dbs2, lead=1),
                        _row(dcs2, lead=0), _row(dcs2, lead=1), _row(S["pre"])], [S["dsk"]], [(CONV_DIM, F32)], tb=128)
    du, dconv_w, dconv_b = _conv_bwd(dpre, proj, W["conv_w"][l])
    G["conv_w"] = dconv_w.reshape(SSD_CONV, 1, CONV_DIM)
    G["conv_b"] = dconv_b.reshape(CONV_DIM)

    dtb = jnp.pad(P["dt_bias"][l].reshape(1, 2 * SSD_HEADS), ((0, 0), (0, LANES - 2 * SSD_HEADS)))
    ddt = jnp.pad(jnp.concatenate([ddt2[0], ddt2[1]], axis=-1), ((0, 0), (0, LANES - 2 * SSD_HEADS)))

    def b7_fn(raw, ddt, b):
        draw = ddt * jax.nn.sigmoid(raw + b)
        return [draw], [jnp.sum(draw, axis=0, keepdims=True)]

    draw, ddtb = _rowwise("dt_bwd", b7_fn, [_row(proj, LANES, OFF_DT // LANES), _row(ddt)], [dtb], [(LANES, BF16)], [(1, LANES)], tb=512)
    G["dt_bias"] = ddtb[0, :2 * SSD_HEADS].reshape(2, SSD_HEADS)

    dproj = jnp.concatenate([dz, dgates, du, draw, dq, dk, dv], axis=-1)
    G["w_in"] = _from_proj_layout(_matmul("dw_in", S["h1"], dproj, "tn"))
    dh1 = _matmul("d_h1", dproj, W["w_in"][l], "nt")

    g_pre = P["pre_mix_norm"][l].reshape(1, D_MODEL)

    def b8_fn(x, dh1, dx2, g):
        d1, dg = _rms_bwd(x, g, dh1)
        return [dx2 + d1], [dg]

    dx, G["pre_mix_norm"] = _rowwise("pre_mix_bwd", b8_fn, [_row(S["x"]), _row(dh1), _row(dx2)], [g_pre], [(D_MODEL, F32)], [(1, D_MODEL)])
    return dx, G


def _step(x, target, shard_w, shard_m, shard_v):
    depth = shard_w["w_in"].shape[0]
    gathered = _allgather_chips("gather_weights", _pack_big(shard_w, BF16))
    conv_rows = shard_w["conv_w"].reshape(-1, D_MODEL)
    conv_g = _allgather_chips("gather_conv", jnp.pad(conv_rows, ((0, 16 - conv_rows.shape[0]), (0, 0))))
    per_chip = [_unpack_big(gathered[s], shard_w) for s in range(4)]
    W = {n: _full_of_shards(n, [per_chip[s][n] for s in range(4)]) for n in BIG if n != "conv_w"}
    W["w_in"] = _to_proj_layout(W["w_in"])
    W["conv_w"] = jnp.concatenate([conv_g[s][:conv_rows.shape[0]].reshape(shard_w["conv_w"].shape) for s in range(4)],
                                  axis=3).reshape(depth, SSD_CONV, CONV_DIM)
    P = {n: shard_w[n] for n in SMALL}
    loss_part, grad_x, full = _local_step(x, target, W, P)

    pieces = jnp.stack([_pack_big({n: _shard_of_full(n, full[n], s) for n in BIG}, BF16) for s in range(4)])
    staged = _scatter_halves("scatter_grads", pieces.reshape(4, 2, BIG_ROWS // 2, D_MODEL))
    half = _sum_slots("sum_grads", staged)
    g_big = _allgather_cores("share_grads", half).reshape(BIG_ROWS, D_MODEL)
    d_big, m_big, v_big = _adamw("adamw_big", _pack_big(shard_w, F32), g_big, _pack_big(shard_m, F32), _pack_big(shard_v, F32))

    small = _allgather_all("gather_small", _pack_small(full, loss_part))
    g_small = _sum_slots("sum_small", small, tb=SMALL_ROWS)
    d_small, m_small, v_small = _adamw("adamw_small", _pack_small(shard_w, jnp.zeros((), F32)), g_small,
                                       _pack_small(shard_m, jnp.zeros((), F32)), _pack_small(shard_v, jnp.zeros((), F32)))

    outs = {}
    for tag, big, sm in (("grad", g_big, g_small), ("delta", d_big, d_small), ("new_m", m_big, m_small), ("new_v", v_big, v_small)):
        ub = _unpack_big(big, shard_w)
        us, extra = _unpack_small(sm, shard_w)
        outs[tag] = {**ub, **us}
        if tag == "grad":
            loss = extra
    return loss, grad_x, outs


def _local_step(x, target, W, P):
    depth = W["w_in"].shape[0]
    W = dict(W, w_in_main=W["w_in"][:, :, :N_MAIN], w_in_qkv=W["w_in"][:, :, N_MAIN:])
    onehot_t = (_bucket_map().reshape(1, -1) == jnp.arange(N_BUCKETS)[:, None]).astype(F32)
    bias4 = _bias_from_table(P["rel_bias_table"], onehot_t).reshape(ATTN_KV, REP, BLOCK, 3 * BLOCK)

    def pre_fn(x, g):
        return [_rms_fwd(x, g)], []

    (h1,) = _rowwise("pre_mix_fwd", pre_fn, [_row(x)], [P["pre_mix_norm"][0].reshape(1, D_MODEL)], [(D_MODEL, BF16)])
    saved = []
    loss_cols = dxl = None
    for l in range(depth):
        S = _layer_fwd(h1, x, W, P, l, bias4)
        saved.append(S)
        g_pmlp = P["post_mlp_norm"][l].reshape(1, D_MODEL)
        if l + 1 < depth:
            def post_fn(x2, f2, g1, g2):
                x3 = x2 + _rms_fwd(f2, g1)
                return [x3, _rms_fwd(x3, g2)], []

            x, h1 = _rowwise("post_mlp_fwd", post_fn, [_row(S["x2"]), _row(S["f2"])],
                             [g_pmlp, P["pre_mix_norm"][l + 1].reshape(1, D_MODEL)], [(D_MODEL, F32), (D_MODEL, BF16)])
        else:
            def loss_fn(x2, f2, tgt, g1):
                diff = x2 + _rms_fwd(f2, g1) - tgt
                return [diff * (1.0 / D_MODEL)], [jnp.sum(diff * diff, axis=0, keepdims=True)]

            dxl, loss_cols = _rowwise("loss", loss_fn, [_row(S["x2"]), _row(S["f2"]), _row(target)], [g_pmlp],
                                      [(D_MODEL, F32)], [(1, D_MODEL)])
    loss_part = 0.5 * jnp.sum(loss_cols) / D_MODEL

    grads = [None] * depth
    dx = dxl
    for l in reversed(range(depth)):
        dx, grads[l] = _layer_bwd(saved[l], dx, W, P, l, bias4, onehot_t)
    grad_x = dx

    full = {n: jnp.stack([grads[l][n] for l in range(depth)]) for n in ALL_W if n != "rel_bias_table"}
    full["rel_bias_table"] = sum(grads[l]["rel_bias_table"] for l in range(depth))
    return loss_part, grad_x, full


def kernel(x, pre_mix_norm, w_in, b_gate, conv_w, conv_b, dt_bias, a_log, d_skip, ssd_norm, w_ssd_out, attn_sink, rel_bias_table, w_attn_out, w_o, post_mix_norm, pre_mlp_norm, w_mlp_in, w_mlp_out, post_mlp_norm, loss_target, m_pre_mix_norm, m_w_in, m_b_gate, m_conv_w, m_conv_b, m_dt_bias, m_a_log, m_d_skip, m_ssd_norm, m_w_ssd_out, m_attn_sink, m_rel_bias_table, m_w_attn_out, m_w_o, m_post_mix_norm, m_pre_mlp_norm, m_w_mlp_in, m_w_mlp_out, m_post_mlp_norm, v_pre_mix_norm, v_w_in, v_b_gate, v_conv_w, v_conv_b, v_dt_bias, v_a_log, v_d_skip, v_ssd_norm, v_w_ssd_out, v_attn_sink, v_rel_bias_table, v_w_attn_out, v_w_o, v_post_mix_norm, v_pre_mlp_norm, v_w_mlp_in, v_w_mlp_out, v_post_mlp_norm):
    a = locals()
    shard_w = {n: a[n] for n in ALL_W}
    shard_m = {n: a["m_" + n] for n in ALL_W}
    shard_v = {n: a["v_" + n] for n in ALL_W}
    loss, grad_x, outs = _step(x[0], loss_target[0], shard_w, shard_m, shard_v)
    return (loss, grad_x[None], *[outs["grad"][n] for n in ALL_W], *[outs["delta"][n] for n in ALL_W],
            *[outs["new_m"][n] for n in ALL_W], *[outs["new_v"][n] for n in ALL_W])
```

```python
import math

import jax
import jax.numpy as jnp
from jax import lax
from jax.experimental import pallas as pl
from jax.experimental.pallas import tpu as pltpu

F32 = jnp.float32
BF16 = jnp.bfloat16
MESH = pl.DeviceIdType.MESH

VMEM_LIMIT_BYTES = 52 * 1024 * 1024
LANES = 128
SUBLANES = 8

EPS = 1e-6
D_MODEL = 1024
D_INNER = 2048
SSD_HEADS = 32
SSD_HEAD_DIM = 64
SSD_GROUPS = 8
SSD_STATE = 128
SSD_CONV = 5
CHUNK = 128
CONV_DIM = 4096
ATTN_HEADS = 16
ATTN_KV = 4
ATTN_DIM = 64
BLOCK = 128
N_BUCKETS = 32
MAX_DISTANCE = 128
D_FF = 4096
N_IN = 9792
NEG = -1e30

OFF_Z, OFF_G, OFF_XBC, OFF_DT, N_MAIN, N_PROJ = 0, 2048, 4096, 8192, 8320, 9856
N_QKV = N_PROJ - N_MAIN

ADAM_LR, ADAM_B1, ADAM_B2, ADAM_EPS, ADAM_WD, ADAM_STEP = 0.001, 0.9, 0.999, 1e-08, 0.01, 10

BIG_ROWS = 11264
SMALL_ROWS = 32


def _cparams(*sem):
    return pltpu.CompilerParams(dimension_semantics=sem, vmem_limit_bytes=VMEM_LIMIT_BYTES)


def _dot(a, b, precision=None):
    return lax.dot_general(a, b, (((1,), (0,)), ((), ())), preferred_element_type=F32, precision=precision)


def _dot_nt(a, b, precision=None):
    return lax.dot_general(a, b, (((1,), (1,)), ((), ())), preferred_element_type=F32, precision=precision)


def _dot_tn(a, b):
    return lax.dot_general(a, b, (((0,), (0,)), ((), ())), preferred_element_type=F32)


def _pick(n, prefs):
    for p in prefs:
        if n % p == 0:
            return p
    return n


def _matmul(name, a, b, mode, out_dtype=F32, epilogue=None, extras=()):
    if mode == "nn":
        (m, k), (_, n) = a.shape, b.shape
    elif mode == "nt":
        (m, k), (n, _) = a.shape, b.shape
    else:
        (k, m), (_, n) = a.shape, b.shape
    tm = _pick(m, (512, 256, 128)) if mode == "tn" else _pick(m, (1024, 512, 256, 128))
    tn = _pick(n, (1024, 896, 640, 512, 256, 128))
    tk = _pick(k, (1024, 896, 512, 256, 128)) if mode != "tn" else _pick(k, (512, 256, 128))
    nk = k // tk
    if mode == "nn":
        a_spec = pl.BlockSpec((tm, tk), lambda i, j, q: (i, q))
        b_spec = pl.BlockSpec((tk, tn), lambda i, j, q: (q, j))
        fn = _dot
    elif mode == "nt":
        a_spec = pl.BlockSpec((tm, tk), lambda i, j, q: (i, q))
        b_spec = pl.BlockSpec((tn, tk), lambda i, j, q: (j, q))
        fn = _dot_nt
    else:
        a_spec = pl.BlockSpec((tk, tm), lambda i, j, q: (q, i))
        b_spec = pl.BlockSpec((tk, tn), lambda i, j, q: (q, j))
        fn = _dot_tn

    tile = pl.BlockSpec((tm, tn), lambda i, j, q: (i, j))
    n_ex = len(extras)

    def body(a_ref, b_ref, *rest):
        ex_refs, o_ref = rest[:n_ex], rest[n_ex]

        def store(acc):
            v = acc if epilogue is None else epilogue(acc, *[r[...] for r in ex_refs])
            o_ref[...] = v.astype(o_ref.dtype)

        p = fn(a_ref[...].astype(BF16), b_ref[...].astype(BF16))
        if nk == 1:
            store(p)
        else:
            acc_ref = rest[n_ex + 1]
            q = pl.program_id(2)

            @pl.when(q == 0)
            def _():
                acc_ref[...] = p

            @pl.when((q > 0) & (q < nk - 1))
            def _():
                acc_ref[...] += p

            @pl.when(q == nk - 1)
            def _():
                store(acc_ref[...] + p)

    return pl.pallas_call(
        body, name=name, grid=(m // tm, n // tn, nk),
        in_specs=[a_spec, b_spec] + [tile] * n_ex, out_specs=tile,
        out_shape=jax.ShapeDtypeStruct((m, n), out_dtype),
        scratch_shapes=[pltpu.VMEM((tm, tn), F32)] if nk > 1 else [],
        compiler_params=_cparams("parallel", "parallel", "arbitrary"),
    )(a, b, *extras)


def _row(arr, width=None, cb=0, lead=None):
    return (arr, width, cb, lead)


def _rowwise(name, fn, rows, vecs, outs, accs=(), tb=256):
    t = rows[0][0].shape[-2]
    tb = min(tb, t)
    in_specs, args = [], []
    for arr, width, cb, lead in rows:
        w = arr.shape[-1] if width is None else width
        if lead is None:
            in_specs.append(pl.BlockSpec((tb, w), lambda i, cb=cb: (i, cb)))
        else:
            in_specs.append(pl.BlockSpec((None, tb, w), lambda i, cb=cb, lead=lead: (lead, i, cb)))
        args.append(arr)
    for v in vecs:
        in_specs.append(pl.BlockSpec(v.shape, lambda i, nd=v.ndim: (0,) * nd))
        args.append(v)
    out_shape = [jax.ShapeDtypeStruct((t, c), dt) for c, dt in outs] + [jax.ShapeDtypeStruct(s, F32) for s in accs]
    out_specs = [pl.BlockSpec((tb, c), lambda i: (i, 0)) for c, _ in outs] + [pl.BlockSpec(s, lambda i: (0, 0)) for s in accs]
    n_in, n_out = len(args), len(outs)

    def body(*refs):
        vals = [r[...] for r in refs[:n_in]]
        o_vals, a_vals = fn(*vals)
        for r, v in zip(refs[n_in:n_in + n_out], o_vals):
            r[...] = v.astype(r.dtype)
        first = pl.program_id(0) == 0
        for r, v in zip(refs[n_in + n_out:], a_vals):
            @pl.when(first)
            def _(r=r, v=v):
                r[...] = v

            @pl.when(jnp.logical_not(first))
            def _(r=r, v=v):
                r[...] += v

    res = pl.pallas_call(
        body, name=name, grid=(t // tb,), in_specs=in_specs, out_specs=out_specs, out_shape=out_shape,
        compiler_params=_cparams("arbitrary"),
    )(*args)
    return res


def _rms_fwd(x, g):
    r = lax.rsqrt(jnp.mean(x * x, axis=-1, keepdims=True) + EPS)
    return x * r * g


def _rms_bwd(x, g, dy):
    r = lax.rsqrt(jnp.mean(x * x, axis=-1, keepdims=True) + EPS)
    xh = x * r
    dxh = dy * g
    dx = r * (dxh - xh * jnp.mean(dxh * xh, axis=-1, keepdims=True))
    return dx, jnp.sum(dy * xh, axis=0, keepdims=True)


def _silu(x):
    return x * jax.nn.sigmoid(x)


def _silu_grad(x):
    s = jax.nn.sigmoid(x)
    return s * (1.0 + x * (1.0 - s))


GROUP_W = D_INNER // SSD_GROUPS


def _gated_norm_fwd(y, z, w):
    u = y * _silu(z)
    parts = []
    for j in range(SSD_GROUPS):
        ug = u[:, j * GROUP_W:(j + 1) * GROUP_W]
        parts.append(ug * lax.rsqrt(jnp.mean(ug * ug, axis=-1, keepdims=True) + EPS))
    return jnp.concatenate(parts, axis=-1) * w


def _gated_norm_bwd(y, z, w, dyn):
    sz = _silu(z)
    u = y * sz
    duh = dyn * w
    du_parts, uh_parts = [], []
    for j in range(SSD_GROUPS):
        sl = slice(j * GROUP_W, (j + 1) * GROUP_W)
        ug = u[:, sl]
        r = lax.rsqrt(jnp.mean(ug * ug, axis=-1, keepdims=True) + EPS)
        uh = ug * r
        dg = duh[:, sl]
        du_parts.append(r * (dg - uh * jnp.mean(dg * uh, axis=-1, keepdims=True)))
        uh_parts.append(uh)
    du = jnp.concatenate(du_parts, axis=-1)
    uh = jnp.concatenate(uh_parts, axis=-1)
    dw = jnp.sum(dyn * uh, axis=0, keepdims=True)
    return du * sz, du * y * _silu_grad(z), dw


HALO = SUBLANES


def _halo_specs(tb, cb, col0, t):
    nblk8 = t // HALO
    per = tb // HALO
    main = pl.BlockSpec((tb, cb), lambda j, i: (i, col0 + j))
    prev = pl.BlockSpec((HALO, cb), lambda j, i: (jnp.maximum(i * per - 1, 0), col0 + j))
    nxt = pl.BlockSpec((HALO, cb), lambda j, i: (jnp.minimum((i + 1) * per, nblk8 - 1), col0 + j))
    return main, prev, nxt


def _fill_ext(ext_ref, cur_ref, prev_ref, next_ref, tb, ni):
    i = pl.program_id(1)
    ext_ref[0:HALO, :] = jnp.where(i > 0, prev_ref[...], 0.0)
    ext_ref[HALO:HALO + tb, :] = cur_ref[...]
    ext_ref[HALO + tb:HALO + tb + HALO, :] = jnp.where(i < ni - 1, next_ref[...], 0.0)


def _conv_fwd(proj, w, b):
    t = proj.shape[0]
    tb, cb = min(512, t), 512
    ni, nj = t // tb, CONV_DIM // cb
    main, prev, nxt = _halo_specs(tb, cb, OFF_XBC // cb, t)
    pad = (SSD_CONV - 1) // 2

    def body(u_ref, up_ref, un_ref, w_ref, b_ref, pre_ref, act_ref, ext_ref):
        _fill_ext(ext_ref, u_ref, up_ref, un_ref, tb, ni)
        acc = jnp.broadcast_to(b_ref[...], (tb, cb))
        for k in range(SSD_CONV):
            acc = acc + w_ref[k:k + 1, :] * ext_ref[pl.ds(HALO + k - pad, tb), :]
        pre_ref[...] = acc
        act_ref[...] = _silu(acc)

    out = pl.BlockSpec((tb, cb), lambda j, i: (i, j))
    return pl.pallas_call(
        body, name="conv_fwd", grid=(nj, ni),
        in_specs=[main, prev, nxt, pl.BlockSpec((SSD_CONV, cb), lambda j, i: (0, j)), pl.BlockSpec((1, cb), lambda j, i: (0, j))],
        out_specs=[out, out],
        out_shape=[jax.ShapeDtypeStruct((t, CONV_DIM), F32)] * 2,
        scratch_shapes=[pltpu.VMEM((tb + 2 * HALO, cb), F32)],
        compiler_params=_cparams("parallel", "arbitrary"),
    )(proj, proj, proj, w, b)


def _conv_bwd(dpre, proj, w):
    t = proj.shape[0]
    tb, cb = min(512, t), 512
    ni, nj = t // tb, CONV_DIM // cb
    umain, uprev, unext = _halo_specs(tb, cb, OFF_XBC // cb, t)
    dmain, dprev, dnext = _halo_specs(tb, cb, 0, t)
    pad = (SSD_CONV - 1) // 2

    def body(d_ref, dp_ref, dn_ref, u_ref, up_ref, un_ref, w_ref, du_ref, dw_ref, db_ref, extd_ref, extu_ref):
        _fill_ext(extd_ref, d_ref, dp_ref, dn_ref, tb, ni)
        _fill_ext(extu_ref, u_ref, up_ref, un_ref, tb, ni)
        d = d_ref[...]
        du = jnp.zeros((tb, cb), F32)
        @pl.when(pl.program_id(1) == 0)
        def _():
            dw_ref[...] = jnp.zeros_like(dw_ref)
            db_ref[...] = jnp.zeros_like(db_ref)

        for k in range(SSD_CONV):
            du = du + w_ref[k:k + 1, :] * extd_ref[pl.ds(HALO - k + pad, tb), :]
            dw_ref[k:k + 1, :] += jnp.sum(d * extu_ref[pl.ds(HALO + k - pad, tb), :], axis=0, keepdims=True)
        du_ref[...] = du.astype(du_ref.dtype)
        db_ref[...] += jnp.sum(d, axis=0, keepdims=True)

    return pl.pallas_call(
        body, name="conv_bwd", grid=(nj, ni),
        in_specs=[dmain, dprev, dnext, umain, uprev, unext, pl.BlockSpec((SSD_CONV, cb), lambda j, i: (0, j))],
        out_specs=[pl.BlockSpec((tb, cb), lambda j, i: (i, j)), pl.BlockSpec((SSD_CONV, cb), lambda j, i: (0, j)),
                   pl.BlockSpec((1, cb), lambda j, i: (0, j))],
        out_shape=[jax.ShapeDtypeStruct((t, CONV_DIM), BF16), jax.ShapeDtypeStruct((SSD_CONV, CONV_DIM), F32),
                   jax.ShapeDtypeStruct((1, CONV_DIM), F32)],
        scratch_shapes=[pltpu.VMEM((tb + 2 * HALO, cb), F32)] * 2,
        compiler_params=_cparams("parallel", "arbitrary"),
    )(dpre, dpre, dpre, proj, proj, proj, w)


PAIR = 2 * SSD_HEAD_DIM
HI = lax.Precision.HIGHEST


def _ssd_prelude(d, dt_ref, dtt_ref, ar_ref, ac_ref):
    li = lax.broadcasted_iota(jnp.int32, (CHUNK, CHUNK), 0)
    si = lax.broadcasted_iota(jnp.int32, (CHUNK, CHUNK), 1)
    fwd = d == 0
    hi, lo = jnp.where(fwd, li, si), jnp.where(fwd, si, li)
    tri = hi >= lo
    trif = tri.astype(F32)
    trit = (hi <= lo).astype(F32)
    dt = dt_ref[...]
    adt = dt * ar_ref[...]
    adtt = dtt_ref[...] * ac_ref[...]
    p = _dot(trif, adt, HI)
    pt = _dot_nt(adtt, trif, HI)
    tot = jnp.sum(adt, axis=0, keepdims=True)
    return tri, trit, dt, p, pt, tot


def _ssd_specs(nc, rev):
    def cidx(d, c):
        up = (d == 1) if rev else (d == 0)
        return jnp.where(up, c, nc - 1 - c)

    specs = [
        pl.BlockSpec((CHUNK, D_INNER), lambda d, c: (cidx(d, c), 0)),
        pl.BlockSpec((CHUNK, 1024), lambda d, c: (cidx(d, c), 2)),
        pl.BlockSpec((CHUNK, 1024), lambda d, c: (cidx(d, c), 3)),
        pl.BlockSpec((None, CHUNK, SSD_HEADS), lambda d, c: (d, cidx(d, c), 0)),
        pl.BlockSpec((None, SSD_HEADS, CHUNK), lambda d, c: (d, 0, cidx(d, c))),
        pl.BlockSpec((None, 1, SSD_HEADS), lambda d, c: (d, 0, 0)),
        pl.BlockSpec((None, SSD_HEADS, 1), lambda d, c: (d, 0, 0)),
        pl.BlockSpec((None, CHUNK, D_INNER), lambda d, c: (d, cidx(d, c), 0)),
    ]
    return cidx, specs


def _head_decay(tri, p, pt, tot, h):
    pb = jnp.broadcast_to(p[:, h:h + 1], (CHUNK, CHUNK))
    dec = jnp.exp(jnp.where(tri, pb - pt[h:h + 1, :], NEG))
    return dec, jnp.exp(tot[:, h:h + 1] - pb), jnp.exp(pb)


def _ssd_fwd(act, dt2, dt2t, a_row, a_col, dte):
    t = act.shape[0]
    nc = t // CHUNK
    cidx, specs = _ssd_specs(nc, rev=False)

    def body(xs_ref, bs_ref, cs_ref, dt_ref, dtt_ref, ar_ref, ac_ref, dte_ref, y_ref, st_ref, h_ref):
        d, c = pl.program_id(0), pl.program_id(1)

        @pl.when(c == 0)
        def _():
            h_ref[...] = jnp.zeros_like(h_ref)

        st_ref[...] = h_ref[...]
        tri, _, _, p, pt, tot = _ssd_prelude(d, dt_ref, dtt_ref, ar_ref, ac_ref)
        etot = jnp.exp(tot)
        lane = lax.broadcasted_iota(jnp.int32, (CHUNK, PAIR), 1) < SSD_HEAD_DIM
        rowh = lax.broadcasted_iota(jnp.int32, (PAIR, SSD_STATE), 0) < SSD_HEAD_DIM
        for g in range(SSD_GROUPS):
            gs = slice(g * SSD_STATE, (g + 1) * SSD_STATE)
            bg = bs_ref[:, gs]
            cb = cs_ref[:, gs].astype(BF16)
            cbm = _dot_nt(cb, bg.astype(BF16))
            for pr in range(2):
                h0 = g * 4 + pr * 2
                h1 = h0 + 1
                sl = slice(h0 * SSD_HEAD_DIM, h0 * SSD_HEAD_DIM + PAIR)
                xdt = (xs_ref[:, sl] * dte_ref[:, sl]).astype(BF16)
                yd, st, epb = [], [], []
                for h in (h0, h1):
                    dec, wb, eb = _head_decay(tri, p, pt, tot, h)
                    yd.append(_dot((cbm * dec).astype(BF16), xdt))
                    st.append(_dot_tn(xdt, (bg * wb).astype(BF16)))
                    epb.append(eb)
                hin = h_ref[sl, :]
                yo = _dot_nt(cb, hin.astype(BF16)) * jnp.where(lane, epb[0], epb[1])
                y_ref[:, sl] = jnp.where(lane, yd[0], yd[1]) + yo
                et = jnp.where(rowh, etot[:, h0:h0 + 1], etot[:, h1:h1 + 1])
                h_ref[sl, :] = hin * et + jnp.where(rowh, st[0], st[1])

    return pl.pallas_call(
        body, name="ssd_fwd", grid=(2, nc), in_specs=specs,
        out_specs=[pl.BlockSpec((None, CHUNK, D_INNER), lambda d, c: (d, cidx(d, c), 0)),
                   pl.BlockSpec((None, None, D_INNER, SSD_STATE), lambda d, c: (d, cidx(d, c), 0, 0))],
        out_shape=[jax.ShapeDtypeStruct((2, t, D_INNER), F32), jax.ShapeDtypeStruct((2, nc, D_INNER, SSD_STATE), F32)],
        scratch_shapes=[pltpu.VMEM((D_INNER, SSD_STATE), F32)],
        compiler_params=_cparams("arbitrary", "arbitrary"),
    )(act, act, act, dt2, dt2t, a_row, a_col, dte)


def _put_col(acc, col, h):
    lane = lax.broadcasted_iota(jnp.int32, acc.shape, 1)
    return jnp.where(lane == h, col, acc)


def _put_row(acc, row, h):
    sub = lax.broadcasted_iota(jnp.int32, acc.shape, 0)
    return jnp.where(sub == h, row, acc)


def _sum_all(x):
    return jnp.sum(jnp.sum(x, axis=0, keepdims=True), axis=1, keepdims=True)


def _ssd_bwd(act, dt2, dt2t, a_row, a_col, dte, dy, states):
    t = act.shape[0]
    nc = t // CHUNK
    cidx, specs = _ssd_specs(nc, rev=True)
    specs = specs + [
        pl.BlockSpec((CHUNK, D_INNER), lambda d, c: (cidx(d, c), 0)),
        pl.BlockSpec((None, None, D_INNER, SSD_STATE), lambda d, c: (d, cidx(d, c), 0, 0)),
    ]

    def body(xs_ref, bs_ref, cs_ref, dt_ref, dtt_ref, ar_ref, ac_ref, dte_ref, dy_ref, st_ref,
             dxs_ref, dbs_ref, dcs_ref, ddt_ref, da_ref, dh_ref):
        d, c = pl.program_id(0), pl.program_id(1)

        @pl.when(c == 0)
        def _():
            dh_ref[...] = jnp.zeros_like(dh_ref)
            da_ref[...] = jnp.zeros_like(da_ref)

        tri, trit, dt, p, pt, tot = _ssd_prelude(d, dt_ref, dtt_ref, ar_ref, ac_ref)
        etot = jnp.exp(tot)
        lane = lax.broadcasted_iota(jnp.int32, (CHUNK, PAIR), 1) < SSD_HEAD_DIM
        rowh = lax.broadcasted_iota(jnp.int32, (PAIR, SSD_STATE), 0) < SSD_HEAD_DIM
        first_head = lax.broadcasted_iota(jnp.int32, (PAIR, LANES), 0) < SSD_HEAD_DIM
        out_lane = lax.broadcasted_iota(jnp.int32, (PAIR, LANES), 1)
        ddtx = jnp.zeros((CHUNK, LANES), F32)
        lane32 = lax.broadcasted_iota(jnp.int32, (CHUNK, SSD_HEADS), 1)
        dp_col = jnp.zeros((CHUNK, SSD_HEADS), F32)
        dp_row = jnp.zeros((SSD_HEADS, CHUNK), F32)
        dtot = jnp.zeros((1, SSD_HEADS), F32)
        for g in range(SSD_GROUPS):
            gs = slice(g * SSD_STATE, (g + 1) * SSD_STATE)
            bg = bs_ref[:, gs]
            bb = bg.astype(BF16)
            cb = cs_ref[:, gs].astype(BF16)
            cbm = _dot_nt(cb, bb)
            dcb = jnp.zeros((CHUNK, CHUNK), F32)
            dc_acc = jnp.zeros((CHUNK, SSD_STATE), F32)
            db_acc = jnp.zeros((CHUNK, SSD_STATE), F32)
            for pr in range(2):
                h0 = g * 4 + pr * 2
                h1 = h0 + 1
                sl = slice(h0 * SSD_HEAD_DIM, h0 * SSD_HEAD_DIM + PAIR)
                xp = xs_ref[:, sl]
                dtp = dte_ref[:, sl]
                xdt_f = xp * dtp
                xdt = xdt_f.astype(BF16)
                dyp = dy_ref[:, sl]
                dyb = dyp.astype(BF16)
                hin = st_ref[sl, :]
                dh = dh_ref[sl, :]
                hb = hin.astype(BF16)
                dhb = dh.astype(BF16)
                heads = [_head_decay(tri, p, pt, tot, h) for h in (h0, h1)]
                dye = dyp * jnp.where(lane, heads[0][2], heads[1][2])
                dyeb = dye.astype(BF16)
                gy = _dot_nt(cb, hb) * dye
                dc_acc = dc_acc + _dot(dyeb, hb)
                dhin = _dot_tn(dyeb, cb)
                hh = dh * hin
                dxdt = jnp.zeros((CHUNK, PAIR), F32)
                for idx, h in enumerate((h0, h1)):
                    hm = lane if idx == 0 else jnp.logical_not(lane)
                    rm = rowh if idx == 0 else jnp.logical_not(rowh)
                    dec, wb, _ = heads[idx]
                    mf = cbm * dec
                    t1 = _dot_tn(mf.astype(BF16), dyb)
                    t2 = _dot_nt((bg * wb).astype(BF16), dhb)
                    dxdt = jnp.where(hm, t1 + t2, dxdt)
                    dm = _dot_nt(jnp.where(hm, dyp, 0.0).astype(BF16), xdt)
                    dcb = dcb + dm * dec
                    e = dm * mf
                    qw = _dot(jnp.where(hm, xdt_f, 0.0).astype(BF16), dhb) * wb
                    db_acc = db_acc + qw
                    qwb = qw * bg
                    col = jnp.sum(e + jnp.where(hm, gy, 0.0) - qwb, axis=1, keepdims=True)
                    dp_col = jnp.where(lane32 == h, col, dp_col)
                    dp_row = _put_row(dp_row, -jnp.sum(e, axis=0, keepdims=True), h)
                    dtot_h = _sum_all(qwb) + etot[:, h:h + 1] * _sum_all(jnp.where(rm, hh, 0.0))
                    dtot = _put_col(dtot, dtot_h, h)
                dxs_ref[:, sl] = dxdt * dtp
                ddx = dxdt * xp
                ddx_hi = ddx.astype(BF16)
                ddx_lo = (ddx - ddx_hi.astype(F32)).astype(BF16)
                route = (out_lane == jnp.where(first_head, h0, h1)).astype(BF16)
                ddtx = ddtx + _dot(ddx_hi, route) + _dot(ddx_lo, route)
                et = jnp.where(rowh, etot[:, h0:h0 + 1], etot[:, h1:h1 + 1])
                dh_ref[sl, :] = dh * et + dhin
            dcbb = dcb.astype(BF16)
            dcs_ref[:, gs] = _dot(dcbb, bb) + dc_acc
            dbs_ref[:, gs] = _dot_tn(dcbb, cb) + db_acc
        d_adt = _dot(trit, dp_col, HI) + _dot_nt(trit, dp_row, HI) + dtot
        ddt_ref[...] = ddtx[:, :SSD_HEADS] + ar_ref[...] * d_adt
        da_ref[...] += jnp.sum(dt * d_adt, axis=0, keepdims=True)

    return pl.pallas_call(
        body, name="ssd_bwd", grid=(2, nc), in_specs=specs,
        out_specs=[pl.BlockSpec((None, CHUNK, D_INNER), lambda d, c: (d, cidx(d, c), 0)),
                   pl.BlockSpec((None, CHUNK, 1024), lambda d, c: (d, cidx(d, c), 0)),
                   pl.BlockSpec((None, CHUNK, 1024), lambda d, c: (d, cidx(d, c), 0)),
                   pl.BlockSpec((None, CHUNK, SSD_HEADS), lambda d, c: (d, cidx(d, c), 0)),
                   pl.BlockSpec((None, 1, SSD_HEADS), lambda d, c: (d, 0, 0))],
        out_shape=[jax.ShapeDtypeStruct((2, t, D_INNER), F32), jax.ShapeDtypeStruct((2, t, 1024), F32),
                   jax.ShapeDtypeStruct((2, t, 1024), F32), jax.ShapeDtypeStruct((2, t, SSD_HEADS), F32),
                   jax.ShapeDtypeStruct((2, 1, SSD_HEADS), F32)],
        scratch_shapes=[pltpu.VMEM((D_INNER, SSD_STATE), F32)],
        compiler_params=_cparams("arbitrary", "arbitrary"),
    )(act, act, act, dt2, dt2t, a_row, a_col, dte, dy, states)


REP = ATTN_HEADS // ATTN_KV
SCALE = ATTN_DIM ** -0.5


def _attn_specs(nb):
    q = pl.BlockSpec((None, REP, BLOCK, ATTN_DIM), lambda g, n: (g, 0, n, 0))
    kp = pl.BlockSpec((None, BLOCK, ATTN_DIM), lambda g, n: (g, jnp.maximum(n - 1, 0), 0))
    kc = pl.BlockSpec((None, BLOCK, ATTN_DIM), lambda g, n: (g, n, 0))
    kn = pl.BlockSpec((None, BLOCK, ATTN_DIM), lambda g, n: (g, jnp.minimum(n + 1, nb - 1), 0))
    bias = pl.BlockSpec((None, REP, BLOCK, 3 * BLOCK), lambda g, n: (g, 0, 0, 0))
    sink = pl.BlockSpec((None, REP, 1, LANES), lambda g, n: (g, 0, 0, 0))
    return q, kp, kc, kn, bias, sink


def _band_mask():
    ii = lax.broadcasted_iota(jnp.int32, (BLOCK, 3 * BLOCK), 0)
    jj = lax.broadcasted_iota(jnp.int32, (BLOCK, 3 * BLOCK), 1)
    return (jj >= ii) & (jj - 2 * BLOCK <= ii)


def _attn_valid(n, nb):
    jj = lax.broadcasted_iota(jnp.int32, (1, 3 * BLOCK), 1)
    return ((jj >= BLOCK) | (n > 0)) & ((jj < 2 * BLOCK) | (n < nb - 1))


def _attn_probs(q, kcat, bias, snk, valid):
    s = jnp.where(valid, _dot_nt(q, kcat) + bias, NEG)
    m = jnp.maximum(jnp.max(s, axis=1, keepdims=True), snk)
    p = jnp.exp(s - m)
    es = jnp.exp(snk - m)
    r = 1.0 / (jnp.sum(p, axis=1, keepdims=True) + es)
    return p * r, es * r


def _attn_fwd(q4, k3, v3, bias4, sink4):
    t = q4.shape[2]
    nb = t // BLOCK
    qs, kp, kc, kn, bs, ss = _attn_specs(nb)

    def body(q_ref, kp_ref, kc_ref, kn_ref, vp_ref, vc_ref, vn_ref, b_ref, s_ref, o_ref):
        n = pl.program_id(1)
        kcat = jnp.concatenate([kp_ref[...], kc_ref[...], kn_ref[...]], axis=0)
        vcat = jnp.concatenate([vp_ref[...], vc_ref[...], vn_ref[...]], axis=0)
        valid = _attn_valid(n, nb)
        for r in range(REP):
            pn, _ = _attn_probs(q_ref[r] * SCALE, kcat, b_ref[r], s_ref[r][:, 0:1], valid)
            o_ref[r] = _dot(pn.astype(BF16), vcat).astype(o_ref.dtype)

    return pl.pallas_call(
        body, name="attn_fwd", grid=(ATTN_KV, nb), in_specs=[qs, kp, kc, kn, kp, kc, kn, bs, ss],
        out_specs=qs, out_shape=jax.ShapeDtypeStruct(q4.shape, BF16),
        compiler_params=_cparams("parallel", "arbitrary"),
    )(q4, k3, k3, k3, v3, v3, v3, bias4, sink4)


def _attn_bwd(q4, k3, v3, bias4, sink4, do4):
    t = q4.shape[2]
    nb = t // BLOCK
    qs, kp, kc, kn, bs, ss = _attn_specs(nb)
    part = pl.BlockSpec((3, None, BLOCK, ATTN_DIM), lambda g, n: (0, g, n, 0))

    def body(q_ref, kp_ref, kc_ref, kn_ref, vp_ref, vc_ref, vn_ref, b_ref, s_ref, do_ref,
             dq_ref, dk_ref, dv_ref, db_ref, ds_ref):
        n = pl.program_id(1)

        @pl.when(n == 0)
        def _():
            db_ref[...] = jnp.zeros_like(db_ref)
            ds_ref[...] = jnp.zeros_like(ds_ref)

        kcat = jnp.concatenate([kp_ref[...], kc_ref[...], kn_ref[...]], axis=0)
        vcat = jnp.concatenate([vp_ref[...], vc_ref[...], vn_ref[...]], axis=0)
        valid = _attn_valid(n, nb)
        dk = jnp.zeros((3 * BLOCK, ATTN_DIM), F32)
        dv = jnp.zeros((3 * BLOCK, ATTN_DIM), F32)
        for r in range(REP):
            q = q_ref[r]
            do = do_ref[r]
            pn, psink = _attn_probs(q * SCALE, kcat, b_ref[r], s_ref[r][:, 0:1], valid)
            dp = _dot_nt(do, vcat)
            delta = jnp.sum(pn * dp, axis=1, keepdims=True)
            dsc = pn * (dp - delta)
            db_ref[r] += dsc
            ds_ref[r] += jnp.broadcast_to(-jnp.sum(psink * delta, axis=0, keepdims=True), (1, LANES))
            dsb = (dsc * SCALE).astype(BF16)
            dq_ref[r] = _dot(dsb, kcat).astype(dq_ref.dtype)
            dk = dk + _dot_tn(dsb, q)
            dv = dv + _dot_tn(pn.astype(BF16), do)
        for j in range(3):
            dk_ref[j] = dk[j * BLOCK:(j + 1) * BLOCK]
            dv_ref[j] = dv[j * BLOCK:(j + 1) * BLOCK]

    return pl.pallas_call(
        body, name="attn_bwd", grid=(ATTN_KV, nb), in_specs=[qs, kp, kc, kn, kp, kc, kn, bs, ss, qs],
        out_specs=[qs, part, part, bs, ss],
        out_shape=[jax.ShapeDtypeStruct(q4.shape, BF16), jax.ShapeDtypeStruct((3,) + k3.shape, F32),
                   jax.ShapeDtypeStruct((3,) + k3.shape, F32), jax.ShapeDtypeStruct(bias4.shape, F32),
                   jax.ShapeDtypeStruct(sink4.shape, F32)],
        compiler_params=_cparams("parallel", "arbitrary"),
    )(q4, k3, k3, k3, v3, v3, v3, bias4, sink4, do4)


def _kv_combine(name, parts):
    _, kv, t, dd = parts.shape

    def body(p_ref, o_ref):
        z = jnp.zeros((BLOCK, dd), F32)
        from_next = jnp.concatenate([p_ref[0, BLOCK:, :], z], axis=0)
        from_prev = jnp.concatenate([z, p_ref[2, :t - BLOCK, :]], axis=0)
        o_ref[...] = (from_next + p_ref[1] + from_prev).astype(o_ref.dtype)

    return pl.pallas_call(
        body, name=name, grid=(kv,),
        in_specs=[pl.BlockSpec((3, None, t, dd), lambda g: (0, g, 0, 0))],
        out_specs=pl.BlockSpec((None, t, dd), lambda g: (g, 0, 0)),
        out_shape=jax.ShapeDtypeStruct((kv, t, dd), BF16),
        compiler_params=_cparams("parallel"),
    )(parts)


def _t5_bucket(rel):
    nb = N_BUCKETS // 2
    max_exact = nb // 2
    ret = jnp.where(rel > 0, nb, 0)
    n = jnp.abs(rel)
    nf = jnp.maximum(n, 1).astype(jnp.float32)
    large = max_exact + (jnp.log(nf / max_exact) / math.log(MAX_DISTANCE / max_exact) * (nb - max_exact)).astype(jnp.int32)
    large = jnp.minimum(large, nb - 1)
    return ret + jnp.where(n < max_exact, n, large)


def _bucket_map():
    i = jnp.arange(BLOCK)[:, None]
    j = jnp.arange(3 * BLOCK)[None, :]
    return _t5_bucket(j - BLOCK - i)


def _bias_from_table(table, onehot_t):
    def body(t_ref, o_ref, out_ref):
        out_ref[...] = _dot(t_ref[...], o_ref[...], HI)

    return pl.pallas_call(body, name="bias_from_table", out_shape=jax.ShapeDtypeStruct((ATTN_HEADS, onehot_t.shape[1]), F32),
                          compiler_params=pltpu.CompilerParams(vmem_limit_bytes=VMEM_LIMIT_BYTES))(table.T, onehot_t)


def _bias_table_grad(dbias, onehot_t):
    def body(d_ref, o_ref, out_ref):
        out_ref[...] = _dot_nt(d_ref[...], o_ref[...], HI)

    return pl.pallas_call(body, name="bias_table_grad", out_shape=jax.ShapeDtypeStruct((ATTN_HEADS, N_BUCKETS), F32),
                          compiler_params=pltpu.CompilerParams(vmem_limit_bytes=VMEM_LIMIT_BYTES))(dbias, onehot_t)


HBM_SPEC = pl.BlockSpec(memory_space=pl.ANY)


def _allgather_chips(name, x):
    r2 = x.shape[0] // 2

    def body(x_ref, out_ref, send_sems, recv_sems, local_sem):
        mx, my, mc = lax.axis_index("x"), lax.axis_index("y"), lax.axis_index("c")
        me = 2 * mx + my
        chips = [(1 - mx, my), (mx, 1 - my), (1 - mx, 1 - my)]
        sibling = (mx, my, 1 - mc)

        def part(slot, h):
            return out_ref.at[slot, pl.ds(h * r2, r2)]

        def copy(k, src, dst, to):
            return pltpu.make_async_remote_copy(src_ref=src, dst_ref=dst, send_sem=send_sems.at[k], recv_sem=recv_sems.at[k],
                                                device_id=to, device_id_type=MESH)

        mine = pltpu.make_async_copy(x_ref, out_ref.at[me], local_sem)
        mine.start()
        first = [copy(k, x_ref.at[pl.ds(mc * r2, r2)], part(me, mc), (px, py, mc)) for k, (px, py) in enumerate(chips)]
        for cp in first:
            cp.start()
        passed = [copy(3 + k, part(2 * px + py, mc), part(2 * px + py, mc), sibling) for k, (px, py) in enumerate(chips)]
        for k, (px, py) in enumerate(chips):
            copy(k, part(2 * px + py, mc), part(2 * px + py, mc), (px, py, mc)).wait_recv()
            passed[k].start()
        for k, (px, py) in enumerate(chips):
            copy(3 + k, part(2 * px + py, 1 - mc), part(2 * px + py, 1 - mc), sibling).wait_recv()
        for cp in first + passed:
            cp.wait_send()
        mine.wait()

    return pl.pallas_call(
        body, name=name, in_specs=[HBM_SPEC], out_specs=HBM_SPEC,
        out_shape=jax.ShapeDtypeStruct((4,) + x.shape, x.dtype),
        scratch_shapes=[pltpu.SemaphoreType.DMA((6,)), pltpu.SemaphoreType.DMA((6,)), pltpu.SemaphoreType.DMA],
    )(x)


def _swap_cores(name, x):
    def body(x_ref, out_ref, send_sem, recv_sem):
        mx, my, mc = lax.axis_index("x"), lax.axis_index("y"), lax.axis_index("c")
        send = pltpu.make_async_remote_copy(src_ref=x_ref.at[1 - mc], dst_ref=out_ref, send_sem=send_sem, recv_sem=recv_sem,
                                            device_id=(mx, my, 1 - mc), device_id_type=MESH)
        send.start()
        send.wait()

    return pl.pallas_call(
        body, name=name, in_specs=[HBM_SPEC], out_specs=HBM_SPEC,
        out_shape=jax.ShapeDtypeStruct(x.shape[1:], x.dtype),
        scratch_shapes=[pltpu.SemaphoreType.DMA, pltpu.SemaphoreType.DMA],
    )(x)


def _scatter_chips(name, g):
    def body(g_ref, out_ref, send_sems, recv_sems, local_sem):
        mx, my, mc = lax.axis_index("x"), lax.axis_index("y"), lax.axis_index("c")
        me = 2 * mx + my
        chips = [(1 - mx, my), (mx, 1 - my), (1 - mx, 1 - my)]
        mine = pltpu.make_async_copy(g_ref.at[me], out_ref.at[me], local_sem)
        mine.start()
        sends = [pltpu.make_async_remote_copy(src_ref=g_ref.at[2 * px + py], dst_ref=out_ref.at[me], send_sem=send_sems.at[k],
                                              recv_sem=recv_sems.at[k], device_id=(px, py, mc), device_id_type=MESH)
                 for k, (px, py) in enumerate(chips)]
        for cp in sends:
            cp.start()
        for k, (px, py) in enumerate(chips):
            pltpu.make_async_remote_copy(src_ref=g_ref.at[0], dst_ref=out_ref.at[2 * px + py], send_sem=send_sems.at[k],
                                         recv_sem=recv_sems.at[k], device_id=(px, py, mc), device_id_type=MESH).wait_recv()
        for cp in sends:
            cp.wait_send()
        mine.wait()

    return pl.pallas_call(
        body, name=name, in_specs=[HBM_SPEC], out_specs=HBM_SPEC,
        out_shape=jax.ShapeDtypeStruct(g.shape, g.dtype),
        scratch_shapes=[pltpu.SemaphoreType.DMA((3,)), pltpu.SemaphoreType.DMA((3,)), pltpu.SemaphoreType.DMA],
    )(g)


def _allgather_all(name, x):
    def body(x_ref, out_ref, send_sems, recv_sems, local_sem):
        mx, my, mc = lax.axis_index("x"), lax.axis_index("y"), lax.axis_index("c")
        me = 4 * mx + 2 * my + mc
        flips = [(fx, fy, fc) for fx in (0, 1) for fy in (0, 1) for fc in (0, 1)][1:]
        peers = [(mx ^ fx, my ^ fy, mc ^ fc) for fx, fy, fc in flips]
        mine = pltpu.make_async_copy(x_ref, out_ref.at[me], local_sem)
        mine.start()
        sends = [pltpu.make_async_remote_copy(src_ref=x_ref, dst_ref=out_ref.at[me], send_sem=send_sems.at[k],
                                              recv_sem=recv_sems.at[k], device_id=peer, device_id_type=MESH)
                 for k, peer in enumerate(peers)]
        for cp in sends:
            cp.start()
        for k, (px, py, pc) in enumerate(peers):
            pltpu.make_async_remote_copy(src_ref=x_ref, dst_ref=out_ref.at[4 * px + 2 * py + pc], send_sem=send_sems.at[k],
                                         recv_sem=recv_sems.at[k], device_id=(px, py, pc), device_id_type=MESH).wait_recv()
        for cp in sends:
            cp.wait_send()
        mine.wait()

    return pl.pallas_call(
        body, name=name, in_specs=[HBM_SPEC], out_specs=HBM_SPEC,
        out_shape=jax.ShapeDtypeStruct((8,) + x.shape, x.dtype),
        scratch_shapes=[pltpu.SemaphoreType.DMA((7,)), pltpu.SemaphoreType.DMA((7,)), pltpu.SemaphoreType.DMA],
    )(x)


def _allgather_cores(name, x):
    def body(x_ref, out_ref, send_sem, recv_sem, local_sem):
        mx, my, mc = lax.axis_index("x"), lax.axis_index("y"), lax.axis_index("c")
        mine = pltpu.make_async_copy(x_ref, out_ref.at[mc], local_sem)
        mine.start()
        send = pltpu.make_async_remote_copy(src_ref=x_ref, dst_ref=out_ref.at[mc], send_sem=send_sem, recv_sem=recv_sem,
                                            device_id=(mx, my, 1 - mc), device_id_type=MESH)
        send.start()
        pltpu.make_async_remote_copy(src_ref=x_ref, dst_ref=out_ref.at[1 - mc], send_sem=send_sem, recv_sem=recv_sem,
                                     device_id=(mx, my, 1 - mc), device_id_type=MESH).wait_recv()
        send.wait_send()
        mine.wait()

    return pl.pallas_call(
        body, name=name, in_specs=[HBM_SPEC], out_specs=HBM_SPEC,
        out_shape=jax.ShapeDtypeStruct((2,) + x.shape, x.dtype),
        scratch_shapes=[pltpu.SemaphoreType.DMA, pltpu.SemaphoreType.DMA, pltpu.SemaphoreType.DMA],
    )(x)


def _sum_slots(name, st, tb=256):
    n, r, c = st.shape
    tb = _pick(r, (tb, 32))

    def body(s_ref, o_ref):
        acc = s_ref[0].astype(F32)
        for k in range(1, n):
            acc = acc + s_ref[k].astype(F32)
        o_ref[...] = acc

    return pl.pallas_call(
        body, name=name, grid=(r // tb,), in_specs=[pl.BlockSpec((n, tb, c), lambda i: (0, i, 0))],
        out_specs=pl.BlockSpec((tb, c), lambda i: (i, 0)), out_shape=jax.ShapeDtypeStruct((r, c), F32),
        compiler_params=_cparams("parallel"),
    )(st)


def _adamw(name, w, g, m, v):
    def fn(w, g, m, v):
        m2 = ADAM_B1 * m + (1.0 - ADAM_B1) * g
        v2 = ADAM_B2 * v + (1.0 - ADAM_B2) * jnp.square(g)
        m_hat = m2 / (1.0 - ADAM_B1 ** ADAM_STEP)
        v_hat = v2 / (1.0 - ADAM_B2 ** ADAM_STEP)
        delta = -ADAM_LR * (m_hat / (jnp.sqrt(v_hat) + ADAM_EPS) + ADAM_WD * w)
        return [delta, m2, v2], []

    tb = _pick(w.shape[0], (256, 32))
    return _rowwise(name, fn, [_row(w), _row(g), _row(m), _row(v)], [], [(w.shape[1], F32)] * 3, tb=tb)


BIG = ("w_in", "w_ssd_out", "w_attn_out", "w_o", "w_mlp_in", "w_mlp_out", "conv_w")
SMALL = ("pre_mix_norm", "b_gate", "conv_b", "dt_bias", "a_log", "d_skip", "ssd_norm", "attn_sink", "rel_bias_table",
         "post_mix_norm", "pre_mlp_norm", "post_mlp_norm")
ALL_W = ("pre_mix_norm", "w_in", "b_gate", "conv_w", "conv_b", "dt_bias", "a_log", "d_skip", "ssd_norm", "w_ssd_out",
         "attn_sink", "rel_bias_table", "w_attn_out", "w_o", "post_mix_norm", "pre_mlp_norm", "w_mlp_in", "w_mlp_out",
         "post_mlp_norm")


def _pack_rows(parts, rows, dtype):
    flat = jnp.concatenate([p.reshape(-1, D_MODEL).astype(dtype) for p in parts], axis=0)
    return jnp.pad(flat, ((0, rows - flat.shape[0]), (0, 0)))


def _pack_big(shards, dtype):
    return _pack_rows([shards[n] for n in BIG], BIG_ROWS, dtype)


def _unpack_big(flat, like):
    out, r = {}, 0
    for n in BIG:
        shp = like[n].shape
        nr = math.prod(shp) // D_MODEL
        out[n] = flat[r:r + nr].reshape(shp)
        r += nr
    return out


def _pack_small(parts, extra=None):
    flat = jnp.concatenate([parts[n].reshape(-1).astype(F32) for n in SMALL] + ([extra.reshape(-1)] if extra is not None else []))
    return jnp.pad(flat, (0, SMALL_ROWS * D_MODEL - flat.shape[0])).reshape(SMALL_ROWS, D_MODEL)


def _unpack_small(flat2, like):
    flat = flat2.reshape(-1)
    out, r = {}, 0
    for n in SMALL:
        shp = like[n].shape
        k = math.prod(shp)
        out[n] = flat[r:r + k].reshape(shp)
        r += k
    return out, flat[r]


def _shard_of_full(name, full, s):
    if name in ("w_in", "w_mlp_in"):
        w = full.shape[2] // 4
        return full[:, :, s * w:(s + 1) * w]
    if name == "conv_w":
        w = full.shape[3] // 4
        return full[:, :, :, s * w:(s + 1) * w]
    w = full.shape[1] // 4
    return full[:, s * w:(s + 1) * w, :]


def _full_of_shards(name, shards):
    axis = {"w_in": 2, "w_mlp_in": 2, "conv_w": 3}.get(name, 1)
    return jnp.concatenate(shards, axis=axis)


def _to_proj_layout(w):
    z, xbc, dt, q, k, v, gates = (w[..., 0:2048], w[..., 2048:6144], w[..., 6144:6208], w[..., 6208:7232],
                                  w[..., 7232:7488], w[..., 7488:7744], w[..., 7744:9792])
    pad = jnp.zeros(w.shape[:-1] + (N_PROJ - N_IN,), w.dtype)
    return jnp.concatenate([z, gates, xbc, dt, pad, q, k, v], axis=-1)


def _from_proj_layout(w):
    z, gates, xbc, dt, q, k, v = (w[..., 0:2048], w[..., 2048:4096], w[..., 4096:8192], w[..., 8192:8256],
                                  w[..., 8320:9344], w[..., 9344:9600], w[..., 9600:9856])
    return jnp.concatenate([z, xbc, dt, q, k, v, gates], axis=-1)


def _heads_major(a, nh):
    t = a.shape[0]
    return a.reshape(t, nh, ATTN_DIM).transpose(1, 0, 2)


def _tokens_major(a):
    nh, t, dd = a.shape
    return a.transpose(1, 0, 2).reshape(t, nh * dd)


def _layer_fwd(h1, x, W, P, l, bias4):
    t = x.shape[0]
    S = {"x": x, "h1": h1}
    proj = _matmul("proj", h1, W["w_in_main"][l], "nn")
    qkv = _matmul("proj_qkv", h1, W["w_in_qkv"][l], "nn", out_dtype=BF16)
    S["proj"] = proj
    pre, act = _conv_fwd(proj, W["conv_w"][l], P["conv_b"][l].reshape(1, CONV_DIM))
    S["pre"], S["act"] = pre, act

    dtb = jnp.pad(P["dt_bias"][l].reshape(1, 2 * SSD_HEADS), ((0, 0), (0, LANES - 2 * SSD_HEADS)))

    def dt_fn(raw, b):
        v = raw + b
        return [jnp.maximum(v, 0.0) + jnp.log1p(jnp.exp(-jnp.abs(v)))], []

    (dt,) = _rowwise("dt_fwd", dt_fn, [_row(proj, LANES, OFF_DT // LANES)], [dtb], [(LANES, F32)], tb=512)
    dt2 = jnp.stack([dt[:, 0:SSD_HEADS], dt[:, SSD_HEADS:2 * SSD_HEADS]])
    dt2t = dt2.transpose(0, 2, 1)
    a = -jnp.exp(P["a_log"][l])
    a_row, a_col = a.reshape(2, 1, SSD_HEADS), a.reshape(2, SSD_HEADS, 1)
    dte = jnp.repeat(dt2, SSD_HEAD_DIM, axis=-1)
    S["dt2"], S["dt2t"], S["a_row"], S["a_col"], S["dte"] = dt2, dt2t, a_row, a_col, dte
    y2, states = _ssd_fwd(act, dt2, dt2t, a_row, a_col, dte)
    S["states"] = states

    dsk = jnp.repeat(P["d_skip"][l], SSD_HEAD_DIM).reshape(1, D_INNER)
    nw = P["ssd_norm"][l].reshape(1, D_INNER)
    S["dsk"], S["nw"] = dsk, nw

    def gn_fn(yf, yb, xs, z, dsk, nw):
        y = yf + yb + xs * dsk
        return [y, _gated_norm_fwd(y, z, nw)], []

    y, yn = _rowwise("gated_norm_fwd", gn_fn,
                     [_row(y2, lead=0), _row(y2, lead=1), _row(act, D_INNER, 0), _row(proj, D_INNER, OFF_Z // D_INNER)],
                     [dsk, nw], [(D_INNER, F32), (D_INNER, BF16)])
    S["y"], S["yn"] = y, yn
    y_ssd = _matmul("ssd_out", yn, W["w_ssd_out"][l], "nn")
    S["y_ssd"] = y_ssd

    q4 = _heads_major(qkv[:, 0:1024], ATTN_HEADS).reshape(ATTN_KV, REP, t, ATTN_DIM)
    k3 = _heads_major(qkv[:, 1024:1280], ATTN_KV)
    v3 = _heads_major(qkv[:, 1280:1536], ATTN_KV)
    sink4 = jnp.broadcast_to(P["attn_sink"][l].reshape(ATTN_KV, REP, 1, 1), (ATTN_KV, REP, 1, LANES))
    S["q4"], S["k3"], S["v3"], S["sink4"] = q4, k3, v3, sink4
    o4 = _attn_fwd(q4, k3, v3, bias4, sink4)
    o = _tokens_major(o4.reshape(ATTN_HEADS, t, ATTN_DIM))
    S["o"] = o
    y_attn = _matmul("attn_out", o, W["w_attn_out"][l], "nn")
    S["y_attn"] = y_attn

    bg = P["b_gate"][l].reshape(1, 2 * D_MODEL)
    S["bg"] = bg

    def merge_fn(gates, ys, ya, b):
        g = jax.nn.sigmoid(gates + b)
        return [g[:, :D_MODEL] * ys + g[:, D_MODEL:] * ya], []

    (mix_in,) = _rowwise("merge_fwd", merge_fn, [_row(proj, 2 * D_MODEL, OFF_G // (2 * D_MODEL)), _row(y_ssd), _row(y_attn)],
                         [bg], [(D_MODEL, BF16)])
    S["mix_in"] = mix_in
    mixed = _matmul("w_o", mix_in, W["w_o"][l], "nn")
    S["mixed"] = mixed

    g_pm = P["post_mix_norm"][l].reshape(1, D_MODEL)
    g_pl = P["pre_mlp_norm"][l].reshape(1, D_MODEL)

    def postmix_fn(x, mixed, g1, g2):
        x2 = x + _rms_fwd(mixed, g1)
        return [x2, _rms_fwd(x2, g2)], []

    x2, h2 = _rowwise("post_mix_fwd", postmix_fn, [_row(x), _row(mixed)], [g_pm, g_pl], [(D_MODEL, F32), (D_MODEL, BF16)])
    S["x2"], S["h2"] = x2, h2
    a1 = _matmul("mlp_in", h2, W["w_mlp_in"][l], "nn", out_dtype=BF16,
                 epilogue=lambda acc: jnp.square(jnp.maximum(acc, 0.0)))
    S["a1"] = a1
    f2 = _matmul("mlp_out", a1, W["w_mlp_out"][l], "nn")
    S["f2"] = f2
    return S


def _layer_bwd(S, dx3, W, P, l, bias4, onehot_t):
    t = dx3.shape[0]
    G = {}
    g_pmlp = P["post_mlp_norm"][l].reshape(1, D_MODEL)

    def b1_fn(f2, dx3, g):
        df2, dg = _rms_bwd(f2, g, dx3)
        return [df2], [dg]

    df2, G["post_mlp_norm"] = _rowwise("post_mlp_bwd", b1_fn, [_row(S["f2"]), _row(dx3)], [g_pmlp], [(D_MODEL, BF16)], [(1, D_MODEL)])
    df1 = _matmul("d_f1", df2, W["w_mlp_out"][l], "nt", out_dtype=BF16,
                  epilogue=lambda acc, a1: acc * (2.0 * jnp.sqrt(a1.astype(F32))), extras=[S["a1"]])
    G["w_mlp_out"] = _matmul("dw_mlp_out", S["a1"], df2, "tn")
    dh2 = _matmul("d_h2", df1, W["w_mlp_in"][l], "nt")
    G["w_mlp_in"] = _matmul("dw_mlp_in", S["h2"], df1, "tn")

    g_pm = P["post_mix_norm"][l].reshape(1, D_MODEL)
    g_pl = P["pre_mlp_norm"][l].reshape(1, D_MODEL)

    def b3_fn(x2, dh2, dx3, mixed, g_pl, g_pm):
        d1, dgl = _rms_bwd(x2, g_pl, dh2)
        dx2 = dx3 + d1
        dmixed, dgm = _rms_bwd(mixed, g_pm, dx2)
        return [dx2, dmixed], [dgl, dgm]

    dx2, dmixed, G["pre_mlp_norm"], G["post_mix_norm"] = _rowwise(
        "post_mix_bwd", b3_fn, [_row(S["x2"]), _row(dh2), _row(dx3), _row(S["mixed"])], [g_pl, g_pm],
        [(D_MODEL, F32), (D_MODEL, BF16)], [(1, D_MODEL), (1, D_MODEL)])
    dmix_in = _matmul("d_mix_in", dmixed, W["w_o"][l], "nt")
    G["w_o"] = _matmul("dw_o", S["mix_in"], dmixed, "tn")

    proj = S["proj"]

    def b4_fn(gates, ys, ya, dmix, b):
        g = jax.nn.sigmoid(gates + b)
        gs, ga = g[:, :D_MODEL], g[:, D_MODEL:]
        dg = jnp.concatenate([ys * dmix, ya * dmix], axis=-1) * g * (1.0 - g)
        return [gs * dmix, ga * dmix, dg], [jnp.sum(dg, axis=0, keepdims=True)]

    dy_ssd, dy_attn, dgates, G["b_gate"] = _rowwise(
        "merge_bwd", b4_fn, [_row(proj, 2 * D_MODEL, OFF_G // (2 * D_MODEL)), _row(S["y_ssd"]), _row(S["y_attn"]), _row(dmix_in)],
        [S["bg"]], [(D_MODEL, BF16), (D_MODEL, BF16), (2 * D_MODEL, BF16)], [(1, 2 * D_MODEL)])

    dyn = _matmul("d_yn", dy_ssd, W["w_ssd_out"][l], "nt")
    G["w_ssd_out"] = _matmul("dw_ssd_out", S["yn"], dy_ssd, "tn")
    do = _matmul("d_o", dy_attn, W["w_attn_out"][l], "nt", out_dtype=BF16)
    G["w_attn_out"] = _matmul("dw_attn_out", S["o"], dy_attn, "tn")

    do4 = _heads_major(do, ATTN_HEADS).reshape(ATTN_KV, REP, t, ATTN_DIM)
    dq4, dkp, dvp, dbias4, dsink4 = _attn_bwd(S["q4"], S["k3"], S["v3"], bias4, S["sink4"], do4)
    dq = _tokens_major(dq4.reshape(ATTN_HEADS, t, ATTN_DIM))
    dk = _tokens_major(_kv_combine("dk_combine", dkp))
    dv = _tokens_major(_kv_combine("dv_combine", dvp))
    G["attn_sink"] = dsink4[:, :, 0, 0].reshape(ATTN_HEADS)
    G["rel_bias_table"] = _bias_table_grad(dbias4.reshape(ATTN_HEADS, BLOCK * 3 * BLOCK), onehot_t).T

    act = S["act"]

    def b5_fn(y, z, xs, dyn, nw, dsk):
        dy, dz, dnw = _gated_norm_bwd(y, z, nw, dyn)
        return [dy, dz], [dnw, jnp.sum(dy * xs, axis=0, keepdims=True)]

    dy, dz, G["ssd_norm"], dskip_cols = _rowwise(
        "gated_norm_bwd", b5_fn, [_row(S["y"]), _row(proj, D_INNER, OFF_Z // D_INNER), _row(act, D_INNER, 0), _row(dyn)],
        [S["nw"], S["dsk"]], [(D_INNER, F32), (D_INNER, BF16)], [(1, D_INNER), (1, D_INNER)])
    G["d_skip"] = dskip_cols.reshape(SSD_HEADS, SSD_HEAD_DIM).sum(axis=-1)

    dxs2, dbs2, dcs2, ddt2, da2 = _ssd_bwd(act, S["dt2"], S["dt2t"], S["a_row"], S["a_col"], S["dte"], dy, S["states"])
    G["a_log"] = da2.reshape(2, SSD_HEADS) * S["a_row"].reshape(2, SSD_HEADS)

    def b6_fn(dxf, dxb, dy, dbf, dbb, dcf, dcb, pre, dsk):
        dact = jnp.concatenate([dxf + dxb + dy * dsk, dbf + dbb, dcf + dcb], axis=-1)
        return [dact * _silu_grad(pre)], []

    (dpre,) = _rowwise("silu_bwd", b6_fn,
                       [_row(dxs2, lead=0), _row(dxs2, lead=1), _row(dy), _row(dbs2, lead=0), _row(dbs2, lead=1),
                        _row(dcs2, lead=0), _row(dcs2, lead=1), _row(S["pre"])], [S["dsk"]], [(CONV_DIM, F32)], tb=128)
    du, dconv_w, dconv_b = _conv_bwd(dpre, proj, W["conv_w"][l])
    G["conv_w"] = dconv_w.reshape(SSD_CONV, 1, CONV_DIM)
    G["conv_b"] = dconv_b.reshape(CONV_DIM)

    dtb = jnp.pad(P["dt_bias"][l].reshape(1, 2 * SSD_HEADS), ((0, 0), (0, LANES - 2 * SSD_HEADS)))
    ddt = jnp.pad(jnp.concatenate([ddt2[0], ddt2[1]], axis=-1), ((0, 0), (0, LANES - 2 * SSD_HEADS)))

    def b7_fn(raw, ddt, b):
        draw = ddt * jax.nn.sigmoid(raw + b)
        return [draw], [jnp.sum(draw, axis=0, keepdims=True)]

    draw, ddtb = _rowwise("dt_bwd", b7_fn, [_row(proj, LANES, OFF_DT // LANES), _row(ddt)], [dtb], [(LANES, BF16)], [(1, LANES)], tb=512)
    G["dt_bias"] = ddtb[0, :2 * SSD_HEADS].reshape(2, SSD_HEADS)

    dproj = jnp.concatenate([dz, dgates, du, draw, dq, dk, dv], axis=-1)
    G["w_in"] = _from_proj_layout(_matmul("dw_in", S["h1"], dproj, "tn"))
    dh1 = _matmul("d_h1", dproj, W["w_in"][l], "nt")

    g_pre = P["pre_mix_norm"][l].reshape(1, D_MODEL)

    def b8_fn(x, dh1, dx2, g):
        d1, dg = _rms_bwd(x, g, dh1)
        return [dx2 + d1], [dg]

    dx, G["pre_mix_norm"] = _rowwise("pre_mix_bwd", b8_fn, [_row(S["x"]), _row(dh1), _row(dx2)], [g_pre], [(D_MODEL, F32)], [(1, D_MODEL)])
    return dx, G


def _step(x, target, shard_w, shard_m, shard_v):
    depth = shard_w["w_in"].shape[0]
    gathered = _allgather_chips("gather_weights", _pack_big(shard_w, BF16))
    conv_rows = shard_w["conv_w"].reshape(-1, D_MODEL)
    conv_g = _allgather_chips("gather_conv", jnp.pad(conv_rows, ((0, 16 - conv_rows.shape[0]), (0, 0))))
    per_chip = [_unpack_big(gathered[s], shard_w) for s in range(4)]
    W = {n: _full_of_shards(n, [per_chip[s][n] for s in range(4)]) for n in BIG if n != "conv_w"}
    W["w_in"] = _to_proj_layout(W["w_in"])
    W["conv_w"] = jnp.concatenate([conv_g[s][:conv_rows.shape[0]].reshape(shard_w["conv_w"].shape) for s in range(4)],
                                  axis=3).reshape(depth, SSD_CONV, CONV_DIM)
    P = {n: shard_w[n] for n in SMALL}
    loss_part, grad_x, full = _local_step(x, target, W, P)

    packs = [_pack_big({n: _shard_of_full(n, full[n], s) for n in BIG}, BF16) for s in range(4)]
    r2 = BIG_ROWS // 2
    pieces = jnp.stack([jnp.stack([p[h * r2:(h + 1) * r2] for p in packs]) for h in range(2)])
    theirs = _swap_cores("swap_grads", pieces).reshape(4 * r2, D_MODEL)
    mine = lax.dynamic_index_in_dim(pieces, lax.axis_index("c"), axis=0, keepdims=False).reshape(4 * r2, D_MODEL)
    (chip_sum,) = _rowwise("presum_grads", lambda a, b: ([a.astype(F32) + b.astype(F32)], []), [_row(mine), _row(theirs)], [],
                           [(D_MODEL, BF16)])
    staged = _scatter_chips("scatter_grads", chip_sum.reshape(4, r2, D_MODEL))
    half = _sum_slots("sum_grads", staged)
    g_big = _allgather_cores("share_grads", half).reshape(BIG_ROWS, D_MODEL)
    d_big, m_big, v_big = _adamw("adamw_big", _pack_big(shard_w, F32), g_big, _pack_big(shard_m, F32), _pack_big(shard_v, F32))

    small = _allgather_all("gather_small", _pack_small(full, loss_part))
    g_small = _sum_slots("sum_small", small, tb=SMALL_ROWS)
    d_small, m_small, v_small = _adamw("adamw_small", _pack_small(shard_w, jnp.zeros((), F32)), g_small,
                                       _pack_small(shard_m, jnp.zeros((), F32)), _pack_small(shard_v, jnp.zeros((), F32)))

    outs = {}
    for tag, big, sm in (("grad", g_big, g_small), ("delta", d_big, d_small), ("new_m", m_big, m_small), ("new_v", v_big, v_small)):
        ub = _unpack_big(big, shard_w)
        us, extra = _unpack_small(sm, shard_w)
        outs[tag] = {**ub, **us}
        if tag == "grad":
            loss = extra
    return loss, grad_x, outs


def _local_step(x, target, W, P):
    depth = W["w_in"].shape[0]
    W = dict(W, w_in_main=W["w_in"][:, :, :N_MAIN], w_in_qkv=W["w_in"][:, :, N_MAIN:])
    onehot_t = (_bucket_map().reshape(1, -1) == jnp.arange(N_BUCKETS)[:, None]).astype(F32)
    bias = _bias_from_table(P["rel_bias_table"], onehot_t).reshape(ATTN_HEADS, BLOCK, 3 * BLOCK)
    bias4 = jnp.where(_band_mask(), bias, NEG).reshape(ATTN_KV, REP, BLOCK, 3 * BLOCK)

    def pre_fn(x, g):
        return [_rms_fwd(x, g)], []

    (h1,) = _rowwise("pre_mix_fwd", pre_fn, [_row(x)], [P["pre_mix_norm"][0].reshape(1, D_MODEL)], [(D_MODEL, BF16)])
    saved = []
    loss_cols = dxl = None
    for l in range(depth):
        S = _layer_fwd(h1, x, W, P, l, bias4)
        saved.append(S)
        g_pmlp = P["post_mlp_norm"][l].reshape(1, D_MODEL)
        if l + 1 < depth:
            def post_fn(x2, f2, g1, g2):
                x3 = x2 + _rms_fwd(f2, g1)
                return [x3, _rms_fwd(x3, g2)], []

            x, h1 = _rowwise("post_mlp_fwd", post_fn, [_row(S["x2"]), _row(S["f2"])],
                             [g_pmlp, P["pre_mix_norm"][l + 1].reshape(1, D_MODEL)], [(D_MODEL, F32), (D_MODEL, BF16)])
        else:
            def loss_fn(x2, f2, tgt, g1):
                diff = x2 + _rms_fwd(f2, g1) - tgt
                return [diff * (1.0 / D_MODEL)], [jnp.sum(diff * diff, axis=0, keepdims=True)]

            dxl, loss_cols = _rowwise("loss", loss_fn, [_row(S["x2"]), _row(S["f2"]), _row(target)], [g_pmlp],
                                      [(D_MODEL, F32)], [(1, D_MODEL)])
    loss_part = 0.5 * jnp.sum(loss_cols) / D_MODEL

    grads = [None] * depth
    dx = dxl
    for l in reversed(range(depth)):
        dx, grads[l] = _layer_bwd(saved[l], dx, W, P, l, bias4, onehot_t)
    grad_x = dx

    full = {n: jnp.stack([grads[l][n] for l in range(depth)]) for n in ALL_W if n != "rel_bias_table"}
    full["rel_bias_table"] = sum(grads[l]["rel_bias_table"] for l in range(depth))
    return loss_part, grad_x, full


def kernel(x, pre_mix_norm, w_in, b_gate, conv_w, conv_b, dt_bias, a_log, d_skip, ssd_norm, w_ssd_out, attn_sink, rel_bias_table, w_attn_out, w_o, post_mix_norm, pre_mlp_norm, w_mlp_in, w_mlp_out, post_mlp_norm, loss_target, m_pre_mix_norm, m_w_in, m_b_gate, m_conv_w, m_conv_b, m_dt_bias, m_a_log, m_d_skip, m_ssd_norm, m_w_ssd_out, m_attn_sink, m_rel_bias_table, m_w_attn_out, m_w_o, m_post_mix_norm, m_pre_mlp_norm, m_w_mlp_in, m_w_mlp_out, m_post_mlp_norm, v_pre_mix_norm, v_w_in, v_b_gate, v_conv_w, v_conv_b, v_dt_bias, v_a_log, v_d_skip, v_ssd_norm, v_w_ssd_out, v_attn_sink, v_rel_bias_table, v_w_attn_out, v_w_o, v_post_mix_norm, v_pre_mlp_norm, v_w_mlp_in, v_w_mlp_out, v_post_mlp_norm):
    a = locals()
    shard_w = {n: a[n] for n in ALL_W}
    shard_m = {n: a["m_" + n] for n in ALL_W}
    shard_v = {n: a["v_" + n] for n in ALL_W}
    loss, grad_x, outs = _step(x[0], loss_target[0], shard_w, shard_m, shard_v)
    return (loss, grad_x[None], *[outs["grad"][n] for n in ALL_W], *[outs["delta"][n] for n in ALL_W],
            *[outs["new_m"][n] for n in ALL_W], *[outs["new_v"][n] for n in ALL_W])
```

```python
import math

import jax
import jax.numpy as jnp
from jax import lax
from jax.experimental import pallas as pl
from jax.experimental.pallas import tpu as pltpu

F32 = jnp.float32
BF16 = jnp.bfloat16
MESH = pl.DeviceIdType.MESH

VMEM_LIMIT_BYTES = 52 * 1024 * 1024
LANES = 128
SUBLANES = 8

EPS = 1e-6
D_MODEL = 1024
D_INNER = 2048
SSD_HEADS = 32
SSD_HEAD_DIM = 64
SSD_GROUPS = 8
SSD_STATE = 128
SSD_CONV = 5
CHUNK = 128
CONV_DIM = 4096
ATTN_HEADS = 16
ATTN_KV = 4
ATTN_DIM = 64
BLOCK = 128
N_BUCKETS = 32
MAX_DISTANCE = 128
D_FF = 4096
N_IN = 9792
NEG = -1e30

OFF_Z, OFF_G, OFF_XBC, OFF_DT, N_MAIN, N_PROJ = 0, 2048, 4096, 8192, 8320, 10368
N_QKV = N_PROJ - N_MAIN

ADAM_LR, ADAM_B1, ADAM_B2, ADAM_EPS, ADAM_WD, ADAM_STEP = 0.001, 0.9, 0.999, 1e-08, 0.01, 10

BIG_ROWS = 6656
SMALL_ROWS = 32


def _cparams(*sem):
    return pltpu.CompilerParams(dimension_semantics=sem, vmem_limit_bytes=VMEM_LIMIT_BYTES)


def _dot(a, b, precision=None):
    return lax.dot_general(a, b, (((1,), (0,)), ((), ())), preferred_element_type=F32, precision=precision)


def _dot_nt(a, b, precision=None):
    return lax.dot_general(a, b, (((1,), (1,)), ((), ())), preferred_element_type=F32, precision=precision)


def _dot_tn(a, b):
    return lax.dot_general(a, b, (((0,), (0,)), ((), ())), preferred_element_type=F32)


def _pick(n, prefs):
    for p in prefs:
        if n % p == 0:
            return p
    return n


def _matmul(name, a, b, mode, out_dtype=F32, epilogue=None, extras=()):
    if mode == "nn":
        (m, k), (_, n) = a.shape, b.shape
    elif mode == "nt":
        (m, k), (n, _) = a.shape, b.shape
    else:
        (k, m), (_, n) = a.shape, b.shape
    tm = _pick(m, (512, 256, 128)) if mode == "tn" else _pick(m, (1024, 512, 256, 128))
    tn = _pick(n, (1024, 1152, 640, 512, 256, 128))
    tk = _pick(k, (1024, 1152, 512, 256, 128)) if mode != "tn" else _pick(k, (512, 256, 128))
    nk = k // tk
    if mode == "nn":
        a_spec = pl.BlockSpec((tm, tk), lambda i, j, q: (i, q))
        b_spec = pl.BlockSpec((tk, tn), lambda i, j, q: (q, j))
        fn = _dot
    elif mode == "nt":
        a_spec = pl.BlockSpec((tm, tk), lambda i, j, q: (i, q))
        b_spec = pl.BlockSpec((tn, tk), lambda i, j, q: (j, q))
        fn = _dot_nt
    else:
        a_spec = pl.BlockSpec((tk, tm), lambda i, j, q: (q, i))
        b_spec = pl.BlockSpec((tk, tn), lambda i, j, q: (q, j))
        fn = _dot_tn

    tile = pl.BlockSpec((tm, tn), lambda i, j, q: (i, j))
    n_ex = len(extras)

    def body(a_ref, b_ref, *rest):
        ex_refs, o_ref = rest[:n_ex], rest[n_ex]

        def store(acc):
            v = acc if epilogue is None else epilogue(acc, *[r[...] for r in ex_refs])
            o_ref[...] = v.astype(o_ref.dtype)

        p = fn(a_ref[...].astype(BF16), b_ref[...].astype(BF16))
        if nk == 1:
            store(p)
        else:
            acc_ref = rest[n_ex + 1]
            q = pl.program_id(2)

            @pl.when(q == 0)
            def _():
                acc_ref[...] = p

            @pl.when((q > 0) & (q < nk - 1))
            def _():
                acc_ref[...] += p

            @pl.when(q == nk - 1)
            def _():
                store(acc_ref[...] + p)

    return pl.pallas_call(
        body, name=name, grid=(m // tm, n // tn, nk),
        in_specs=[a_spec, b_spec] + [tile] * n_ex, out_specs=tile,
        out_shape=jax.ShapeDtypeStruct((m, n), out_dtype),
        scratch_shapes=[pltpu.VMEM((tm, tn), F32)] if nk > 1 else [],
        compiler_params=_cparams("parallel", "parallel", "arbitrary"),
    )(a, b, *extras)


def _row(arr, width=None, cb=0, lead=None):
    return (arr, width, cb, lead)


def _rowwise(name, fn, rows, vecs, outs, accs=(), tb=256):
    t = rows[0][0].shape[-2]
    tb = min(tb, t)
    in_specs, args = [], []
    for arr, width, cb, lead in rows:
        w = arr.shape[-1] if width is None else width
        if lead is None:
            in_specs.append(pl.BlockSpec((tb, w), lambda i, cb=cb: (i, cb)))
        else:
            in_specs.append(pl.BlockSpec((None, tb, w), lambda i, cb=cb, lead=lead: (lead, i, cb)))
        args.append(arr)
    for v in vecs:
        in_specs.append(pl.BlockSpec(v.shape, lambda i, nd=v.ndim: (0,) * nd))
        args.append(v)
    out_shape = [jax.ShapeDtypeStruct((t, c), dt) for c, dt in outs] + [jax.ShapeDtypeStruct(s, F32) for s in accs]
    out_specs = [pl.BlockSpec((tb, c), lambda i: (i, 0)) for c, _ in outs] + [pl.BlockSpec(s, lambda i: (0, 0)) for s in accs]
    n_in, n_out = len(args), len(outs)

    def body(*refs):
        vals = [r[...] for r in refs[:n_in]]
        o_vals, a_vals = fn(*vals)
        for r, v in zip(refs[n_in:n_in + n_out], o_vals):
            r[...] = v.astype(r.dtype)
        first = pl.program_id(0) == 0
        for r, v in zip(refs[n_in + n_out:], a_vals):
            @pl.when(first)
            def _(r=r, v=v):
                r[...] = v

            @pl.when(jnp.logical_not(first))
            def _(r=r, v=v):
                r[...] += v

    res = pl.pallas_call(
        body, name=name, grid=(t // tb,), in_specs=in_specs, out_specs=out_specs, out_shape=out_shape,
        compiler_params=_cparams("arbitrary"),
    )(*args)
    return res


def _rms_fwd(x, g):
    r = lax.rsqrt(jnp.mean(x * x, axis=-1, keepdims=True) + EPS)
    return x * r * g


def _rms_bwd(x, g, dy):
    r = lax.rsqrt(jnp.mean(x * x, axis=-1, keepdims=True) + EPS)
    xh = x * r
    dxh = dy * g
    dx = r * (dxh - xh * jnp.mean(dxh * xh, axis=-1, keepdims=True))
    return dx, jnp.sum(dy * xh, axis=0, keepdims=True)


def _silu(x):
    return x * jax.nn.sigmoid(x)


def _silu_grad(x):
    s = jax.nn.sigmoid(x)
    return s * (1.0 + x * (1.0 - s))


GROUP_W = D_INNER // SSD_GROUPS


def _gated_norm_fwd(y, z, w):
    u = y * _silu(z)
    parts = []
    for j in range(SSD_GROUPS):
        ug = u[:, j * GROUP_W:(j + 1) * GROUP_W]
        parts.append(ug * lax.rsqrt(jnp.mean(ug * ug, axis=-1, keepdims=True) + EPS))
    return jnp.concatenate(parts, axis=-1) * w


def _gated_norm_bwd(y, z, w, dyn):
    sz = _silu(z)
    u = y * sz
    duh = dyn * w
    du_parts, uh_parts = [], []
    for j in range(SSD_GROUPS):
        sl = slice(j * GROUP_W, (j + 1) * GROUP_W)
        ug = u[:, sl]
        r = lax.rsqrt(jnp.mean(ug * ug, axis=-1, keepdims=True) + EPS)
        uh = ug * r
        dg = duh[:, sl]
        du_parts.append(r * (dg - uh * jnp.mean(dg * uh, axis=-1, keepdims=True)))
        uh_parts.append(uh)
    du = jnp.concatenate(du_parts, axis=-1)
    uh = jnp.concatenate(uh_parts, axis=-1)
    dw = jnp.sum(dyn * uh, axis=0, keepdims=True)
    return du * sz, du * y * _silu_grad(z), dw


HALO = SUBLANES


def _halo_specs(tb, cb, col0, t):
    nblk8 = t // HALO
    per = tb // HALO
    main = pl.BlockSpec((tb, cb), lambda j, i: (i, col0 + j))
    prev = pl.BlockSpec((HALO, cb), lambda j, i: (jnp.maximum(i * per - 1, 0), col0 + j))
    nxt = pl.BlockSpec((HALO, cb), lambda j, i: (jnp.minimum((i + 1) * per, nblk8 - 1), col0 + j))
    return main, prev, nxt


def _fill_ext(ext_ref, cur_ref, prev_ref, next_ref, tb, ni):
    i = pl.program_id(1)
    ext_ref[0:HALO, :] = jnp.where(i > 0, prev_ref[...], 0.0)
    ext_ref[HALO:HALO + tb, :] = cur_ref[...]
    ext_ref[HALO + tb:HALO + tb + HALO, :] = jnp.where(i < ni - 1, next_ref[...], 0.0)


def _conv_fwd(proj, w, b):
    t = proj.shape[0]
    tb, cb = min(512, t), 512
    ni, nj = t // tb, CONV_DIM // cb
    main, prev, nxt = _halo_specs(tb, cb, OFF_XBC // cb, t)
    pad = (SSD_CONV - 1) // 2

    def body(u_ref, up_ref, un_ref, w_ref, b_ref, pre_ref, act_ref, ext_ref):
        _fill_ext(ext_ref, u_ref, up_ref, un_ref, tb, ni)
        acc = jnp.broadcast_to(b_ref[...], (tb, cb))
        for k in range(SSD_CONV):
            acc = acc + w_ref[k:k + 1, :] * ext_ref[pl.ds(HALO + k - pad, tb), :]
        pre_ref[...] = acc
        act_ref[...] = _silu(acc)

    out = pl.BlockSpec((tb, cb), lambda j, i: (i, j))
    return pl.pallas_call(
        body, name="conv_fwd", grid=(nj, ni),
        in_specs=[main, prev, nxt, pl.BlockSpec((SSD_CONV, cb), lambda j, i: (0, j)), pl.BlockSpec((1, cb), lambda j, i: (0, j))],
        out_specs=[out, out],
        out_shape=[jax.ShapeDtypeStruct((t, CONV_DIM), F32)] * 2,
        scratch_shapes=[pltpu.VMEM((tb + 2 * HALO, cb), F32)],
        compiler_params=_cparams("parallel", "arbitrary"),
    )(proj, proj, proj, w, b)


def _conv_bwd(dpre, proj, w):
    t = proj.shape[0]
    tb, cb = min(512, t), 512
    ni, nj = t // tb, CONV_DIM // cb
    umain, uprev, unext = _halo_specs(tb, cb, OFF_XBC // cb, t)
    dmain, dprev, dnext = _halo_specs(tb, cb, 0, t)
    pad = (SSD_CONV - 1) // 2

    def body(d_ref, dp_ref, dn_ref, u_ref, up_ref, un_ref, w_ref, du_ref, dw_ref, db_ref, extd_ref, extu_ref):
        _fill_ext(extd_ref, d_ref, dp_ref, dn_ref, tb, ni)
        _fill_ext(extu_ref, u_ref, up_ref, un_ref, tb, ni)
        d = d_ref[...]
        du = jnp.zeros((tb, cb), F32)
        @pl.when(pl.program_id(1) == 0)
        def _():
            dw_ref[...] = jnp.zeros_like(dw_ref)
            db_ref[...] = jnp.zeros_like(db_ref)

        for k in range(SSD_CONV):
            du = du + w_ref[k:k + 1, :] * extd_ref[pl.ds(HALO - k + pad, tb), :]
            dw_ref[k:k + 1, :] += jnp.sum(d * extu_ref[pl.ds(HALO + k - pad, tb), :], axis=0, keepdims=True)
        du_ref[...] = du.astype(du_ref.dtype)
        db_ref[...] += jnp.sum(d, axis=0, keepdims=True)

    return pl.pallas_call(
        body, name="conv_bwd", grid=(nj, ni),
        in_specs=[dmain, dprev, dnext, umain, uprev, unext, pl.BlockSpec((SSD_CONV, cb), lambda j, i: (0, j))],
        out_specs=[pl.BlockSpec((tb, cb), lambda j, i: (i, j)), pl.BlockSpec((SSD_CONV, cb), lambda j, i: (0, j)),
                   pl.BlockSpec((1, cb), lambda j, i: (0, j))],
        out_shape=[jax.ShapeDtypeStruct((t, CONV_DIM), BF16), jax.ShapeDtypeStruct((SSD_CONV, CONV_DIM), F32),
                   jax.ShapeDtypeStruct((1, CONV_DIM), F32)],
        scratch_shapes=[pltpu.VMEM((tb + 2 * HALO, cb), F32)] * 2,
        compiler_params=_cparams("parallel", "arbitrary"),
    )(dpre, dpre, dpre, proj, proj, proj, w)


PAIR = 2 * SSD_HEAD_DIM
HI = lax.Precision.HIGHEST


def _ssd_prelude(d, dt_ref, dtt_ref, ar_ref, ac_ref):
    li = lax.broadcasted_iota(jnp.int32, (CHUNK, CHUNK), 0)
    si = lax.broadcasted_iota(jnp.int32, (CHUNK, CHUNK), 1)
    fwd = d == 0
    hi, lo = jnp.where(fwd, li, si), jnp.where(fwd, si, li)
    tri = hi >= lo
    trif = tri.astype(F32)
    trit = (hi <= lo).astype(F32)
    dt = dt_ref[...]
    adt = dt * ar_ref[...]
    adtt = dtt_ref[...] * ac_ref[...]
    p = _dot(trif, adt, HI)
    pt = _dot_nt(adtt, trif, HI)
    tot = jnp.sum(adt, axis=0, keepdims=True)
    return tri, trit, dt, p, pt, tot


def _ssd_specs(nc, rev):
    def cidx(d, c):
        up = (d == 1) if rev else (d == 0)
        return jnp.where(up, c, nc - 1 - c)

    specs = [
        pl.BlockSpec((CHUNK, D_INNER), lambda d, c: (cidx(d, c), 0)),
        pl.BlockSpec((CHUNK, 1024), lambda d, c: (cidx(d, c), 2)),
        pl.BlockSpec((CHUNK, 1024), lambda d, c: (cidx(d, c), 3)),
        pl.BlockSpec((None, CHUNK, SSD_HEADS), lambda d, c: (d, cidx(d, c), 0)),
        pl.BlockSpec((None, SSD_HEADS, CHUNK), lambda d, c: (d, 0, cidx(d, c))),
        pl.BlockSpec((None, 1, SSD_HEADS), lambda d, c: (d, 0, 0)),
        pl.BlockSpec((None, SSD_HEADS, 1), lambda d, c: (d, 0, 0)),
        pl.BlockSpec((None, CHUNK, D_INNER), lambda d, c: (d, cidx(d, c), 0)),
    ]
    return cidx, specs


def _head_decay(tri, p, pt, tot, h):
    pb = jnp.broadcast_to(p[:, h:h + 1], (CHUNK, CHUNK))
    dec = jnp.exp(jnp.where(tri, pb - pt[h:h + 1, :], NEG))
    return dec, jnp.exp(tot[:, h:h + 1] - pb), jnp.exp(pb)


def _ssd_fwd(act, dt2, dt2t, a_row, a_col, dte):
    t = act.shape[0]
    nc = t // CHUNK
    cidx, specs = _ssd_specs(nc, rev=False)

    def body(xs_ref, bs_ref, cs_ref, dt_ref, dtt_ref, ar_ref, ac_ref, dte_ref, y_ref, st_ref, h_ref):
        d, c = pl.program_id(0), pl.program_id(1)

        @pl.when(c == 0)
        def _():
            h_ref[...] = jnp.zeros_like(h_ref)

        st_ref[...] = h_ref[...]
        tri, _, _, p, pt, tot = _ssd_prelude(d, dt_ref, dtt_ref, ar_ref, ac_ref)
        etot = jnp.exp(tot)
        lane = lax.broadcasted_iota(jnp.int32, (CHUNK, PAIR), 1) < SSD_HEAD_DIM
        rowh = lax.broadcasted_iota(jnp.int32, (PAIR, SSD_STATE), 0) < SSD_HEAD_DIM
        for g in range(SSD_GROUPS):
            gs = slice(g * SSD_STATE, (g + 1) * SSD_STATE)
            bg = bs_ref[:, gs]
            cb = cs_ref[:, gs].astype(BF16)
            cbm = _dot_nt(cb, bg.astype(BF16))
            for pr in range(2):
                h0 = g * 4 + pr * 2
                h1 = h0 + 1
                sl = slice(h0 * SSD_HEAD_DIM, h0 * SSD_HEAD_DIM + PAIR)
                xdt = (xs_ref[:, sl] * dte_ref[:, sl]).astype(BF16)
                yd, st, epb = [], [], []
                for h in (h0, h1):
                    dec, wb, eb = _head_decay(tri, p, pt, tot, h)
                    yd.append(_dot((cbm * dec).astype(BF16), xdt))
                    st.append(_dot_tn(xdt, (bg * wb).astype(BF16)))
                    epb.append(eb)
                hin = h_ref[sl, :]
                yo = _dot_nt(cb, hin.astype(BF16)) * jnp.where(lane, epb[0], epb[1])
                y_ref[:, sl] = jnp.where(lane, yd[0], yd[1]) + yo
                et = jnp.where(rowh, etot[:, h0:h0 + 1], etot[:, h1:h1 + 1])
                h_ref[sl, :] = hin * et + jnp.where(rowh, st[0], st[1])

    return pl.pallas_call(
        body, name="ssd_fwd", grid=(2, nc), in_specs=specs,
        out_specs=[pl.BlockSpec((None, CHUNK, D_INNER), lambda d, c: (d, cidx(d, c), 0)),
                   pl.BlockSpec((None, None, D_INNER, SSD_STATE), lambda d, c: (d, cidx(d, c), 0, 0))],
        out_shape=[jax.ShapeDtypeStruct((2, t, D_INNER), F32), jax.ShapeDtypeStruct((2, nc, D_INNER, SSD_STATE), F32)],
        scratch_shapes=[pltpu.VMEM((D_INNER, SSD_STATE), F32)],
        compiler_params=_cparams("arbitrary", "arbitrary"),
    )(act, act, act, dt2, dt2t, a_row, a_col, dte)


def _put_col(acc, col, h):
    lane = lax.broadcasted_iota(jnp.int32, acc.shape, 1)
    return jnp.where(lane == h, col, acc)


def _put_row(acc, row, h):
    sub = lax.broadcasted_iota(jnp.int32, acc.shape, 0)
    return jnp.where(sub == h, row, acc)


def _sum_all(x):
    return jnp.sum(jnp.sum(x, axis=0, keepdims=True), axis=1, keepdims=True)


def _ssd_bwd(act, dt2, dt2t, a_row, a_col, dte, dy, states):
    t = act.shape[0]
    nc = t // CHUNK
    cidx, specs = _ssd_specs(nc, rev=True)
    specs = specs + [
        pl.BlockSpec((CHUNK, D_INNER), lambda d, c: (cidx(d, c), 0)),
        pl.BlockSpec((None, None, D_INNER, SSD_STATE), lambda d, c: (d, cidx(d, c), 0, 0)),
    ]

    def body(xs_ref, bs_ref, cs_ref, dt_ref, dtt_ref, ar_ref, ac_ref, dte_ref, dy_ref, st_ref,
             dxs_ref, dbs_ref, dcs_ref, ddt_ref, da_ref, dh_ref):
        d, c = pl.program_id(0), pl.program_id(1)

        @pl.when(c == 0)
        def _():
            dh_ref[...] = jnp.zeros_like(dh_ref)
            da_ref[...] = jnp.zeros_like(da_ref)

        tri, trit, dt, p, pt, tot = _ssd_prelude(d, dt_ref, dtt_ref, ar_ref, ac_ref)
        etot = jnp.exp(tot)
        lane = lax.broadcasted_iota(jnp.int32, (CHUNK, PAIR), 1) < SSD_HEAD_DIM
        rowh = lax.broadcasted_iota(jnp.int32, (PAIR, SSD_STATE), 0) < SSD_HEAD_DIM
        first_head = lax.broadcasted_iota(jnp.int32, (PAIR, LANES), 0) < SSD_HEAD_DIM
        out_lane = lax.broadcasted_iota(jnp.int32, (PAIR, LANES), 1)
        ddtx = jnp.zeros((CHUNK, LANES), F32)
        lane32 = lax.broadcasted_iota(jnp.int32, (CHUNK, SSD_HEADS), 1)
        dp_col = jnp.zeros((CHUNK, SSD_HEADS), F32)
        dp_row = jnp.zeros((SSD_HEADS, CHUNK), F32)
        dtot = jnp.zeros((1, SSD_HEADS), F32)
        for g in range(SSD_GROUPS):
            gs = slice(g * SSD_STATE, (g + 1) * SSD_STATE)
            bg = bs_ref[:, gs]
            bb = bg.astype(BF16)
            cb = cs_ref[:, gs].astype(BF16)
            cbm = _dot_nt(cb, bb)
            dcb = jnp.zeros((CHUNK, CHUNK), F32)
            dc_acc = jnp.zeros((CHUNK, SSD_STATE), F32)
            db_acc = jnp.zeros((CHUNK, SSD_STATE), F32)
            for pr in range(2):
                h0 = g * 4 + pr * 2
                h1 = h0 + 1
                sl = slice(h0 * SSD_HEAD_DIM, h0 * SSD_HEAD_DIM + PAIR)
                xp = xs_ref[:, sl]
                dtp = dte_ref[:, sl]
                xdt_f = xp * dtp
                xdt = xdt_f.astype(BF16)
                dyp = dy_ref[:, sl]
                dyb = dyp.astype(BF16)
                hin = st_ref[sl, :]
                dh = dh_ref[sl, :]
                hb = hin.astype(BF16)
                dhb = dh.astype(BF16)
                heads = [_head_decay(tri, p, pt, tot, h) for h in (h0, h1)]
                dye = dyp * jnp.where(lane, heads[0][2], heads[1][2])
                dyeb = dye.astype(BF16)
                gy = _dot_nt(cb, hb) * dye
                dc_acc = dc_acc + _dot(dyeb, hb)
                dhin = _dot_tn(dyeb, cb)
                hh = dh * hin
                dxdt = jnp.zeros((CHUNK, PAIR), F32)
                for idx, h in enumerate((h0, h1)):
                    hm = lane if idx == 0 else jnp.logical_not(lane)
                    rm = rowh if idx == 0 else jnp.logical_not(rowh)
                    dec, wb, _ = heads[idx]
                    mf = cbm * dec
                    t1 = _dot_tn(mf.astype(BF16), dyb)
                    t2 = _dot_nt((bg * wb).astype(BF16), dhb)
                    dxdt = jnp.where(hm, t1 + t2, dxdt)
                    dm = _dot_nt(jnp.where(hm, dyp, 0.0).astype(BF16), xdt)
                    dcb = dcb + dm * dec
                    e = dm * mf
                    qw = _dot(jnp.where(hm, xdt_f, 0.0).astype(BF16), dhb) * wb
                    db_acc = db_acc + qw
                    qwb = qw * bg
                    col = jnp.sum(e + jnp.where(hm, gy, 0.0) - qwb, axis=1, keepdims=True)
                    dp_col = jnp.where(lane32 == h, col, dp_col)
                    dp_row = _put_row(dp_row, -jnp.sum(e, axis=0, keepdims=True), h)
                    dtot_h = _sum_all(qwb) + etot[:, h:h + 1] * _sum_all(jnp.where(rm, hh, 0.0))
                    dtot = _put_col(dtot, dtot_h, h)
                dxs_ref[:, sl] = dxdt * dtp
                ddx = dxdt * xp
                ddx_hi = ddx.astype(BF16)
                ddx_lo = (ddx - ddx_hi.astype(F32)).astype(BF16)
                route = (out_lane == jnp.where(first_head, h0, h1)).astype(BF16)
                ddtx = ddtx + _dot(ddx_hi, route) + _dot(ddx_lo, route)
                et = jnp.where(rowh, etot[:, h0:h0 + 1], etot[:, h1:h1 + 1])
                dh_ref[sl, :] = dh * et + dhin
            dcbb = dcb.astype(BF16)
            dcs_ref[:, gs] = _dot(dcbb, bb) + dc_acc
            dbs_ref[:, gs] = _dot_tn(dcbb, cb) + db_acc
        d_adt = _dot(trit, dp_col, HI) + _dot_nt(trit, dp_row, HI) + dtot
        ddt_ref[...] = ddtx[:, :SSD_HEADS] + ar_ref[...] * d_adt
        da_ref[...] += jnp.sum(dt * d_adt, axis=0, keepdims=True)

    return pl.pallas_call(
        body, name="ssd_bwd", grid=(2, nc), in_specs=specs,
        out_specs=[pl.BlockSpec((None, CHUNK, D_INNER), lambda d, c: (d, cidx(d, c), 0)),
                   pl.BlockSpec((None, CHUNK, 1024), lambda d, c: (d, cidx(d, c), 0)),
                   pl.BlockSpec((None, CHUNK, 1024), lambda d, c: (d, cidx(d, c), 0)),
                   pl.BlockSpec((None, CHUNK, SSD_HEADS), lambda d, c: (d, cidx(d, c), 0)),
                   pl.BlockSpec((None, 1, SSD_HEADS), lambda d, c: (d, 0, 0))],
        out_shape=[jax.ShapeDtypeStruct((2, t, D_INNER), F32), jax.ShapeDtypeStruct((2, t, 1024), F32),
                   jax.ShapeDtypeStruct((2, t, 1024), F32), jax.ShapeDtypeStruct((2, t, SSD_HEADS), F32),
                   jax.ShapeDtypeStruct((2, 1, SSD_HEADS), F32)],
        scratch_shapes=[pltpu.VMEM((D_INNER, SSD_STATE), F32)],
        compiler_params=_cparams("arbitrary", "arbitrary"),
    )(act, act, act, dt2, dt2t, a_row, a_col, dte, dy, states)


REP = ATTN_HEADS // ATTN_KV
SCALE = ATTN_DIM ** -0.5
GROUP_Q = REP * ATTN_DIM
K_BLK0 = D_MODEL // LANES
V_BLK0 = K_BLK0 + ATTN_KV


def _attn_specs(nb):
    q = pl.BlockSpec((BLOCK, GROUP_Q), lambda g, n: (n, g))

    def kv(blk0):
        return [pl.BlockSpec((BLOCK, LANES), lambda g, n: (jnp.maximum(n - 1, 0), blk0 + g)),
                pl.BlockSpec((BLOCK, LANES), lambda g, n: (n, blk0 + g)),
                pl.BlockSpec((BLOCK, LANES), lambda g, n: (jnp.minimum(n + 1, nb - 1), blk0 + g))]

    bias = pl.BlockSpec((None, REP, BLOCK, 3 * BLOCK), lambda g, n: (g, 0, 0, 0))
    sink = pl.BlockSpec((None, REP, 1, LANES), lambda g, n: (g, 0, 0, 0))
    return q, kv(K_BLK0), kv(V_BLK0), bias, sink


def _band_mask():
    ii = lax.broadcasted_iota(jnp.int32, (BLOCK, 3 * BLOCK), 0)
    jj = lax.broadcasted_iota(jnp.int32, (BLOCK, 3 * BLOCK), 1)
    return (jj >= ii) & (jj - 2 * BLOCK <= ii)


def _attn_valid(n, nb):
    jj = lax.broadcasted_iota(jnp.int32, (1, 3 * BLOCK), 1)
    return ((jj >= BLOCK) | (n > 0)) & ((jj < 2 * BLOCK) | (n < nb - 1))


def _attn_probs(q, kcat, bias, snk, valid):
    s = jnp.where(valid, _dot_nt(q, kcat) + bias, NEG)
    m = jnp.maximum(jnp.max(s, axis=1, keepdims=True), snk)
    p = jnp.exp(s - m)
    es = jnp.exp(snk - m)
    r = 1.0 / (jnp.sum(p, axis=1, keepdims=True) + es)
    return p * r, es * r


def _attn_fwd(qkv, bias4, sink4):
    t = qkv.shape[0]
    nb = t // BLOCK
    qs, ks, vs, bs, ss = _attn_specs(nb)

    def body(q_ref, kp_ref, kc_ref, kn_ref, vp_ref, vc_ref, vn_ref, b_ref, s_ref, o_ref):
        n = pl.program_id(1)
        kcat = jnp.concatenate([kp_ref[...], kc_ref[...], kn_ref[...]], axis=0)
        vcat = jnp.concatenate([vp_ref[...], vc_ref[...], vn_ref[...]], axis=0)
        valid = _attn_valid(n, nb)
        lane = lax.broadcasted_iota(jnp.int32, (BLOCK, LANES), 1) < ATTN_DIM
        for pr in range(REP // 2):
            qp = q_ref[:, pr * LANES:(pr + 1) * LANES] * SCALE
            outs = []
            for idx in range(2):
                r = 2 * pr + idx
                qm = jnp.where(lane if idx == 0 else jnp.logical_not(lane), qp, 0.0)
                pn, _ = _attn_probs(qm, kcat, b_ref[r], s_ref[r][:, 0:1], valid)
                outs.append(_dot(pn.astype(BF16), vcat))
            o_ref[:, pr * LANES:(pr + 1) * LANES] = jnp.where(lane, outs[0], outs[1]).astype(o_ref.dtype)

    return pl.pallas_call(
        body, name="attn_fwd", grid=(ATTN_KV, nb), in_specs=[qs] + ks + vs + [bs, ss],
        out_specs=qs, out_shape=jax.ShapeDtypeStruct((t, D_MODEL), BF16),
        compiler_params=_cparams("parallel", "arbitrary"),
    )(qkv, qkv, qkv, qkv, qkv, qkv, qkv, bias4, sink4)


def _attn_bwd(qkv, bias4, sink4, do):
    t = qkv.shape[0]
    nb = t // BLOCK
    qs, ks, vs, bs, ss = _attn_specs(nb)
    part = pl.BlockSpec((3, BLOCK, LANES), lambda g, n: (0, n, g))

    def body(q_ref, kp_ref, kc_ref, kn_ref, vp_ref, vc_ref, vn_ref, b_ref, s_ref, do_ref,
             dq_ref, dk_ref, dv_ref, db_ref, ds_ref):
        n = pl.program_id(1)

        @pl.when(n == 0)
        def _():
            db_ref[...] = jnp.zeros_like(db_ref)
            ds_ref[...] = jnp.zeros_like(ds_ref)

        kcat = jnp.concatenate([kp_ref[...], kc_ref[...], kn_ref[...]], axis=0)
        vcat = jnp.concatenate([vp_ref[...], vc_ref[...], vn_ref[...]], axis=0)
        valid = _attn_valid(n, nb)
        lane = lax.broadcasted_iota(jnp.int32, (BLOCK, LANES), 1) < ATTN_DIM
        dk = jnp.zeros((3 * BLOCK, LANES), F32)
        dv = jnp.zeros((3 * BLOCK, LANES), F32)
        for pr in range(REP // 2):
            qp = q_ref[:, pr * LANES:(pr + 1) * LANES]
            dop = do_ref[:, pr * LANES:(pr + 1) * LANES]
            dqs = []
            for idx in range(2):
                r = 2 * pr + idx
                hm = lane if idx == 0 else jnp.logical_not(lane)
                qm = jnp.where(hm, qp, 0.0)
                dom = jnp.where(hm, dop, 0.0)
                pn, psink = _attn_probs(qm * SCALE, kcat, b_ref[r], s_ref[r][:, 0:1], valid)
                dp = _dot_nt(dom, vcat)
                delta = jnp.sum(pn * dp, axis=1, keepdims=True)
                dsc = pn * (dp - delta)
                db_ref[r] += dsc
                ds_ref[r] += jnp.broadcast_to(-jnp.sum(psink * delta, axis=0, keepdims=True), (1, LANES))
                dsb = (dsc * SCALE).astype(BF16)
                dqs.append(_dot(dsb, kcat))
                dk = dk + _dot_tn(dsb, qm)
                dv = dv + _dot_tn(pn.astype(BF16), dom)
            dq_ref[:, pr * LANES:(pr + 1) * LANES] = jnp.where(lane, dqs[0], dqs[1]).astype(dq_ref.dtype)
        for j in range(3):
            dk_ref[j] = dk[j * BLOCK:(j + 1) * BLOCK]
            dv_ref[j] = dv[j * BLOCK:(j + 1) * BLOCK]

    kv_cols = ATTN_KV * LANES
    return pl.pallas_call(
        body, name="attn_bwd", grid=(ATTN_KV, nb), in_specs=[qs] + ks + vs + [bs, ss, qs],
        out_specs=[qs, part, part, bs, ss],
        out_shape=[jax.ShapeDtypeStruct((t, D_MODEL), BF16), jax.ShapeDtypeStruct((3, t, kv_cols), F32),
                   jax.ShapeDtypeStruct((3, t, kv_cols), F32), jax.ShapeDtypeStruct(bias4.shape, F32),
                   jax.ShapeDtypeStruct(sink4.shape, F32)],
        compiler_params=_cparams("parallel", "arbitrary"),
    )(qkv, qkv, qkv, qkv, qkv, qkv, qkv, bias4, sink4, do)


def _kv_combine(name, parts):
    _, t, cols = parts.shape

    def body(p_ref, o_ref):
        z = jnp.zeros((BLOCK, LANES), F32)
        from_next = jnp.concatenate([p_ref[0, BLOCK:, :], z], axis=0)
        from_prev = jnp.concatenate([z, p_ref[2, :t - BLOCK, :]], axis=0)
        o_ref[...] = (from_next + p_ref[1] + from_prev).astype(o_ref.dtype)

    return pl.pallas_call(
        body, name=name, grid=(cols // LANES,),
        in_specs=[pl.BlockSpec((3, t, LANES), lambda g: (0, 0, g))],
        out_specs=pl.BlockSpec((t, LANES), lambda g: (0, g)),
        out_shape=jax.ShapeDtypeStruct((t, cols), BF16),
        compiler_params=_cparams("parallel"),
    )(parts)


def _t5_bucket(rel):
    nb = N_BUCKETS // 2
    max_exact = nb // 2
    ret = jnp.where(rel > 0, nb, 0)
    n = jnp.abs(rel)
    nf = jnp.maximum(n, 1).astype(jnp.float32)
    large = max_exact + (jnp.log(nf / max_exact) / math.log(MAX_DISTANCE / max_exact) * (nb - max_exact)).astype(jnp.int32)
    large = jnp.minimum(large, nb - 1)
    return ret + jnp.where(n < max_exact, n, large)


def _bucket_map():
    i = jnp.arange(BLOCK)[:, None]
    j = jnp.arange(3 * BLOCK)[None, :]
    return _t5_bucket(j - BLOCK - i)


def _bias_from_table(table, onehot_t):
    def body(t_ref, o_ref, out_ref):
        out_ref[...] = _dot(t_ref[...], o_ref[...], HI)

    return pl.pallas_call(body, name="bias_from_table", out_shape=jax.ShapeDtypeStruct((ATTN_HEADS, onehot_t.shape[1]), F32),
                          compiler_params=pltpu.CompilerParams(vmem_limit_bytes=VMEM_LIMIT_BYTES))(table.T, onehot_t)


def _bias_table_grad(dbias, onehot_t):
    def body(d_ref, o_ref, out_ref):
        out_ref[...] = _dot_nt(d_ref[...], o_ref[...], HI)

    return pl.pallas_call(body, name="bias_table_grad", out_shape=jax.ShapeDtypeStruct((ATTN_HEADS, N_BUCKETS), F32),
                          compiler_params=pltpu.CompilerParams(vmem_limit_bytes=VMEM_LIMIT_BYTES))(dbias, onehot_t)


HBM_SPEC = pl.BlockSpec(memory_space=pl.ANY)


def _allgather_chips(name, x):
    r2 = x.shape[0] // 2

    def body(x_ref, out_ref, send_sems, recv_sems, local_sem):
        mx, my, mc = lax.axis_index("x"), lax.axis_index("y"), lax.axis_index("c")
        me = 2 * mx + my
        chips = [(1 - mx, my), (mx, 1 - my), (1 - mx, 1 - my)]
        sibling = (mx, my, 1 - mc)

        def part(slot, h):
            return out_ref.at[slot, pl.ds(h * r2, r2)]

        def copy(k, src, dst, to):
            return pltpu.make_async_remote_copy(src_ref=src, dst_ref=dst, send_sem=send_sems.at[k], recv_sem=recv_sems.at[k],
                                                device_id=to, device_id_type=MESH)

        mine = pltpu.make_async_copy(x_ref, out_ref.at[me], local_sem)
        mine.start()
        first = [copy(k, x_ref.at[pl.ds(mc * r2, r2)], part(me, mc), (px, py, mc)) for k, (px, py) in enumerate(chips)]
        for cp in first:
            cp.start()
        passed = [copy(3 + k, part(2 * px + py, mc), part(2 * px + py, mc), sibling) for k, (px, py) in enumerate(chips)]
        for k, (px, py) in enumerate(chips):
            copy(k, part(2 * px + py, mc), part(2 * px + py, mc), (px, py, mc)).wait_recv()
            passed[k].start()
        for k, (px, py) in enumerate(chips):
            copy(3 + k, part(2 * px + py, 1 - mc), part(2 * px + py, 1 - mc), sibling).wait_recv()
        for cp in first + passed:
            cp.wait_send()
        mine.wait()

    return pl.pallas_call(
        body, name=name, in_specs=[HBM_SPEC], out_specs=HBM_SPEC,
        out_shape=jax.ShapeDtypeStruct((4,) + x.shape, x.dtype),
        scratch_shapes=[pltpu.SemaphoreType.DMA((6,)), pltpu.SemaphoreType.DMA((6,)), pltpu.SemaphoreType.DMA],
    )(x)


def _swap_cores(name, x):
    def body(x_ref, out_ref, send_sem, recv_sem):
        mx, my, mc = lax.axis_index("x"), lax.axis_index("y"), lax.axis_index("c")
        send = pltpu.make_async_remote_copy(src_ref=x_ref.at[1 - mc], dst_ref=out_ref, send_sem=send_sem, recv_sem=recv_sem,
                                            device_id=(mx, my, 1 - mc), device_id_type=MESH)
        send.start()
        send.wait()

    return pl.pallas_call(
        body, name=name, in_specs=[HBM_SPEC], out_specs=HBM_SPEC,
        out_shape=jax.ShapeDtypeStruct(x.shape[1:], x.dtype),
        scratch_shapes=[pltpu.SemaphoreType.DMA, pltpu.SemaphoreType.DMA],
    )(x)


def _scatter_chips(name, g):
    def body(g_ref, out_ref, send_sems, recv_sems, local_sem):
        mx, my, mc = lax.axis_index("x"), lax.axis_index("y"), lax.axis_index("c")
        me = 2 * mx + my
        chips = [(1 - mx, my), (mx, 1 - my), (1 - mx, 1 - my)]
        mine = pltpu.make_async_copy(g_ref.at[me], out_ref.at[me], local_sem)
        mine.start()
        sends = [pltpu.make_async_remote_copy(src_ref=g_ref.at[2 * px + py], dst_ref=out_ref.at[me], send_sem=send_sems.at[k],
                                              recv_sem=recv_sems.at[k], device_id=(px, py, mc), device_id_type=MESH)
                 for k, (px, py) in enumerate(chips)]
        for cp in sends:
            cp.start()
        for k, (px, py) in enumerate(chips):
            pltpu.make_async_remote_copy(src_ref=g_ref.at[0], dst_ref=out_ref.at[2 * px + py], send_sem=send_sems.at[k],
                                         recv_sem=recv_sems.at[k], device_id=(px, py, mc), device_id_type=MESH).wait_recv()
        for cp in sends:
            cp.wait_send()
        mine.wait()

    return pl.pallas_call(
        body, name=name, in_specs=[HBM_SPEC], out_specs=HBM_SPEC,
        out_shape=jax.ShapeDtypeStruct(g.shape, g.dtype),
        scratch_shapes=[pltpu.SemaphoreType.DMA((3,)), pltpu.SemaphoreType.DMA((3,)), pltpu.SemaphoreType.DMA],
    )(g)


def _allgather_all(name, x):
    def body(x_ref, out_ref, send_sems, recv_sems, local_sem):
        mx, my, mc = lax.axis_index("x"), lax.axis_index("y"), lax.axis_index("c")
        me = 4 * mx + 2 * my + mc
        flips = [(fx, fy, fc) for fx in (0, 1) for fy in (0, 1) for fc in (0, 1)][1:]
        peers = [(mx ^ fx, my ^ fy, mc ^ fc) for fx, fy, fc in flips]
        mine = pltpu.make_async_copy(x_ref, out_ref.at[me], local_sem)
        mine.start()
        sends = [pltpu.make_async_remote_copy(src_ref=x_ref, dst_ref=out_ref.at[me], send_sem=send_sems.at[k],
                                              recv_sem=recv_sems.at[k], device_id=peer, device_id_type=MESH)
                 for k, peer in enumerate(peers)]
        for cp in sends:
            cp.start()
        for k, (px, py, pc) in enumerate(peers):
            pltpu.make_async_remote_copy(src_ref=x_ref, dst_ref=out_ref.at[4 * px + 2 * py + pc], send_sem=send_sems.at[k],
                                         recv_sem=recv_sems.at[k], device_id=(px, py, pc), device_id_type=MESH).wait_recv()
        for cp in sends:
            cp.wait_send()
        mine.wait()

    return pl.pallas_call(
        body, name=name, in_specs=[HBM_SPEC], out_specs=HBM_SPEC,
        out_shape=jax.ShapeDtypeStruct((8,) + x.shape, x.dtype),
        scratch_shapes=[pltpu.SemaphoreType.DMA((7,)), pltpu.SemaphoreType.DMA((7,)), pltpu.SemaphoreType.DMA],
    )(x)


def _allgather_cores(name, x):
    def body(x_ref, out_ref, send_sem, recv_sem, local_sem):
        mx, my, mc = lax.axis_index("x"), lax.axis_index("y"), lax.axis_index("c")
        mine = pltpu.make_async_copy(x_ref, out_ref.at[mc], local_sem)
        mine.start()
        send = pltpu.make_async_remote_copy(src_ref=x_ref, dst_ref=out_ref.at[mc], send_sem=send_sem, recv_sem=recv_sem,
                                            device_id=(mx, my, 1 - mc), device_id_type=MESH)
        send.start()
        pltpu.make_async_remote_copy(src_ref=x_ref, dst_ref=out_ref.at[1 - mc], send_sem=send_sem, recv_sem=recv_sem,
                                     device_id=(mx, my, 1 - mc), device_id_type=MESH).wait_recv()
        send.wait_send()
        mine.wait()

    return pl.pallas_call(
        body, name=name, in_specs=[HBM_SPEC], out_specs=HBM_SPEC,
        out_shape=jax.ShapeDtypeStruct((2,) + x.shape, x.dtype),
        scratch_shapes=[pltpu.SemaphoreType.DMA, pltpu.SemaphoreType.DMA, pltpu.SemaphoreType.DMA],
    )(x)


def _sum_slots(name, st, tb=256):
    n, r, c = st.shape
    tb = _pick(r, (tb, 32))

    def body(s_ref, o_ref):
        acc = s_ref[0].astype(F32)
        for k in range(1, n):
            acc = acc + s_ref[k].astype(F32)
        o_ref[...] = acc

    return pl.pallas_call(
        body, name=name, grid=(r // tb,), in_specs=[pl.BlockSpec((n, tb, c), lambda i: (0, i, 0))],
        out_specs=pl.BlockSpec((tb, c), lambda i: (i, 0)), out_shape=jax.ShapeDtypeStruct((r, c), F32),
        compiler_params=_cparams("parallel"),
    )(st)


def _adamw(name, w, g, m, v):
    def fn(w, g, m, v):
        m2 = ADAM_B1 * m + (1.0 - ADAM_B1) * g
        v2 = ADAM_B2 * v + (1.0 - ADAM_B2) * jnp.square(g)
        m_hat = m2 / (1.0 - ADAM_B1 ** ADAM_STEP)
        v_hat = v2 / (1.0 - ADAM_B2 ** ADAM_STEP)
        delta = -ADAM_LR * (m_hat / (jnp.sqrt(v_hat) + ADAM_EPS) + ADAM_WD * w)
        return [delta, m2, v2], []

    tb = _pick(w.shape[0], (256, 32))
    return _rowwise(name, fn, [_row(w), _row(g), _row(m), _row(v)], [], [(w.shape[1], F32)] * 3, tb=tb)


BIG = ("w_ssd_out", "w_attn_out", "w_o", "w_mlp_in", "w_mlp_out", "conv_w")
SMALL = ("pre_mix_norm", "b_gate", "conv_b", "dt_bias", "a_log", "d_skip", "ssd_norm", "attn_sink", "rel_bias_table",
         "post_mix_norm", "pre_mlp_norm", "post_mlp_norm")
ALL_W = ("pre_mix_norm", "w_in", "b_gate", "conv_w", "conv_b", "dt_bias", "a_log", "d_skip", "ssd_norm", "w_ssd_out",
         "attn_sink", "rel_bias_table", "w_attn_out", "w_o", "post_mix_norm", "pre_mlp_norm", "w_mlp_in", "w_mlp_out",
         "post_mlp_norm")


def _pack_rows(parts, rows, dtype):
    flat = jnp.concatenate([p.reshape(-1, D_MODEL).astype(dtype) for p in parts], axis=0)
    return jnp.pad(flat, ((0, rows - flat.shape[0]), (0, 0)))


def _pack_big(shards, dtype):
    return _pack_rows([shards[n] for n in BIG], BIG_ROWS, dtype)


def _unpack_big(flat, like):
    out, r = {}, 0
    for n in BIG:
        shp = like[n].shape
        nr = math.prod(shp) // D_MODEL
        out[n] = flat[r:r + nr].reshape(shp)
        r += nr
    return out


def _pack_small(parts, extra=None):
    flat = jnp.concatenate([parts[n].reshape(-1).astype(F32) for n in SMALL] + ([extra.reshape(-1)] if extra is not None else []))
    return jnp.pad(flat, (0, SMALL_ROWS * D_MODEL - flat.shape[0])).reshape(SMALL_ROWS, D_MODEL)


def _unpack_small(flat2, like):
    flat = flat2.reshape(-1)
    out, r = {}, 0
    for n in SMALL:
        shp = like[n].shape
        k = math.prod(shp)
        out[n] = flat[r:r + k].reshape(shp)
        r += k
    return out, flat[r]


def _shard_of_full(name, full, s):
    if name in ("w_in", "w_mlp_in"):
        w = full.shape[2] // 4
        return full[:, :, s * w:(s + 1) * w]
    if name == "conv_w":
        w = full.shape[3] // 4
        return full[:, :, :, s * w:(s + 1) * w]
    w = full.shape[1] // 4
    return full[:, s * w:(s + 1) * w, :]


def _full_of_shards(name, shards):
    axis = {"w_in": 2, "w_mlp_in": 2, "conv_w": 3}.get(name, 1)
    return jnp.concatenate(shards, axis=axis)


def _to_proj_layout(w):
    z, xbc, dt, q, k, v, gates = (w[..., 0:2048], w[..., 2048:6144], w[..., 6144:6208], w[..., 6208:7232],
                                  w[..., 7232:7488], w[..., 7488:7744], w[..., 7744:9792])
    pad = jnp.zeros(w.shape[:-1] + (N_MAIN - OFF_DT - dt.shape[-1],), w.dtype)

    def doubled(a):
        h = a.reshape(a.shape[:-1] + (ATTN_KV, 1, ATTN_DIM))
        return jnp.broadcast_to(h, a.shape[:-1] + (ATTN_KV, 2, ATTN_DIM)).reshape(a.shape[:-1] + (2 * a.shape[-1],))

    return jnp.concatenate([z, gates, xbc, dt, pad, q, doubled(k), doubled(v)], axis=-1)


def _from_proj_layout(w):
    z, gates, xbc, dt, q, k2, v2 = (w[..., 0:2048], w[..., 2048:4096], w[..., 4096:8192], w[..., 8192:8256],
                                    w[..., 8320:9344], w[..., 9344:9856], w[..., 9856:10368])

    def folded(a):
        return a.reshape(a.shape[:-1] + (ATTN_KV, 2, ATTN_DIM)).sum(axis=-2).reshape(a.shape[:-1] + (a.shape[-1] // 2,))

    return jnp.concatenate([z, xbc, dt, q, folded(k2), folded(v2), gates], axis=-1)


def _layer_fwd(h1, x, W, P, l, bias4):
    t = x.shape[0]
    S = {"x": x, "h1": h1}
    proj = _matmul("proj", h1, W["w_in_main"][l], "nn")
    qkv = _matmul("proj_qkv", h1, W["w_in_qkv"][l], "nn", out_dtype=BF16)
    S["proj"] = proj
    pre, act = _conv_fwd(proj, W["conv_w"][l], P["conv_b"][l].reshape(1, CONV_DIM))
    S["pre"], S["act"] = pre, act

    dtb = jnp.pad(P["dt_bias"][l].reshape(1, 2 * SSD_HEADS), ((0, 0), (0, LANES - 2 * SSD_HEADS)))

    def dt_fn(raw, b):
        v = raw + b
        return [jnp.maximum(v, 0.0) + jnp.log1p(jnp.exp(-jnp.abs(v)))], []

    (dt,) = _rowwise("dt_fwd", dt_fn, [_row(proj, LANES, OFF_DT // LANES)], [dtb], [(LANES, F32)], tb=512)
    dt2 = jnp.stack([dt[:, 0:SSD_HEADS], dt[:, SSD_HEADS:2 * SSD_HEADS]])
    dt2t = dt2.transpose(0, 2, 1)
    a = -jnp.exp(P["a_log"][l])
    a_row, a_col = a.reshape(2, 1, SSD_HEADS), a.reshape(2, SSD_HEADS, 1)
    dte = jnp.repeat(dt2, SSD_HEAD_DIM, axis=-1)
    S["dt2"], S["dt2t"], S["a_row"], S["a_col"], S["dte"] = dt2, dt2t, a_row, a_col, dte
    y2, states = _ssd_fwd(act, dt2, dt2t, a_row, a_col, dte)
    S["states"] = states

    dsk = jnp.repeat(P["d_skip"][l], SSD_HEAD_DIM).reshape(1, D_INNER)
    nw = P["ssd_norm"][l].reshape(1, D_INNER)
    S["dsk"], S["nw"] = dsk, nw

    def gn_fn(yf, yb, xs, z, dsk, nw):
        y = yf + yb + xs * dsk
        return [y, _gated_norm_fwd(y, z, nw)], []

    y, yn = _rowwise("gated_norm_fwd", gn_fn,
                     [_row(y2, lead=0), _row(y2, lead=1), _row(act, D_INNER, 0), _row(proj, D_INNER, OFF_Z // D_INNER)],
                     [dsk, nw], [(D_INNER, F32), (D_INNER, BF16)])
    S["y"], S["yn"] = y, yn
    y_ssd = _matmul("ssd_out", yn, W["w_ssd_out"][l], "nn")
    S["y_ssd"] = y_ssd

    sink4 = jnp.broadcast_to(P["attn_sink"][l].reshape(ATTN_KV, REP, 1, 1), (ATTN_KV, REP, 1, LANES))
    S["qkv"], S["sink4"] = qkv, sink4
    o = _attn_fwd(qkv, bias4, sink4)
    S["o"] = o
    y_attn = _matmul("attn_out", o, W["w_attn_out"][l], "nn")
    S["y_attn"] = y_attn

    bg = P["b_gate"][l].reshape(1, 2 * D_MODEL)
    S["bg"] = bg

    def merge_fn(gates, ys, ya, b):
        g = jax.nn.sigmoid(gates + b)
        return [g[:, :D_MODEL] * ys + g[:, D_MODEL:] * ya], []

    (mix_in,) = _rowwise("merge_fwd", merge_fn, [_row(proj, 2 * D_MODEL, OFF_G // (2 * D_MODEL)), _row(y_ssd), _row(y_attn)],
                         [bg], [(D_MODEL, BF16)])
    S["mix_in"] = mix_in
    mixed = _matmul("w_o", mix_in, W["w_o"][l], "nn")
    S["mixed"] = mixed

    g_pm = P["post_mix_norm"][l].reshape(1, D_MODEL)
    g_pl = P["pre_mlp_norm"][l].reshape(1, D_MODEL)

    def postmix_fn(x, mixed, g1, g2):
        x2 = x + _rms_fwd(mixed, g1)
        return [x2, _rms_fwd(x2, g2)], []

    x2, h2 = _rowwise("post_mix_fwd", postmix_fn, [_row(x), _row(mixed)], [g_pm, g_pl], [(D_MODEL, F32), (D_MODEL, BF16)])
    S["x2"], S["h2"] = x2, h2
    a1 = _matmul("mlp_in", h2, W["w_mlp_in"][l], "nn", out_dtype=BF16,
                 epilogue=lambda acc: jnp.square(jnp.maximum(acc, 0.0)))
    S["a1"] = a1
    f2 = _matmul("mlp_out", a1, W["w_mlp_out"][l], "nn")
    S["f2"] = f2
    return S


def _layer_bwd(S, dx3, W, P, l, bias4, onehot_t):
    t = dx3.shape[0]
    G = {}
    g_pmlp = P["post_mlp_norm"][l].reshape(1, D_MODEL)

    def b1_fn(f2, dx3, g):
        df2, dg = _rms_bwd(f2, g, dx3)
        return [df2], [dg]

    df2, G["post_mlp_norm"] = _rowwise("post_mlp_bwd", b1_fn, [_row(S["f2"]), _row(dx3)], [g_pmlp], [(D_MODEL, BF16)], [(1, D_MODEL)])
    df1 = _matmul("d_f1", df2, W["w_mlp_out"][l], "nt", out_dtype=BF16,
                  epilogue=lambda acc, a1: acc * (2.0 * jnp.sqrt(a1.astype(F32))), extras=[S["a1"]])
    G["w_mlp_out"] = _matmul("dw_mlp_out", S["a1"], df2, "tn")
    dh2 = _matmul("d_h2", df1, W["w_mlp_in"][l], "nt")
    G["w_mlp_in"] = _matmul("dw_mlp_in", S["h2"], df1, "tn")

    g_pm = P["post_mix_norm"][l].reshape(1, D_MODEL)
    g_pl = P["pre_mlp_norm"][l].reshape(1, D_MODEL)

    def b3_fn(x2, dh2, dx3, mixed, g_pl, g_pm):
        d1, dgl = _rms_bwd(x2, g_pl, dh2)
        dx2 = dx3 + d1
        dmixed, dgm = _rms_bwd(mixed, g_pm, dx2)
        return [dx2, dmixed], [dgl, dgm]

    dx2, dmixed, G["pre_mlp_norm"], G["post_mix_norm"] = _rowwise(
        "post_mix_bwd", b3_fn, [_row(S["x2"]), _row(dh2), _row(dx3), _row(S["mixed"])], [g_pl, g_pm],
        [(D_MODEL, F32), (D_MODEL, BF16)], [(1, D_MODEL), (1, D_MODEL)])
    dmix_in = _matmul("d_mix_in", dmixed, W["w_o"][l], "nt")
    G["w_o"] = _matmul("dw_o", S["mix_in"], dmixed, "tn")

    proj = S["proj"]

    def b4_fn(gates, ys, ya, dmix, b):
        g = jax.nn.sigmoid(gates + b)
        gs, ga = g[:, :D_MODEL], g[:, D_MODEL:]
        dg = jnp.concatenate([ys * dmix, ya * dmix], axis=-1) * g * (1.0 - g)
        return [gs * dmix, ga * dmix, dg], [jnp.sum(dg, axis=0, keepdims=True)]

    dy_ssd, dy_attn, dgates, G["b_gate"] = _rowwise(
        "merge_bwd", b4_fn, [_row(proj, 2 * D_MODEL, OFF_G // (2 * D_MODEL)), _row(S["y_ssd"]), _row(S["y_attn"]), _row(dmix_in)],
        [S["bg"]], [(D_MODEL, BF16), (D_MODEL, BF16), (2 * D_MODEL, BF16)], [(1, 2 * D_MODEL)])

    dyn = _matmul("d_yn", dy_ssd, W["w_ssd_out"][l], "nt")
    G["w_ssd_out"] = _matmul("dw_ssd_out", S["yn"], dy_ssd, "tn")
    do = _matmul("d_o", dy_attn, W["w_attn_out"][l], "nt", out_dtype=BF16)
    G["w_attn_out"] = _matmul("dw_attn_out", S["o"], dy_attn, "tn")

    dq, dkp, dvp, dbias4, dsink4 = _attn_bwd(S["qkv"], bias4, S["sink4"], do)
    dk = _kv_combine("dk_combine", dkp)
    dv = _kv_combine("dv_combine", dvp)
    G["attn_sink"] = dsink4[:, :, 0, 0].reshape(ATTN_HEADS)
    G["rel_bias_table"] = _bias_table_grad(dbias4.reshape(ATTN_HEADS, BLOCK * 3 * BLOCK), onehot_t).T

    act = S["act"]

    def b5_fn(y, z, xs, dyn, nw, dsk):
        dy, dz, dnw = _gated_norm_bwd(y, z, nw, dyn)
        return [dy, dz], [dnw, jnp.sum(dy * xs, axis=0, keepdims=True)]

    dy, dz, G["ssd_norm"], dskip_cols = _rowwise(
        "gated_norm_bwd", b5_fn, [_row(S["y"]), _row(proj, D_INNER, OFF_Z // D_INNER), _row(act, D_INNER, 0), _row(dyn)],
        [S["nw"], S["dsk"]], [(D_INNER, F32), (D_INNER, BF16)], [(1, D_INNER), (1, D_INNER)])
    G["d_skip"] = dskip_cols.reshape(SSD_HEADS, SSD_HEAD_DIM).sum(axis=-1)

    dxs2, dbs2, dcs2, ddt2, da2 = _ssd_bwd(act, S["dt2"], S["dt2t"], S["a_row"], S["a_col"], S["dte"], dy, S["states"])
    G["a_log"] = da2.reshape(2, SSD_HEADS) * S["a_row"].reshape(2, SSD_HEADS)

    def b6_fn(dxf, dxb, dy, dbf, dbb, dcf, dcb, pre, dsk):
        dact = jnp.concatenate([dxf + dxb + dy * dsk, dbf + dbb, dcf + dcb], axis=-1)
        return [dact * _silu_grad(pre)], []

    (dpre,) = _rowwise("silu_bwd", b6_fn,
                       [_row(dxs2, lead=0), _row(dxs2, lead=1), _row(dy), _row(dbs2, lead=0), _row(dbs2, lead=1),
                        _row(dcs2, lead=0), _row(dcs2, lead=1), _row(S["pre"])], [S["dsk"]], [(CONV_DIM, F32)], tb=128)
    du, dconv_w, dconv_b = _conv_bwd(dpre, proj, W["conv_w"][l])
    G["conv_w"] = dconv_w.reshape(SSD_CONV, 1, CONV_DIM)
    G["conv_b"] = dconv_b.reshape(CONV_DIM)

    dtb = jnp.pad(P["dt_bias"][l].reshape(1, 2 * SSD_HEADS), ((0, 0), (0, LANES - 2 * SSD_HEADS)))
    ddt = jnp.pad(jnp.concatenate([ddt2[0], ddt2[1]], axis=-1), ((0, 0), (0, LANES - 2 * SSD_HEADS)))

    def b7_fn(raw, ddt, b):
        draw = ddt * jax.nn.sigmoid(raw + b)
        return [draw], [jnp.sum(draw, axis=0, keepdims=True)]

    draw, ddtb = _rowwise("dt_bwd", b7_fn, [_row(proj, LANES, OFF_DT // LANES), _row(ddt)], [dtb], [(LANES, BF16)], [(1, LANES)], tb=512)
    G["dt_bias"] = ddtb[0, :2 * SSD_HEADS].reshape(2, SSD_HEADS)

    dproj = jnp.concatenate([dz, dgates, du, draw, dq, dk, dv], axis=-1)
    G["w_in"] = _from_proj_layout(_matmul("dw_in", S["h1"], dproj, "tn"))
    dh1 = _matmul("d_h1", dproj, W["w_in"][l], "nt")

    g_pre = P["pre_mix_norm"][l].reshape(1, D_MODEL)

    def b8_fn(x, dh1, dx2, g):
        d1, dg = _rms_bwd(x, g, dh1)
        return [dx2 + d1], [dg]

    dx, G["pre_mix_norm"] = _rowwise("pre_mix_bwd", b8_fn, [_row(S["x"]), _row(dh1), _row(dx2)], [g_pre], [(D_MODEL, F32)], [(1, D_MODEL)])
    return dx, G


def _reduce_scatter(tag, pieces):
    _, _, r2, c = pieces.shape
    theirs = _swap_cores("swap_" + tag, pieces).reshape(4 * r2, c)
    mine = lax.dynamic_index_in_dim(pieces, lax.axis_index("c"), axis=0, keepdims=False).reshape(4 * r2, c)
    (chip_sum,) = _rowwise("presum_" + tag, lambda a, b: ([a.astype(F32) + b.astype(F32)], []), [_row(mine), _row(theirs)], [],
                           [(c, BF16)])
    staged = _scatter_chips("scatter_" + tag, chip_sum.reshape(4, r2, c))
    half = _sum_slots("sum_" + tag, staged)
    return _allgather_cores("share_" + tag, half).reshape(2 * r2, c)


def _step(x, target, shard_w, shard_m, shard_v):
    depth, _, win_cols = shard_w["w_in"].shape
    win_rows = depth * D_MODEL

    def win2d(a):
        return a.reshape(win_rows, win_cols)

    win_g = _allgather_chips("gather_w_in", win2d(shard_w["w_in"]).astype(BF16))
    gathered = _allgather_chips("gather_weights", _pack_big(shard_w, BF16))
    conv_rows = shard_w["conv_w"].reshape(-1, D_MODEL)
    conv_g = _allgather_chips("gather_conv", jnp.pad(conv_rows, ((0, 16 - conv_rows.shape[0]), (0, 0))))
    per_chip = [_unpack_big(gathered[s], shard_w) for s in range(4)]
    W = {n: _full_of_shards(n, [per_chip[s][n] for s in range(4)]) for n in BIG if n != "conv_w"}
    W["w_in"] = _to_proj_layout(jnp.concatenate([win_g[s].reshape(depth, D_MODEL, win_cols) for s in range(4)], axis=2))
    W["conv_w"] = jnp.concatenate([conv_g[s][:conv_rows.shape[0]].reshape(shard_w["conv_w"].shape) for s in range(4)],
                                  axis=3).reshape(depth, SSD_CONV, CONV_DIM)
    P = {n: shard_w[n] for n in SMALL}
    loss_part, grad_x, full = _local_step(x, target, W, P)

    assert depth == 2
    win_pieces = jnp.stack([jnp.stack([full["w_in"][h][:, s * win_cols:(s + 1) * win_cols] for s in range(4)])
                            for h in range(2)]).astype(BF16)
    g_win = _reduce_scatter("w_in", win_pieces)
    d_win, m_win, v_win = _adamw("adamw_w_in", win2d(shard_w["w_in"]), g_win, win2d(shard_m["w_in"]), win2d(shard_v["w_in"]))
    packs = [_pack_big({n: _shard_of_full(n, full[n], s) for n in BIG}, BF16) for s in range(4)]
    r2 = BIG_ROWS // 2
    pieces = jnp.stack([jnp.stack([p[h * r2:(h + 1) * r2] for p in packs]) for h in range(2)])
    g_big = _reduce_scatter("grads", pieces)
    d_big, m_big, v_big = _adamw("adamw_big", _pack_big(shard_w, F32), g_big, _pack_big(shard_m, F32), _pack_big(shard_v, F32))

    small = _allgather_all("gather_small", _pack_small(full, loss_part))
    g_small = _sum_slots("sum_small", small, tb=SMALL_ROWS)
    d_small, m_small, v_small = _adamw("adamw_small", _pack_small(shard_w, jnp.zeros((), F32)), g_small,
                                       _pack_small(shard_m, jnp.zeros((), F32)), _pack_small(shard_v, jnp.zeros((), F32)))

    outs = {}
    for tag, win, big, sm in (("grad", g_win, g_big, g_small), ("delta", d_win, d_big, d_small),
                              ("new_m", m_win, m_big, m_small), ("new_v", v_win, v_big, v_small)):
        ub = _unpack_big(big, shard_w)
        us, extra = _unpack_small(sm, shard_w)
        outs[tag] = {"w_in": win.reshape(shard_w["w_in"].shape), **ub, **us}
        if tag == "grad":
            loss = extra
    return loss, grad_x, outs


def _local_step(x, target, W, P):
    depth = W["w_in"].shape[0]
    W = dict(W, w_in_main=W["w_in"][:, :, :N_MAIN], w_in_qkv=W["w_in"][:, :, N_MAIN:])
    onehot_t = (_bucket_map().reshape(1, -1) == jnp.arange(N_BUCKETS)[:, None]).astype(F32)
    bias = _bias_from_table(P["rel_bias_table"], onehot_t).reshape(ATTN_HEADS, BLOCK, 3 * BLOCK)
    bias4 = jnp.where(_band_mask(), bias, NEG).reshape(ATTN_KV, REP, BLOCK, 3 * BLOCK)

    def pre_fn(x, g):
        return [_rms_fwd(x, g)], []

    (h1,) = _rowwise("pre_mix_fwd", pre_fn, [_row(x)], [P["pre_mix_norm"][0].reshape(1, D_MODEL)], [(D_MODEL, BF16)])
    saved = []
    loss_cols = dxl = None
    for l in range(depth):
        S = _layer_fwd(h1, x, W, P, l, bias4)
        saved.append(S)
        g_pmlp = P["post_mlp_norm"][l].reshape(1, D_MODEL)
        if l + 1 < depth:
            def post_fn(x2, f2, g1, g2):
                x3 = x2 + _rms_fwd(f2, g1)
                return [x3, _rms_fwd(x3, g2)], []

            x, h1 = _rowwise("post_mlp_fwd", post_fn, [_row(S["x2"]), _row(S["f2"])],
                             [g_pmlp, P["pre_mix_norm"][l + 1].reshape(1, D_MODEL)], [(D_MODEL, F32), (D_MODEL, BF16)])
        else:
            def loss_fn(x2, f2, tgt, g1):
                diff = x2 + _rms_fwd(f2, g1) - tgt
                return [diff * (1.0 / D_MODEL)], [jnp.sum(diff * diff, axis=0, keepdims=True)]

            dxl, loss_cols = _rowwise("loss", loss_fn, [_row(S["x2"]), _row(S["f2"]), _row(target)], [g_pmlp],
                                      [(D_MODEL, F32)], [(1, D_MODEL)])
    loss_part = 0.5 * jnp.sum(loss_cols) / D_MODEL

    grads = [None] * depth
    dx = dxl
    for l in reversed(range(depth)):
        dx, grads[l] = _layer_bwd(saved[l], dx, W, P, l, bias4, onehot_t)
    grad_x = dx

    full = {n: jnp.stack([grads[l][n] for l in range(depth)]) for n in ALL_W if n != "rel_bias_table"}
    full["rel_bias_table"] = sum(grads[l]["rel_bias_table"] for l in range(depth))
    return loss_part, grad_x, full


def kernel(x, pre_mix_norm, w_in, b_gate, conv_w, conv_b, dt_bias, a_log, d_skip, ssd_norm, w_ssd_out, attn_sink, rel_bias_table, w_attn_out, w_o, post_mix_norm, pre_mlp_norm, w_mlp_in, w_mlp_out, post_mlp_norm, loss_target, m_pre_mix_norm, m_w_in, m_b_gate, m_conv_w, m_conv_b, m_dt_bias, m_a_log, m_d_skip, m_ssd_norm, m_w_ssd_out, m_attn_sink, m_rel_bias_table, m_w_attn_out, m_w_o, m_post_mix_norm, m_pre_mlp_norm, m_w_mlp_in, m_w_mlp_out, m_post_mlp_norm, v_pre_mix_norm, v_w_in, v_b_gate, v_conv_w, v_conv_b, v_dt_bias, v_a_log, v_d_skip, v_ssd_norm, v_w_ssd_out, v_attn_sink, v_rel_bias_table, v_w_attn_out, v_w_o, v_post_mix_norm, v_pre_mlp_norm, v_w_mlp_in, v_w_mlp_out, v_post_mlp_norm):
    a = locals()
    shard_w = {n: a[n] for n in ALL_W}
    shard_m = {n: a["m_" + n] for n in ALL_W}
    shard_v = {n: a["v_" + n] for n in ALL_W}
    loss, grad_x, outs = _step(x[0], loss_target[0], shard_w, shard_m, shard_v)
    return (loss, grad_x[None], *[outs["grad"][n] for n in ALL_W], *[outs["delta"][n] for n in ALL_W],
            *[outs["new_m"][n] for n in ALL_W], *[outs["new_v"][n] for n in ALL_W])
```

```python
import math

import jax
import jax.numpy as jnp
from jax import lax
from jax.experimental import pallas as pl
from jax.experimental.pallas import tpu as pltpu

F32 = jnp.float32
BF16 = jnp.bfloat16
MESH = pl.DeviceIdType.MESH

VMEM_LIMIT_BYTES = 52 * 1024 * 1024
LANES = 128
SUBLANES = 8

EPS = 1e-6
D_MODEL = 1024
D_INNER = 2048
SSD_HEADS = 32
SSD_HEAD_DIM = 64
SSD_GROUPS = 8
SSD_STATE = 128
SSD_CONV = 5
CHUNK = 128
CONV_DIM = 4096
ATTN_HEADS = 16
ATTN_KV = 4
ATTN_DIM = 64
BLOCK = 128
N_BUCKETS = 32
MAX_DISTANCE = 128
D_FF = 4096
N_IN = 9792
NEG = -1e30

OFF_Z, OFF_G, OFF_XBC, OFF_DT, N_MAIN, N_PROJ = 0, 2048, 4096, 8192, 8320, 10368
N_QKV = N_PROJ - N_MAIN

ADAM_LR, ADAM_B1, ADAM_B2, ADAM_EPS, ADAM_WD, ADAM_STEP = 0.001, 0.9, 0.999, 1e-08, 0.01, 10

BIG_ROWS = 6656
SMALL_ROWS = 32


def _cparams(*sem):
    return pltpu.CompilerParams(dimension_semantics=sem, vmem_limit_bytes=VMEM_LIMIT_BYTES)


def _dot(a, b, precision=None):
    return lax.dot_general(a, b, (((1,), (0,)), ((), ())), preferred_element_type=F32, precision=precision)


def _dot_nt(a, b, precision=None):
    return lax.dot_general(a, b, (((1,), (1,)), ((), ())), preferred_element_type=F32, precision=precision)


def _dot_tn(a, b):
    return lax.dot_general(a, b, (((0,), (0,)), ((), ())), preferred_element_type=F32)


def _pick(n, prefs):
    for p in prefs:
        if n % p == 0:
            return p
    return n


def _matmul(name, a, b, mode, out_dtype=F32, epilogue=None, extras=()):
    if mode == "nn":
        (m, k), (_, n) = a.shape, b.shape
    elif mode == "nt":
        (m, k), (n, _) = a.shape, b.shape
    else:
        (k, m), (_, n) = a.shape, b.shape
    tm = _pick(m, (512, 256, 128)) if mode == "tn" else _pick(m, (1024, 512, 256, 128))
    tn = _pick(n, (1024, 1152, 640, 512, 256, 128))
    tk = _pick(k, (1024, 1152, 512, 256, 128)) if mode != "tn" else _pick(k, (512, 256, 128))
    nk = k // tk
    if mode == "nn":
        a_spec = pl.BlockSpec((tm, tk), lambda i, j, q: (i, q))
        b_spec = pl.BlockSpec((tk, tn), lambda i, j, q: (q, j))
        fn = _dot
    elif mode == "nt":
        a_spec = pl.BlockSpec((tm, tk), lambda i, j, q: (i, q))
        b_spec = pl.BlockSpec((tn, tk), lambda i, j, q: (j, q))
        fn = _dot_nt
    else:
        a_spec = pl.BlockSpec((tk, tm), lambda i, j, q: (q, i))
        b_spec = pl.BlockSpec((tk, tn), lambda i, j, q: (q, j))
        fn = _dot_tn

    tile = pl.BlockSpec((tm, tn), lambda i, j, q: (i, j))
    n_ex = len(extras)

    def body(a_ref, b_ref, *rest):
        ex_refs, o_ref = rest[:n_ex], rest[n_ex]

        def store(acc):
            v = acc if epilogue is None else epilogue(acc, *[r[...] for r in ex_refs])
            o_ref[...] = v.astype(o_ref.dtype)

        p = fn(a_ref[...].astype(BF16), b_ref[...].astype(BF16))
        if nk == 1:
            store(p)
        else:
            acc_ref = rest[n_ex + 1]
            q = pl.program_id(2)

            @pl.when(q == 0)
            def _():
                acc_ref[...] = p

            @pl.when((q > 0) & (q < nk - 1))
            def _():
                acc_ref[...] += p

            @pl.when(q == nk - 1)
            def _():
                store(acc_ref[...] + p)

    return pl.pallas_call(
        body, name=name, grid=(m // tm, n // tn, nk),
        in_specs=[a_spec, b_spec] + [tile] * n_ex, out_specs=tile,
        out_shape=jax.ShapeDtypeStruct((m, n), out_dtype),
        scratch_shapes=[pltpu.VMEM((tm, tn), F32)] if nk > 1 else [],
        compiler_params=_cparams("parallel", "parallel", "arbitrary"),
    )(a, b, *extras)


def _row(arr, width=None, cb=0, lead=None):
    return (arr, width, cb, lead)


def _rowwise(name, fn, rows, vecs, outs, accs=(), tb=256):
    t = rows[0][0].shape[-2]
    tb = min(tb, t)
    in_specs, args = [], []
    for arr, width, cb, lead in rows:
        w = arr.shape[-1] if width is None else width
        if lead is None:
            in_specs.append(pl.BlockSpec((tb, w), lambda i, cb=cb: (i, cb)))
        else:
            in_specs.append(pl.BlockSpec((None, tb, w), lambda i, cb=cb, lead=lead: (lead, i, cb)))
        args.append(arr)
    for v in vecs:
        in_specs.append(pl.BlockSpec(v.shape, lambda i, nd=v.ndim: (0,) * nd))
        args.append(v)
    out_shape = [jax.ShapeDtypeStruct((t, c), dt) for c, dt in outs] + [jax.ShapeDtypeStruct(s, F32) for s in accs]
    out_specs = [pl.BlockSpec((tb, c), lambda i: (i, 0)) for c, _ in outs] + [pl.BlockSpec(s, lambda i: (0, 0)) for s in accs]
    n_in, n_out = len(args), len(outs)

    def body(*refs):
        vals = [r[...] for r in refs[:n_in]]
        o_vals, a_vals = fn(*vals)
        for r, v in zip(refs[n_in:n_in + n_out], o_vals):
            r[...] = v.astype(r.dtype)
        first = pl.program_id(0) == 0
        for r, v in zip(refs[n_in + n_out:], a_vals):
            @pl.when(first)
            def _(r=r, v=v):
                r[...] = v

            @pl.when(jnp.logical_not(first))
            def _(r=r, v=v):
                r[...] += v

    res = pl.pallas_call(
        body, name=name, grid=(t // tb,), in_specs=in_specs, out_specs=out_specs, out_shape=out_shape,
        compiler_params=_cparams("arbitrary"),
    )(*args)
    return res


def _rms_fwd(x, g):
    r = lax.rsqrt(jnp.mean(x * x, axis=-1, keepdims=True) + EPS)
    return x * r * g


def _rms_bwd(x, g, dy):
    r = lax.rsqrt(jnp.mean(x * x, axis=-1, keepdims=True) + EPS)
    xh = x * r
    dxh = dy * g
    dx = r * (dxh - xh * jnp.mean(dxh * xh, axis=-1, keepdims=True))
    return dx, jnp.sum(dy * xh, axis=0, keepdims=True)


def _silu(x):
    return x * jax.nn.sigmoid(x)


def _silu_grad(x):
    s = jax.nn.sigmoid(x)
    return s * (1.0 + x * (1.0 - s))


GROUP_W = D_INNER // SSD_GROUPS


def _gated_norm_fwd(y, z, w):
    u = y * _silu(z)
    parts = []
    for j in range(SSD_GROUPS):
        ug = u[:, j * GROUP_W:(j + 1) * GROUP_W]
        parts.append(ug * lax.rsqrt(jnp.mean(ug * ug, axis=-1, keepdims=True) + EPS))
    return jnp.concatenate(parts, axis=-1) * w


def _gated_norm_bwd(y, z, w, dyn):
    sz = _silu(z)
    u = y * sz
    duh = dyn * w
    du_parts, uh_parts = [], []
    for j in range(SSD_GROUPS):
        sl = slice(j * GROUP_W, (j + 1) * GROUP_W)
        ug = u[:, sl]
        r = lax.rsqrt(jnp.mean(ug * ug, axis=-1, keepdims=True) + EPS)
        uh = ug * r
        dg = duh[:, sl]
        du_parts.append(r * (dg - uh * jnp.mean(dg * uh, axis=-1, keepdims=True)))
        uh_parts.append(uh)
    du = jnp.concatenate(du_parts, axis=-1)
    uh = jnp.concatenate(uh_parts, axis=-1)
    dw = jnp.sum(dyn * uh, axis=0, keepdims=True)
    return du * sz, du * y * _silu_grad(z), dw


HALO = SUBLANES


def _halo_specs(tb, cb, col0, t):
    nblk8 = t // HALO
    per = tb // HALO
    main = pl.BlockSpec((tb, cb), lambda j, i: (i, col0 + j))
    prev = pl.BlockSpec((HALO, cb), lambda j, i: (jnp.maximum(i * per - 1, 0), col0 + j))
    nxt = pl.BlockSpec((HALO, cb), lambda j, i: (jnp.minimum((i + 1) * per, nblk8 - 1), col0 + j))
    return main, prev, nxt


def _fill_ext(ext_ref, cur_ref, prev_ref, next_ref, tb, ni):
    i = pl.program_id(1)
    ext_ref[0:HALO, :] = jnp.where(i > 0, prev_ref[...], 0.0)
    ext_ref[HALO:HALO + tb, :] = cur_ref[...]
    ext_ref[HALO + tb:HALO + tb + HALO, :] = jnp.where(i < ni - 1, next_ref[...], 0.0)


def _conv_fwd(proj, w, b):
    t = proj.shape[0]
    tb, cb = min(512, t), 512
    ni, nj = t // tb, CONV_DIM // cb
    main, prev, nxt = _halo_specs(tb, cb, OFF_XBC // cb, t)
    pad = (SSD_CONV - 1) // 2

    def body(u_ref, up_ref, un_ref, w_ref, b_ref, pre_ref, act_ref, ext_ref):
        _fill_ext(ext_ref, u_ref, up_ref, un_ref, tb, ni)
        acc = jnp.broadcast_to(b_ref[...], (tb, cb))
        for k in range(SSD_CONV):
            acc = acc + w_ref[k:k + 1, :] * ext_ref[pl.ds(HALO + k - pad, tb), :]
        pre_ref[...] = acc
        act_ref[...] = _silu(acc)

    out = pl.BlockSpec((tb, cb), lambda j, i: (i, j))
    return pl.pallas_call(
        body, name="conv_fwd", grid=(nj, ni),
        in_specs=[main, prev, nxt, pl.BlockSpec((SSD_CONV, cb), lambda j, i: (0, j)), pl.BlockSpec((1, cb), lambda j, i: (0, j))],
        out_specs=[out, out],
        out_shape=[jax.ShapeDtypeStruct((t, CONV_DIM), F32)] * 2,
        scratch_shapes=[pltpu.VMEM((tb + 2 * HALO, cb), F32)],
        compiler_params=_cparams("parallel", "arbitrary"),
    )(proj, proj, proj, w, b)


def _conv_bwd(dpre, proj, w):
    t = proj.shape[0]
    tb, cb = min(512, t), 512
    ni, nj = t // tb, CONV_DIM // cb
    umain, uprev, unext = _halo_specs(tb, cb, OFF_XBC // cb, t)
    dmain, dprev, dnext = _halo_specs(tb, cb, 0, t)
    pad = (SSD_CONV - 1) // 2

    def body(d_ref, dp_ref, dn_ref, u_ref, up_ref, un_ref, w_ref, du_ref, dw_ref, db_ref, extd_ref, extu_ref):
        _fill_ext(extd_ref, d_ref, dp_ref, dn_ref, tb, ni)
        _fill_ext(extu_ref, u_ref, up_ref, un_ref, tb, ni)
        d = d_ref[...]
        du = jnp.zeros((tb, cb), F32)
        @pl.when(pl.program_id(1) == 0)
        def _():
            dw_ref[...] = jnp.zeros_like(dw_ref)
            db_ref[...] = jnp.zeros_like(db_ref)

        for k in range(SSD_CONV):
            du = du + w_ref[k:k + 1, :] * extd_ref[pl.ds(HALO - k + pad, tb), :]
            dw_ref[k:k + 1, :] += jnp.sum(d * extu_ref[pl.ds(HALO + k - pad, tb), :], axis=0, keepdims=True)
        du_ref[...] = du.astype(du_ref.dtype)
        db_ref[...] += jnp.sum(d, axis=0, keepdims=True)

    return pl.pallas_call(
        body, name="conv_bwd", grid=(nj, ni),
        in_specs=[dmain, dprev, dnext, umain, uprev, unext, pl.BlockSpec((SSD_CONV, cb), lambda j, i: (0, j))],
        out_specs=[pl.BlockSpec((tb, cb), lambda j, i: (i, j)), pl.BlockSpec((SSD_CONV, cb), lambda j, i: (0, j)),
                   pl.BlockSpec((1, cb), lambda j, i: (0, j))],
        out_shape=[jax.ShapeDtypeStruct((t, CONV_DIM), BF16), jax.ShapeDtypeStruct((SSD_CONV, CONV_DIM), F32),
                   jax.ShapeDtypeStruct((1, CONV_DIM), F32)],
        scratch_shapes=[pltpu.VMEM((tb + 2 * HALO, cb), F32)] * 2,
        compiler_params=_cparams("parallel", "arbitrary"),
    )(dpre, dpre, dpre, proj, proj, proj, w)


PAIR = 2 * SSD_HEAD_DIM
HI = lax.Precision.HIGHEST


def _ssd_prelude(d, dt_ref, dtt_ref, ar_ref, ac_ref):
    li = lax.broadcasted_iota(jnp.int32, (CHUNK, CHUNK), 0)
    si = lax.broadcasted_iota(jnp.int32, (CHUNK, CHUNK), 1)
    fwd = d == 0
    hi, lo = jnp.where(fwd, li, si), jnp.where(fwd, si, li)
    tri = hi >= lo
    trif = tri.astype(F32)
    trit = (hi <= lo).astype(F32)
    dt = dt_ref[...]
    adt = dt * ar_ref[...]
    adtt = dtt_ref[...] * ac_ref[...]
    p = _dot(trif, adt, HI)
    pt = _dot_nt(adtt, trif, HI)
    tot = jnp.sum(adt, axis=0, keepdims=True)
    return tri, trit, dt, p, pt, tot


def _ssd_specs(nc, rev):
    def cidx(d, c):
        up = (d == 1) if rev else (d == 0)
        return jnp.where(up, c, nc - 1 - c)

    specs = [
        pl.BlockSpec((CHUNK, D_INNER), lambda d, c: (cidx(d, c), 0)),
        pl.BlockSpec((CHUNK, 1024), lambda d, c: (cidx(d, c), 2)),
        pl.BlockSpec((CHUNK, 1024), lambda d, c: (cidx(d, c), 3)),
        pl.BlockSpec((None, CHUNK, SSD_HEADS), lambda d, c: (d, cidx(d, c), 0)),
        pl.BlockSpec((None, SSD_HEADS, CHUNK), lambda d, c: (d, 0, cidx(d, c))),
        pl.BlockSpec((None, 1, SSD_HEADS), lambda d, c: (d, 0, 0)),
        pl.BlockSpec((None, SSD_HEADS, 1), lambda d, c: (d, 0, 0)),
        pl.BlockSpec((None, CHUNK, D_INNER), lambda d, c: (d, cidx(d, c), 0)),
    ]
    return cidx, specs


def _head_decay(tri, p, pt, tot, h):
    pb = jnp.broadcast_to(p[:, h:h + 1], (CHUNK, CHUNK))
    dec = jnp.exp(jnp.where(tri, pb - pt[h:h + 1, :], NEG))
    return dec, jnp.exp(tot[:, h:h + 1] - pb), jnp.exp(pb)


def _ssd_fwd(act, dt2, dt2t, a_row, a_col, dte):
    t = act.shape[0]
    nc = t // CHUNK
    cidx, specs = _ssd_specs(nc, rev=False)

    def body(xs_ref, bs_ref, cs_ref, dt_ref, dtt_ref, ar_ref, ac_ref, dte_ref, y_ref, st_ref, h_ref):
        d, c = pl.program_id(0), pl.program_id(1)

        @pl.when(c == 0)
        def _():
            h_ref[...] = jnp.zeros_like(h_ref)

        st_ref[...] = h_ref[...]
        tri, _, _, p, pt, tot = _ssd_prelude(d, dt_ref, dtt_ref, ar_ref, ac_ref)
        etot = jnp.exp(tot)
        lane = lax.broadcasted_iota(jnp.int32, (CHUNK, PAIR), 1) < SSD_HEAD_DIM
        rowh = lax.broadcasted_iota(jnp.int32, (PAIR, SSD_STATE), 0) < SSD_HEAD_DIM
        for g in range(SSD_GROUPS):
            gs = slice(g * SSD_STATE, (g + 1) * SSD_STATE)
            bg = bs_ref[:, gs]
            cb = cs_ref[:, gs].astype(BF16)
            cbm = _dot_nt(cb, bg.astype(BF16))
            for pr in range(2):
                h0 = g * 4 + pr * 2
                h1 = h0 + 1
                sl = slice(h0 * SSD_HEAD_DIM, h0 * SSD_HEAD_DIM + PAIR)
                xdt = (xs_ref[:, sl] * dte_ref[:, sl]).astype(BF16)
                yd, st, epb = [], [], []
                for h in (h0, h1):
                    dec, wb, eb = _head_decay(tri, p, pt, tot, h)
                    yd.append(_dot((cbm * dec).astype(BF16), xdt))
                    st.append(_dot_tn(xdt, (bg * wb).astype(BF16)))
                    epb.append(eb)
                hin = h_ref[sl, :]
                yo = _dot_nt(cb, hin.astype(BF16)) * jnp.where(lane, epb[0], epb[1])
                y_ref[:, sl] = jnp.where(lane, yd[0], yd[1]) + yo
                et = jnp.where(rowh, etot[:, h0:h0 + 1], etot[:, h1:h1 + 1])
                h_ref[sl, :] = hin * et + jnp.where(rowh, st[0], st[1])

    return pl.pallas_call(
        body, name="ssd_fwd", grid=(2, nc), in_specs=specs,
        out_specs=[pl.BlockSpec((None, CHUNK, D_INNER), lambda d, c: (d, cidx(d, c), 0)),
                   pl.BlockSpec((None, None, D_INNER, SSD_STATE), lambda d, c: (d, cidx(d, c), 0, 0))],
        out_shape=[jax.ShapeDtypeStruct((2, t, D_INNER), F32), jax.ShapeDtypeStruct((2, nc, D_INNER, SSD_STATE), F32)],
        scratch_shapes=[pltpu.VMEM((D_INNER, SSD_STATE), F32)],
        compiler_params=_cparams("arbitrary", "arbitrary"),
    )(act, act, act, dt2, dt2t, a_row, a_col, dte)


def _put_col(acc, col, h):
    lane = lax.broadcasted_iota(jnp.int32, acc.shape, 1)
    return jnp.where(lane == h, col, acc)


def _put_row(acc, row, h):
    sub = lax.broadcasted_iota(jnp.int32, acc.shape, 0)
    return jnp.where(sub == h, row, acc)


def _sum_all(x):
    return jnp.sum(jnp.sum(x, axis=0, keepdims=True), axis=1, keepdims=True)


def _ssd_bwd(act, dt2, dt2t, a_row, a_col, dte, dy, states):
    t = act.shape[0]
    nc = t // CHUNK
    cidx, specs = _ssd_specs(nc, rev=True)
    specs = specs + [
        pl.BlockSpec((CHUNK, D_INNER), lambda d, c: (cidx(d, c), 0)),
        pl.BlockSpec((None, None, D_INNER, SSD_STATE), lambda d, c: (d, cidx(d, c), 0, 0)),
    ]

    def body(xs_ref, bs_ref, cs_ref, dt_ref, dtt_ref, ar_ref, ac_ref, dte_ref, dy_ref, st_ref,
             dxs_ref, dbs_ref, dcs_ref, ddt_ref, da_ref, dh_ref):
        d, c = pl.program_id(0), pl.program_id(1)

        @pl.when(c == 0)
        def _():
            dh_ref[...] = jnp.zeros_like(dh_ref)
            da_ref[...] = jnp.zeros_like(da_ref)

        tri, trit, dt, p, pt, tot = _ssd_prelude(d, dt_ref, dtt_ref, ar_ref, ac_ref)
        etot = jnp.exp(tot)
        lane = lax.broadcasted_iota(jnp.int32, (CHUNK, PAIR), 1) < SSD_HEAD_DIM
        rowh = lax.broadcasted_iota(jnp.int32, (PAIR, SSD_STATE), 0) < SSD_HEAD_DIM
        first_head = lax.broadcasted_iota(jnp.int32, (PAIR, LANES), 0) < SSD_HEAD_DIM
        out_lane = lax.broadcasted_iota(jnp.int32, (PAIR, LANES), 1)
        ddtx = jnp.zeros((CHUNK, LANES), F32)
        lane32 = lax.broadcasted_iota(jnp.int32, (CHUNK, SSD_HEADS), 1)
        dp_col = jnp.zeros((CHUNK, SSD_HEADS), F32)
        dp_row = jnp.zeros((SSD_HEADS, CHUNK), F32)
        dtot = jnp.zeros((1, SSD_HEADS), F32)
        for g in range(SSD_GROUPS):
            gs = slice(g * SSD_STATE, (g + 1) * SSD_STATE)
            bg = bs_ref[:, gs]
            bb = bg.astype(BF16)
            cb = cs_ref[:, gs].astype(BF16)
            cbm = _dot_nt(cb, bb)
            dcb = jnp.zeros((CHUNK, CHUNK), F32)
            dc_acc = jnp.zeros((CHUNK, SSD_STATE), F32)
            db_acc = jnp.zeros((CHUNK, SSD_STATE), F32)
            for pr in range(2):
                h0 = g * 4 + pr * 2
                h1 = h0 + 1
                sl = slice(h0 * SSD_HEAD_DIM, h0 * SSD_HEAD_DIM + PAIR)
                xp = xs_ref[:, sl]
                dtp = dte_ref[:, sl]
                xdt_f = xp * dtp
                xdt = xdt_f.astype(BF16)
                dyp = dy_ref[:, sl]
                dyb = dyp.astype(BF16)
                hin = st_ref[sl, :]
                dh = dh_ref[sl, :]
                hb = hin.astype(BF16)
                dhb = dh.astype(BF16)
                heads = [_head_decay(tri, p, pt, tot, h) for h in (h0, h1)]
                dye = dyp * jnp.where(lane, heads[0][2], heads[1][2])
                dyeb = dye.astype(BF16)
                gy = _dot_nt(cb, hb) * dye
                dc_acc = dc_acc + _dot(dyeb, hb)
                dhin = _dot_tn(dyeb, cb)
                hh = dh * hin
                dxdt = jnp.zeros((CHUNK, PAIR), F32)
                for idx, h in enumerate((h0, h1)):
                    hm = lane if idx == 0 else jnp.logical_not(lane)
                    rm = rowh if idx == 0 else jnp.logical_not(rowh)
                    dec, wb, _ = heads[idx]
                    mf = cbm * dec
                    t1 = _dot_tn(mf.astype(BF16), dyb)
                    t2 = _dot_nt((bg * wb).astype(BF16), dhb)
                    dxdt = jnp.where(hm, t1 + t2, dxdt)
                    dm = _dot_nt(jnp.where(hm, dyp, 0.0).astype(BF16), xdt)
                    dcb = dcb + dm * dec
                    e = dm * mf
                    qw = _dot(jnp.where(hm, xdt_f, 0.0).astype(BF16), dhb) * wb
                    db_acc = db_acc + qw
                    qwb = qw * bg
                    col = jnp.sum(e + jnp.where(hm, gy, 0.0) - qwb, axis=1, keepdims=True)
                    dp_col = jnp.where(lane32 == h, col, dp_col)
                    dp_row = _put_row(dp_row, -jnp.sum(e, axis=0, keepdims=True), h)
                    dtot_h = _sum_all(qwb) + etot[:, h:h + 1] * _sum_all(jnp.where(rm, hh, 0.0))
                    dtot = _put_col(dtot, dtot_h, h)
                dxs_ref[:, sl] = dxdt * dtp
                ddx = dxdt * xp
                ddx_hi = ddx.astype(BF16)
                ddx_lo = (ddx - ddx_hi.astype(F32)).astype(BF16)
                route = (out_lane == jnp.where(first_head, h0, h1)).astype(BF16)
                ddtx = ddtx + _dot(ddx_hi, route) + _dot(ddx_lo, route)
                et = jnp.where(rowh, etot[:, h0:h0 + 1], etot[:, h1:h1 + 1])
                dh_ref[sl, :] = dh * et + dhin
            dcbb = dcb.astype(BF16)
            dcs_ref[:, gs] = _dot(dcbb, bb) + dc_acc
            dbs_ref[:, gs] = _dot_tn(dcbb, cb) + db_acc
        d_adt = _dot(trit, dp_col, HI) + _dot_nt(trit, dp_row, HI) + dtot
        ddt_ref[...] = ddtx[:, :SSD_HEADS] + ar_ref[...] * d_adt
        da_ref[...] += jnp.sum(dt * d_adt, axis=0, keepdims=True)

    return pl.pallas_call(
        body, name="ssd_bwd", grid=(2, nc), in_specs=specs,
        out_specs=[pl.BlockSpec((None, CHUNK, D_INNER), lambda d, c: (d, cidx(d, c), 0)),
                   pl.BlockSpec((None, CHUNK, 1024), lambda d, c: (d, cidx(d, c), 0)),
                   pl.BlockSpec((None, CHUNK, 1024), lambda d, c: (d, cidx(d, c), 0)),
                   pl.BlockSpec((None, CHUNK, SSD_HEADS), lambda d, c: (d, cidx(d, c), 0)),
                   pl.BlockSpec((None, 1, SSD_HEADS), lambda d, c: (d, 0, 0))],
        out_shape=[jax.ShapeDtypeStruct((2, t, D_INNER), F32), jax.ShapeDtypeStruct((2, t, 1024), F32),
                   jax.ShapeDtypeStruct((2, t, 1024), F32), jax.ShapeDtypeStruct((2, t, SSD_HEADS), F32),
                   jax.ShapeDtypeStruct((2, 1, SSD_HEADS), F32)],
        scratch_shapes=[pltpu.VMEM((D_INNER, SSD_STATE), F32)],
        compiler_params=_cparams("arbitrary", "arbitrary"),
    )(act, act, act, dt2, dt2t, a_row, a_col, dte, dy, states)


REP = ATTN_HEADS // ATTN_KV
SCALE = ATTN_DIM ** -0.5
GROUP_Q = REP * ATTN_DIM
K_BLK0 = D_MODEL // LANES
V_BLK0 = K_BLK0 + ATTN_KV


def _attn_specs(nb):
    q = pl.BlockSpec((BLOCK, GROUP_Q), lambda g, n: (n, g))

    def kv(blk0):
        return [pl.BlockSpec((BLOCK, LANES), lambda g, n: (jnp.maximum(n - 1, 0), blk0 + g)),
                pl.BlockSpec((BLOCK, LANES), lambda g, n: (n, blk0 + g)),
                pl.BlockSpec((BLOCK, LANES), lambda g, n: (jnp.minimum(n + 1, nb - 1), blk0 + g))]

    bias = pl.BlockSpec((None, REP, BLOCK, 3 * BLOCK), lambda g, n: (g, 0, 0, 0))
    sink = pl.BlockSpec((None, REP, 1, LANES), lambda g, n: (g, 0, 0, 0))
    return q, kv(K_BLK0), kv(V_BLK0), bias, sink


def _band_mask():
    ii = lax.broadcasted_iota(jnp.int32, (BLOCK, 3 * BLOCK), 0)
    jj = lax.broadcasted_iota(jnp.int32, (BLOCK, 3 * BLOCK), 1)
    return (jj >= ii) & (jj - 2 * BLOCK <= ii)


def _attn_valid(n, nb):
    jj = lax.broadcasted_iota(jnp.int32, (1, 3 * BLOCK), 1)
    return ((jj >= BLOCK) | (n > 0)) & ((jj < 2 * BLOCK) | (n < nb - 1))


def _attn_probs(q, kcat, bias, snk, valid):
    s = jnp.where(valid, _dot_nt(q, kcat) + bias, NEG)
    m = jnp.maximum(jnp.max(s, axis=1, keepdims=True), snk)
    p = jnp.exp(s - m)
    es = jnp.exp(snk - m)
    r = 1.0 / (jnp.sum(p, axis=1, keepdims=True) + es)
    return p * r, es * r


def _stack_heads(ref, lane):
    parts = []
    for pr in range(REP // 2):
        tile = ref[:, pr * LANES:(pr + 1) * LANES]
        parts += [jnp.where(lane, tile, 0.0), jnp.where(lane, 0.0, tile)]
    return jnp.concatenate(parts, axis=0)


def _unstack_heads(x, lane):
    return jnp.concatenate([jnp.where(lane, x[(2 * pr) * BLOCK:(2 * pr + 1) * BLOCK], x[(2 * pr + 1) * BLOCK:(2 * pr + 2) * BLOCK])
                            for pr in range(REP // 2)], axis=1)


def _stack_bias(b_ref, s_ref):
    bias = jnp.concatenate([b_ref[r] for r in range(REP)], axis=0)
    snk = jnp.concatenate([jnp.broadcast_to(s_ref[r][:, 0:1], (BLOCK, 1)) for r in range(REP)], axis=0)
    return bias, snk


def _attn_fwd(qkv, bias4, sink4):
    t = qkv.shape[0]
    nb = t // BLOCK
    qs, ks, vs, bs, ss = _attn_specs(nb)

    def body(q_ref, kp_ref, kc_ref, kn_ref, vp_ref, vc_ref, vn_ref, b_ref, s_ref, o_ref):
        n = pl.program_id(1)
        kcat = jnp.concatenate([kp_ref[...], kc_ref[...], kn_ref[...]], axis=0)
        vcat = jnp.concatenate([vp_ref[...], vc_ref[...], vn_ref[...]], axis=0)
        valid = _attn_valid(n, nb)
        lane = lax.broadcasted_iota(jnp.int32, (BLOCK, LANES), 1) < ATTN_DIM
        bias, snk = _stack_bias(b_ref, s_ref)
        pn, _ = _attn_probs(_stack_heads(q_ref, lane) * SCALE, kcat, bias, snk, valid)
        o_ref[...] = _unstack_heads(_dot(pn.astype(BF16), vcat), lane).astype(o_ref.dtype)

    return pl.pallas_call(
        body, name="attn_fwd", grid=(ATTN_KV, nb), in_specs=[qs] + ks + vs + [bs, ss],
        out_specs=qs, out_shape=jax.ShapeDtypeStruct((t, D_MODEL), BF16),
        compiler_params=_cparams("parallel", "arbitrary"),
    )(qkv, qkv, qkv, qkv, qkv, qkv, qkv, bias4, sink4)


def _attn_bwd(qkv, bias4, sink4, do):
    t = qkv.shape[0]
    nb = t // BLOCK
    qs, ks, vs, bs, ss = _attn_specs(nb)
    part = pl.BlockSpec((3, BLOCK, LANES), lambda g, n: (0, n, g))

    def body(q_ref, kp_ref, kc_ref, kn_ref, vp_ref, vc_ref, vn_ref, b_ref, s_ref, do_ref,
             dq_ref, dk_ref, dv_ref, db_ref, ds_ref):
        n = pl.program_id(1)

        @pl.when(n == 0)
        def _():
            db_ref[...] = jnp.zeros_like(db_ref)
            ds_ref[...] = jnp.zeros_like(ds_ref)

        kcat = jnp.concatenate([kp_ref[...], kc_ref[...], kn_ref[...]], axis=0)
        vcat = jnp.concatenate([vp_ref[...], vc_ref[...], vn_ref[...]], axis=0)
        valid = _attn_valid(n, nb)
        lane = lax.broadcasted_iota(jnp.int32, (BLOCK, LANES), 1) < ATTN_DIM
        bias, snk = _stack_bias(b_ref, s_ref)
        q = _stack_heads(q_ref, lane)
        do = _stack_heads(do_ref, lane)
        pn, psink = _attn_probs(q * SCALE, kcat, bias, snk, valid)
        dp = _dot_nt(do, vcat)
        delta = jnp.sum(pn * dp, axis=1, keepdims=True)
        dsc = pn * (dp - delta)
        dsink = psink * delta
        for r in range(REP):
            rows = slice(r * BLOCK, (r + 1) * BLOCK)
            db_ref[r] += dsc[rows]
            ds_ref[r] += jnp.broadcast_to(-jnp.sum(dsink[rows], axis=0, keepdims=True), (1, LANES))
        dsb = (dsc * SCALE).astype(BF16)
        dq_ref[...] = _unstack_heads(_dot(dsb, kcat), lane).astype(dq_ref.dtype)
        dk = _dot_tn(dsb, q)
        dv = _dot_tn(pn.astype(BF16), do)
        for j in range(3):
            dk_ref[j] = dk[j * BLOCK:(j + 1) * BLOCK]
            dv_ref[j] = dv[j * BLOCK:(j + 1) * BLOCK]

    kv_cols = ATTN_KV * LANES
    return pl.pallas_call(
        body, name="attn_bwd", grid=(ATTN_KV, nb), in_specs=[qs] + ks + vs + [bs, ss, qs],
        out_specs=[qs, part, part, bs, ss],
        out_shape=[jax.ShapeDtypeStruct((t, D_MODEL), BF16), jax.ShapeDtypeStruct((3, t, kv_cols), F32),
                   jax.ShapeDtypeStruct((3, t, kv_cols), F32), jax.ShapeDtypeStruct(bias4.shape, F32),
                   jax.ShapeDtypeStruct(sink4.shape, F32)],
        compiler_params=_cparams("parallel", "arbitrary"),
    )(qkv, qkv, qkv, qkv, qkv, qkv, qkv, bias4, sink4, do)


def _kv_combine(name, parts):
    _, t, cols = parts.shape

    def body(p_ref, o_ref):
        z = jnp.zeros((BLOCK, LANES), F32)
        from_next = jnp.concatenate([p_ref[0, BLOCK:, :], z], axis=0)
        from_prev = jnp.concatenate([z, p_ref[2, :t - BLOCK, :]], axis=0)
        o_ref[...] = (from_next + p_ref[1] + from_prev).astype(o_ref.dtype)

    return pl.pallas_call(
        body, name=name, grid=(cols // LANES,),
        in_specs=[pl.BlockSpec((3, t, LANES), lambda g: (0, 0, g))],
        out_specs=pl.BlockSpec((t, LANES), lambda g: (0, g)),
        out_shape=jax.ShapeDtypeStruct((t, cols), BF16),
        compiler_params=_cparams("parallel"),
    )(parts)


def _t5_bucket(rel):
    nb = N_BUCKETS // 2
    max_exact = nb // 2
    ret = jnp.where(rel > 0, nb, 0)
    n = jnp.abs(rel)
    nf = jnp.maximum(n, 1).astype(jnp.float32)
    large = max_exact + (jnp.log(nf / max_exact) / math.log(MAX_DISTANCE / max_exact) * (nb - max_exact)).astype(jnp.int32)
    large = jnp.minimum(large, nb - 1)
    return ret + jnp.where(n < max_exact, n, large)


def _bucket_map():
    i = jnp.arange(BLOCK)[:, None]
    j = jnp.arange(3 * BLOCK)[None, :]
    return _t5_bucket(j - BLOCK - i)


def _bias_from_table(table, onehot_t):
    def body(t_ref, o_ref, out_ref):
        out_ref[...] = _dot(t_ref[...], o_ref[...], HI)

    return pl.pallas_call(body, name="bias_from_table", out_shape=jax.ShapeDtypeStruct((ATTN_HEADS, onehot_t.shape[1]), F32),
                          compiler_params=pltpu.CompilerParams(vmem_limit_bytes=VMEM_LIMIT_BYTES))(table.T, onehot_t)


def _bias_table_grad(dbias, onehot_t):
    def body(d_ref, o_ref, out_ref):
        out_ref[...] = _dot_nt(d_ref[...], o_ref[...], HI)

    return pl.pallas_call(body, name="bias_table_grad", out_shape=jax.ShapeDtypeStruct((ATTN_HEADS, N_BUCKETS), F32),
                          compiler_params=pltpu.CompilerParams(vmem_limit_bytes=VMEM_LIMIT_BYTES))(dbias, onehot_t)


HBM_SPEC = pl.BlockSpec(memory_space=pl.ANY)


def _allgather_chips(name, x):
    r2 = x.shape[0] // 2

    def body(x_ref, out_ref, send_sems, recv_sems, local_sem):
        mx, my, mc = lax.axis_index("x"), lax.axis_index("y"), lax.axis_index("c")
        me = 2 * mx + my
        chips = [(1 - mx, my), (mx, 1 - my), (1 - mx, 1 - my)]
        sibling = (mx, my, 1 - mc)

        def part(slot, h):
            return out_ref.at[slot, pl.ds(h * r2, r2)]

        def copy(k, src, dst, to):
            return pltpu.make_async_remote_copy(src_ref=src, dst_ref=dst, send_sem=send_sems.at[k], recv_sem=recv_sems.at[k],
                                                device_id=to, device_id_type=MESH)

        mine = pltpu.make_async_copy(x_ref, out_ref.at[me], local_sem)
        mine.start()
        first = [copy(k, x_ref.at[pl.ds(mc * r2, r2)], part(me, mc), (px, py, mc)) for k, (px, py) in enumerate(chips)]
        for cp in first:
            cp.start()
        passed = [copy(3 + k, part(2 * px + py, mc), part(2 * px + py, mc), sibling) for k, (px, py) in enumerate(chips)]
        for k, (px, py) in enumerate(chips):
            copy(k, part(2 * px + py, mc), part(2 * px + py, mc), (px, py, mc)).wait_recv()
            passed[k].start()
        for k, (px, py) in enumerate(chips):
            copy(3 + k, part(2 * px + py, 1 - mc), part(2 * px + py, 1 - mc), sibling).wait_recv()
        for cp in first + passed:
            cp.wait_send()
        mine.wait()

    return pl.pallas_call(
        body, name=name, in_specs=[HBM_SPEC], out_specs=HBM_SPEC,
        out_shape=jax.ShapeDtypeStruct((4,) + x.shape, x.dtype),
        scratch_shapes=[pltpu.SemaphoreType.DMA((6,)), pltpu.SemaphoreType.DMA((6,)), pltpu.SemaphoreType.DMA],
    )(x)


def _swap_cores(name, x):
    def body(x_ref, out_ref, send_sem, recv_sem):
        mx, my, mc = lax.axis_index("x"), lax.axis_index("y"), lax.axis_index("c")
        send = pltpu.make_async_remote_copy(src_ref=x_ref.at[1 - mc], dst_ref=out_ref, send_sem=send_sem, recv_sem=recv_sem,
                                            device_id=(mx, my, 1 - mc), device_id_type=MESH)
        send.start()
        send.wait()

    return pl.pallas_call(
        body, name=name, in_specs=[HBM_SPEC], out_specs=HBM_SPEC,
        out_shape=jax.ShapeDtypeStruct(x.shape[1:], x.dtype),
        scratch_shapes=[pltpu.SemaphoreType.DMA, pltpu.SemaphoreType.DMA],
    )(x)


def _scatter_chips(name, g):
    def body(g_ref, out_ref, send_sems, recv_sems, local_sem):
        mx, my, mc = lax.axis_index("x"), lax.axis_index("y"), lax.axis_index("c")
        me = 2 * mx + my
        chips = [(1 - mx, my), (mx, 1 - my), (1 - mx, 1 - my)]
        mine = pltpu.make_async_copy(g_ref.at[me], out_ref.at[me], local_sem)
        mine.start()
        sends = [pltpu.make_async_remote_copy(src_ref=g_ref.at[2 * px + py], dst_ref=out_ref.at[me], send_sem=send_sems.at[k],
                                              recv_sem=recv_sems.at[k], device_id=(px, py, mc), device_id_type=MESH)
                 for k, (px, py) in enumerate(chips)]
        for cp in sends:
            cp.start()
        for k, (px, py) in enumerate(chips):
            pltpu.make_async_remote_copy(src_ref=g_ref.at[0], dst_ref=out_ref.at[2 * px + py], send_sem=send_sems.at[k],
                                         recv_sem=recv_sems.at[k], device_id=(px, py, mc), device_id_type=MESH).wait_recv()
        for cp in sends:
            cp.wait_send()
        mine.wait()

    return pl.pallas_call(
        body, name=name, in_specs=[HBM_SPEC], out_specs=HBM_SPEC,
        out_shape=jax.ShapeDtypeStruct(g.shape, g.dtype),
        scratch_shapes=[pltpu.SemaphoreType.DMA((3,)), pltpu.SemaphoreType.DMA((3,)), pltpu.SemaphoreType.DMA],
    )(g)


def _allgather_all(name, x):
    def body(x_ref, out_ref, send_sems, recv_sems, local_sem):
        mx, my, mc = lax.axis_index("x"), lax.axis_index("y"), lax.axis_index("c")
        me = 4 * mx + 2 * my + mc
        flips = [(fx, fy, fc) for fx in (0, 1) for fy in (0, 1) for fc in (0, 1)][1:]
        peers = [(mx ^ fx, my ^ fy, mc ^ fc) for fx, fy, fc in flips]
        mine = pltpu.make_async_copy(x_ref, out_ref.at[me], local_sem)
        mine.start()
        sends = [pltpu.make_async_remote_copy(src_ref=x_ref, dst_ref=out_ref.at[me], send_sem=send_sems.at[k],
                                              recv_sem=recv_sems.at[k], device_id=peer, device_id_type=MESH)
                 for k, peer in enumerate(peers)]
        for cp in sends:
            cp.start()
        for k, (px, py, pc) in enumerate(peers):
            pltpu.make_async_remote_copy(src_ref=x_ref, dst_ref=out_ref.at[4 * px + 2 * py + pc], send_sem=send_sems.at[k],
                                         recv_sem=recv_sems.at[k], device_id=(px, py, pc), device_id_type=MESH).wait_recv()
        for cp in sends:
            cp.wait_send()
        mine.wait()

    return pl.pallas_call(
        body, name=name, in_specs=[HBM_SPEC], out_specs=HBM_SPEC,
        out_shape=jax.ShapeDtypeStruct((8,) + x.shape, x.dtype),
        scratch_shapes=[pltpu.SemaphoreType.DMA((7,)), pltpu.SemaphoreType.DMA((7,)), pltpu.SemaphoreType.DMA],
    )(x)


def _allgather_cores(name, x):
    def body(x_ref, out_ref, send_sem, recv_sem, local_sem):
        mx, my, mc = lax.axis_index("x"), lax.axis_index("y"), lax.axis_index("c")
        mine = pltpu.make_async_copy(x_ref, out_ref.at[mc], local_sem)
        mine.start()
        send = pltpu.make_async_remote_copy(src_ref=x_ref, dst_ref=out_ref.at[mc], send_sem=send_sem, recv_sem=recv_sem,
                                            device_id=(mx, my, 1 - mc), device_id_type=MESH)
        send.start()
        pltpu.make_async_remote_copy(src_ref=x_ref, dst_ref=out_ref.at[1 - mc], send_sem=send_sem, recv_sem=recv_sem,
                                     device_id=(mx, my, 1 - mc), device_id_type=MESH).wait_recv()
        send.wait_send()
        mine.wait()

    return pl.pallas_call(
        body, name=name, in_specs=[HBM_SPEC], out_specs=HBM_SPEC,
        out_shape=jax.ShapeDtypeStruct((2,) + x.shape, x.dtype),
        scratch_shapes=[pltpu.SemaphoreType.DMA, pltpu.SemaphoreType.DMA, pltpu.SemaphoreType.DMA],
    )(x)


def _sum_slots(name, st, tb=256):
    n, r, c = st.shape
    tb = _pick(r, (tb, 32))

    def body(s_ref, o_ref):
        acc = s_ref[0].astype(F32)
        for k in range(1, n):
            acc = acc + s_ref[k].astype(F32)
        o_ref[...] = acc

    return pl.pallas_call(
        body, name=name, grid=(r // tb,), in_specs=[pl.BlockSpec((n, tb, c), lambda i: (0, i, 0))],
        out_specs=pl.BlockSpec((tb, c), lambda i: (i, 0)), out_shape=jax.ShapeDtypeStruct((r, c), F32),
        compiler_params=_cparams("parallel"),
    )(st)


def _adamw(name, w, g, m, v):
    def fn(w, g, m, v):
        m2 = ADAM_B1 * m + (1.0 - ADAM_B1) * g
        v2 = ADAM_B2 * v + (1.0 - ADAM_B2) * jnp.square(g)
        m_hat = m2 / (1.0 - ADAM_B1 ** ADAM_STEP)
        v_hat = v2 / (1.0 - ADAM_B2 ** ADAM_STEP)
        delta = -ADAM_LR * (m_hat / (jnp.sqrt(v_hat) + ADAM_EPS) + ADAM_WD * w)
        return [delta, m2, v2], []

    tb = _pick(w.shape[0], (256, 32))
    return _rowwise(name, fn, [_row(w), _row(g), _row(m), _row(v)], [], [(w.shape[1], F32)] * 3, tb=tb)


BIG = ("w_ssd_out", "w_attn_out", "w_o", "w_mlp_in", "w_mlp_out", "conv_w")
SMALL = ("pre_mix_norm", "b_gate", "conv_b", "dt_bias", "a_log", "d_skip", "ssd_norm", "attn_sink", "rel_bias_table",
         "post_mix_norm", "pre_mlp_norm", "post_mlp_norm")
ALL_W = ("pre_mix_norm", "w_in", "b_gate", "conv_w", "conv_b", "dt_bias", "a_log", "d_skip", "ssd_norm", "w_ssd_out",
         "attn_sink", "rel_bias_table", "w_attn_out", "w_o", "post_mix_norm", "pre_mlp_norm", "w_mlp_in", "w_mlp_out",
         "post_mlp_norm")


def _pack_rows(parts, rows, dtype):
    flat = jnp.concatenate([p.reshape(-1, D_MODEL).astype(dtype) for p in parts], axis=0)
    return jnp.pad(flat, ((0, rows - flat.shape[0]), (0, 0)))


def _pack_big(shards, dtype):
    return _pack_rows([shards[n] for n in BIG], BIG_ROWS, dtype)


def _unpack_big(flat, like):
    out, r = {}, 0
    for n in BIG:
        shp = like[n].shape
        nr = math.prod(shp) // D_MODEL
        out[n] = flat[r:r + nr].reshape(shp)
        r += nr
    return out


def _pack_small(parts, extra=None):
    flat = jnp.concatenate([parts[n].reshape(-1).astype(F32) for n in SMALL] + ([extra.reshape(-1)] if extra is not None else []))
    return jnp.pad(flat, (0, SMALL_ROWS * D_MODEL - flat.shape[0])).reshape(SMALL_ROWS, D_MODEL)


def _unpack_small(flat2, like):
    flat = flat2.reshape(-1)
    out, r = {}, 0
    for n in SMALL:
        shp = like[n].shape
        k = math.prod(shp)
        out[n] = flat[r:r + k].reshape(shp)
        r += k
    return out, flat[r]


def _shard_of_full(name, full, s):
    if name in ("w_in", "w_mlp_in"):
        w = full.shape[2] // 4
        return full[:, :, s * w:(s + 1) * w]
    if name == "conv_w":
        w = full.shape[3] // 4
        return full[:, :, :, s * w:(s + 1) * w]
    w = full.shape[1] // 4
    return full[:, s * w:(s + 1) * w, :]


def _full_of_shards(name, shards):
    axis = {"w_in": 2, "w_mlp_in": 2, "conv_w": 3}.get(name, 1)
    return jnp.concatenate(shards, axis=axis)


def _to_proj_layout(w):
    z, xbc, dt, q, k, v, gates = (w[..., 0:2048], w[..., 2048:6144], w[..., 6144:6208], w[..., 6208:7232],
                                  w[..., 7232:7488], w[..., 7488:7744], w[..., 7744:9792])
    pad = jnp.zeros(w.shape[:-1] + (N_MAIN - OFF_DT - dt.shape[-1],), w.dtype)

    def doubled(a):
        h = a.reshape(a.shape[:-1] + (ATTN_KV, 1, ATTN_DIM))
        return jnp.broadcast_to(h, a.shape[:-1] + (ATTN_KV, 2, ATTN_DIM)).reshape(a.shape[:-1] + (2 * a.shape[-1],))

    return jnp.concatenate([z, gates, xbc, dt, pad, q, doubled(k), doubled(v)], axis=-1)


def _from_proj_layout(w):
    z, gates, xbc, dt, q, k2, v2 = (w[..., 0:2048], w[..., 2048:4096], w[..., 4096:8192], w[..., 8192:8256],
                                    w[..., 8320:9344], w[..., 9344:9856], w[..., 9856:10368])

    def folded(a):
        return a.reshape(a.shape[:-1] + (ATTN_KV, 2, ATTN_DIM)).sum(axis=-2).reshape(a.shape[:-1] + (a.shape[-1] // 2,))

    return jnp.concatenate([z, xbc, dt, q, folded(k2), folded(v2), gates], axis=-1)


def _layer_fwd(h1, x, W, P, l, bias4):
    t = x.shape[0]
    S = {"x": x, "h1": h1}
    proj = _matmul("proj", h1, W["w_in_main"][l], "nn")
    qkv = _matmul("proj_qkv", h1, W["w_in_qkv"][l], "nn", out_dtype=BF16)
    S["proj"] = proj
    pre, act = _conv_fwd(proj, W["conv_w"][l], P["conv_b"][l].reshape(1, CONV_DIM))
    S["pre"], S["act"] = pre, act

    dtb = jnp.pad(P["dt_bias"][l].reshape(1, 2 * SSD_HEADS), ((0, 0), (0, LANES - 2 * SSD_HEADS)))

    def dt_fn(raw, b):
        v = raw + b
        return [jnp.maximum(v, 0.0) + jnp.log1p(jnp.exp(-jnp.abs(v)))], []

    (dt,) = _rowwise("dt_fwd", dt_fn, [_row(proj, LANES, OFF_DT // LANES)], [dtb], [(LANES, F32)], tb=512)
    dt2 = jnp.stack([dt[:, 0:SSD_HEADS], dt[:, SSD_HEADS:2 * SSD_HEADS]])
    dt2t = dt2.transpose(0, 2, 1)
    a = -jnp.exp(P["a_log"][l])
    a_row, a_col = a.reshape(2, 1, SSD_HEADS), a.reshape(2, SSD_HEADS, 1)
    dte = jnp.repeat(dt2, SSD_HEAD_DIM, axis=-1)
    S["dt2"], S["dt2t"], S["a_row"], S["a_col"], S["dte"] = dt2, dt2t, a_row, a_col, dte
    y2, states = _ssd_fwd(act, dt2, dt2t, a_row, a_col, dte)
    S["states"] = states

    dsk = jnp.repeat(P["d_skip"][l], SSD_HEAD_DIM).reshape(1, D_INNER)
    nw = P["ssd_norm"][l].reshape(1, D_INNER)
    S["dsk"], S["nw"] = dsk, nw

    def gn_fn(yf, yb, xs, z, dsk, nw):
        y = yf + yb + xs * dsk
        return [y, _gated_norm_fwd(y, z, nw)], []

    y, yn = _rowwise("gated_norm_fwd", gn_fn,
                     [_row(y2, lead=0), _row(y2, lead=1), _row(act, D_INNER, 0), _row(proj, D_INNER, OFF_Z // D_INNER)],
                     [dsk, nw], [(D_INNER, F32), (D_INNER, BF16)])
    S["y"], S["yn"] = y, yn
    y_ssd = _matmul("ssd_out", yn, W["w_ssd_out"][l], "nn")
    S["y_ssd"] = y_ssd

    sink4 = jnp.broadcast_to(P["attn_sink"][l].reshape(ATTN_KV, REP, 1, 1), (ATTN_KV, REP, 1, LANES))
    S["qkv"], S["sink4"] = qkv, sink4
    o = _attn_fwd(qkv, bias4, sink4)
    S["o"] = o
    y_attn = _matmul("attn_out", o, W["w_attn_out"][l], "nn")
    S["y_attn"] = y_attn

    bg = P["b_gate"][l].reshape(1, 2 * D_MODEL)
    S["bg"] = bg

    def merge_fn(gates, ys, ya, b):
        g = jax.nn.sigmoid(gates + b)
        return [g[:, :D_MODEL] * ys + g[:, D_MODEL:] * ya], []

    (mix_in,) = _rowwise("merge_fwd", merge_fn, [_row(proj, 2 * D_MODEL, OFF_G // (2 * D_MODEL)), _row(y_ssd), _row(y_attn)],
                         [bg], [(D_MODEL, BF16)])
    S["mix_in"] = mix_in
    mixed = _matmul("w_o", mix_in, W["w_o"][l], "nn")
    S["mixed"] = mixed

    g_pm = P["post_mix_norm"][l].reshape(1, D_MODEL)
    g_pl = P["pre_mlp_norm"][l].reshape(1, D_MODEL)

    def postmix_fn(x, mixed, g1, g2):
        x2 = x + _rms_fwd(mixed, g1)
        return [x2, _rms_fwd(x2, g2)], []

    x2, h2 = _rowwise("post_mix_fwd", postmix_fn, [_row(x), _row(mixed)], [g_pm, g_pl], [(D_MODEL, F32), (D_MODEL, BF16)])
    S["x2"], S["h2"] = x2, h2
    a1 = _matmul("mlp_in", h2, W["w_mlp_in"][l], "nn", out_dtype=BF16,
                 epilogue=lambda acc: jnp.square(jnp.maximum(acc, 0.0)))
    S["a1"] = a1
    f2 = _matmul("mlp_out", a1, W["w_mlp_out"][l], "nn")
    S["f2"] = f2
    return S


def _layer_bwd(S, dx3, W, P, l, bias4, onehot_t):
    t = dx3.shape[0]
    G = {}
    g_pmlp = P["post_mlp_norm"][l].reshape(1, D_MODEL)

    def b1_fn(f2, dx3, g):
        df2, dg = _rms_bwd(f2, g, dx3)
        return [df2], [dg]

    df2, G["post_mlp_norm"] = _rowwise("post_mlp_bwd", b1_fn, [_row(S["f2"]), _row(dx3)], [g_pmlp], [(D_MODEL, BF16)], [(1, D_MODEL)])
    df1 = _matmul("d_f1", df2, W["w_mlp_out"][l], "nt", out_dtype=BF16,
                  epilogue=lambda acc, a1: acc * (2.0 * jnp.sqrt(a1.astype(F32))), extras=[S["a1"]])
    G["w_mlp_out"] = _matmul("dw_mlp_out", S["a1"], df2, "tn")
    dh2 = _matmul("d_h2", df1, W["w_mlp_in"][l], "nt")
    G["w_mlp_in"] = _matmul("dw_mlp_in", S["h2"], df1, "tn")

    g_pm = P["post_mix_norm"][l].reshape(1, D_MODEL)
    g_pl = P["pre_mlp_norm"][l].reshape(1, D_MODEL)

    def b3_fn(x2, dh2, dx3, mixed, g_pl, g_pm):
        d1, dgl = _rms_bwd(x2, g_pl, dh2)
        dx2 = dx3 + d1
        dmixed, dgm = _rms_bwd(mixed, g_pm, dx2)
        return [dx2, dmixed], [dgl, dgm]

    dx2, dmixed, G["pre_mlp_norm"], G["post_mix_norm"] = _rowwise(
        "post_mix_bwd", b3_fn, [_row(S["x2"]), _row(dh2), _row(dx3), _row(S["mixed"])], [g_pl, g_pm],
        [(D_MODEL, F32), (D_MODEL, BF16)], [(1, D_MODEL), (1, D_MODEL)])
    dmix_in = _matmul("d_mix_in", dmixed, W["w_o"][l], "nt")
    G["w_o"] = _matmul("dw_o", S["mix_in"], dmixed, "tn")

    proj = S["proj"]

    def b4_fn(gates, ys, ya, dmix, b):
        g = jax.nn.sigmoid(gates + b)
        gs, ga = g[:, :D_MODEL], g[:, D_MODEL:]
        dg = jnp.concatenate([ys * dmix, ya * dmix], axis=-1) * g * (1.0 - g)
        return [gs * dmix, ga * dmix, dg], [jnp.sum(dg, axis=0, keepdims=True)]

    dy_ssd, dy_attn, dgates, G["b_gate"] = _rowwise(
        "merge_bwd", b4_fn, [_row(proj, 2 * D_MODEL, OFF_G // (2 * D_MODEL)), _row(S["y_ssd"]), _row(S["y_attn"]), _row(dmix_in)],
        [S["bg"]], [(D_MODEL, BF16), (D_MODEL, BF16), (2 * D_MODEL, BF16)], [(1, 2 * D_MODEL)])

    dyn = _matmul("d_yn", dy_ssd, W["w_ssd_out"][l], "nt")
    G["w_ssd_out"] = _matmul("dw_ssd_out", S["yn"], dy_ssd, "tn")
    do = _matmul("d_o", dy_attn, W["w_attn_out"][l], "nt", out_dtype=BF16)
    G["w_attn_out"] = _matmul("dw_attn_out", S["o"], dy_attn, "tn")

    dq, dkp, dvp, dbias4, dsink4 = _attn_bwd(S["qkv"], bias4, S["sink4"], do)
    dk = _kv_combine("dk_combine", dkp)
    dv = _kv_combine("dv_combine", dvp)
    G["attn_sink"] = dsink4[:, :, 0, 0].reshape(ATTN_HEADS)
    G["rel_bias_table"] = _bias_table_grad(dbias4.reshape(ATTN_HEADS, BLOCK * 3 * BLOCK), onehot_t).T

    act = S["act"]

    def b5_fn(y, z, xs, dyn, nw, dsk):
        dy, dz, dnw = _gated_norm_bwd(y, z, nw, dyn)
        return [dy, dz], [dnw, jnp.sum(dy * xs, axis=0, keepdims=True)]

    dy, dz, G["ssd_norm"], dskip_cols = _rowwise(
        "gated_norm_bwd", b5_fn, [_row(S["y"]), _row(proj, D_INNER, OFF_Z // D_INNER), _row(act, D_INNER, 0), _row(dyn)],
        [S["nw"], S["dsk"]], [(D_INNER, F32), (D_INNER, BF16)], [(1, D_INNER), (1, D_INNER)])
    G["d_skip"] = dskip_cols.reshape(SSD_HEADS, SSD_HEAD_DIM).sum(axis=-1)

    dxs2, dbs2, dcs2, ddt2, da2 = _ssd_bwd(act, S["dt2"], S["dt2t"], S["a_row"], S["a_col"], S["dte"], dy, S["states"])
    G["a_log"] = da2.reshape(2, SSD_HEADS) * S["a_row"].reshape(2, SSD_HEADS)

    def b6_fn(dxf, dxb, dy, dbf, dbb, dcf, dcb, pre, dsk):
        dact = jnp.concatenate([dxf + dxb + dy * dsk, dbf + dbb, dcf + dcb], axis=-1)
        return [dact * _silu_grad(pre)], []

    (dpre,) = _rowwise("silu_bwd", b6_fn,
                       [_row(dxs2, lead=0), _row(dxs2, lead=1), _row(dy), _row(dbs2, lead=0), _row(dbs2, lead=1),
                        _row(dcs2, lead=0), _row(dcs2, lead=1), _row(S["pre"])], [S["dsk"]], [(CONV_DIM, F32)], tb=128)
    du, dconv_w, dconv_b = _conv_bwd(dpre, proj, W["conv_w"][l])
    G["conv_w"] = dconv_w.reshape(SSD_CONV, 1, CONV_DIM)
    G["conv_b"] = dconv_b.reshape(CONV_DIM)

    dtb = jnp.pad(P["dt_bias"][l].reshape(1, 2 * SSD_HEADS), ((0, 0), (0, LANES - 2 * SSD_HEADS)))
    ddt = jnp.pad(jnp.concatenate([ddt2[0], ddt2[1]], axis=-1), ((0, 0), (0, LANES - 2 * SSD_HEADS)))

    def b7_fn(raw, ddt, b):
        draw = ddt * jax.nn.sigmoid(raw + b)
        return [draw], [jnp.sum(draw, axis=0, keepdims=True)]

    draw, ddtb = _rowwise("dt_bwd", b7_fn, [_row(proj, LANES, OFF_DT // LANES), _row(ddt)], [dtb], [(LANES, BF16)], [(1, LANES)], tb=512)
    G["dt_bias"] = ddtb[0, :2 * SSD_HEADS].reshape(2, SSD_HEADS)

    dproj = jnp.concatenate([dz, dgates, du, draw, dq, dk, dv], axis=-1)
    G["w_in"] = _from_proj_layout(_matmul("dw_in", S["h1"], dproj, "tn"))
    dh1 = _matmul("d_h1", dproj, W["w_in"][l], "nt")

    g_pre = P["pre_mix_norm"][l].reshape(1, D_MODEL)

    def b8_fn(x, dh1, dx2, g):
        d1, dg = _rms_bwd(x, g, dh1)
        return [dx2 + d1], [dg]

    dx, G["pre_mix_norm"] = _rowwise("pre_mix_bwd", b8_fn, [_row(S["x"]), _row(dh1), _row(dx2)], [g_pre], [(D_MODEL, F32)], [(1, D_MODEL)])
    return dx, G


def _reduce_scatter(tag, pieces):
    _, _, r2, c = pieces.shape
    theirs = _swap_cores("swap_" + tag, pieces).reshape(4 * r2, c)
    mine = lax.dynamic_index_in_dim(pieces, lax.axis_index("c"), axis=0, keepdims=False).reshape(4 * r2, c)
    (chip_sum,) = _rowwise("presum_" + tag, lambda a, b: ([a.astype(F32) + b.astype(F32)], []), [_row(mine), _row(theirs)], [],
                           [(c, BF16)])
    staged = _scatter_chips("scatter_" + tag, chip_sum.reshape(4, r2, c))
    half = _sum_slots("sum_" + tag, staged)
    return _allgather_cores("share_" + tag, half).reshape(2 * r2, c)


def _step(x, target, shard_w, shard_m, shard_v):
    depth, _, win_cols = shard_w["w_in"].shape
    win_rows = depth * D_MODEL

    def win2d(a):
        return a.reshape(win_rows, win_cols)

    win_g = _allgather_chips("gather_w_in", win2d(shard_w["w_in"]).astype(BF16))
    gathered = _allgather_chips("gather_weights", _pack_big(shard_w, BF16))
    conv_rows = shard_w["conv_w"].reshape(-1, D_MODEL)
    conv_g = _allgather_chips("gather_conv", jnp.pad(conv_rows, ((0, 16 - conv_rows.shape[0]), (0, 0))))
    per_chip = [_unpack_big(gathered[s], shard_w) for s in range(4)]
    W = {n: _full_of_shards(n, [per_chip[s][n] for s in range(4)]) for n in BIG if n != "conv_w"}
    W["w_in"] = _to_proj_layout(jnp.concatenate([win_g[s].reshape(depth, D_MODEL, win_cols) for s in range(4)], axis=2))
    W["conv_w"] = jnp.concatenate([conv_g[s][:conv_rows.shape[0]].reshape(shard_w["conv_w"].shape) for s in range(4)],
                                  axis=3).reshape(depth, SSD_CONV, CONV_DIM)
    P = {n: shard_w[n] for n in SMALL}
    loss_part, grad_x, full = _local_step(x, target, W, P)

    assert depth == 2
    win_pieces = jnp.stack([jnp.stack([full["w_in"][h][:, s * win_cols:(s + 1) * win_cols] for s in range(4)])
                            for h in range(2)]).astype(BF16)
    g_win = _reduce_scatter("w_in", win_pieces)
    d_win, m_win, v_win = _adamw("adamw_w_in", win2d(shard_w["w_in"]), g_win, win2d(shard_m["w_in"]), win2d(shard_v["w_in"]))
    packs = [_pack_big({n: _shard_of_full(n, full[n], s) for n in BIG}, BF16) for s in range(4)]
    r2 = BIG_ROWS // 2
    pieces = jnp.stack([jnp.stack([p[h * r2:(h + 1) * r2] for p in packs]) for h in range(2)])
    g_big = _reduce_scatter("grads", pieces)
    d_big, m_big, v_big = _adamw("adamw_big", _pack_big(shard_w, F32), g_big, _pack_big(shard_m, F32), _pack_big(shard_v, F32))

    small = _allgather_all("gather_small", _pack_small(full, loss_part))
    g_small = _sum_slots("sum_small", small, tb=SMALL_ROWS)
    d_small, m_small, v_small = _adamw("adamw_small", _pack_small(shard_w, jnp.zeros((), F32)), g_small,
                                       _pack_small(shard_m, jnp.zeros((), F32)), _pack_small(shard_v, jnp.zeros((), F32)))

    outs = {}
    for tag, win, big, sm in (("grad", g_win, g_big, g_small), ("delta", d_win, d_big, d_small),
                              ("new_m", m_win, m_big, m_small), ("new_v", v_win, v_big, v_small)):
        ub = _unpack_big(big, shard_w)
        us, extra = _unpack_small(sm, shard_w)
        outs[tag] = {"w_in": win.reshape(shard_w["w_in"].shape), **ub, **us}
        if tag == "grad":
            loss = extra
    return loss, grad_x, outs


def _local_step(x, target, W, P):
    depth = W["w_in"].shape[0]
    W = dict(W, w_in_main=W["w_in"][:, :, :N_MAIN], w_in_qkv=W["w_in"][:, :, N_MAIN:])
    onehot_t = (_bucket_map().reshape(1, -1) == jnp.arange(N_BUCKETS)[:, None]).astype(F32)
    bias = _bias_from_table(P["rel_bias_table"], onehot_t).reshape(ATTN_HEADS, BLOCK, 3 * BLOCK)
    bias4 = jnp.where(_band_mask(), bias, NEG).reshape(ATTN_KV, REP, BLOCK, 3 * BLOCK)

    def pre_fn(x, g):
        return [_rms_fwd(x, g)], []

    (h1,) = _rowwise("pre_mix_fwd", pre_fn, [_row(x)], [P["pre_mix_norm"][0].reshape(1, D_MODEL)], [(D_MODEL, BF16)])
    saved = []
    loss_cols = dxl = None
    for l in range(depth):
        S = _layer_fwd(h1, x, W, P, l, bias4)
        saved.append(S)
        g_pmlp = P["post_mlp_norm"][l].reshape(1, D_MODEL)
        if l + 1 < depth:
            def post_fn(x2, f2, g1, g2):
                x3 = x2 + _rms_fwd(f2, g1)
                return [x3, _rms_fwd(x3, g2)], []

            x, h1 = _rowwise("post_mlp_fwd", post_fn, [_row(S["x2"]), _row(S["f2"])],
                             [g_pmlp, P["pre_mix_norm"][l + 1].reshape(1, D_MODEL)], [(D_MODEL, F32), (D_MODEL, BF16)])
        else:
            def loss_fn(x2, f2, tgt, g1):
                diff = x2 + _rms_fwd(f2, g1) - tgt
                return [diff * (1.0 / D_MODEL)], [jnp.sum(diff * diff, axis=0, keepdims=True)]

            dxl, loss_cols = _rowwise("loss", loss_fn, [_row(S["x2"]), _row(S["f2"]), _row(target)], [g_pmlp],
                                      [(D_MODEL, F32)], [(1, D_MODEL)])
    loss_part = 0.5 * jnp.sum(loss_cols) / D_MODEL

    grads = [None] * depth
    dx = dxl
    for l in reversed(range(depth)):
        dx, grads[l] = _layer_bwd(saved[l], dx, W, P, l, bias4, onehot_t)
    grad_x = dx

    full = {n: jnp.stack([grads[l][n] for l in range(depth)]) for n in ALL_W if n != "rel_bias_table"}
    full["rel_bias_table"] = sum(grads[l]["rel_bias_table"] for l in range(depth))
    return loss_part, grad_x, full


def kernel(x, pre_mix_norm, w_in, b_gate, conv_w, conv_b, dt_bias, a_log, d_skip, ssd_norm, w_ssd_out, attn_sink, rel_bias_table, w_attn_out, w_o, post_mix_norm, pre_mlp_norm, w_mlp_in, w_mlp_out, post_mlp_norm, loss_target, m_pre_mix_norm, m_w_in, m_b_gate, m_conv_w, m_conv_b, m_dt_bias, m_a_log, m_d_skip, m_ssd_norm, m_w_ssd_out, m_attn_sink, m_rel_bias_table, m_w_attn_out, m_w_o, m_post_mix_norm, m_pre_mlp_norm, m_w_mlp_in, m_w_mlp_out, m_post_mlp_norm, v_pre_mix_norm, v_w_in, v_b_gate, v_conv_w, v_conv_b, v_dt_bias, v_a_log, v_d_skip, v_ssd_norm, v_w_ssd_out, v_attn_sink, v_rel_bias_table, v_w_attn_out, v_w_o, v_post_mix_norm, v_pre_mlp_norm, v_w_mlp_in, v_w_mlp_out, v_post_mlp_norm):
    a = locals()
    shard_w = {n: a[n] for n in ALL_W}
    shard_m = {n: a["m_" + n] for n in ALL_W}
    shard_v = {n: a["v_" + n] for n in ALL_W}
    loss, grad_x, outs = _step(x[0], loss_target[0], shard_w, shard_m, shard_v)
    return (loss, grad_x[None], *[outs["grad"][n] for n in ALL_W], *[outs["delta"][n] for n in ALL_W],
            *[outs["new_m"][n] for n in ALL_W], *[outs["new_v"][n] for n in ALL_W])
```

```python
import math

import jax
import jax.numpy as jnp
from jax import lax
from jax.experimental import pallas as pl
from jax.experimental.pallas import tpu as pltpu

F32 = jnp.float32
BF16 = jnp.bfloat16
MESH = pl.DeviceIdType.MESH

VMEM_LIMIT_BYTES = 52 * 1024 * 1024
LANES = 128
SUBLANES = 8

EPS = 1e-6
D_MODEL = 1024
D_INNER = 2048
SSD_HEADS = 32
SSD_HEAD_DIM = 64
SSD_GROUPS = 8
SSD_STATE = 128
SSD_CONV = 5
CHUNK = 128
CONV_DIM = 4096
ATTN_HEADS = 16
ATTN_KV = 4
ATTN_DIM = 64
BLOCK = 128
N_BUCKETS = 32
MAX_DISTANCE = 128
D_FF = 4096
N_IN = 9792
NEG = -1e30

OFF_Z, OFF_G, OFF_XBC, OFF_DT, N_MAIN, N_PROJ = 0, 2048, 4096, 8192, 8320, 10368
N_QKV = N_PROJ - N_MAIN

ADAM_LR, ADAM_B1, ADAM_B2, ADAM_EPS, ADAM_WD, ADAM_STEP = 0.001, 0.9, 0.999, 1e-08, 0.01, 10

BIG_ROWS = 6656
SMALL_ROWS = 32


def _cparams(*sem):
    return pltpu.CompilerParams(dimension_semantics=sem, vmem_limit_bytes=VMEM_LIMIT_BYTES)


def _dot(a, b, precision=None):
    return lax.dot_general(a, b, (((1,), (0,)), ((), ())), preferred_element_type=F32, precision=precision)


def _dot_nt(a, b, precision=None):
    return lax.dot_general(a, b, (((1,), (1,)), ((), ())), preferred_element_type=F32, precision=precision)


def _dot_tn(a, b):
    return lax.dot_general(a, b, (((0,), (0,)), ((), ())), preferred_element_type=F32)


def _pick(n, prefs):
    for p in prefs:
        if n % p == 0:
            return p
    return n


def _matmul(name, a, b, mode, out_dtype=F32, epilogue=None, extras=()):
    if mode == "nn":
        (m, k), (_, n) = a.shape, b.shape
    elif mode == "nt":
        (m, k), (n, _) = a.shape, b.shape
    else:
        (k, m), (_, n) = a.shape, b.shape
    tm = _pick(m, (512, 256, 128)) if mode == "tn" else _pick(m, (1024, 512, 256, 128))
    tn = _pick(n, (1024, 1152, 640, 512, 256, 128))
    tk = _pick(k, (1024, 1152, 512, 256, 128)) if mode != "tn" else _pick(k, (512, 256, 128))
    nk = k // tk
    if mode == "nn":
        a_spec = pl.BlockSpec((tm, tk), lambda i, j, q: (i, q))
        b_spec = pl.BlockSpec((tk, tn), lambda i, j, q: (q, j))
        fn = _dot
    elif mode == "nt":
        a_spec = pl.BlockSpec((tm, tk), lambda i, j, q: (i, q))
        b_spec = pl.BlockSpec((tn, tk), lambda i, j, q: (j, q))
        fn = _dot_nt
    else:
        a_spec = pl.BlockSpec((tk, tm), lambda i, j, q: (q, i))
        b_spec = pl.BlockSpec((tk, tn), lambda i, j, q: (q, j))
        fn = _dot_tn

    tile = pl.BlockSpec((tm, tn), lambda i, j, q: (i, j))
    n_ex = len(extras)

    def body(a_ref, b_ref, *rest):
        ex_refs, o_ref = rest[:n_ex], rest[n_ex]

        def store(acc):
            v = acc if epilogue is None else epilogue(acc, *[r[...] for r in ex_refs])
            o_ref[...] = v.astype(o_ref.dtype)

        p = fn(a_ref[...].astype(BF16), b_ref[...].astype(BF16))
        if nk == 1:
            store(p)
        else:
            acc_ref = rest[n_ex + 1]
            q = pl.program_id(2)

            @pl.when(q == 0)
            def _():
                acc_ref[...] = p

            @pl.when((q > 0) & (q < nk - 1))
            def _():
                acc_ref[...] += p

            @pl.when(q == nk - 1)
            def _():
                store(acc_ref[...] + p)

    return pl.pallas_call(
        body, name=name, grid=(m // tm, n // tn, nk),
        in_specs=[a_spec, b_spec] + [tile] * n_ex, out_specs=tile,
        out_shape=jax.ShapeDtypeStruct((m, n), out_dtype),
        scratch_shapes=[pltpu.VMEM((tm, tn), F32)] if nk > 1 else [],
        compiler_params=_cparams("parallel", "parallel", "arbitrary"),
    )(a, b, *extras)


def _row(arr, width=None, cb=0, lead=None):
    return (arr, width, cb, lead)


def _rowwise(name, fn, rows, vecs, outs, accs=(), tb=256):
    t = rows[0][0].shape[-2]
    tb = min(tb, t)
    in_specs, args = [], []
    for arr, width, cb, lead in rows:
        w = arr.shape[-1] if width is None else width
        if lead is None:
            in_specs.append(pl.BlockSpec((tb, w), lambda i, cb=cb: (i, cb)))
        else:
            in_specs.append(pl.BlockSpec((None, tb, w), lambda i, cb=cb, lead=lead: (lead, i, cb)))
        args.append(arr)
    for v in vecs:
        in_specs.append(pl.BlockSpec(v.shape, lambda i, nd=v.ndim: (0,) * nd))
        args.append(v)
    out_shape = [jax.ShapeDtypeStruct((t, c), dt) for c, dt in outs] + [jax.ShapeDtypeStruct(s, F32) for s in accs]
    out_specs = [pl.BlockSpec((tb, c), lambda i: (i, 0)) for c, _ in outs] + [pl.BlockSpec(s, lambda i: (0, 0)) for s in accs]
    n_in, n_out = len(args), len(outs)

    def body(*refs):
        vals = [r[...] for r in refs[:n_in]]
        o_vals, a_vals = fn(*vals)
        for r, v in zip(refs[n_in:n_in + n_out], o_vals):
            r[...] = v.astype(r.dtype)
        first = pl.program_id(0) == 0
        for r, v in zip(refs[n_in + n_out:], a_vals):
            @pl.when(first)
            def _(r=r, v=v):
                r[...] = v

            @pl.when(jnp.logical_not(first))
            def _(r=r, v=v):
                r[...] += v

    res = pl.pallas_call(
        body, name=name, grid=(t // tb,), in_specs=in_specs, out_specs=out_specs, out_shape=out_shape,
        compiler_params=_cparams("arbitrary"),
    )(*args)
    return res


def _rms_fwd(x, g):
    r = lax.rsqrt(jnp.mean(x * x, axis=-1, keepdims=True) + EPS)
    return x * r * g


def _rms_bwd(x, g, dy):
    r = lax.rsqrt(jnp.mean(x * x, axis=-1, keepdims=True) + EPS)
    xh = x * r
    dxh = dy * g
    dx = r * (dxh - xh * jnp.mean(dxh * xh, axis=-1, keepdims=True))
    return dx, jnp.sum(dy * xh, axis=0, keepdims=True)


def _silu(x):
    return x * jax.nn.sigmoid(x)


def _silu_grad(x):
    s = jax.nn.sigmoid(x)
    return s * (1.0 + x * (1.0 - s))


GROUP_W = D_INNER // SSD_GROUPS


def _gated_norm_fwd(y, z, w):
    u = y * _silu(z)
    parts = []
    for j in range(SSD_GROUPS):
        ug = u[:, j * GROUP_W:(j + 1) * GROUP_W]
        parts.append(ug * lax.rsqrt(jnp.mean(ug * ug, axis=-1, keepdims=True) + EPS))
    return jnp.concatenate(parts, axis=-1) * w


def _gated_norm_bwd(y, z, w, dyn):
    sz = _silu(z)
    u = y * sz
    duh = dyn * w
    du_parts, uh_parts = [], []
    for j in range(SSD_GROUPS):
        sl = slice(j * GROUP_W, (j + 1) * GROUP_W)
        ug = u[:, sl]
        r = lax.rsqrt(jnp.mean(ug * ug, axis=-1, keepdims=True) + EPS)
        uh = ug * r
        dg = duh[:, sl]
        du_parts.append(r * (dg - uh * jnp.mean(dg * uh, axis=-1, keepdims=True)))
        uh_parts.append(uh)
    du = jnp.concatenate(du_parts, axis=-1)
    uh = jnp.concatenate(uh_parts, axis=-1)
    dw = jnp.sum(dyn * uh, axis=0, keepdims=True)
    return du * sz, du * y * _silu_grad(z), dw


HALO = SUBLANES


def _halo_specs(tb, cb, col0, t):
    nblk8 = t // HALO
    per = tb // HALO
    main = pl.BlockSpec((tb, cb), lambda j, i: (i, col0 + j))
    prev = pl.BlockSpec((HALO, cb), lambda j, i: (jnp.maximum(i * per - 1, 0), col0 + j))
    nxt = pl.BlockSpec((HALO, cb), lambda j, i: (jnp.minimum((i + 1) * per, nblk8 - 1), col0 + j))
    return main, prev, nxt


def _fill_ext(ext_ref, cur_ref, prev_ref, next_ref, tb, ni):
    i = pl.program_id(1)
    ext_ref[0:HALO, :] = jnp.where(i > 0, prev_ref[...], 0.0)
    ext_ref[HALO:HALO + tb, :] = cur_ref[...]
    ext_ref[HALO + tb:HALO + tb + HALO, :] = jnp.where(i < ni - 1, next_ref[...], 0.0)


def _conv_fwd(proj, w, b):
    t = proj.shape[0]
    tb, cb = min(512, t), 512
    ni, nj = t // tb, CONV_DIM // cb
    main, prev, nxt = _halo_specs(tb, cb, OFF_XBC // cb, t)
    pad = (SSD_CONV - 1) // 2

    def body(u_ref, up_ref, un_ref, w_ref, b_ref, pre_ref, act_ref, ext_ref):
        _fill_ext(ext_ref, u_ref, up_ref, un_ref, tb, ni)
        acc = jnp.broadcast_to(b_ref[...], (tb, cb))
        for k in range(SSD_CONV):
            acc = acc + w_ref[k:k + 1, :] * ext_ref[pl.ds(HALO + k - pad, tb), :]
        pre_ref[...] = acc
        act_ref[...] = _silu(acc)

    out = pl.BlockSpec((tb, cb), lambda j, i: (i, j))
    return pl.pallas_call(
        body, name="conv_fwd", grid=(nj, ni),
        in_specs=[main, prev, nxt, pl.BlockSpec((SSD_CONV, cb), lambda j, i: (0, j)), pl.BlockSpec((1, cb), lambda j, i: (0, j))],
        out_specs=[out, out],
        out_shape=[jax.ShapeDtypeStruct((t, CONV_DIM), F32)] * 2,
        scratch_shapes=[pltpu.VMEM((tb + 2 * HALO, cb), F32)],
        compiler_params=_cparams("parallel", "arbitrary"),
    )(proj, proj, proj, w, b)


def _conv_bwd(dpre, proj, w):
    t = proj.shape[0]
    tb, cb = min(512, t), 512
    ni, nj = t // tb, CONV_DIM // cb
    umain, uprev, unext = _halo_specs(tb, cb, OFF_XBC // cb, t)
    dmain, dprev, dnext = _halo_specs(tb, cb, 0, t)
    pad = (SSD_CONV - 1) // 2

    def body(d_ref, dp_ref, dn_ref, u_ref, up_ref, un_ref, w_ref, du_ref, dw_ref, db_ref, extd_ref, extu_ref):
        _fill_ext(extd_ref, d_ref, dp_ref, dn_ref, tb, ni)
        _fill_ext(extu_ref, u_ref, up_ref, un_ref, tb, ni)
        d = d_ref[...]
        du = jnp.zeros((tb, cb), F32)
        @pl.when(pl.program_id(1) == 0)
        def _():
            dw_ref[...] = jnp.zeros_like(dw_ref)
            db_ref[...] = jnp.zeros_like(db_ref)

        for k in range(SSD_CONV):
            du = du + w_ref[k:k + 1, :] * extd_ref[pl.ds(HALO - k + pad, tb), :]
            dw_ref[k:k + 1, :] += jnp.sum(d * extu_ref[pl.ds(HALO + k - pad, tb), :], axis=0, keepdims=True)
        du_ref[...] = du.astype(du_ref.dtype)
        db_ref[...] += jnp.sum(d, axis=0, keepdims=True)

    return pl.pallas_call(
        body, name="conv_bwd", grid=(nj, ni),
        in_specs=[dmain, dprev, dnext, umain, uprev, unext, pl.BlockSpec((SSD_CONV, cb), lambda j, i: (0, j))],
        out_specs=[pl.BlockSpec((tb, cb), lambda j, i: (i, j)), pl.BlockSpec((SSD_CONV, cb), lambda j, i: (0, j)),
                   pl.BlockSpec((1, cb), lambda j, i: (0, j))],
        out_shape=[jax.ShapeDtypeStruct((t, CONV_DIM), BF16), jax.ShapeDtypeStruct((SSD_CONV, CONV_DIM), F32),
                   jax.ShapeDtypeStruct((1, CONV_DIM), F32)],
        scratch_shapes=[pltpu.VMEM((tb + 2 * HALO, cb), F32)] * 2,
        compiler_params=_cparams("parallel", "arbitrary"),
    )(dpre, dpre, dpre, proj, proj, proj, w)


PAIR = 2 * SSD_HEAD_DIM
HI = lax.Precision.HIGHEST


def _ssd_prelude(d, dt_ref, dtt_ref, ar_ref, ac_ref):
    li = lax.broadcasted_iota(jnp.int32, (CHUNK, CHUNK), 0)
    si = lax.broadcasted_iota(jnp.int32, (CHUNK, CHUNK), 1)
    fwd = d == 0
    hi, lo = jnp.where(fwd, li, si), jnp.where(fwd, si, li)
    tri = hi >= lo
    trif = tri.astype(F32)
    trit = (hi <= lo).astype(F32)
    dt = dt_ref[...]
    adt = dt * ar_ref[...]
    adtt = dtt_ref[...] * ac_ref[...]
    p = _dot(trif, adt, HI)
    pt = _dot_nt(adtt, trif, HI)
    tot = jnp.sum(adt, axis=0, keepdims=True)
    return tri, trit, dt, p, pt, tot


def _ssd_specs(nc, rev):
    def cidx(d, c):
        up = (d == 1) if rev else (d == 0)
        return jnp.where(up, c, nc - 1 - c)

    specs = [
        pl.BlockSpec((CHUNK, D_INNER), lambda d, c: (cidx(d, c), 0)),
        pl.BlockSpec((CHUNK, 1024), lambda d, c: (cidx(d, c), 2)),
        pl.BlockSpec((CHUNK, 1024), lambda d, c: (cidx(d, c), 3)),
        pl.BlockSpec((None, CHUNK, SSD_HEADS), lambda d, c: (d, cidx(d, c), 0)),
        pl.BlockSpec((None, SSD_HEADS, CHUNK), lambda d, c: (d, 0, cidx(d, c))),
        pl.BlockSpec((None, 1, SSD_HEADS), lambda d, c: (d, 0, 0)),
        pl.BlockSpec((None, SSD_HEADS, 1), lambda d, c: (d, 0, 0)),
        pl.BlockSpec((CHUNK, D_INNER), lambda d, c: (cidx(d, c), d)),
    ]
    return cidx, specs


def _head_decay(tri, p, pt, tot, h):
    pb = jnp.broadcast_to(p[:, h:h + 1], (CHUNK, CHUNK))
    dec = jnp.exp(jnp.where(tri, pb - pt[h:h + 1, :], NEG))
    return dec, jnp.exp(tot[:, h:h + 1] - pb), jnp.exp(pb)


def _ssd_fwd(act, dt2, dt2t, a_row, a_col, dte):
    t = act.shape[0]
    nc = t // CHUNK
    cidx, specs = _ssd_specs(nc, rev=False)

    def body(xs_ref, bs_ref, cs_ref, dt_ref, dtt_ref, ar_ref, ac_ref, dte_ref, y_ref, st_ref, h_ref):
        d, c = pl.program_id(0), pl.program_id(1)

        @pl.when(c == 0)
        def _():
            h_ref[...] = jnp.zeros_like(h_ref)

        st_ref[...] = h_ref[...]
        tri, _, _, p, pt, tot = _ssd_prelude(d, dt_ref, dtt_ref, ar_ref, ac_ref)
        etot = jnp.exp(tot)
        lane = lax.broadcasted_iota(jnp.int32, (CHUNK, PAIR), 1) < SSD_HEAD_DIM
        rowh = lax.broadcasted_iota(jnp.int32, (PAIR, SSD_STATE), 0) < SSD_HEAD_DIM
        for g in range(SSD_GROUPS):
            gs = slice(g * SSD_STATE, (g + 1) * SSD_STATE)
            bg = bs_ref[:, gs]
            cb = cs_ref[:, gs].astype(BF16)
            cbm = _dot_nt(cb, bg.astype(BF16))
            for pr in range(2):
                h0 = g * 4 + pr * 2
                h1 = h0 + 1
                sl = slice(h0 * SSD_HEAD_DIM, h0 * SSD_HEAD_DIM + PAIR)
                xdt = (xs_ref[:, sl] * dte_ref[:, sl]).astype(BF16)
                yd, st, epb = [], [], []
                for h in (h0, h1):
                    dec, wb, eb = _head_decay(tri, p, pt, tot, h)
                    yd.append(_dot((cbm * dec).astype(BF16), xdt))
                    st.append(_dot_tn(xdt, (bg * wb).astype(BF16)))
                    epb.append(eb)
                hin = h_ref[sl, :]
                yo = _dot_nt(cb, hin.astype(BF16)) * jnp.where(lane, epb[0], epb[1])
                y_ref[:, sl] = jnp.where(lane, yd[0], yd[1]) + yo
                et = jnp.where(rowh, etot[:, h0:h0 + 1], etot[:, h1:h1 + 1])
                h_ref[sl, :] = hin * et + jnp.where(rowh, st[0], st[1])

    return pl.pallas_call(
        body, name="ssd_fwd", grid=(2, nc), in_specs=specs,
        out_specs=[pl.BlockSpec((None, CHUNK, D_INNER), lambda d, c: (d, cidx(d, c), 0)),
                   pl.BlockSpec((None, None, D_INNER, SSD_STATE), lambda d, c: (d, cidx(d, c), 0, 0))],
        out_shape=[jax.ShapeDtypeStruct((2, t, D_INNER), F32), jax.ShapeDtypeStruct((2, nc, D_INNER, SSD_STATE), F32)],
        scratch_shapes=[pltpu.VMEM((D_INNER, SSD_STATE), F32)],
        compiler_params=_cparams("arbitrary", "arbitrary"),
    )(act, act, act, dt2, dt2t, a_row, a_col, dte)


def _put_col(acc, col, h):
    lane = lax.broadcasted_iota(jnp.int32, acc.shape, 1)
    return jnp.where(lane == h, col, acc)


def _put_row(acc, row, h):
    sub = lax.broadcasted_iota(jnp.int32, acc.shape, 0)
    return jnp.where(sub == h, row, acc)


def _sum_all(x):
    return jnp.sum(jnp.sum(x, axis=0, keepdims=True), axis=1, keepdims=True)


def _ssd_bwd(act, dt2, dt2t, a_row, a_col, dte, dy, states):
    t = act.shape[0]
    nc = t // CHUNK
    cidx, specs = _ssd_specs(nc, rev=True)
    specs = specs + [
        pl.BlockSpec((CHUNK, D_INNER), lambda d, c: (cidx(d, c), 0)),
        pl.BlockSpec((None, None, D_INNER, SSD_STATE), lambda d, c: (d, cidx(d, c), 0, 0)),
    ]

    def body(xs_ref, bs_ref, cs_ref, dt_ref, dtt_ref, ar_ref, ac_ref, dte_ref, dy_ref, st_ref,
             dxs_ref, dbs_ref, dcs_ref, ddt_ref, da_ref, dh_ref):
        d, c = pl.program_id(0), pl.program_id(1)

        @pl.when(c == 0)
        def _():
            dh_ref[...] = jnp.zeros_like(dh_ref)
            da_ref[...] = jnp.zeros_like(da_ref)

        tri, trit, dt, p, pt, tot = _ssd_prelude(d, dt_ref, dtt_ref, ar_ref, ac_ref)
        etot = jnp.exp(tot)
        lane = lax.broadcasted_iota(jnp.int32, (CHUNK, PAIR), 1) < SSD_HEAD_DIM
        rowh = lax.broadcasted_iota(jnp.int32, (PAIR, SSD_STATE), 0) < SSD_HEAD_DIM
        first_head = lax.broadcasted_iota(jnp.int32, (PAIR, LANES), 0) < SSD_HEAD_DIM
        out_lane = lax.broadcasted_iota(jnp.int32, (PAIR, LANES), 1)
        ddtx = jnp.zeros((CHUNK, LANES), F32)
        lane32 = lax.broadcasted_iota(jnp.int32, (CHUNK, SSD_HEADS), 1)
        dp_col = jnp.zeros((CHUNK, SSD_HEADS), F32)
        dp_row = jnp.zeros((SSD_HEADS, CHUNK), F32)
        dtot = jnp.zeros((1, SSD_HEADS), F32)
        for g in range(SSD_GROUPS):
            gs = slice(g * SSD_STATE, (g + 1) * SSD_STATE)
            bg = bs_ref[:, gs]
            bb = bg.astype(BF16)
            cb = cs_ref[:, gs].astype(BF16)
            cbm = _dot_nt(cb, bb)
            dcb = jnp.zeros((CHUNK, CHUNK), F32)
            dc_acc = jnp.zeros((CHUNK, SSD_STATE), F32)
            db_acc = jnp.zeros((CHUNK, SSD_STATE), F32)
            for pr in range(2):
                h0 = g * 4 + pr * 2
                h1 = h0 + 1
                sl = slice(h0 * SSD_HEAD_DIM, h0 * SSD_HEAD_DIM + PAIR)
                xp = xs_ref[:, sl]
                dtp = dte_ref[:, sl]
                xdt_f = xp * dtp
                xdt = xdt_f.astype(BF16)
                dyp = dy_ref[:, sl]
                dyb = dyp.astype(BF16)
                hin = st_ref[sl, :]
                dh = dh_ref[sl, :]
                hb = hin.astype(BF16)
                dhb = dh.astype(BF16)
                heads = [_head_decay(tri, p, pt, tot, h) for h in (h0, h1)]
                dye = dyp * jnp.where(lane, heads[0][2], heads[1][2])
                dyeb = dye.astype(BF16)
                gy = _dot_nt(cb, hb) * dye
                dc_acc = dc_acc + _dot(dyeb, hb)
                dhin = _dot_tn(dyeb, cb)
                hh = dh * hin
                dxdt = jnp.zeros((CHUNK, PAIR), F32)
                for idx, h in enumerate((h0, h1)):
                    hm = lane if idx == 0 else jnp.logical_not(lane)
                    rm = rowh if idx == 0 else jnp.logical_not(rowh)
                    dec, wb, _ = heads[idx]
                    mf = cbm * dec
                    t1 = _dot_tn(mf.astype(BF16), dyb)
                    t2 = _dot_nt((bg * wb).astype(BF16), dhb)
                    dxdt = jnp.where(hm, t1 + t2, dxdt)
                    dm = _dot_nt(jnp.where(hm, dyp, 0.0).astype(BF16), xdt)
                    dcb = dcb + dm * dec
                    e = dm * mf
                    qw = _dot(jnp.where(hm, xdt_f, 0.0).astype(BF16), dhb) * wb
                    db_acc = db_acc + qw
                    qwb = qw * bg
                    col = jnp.sum(e + jnp.where(hm, gy, 0.0) - qwb, axis=1, keepdims=True)
                    dp_col = jnp.where(lane32 == h, col, dp_col)
                    dp_row = _put_row(dp_row, -jnp.sum(e, axis=0, keepdims=True), h)
                    dtot_h = _sum_all(qwb) + etot[:, h:h + 1] * _sum_all(jnp.where(rm, hh, 0.0))
                    dtot = _put_col(dtot, dtot_h, h)
                dxs_ref[:, sl] = dxdt * dtp
                ddx = dxdt * xp
                ddx_hi = ddx.astype(BF16)
                ddx_lo = (ddx - ddx_hi.astype(F32)).astype(BF16)
                route = (out_lane == jnp.where(first_head, h0, h1)).astype(BF16)
                ddtx = ddtx + _dot(ddx_hi, route) + _dot(ddx_lo, route)
                et = jnp.where(rowh, etot[:, h0:h0 + 1], etot[:, h1:h1 + 1])
                dh_ref[sl, :] = dh * et + dhin
            dcbb = dcb.astype(BF16)
            dcs_ref[:, gs] = _dot(dcbb, bb) + dc_acc
            dbs_ref[:, gs] = _dot_tn(dcbb, cb) + db_acc
        d_adt = _dot(trit, dp_col, HI) + _dot_nt(trit, dp_row, HI) + dtot
        ddt_ref[...] = ddtx[:, :SSD_HEADS] + ar_ref[...] * d_adt
        da_ref[...] += jnp.sum(dt * d_adt, axis=0, keepdims=True)

    return pl.pallas_call(
        body, name="ssd_bwd", grid=(2, nc), in_specs=specs,
        out_specs=[pl.BlockSpec((None, CHUNK, D_INNER), lambda d, c: (d, cidx(d, c), 0)),
                   pl.BlockSpec((None, CHUNK, 1024), lambda d, c: (d, cidx(d, c), 0)),
                   pl.BlockSpec((None, CHUNK, 1024), lambda d, c: (d, cidx(d, c), 0)),
                   pl.BlockSpec((None, CHUNK, SSD_HEADS), lambda d, c: (d, cidx(d, c), 0)),
                   pl.BlockSpec((None, 1, SSD_HEADS), lambda d, c: (d, 0, 0))],
        out_shape=[jax.ShapeDtypeStruct((2, t, D_INNER), F32), jax.ShapeDtypeStruct((2, t, 1024), F32),
                   jax.ShapeDtypeStruct((2, t, 1024), F32), jax.ShapeDtypeStruct((2, t, SSD_HEADS), F32),
                   jax.ShapeDtypeStruct((2, 1, SSD_HEADS), F32)],
        scratch_shapes=[pltpu.VMEM((D_INNER, SSD_STATE), F32)],
        compiler_params=_cparams("arbitrary", "arbitrary"),
    )(act, act, act, dt2, dt2t, a_row, a_col, dte, dy, states)


REP = ATTN_HEADS // ATTN_KV
SCALE = ATTN_DIM ** -0.5
GROUP_Q = REP * ATTN_DIM
K_BLK0 = D_MODEL // LANES
V_BLK0 = K_BLK0 + ATTN_KV


def _attn_specs(nb):
    q = pl.BlockSpec((BLOCK, GROUP_Q), lambda g, n: (n, g))

    def kv(blk0):
        return [pl.BlockSpec((BLOCK, LANES), lambda g, n: (jnp.maximum(n - 1, 0), blk0 + g)),
                pl.BlockSpec((BLOCK, LANES), lambda g, n: (n, blk0 + g)),
                pl.BlockSpec((BLOCK, LANES), lambda g, n: (jnp.minimum(n + 1, nb - 1), blk0 + g))]

    bias = pl.BlockSpec((None, REP, BLOCK, 3 * BLOCK), lambda g, n: (g, 0, 0, 0))
    sink = pl.BlockSpec((None, REP, 1, LANES), lambda g, n: (g, 0, 0, 0))
    return q, kv(K_BLK0), kv(V_BLK0), bias, sink


def _band_mask():
    ii = lax.broadcasted_iota(jnp.int32, (BLOCK, 3 * BLOCK), 0)
    jj = lax.broadcasted_iota(jnp.int32, (BLOCK, 3 * BLOCK), 1)
    return (jj >= ii) & (jj - 2 * BLOCK <= ii)


def _attn_valid(n, nb):
    jj = lax.broadcasted_iota(jnp.int32, (1, 3 * BLOCK), 1)
    return ((jj >= BLOCK) | (n > 0)) & ((jj < 2 * BLOCK) | (n < nb - 1))


def _attn_probs(q, kcat, bias, snk, valid):
    s = jnp.where(valid, _dot_nt(q, kcat) + bias, NEG)
    m = jnp.maximum(jnp.max(s, axis=1, keepdims=True), snk)
    p = jnp.exp(s - m)
    es = jnp.exp(snk - m)
    r = 1.0 / (jnp.sum(p, axis=1, keepdims=True) + es)
    return p * r, es * r


def _stack_heads(ref, lane):
    parts = []
    for pr in range(REP // 2):
        tile = ref[:, pr * LANES:(pr + 1) * LANES]
        parts += [jnp.where(lane, tile, 0.0), jnp.where(lane, 0.0, tile)]
    return jnp.concatenate(parts, axis=0)


def _unstack_heads(x, lane):
    return jnp.concatenate([jnp.where(lane, x[(2 * pr) * BLOCK:(2 * pr + 1) * BLOCK], x[(2 * pr + 1) * BLOCK:(2 * pr + 2) * BLOCK])
                            for pr in range(REP // 2)], axis=1)


def _stack_bias(b_ref, s_ref):
    bias = jnp.concatenate([b_ref[r] for r in range(REP)], axis=0)
    snk = jnp.concatenate([jnp.broadcast_to(s_ref[r][:, 0:1], (BLOCK, 1)) for r in range(REP)], axis=0)
    return bias, snk


def _attn_fwd(qkv, bias4, sink4):
    t = qkv.shape[0]
    nb = t // BLOCK
    qs, ks, vs, bs, ss = _attn_specs(nb)

    def body(q_ref, kp_ref, kc_ref, kn_ref, vp_ref, vc_ref, vn_ref, b_ref, s_ref, o_ref):
        n = pl.program_id(1)
        kcat = jnp.concatenate([kp_ref[...], kc_ref[...], kn_ref[...]], axis=0)
        vcat = jnp.concatenate([vp_ref[...], vc_ref[...], vn_ref[...]], axis=0)
        valid = _attn_valid(n, nb)
        lane = lax.broadcasted_iota(jnp.int32, (BLOCK, LANES), 1) < ATTN_DIM
        bias, snk = _stack_bias(b_ref, s_ref)
        pn, _ = _attn_probs(_stack_heads(q_ref, lane) * SCALE, kcat, bias, snk, valid)
        o_ref[...] = _unstack_heads(_dot(pn.astype(BF16), vcat), lane).astype(o_ref.dtype)

    return pl.pallas_call(
        body, name="attn_fwd", grid=(ATTN_KV, nb), in_specs=[qs] + ks + vs + [bs, ss],
        out_specs=qs, out_shape=jax.ShapeDtypeStruct((t, D_MODEL), BF16),
        compiler_params=_cparams("parallel", "arbitrary"),
    )(qkv, qkv, qkv, qkv, qkv, qkv, qkv, bias4, sink4)


def _attn_bwd(qkv, bias4, sink4, do):
    t = qkv.shape[0]
    nb = t // BLOCK
    qs, ks, vs, bs, ss = _attn_specs(nb)
    part = pl.BlockSpec((3, BLOCK, LANES), lambda g, n: (0, n, g))

    def body(q_ref, kp_ref, kc_ref, kn_ref, vp_ref, vc_ref, vn_ref, b_ref, s_ref, do_ref,
             dq_ref, dk_ref, dv_ref, db_ref, ds_ref):
        n = pl.program_id(1)

        @pl.when(n == 0)
        def _():
            db_ref[...] = jnp.zeros_like(db_ref)
            ds_ref[...] = jnp.zeros_like(ds_ref)

        kcat = jnp.concatenate([kp_ref[...], kc_ref[...], kn_ref[...]], axis=0)
        vcat = jnp.concatenate([vp_ref[...], vc_ref[...], vn_ref[...]], axis=0)
        valid = _attn_valid(n, nb)
        lane = lax.broadcasted_iota(jnp.int32, (BLOCK, LANES), 1) < ATTN_DIM
        bias, snk = _stack_bias(b_ref, s_ref)
        q = _stack_heads(q_ref, lane)
        do = _stack_heads(do_ref, lane)
        pn, psink = _attn_probs(q * SCALE, kcat, bias, snk, valid)
        dp = _dot_nt(do, vcat)
        delta = jnp.sum(pn * dp, axis=1, keepdims=True)
        dsc = pn * (dp - delta)
        dsink = psink * delta
        for r in range(REP):
            rows = slice(r * BLOCK, (r + 1) * BLOCK)
            db_ref[r] += dsc[rows]
            ds_ref[r] += jnp.broadcast_to(-jnp.sum(dsink[rows], axis=0, keepdims=True), (1, LANES))
        dsb = (dsc * SCALE).astype(BF16)
        dq_ref[...] = _unstack_heads(_dot(dsb, kcat), lane).astype(dq_ref.dtype)
        dk = _dot_tn(dsb, q)
        dv = _dot_tn(pn.astype(BF16), do)
        for j in range(3):
            dk_ref[j] = dk[j * BLOCK:(j + 1) * BLOCK]
            dv_ref[j] = dv[j * BLOCK:(j + 1) * BLOCK]

    kv_cols = ATTN_KV * LANES
    return pl.pallas_call(
        body, name="attn_bwd", grid=(ATTN_KV, nb), in_specs=[qs] + ks + vs + [bs, ss, qs],
        out_specs=[qs, part, part, bs, ss],
        out_shape=[jax.ShapeDtypeStruct((t, D_MODEL), BF16), jax.ShapeDtypeStruct((3, t, kv_cols), F32),
                   jax.ShapeDtypeStruct((3, t, kv_cols), F32), jax.ShapeDtypeStruct(bias4.shape, F32),
                   jax.ShapeDtypeStruct(sink4.shape, F32)],
        compiler_params=_cparams("parallel", "arbitrary"),
    )(qkv, qkv, qkv, qkv, qkv, qkv, qkv, bias4, sink4, do)


def _kv_combine(name, parts):
    _, t, cols = parts.shape

    def body(p_ref, o_ref):
        z = jnp.zeros((BLOCK, LANES), F32)
        from_next = jnp.concatenate([p_ref[0, BLOCK:, :], z], axis=0)
        from_prev = jnp.concatenate([z, p_ref[2, :t - BLOCK, :]], axis=0)
        o_ref[...] = (from_next + p_ref[1] + from_prev).astype(o_ref.dtype)

    return pl.pallas_call(
        body, name=name, grid=(cols // LANES,),
        in_specs=[pl.BlockSpec((3, t, LANES), lambda g: (0, 0, g))],
        out_specs=pl.BlockSpec((t, LANES), lambda g: (0, g)),
        out_shape=jax.ShapeDtypeStruct((t, cols), BF16),
        compiler_params=_cparams("parallel"),
    )(parts)


def _t5_bucket(rel):
    nb = N_BUCKETS // 2
    max_exact = nb // 2
    ret = jnp.where(rel > 0, nb, 0)
    n = jnp.abs(rel)
    nf = jnp.maximum(n, 1).astype(jnp.float32)
    large = max_exact + (jnp.log(nf / max_exact) / math.log(MAX_DISTANCE / max_exact) * (nb - max_exact)).astype(jnp.int32)
    large = jnp.minimum(large, nb - 1)
    return ret + jnp.where(n < max_exact, n, large)


def _bucket_map():
    i = jnp.arange(BLOCK)[:, None]
    j = jnp.arange(3 * BLOCK)[None, :]
    return _t5_bucket(j - BLOCK - i)


def _bias_from_table(table, onehot_t):
    def body(t_ref, o_ref, out_ref):
        out_ref[...] = _dot(t_ref[...], o_ref[...], HI)

    return pl.pallas_call(body, name="bias_from_table", out_shape=jax.ShapeDtypeStruct((ATTN_HEADS, onehot_t.shape[1]), F32),
                          compiler_params=pltpu.CompilerParams(vmem_limit_bytes=VMEM_LIMIT_BYTES))(table.T, onehot_t)


def _bias_table_grad(dbias, onehot_t):
    def body(d_ref, o_ref, out_ref):
        out_ref[...] = _dot_nt(d_ref[...], o_ref[...], HI)

    return pl.pallas_call(body, name="bias_table_grad", out_shape=jax.ShapeDtypeStruct((ATTN_HEADS, N_BUCKETS), F32),
                          compiler_params=pltpu.CompilerParams(vmem_limit_bytes=VMEM_LIMIT_BYTES))(dbias, onehot_t)


HBM_SPEC = pl.BlockSpec(memory_space=pl.ANY)


def _comm_call(name, body, xs, out_shapes, n_sems):
    n = len(xs)
    return pl.pallas_call(
        body, name=name, in_specs=[HBM_SPEC] * n, out_specs=[HBM_SPEC] * n, out_shape=out_shapes,
        scratch_shapes=[pltpu.SemaphoreType.DMA((n * n_sems,)), pltpu.SemaphoreType.DMA((n * n_sems,)),
                        pltpu.SemaphoreType.DMA((n,))],
    )(*xs)


def _allgather_chips(name, xs):
    n = len(xs)

    def body(*refs):
        x_refs, out_refs, (send_sems, recv_sems, local_sems) = refs[:n], refs[n:2 * n], refs[2 * n:]
        mx, my, mc = lax.axis_index("x"), lax.axis_index("y"), lax.axis_index("c")
        me = 2 * mx + my
        chips = [(1 - mx, my), (mx, 1 - my), (1 - mx, 1 - my)]
        sibling = (mx, my, 1 - mc)

        def part(i, slot, h):
            r2 = xs[i].shape[0] // 2
            return out_refs[i].at[slot, pl.ds(h * r2, r2)]

        def copy(i, k, src, dst, to):
            return pltpu.make_async_remote_copy(src_ref=src, dst_ref=dst, send_sem=send_sems.at[6 * i + k],
                                                recv_sem=recv_sems.at[6 * i + k], device_id=to, device_id_type=MESH)

        mine = [pltpu.make_async_copy(x_refs[i], out_refs[i].at[me], local_sems.at[i]) for i in range(n)]
        first = [copy(i, k, x_refs[i].at[pl.ds(mc * (xs[i].shape[0] // 2), xs[i].shape[0] // 2)], part(i, me, mc), (px, py, mc))
                 for i in range(n) for k, (px, py) in enumerate(chips)]
        for cp in mine + first:
            cp.start()
        passed = []
        for k, (px, py) in enumerate(chips):
            for i in range(n):
                landed = part(i, 2 * px + py, mc)
                copy(i, k, landed, landed, (px, py, mc)).wait_recv()
                passed.append(copy(i, 3 + k, landed, landed, sibling))
                passed[-1].start()
        for k, (px, py) in enumerate(chips):
            for i in range(n):
                theirs = part(i, 2 * px + py, 1 - mc)
                copy(i, 3 + k, theirs, theirs, sibling).wait_recv()
        for cp in first + passed:
            cp.wait_send()
        for cp in mine:
            cp.wait()

    return _comm_call(name, body, xs, [jax.ShapeDtypeStruct((4,) + x.shape, x.dtype) for x in xs], 6)


def _swap_cores(name, xs):
    n = len(xs)

    def body(*refs):
        x_refs, out_refs, (send_sems, recv_sems, _) = refs[:n], refs[n:2 * n], refs[2 * n:]
        mx, my, mc = lax.axis_index("x"), lax.axis_index("y"), lax.axis_index("c")
        sends = [pltpu.make_async_remote_copy(src_ref=x_refs[i].at[1 - mc], dst_ref=out_refs[i], send_sem=send_sems.at[i],
                                              recv_sem=recv_sems.at[i], device_id=(mx, my, 1 - mc), device_id_type=MESH)
                 for i in range(n)]
        for cp in sends:
            cp.start()
        for cp in sends:
            cp.wait()

    return _comm_call(name, body, xs, [jax.ShapeDtypeStruct(x.shape[1:], x.dtype) for x in xs], 1)


def _scatter_chips(name, gs):
    n = len(gs)

    def body(*refs):
        g_refs, out_refs, (send_sems, recv_sems, local_sems) = refs[:n], refs[n:2 * n], refs[2 * n:]
        mx, my, mc = lax.axis_index("x"), lax.axis_index("y"), lax.axis_index("c")
        me = 2 * mx + my
        chips = [(1 - mx, my), (mx, 1 - my), (1 - mx, 1 - my)]

        def copy(i, k, src, dst, to):
            return pltpu.make_async_remote_copy(src_ref=src, dst_ref=dst, send_sem=send_sems.at[3 * i + k],
                                                recv_sem=recv_sems.at[3 * i + k], device_id=to, device_id_type=MESH)

        mine = [pltpu.make_async_copy(g_refs[i].at[me], out_refs[i].at[me], local_sems.at[i]) for i in range(n)]
        sends = [copy(i, k, g_refs[i].at[2 * px + py], out_refs[i].at[me], (px, py, mc))
                 for i in range(n) for k, (px, py) in enumerate(chips)]
        for cp in mine + sends:
            cp.start()
        for i in range(n):
            for k, (px, py) in enumerate(chips):
                copy(i, k, g_refs[i].at[0], out_refs[i].at[2 * px + py], (px, py, mc)).wait_recv()
        for cp in sends:
            cp.wait_send()
        for cp in mine:
            cp.wait()

    return _comm_call(name, body, gs, [jax.ShapeDtypeStruct(g.shape, g.dtype) for g in gs], 3)


def _allgather_all(name, x):
    def body(x_ref, out_ref, send_sems, recv_sems, local_sem):
        mx, my, mc = lax.axis_index("x"), lax.axis_index("y"), lax.axis_index("c")
        me = 4 * mx + 2 * my + mc
        flips = [(fx, fy, fc) for fx in (0, 1) for fy in (0, 1) for fc in (0, 1)][1:]
        peers = [(mx ^ fx, my ^ fy, mc ^ fc) for fx, fy, fc in flips]
        mine = pltpu.make_async_copy(x_ref, out_ref.at[me], local_sem)
        mine.start()
        sends = [pltpu.make_async_remote_copy(src_ref=x_ref, dst_ref=out_ref.at[me], send_sem=send_sems.at[k],
                                              recv_sem=recv_sems.at[k], device_id=peer, device_id_type=MESH)
                 for k, peer in enumerate(peers)]
        for cp in sends:
            cp.start()
        for k, (px, py, pc) in enumerate(peers):
            pltpu.make_async_remote_copy(src_ref=x_ref, dst_ref=out_ref.at[4 * px + 2 * py + pc], send_sem=send_sems.at[k],
                                         recv_sem=recv_sems.at[k], device_id=(px, py, pc), device_id_type=MESH).wait_recv()
        for cp in sends:
            cp.wait_send()
        mine.wait()

    return pl.pallas_call(
        body, name=name, in_specs=[HBM_SPEC], out_specs=HBM_SPEC,
        out_shape=jax.ShapeDtypeStruct((8,) + x.shape, x.dtype),
        scratch_shapes=[pltpu.SemaphoreType.DMA((7,)), pltpu.SemaphoreType.DMA((7,)), pltpu.SemaphoreType.DMA],
    )(x)


def _allgather_cores(name, xs):
    n = len(xs)

    def body(*refs):
        x_refs, out_refs, (send_sems, recv_sems, local_sems) = refs[:n], refs[n:2 * n], refs[2 * n:]
        mx, my, mc = lax.axis_index("x"), lax.axis_index("y"), lax.axis_index("c")

        def copy(i, slot):
            return pltpu.make_async_remote_copy(src_ref=x_refs[i], dst_ref=out_refs[i].at[slot], send_sem=send_sems.at[i],
                                                recv_sem=recv_sems.at[i], device_id=(mx, my, 1 - mc), device_id_type=MESH)

        mine = [pltpu.make_async_copy(x_refs[i], out_refs[i].at[mc], local_sems.at[i]) for i in range(n)]
        sends = [copy(i, mc) for i in range(n)]
        for cp in mine + sends:
            cp.start()
        for i in range(n):
            copy(i, 1 - mc).wait_recv()
        for cp in sends:
            cp.wait_send()
        for cp in mine:
            cp.wait()

    return _comm_call(name, body, xs, [jax.ShapeDtypeStruct((2,) + x.shape, x.dtype) for x in xs], 1)


def _sum_slots(name, st, tb=256):
    n, r, c = st.shape
    tb = _pick(r, (tb, 32))

    def body(s_ref, o_ref):
        acc = s_ref[0].astype(F32)
        for k in range(1, n):
            acc = acc + s_ref[k].astype(F32)
        o_ref[...] = acc

    return pl.pallas_call(
        body, name=name, grid=(r // tb,), in_specs=[pl.BlockSpec((n, tb, c), lambda i: (0, i, 0))],
        out_specs=pl.BlockSpec((tb, c), lambda i: (i, 0)), out_shape=jax.ShapeDtypeStruct((r, c), F32),
        compiler_params=_cparams("parallel"),
    )(st)


def _adamw(name, w, g, m, v):
    def fn(w, g, m, v):
        m2 = ADAM_B1 * m + (1.0 - ADAM_B1) * g
        v2 = ADAM_B2 * v + (1.0 - ADAM_B2) * jnp.square(g)
        m_hat = m2 / (1.0 - ADAM_B1 ** ADAM_STEP)
        v_hat = v2 / (1.0 - ADAM_B2 ** ADAM_STEP)
        delta = -ADAM_LR * (m_hat / (jnp.sqrt(v_hat) + ADAM_EPS) + ADAM_WD * w)
        return [delta, m2, v2], []

    tb = _pick(w.shape[0], (256, 32))
    return _rowwise(name, fn, [_row(w), _row(g), _row(m), _row(v)], [], [(w.shape[1], F32)] * 3, tb=tb)


BIG = ("w_ssd_out", "w_attn_out", "w_o", "w_mlp_in", "w_mlp_out", "conv_w")
SMALL = ("pre_mix_norm", "b_gate", "conv_b", "dt_bias", "a_log", "d_skip", "ssd_norm", "attn_sink", "rel_bias_table",
         "post_mix_norm", "pre_mlp_norm", "post_mlp_norm")
ALL_W = ("pre_mix_norm", "w_in", "b_gate", "conv_w", "conv_b", "dt_bias", "a_log", "d_skip", "ssd_norm", "w_ssd_out",
         "attn_sink", "rel_bias_table", "w_attn_out", "w_o", "post_mix_norm", "pre_mlp_norm", "w_mlp_in", "w_mlp_out",
         "post_mlp_norm")


def _pack_rows(parts, rows, dtype):
    flat = jnp.concatenate([p.reshape(-1, D_MODEL).astype(dtype) for p in parts], axis=0)
    return jnp.pad(flat, ((0, rows - flat.shape[0]), (0, 0)))


def _pack_big(shards, dtype):
    return _pack_rows([shards[n] for n in BIG], BIG_ROWS, dtype)


def _unpack_big(flat, like):
    out, r = {}, 0
    for n in BIG:
        shp = like[n].shape
        nr = math.prod(shp) // D_MODEL
        out[n] = flat[r:r + nr].reshape(shp)
        r += nr
    return out


def _pack_small(parts, extra=None):
    flat = jnp.concatenate([parts[n].reshape(-1).astype(F32) for n in SMALL] + ([extra.reshape(-1)] if extra is not None else []))
    return jnp.pad(flat, (0, SMALL_ROWS * D_MODEL - flat.shape[0])).reshape(SMALL_ROWS, D_MODEL)


def _unpack_small(flat2, like):
    flat = flat2.reshape(-1)
    out, r = {}, 0
    for n in SMALL:
        shp = like[n].shape
        k = math.prod(shp)
        out[n] = flat[r:r + k].reshape(shp)
        r += k
    return out, flat[r]


def _shard_of_full(name, full, s):
    if name in ("w_in", "w_mlp_in"):
        w = full.shape[2] // 4
        return full[:, :, s * w:(s + 1) * w]
    if name == "conv_w":
        w = full.shape[3] // 4
        return full[:, :, :, s * w:(s + 1) * w]
    w = full.shape[1] // 4
    return full[:, s * w:(s + 1) * w, :]


def _full_of_shards(name, shards):
    axis = {"w_in": 2, "w_mlp_in": 2, "conv_w": 3}.get(name, 1)
    return jnp.concatenate(shards, axis=axis)


def _to_proj_layout(w):
    z, xbc, dt, q, k, v, gates = (w[..., 0:2048], w[..., 2048:6144], w[..., 6144:6208], w[..., 6208:7232],
                                  w[..., 7232:7488], w[..., 7488:7744], w[..., 7744:9792])
    pad = jnp.zeros(w.shape[:-1] + (N_MAIN - OFF_DT - dt.shape[-1],), w.dtype)

    def doubled(a):
        h = a.reshape(a.shape[:-1] + (ATTN_KV, 1, ATTN_DIM))
        return jnp.broadcast_to(h, a.shape[:-1] + (ATTN_KV, 2, ATTN_DIM)).reshape(a.shape[:-1] + (2 * a.shape[-1],))

    return jnp.concatenate([z, gates, xbc, dt, pad, q, doubled(k), doubled(v)], axis=-1)


def _from_proj_layout(w):
    z, gates, xbc, dt, q, k2, v2 = (w[..., 0:2048], w[..., 2048:4096], w[..., 4096:8192], w[..., 8192:8256],
                                    w[..., 8320:9344], w[..., 9344:9856], w[..., 9856:10368])

    def folded(a):
        return a.reshape(a.shape[:-1] + (ATTN_KV, 2, ATTN_DIM)).sum(axis=-2).reshape(a.shape[:-1] + (a.shape[-1] // 2,))

    return jnp.concatenate([z, xbc, dt, q, folded(k2), folded(v2), gates], axis=-1)


def _layer_fwd(h1, x, W, P, l, bias4):
    t = x.shape[0]
    S = {"x": x, "h1": h1}
    proj = _matmul("proj", h1, W["w_in_main"][l], "nn")
    qkv = _matmul("proj_qkv", h1, W["w_in_qkv"][l], "nn", out_dtype=BF16)
    S["proj"] = proj
    pre, act = _conv_fwd(proj, W["conv_w"][l], P["conv_b"][l].reshape(1, CONV_DIM))
    S["pre"], S["act"] = pre, act

    dtb = jnp.pad(P["dt_bias"][l].reshape(1, 2 * SSD_HEADS), ((0, 0), (0, LANES - 2 * SSD_HEADS)))

    def dt_fn(raw, b):
        v = raw + b
        dt = jnp.maximum(v, 0.0) + jnp.log1p(jnp.exp(-jnp.abs(v)))
        expand = (jnp.right_shift(lax.broadcasted_iota(jnp.int32, (LANES, 2 * D_INNER), 1), 6)
                  == lax.broadcasted_iota(jnp.int32, (LANES, 2 * D_INNER), 0)).astype(BF16)
        hi = dt.astype(BF16)
        rest = dt - hi.astype(F32)
        mid = rest.astype(BF16)
        lo = (rest - mid.astype(F32)).astype(BF16)
        return [dt, _dot(hi, expand) + _dot(mid, expand) + _dot(lo, expand)], []

    dt, dte = _rowwise("dt_fwd", dt_fn, [_row(proj, LANES, OFF_DT // LANES)], [dtb], [(LANES, F32), (2 * D_INNER, F32)], tb=512)
    dt2 = jnp.stack([dt[:, 0:SSD_HEADS], dt[:, SSD_HEADS:2 * SSD_HEADS]])
    dt2t = dt2.transpose(0, 2, 1)
    a = -jnp.exp(P["a_log"][l])
    a_row, a_col = a.reshape(2, 1, SSD_HEADS), a.reshape(2, SSD_HEADS, 1)
    S["dt2"], S["dt2t"], S["a_row"], S["a_col"], S["dte"] = dt2, dt2t, a_row, a_col, dte
    y2, states = _ssd_fwd(act, dt2, dt2t, a_row, a_col, dte)
    S["states"] = states

    dsk = jnp.repeat(P["d_skip"][l], SSD_HEAD_DIM).reshape(1, D_INNER)
    nw = P["ssd_norm"][l].reshape(1, D_INNER)
    S["dsk"], S["nw"] = dsk, nw

    def gn_fn(yf, yb, xs, z, dsk, nw):
        y = yf + yb + xs * dsk
        return [y, _gated_norm_fwd(y, z, nw)], []

    y, yn = _rowwise("gated_norm_fwd", gn_fn,
                     [_row(y2, lead=0), _row(y2, lead=1), _row(act, D_INNER, 0), _row(proj, D_INNER, OFF_Z // D_INNER)],
                     [dsk, nw], [(D_INNER, F32), (D_INNER, BF16)])
    S["y"], S["yn"] = y, yn
    y_ssd = _matmul("ssd_out", yn, W["w_ssd_out"][l], "nn")
    S["y_ssd"] = y_ssd

    sink4 = jnp.broadcast_to(P["attn_sink"][l].reshape(ATTN_KV, REP, 1, 1), (ATTN_KV, REP, 1, LANES))
    S["qkv"], S["sink4"] = qkv, sink4
    o = _attn_fwd(qkv, bias4, sink4)
    S["o"] = o
    y_attn = _matmul("attn_out", o, W["w_attn_out"][l], "nn")
    S["y_attn"] = y_attn

    bg = P["b_gate"][l].reshape(1, 2 * D_MODEL)
    S["bg"] = bg

    def merge_fn(gates, ys, ya, b):
        g = jax.nn.sigmoid(gates + b)
        return [g[:, :D_MODEL] * ys + g[:, D_MODEL:] * ya], []

    (mix_in,) = _rowwise("merge_fwd", merge_fn, [_row(proj, 2 * D_MODEL, OFF_G // (2 * D_MODEL)), _row(y_ssd), _row(y_attn)],
                         [bg], [(D_MODEL, BF16)])
    S["mix_in"] = mix_in
    mixed = _matmul("w_o", mix_in, W["w_o"][l], "nn")
    S["mixed"] = mixed

    g_pm = P["post_mix_norm"][l].reshape(1, D_MODEL)
    g_pl = P["pre_mlp_norm"][l].reshape(1, D_MODEL)

    def postmix_fn(x, mixed, g1, g2):
        x2 = x + _rms_fwd(mixed, g1)
        return [x2, _rms_fwd(x2, g2)], []

    x2, h2 = _rowwise("post_mix_fwd", postmix_fn, [_row(x), _row(mixed)], [g_pm, g_pl], [(D_MODEL, F32), (D_MODEL, BF16)])
    S["x2"], S["h2"] = x2, h2
    a1 = _matmul("mlp_in", h2, W["w_mlp_in"][l], "nn", out_dtype=BF16,
                 epilogue=lambda acc: jnp.square(jnp.maximum(acc, 0.0)))
    S["a1"] = a1
    f2 = _matmul("mlp_out", a1, W["w_mlp_out"][l], "nn")
    S["f2"] = f2
    return S


def _layer_bwd(S, dx3, W, P, l, bias4, onehot_t):
    t = dx3.shape[0]
    G = {}
    g_pmlp = P["post_mlp_norm"][l].reshape(1, D_MODEL)

    def b1_fn(f2, dx3, g):
        df2, dg = _rms_bwd(f2, g, dx3)
        return [df2], [dg]

    df2, G["post_mlp_norm"] = _rowwise("post_mlp_bwd", b1_fn, [_row(S["f2"]), _row(dx3)], [g_pmlp], [(D_MODEL, BF16)], [(1, D_MODEL)])
    df1 = _matmul("d_f1", df2, W["w_mlp_out"][l], "nt", out_dtype=BF16,
                  epilogue=lambda acc, a1: acc * (2.0 * jnp.sqrt(a1.astype(F32))), extras=[S["a1"]])
    G["w_mlp_out"] = _matmul("dw_mlp_out", S["a1"], df2, "tn")
    dh2 = _matmul("d_h2", df1, W["w_mlp_in"][l], "nt")
    G["w_mlp_in"] = _matmul("dw_mlp_in", S["h2"], df1, "tn")

    g_pm = P["post_mix_norm"][l].reshape(1, D_MODEL)
    g_pl = P["pre_mlp_norm"][l].reshape(1, D_MODEL)

    def b3_fn(x2, dh2, dx3, mixed, g_pl, g_pm):
        d1, dgl = _rms_bwd(x2, g_pl, dh2)
        dx2 = dx3 + d1
        dmixed, dgm = _rms_bwd(mixed, g_pm, dx2)
        return [dx2, dmixed], [dgl, dgm]

    dx2, dmixed, G["pre_mlp_norm"], G["post_mix_norm"] = _rowwise(
        "post_mix_bwd", b3_fn, [_row(S["x2"]), _row(dh2), _row(dx3), _row(S["mixed"])], [g_pl, g_pm],
        [(D_MODEL, F32), (D_MODEL, BF16)], [(1, D_MODEL), (1, D_MODEL)])
    dmix_in = _matmul("d_mix_in", dmixed, W["w_o"][l], "nt")
    G["w_o"] = _matmul("dw_o", S["mix_in"], dmixed, "tn")

    proj = S["proj"]

    def b4_fn(gates, ys, ya, dmix, b):
        g = jax.nn.sigmoid(gates + b)
        gs, ga = g[:, :D_MODEL], g[:, D_MODEL:]
        dg = jnp.concatenate([ys * dmix, ya * dmix], axis=-1) * g * (1.0 - g)
        return [gs * dmix, ga * dmix, dg], [jnp.sum(dg, axis=0, keepdims=True)]

    dy_ssd, dy_attn, dgates, G["b_gate"] = _rowwise(
        "merge_bwd", b4_fn, [_row(proj, 2 * D_MODEL, OFF_G // (2 * D_MODEL)), _row(S["y_ssd"]), _row(S["y_attn"]), _row(dmix_in)],
        [S["bg"]], [(D_MODEL, BF16), (D_MODEL, BF16), (2 * D_MODEL, BF16)], [(1, 2 * D_MODEL)])

    dyn = _matmul("d_yn", dy_ssd, W["w_ssd_out"][l], "nt")
    G["w_ssd_out"] = _matmul("dw_ssd_out", S["yn"], dy_ssd, "tn")
    do = _matmul("d_o", dy_attn, W["w_attn_out"][l], "nt", out_dtype=BF16)
    G["w_attn_out"] = _matmul("dw_attn_out", S["o"], dy_attn, "tn")

    dq, dkp, dvp, dbias4, dsink4 = _attn_bwd(S["qkv"], bias4, S["sink4"], do)
    dk = _kv_combine("dk_combine", dkp)
    dv = _kv_combine("dv_combine", dvp)
    G["attn_sink"] = dsink4[:, :, 0, 0].reshape(ATTN_HEADS)
    G["rel_bias_table"] = _bias_table_grad(dbias4.reshape(ATTN_HEADS, BLOCK * 3 * BLOCK), onehot_t).T

    act = S["act"]

    def b5_fn(y, z, xs, dyn, nw, dsk):
        dy, dz, dnw = _gated_norm_bwd(y, z, nw, dyn)
        return [dy, dz], [dnw, jnp.sum(dy * xs, axis=0, keepdims=True)]

    dy, dz, G["ssd_norm"], dskip_cols = _rowwise(
        "gated_norm_bwd", b5_fn, [_row(S["y"]), _row(proj, D_INNER, OFF_Z // D_INNER), _row(act, D_INNER, 0), _row(dyn)],
        [S["nw"], S["dsk"]], [(D_INNER, F32), (D_INNER, BF16)], [(1, D_INNER), (1, D_INNER)])
    G["d_skip"] = dskip_cols.reshape(SSD_HEADS, SSD_HEAD_DIM).sum(axis=-1)

    dxs2, dbs2, dcs2, ddt2, da2 = _ssd_bwd(act, S["dt2"], S["dt2t"], S["a_row"], S["a_col"], S["dte"], dy, S["states"])
    G["a_log"] = da2.reshape(2, SSD_HEADS) * S["a_row"].reshape(2, SSD_HEADS)

    def b6_fn(dxf, dxb, dy, dbf, dbb, dcf, dcb, pre, dsk):
        dact = jnp.concatenate([dxf + dxb + dy * dsk, dbf + dbb, dcf + dcb], axis=-1)
        return [dact * _silu_grad(pre)], []

    (dpre,) = _rowwise("silu_bwd", b6_fn,
                       [_row(dxs2, lead=0), _row(dxs2, lead=1), _row(dy), _row(dbs2, lead=0), _row(dbs2, lead=1),
                        _row(dcs2, lead=0), _row(dcs2, lead=1), _row(S["pre"])], [S["dsk"]], [(CONV_DIM, F32)], tb=128)
    du, dconv_w, dconv_b = _conv_bwd(dpre, proj, W["conv_w"][l])
    G["conv_w"] = dconv_w.reshape(SSD_CONV, 1, CONV_DIM)
    G["conv_b"] = dconv_b.reshape(CONV_DIM)

    dtb = jnp.pad(P["dt_bias"][l].reshape(1, 2 * SSD_HEADS), ((0, 0), (0, LANES - 2 * SSD_HEADS)))
    ddt = jnp.pad(jnp.concatenate([ddt2[0], ddt2[1]], axis=-1), ((0, 0), (0, LANES - 2 * SSD_HEADS)))

    def b7_fn(raw, ddt, b):
        draw = ddt * jax.nn.sigmoid(raw + b)
        return [draw], [jnp.sum(draw, axis=0, keepdims=True)]

    draw, ddtb = _rowwise("dt_bwd", b7_fn, [_row(proj, LANES, OFF_DT // LANES), _row(ddt)], [dtb], [(LANES, BF16)], [(1, LANES)], tb=512)
    G["dt_bias"] = ddtb[0, :2 * SSD_HEADS].reshape(2, SSD_HEADS)

    dproj = jnp.concatenate([dz, dgates, du, draw, dq, dk, dv], axis=-1)
    G["w_in"] = _from_proj_layout(_matmul("dw_in", S["h1"], dproj, "tn"))
    dh1 = _matmul("d_h1", dproj, W["w_in"][l], "nt")

    g_pre = P["pre_mix_norm"][l].reshape(1, D_MODEL)

    def b8_fn(x, dh1, dx2, g):
        d1, dg = _rms_bwd(x, g, dh1)
        return [dx2 + d1], [dg]

    dx, G["pre_mix_norm"] = _rowwise("pre_mix_bwd", b8_fn, [_row(S["x"]), _row(dh1), _row(dx2)], [g_pre], [(D_MODEL, F32)], [(1, D_MODEL)])
    return dx, G


def _reduce_scatter(tag, pieces):
    theirs = _swap_cores("swap_" + tag, pieces)
    chip_sums = []
    for i, p in enumerate(pieces):
        _, _, r2, c = p.shape
        mine = lax.dynamic_index_in_dim(p, lax.axis_index("c"), axis=0, keepdims=False).reshape(4 * r2, c)
        (cs,) = _rowwise(f"presum_{tag}{i}", lambda a, b: ([a.astype(F32) + b.astype(F32)], []),
                         [_row(mine), _row(theirs[i].reshape(4 * r2, c))], [], [(c, BF16)])
        chip_sums.append(cs.reshape(4, r2, c))
    staged = _scatter_chips("scatter_" + tag, chip_sums)
    halves = [_sum_slots(f"sum_{tag}{i}", st) for i, st in enumerate(staged)]
    shared = _allgather_cores("share_" + tag, halves)
    return [sh.reshape(2 * sh.shape[1], sh.shape[2]) for sh in shared]


def _step(x, target, shard_w, shard_m, shard_v):
    depth, _, win_cols = shard_w["w_in"].shape
    win_rows = depth * D_MODEL

    def win2d(a):
        return a.reshape(win_rows, win_cols)

    conv_rows = shard_w["conv_w"].reshape(-1, D_MODEL)
    win_g, gathered, conv_g = _allgather_chips("gather_weights", [
        win2d(shard_w["w_in"]).astype(BF16), _pack_big(shard_w, BF16), jnp.pad(conv_rows, ((0, 16 - conv_rows.shape[0]), (0, 0)))])
    per_chip = [_unpack_big(gathered[s], shard_w) for s in range(4)]
    W = {n: _full_of_shards(n, [per_chip[s][n] for s in range(4)]) for n in BIG if n != "conv_w"}
    W["w_in"] = _to_proj_layout(jnp.concatenate([win_g[s].reshape(depth, D_MODEL, win_cols) for s in range(4)], axis=2))
    W["conv_w"] = jnp.concatenate([conv_g[s][:conv_rows.shape[0]].reshape(shard_w["conv_w"].shape) for s in range(4)],
                                  axis=3).reshape(depth, SSD_CONV, CONV_DIM)
    P = {n: shard_w[n] for n in SMALL}
    loss_part, grad_x, full = _local_step(x, target, W, P)

    assert depth == 2
    win_pieces = jnp.stack([jnp.stack([full["w_in"][h][:, s * win_cols:(s + 1) * win_cols] for s in range(4)])
                            for h in range(2)]).astype(BF16)
    packs = [_pack_big({n: _shard_of_full(n, full[n], s) for n in BIG}, BF16) for s in range(4)]
    r2 = BIG_ROWS // 2
    pieces = jnp.stack([jnp.stack([p[h * r2:(h + 1) * r2] for p in packs]) for h in range(2)])
    g_win, g_big = _reduce_scatter("grads", [win_pieces, pieces])
    d_win, m_win, v_win = _adamw("adamw_w_in", win2d(shard_w["w_in"]), g_win, win2d(shard_m["w_in"]), win2d(shard_v["w_in"]))
    d_big, m_big, v_big = _adamw("adamw_big", _pack_big(shard_w, F32), g_big, _pack_big(shard_m, F32), _pack_big(shard_v, F32))

    small = _allgather_all("gather_small", _pack_small(full, loss_part))
    g_small = _sum_slots("sum_small", small, tb=SMALL_ROWS)
    d_small, m_small, v_small = _adamw("adamw_small", _pack_small(shard_w, jnp.zeros((), F32)), g_small,
                                       _pack_small(shard_m, jnp.zeros((), F32)), _pack_small(shard_v, jnp.zeros((), F32)))

    outs = {}
    for tag, win, big, sm in (("grad", g_win, g_big, g_small), ("delta", d_win, d_big, d_small),
                              ("new_m", m_win, m_big, m_small), ("new_v", v_win, v_big, v_small)):
        ub = _unpack_big(big, shard_w)
        us, extra = _unpack_small(sm, shard_w)
        outs[tag] = {"w_in": win.reshape(shard_w["w_in"].shape), **ub, **us}
        if tag == "grad":
            loss = extra
    return loss, grad_x, outs


def _local_step(x, target, W, P):
    depth = W["w_in"].shape[0]
    W = dict(W, w_in_main=W["w_in"][:, :, :N_MAIN], w_in_qkv=W["w_in"][:, :, N_MAIN:])
    onehot_t = (_bucket_map().reshape(1, -1) == jnp.arange(N_BUCKETS)[:, None]).astype(F32)
    bias = _bias_from_table(P["rel_bias_table"], onehot_t).reshape(ATTN_HEADS, BLOCK, 3 * BLOCK)
    bias4 = jnp.where(_band_mask(), bias, NEG).reshape(ATTN_KV, REP, BLOCK, 3 * BLOCK)

    def pre_fn(x, g):
        return [_rms_fwd(x, g)], []

    (h1,) = _rowwise("pre_mix_fwd", pre_fn, [_row(x)], [P["pre_mix_norm"][0].reshape(1, D_MODEL)], [(D_MODEL, BF16)])
    saved = []
    loss_cols = dxl = None
    for l in range(depth):
        S = _layer_fwd(h1, x, W, P, l, bias4)
        saved.append(S)
        g_pmlp = P["post_mlp_norm"][l].reshape(1, D_MODEL)
        if l + 1 < depth:
            def post_fn(x2, f2, g1, g2):
                x3 = x2 + _rms_fwd(f2, g1)
                return [x3, _rms_fwd(x3, g2)], []

            x, h1 = _rowwise("post_mlp_fwd", post_fn, [_row(S["x2"]), _row(S["f2"])],
                             [g_pmlp, P["pre_mix_norm"][l + 1].reshape(1, D_MODEL)], [(D_MODEL, F32), (D_MODEL, BF16)])
        else:
            def loss_fn(x2, f2, tgt, g1):
                diff = x2 + _rms_fwd(f2, g1) - tgt
                return [diff * (1.0 / D_MODEL)], [jnp.sum(diff * diff, axis=0, keepdims=True)]

            dxl, loss_cols = _rowwise("loss", loss_fn, [_row(S["x2"]), _row(S["f2"]), _row(target)], [g_pmlp],
                                      [(D_MODEL, F32)], [(1, D_MODEL)])
    loss_part = 0.5 * jnp.sum(loss_cols) / D_MODEL

    grads = [None] * depth
    dx = dxl
    for l in reversed(range(depth)):
        dx, grads[l] = _layer_bwd(saved[l], dx, W, P, l, bias4, onehot_t)
    grad_x = dx

    full = {n: jnp.stack([grads[l][n] for l in range(depth)]) for n in ALL_W if n != "rel_bias_table"}
    full["rel_bias_table"] = sum(grads[l]["rel_bias_table"] for l in range(depth))
    return loss_part, grad_x, full


def kernel(x, pre_mix_norm, w_in, b_gate, conv_w, conv_b, dt_bias, a_log, d_skip, ssd_norm, w_ssd_out, attn_sink, rel_bias_table, w_attn_out, w_o, post_mix_norm, pre_mlp_norm, w_mlp_in, w_mlp_out, post_mlp_norm, loss_target, m_pre_mix_norm, m_w_in, m_b_gate, m_conv_w, m_conv_b, m_dt_bias, m_a_log, m_d_skip, m_ssd_norm, m_w_ssd_out, m_attn_sink, m_rel_bias_table, m_w_attn_out, m_w_o, m_post_mix_norm, m_pre_mlp_norm, m_w_mlp_in, m_w_mlp_out, m_post_mlp_norm, v_pre_mix_norm, v_w_in, v_b_gate, v_conv_w, v_conv_b, v_dt_bias, v_a_log, v_d_skip, v_ssd_norm, v_w_ssd_out, v_attn_sink, v_rel_bias_table, v_w_attn_out, v_w_o, v_post_mix_norm, v_pre_mlp_norm, v_w_mlp_in, v_w_mlp_out, v_post_mlp_norm):
    a = locals()
    shard_w = {n: a[n] for n in ALL_W}
    shard_m = {n: a["m_" + n] for n in ALL_W}
    shard_v = {n: a["v_" + n] for n in ALL_W}
    loss, grad_x, outs = _step(x[0], loss_target[0], shard_w, shard_m, shard_v)
    return (loss, grad_x[None], *[outs["grad"][n] for n in ALL_W], *[outs["delta"][n] for n in ALL_W],
            *[outs["new_m"][n] for n in ALL_W], *[outs["new_v"][n] for n in ALL_W])
```

```python
import math

import jax
import jax.numpy as jnp
from jax import lax
from jax.experimental import pallas as pl
from jax.experimental.pallas import tpu as pltpu

F32 = jnp.float32
BF16 = jnp.bfloat16
MESH = pl.DeviceIdType.MESH

VMEM_LIMIT_BYTES = 52 * 1024 * 1024
LANES = 128
SUBLANES = 8

EPS = 1e-6
D_MODEL = 1024
D_INNER = 2048
SSD_HEADS = 32
SSD_HEAD_DIM = 64
SSD_GROUPS = 8
SSD_STATE = 128
SSD_CONV = 5
CHUNK = 128
CONV_DIM = 4096
ATTN_HEADS = 16
ATTN_KV = 4
ATTN_DIM = 64
BLOCK = 128
N_BUCKETS = 32
MAX_DISTANCE = 128
D_FF = 4096
N_IN = 9792
NEG = -1e30

OFF_Z, OFF_G, OFF_XBC, OFF_DT, N_MAIN, N_PROJ = 0, 2048, 4096, 8192, 8320, 10368
N_QKV = N_PROJ - N_MAIN

ADAM_LR, ADAM_B1, ADAM_B2, ADAM_EPS, ADAM_WD, ADAM_STEP = 0.001, 0.9, 0.999, 1e-08, 0.01, 10

BIG_ROWS = 6656
SMALL_ROWS = 32


def _cparams(*sem):
    return pltpu.CompilerParams(dimension_semantics=sem, vmem_limit_bytes=VMEM_LIMIT_BYTES)


def _dot(a, b, precision=None):
    return lax.dot_general(a, b, (((1,), (0,)), ((), ())), preferred_element_type=F32, precision=precision)


def _dot_nt(a, b, precision=None):
    return lax.dot_general(a, b, (((1,), (1,)), ((), ())), preferred_element_type=F32, precision=precision)


def _dot_tn(a, b):
    return lax.dot_general(a, b, (((0,), (0,)), ((), ())), preferred_element_type=F32)


def _pick(n, prefs):
    for p in prefs:
        if n % p == 0:
            return p
    return n


def _matmul(name, a, b, mode, out_dtype=F32, epilogue=None, extras=()):
    if mode == "nn":
        (m, k), (_, n) = a.shape, b.shape
    elif mode == "nt":
        (m, k), (n, _) = a.shape, b.shape
    else:
        (k, m), (_, n) = a.shape, b.shape
    tm = _pick(m, (512, 256, 128)) if mode == "tn" else _pick(m, (1024, 512, 256, 128))
    tn = _pick(n, (1024, 1152, 640, 512, 256, 128))
    tk = _pick(k, (1024, 3456, 512, 256, 128)) if mode != "tn" else _pick(k, (2048, 512, 256, 128))
    if tk > 2048:
        tm = _pick(m, (512, 256, 128))
    nk = k // tk
    if mode == "nn":
        a_spec = pl.BlockSpec((tm, tk), lambda i, j, q: (i, q))
        b_spec = pl.BlockSpec((tk, tn), lambda i, j, q: (q, j))
        fn = _dot
    elif mode == "nt":
        a_spec = pl.BlockSpec((tm, tk), lambda i, j, q: (i, q))
        b_spec = pl.BlockSpec((tn, tk), lambda i, j, q: (j, q))
        fn = _dot_nt
    else:
        a_spec = pl.BlockSpec((tk, tm), lambda i, j, q: (q, i))
        b_spec = pl.BlockSpec((tk, tn), lambda i, j, q: (q, j))
        fn = _dot_tn

    tile = pl.BlockSpec((tm, tn), lambda i, j, q: (i, j))
    n_ex = len(extras)

    def body(a_ref, b_ref, *rest):
        ex_refs, o_ref = rest[:n_ex], rest[n_ex]

        def store(acc):
            v = acc if epilogue is None else epilogue(acc, *[r[...] for r in ex_refs])
            o_ref[...] = v.astype(o_ref.dtype)

        p = fn(a_ref[...].astype(BF16), b_ref[...].astype(BF16))
        if nk == 1:
            store(p)
        else:
            acc_ref = rest[n_ex + 1]
            q = pl.program_id(2)

            @pl.when(q == 0)
            def _():
                acc_ref[...] = p

            @pl.when((q > 0) & (q < nk - 1))
            def _():
                acc_ref[...] += p

            @pl.when(q == nk - 1)
            def _():
                store(acc_ref[...] + p)

    return pl.pallas_call(
        body, name=name, grid=(m // tm, n // tn, nk),
        in_specs=[a_spec, b_spec] + [tile] * n_ex, out_specs=tile,
        out_shape=jax.ShapeDtypeStruct((m, n), out_dtype),
        scratch_shapes=[pltpu.VMEM((tm, tn), F32)] if nk > 1 else [],
        compiler_params=_cparams("parallel", "parallel", "arbitrary"),
    )(a, b, *extras)


def _row(arr, width=None, cb=0, lead=None):
    return (arr, width, cb, lead)


def _rowwise(name, fn, rows, vecs, outs, accs=(), tb=256):
    t = rows[0][0].shape[-2]
    tb = min(tb, t)
    in_specs, args = [], []
    for arr, width, cb, lead in rows:
        w = arr.shape[-1] if width is None else width
        if lead is None:
            in_specs.append(pl.BlockSpec((tb, w), lambda i, cb=cb: (i, cb)))
        else:
            in_specs.append(pl.BlockSpec((None, tb, w), lambda i, cb=cb, lead=lead: (lead, i, cb)))
        args.append(arr)
    for v in vecs:
        in_specs.append(pl.BlockSpec(v.shape, lambda i, nd=v.ndim: (0,) * nd))
        args.append(v)
    out_shape = [jax.ShapeDtypeStruct((t, c), dt) for c, dt in outs] + [jax.ShapeDtypeStruct(s, F32) for s in accs]
    out_specs = [pl.BlockSpec((tb, c), lambda i: (i, 0)) for c, _ in outs] + [pl.BlockSpec(s, lambda i: (0, 0)) for s in accs]
    n_in, n_out = len(args), len(outs)

    def body(*refs):
        vals = [r[...] for r in refs[:n_in]]
        o_vals, a_vals = fn(*vals)
        for r, v in zip(refs[n_in:n_in + n_out], o_vals):
            r[...] = v.astype(r.dtype)
        first = pl.program_id(0) == 0
        for r, v in zip(refs[n_in + n_out:], a_vals):
            @pl.when(first)
            def _(r=r, v=v):
                r[...] = v

            @pl.when(jnp.logical_not(first))
            def _(r=r, v=v):
                r[...] += v

    res = pl.pallas_call(
        body, name=name, grid=(t // tb,), in_specs=in_specs, out_specs=out_specs, out_shape=out_shape,
        compiler_params=_cparams("arbitrary"),
    )(*args)
    return res


def _rms_fwd(x, g):
    r = lax.rsqrt(jnp.mean(x * x, axis=-1, keepdims=True) + EPS)
    return x * r * g


def _rms_bwd(x, g, dy):
    r = lax.rsqrt(jnp.mean(x * x, axis=-1, keepdims=True) + EPS)
    xh = x * r
    dxh = dy * g
    dx = r * (dxh - xh * jnp.mean(dxh * xh, axis=-1, keepdims=True))
    return dx, jnp.sum(dy * xh, axis=0, keepdims=True)


def _silu(x):
    return x * jax.nn.sigmoid(x)


def _silu_grad(x):
    s = jax.nn.sigmoid(x)
    return s * (1.0 + x * (1.0 - s))


GROUP_W = D_INNER // SSD_GROUPS


def _gated_norm_fwd(y, z, w):
    u = y * _silu(z)
    parts = []
    for j in range(SSD_GROUPS):
        ug = u[:, j * GROUP_W:(j + 1) * GROUP_W]
        parts.append(ug * lax.rsqrt(jnp.mean(ug * ug, axis=-1, keepdims=True) + EPS))
    return jnp.concatenate(parts, axis=-1) * w


def _gated_norm_bwd(y, z, w, dyn):
    sz = _silu(z)
    u = y * sz
    duh = dyn * w
    du_parts, uh_parts = [], []
    for j in range(SSD_GROUPS):
        sl = slice(j * GROUP_W, (j + 1) * GROUP_W)
        ug = u[:, sl]
        r = lax.rsqrt(jnp.mean(ug * ug, axis=-1, keepdims=True) + EPS)
        uh = ug * r
        dg = duh[:, sl]
        du_parts.append(r * (dg - uh * jnp.mean(dg * uh, axis=-1, keepdims=True)))
        uh_parts.append(uh)
    du = jnp.concatenate(du_parts, axis=-1)
    uh = jnp.concatenate(uh_parts, axis=-1)
    dw = jnp.sum(dyn * uh, axis=0, keepdims=True)
    return du * sz, du * y * _silu_grad(z), dw


HALO = SUBLANES


def _halo_specs(tb, cb, col0, t):
    nblk8 = t // HALO
    per = tb // HALO
    main = pl.BlockSpec((tb, cb), lambda j, i: (i, col0 + j))
    prev = pl.BlockSpec((HALO, cb), lambda j, i: (jnp.maximum(i * per - 1, 0), col0 + j))
    nxt = pl.BlockSpec((HALO, cb), lambda j, i: (jnp.minimum((i + 1) * per, nblk8 - 1), col0 + j))
    return main, prev, nxt


def _fill_ext(ext_ref, cur_ref, prev_ref, next_ref, tb, ni):
    i = pl.program_id(1)
    ext_ref[0:HALO, :] = jnp.where(i > 0, prev_ref[...], 0.0)
    ext_ref[HALO:HALO + tb, :] = cur_ref[...]
    ext_ref[HALO + tb:HALO + tb + HALO, :] = jnp.where(i < ni - 1, next_ref[...], 0.0)


def _conv_fwd(proj, w, b):
    t = proj.shape[0]
    tb, cb = min(512, t), 512
    ni, nj = t // tb, CONV_DIM // cb
    main, prev, nxt = _halo_specs(tb, cb, OFF_XBC // cb, t)
    pad = (SSD_CONV - 1) // 2

    def body(u_ref, up_ref, un_ref, w_ref, b_ref, pre_ref, act_ref, ext_ref):
        _fill_ext(ext_ref, u_ref, up_ref, un_ref, tb, ni)
        acc = jnp.broadcast_to(b_ref[...], (tb, cb))
        for k in range(SSD_CONV):
            acc = acc + w_ref[k:k + 1, :] * ext_ref[pl.ds(HALO + k - pad, tb), :]
        pre_ref[...] = acc
        act_ref[...] = _silu(acc)

    out = pl.BlockSpec((tb, cb), lambda j, i: (i, j))
    return pl.pallas_call(
        body, name="conv_fwd", grid=(nj, ni),
        in_specs=[main, prev, nxt, pl.BlockSpec((SSD_CONV, cb), lambda j, i: (0, j)), pl.BlockSpec((1, cb), lambda j, i: (0, j))],
        out_specs=[out, out],
        out_shape=[jax.ShapeDtypeStruct((t, CONV_DIM), F32)] * 2,
        scratch_shapes=[pltpu.VMEM((tb + 2 * HALO, cb), F32)],
        compiler_params=_cparams("parallel", "arbitrary"),
    )(proj, proj, proj, w, b)


def _conv_bwd(dpre, proj, w):
    t = proj.shape[0]
    tb, cb = min(512, t), 512
    ni, nj = t // tb, CONV_DIM // cb
    umain, uprev, unext = _halo_specs(tb, cb, OFF_XBC // cb, t)
    dmain, dprev, dnext = _halo_specs(tb, cb, 0, t)
    pad = (SSD_CONV - 1) // 2

    def body(d_ref, dp_ref, dn_ref, u_ref, up_ref, un_ref, w_ref, du_ref, dw_ref, db_ref, extd_ref, extu_ref):
        _fill_ext(extd_ref, d_ref, dp_ref, dn_ref, tb, ni)
        _fill_ext(extu_ref, u_ref, up_ref, un_ref, tb, ni)
        d = d_ref[...]
        du = jnp.zeros((tb, cb), F32)
        @pl.when(pl.program_id(1) == 0)
        def _():
            dw_ref[...] = jnp.zeros_like(dw_ref)
            db_ref[...] = jnp.zeros_like(db_ref)

        for k in range(SSD_CONV):
            du = du + w_ref[k:k + 1, :] * extd_ref[pl.ds(HALO - k + pad, tb), :]
            dw_ref[k:k + 1, :] += jnp.sum(d * extu_ref[pl.ds(HALO + k - pad, tb), :], axis=0, keepdims=True)
        du_ref[...] = du.astype(du_ref.dtype)
        db_ref[...] += jnp.sum(d, axis=0, keepdims=True)

    return pl.pallas_call(
        body, name="conv_bwd", grid=(nj, ni),
        in_specs=[dmain, dprev, dnext, umain, uprev, unext, pl.BlockSpec((SSD_CONV, cb), lambda j, i: (0, j))],
        out_specs=[pl.BlockSpec((tb, cb), lambda j, i: (i, j)), pl.BlockSpec((SSD_CONV, cb), lambda j, i: (0, j)),
                   pl.BlockSpec((1, cb), lambda j, i: (0, j))],
        out_shape=[jax.ShapeDtypeStruct((t, CONV_DIM), BF16), jax.ShapeDtypeStruct((SSD_CONV, CONV_DIM), F32),
                   jax.ShapeDtypeStruct((1, CONV_DIM), F32)],
        scratch_shapes=[pltpu.VMEM((tb + 2 * HALO, cb), F32)] * 2,
        compiler_params=_cparams("parallel", "arbitrary"),
    )(dpre, dpre, dpre, proj, proj, proj, w)


PAIR = 2 * SSD_HEAD_DIM
HI = lax.Precision.HIGHEST


def _ssd_prelude(d, dt_ref, dtt_ref, ar_ref, ac_ref):
    li = lax.broadcasted_iota(jnp.int32, (CHUNK, CHUNK), 0)
    si = lax.broadcasted_iota(jnp.int32, (CHUNK, CHUNK), 1)
    fwd = d == 0
    hi, lo = jnp.where(fwd, li, si), jnp.where(fwd, si, li)
    tri = hi >= lo
    trif = tri.astype(F32)
    trit = (hi <= lo).astype(F32)
    dt = dt_ref[...]
    adt = dt * ar_ref[...]
    adtt = dtt_ref[...] * ac_ref[...]
    p = _dot(trif, adt, HI)
    pt = _dot_nt(adtt, trif, HI)
    tot = jnp.sum(adt, axis=0, keepdims=True)
    return tri, trit, dt, p, pt, tot


def _ssd_specs(nc, rev):
    def cidx(d, c):
        up = (d == 1) if rev else (d == 0)
        return jnp.where(up, c, nc - 1 - c)

    specs = [
        pl.BlockSpec((CHUNK, D_INNER), lambda d, c: (cidx(d, c), 0)),
        pl.BlockSpec((CHUNK, 1024), lambda d, c: (cidx(d, c), 2)),
        pl.BlockSpec((CHUNK, 1024), lambda d, c: (cidx(d, c), 3)),
        pl.BlockSpec((None, CHUNK, SSD_HEADS), lambda d, c: (d, cidx(d, c), 0)),
        pl.BlockSpec((None, SSD_HEADS, CHUNK), lambda d, c: (d, 0, cidx(d, c))),
        pl.BlockSpec((None, 1, SSD_HEADS), lambda d, c: (d, 0, 0)),
        pl.BlockSpec((None, SSD_HEADS, 1), lambda d, c: (d, 0, 0)),
        pl.BlockSpec((CHUNK, D_INNER), lambda d, c: (cidx(d, c), d)),
    ]
    return cidx, specs


def _head_decay(tri, p, pt, tot, h):
    pb = jnp.broadcast_to(p[:, h:h + 1], (CHUNK, CHUNK))
    dec = jnp.exp(jnp.where(tri, pb - pt[h:h + 1, :], NEG))
    return dec, jnp.exp(tot[:, h:h + 1] - pb), jnp.exp(pb)


def _ssd_fwd(act, dt2, dt2t, a_row, a_col, dte):
    t = act.shape[0]
    nc = t // CHUNK
    cidx, specs = _ssd_specs(nc, rev=False)

    def body(xs_ref, bs_ref, cs_ref, dt_ref, dtt_ref, ar_ref, ac_ref, dte_ref, y_ref, st_ref, h_ref):
        d, c = pl.program_id(0), pl.program_id(1)

        @pl.when(c == 0)
        def _():
            h_ref[...] = jnp.zeros_like(h_ref)

        st_ref[...] = h_ref[...]
        tri, _, _, p, pt, tot = _ssd_prelude(d, dt_ref, dtt_ref, ar_ref, ac_ref)
        etot = jnp.exp(tot)
        lane = lax.broadcasted_iota(jnp.int32, (CHUNK, PAIR), 1) < SSD_HEAD_DIM
        rowh = lax.broadcasted_iota(jnp.int32, (PAIR, SSD_STATE), 0) < SSD_HEAD_DIM
        for g in range(SSD_GROUPS):
            gs = slice(g * SSD_STATE, (g + 1) * SSD_STATE)
            bg = bs_ref[:, gs]
            cb = cs_ref[:, gs].astype(BF16)
            cbm = _dot_nt(cb, bg.astype(BF16))
            for pr in range(2):
                h0 = g * 4 + pr * 2
                h1 = h0 + 1
                sl = slice(h0 * SSD_HEAD_DIM, h0 * SSD_HEAD_DIM + PAIR)
                xdt = (xs_ref[:, sl] * dte_ref[:, sl]).astype(BF16)
                yd, st, epb = [], [], []
                for h in (h0, h1):
                    dec, wb, eb = _head_decay(tri, p, pt, tot, h)
                    yd.append(_dot((cbm * dec).astype(BF16), xdt))
                    st.append(_dot_tn(xdt, (bg * wb).astype(BF16)))
                    epb.append(eb)
                hin = h_ref[sl, :]
                yo = _dot_nt(cb, hin.astype(BF16)) * jnp.where(lane, epb[0], epb[1])
                y_ref[:, sl] = jnp.where(lane, yd[0], yd[1]) + yo
                et = jnp.where(rowh, etot[:, h0:h0 + 1], etot[:, h1:h1 + 1])
                h_ref[sl, :] = hin * et + jnp.where(rowh, st[0], st[1])

    return pl.pallas_call(
        body, name="ssd_fwd", grid=(2, nc), in_specs=specs,
        out_specs=[pl.BlockSpec((None, CHUNK, D_INNER), lambda d, c: (d, cidx(d, c), 0)),
                   pl.BlockSpec((None, None, D_INNER, SSD_STATE), lambda d, c: (d, cidx(d, c), 0, 0))],
        out_shape=[jax.ShapeDtypeStruct((2, t, D_INNER), F32), jax.ShapeDtypeStruct((2, nc, D_INNER, SSD_STATE), F32)],
        scratch_shapes=[pltpu.VMEM((D_INNER, SSD_STATE), F32)],
        compiler_params=_cparams("arbitrary", "arbitrary"),
    )(act, act, act, dt2, dt2t, a_row, a_col, dte)


def _put_col(acc, col, h):
    lane = lax.broadcasted_iota(jnp.int32, acc.shape, 1)
    return jnp.where(lane == h, col, acc)


def _put_row(acc, row, h):
    sub = lax.broadcasted_iota(jnp.int32, acc.shape, 0)
    return jnp.where(sub == h, row, acc)


def _sum_all(x):
    return jnp.sum(jnp.sum(x, axis=0, keepdims=True), axis=1, keepdims=True)


def _ssd_bwd(act, dt2, dt2t, a_row, a_col, dte, dy, states):
    t = act.shape[0]
    nc = t // CHUNK
    cidx, specs = _ssd_specs(nc, rev=True)
    specs = specs + [
        pl.BlockSpec((CHUNK, D_INNER), lambda d, c: (cidx(d, c), 0)),
        pl.BlockSpec((None, None, D_INNER, SSD_STATE), lambda d, c: (d, cidx(d, c), 0, 0)),
    ]

    def body(xs_ref, bs_ref, cs_ref, dt_ref, dtt_ref, ar_ref, ac_ref, dte_ref, dy_ref, st_ref,
             dxs_ref, dbs_ref, dcs_ref, ddt_ref, da_ref, dh_ref):
        d, c = pl.program_id(0), pl.program_id(1)

        @pl.when(c == 0)
        def _():
            dh_ref[...] = jnp.zeros_like(dh_ref)
            da_ref[...] = jnp.zeros_like(da_ref)

        tri, trit, dt, p, pt, tot = _ssd_prelude(d, dt_ref, dtt_ref, ar_ref, ac_ref)
        etot = jnp.exp(tot)
        lane = lax.broadcasted_iota(jnp.int32, (CHUNK, PAIR), 1) < SSD_HEAD_DIM
        rowh = lax.broadcasted_iota(jnp.int32, (PAIR, SSD_STATE), 0) < SSD_HEAD_DIM
        first_head = lax.broadcasted_iota(jnp.int32, (PAIR, LANES), 0) < SSD_HEAD_DIM
        out_lane = lax.broadcasted_iota(jnp.int32, (PAIR, LANES), 1)
        ddtx = jnp.zeros((CHUNK, LANES), F32)
        lane32 = lax.broadcasted_iota(jnp.int32, (CHUNK, SSD_HEADS), 1)
        dp_col = jnp.zeros((CHUNK, SSD_HEADS), F32)
        dp_row = jnp.zeros((SSD_HEADS, CHUNK), F32)
        dtot = jnp.zeros((1, SSD_HEADS), F32)
        for g in range(SSD_GROUPS):
            gs = slice(g * SSD_STATE, (g + 1) * SSD_STATE)
            bg = bs_ref[:, gs]
            bb = bg.astype(BF16)
            cb = cs_ref[:, gs].astype(BF16)
            cbm = _dot_nt(cb, bb)
            dcb = jnp.zeros((CHUNK, CHUNK), F32)
            dc_acc = jnp.zeros((CHUNK, SSD_STATE), F32)
            db_acc = jnp.zeros((CHUNK, SSD_STATE), F32)
            for pr in range(2):
                h0 = g * 4 + pr * 2
                h1 = h0 + 1
                sl = slice(h0 * SSD_HEAD_DIM, h0 * SSD_HEAD_DIM + PAIR)
                xp = xs_ref[:, sl]
                dtp = dte_ref[:, sl]
                xdt_f = xp * dtp
                xdt = xdt_f.astype(BF16)
                dyp = dy_ref[:, sl]
                dyb = dyp.astype(BF16)
                hin = st_ref[sl, :]
                dh = dh_ref[sl, :]
                hb = hin.astype(BF16)
                dhb = dh.astype(BF16)
                heads = [_head_decay(tri, p, pt, tot, h) for h in (h0, h1)]
                dye = dyp * jnp.where(lane, heads[0][2], heads[1][2])
                dyeb = dye.astype(BF16)
                gy = _dot_nt(cb, hb) * dye
                dc_acc = dc_acc + _dot(dyeb, hb)
                dhin = _dot_tn(dyeb, cb)
                hh = dh * hin
                dxdt = jnp.zeros((CHUNK, PAIR), F32)
                for idx, h in enumerate((h0, h1)):
                    hm = lane if idx == 0 else jnp.logical_not(lane)
                    rm = rowh if idx == 0 else jnp.logical_not(rowh)
                    dec, wb, _ = heads[idx]
                    mf = cbm * dec
                    t1 = _dot_tn(mf.astype(BF16), dyb)
                    t2 = _dot_nt((bg * wb).astype(BF16), dhb)
                    dxdt = jnp.where(hm, t1 + t2, dxdt)
                    dm = _dot_nt(jnp.where(hm, dyp, 0.0).astype(BF16), xdt)
                    dcb = dcb + dm * dec
                    e = dm * mf
                    qw = _dot(jnp.where(hm, xdt_f, 0.0).astype(BF16), dhb) * wb
                    db_acc = db_acc + qw
                    qwb = qw * bg
                    col = jnp.sum(e + jnp.where(hm, gy, 0.0) - qwb, axis=1, keepdims=True)
                    dp_col = jnp.where(lane32 == h, col, dp_col)
                    dp_row = _put_row(dp_row, -jnp.sum(e, axis=0, keepdims=True), h)
                    dtot_h = _sum_all(qwb) + etot[:, h:h + 1] * _sum_all(jnp.where(rm, hh, 0.0))
                    dtot = _put_col(dtot, dtot_h, h)
                dxs_ref[:, sl] = dxdt * dtp
                ddx = dxdt * xp
                ddx_hi = ddx.astype(BF16)
                ddx_lo = (ddx - ddx_hi.astype(F32)).astype(BF16)
                route = (out_lane == jnp.where(first_head, h0, h1)).astype(BF16)
                ddtx = ddtx + _dot(ddx_hi, route) + _dot(ddx_lo, route)
                et = jnp.where(rowh, etot[:, h0:h0 + 1], etot[:, h1:h1 + 1])
                dh_ref[sl, :] = dh * et + dhin
            dcbb = dcb.astype(BF16)
            dcs_ref[:, gs] = _dot(dcbb, bb) + dc_acc
            dbs_ref[:, gs] = _dot_tn(dcbb, cb) + db_acc
        d_adt = _dot(trit, dp_col, HI) + _dot_nt(trit, dp_row, HI) + dtot
        ddt_ref[...] = ddtx[:, :SSD_HEADS] + ar_ref[...] * d_adt
        da_ref[...] += jnp.sum(dt * d_adt, axis=0, keepdims=True)

    return pl.pallas_call(
        body, name="ssd_bwd", grid=(2, nc), in_specs=specs,
        out_specs=[pl.BlockSpec((None, CHUNK, D_INNER), lambda d, c: (d, cidx(d, c), 0)),
                   pl.BlockSpec((None, CHUNK, 1024), lambda d, c: (d, cidx(d, c), 0)),
                   pl.BlockSpec((None, CHUNK, 1024), lambda d, c: (d, cidx(d, c), 0)),
                   pl.BlockSpec((None, CHUNK, SSD_HEADS), lambda d, c: (d, cidx(d, c), 0)),
                   pl.BlockSpec((None, 1, SSD_HEADS), lambda d, c: (d, 0, 0))],
        out_shape=[jax.ShapeDtypeStruct((2, t, D_INNER), F32), jax.ShapeDtypeStruct((2, t, 1024), F32),
                   jax.ShapeDtypeStruct((2, t, 1024), F32), jax.ShapeDtypeStruct((2, t, SSD_HEADS), F32),
                   jax.ShapeDtypeStruct((2, 1, SSD_HEADS), F32)],
        scratch_shapes=[pltpu.VMEM((D_INNER, SSD_STATE), F32)],
        compiler_params=_cparams("arbitrary", "arbitrary"),
    )(act, act, act, dt2, dt2t, a_row, a_col, dte, dy, states)


REP = ATTN_HEADS // ATTN_KV
SCALE = ATTN_DIM ** -0.5
GROUP_Q = REP * ATTN_DIM
K_BLK0 = D_MODEL // LANES
V_BLK0 = K_BLK0 + ATTN_KV


def _attn_specs(nb):
    q = pl.BlockSpec((BLOCK, GROUP_Q), lambda g, n: (n, g))

    def kv(blk0):
        return [pl.BlockSpec((BLOCK, LANES), lambda g, n: (jnp.maximum(n - 1, 0), blk0 + g)),
                pl.BlockSpec((BLOCK, LANES), lambda g, n: (n, blk0 + g)),
                pl.BlockSpec((BLOCK, LANES), lambda g, n: (jnp.minimum(n + 1, nb - 1), blk0 + g))]

    bias = pl.BlockSpec((None, REP, BLOCK, 3 * BLOCK), lambda g, n: (g, 0, 0, 0))
    sink = pl.BlockSpec((None, REP, 1, LANES), lambda g, n: (g, 0, 0, 0))
    return q, kv(K_BLK0), kv(V_BLK0), bias, sink


def _band_mask():
    ii = lax.broadcasted_iota(jnp.int32, (BLOCK, 3 * BLOCK), 0)
    jj = lax.broadcasted_iota(jnp.int32, (BLOCK, 3 * BLOCK), 1)
    return (jj >= ii) & (jj - 2 * BLOCK <= ii)


def _attn_valid(n, nb):
    jj = lax.broadcasted_iota(jnp.int32, (1, 3 * BLOCK), 1)
    return ((jj >= BLOCK) | (n > 0)) & ((jj < 2 * BLOCK) | (n < nb - 1))


def _attn_probs(q, kcat, bias, snk, valid):
    s = jnp.where(valid, _dot_nt(q, kcat) + bias, NEG)
    m = jnp.maximum(jnp.max(s, axis=1, keepdims=True), snk)
    p = jnp.exp(s - m)
    es = jnp.exp(snk - m)
    r = 1.0 / (jnp.sum(p, axis=1, keepdims=True) + es)
    return p * r, es * r


def _stack_heads(ref, lane):
    parts = []
    for pr in range(REP // 2):
        tile = ref[:, pr * LANES:(pr + 1) * LANES]
        parts += [jnp.where(lane, tile, 0.0), jnp.where(lane, 0.0, tile)]
    return jnp.concatenate(parts, axis=0)


def _unstack_heads(x, lane):
    return jnp.concatenate([jnp.where(lane, x[(2 * pr) * BLOCK:(2 * pr + 1) * BLOCK], x[(2 * pr + 1) * BLOCK:(2 * pr + 2) * BLOCK])
                            for pr in range(REP // 2)], axis=1)


def _stack_bias(b_ref, s_ref):
    bias = jnp.concatenate([b_ref[r] for r in range(REP)], axis=0)
    snk = jnp.concatenate([jnp.broadcast_to(s_ref[r][:, 0:1], (BLOCK, 1)) for r in range(REP)], axis=0)
    return bias, snk


def _attn_fwd(qkv, bias4, sink4):
    t = qkv.shape[0]
    nb = t // BLOCK
    qs, ks, vs, bs, ss = _attn_specs(nb)

    def body(q_ref, kp_ref, kc_ref, kn_ref, vp_ref, vc_ref, vn_ref, b_ref, s_ref, o_ref):
        n = pl.program_id(1)
        kcat = jnp.concatenate([kp_ref[...], kc_ref[...], kn_ref[...]], axis=0)
        vcat = jnp.concatenate([vp_ref[...], vc_ref[...], vn_ref[...]], axis=0)
        valid = _attn_valid(n, nb)
        lane = lax.broadcasted_iota(jnp.int32, (BLOCK, LANES), 1) < ATTN_DIM
        bias, snk = _stack_bias(b_ref, s_ref)
        pn, _ = _attn_probs(_stack_heads(q_ref, lane) * SCALE, kcat, bias, snk, valid)
        o_ref[...] = _unstack_heads(_dot(pn.astype(BF16), vcat), lane).astype(o_ref.dtype)

    return pl.pallas_call(
        body, name="attn_fwd", grid=(ATTN_KV, nb), in_specs=[qs] + ks + vs + [bs, ss],
        out_specs=qs, out_shape=jax.ShapeDtypeStruct((t, D_MODEL), BF16),
        compiler_params=_cparams("parallel", "arbitrary"),
    )(qkv, qkv, qkv, qkv, qkv, qkv, qkv, bias4, sink4)


def _attn_bwd(qkv, bias4, sink4, do):
    t = qkv.shape[0]
    nb = t // BLOCK
    qs, ks, vs, bs, ss = _attn_specs(nb)
    part = pl.BlockSpec((3, BLOCK, LANES), lambda g, n: (0, n, g))

    def body(q_ref, kp_ref, kc_ref, kn_ref, vp_ref, vc_ref, vn_ref, b_ref, s_ref, do_ref,
             dq_ref, dk_ref, dv_ref, db_ref, ds_ref):
        n = pl.program_id(1)

        @pl.when(n == 0)
        def _():
            db_ref[...] = jnp.zeros_like(db_ref)
            ds_ref[...] = jnp.zeros_like(ds_ref)

        kcat = jnp.concatenate([kp_ref[...], kc_ref[...], kn_ref[...]], axis=0)
        vcat = jnp.concatenate([vp_ref[...], vc_ref[...], vn_ref[...]], axis=0)
        valid = _attn_valid(n, nb)
        lane = lax.broadcasted_iota(jnp.int32, (BLOCK, LANES), 1) < ATTN_DIM
        bias, snk = _stack_bias(b_ref, s_ref)
        q = _stack_heads(q_ref, lane)
        do = _stack_heads(do_ref, lane)
        pn, psink = _attn_probs(q * SCALE, kcat, bias, snk, valid)
        dp = _dot_nt(do, vcat)
        delta = jnp.sum(pn * dp, axis=1, keepdims=True)
        dsc = pn * (dp - delta)
        dsink = psink * delta
        for r in range(REP):
            rows = slice(r * BLOCK, (r + 1) * BLOCK)
            db_ref[r] += dsc[rows]
            ds_ref[r] += jnp.broadcast_to(-jnp.sum(dsink[rows], axis=0, keepdims=True), (1, LANES))
        dsb = (dsc * SCALE).astype(BF16)
        dq_ref[...] = _unstack_heads(_dot(dsb, kcat), lane).astype(dq_ref.dtype)
        dk = _dot_tn(dsb, q)
        dv = _dot_tn(pn.astype(BF16), do)
        for j in range(3):
            dk_ref[j] = dk[j * BLOCK:(j + 1) * BLOCK]
            dv_ref[j] = dv[j * BLOCK:(j + 1) * BLOCK]

    kv_cols = ATTN_KV * LANES
    return pl.pallas_call(
        body, name="attn_bwd", grid=(ATTN_KV, nb), in_specs=[qs] + ks + vs + [bs, ss, qs],
        out_specs=[qs, part, part, bs, ss],
        out_shape=[jax.ShapeDtypeStruct((t, D_MODEL), BF16), jax.ShapeDtypeStruct((3, t, kv_cols), F32),
                   jax.ShapeDtypeStruct((3, t, kv_cols), F32), jax.ShapeDtypeStruct(bias4.shape, F32),
                   jax.ShapeDtypeStruct(sink4.shape, F32)],
        compiler_params=_cparams("parallel", "arbitrary"),
    )(qkv, qkv, qkv, qkv, qkv, qkv, qkv, bias4, sink4, do)


def _kv_combine(name, parts):
    _, t, cols = parts.shape

    def body(p_ref, o_ref):
        z = jnp.zeros((BLOCK, LANES), F32)
        from_next = jnp.concatenate([p_ref[0, BLOCK:, :], z], axis=0)
        from_prev = jnp.concatenate([z, p_ref[2, :t - BLOCK, :]], axis=0)
        o_ref[...] = (from_next + p_ref[1] + from_prev).astype(o_ref.dtype)

    return pl.pallas_call(
        body, name=name, grid=(cols // LANES,),
        in_specs=[pl.BlockSpec((3, t, LANES), lambda g: (0, 0, g))],
        out_specs=pl.BlockSpec((t, LANES), lambda g: (0, g)),
        out_shape=jax.ShapeDtypeStruct((t, cols), BF16),
        compiler_params=_cparams("parallel"),
    )(parts)


def _t5_bucket(rel):
    nb = N_BUCKETS // 2
    max_exact = nb // 2
    ret = jnp.where(rel > 0, nb, 0)
    n = jnp.abs(rel)
    nf = jnp.maximum(n, 1).astype(jnp.float32)
    large = max_exact + (jnp.log(nf / max_exact) / math.log(MAX_DISTANCE / max_exact) * (nb - max_exact)).astype(jnp.int32)
    large = jnp.minimum(large, nb - 1)
    return ret + jnp.where(n < max_exact, n, large)


def _bucket_map():
    i = jnp.arange(BLOCK)[:, None]
    j = jnp.arange(3 * BLOCK)[None, :]
    return _t5_bucket(j - BLOCK - i)


def _bias_from_table(table, onehot_t):
    def body(t_ref, o_ref, out_ref):
        out_ref[...] = _dot(t_ref[...], o_ref[...], HI)

    return pl.pallas_call(body, name="bias_from_table", out_shape=jax.ShapeDtypeStruct((ATTN_HEADS, onehot_t.shape[1]), F32),
                          compiler_params=pltpu.CompilerParams(vmem_limit_bytes=VMEM_LIMIT_BYTES))(table.T, onehot_t)


def _bias_table_grad(dbias, onehot_t):
    def body(d_ref, o_ref, out_ref):
        out_ref[...] = _dot_nt(d_ref[...], o_ref[...], HI)

    return pl.pallas_call(body, name="bias_table_grad", out_shape=jax.ShapeDtypeStruct((ATTN_HEADS, N_BUCKETS), F32),
                          compiler_params=pltpu.CompilerParams(vmem_limit_bytes=VMEM_LIMIT_BYTES))(dbias, onehot_t)


HBM_SPEC = pl.BlockSpec(memory_space=pl.ANY)


def _comm_call(name, body, xs, out_shapes, n_sems):
    n = len(xs)
    return pl.pallas_call(
        body, name=name, in_specs=[HBM_SPEC] * n, out_specs=[HBM_SPEC] * n, out_shape=out_shapes,
        scratch_shapes=[pltpu.SemaphoreType.DMA((n * n_sems,)), pltpu.SemaphoreType.DMA((n * n_sems,)),
                        pltpu.SemaphoreType.DMA((n,))],
    )(*xs)


def _allgather_chips(name, xs):
    n = len(xs)

    def body(*refs):
        x_refs, out_refs, (send_sems, recv_sems, local_sems) = refs[:n], refs[n:2 * n], refs[2 * n:]
        mx, my, mc = lax.axis_index("x"), lax.axis_index("y"), lax.axis_index("c")
        me = 2 * mx + my
        chips = [(1 - mx, my), (mx, 1 - my), (1 - mx, 1 - my)]
        sibling = (mx, my, 1 - mc)

        def part(i, slot, h):
            r2 = xs[i].shape[0] // 2
            return out_refs[i].at[slot, pl.ds(h * r2, r2)]

        def copy(i, k, src, dst, to):
            return pltpu.make_async_remote_copy(src_ref=src, dst_ref=dst, send_sem=send_sems.at[6 * i + k],
                                                recv_sem=recv_sems.at[6 * i + k], device_id=to, device_id_type=MESH)

        mine = [pltpu.make_async_copy(x_refs[i], out_refs[i].at[me], local_sems.at[i]) for i in range(n)]
        first = [copy(i, k, x_refs[i].at[pl.ds(mc * (xs[i].shape[0] // 2), xs[i].shape[0] // 2)], part(i, me, mc), (px, py, mc))
                 for i in range(n) for k, (px, py) in enumerate(chips)]
        for cp in mine + first:
            cp.start()
        passed = []
        for k, (px, py) in enumerate(chips):
            for i in range(n):
                landed = part(i, 2 * px + py, mc)
                copy(i, k, landed, landed, (px, py, mc)).wait_recv()
                passed.append(copy(i, 3 + k, landed, landed, sibling))
                passed[-1].start()
        for k, (px, py) in enumerate(chips):
            for i in range(n):
                theirs = part(i, 2 * px + py, 1 - mc)
                copy(i, 3 + k, theirs, theirs, sibling).wait_recv()
        for cp in first + passed:
            cp.wait_send()
        for cp in mine:
            cp.wait()

    return _comm_call(name, body, xs, [jax.ShapeDtypeStruct((4,) + x.shape, x.dtype) for x in xs], 6)


def _swap_cores(name, xs):
    n = len(xs)

    def body(*refs):
        x_refs, out_refs, (send_sems, recv_sems, _) = refs[:n], refs[n:2 * n], refs[2 * n:]
        mx, my, mc = lax.axis_index("x"), lax.axis_index("y"), lax.axis_index("c")
        sends = [pltpu.make_async_remote_copy(src_ref=x_refs[i].at[1 - mc], dst_ref=out_refs[i], send_sem=send_sems.at[i],
                                              recv_sem=recv_sems.at[i], device_id=(mx, my, 1 - mc), device_id_type=MESH)
                 for i in range(n)]
        for cp in sends:
            cp.start()
        for cp in sends:
            cp.wait()

    return _comm_call(name, body, xs, [jax.ShapeDtypeStruct(x.shape[1:], x.dtype) for x in xs], 1)


def _scatter_chips(name, gs):
    n = len(gs)

    def body(*refs):
        g_refs, out_refs, (send_sems, recv_sems, local_sems) = refs[:n], refs[n:2 * n], refs[2 * n:]
        mx, my, mc = lax.axis_index("x"), lax.axis_index("y"), lax.axis_index("c")
        me = 2 * mx + my
        chips = [(1 - mx, my), (mx, 1 - my), (1 - mx, 1 - my)]

        def copy(i, k, src, dst, to):
            return pltpu.make_async_remote_copy(src_ref=src, dst_ref=dst, send_sem=send_sems.at[3 * i + k],
                                                recv_sem=recv_sems.at[3 * i + k], device_id=to, device_id_type=MESH)

        mine = [pltpu.make_async_copy(g_refs[i].at[me], out_refs[i].at[me], local_sems.at[i]) for i in range(n)]
        sends = [copy(i, k, g_refs[i].at[2 * px + py], out_refs[i].at[me], (px, py, mc))
                 for i in range(n) for k, (px, py) in enumerate(chips)]
        for cp in mine + sends:
            cp.start()
        for i in range(n):
            for k, (px, py) in enumerate(chips):
                copy(i, k, g_refs[i].at[0], out_refs[i].at[2 * px + py], (px, py, mc)).wait_recv()
        for cp in sends:
            cp.wait_send()
        for cp in mine:
            cp.wait()

    return _comm_call(name, body, gs, [jax.ShapeDtypeStruct(g.shape, g.dtype) for g in gs], 3)


def _allgather_all(name, x):
    def body(x_ref, out_ref, send_sems, recv_sems, local_sem):
        mx, my, mc = lax.axis_index("x"), lax.axis_index("y"), lax.axis_index("c")
        me = 4 * mx + 2 * my + mc
        flips = [(fx, fy, fc) for fx in (0, 1) for fy in (0, 1) for fc in (0, 1)][1:]
        peers = [(mx ^ fx, my ^ fy, mc ^ fc) for fx, fy, fc in flips]
        mine = pltpu.make_async_copy(x_ref, out_ref.at[me], local_sem)
        mine.start()
        sends = [pltpu.make_async_remote_copy(src_ref=x_ref, dst_ref=out_ref.at[me], send_sem=send_sems.at[k],
                                              recv_sem=recv_sems.at[k], device_id=peer, device_id_type=MESH)
                 for k, peer in enumerate(peers)]
        for cp in sends:
            cp.start()
        for k, (px, py, pc) in enumerate(peers):
            pltpu.make_async_remote_copy(src_ref=x_ref, dst_ref=out_ref.at[4 * px + 2 * py + pc], send_sem=send_sems.at[k],
                                         recv_sem=recv_sems.at[k], device_id=(px, py, pc), device_id_type=MESH).wait_recv()
        for cp in sends:
            cp.wait_send()
        mine.wait()

    return pl.pallas_call(
        body, name=name, in_specs=[HBM_SPEC], out_specs=HBM_SPEC,
        out_shape=jax.ShapeDtypeStruct((8,) + x.shape, x.dtype),
        scratch_shapes=[pltpu.SemaphoreType.DMA((7,)), pltpu.SemaphoreType.DMA((7,)), pltpu.SemaphoreType.DMA],
    )(x)


def _allgather_cores(name, xs):
    n = len(xs)

    def body(*refs):
        x_refs, out_refs, (send_sems, recv_sems, local_sems) = refs[:n], refs[n:2 * n], refs[2 * n:]
        mx, my, mc = lax.axis_index("x"), lax.axis_index("y"), lax.axis_index("c")

        def copy(i, slot):
            return pltpu.make_async_remote_copy(src_ref=x_refs[i], dst_ref=out_refs[i].at[slot], send_sem=send_sems.at[i],
                                                recv_sem=recv_sems.at[i], device_id=(mx, my, 1 - mc), device_id_type=MESH)

        mine = [pltpu.make_async_copy(x_refs[i], out_refs[i].at[mc], local_sems.at[i]) for i in range(n)]
        sends = [copy(i, mc) for i in range(n)]
        for cp in mine + sends:
            cp.start()
        for i in range(n):
            copy(i, 1 - mc).wait_recv()
        for cp in sends:
            cp.wait_send()
        for cp in mine:
            cp.wait()

    return _comm_call(name, body, xs, [jax.ShapeDtypeStruct((2,) + x.shape, x.dtype) for x in xs], 1)


def _sum_slots(name, st, tb=256):
    n, r, c = st.shape
    tb = _pick(r, (tb, 32))

    def body(s_ref, o_ref):
        acc = s_ref[0].astype(F32)
        for k in range(1, n):
            acc = acc + s_ref[k].astype(F32)
        o_ref[...] = acc

    return pl.pallas_call(
        body, name=name, grid=(r // tb,), in_specs=[pl.BlockSpec((n, tb, c), lambda i: (0, i, 0))],
        out_specs=pl.BlockSpec((tb, c), lambda i: (i, 0)), out_shape=jax.ShapeDtypeStruct((r, c), F32),
        compiler_params=_cparams("parallel"),
    )(st)


def _adamw(name, w, g, m, v):
    def fn(w, g, m, v):
        m2 = ADAM_B1 * m + (1.0 - ADAM_B1) * g
        v2 = ADAM_B2 * v + (1.0 - ADAM_B2) * jnp.square(g)
        m_hat = m2 / (1.0 - ADAM_B1 ** ADAM_STEP)
        v_hat = v2 / (1.0 - ADAM_B2 ** ADAM_STEP)
        delta = -ADAM_LR * (m_hat / (jnp.sqrt(v_hat) + ADAM_EPS) + ADAM_WD * w)
        return [delta, m2, v2], []

    tb = _pick(w.shape[0], (256, 32))
    return _rowwise(name, fn, [_row(w), _row(g), _row(m), _row(v)], [], [(w.shape[1], F32)] * 3, tb=tb)


BIG = ("w_ssd_out", "w_attn_out", "w_o", "w_mlp_in", "w_mlp_out", "conv_w")
SMALL = ("pre_mix_norm", "b_gate", "conv_b", "dt_bias", "a_log", "d_skip", "ssd_norm", "attn_sink", "rel_bias_table",
         "post_mix_norm", "pre_mlp_norm", "post_mlp_norm")
ALL_W = ("pre_mix_norm", "w_in", "b_gate", "conv_w", "conv_b", "dt_bias", "a_log", "d_skip", "ssd_norm", "w_ssd_out",
         "attn_sink", "rel_bias_table", "w_attn_out", "w_o", "post_mix_norm", "pre_mlp_norm", "w_mlp_in", "w_mlp_out",
         "post_mlp_norm")


def _pack_rows(parts, rows, dtype):
    flat = jnp.concatenate([p.reshape(-1, D_MODEL).astype(dtype) for p in parts], axis=0)
    return jnp.pad(flat, ((0, rows - flat.shape[0]), (0, 0)))


def _pack_big(shards, dtype):
    return _pack_rows([shards[n] for n in BIG], BIG_ROWS, dtype)


def _unpack_big(flat, like):
    out, r = {}, 0
    for n in BIG:
        shp = like[n].shape
        nr = math.prod(shp) // D_MODEL
        out[n] = flat[r:r + nr].reshape(shp)
        r += nr
    return out


def _pack_small(parts, extra=None):
    flat = jnp.concatenate([parts[n].reshape(-1).astype(F32) for n in SMALL] + ([extra.reshape(-1)] if extra is not None else []))
    return jnp.pad(flat, (0, SMALL_ROWS * D_MODEL - flat.shape[0])).reshape(SMALL_ROWS, D_MODEL)


def _unpack_small(flat2, like):
    flat = flat2.reshape(-1)
    out, r = {}, 0
    for n in SMALL:
        shp = like[n].shape
        k = math.prod(shp)
        out[n] = flat[r:r + k].reshape(shp)
        r += k
    return out, flat[r]


def _shard_of_full(name, full, s):
    if name in ("w_in", "w_mlp_in"):
        w = full.shape[2] // 4
        return full[:, :, s * w:(s + 1) * w]
    if name == "conv_w":
        w = full.shape[3] // 4
        return full[:, :, :, s * w:(s + 1) * w]
    w = full.shape[1] // 4
    return full[:, s * w:(s + 1) * w, :]


def _full_of_shards(name, shards):
    axis = {"w_in": 2, "w_mlp_in": 2, "conv_w": 3}.get(name, 1)
    return jnp.concatenate(shards, axis=axis)


def _to_proj_layout(w):
    z, xbc, dt, q, k, v, gates = (w[..., 0:2048], w[..., 2048:6144], w[..., 6144:6208], w[..., 6208:7232],
                                  w[..., 7232:7488], w[..., 7488:7744], w[..., 7744:9792])
    pad = jnp.zeros(w.shape[:-1] + (N_MAIN - OFF_DT - dt.shape[-1],), w.dtype)

    def doubled(a):
        h = a.reshape(a.shape[:-1] + (ATTN_KV, 1, ATTN_DIM))
        return jnp.broadcast_to(h, a.shape[:-1] + (ATTN_KV, 2, ATTN_DIM)).reshape(a.shape[:-1] + (2 * a.shape[-1],))

    return jnp.concatenate([z, gates, xbc, dt, pad, q, doubled(k), doubled(v)], axis=-1)


def _from_proj_layout(w):
    z, gates, xbc, dt, q, k2, v2 = (w[..., 0:2048], w[..., 2048:4096], w[..., 4096:8192], w[..., 8192:8256],
                                    w[..., 8320:9344], w[..., 9344:9856], w[..., 9856:10368])

    def folded(a):
        return a.reshape(a.shape[:-1] + (ATTN_KV, 2, ATTN_DIM)).sum(axis=-2).reshape(a.shape[:-1] + (a.shape[-1] // 2,))

    return jnp.concatenate([z, xbc, dt, q, folded(k2), folded(v2), gates], axis=-1)


def _layer_fwd(h1, x, W, P, l, bias4):
    t = x.shape[0]
    S = {"x": x, "h1": h1}
    proj = _matmul("proj", h1, W["w_in_main"][l], "nn")
    qkv = _matmul("proj_qkv", h1, W["w_in_qkv"][l], "nn", out_dtype=BF16)
    S["proj"] = proj
    pre, act = _conv_fwd(proj, W["conv_w"][l], P["conv_b"][l].reshape(1, CONV_DIM))
    S["pre"], S["act"] = pre, act

    dtb = jnp.pad(P["dt_bias"][l].reshape(1, 2 * SSD_HEADS), ((0, 0), (0, LANES - 2 * SSD_HEADS)))

    def dt_fn(raw, b):
        v = raw + b
        dt = jnp.maximum(v, 0.0) + jnp.log1p(jnp.exp(-jnp.abs(v)))
        expand = (jnp.right_shift(lax.broadcasted_iota(jnp.int32, (LANES, 2 * D_INNER), 1), 6)
                  == lax.broadcasted_iota(jnp.int32, (LANES, 2 * D_INNER), 0)).astype(BF16)
        hi = dt.astype(BF16)
        rest = dt - hi.astype(F32)
        mid = rest.astype(BF16)
        lo = (rest - mid.astype(F32)).astype(BF16)
        return [dt, _dot(hi, expand) + _dot(mid, expand) + _dot(lo, expand)], []

    dt, dte = _rowwise("dt_fwd", dt_fn, [_row(proj, LANES, OFF_DT // LANES)], [dtb], [(LANES, F32), (2 * D_INNER, F32)], tb=512)
    dt2 = jnp.stack([dt[:, 0:SSD_HEADS], dt[:, SSD_HEADS:2 * SSD_HEADS]])
    dt2t = dt2.transpose(0, 2, 1)
    a = -jnp.exp(P["a_log"][l])
    a_row, a_col = a.reshape(2, 1, SSD_HEADS), a.reshape(2, SSD_HEADS, 1)
    S["dt2"], S["dt2t"], S["a_row"], S["a_col"], S["dte"] = dt2, dt2t, a_row, a_col, dte
    y2, states = _ssd_fwd(act, dt2, dt2t, a_row, a_col, dte)
    S["states"] = states

    dsk = jnp.repeat(P["d_skip"][l], SSD_HEAD_DIM).reshape(1, D_INNER)
    nw = P["ssd_norm"][l].reshape(1, D_INNER)
    S["dsk"], S["nw"] = dsk, nw

    def gn_fn(yf, yb, xs, z, dsk, nw):
        y = yf + yb + xs * dsk
        return [y, _gated_norm_fwd(y, z, nw)], []

    y, yn = _rowwise("gated_norm_fwd", gn_fn,
                     [_row(y2, lead=0), _row(y2, lead=1), _row(act, D_INNER, 0), _row(proj, D_INNER, OFF_Z // D_INNER)],
                     [dsk, nw], [(D_INNER, F32), (D_INNER, BF16)])
    S["y"], S["yn"] = y, yn
    y_ssd = _matmul("ssd_out", yn, W["w_ssd_out"][l], "nn")
    S["y_ssd"] = y_ssd

    sink4 = jnp.broadcast_to(P["attn_sink"][l].reshape(ATTN_KV, REP, 1, 1), (ATTN_KV, REP, 1, LANES))
    S["qkv"], S["sink4"] = qkv, sink4
    o = _attn_fwd(qkv, bias4, sink4)
    S["o"] = o
    y_attn = _matmul("attn_out", o, W["w_attn_out"][l], "nn")
    S["y_attn"] = y_attn

    bg = P["b_gate"][l].reshape(1, 2 * D_MODEL)
    S["bg"] = bg

    def merge_fn(gates, ys, ya, b):
        g = jax.nn.sigmoid(gates + b)
        return [g[:, :D_MODEL] * ys + g[:, D_MODEL:] * ya], []

    (mix_in,) = _rowwise("merge_fwd", merge_fn, [_row(proj, 2 * D_MODEL, OFF_G // (2 * D_MODEL)), _row(y_ssd), _row(y_attn)],
                         [bg], [(D_MODEL, BF16)])
    S["mix_in"] = mix_in
    mixed = _matmul("w_o", mix_in, W["w_o"][l], "nn")
    S["mixed"] = mixed

    g_pm = P["post_mix_norm"][l].reshape(1, D_MODEL)
    g_pl = P["pre_mlp_norm"][l].reshape(1, D_MODEL)

    def postmix_fn(x, mixed, g1, g2):
        x2 = x + _rms_fwd(mixed, g1)
        return [x2, _rms_fwd(x2, g2)], []

    x2, h2 = _rowwise("post_mix_fwd", postmix_fn, [_row(x), _row(mixed)], [g_pm, g_pl], [(D_MODEL, F32), (D_MODEL, BF16)])
    S["x2"], S["h2"] = x2, h2
    a1 = _matmul("mlp_in", h2, W["w_mlp_in"][l], "nn", out_dtype=BF16,
                 epilogue=lambda acc: jnp.square(jnp.maximum(acc, 0.0)))
    S["a1"] = a1
    f2 = _matmul("mlp_out", a1, W["w_mlp_out"][l], "nn")
    S["f2"] = f2
    return S


def _layer_bwd(S, dx3, W, P, l, bias4, onehot_t):
    t = dx3.shape[0]
    G = {}
    g_pmlp = P["post_mlp_norm"][l].reshape(1, D_MODEL)

    def b1_fn(f2, dx3, g):
        df2, dg = _rms_bwd(f2, g, dx3)
        return [df2], [dg]

    df2, G["post_mlp_norm"] = _rowwise("post_mlp_bwd", b1_fn, [_row(S["f2"]), _row(dx3)], [g_pmlp], [(D_MODEL, BF16)], [(1, D_MODEL)])
    df1 = _matmul("d_f1", df2, W["w_mlp_out"][l], "nt", out_dtype=BF16,
                  epilogue=lambda acc, a1: acc * (2.0 * jnp.sqrt(a1.astype(F32))), extras=[S["a1"]])
    G["w_mlp_out"] = _matmul("dw_mlp_out", S["a1"], df2, "tn")
    dh2 = _matmul("d_h2", df1, W["w_mlp_in"][l], "nt")
    G["w_mlp_in"] = _matmul("dw_mlp_in", S["h2"], df1, "tn")

    g_pm = P["post_mix_norm"][l].reshape(1, D_MODEL)
    g_pl = P["pre_mlp_norm"][l].reshape(1, D_MODEL)

    def b3_fn(x2, dh2, dx3, mixed, g_pl, g_pm):
        d1, dgl = _rms_bwd(x2, g_pl, dh2)
        dx2 = dx3 + d1
        dmixed, dgm = _rms_bwd(mixed, g_pm, dx2)
        return [dx2, dmixed], [dgl, dgm]

    dx2, dmixed, G["pre_mlp_norm"], G["post_mix_norm"] = _rowwise(
        "post_mix_bwd", b3_fn, [_row(S["x2"]), _row(dh2), _row(dx3), _row(S["mixed"])], [g_pl, g_pm],
        [(D_MODEL, F32), (D_MODEL, BF16)], [(1, D_MODEL), (1, D_MODEL)])
    dmix_in = _matmul("d_mix_in", dmixed, W["w_o"][l], "nt")
    G["w_o"] = _matmul("dw_o", S["mix_in"], dmixed, "tn")

    proj = S["proj"]

    def b4_fn(gates, ys, ya, dmix, b):
        g = jax.nn.sigmoid(gates + b)
        gs, ga = g[:, :D_MODEL], g[:, D_MODEL:]
        dg = jnp.concatenate([ys * dmix, ya * dmix], axis=-1) * g * (1.0 - g)
        return [gs * dmix, ga * dmix, dg], [jnp.sum(dg, axis=0, keepdims=True)]

    dy_ssd, dy_attn, dgates, G["b_gate"] = _rowwise(
        "merge_bwd", b4_fn, [_row(proj, 2 * D_MODEL, OFF_G // (2 * D_MODEL)), _row(S["y_ssd"]), _row(S["y_attn"]), _row(dmix_in)],
        [S["bg"]], [(D_MODEL, BF16), (D_MODEL, BF16), (2 * D_MODEL, BF16)], [(1, 2 * D_MODEL)])

    dyn = _matmul("d_yn", dy_ssd, W["w_ssd_out"][l], "nt")
    G["w_ssd_out"] = _matmul("dw_ssd_out", S["yn"], dy_ssd, "tn")
    do = _matmul("d_o", dy_attn, W["w_attn_out"][l], "nt", out_dtype=BF16)
    G["w_attn_out"] = _matmul("dw_attn_out", S["o"], dy_attn, "tn")

    dq, dkp, dvp, dbias4, dsink4 = _attn_bwd(S["qkv"], bias4, S["sink4"], do)
    dk = _kv_combine("dk_combine", dkp)
    dv = _kv_combine("dv_combine", dvp)
    G["attn_sink"] = dsink4[:, :, 0, 0].reshape(ATTN_HEADS)
    G["rel_bias_table"] = _bias_table_grad(dbias4.reshape(ATTN_HEADS, BLOCK * 3 * BLOCK), onehot_t).T

    act = S["act"]

    def b5_fn(y, z, xs, dyn, nw, dsk):
        dy, dz, dnw = _gated_norm_bwd(y, z, nw, dyn)
        return [dy, dz], [dnw, jnp.sum(dy * xs, axis=0, keepdims=True)]

    dy, dz, G["ssd_norm"], dskip_cols = _rowwise(
        "gated_norm_bwd", b5_fn, [_row(S["y"]), _row(proj, D_INNER, OFF_Z // D_INNER), _row(act, D_INNER, 0), _row(dyn)],
        [S["nw"], S["dsk"]], [(D_INNER, F32), (D_INNER, BF16)], [(1, D_INNER), (1, D_INNER)])
    G["d_skip"] = dskip_cols.reshape(SSD_HEADS, SSD_HEAD_DIM).sum(axis=-1)

    dxs2, dbs2, dcs2, ddt2, da2 = _ssd_bwd(act, S["dt2"], S["dt2t"], S["a_row"], S["a_col"], S["dte"], dy, S["states"])
    G["a_log"] = da2.reshape(2, SSD_HEADS) * S["a_row"].reshape(2, SSD_HEADS)

    def b6_fn(dxf, dxb, dy, dbf, dbb, dcf, dcb, pre, dsk):
        dact = jnp.concatenate([dxf + dxb + dy * dsk, dbf + dbb, dcf + dcb], axis=-1)
        return [dact * _silu_grad(pre)], []

    (dpre,) = _rowwise("silu_bwd", b6_fn,
                       [_row(dxs2, lead=0), _row(dxs2, lead=1), _row(dy), _row(dbs2, lead=0), _row(dbs2, lead=1),
                        _row(dcs2, lead=0), _row(dcs2, lead=1), _row(S["pre"])], [S["dsk"]], [(CONV_DIM, F32)], tb=128)
    du, dconv_w, dconv_b = _conv_bwd(dpre, proj, W["conv_w"][l])
    G["conv_w"] = dconv_w.reshape(SSD_CONV, 1, CONV_DIM)
    G["conv_b"] = dconv_b.reshape(CONV_DIM)

    dtb = jnp.pad(P["dt_bias"][l].reshape(1, 2 * SSD_HEADS), ((0, 0), (0, LANES - 2 * SSD_HEADS)))
    ddt = jnp.pad(jnp.concatenate([ddt2[0], ddt2[1]], axis=-1), ((0, 0), (0, LANES - 2 * SSD_HEADS)))

    def b7_fn(raw, ddt, b):
        draw = ddt * jax.nn.sigmoid(raw + b)
        return [draw], [jnp.sum(draw, axis=0, keepdims=True)]

    draw, ddtb = _rowwise("dt_bwd", b7_fn, [_row(proj, LANES, OFF_DT // LANES), _row(ddt)], [dtb], [(LANES, BF16)], [(1, LANES)], tb=512)
    G["dt_bias"] = ddtb[0, :2 * SSD_HEADS].reshape(2, SSD_HEADS)

    dproj = jnp.concatenate([dz, dgates, du, draw, dq, dk, dv], axis=-1)
    G["w_in"] = _from_proj_layout(_matmul("dw_in", S["h1"], dproj, "tn"))
    dh1 = _matmul("d_h1", dproj, W["w_in"][l], "nt")

    g_pre = P["pre_mix_norm"][l].reshape(1, D_MODEL)

    def b8_fn(x, dh1, dx2, g):
        d1, dg = _rms_bwd(x, g, dh1)
        return [dx2 + d1], [dg]

    dx, G["pre_mix_norm"] = _rowwise("pre_mix_bwd", b8_fn, [_row(S["x"]), _row(dh1), _row(dx2)], [g_pre], [(D_MODEL, F32)], [(1, D_MODEL)])
    return dx, G


def _reduce_scatter(tag, pieces):
    theirs = _swap_cores("swap_" + tag, pieces)
    chip_sums = []
    for i, p in enumerate(pieces):
        _, _, r2, c = p.shape
        mine = lax.dynamic_index_in_dim(p, lax.axis_index("c"), axis=0, keepdims=False).reshape(4 * r2, c)
        (cs,) = _rowwise(f"presum_{tag}{i}", lambda a, b: ([a.astype(F32) + b.astype(F32)], []),
                         [_row(mine), _row(theirs[i].reshape(4 * r2, c))], [], [(c, BF16)])
        chip_sums.append(cs.reshape(4, r2, c))
    staged = _scatter_chips("scatter_" + tag, chip_sums)
    halves = [_sum_slots(f"sum_{tag}{i}", st) for i, st in enumerate(staged)]
    shared = _allgather_cores("share_" + tag, halves)
    return [sh.reshape(2 * sh.shape[1], sh.shape[2]) for sh in shared]


def _step(x, target, shard_w, shard_m, shard_v):
    depth, _, win_cols = shard_w["w_in"].shape
    win_rows = depth * D_MODEL

    def win2d(a):
        return a.reshape(win_rows, win_cols)

    conv_rows = shard_w["conv_w"].reshape(-1, D_MODEL)
    win_g, gathered, conv_g = _allgather_chips("gather_weights", [
        win2d(shard_w["w_in"]).astype(BF16), _pack_big(shard_w, BF16), jnp.pad(conv_rows, ((0, 16 - conv_rows.shape[0]), (0, 0)))])
    per_chip = [_unpack_big(gathered[s], shard_w) for s in range(4)]
    W = {n: _full_of_shards(n, [per_chip[s][n] for s in range(4)]) for n in BIG if n != "conv_w"}
    W["w_in"] = _to_proj_layout(jnp.concatenate([win_g[s].reshape(depth, D_MODEL, win_cols) for s in range(4)], axis=2))
    W["conv_w"] = jnp.concatenate([conv_g[s][:conv_rows.shape[0]].reshape(shard_w["conv_w"].shape) for s in range(4)],
                                  axis=3).reshape(depth, SSD_CONV, CONV_DIM)
    P = {n: shard_w[n] for n in SMALL}
    loss_part, grad_x, full = _local_step(x, target, W, P)

    assert depth == 2
    win_pieces = jnp.stack([jnp.stack([full["w_in"][h][:, s * win_cols:(s + 1) * win_cols] for s in range(4)])
                            for h in range(2)]).astype(BF16)
    packs = [_pack_big({n: _shard_of_full(n, full[n], s) for n in BIG}, BF16) for s in range(4)]
    r2 = BIG_ROWS // 2
    pieces = jnp.stack([jnp.stack([p[h * r2:(h + 1) * r2] for p in packs]) for h in range(2)])
    g_win, g_big = _reduce_scatter("grads", [win_pieces, pieces])
    d_win, m_win, v_win = _adamw("adamw_w_in", win2d(shard_w["w_in"]), g_win, win2d(shard_m["w_in"]), win2d(shard_v["w_in"]))
    d_big, m_big, v_big = _adamw("adamw_big", _pack_big(shard_w, F32), g_big, _pack_big(shard_m, F32), _pack_big(shard_v, F32))

    small = _allgather_all("gather_small", _pack_small(full, loss_part))
    g_small = _sum_slots("sum_small", small, tb=SMALL_ROWS)
    d_small, m_small, v_small = _adamw("adamw_small", _pack_small(shard_w, jnp.zeros((), F32)), g_small,
                                       _pack_small(shard_m, jnp.zeros((), F32)), _pack_small(shard_v, jnp.zeros((), F32)))

    outs = {}
    for tag, win, big, sm in (("grad", g_win, g_big, g_small), ("delta", d_win, d_big, d_small),
                              ("new_m", m_win, m_big, m_small), ("new_v", v_win, v_big, v_small)):
        ub = _unpack_big(big, shard_w)
        us, extra = _unpack_small(sm, shard_w)
        outs[tag] = {"w_in": win.reshape(shard_w["w_in"].shape), **ub, **us}
        if tag == "grad":
            loss = extra
    return loss, grad_x, outs


def _local_step(x, target, W, P):
    depth = W["w_in"].shape[0]
    W = dict(W, w_in_main=W["w_in"][:, :, :N_MAIN], w_in_qkv=W["w_in"][:, :, N_MAIN:])
    onehot_t = (_bucket_map().reshape(1, -1) == jnp.arange(N_BUCKETS)[:, None]).astype(F32)
    bias = _bias_from_table(P["rel_bias_table"], onehot_t).reshape(ATTN_HEADS, BLOCK, 3 * BLOCK)
    bias4 = jnp.where(_band_mask(), bias, NEG).reshape(ATTN_KV, REP, BLOCK, 3 * BLOCK)

    def pre_fn(x, g):
        return [_rms_fwd(x, g)], []

    (h1,) = _rowwise("pre_mix_fwd", pre_fn, [_row(x)], [P["pre_mix_norm"][0].reshape(1, D_MODEL)], [(D_MODEL, BF16)])
    saved = []
    loss_cols = dxl = None
    for l in range(depth):
        S = _layer_fwd(h1, x, W, P, l, bias4)
        saved.append(S)
        g_pmlp = P["post_mlp_norm"][l].reshape(1, D_MODEL)
        if l + 1 < depth:
            def post_fn(x2, f2, g1, g2):
                x3 = x2 + _rms_fwd(f2, g1)
                return [x3, _rms_fwd(x3, g2)], []

            x, h1 = _rowwise("post_mlp_fwd", post_fn, [_row(S["x2"]), _row(S["f2"])],
                             [g_pmlp, P["pre_mix_norm"][l + 1].reshape(1, D_MODEL)], [(D_MODEL, F32), (D_MODEL, BF16)])
        else:
            def loss_fn(x2, f2, tgt, g1):
                diff = x2 + _rms_fwd(f2, g1) - tgt
                return [diff * (1.0 / D_MODEL)], [jnp.sum(diff * diff, axis=0, keepdims=True)]

            dxl, loss_cols = _rowwise("loss", loss_fn, [_row(S["x2"]), _row(S["f2"]), _row(target)], [g_pmlp],
                                      [(D_MODEL, F32)], [(1, D_MODEL)])
    loss_part = 0.5 * jnp.sum(loss_cols) / D_MODEL

    grads = [None] * depth
    dx = dxl
    for l in reversed(range(depth)):
        dx, grads[l] = _layer_bwd(saved[l], dx, W, P, l, bias4, onehot_t)
    grad_x = dx

    full = {n: jnp.stack([grads[l][n] for l in range(depth)]) for n in ALL_W if n != "rel_bias_table"}
    full["rel_bias_table"] = sum(grads[l]["rel_bias_table"] for l in range(depth))
    return loss_part, grad_x, full


def kernel(x, pre_mix_norm, w_in, b_gate, conv_w, conv_b, dt_bias, a_log, d_skip, ssd_norm, w_ssd_out, attn_sink, rel_bias_table, w_attn_out, w_o, post_mix_norm, pre_mlp_norm, w_mlp_in, w_mlp_out, post_mlp_norm, loss_target, m_pre_mix_norm, m_w_in, m_b_gate, m_conv_w, m_conv_b, m_dt_bias, m_a_log, m_d_skip, m_ssd_norm, m_w_ssd_out, m_attn_sink, m_rel_bias_table, m_w_attn_out, m_w_o, m_post_mix_norm, m_pre_mlp_norm, m_w_mlp_in, m_w_mlp_out, m_post_mlp_norm, v_pre_mix_norm, v_w_in, v_b_gate, v_conv_w, v_conv_b, v_dt_bias, v_a_log, v_d_skip, v_ssd_norm, v_w_ssd_out, v_attn_sink, v_rel_bias_table, v_w_attn_out, v_w_o, v_post_mix_norm, v_pre_mlp_norm, v_w_mlp_in, v_w_mlp_out, v_post_mlp_norm):
    a = locals()
    shard_w = {n: a[n] for n in ALL_W}
    shard_m = {n: a["m_" + n] for n in ALL_W}
    shard_v = {n: a["v_" + n] for n in ALL_W}
    loss, grad_x, outs = _step(x[0], loss_target[0], shard_w, shard_m, shard_v)
    return (loss, grad_x[None], *[outs["grad"][n] for n in ALL_W], *[outs["delta"][n] for n in ALL_W],
            *[outs["new_m"][n] for n in ALL_W], *[outs["new_v"][n] for n in ALL_W])
```

```python
import math

import jax
import jax.numpy as jnp
from jax import lax
from jax.experimental import pallas as pl
from jax.experimental.pallas import tpu as pltpu

F32 = jnp.float32
BF16 = jnp.bfloat16
MESH = pl.DeviceIdType.MESH

VMEM_LIMIT_BYTES = 52 * 1024 * 1024
LANES = 128
SUBLANES = 8

EPS = 1e-6
D_MODEL = 1024
D_INNER = 2048
SSD_HEADS = 32
SSD_HEAD_DIM = 64
SSD_GROUPS = 8
SSD_STATE = 128
SSD_CONV = 5
CHUNK = 128
CONV_DIM = 4096
ATTN_HEADS = 16
ATTN_KV = 4
ATTN_DIM = 64
BLOCK = 128
N_BUCKETS = 32
MAX_DISTANCE = 128
D_FF = 4096
N_IN = 9792
NEG = -1e30

OFF_Z, OFF_G, OFF_XBC, OFF_DT, N_MAIN, N_PROJ = 0, 2048, 4096, 8192, 8320, 10368
N_QKV = N_PROJ - N_MAIN

ADAM_LR, ADAM_B1, ADAM_B2, ADAM_EPS, ADAM_WD, ADAM_STEP = 0.001, 0.9, 0.999, 1e-08, 0.01, 10

BIG_ROWS = 6656
SMALL_ROWS = 32


def _cparams(*sem):
    return pltpu.CompilerParams(dimension_semantics=sem, vmem_limit_bytes=VMEM_LIMIT_BYTES)


def _dot(a, b, precision=None):
    return lax.dot_general(a, b, (((1,), (0,)), ((), ())), preferred_element_type=F32, precision=precision)


def _dot_nt(a, b, precision=None):
    return lax.dot_general(a, b, (((1,), (1,)), ((), ())), preferred_element_type=F32, precision=precision)


def _dot_tn(a, b):
    return lax.dot_general(a, b, (((0,), (0,)), ((), ())), preferred_element_type=F32)


def _pick(n, prefs):
    for p in prefs:
        if n % p == 0:
            return p
    return n


def _matmul(name, a, b, mode, out_dtype=F32, epilogue=None, extras=()):
    if mode == "nn":
        (m, k), (_, n) = a.shape, b.shape
    elif mode == "nt":
        (m, k), (n, _) = a.shape, b.shape
    else:
        (k, m), (_, n) = a.shape, b.shape
    tm = _pick(m, (512, 256, 128)) if mode == "tn" else _pick(m, (1024, 512, 256, 128))
    tn = _pick(n, (1024, 1152, 640, 512, 256, 128))
    tk = _pick(k, (1024, 3456, 512, 256, 128)) if mode != "tn" else _pick(k, (2048, 512, 256, 128))
    if tk > 2048:
        tm = _pick(m, (512, 256, 128))
    nk = k // tk
    if mode == "nn":
        a_spec = pl.BlockSpec((tm, tk), lambda i, j, q: (i, q))
        b_spec = pl.BlockSpec((tk, tn), lambda i, j, q: (q, j))
        fn = _dot
    elif mode == "nt":
        a_spec = pl.BlockSpec((tm, tk), lambda i, j, q: (i, q))
        b_spec = pl.BlockSpec((tn, tk), lambda i, j, q: (j, q))
        fn = _dot_nt
    else:
        a_spec = pl.BlockSpec((tk, tm), lambda i, j, q: (q, i))
        b_spec = pl.BlockSpec((tk, tn), lambda i, j, q: (q, j))
        fn = _dot_tn

    tile = pl.BlockSpec((tm, tn), lambda i, j, q: (i, j))
    n_ex = len(extras)

    def body(a_ref, b_ref, *rest):
        ex_refs, o_ref = rest[:n_ex], rest[n_ex]

        def store(acc):
            v = acc if epilogue is None else epilogue(acc, *[r[...] for r in ex_refs])
            o_ref[...] = v.astype(o_ref.dtype)

        p = fn(a_ref[...].astype(BF16), b_ref[...].astype(BF16))
        if nk == 1:
            store(p)
        else:
            acc_ref = rest[n_ex + 1]
            q = pl.program_id(2)

            @pl.when(q == 0)
            def _():
                acc_ref[...] = p

            @pl.when((q > 0) & (q < nk - 1))
            def _():
                acc_ref[...] += p

            @pl.when(q == nk - 1)
            def _():
                store(acc_ref[...] + p)

    return pl.pallas_call(
        body, name=name, grid=(m // tm, n // tn, nk),
        in_specs=[a_spec, b_spec] + [tile] * n_ex, out_specs=tile,
        out_shape=jax.ShapeDtypeStruct((m, n), out_dtype),
        scratch_shapes=[pltpu.VMEM((tm, tn), F32)] if nk > 1 else [],
        compiler_params=_cparams("parallel", "parallel", "arbitrary"),
    )(a, b, *extras)


def _row(arr, width=None, cb=0, lead=None):
    return (arr, width, cb, lead)


def _rowwise(name, fn, rows, vecs, outs, accs=(), tb=256):
    t = rows[0][0].shape[-2]
    tb = min(tb, t)
    in_specs, args = [], []
    for arr, width, cb, lead in rows:
        w = arr.shape[-1] if width is None else width
        if lead is None:
            in_specs.append(pl.BlockSpec((tb, w), lambda i, cb=cb: (i, cb)))
        else:
            in_specs.append(pl.BlockSpec((None, tb, w), lambda i, cb=cb, lead=lead: (lead, i, cb)))
        args.append(arr)
    for v in vecs:
        in_specs.append(pl.BlockSpec(v.shape, lambda i, nd=v.ndim: (0,) * nd))
        args.append(v)
    out_shape = [jax.ShapeDtypeStruct((t, c), dt) for c, dt in outs] + [jax.ShapeDtypeStruct(s, F32) for s in accs]
    out_specs = [pl.BlockSpec((tb, c), lambda i: (i, 0)) for c, _ in outs] + [pl.BlockSpec(s, lambda i: (0, 0)) for s in accs]
    n_in, n_out = len(args), len(outs)

    def body(*refs):
        vals = [r[...] for r in refs[:n_in]]
        o_vals, a_vals = fn(*vals)
        for r, v in zip(refs[n_in:n_in + n_out], o_vals):
            r[...] = v.astype(r.dtype)
        first = pl.program_id(0) == 0
        for r, v in zip(refs[n_in + n_out:], a_vals):
            @pl.when(first)
            def _(r=r, v=v):
                r[...] = v

            @pl.when(jnp.logical_not(first))
            def _(r=r, v=v):
                r[...] += v

    res = pl.pallas_call(
        body, name=name, grid=(t // tb,), in_specs=in_specs, out_specs=out_specs, out_shape=out_shape,
        compiler_params=_cparams("arbitrary"),
    )(*args)
    return res


def _rms_fwd(x, g):
    r = lax.rsqrt(jnp.mean(x * x, axis=-1, keepdims=True) + EPS)
    return x * r * g


def _rms_bwd(x, g, dy):
    r = lax.rsqrt(jnp.mean(x * x, axis=-1, keepdims=True) + EPS)
    xh = x * r
    dxh = dy * g
    dx = r * (dxh - xh * jnp.mean(dxh * xh, axis=-1, keepdims=True))
    return dx, jnp.sum(dy * xh, axis=0, keepdims=True)


def _silu(x):
    return x * jax.nn.sigmoid(x)


def _silu_grad(x):
    s = jax.nn.sigmoid(x)
    return s * (1.0 + x * (1.0 - s))


GROUP_W = D_INNER // SSD_GROUPS


def _gated_norm_fwd(y, z, w):
    u = y * _silu(z)
    parts = []
    for j in range(SSD_GROUPS):
        ug = u[:, j * GROUP_W:(j + 1) * GROUP_W]
        parts.append(ug * lax.rsqrt(jnp.mean(ug * ug, axis=-1, keepdims=True) + EPS))
    return jnp.concatenate(parts, axis=-1) * w


def _gated_norm_bwd(y, z, w, dyn):
    sz = _silu(z)
    u = y * sz
    duh = dyn * w
    du_parts, uh_parts = [], []
    for j in range(SSD_GROUPS):
        sl = slice(j * GROUP_W, (j + 1) * GROUP_W)
        ug = u[:, sl]
        r = lax.rsqrt(jnp.mean(ug * ug, axis=-1, keepdims=True) + EPS)
        uh = ug * r
        dg = duh[:, sl]
        du_parts.append(r * (dg - uh * jnp.mean(dg * uh, axis=-1, keepdims=True)))
        uh_parts.append(uh)
    du = jnp.concatenate(du_parts, axis=-1)
    uh = jnp.concatenate(uh_parts, axis=-1)
    dw = jnp.sum(dyn * uh, axis=0, keepdims=True)
    return du * sz, du * y * _silu_grad(z), dw


HALO = SUBLANES


def _halo_specs(tb, cb, col0, t):
    nblk8 = t // HALO
    per = tb // HALO
    main = pl.BlockSpec((tb, cb), lambda j, i: (i, col0 + j))
    prev = pl.BlockSpec((HALO, cb), lambda j, i: (jnp.maximum(i * per - 1, 0), col0 + j))
    nxt = pl.BlockSpec((HALO, cb), lambda j, i: (jnp.minimum((i + 1) * per, nblk8 - 1), col0 + j))
    return main, prev, nxt


def _fill_ext(ext_ref, cur_ref, prev_ref, next_ref, tb, ni):
    i = pl.program_id(1)
    ext_ref[0:HALO, :] = jnp.where(i > 0, prev_ref[...], 0.0)
    ext_ref[HALO:HALO + tb, :] = cur_ref[...]
    ext_ref[HALO + tb:HALO + tb + HALO, :] = jnp.where(i < ni - 1, next_ref[...], 0.0)


def _conv_fwd(proj, w, b):
    t = proj.shape[0]
    tb, cb = min(512, t), 512
    ni, nj = t // tb, CONV_DIM // cb
    main, prev, nxt = _halo_specs(tb, cb, OFF_XBC // cb, t)
    pad = (SSD_CONV - 1) // 2

    def body(u_ref, up_ref, un_ref, w_ref, b_ref, pre_ref, act_ref, ext_ref):
        _fill_ext(ext_ref, u_ref, up_ref, un_ref, tb, ni)
        acc = jnp.broadcast_to(b_ref[...], (tb, cb))
        for k in range(SSD_CONV):
            acc = acc + w_ref[k:k + 1, :] * ext_ref[pl.ds(HALO + k - pad, tb), :]
        pre_ref[...] = acc
        act_ref[...] = _silu(acc)

    out = pl.BlockSpec((tb, cb), lambda j, i: (i, j))
    return pl.pallas_call(
        body, name="conv_fwd", grid=(nj, ni),
        in_specs=[main, prev, nxt, pl.BlockSpec((SSD_CONV, cb), lambda j, i: (0, j)), pl.BlockSpec((1, cb), lambda j, i: (0, j))],
        out_specs=[out, out],
        out_shape=[jax.ShapeDtypeStruct((t, CONV_DIM), F32)] * 2,
        scratch_shapes=[pltpu.VMEM((tb + 2 * HALO, cb), F32)],
        compiler_params=_cparams("parallel", "arbitrary"),
    )(proj, proj, proj, w, b)


def _conv_bwd(dpre, proj, w):
    t = proj.shape[0]
    tb, cb = min(512, t), 512
    ni, nj = t // tb, CONV_DIM // cb
    umain, uprev, unext = _halo_specs(tb, cb, OFF_XBC // cb, t)
    dmain, dprev, dnext = _halo_specs(tb, cb, 0, t)
    pad = (SSD_CONV - 1) // 2

    def body(d_ref, dp_ref, dn_ref, u_ref, up_ref, un_ref, w_ref, du_ref, dw_ref, db_ref, extd_ref, extu_ref):
        _fill_ext(extd_ref, d_ref, dp_ref, dn_ref, tb, ni)
        _fill_ext(extu_ref, u_ref, up_ref, un_ref, tb, ni)
        d = d_ref[...]
        du = jnp.zeros((tb, cb), F32)
        @pl.when(pl.program_id(1) == 0)
        def _():
            dw_ref[...] = jnp.zeros_like(dw_ref)
            db_ref[...] = jnp.zeros_like(db_ref)

        for k in range(SSD_CONV):
            du = du + w_ref[k:k + 1, :] * extd_ref[pl.ds(HALO - k + pad, tb), :]
            dw_ref[k:k + 1, :] += jnp.sum(d * extu_ref[pl.ds(HALO + k - pad, tb), :], axis=0, keepdims=True)
        du_ref[...] = du.astype(du_ref.dtype)
        db_ref[...] += jnp.sum(d, axis=0, keepdims=True)

    return pl.pallas_call(
        body, name="conv_bwd", grid=(nj, ni),
        in_specs=[dmain, dprev, dnext, umain, uprev, unext, pl.BlockSpec((SSD_CONV, cb), lambda j, i: (0, j))],
        out_specs=[pl.BlockSpec((tb, cb), lambda j, i: (i, j)), pl.BlockSpec((SSD_CONV, cb), lambda j, i: (0, j)),
                   pl.BlockSpec((1, cb), lambda j, i: (0, j))],
        out_shape=[jax.ShapeDtypeStruct((t, CONV_DIM), BF16), jax.ShapeDtypeStruct((SSD_CONV, CONV_DIM), F32),
                   jax.ShapeDtypeStruct((1, CONV_DIM), F32)],
        scratch_shapes=[pltpu.VMEM((tb + 2 * HALO, cb), F32)] * 2,
        compiler_params=_cparams("parallel", "arbitrary"),
    )(dpre, dpre, dpre, proj, proj, proj, w)


PAIR = 2 * SSD_HEAD_DIM
HI = lax.Precision.HIGHEST


def _ssd_prelude(d, dt_ref, dtt_ref, ar_ref, ac_ref):
    li = lax.broadcasted_iota(jnp.int32, (CHUNK, CHUNK), 0)
    si = lax.broadcasted_iota(jnp.int32, (CHUNK, CHUNK), 1)
    fwd = d == 0
    hi, lo = jnp.where(fwd, li, si), jnp.where(fwd, si, li)
    tri = hi >= lo
    trif = tri.astype(F32)
    trit = (hi <= lo).astype(F32)
    dt = dt_ref[...]
    adt = dt * ar_ref[...]
    adtt = dtt_ref[...] * ac_ref[...]
    p = _dot(trif, adt, HI)
    pt = _dot_nt(adtt, trif, HI)
    tot = jnp.sum(adt, axis=0, keepdims=True)
    return tri, trit, dt, p, pt, tot


def _ssd_specs(nc, rev):
    def cidx(d, c):
        up = (d == 1) if rev else (d == 0)
        return jnp.where(up, c, nc - 1 - c)

    specs = [
        pl.BlockSpec((CHUNK, D_INNER), lambda d, c: (cidx(d, c), 0)),
        pl.BlockSpec((CHUNK, 1024), lambda d, c: (cidx(d, c), 2)),
        pl.BlockSpec((CHUNK, 1024), lambda d, c: (cidx(d, c), 3)),
        pl.BlockSpec((None, CHUNK, SSD_HEADS), lambda d, c: (d, cidx(d, c), 0)),
        pl.BlockSpec((None, SSD_HEADS, CHUNK), lambda d, c: (d, 0, cidx(d, c))),
        pl.BlockSpec((None, 1, SSD_HEADS), lambda d, c: (d, 0, 0)),
        pl.BlockSpec((None, SSD_HEADS, 1), lambda d, c: (d, 0, 0)),
        pl.BlockSpec((CHUNK, D_INNER), lambda d, c: (cidx(d, c), d)),
    ]
    return cidx, specs


def _head_decay(tri, p, pt, tot, h):
    pb = jnp.broadcast_to(p[:, h:h + 1], (CHUNK, CHUNK))
    dec = jnp.exp(jnp.where(tri, pb - pt[h:h + 1, :], NEG))
    return dec, jnp.exp(tot[:, h:h + 1] - pb), jnp.exp(pb)


def _ssd_fwd(act, dt2, dt2t, a_row, a_col, dte):
    t = act.shape[0]
    nc = t // CHUNK
    cidx, specs = _ssd_specs(nc, rev=False)

    def body(xs_ref, bs_ref, cs_ref, dt_ref, dtt_ref, ar_ref, ac_ref, dte_ref, y_ref, st_ref, h_ref):
        d, c = pl.program_id(0), pl.program_id(1)

        @pl.when(c == 0)
        def _():
            h_ref[...] = jnp.zeros_like(h_ref)

        st_ref[...] = h_ref[...]
        tri, _, _, p, pt, tot = _ssd_prelude(d, dt_ref, dtt_ref, ar_ref, ac_ref)
        etot = jnp.exp(tot)
        lane = lax.broadcasted_iota(jnp.int32, (CHUNK, PAIR), 1) < SSD_HEAD_DIM
        rowh = lax.broadcasted_iota(jnp.int32, (PAIR, SSD_STATE), 0) < SSD_HEAD_DIM
        for g in range(SSD_GROUPS):
            gs = slice(g * SSD_STATE, (g + 1) * SSD_STATE)
            bg = bs_ref[:, gs]
            cb = cs_ref[:, gs].astype(BF16)
            cbm = _dot_nt(cb, bg.astype(BF16))
            for pr in range(2):
                h0 = g * 4 + pr * 2
                h1 = h0 + 1
                sl = slice(h0 * SSD_HEAD_DIM, h0 * SSD_HEAD_DIM + PAIR)
                xdt = (xs_ref[:, sl] * dte_ref[:, sl]).astype(BF16)
                yd, st, epb = [], [], []
                for h in (h0, h1):
                    dec, wb, eb = _head_decay(tri, p, pt, tot, h)
                    yd.append(_dot((cbm * dec).astype(BF16), xdt))
                    st.append(_dot_tn(xdt, (bg * wb).astype(BF16)))
                    epb.append(eb)
                hin = h_ref[sl, :]
                yo = _dot_nt(cb, hin.astype(BF16)) * jnp.where(lane, epb[0], epb[1])
                y_ref[:, sl] = jnp.where(lane, yd[0], yd[1]) + yo
                et = jnp.where(rowh, etot[:, h0:h0 + 1], etot[:, h1:h1 + 1])
                h_ref[sl, :] = hin * et + jnp.where(rowh, st[0], st[1])

    return pl.pallas_call(
        body, name="ssd_fwd", grid=(2, nc), in_specs=specs,
        out_specs=[pl.BlockSpec((None, CHUNK, D_INNER), lambda d, c: (d, cidx(d, c), 0)),
                   pl.BlockSpec((None, None, D_INNER, SSD_STATE), lambda d, c: (d, cidx(d, c), 0, 0))],
        out_shape=[jax.ShapeDtypeStruct((2, t, D_INNER), F32), jax.ShapeDtypeStruct((2, nc, D_INNER, SSD_STATE), F32)],
        scratch_shapes=[pltpu.VMEM((D_INNER, SSD_STATE), F32)],
        compiler_params=_cparams("arbitrary", "arbitrary"),
    )(act, act, act, dt2, dt2t, a_row, a_col, dte)


def _put_col(acc, col, h):
    lane = lax.broadcasted_iota(jnp.int32, acc.shape, 1)
    return jnp.where(lane == h, col, acc)


def _put_row(acc, row, h):
    sub = lax.broadcasted_iota(jnp.int32, acc.shape, 0)
    return jnp.where(sub == h, row, acc)


def _sum_all(x):
    return jnp.sum(jnp.sum(x, axis=0, keepdims=True), axis=1, keepdims=True)


def _ssd_bwd(act, dt2, dt2t, a_row, a_col, dte, dy, states):
    t = act.shape[0]
    nc = t // CHUNK
    cidx, specs = _ssd_specs(nc, rev=True)
    specs = specs + [
        pl.BlockSpec((CHUNK, D_INNER), lambda d, c: (cidx(d, c), 0)),
        pl.BlockSpec((None, None, D_INNER, SSD_STATE), lambda d, c: (d, cidx(d, c), 0, 0)),
    ]

    def body(xs_ref, bs_ref, cs_ref, dt_ref, dtt_ref, ar_ref, ac_ref, dte_ref, dy_ref, st_ref,
             dxs_ref, dbs_ref, dcs_ref, ddt_ref, da_ref, dh_ref):
        d, c = pl.program_id(0), pl.program_id(1)

        @pl.when(c == 0)
        def _():
            dh_ref[...] = jnp.zeros_like(dh_ref)
            da_ref[...] = jnp.zeros_like(da_ref)

        tri, trit, dt, p, pt, tot = _ssd_prelude(d, dt_ref, dtt_ref, ar_ref, ac_ref)
        etot = jnp.exp(tot)
        lane = lax.broadcasted_iota(jnp.int32, (CHUNK, PAIR), 1) < SSD_HEAD_DIM
        rowh = lax.broadcasted_iota(jnp.int32, (PAIR, SSD_STATE), 0) < SSD_HEAD_DIM
        first_head = lax.broadcasted_iota(jnp.int32, (PAIR, LANES), 0) < SSD_HEAD_DIM
        out_lane = lax.broadcasted_iota(jnp.int32, (PAIR, LANES), 1)
        ddtx = jnp.zeros((CHUNK, LANES), F32)
        lane32 = lax.broadcasted_iota(jnp.int32, (CHUNK, SSD_HEADS), 1)
        dp_col = jnp.zeros((CHUNK, SSD_HEADS), F32)
        dp_row = jnp.zeros((SSD_HEADS, CHUNK), F32)
        dtot = jnp.zeros((1, SSD_HEADS), F32)
        for g in range(SSD_GROUPS):
            gs = slice(g * SSD_STATE, (g + 1) * SSD_STATE)
            bg = bs_ref[:, gs]
            bb = bg.astype(BF16)
            cb = cs_ref[:, gs].astype(BF16)
            cbm = _dot_nt(cb, bb)
            dcb = jnp.zeros((CHUNK, CHUNK), F32)
            dc_acc = jnp.zeros((CHUNK, SSD_STATE), F32)
            db_acc = jnp.zeros((CHUNK, SSD_STATE), F32)
            for pr in range(2):
                h0 = g * 4 + pr * 2
                h1 = h0 + 1
                sl = slice(h0 * SSD_HEAD_DIM, h0 * SSD_HEAD_DIM + PAIR)
                xp = xs_ref[:, sl]
                dtp = dte_ref[:, sl]
                xdt_f = xp * dtp
                xdt = xdt_f.astype(BF16)
                dyp = dy_ref[:, sl]
                dyb = dyp.astype(BF16)
                hin = st_ref[sl, :]
                dh = dh_ref[sl, :]
                hb = hin.astype(BF16)
                dhb = dh.astype(BF16)
                heads = [_head_decay(tri, p, pt, tot, h) for h in (h0, h1)]
                dye = dyp * jnp.where(lane, heads[0][2], heads[1][2])
                dyeb = dye.astype(BF16)
                gy = _dot_nt(cb, hb) * dye
                dc_acc = dc_acc + _dot(dyeb, hb)
                dhin = _dot_tn(dyeb, cb)
                hh = dh * hin
                dxdt = jnp.zeros((CHUNK, PAIR), F32)
                for idx, h in enumerate((h0, h1)):
                    hm = lane if idx == 0 else jnp.logical_not(lane)
                    rm = rowh if idx == 0 else jnp.logical_not(rowh)
                    dec, wb, _ = heads[idx]
                    mf = cbm * dec
                    t1 = _dot_tn(mf.astype(BF16), dyb)
                    t2 = _dot_nt((bg * wb).astype(BF16), dhb)
                    dxdt = jnp.where(hm, t1 + t2, dxdt)
                    dm = _dot_nt(jnp.where(hm, dyp, 0.0).astype(BF16), xdt)
                    dcb = dcb + dm * dec
                    e = dm * mf
                    qw = _dot(jnp.where(hm, xdt_f, 0.0).astype(BF16), dhb) * wb
                    db_acc = db_acc + qw
                    qwb = qw * bg
                    col = jnp.sum(e + jnp.where(hm, gy, 0.0) - qwb, axis=1, keepdims=True)
                    dp_col = jnp.where(lane32 == h, col, dp_col)
                    dp_row = _put_row(dp_row, -jnp.sum(e, axis=0, keepdims=True), h)
                    dtot_h = _sum_all(qwb) + etot[:, h:h + 1] * _sum_all(jnp.where(rm, hh, 0.0))
                    dtot = _put_col(dtot, dtot_h, h)
                dxs_ref[:, sl] = dxdt * dtp
                ddx = dxdt * xp
                ddx_hi = ddx.astype(BF16)
                ddx_lo = (ddx - ddx_hi.astype(F32)).astype(BF16)
                route = (out_lane == jnp.where(first_head, h0, h1)).astype(BF16)
                ddtx = ddtx + _dot(ddx_hi, route) + _dot(ddx_lo, route)
                et = jnp.where(rowh, etot[:, h0:h0 + 1], etot[:, h1:h1 + 1])
                dh_ref[sl, :] = dh * et + dhin
            dcbb = dcb.astype(BF16)
            dcs_ref[:, gs] = _dot(dcbb, bb) + dc_acc
            dbs_ref[:, gs] = _dot_tn(dcbb, cb) + db_acc
        d_adt = _dot(trit, dp_col, HI) + _dot_nt(trit, dp_row, HI) + dtot
        ddt_ref[...] = ddtx[:, :SSD_HEADS] + ar_ref[...] * d_adt
        da_ref[...] += jnp.sum(dt * d_adt, axis=0, keepdims=True)

    return pl.pallas_call(
        body, name="ssd_bwd", grid=(2, nc), in_specs=specs,
        out_specs=[pl.BlockSpec((None, CHUNK, D_INNER), lambda d, c: (d, cidx(d, c), 0)),
                   pl.BlockSpec((None, CHUNK, 1024), lambda d, c: (d, cidx(d, c), 0)),
                   pl.BlockSpec((None, CHUNK, 1024), lambda d, c: (d, cidx(d, c), 0)),
                   pl.BlockSpec((None, CHUNK, SSD_HEADS), lambda d, c: (d, cidx(d, c), 0)),
                   pl.BlockSpec((None, 1, SSD_HEADS), lambda d, c: (d, 0, 0))],
        out_shape=[jax.ShapeDtypeStruct((2, t, D_INNER), F32), jax.ShapeDtypeStruct((2, t, 1024), F32),
                   jax.ShapeDtypeStruct((2, t, 1024), F32), jax.ShapeDtypeStruct((2, t, SSD_HEADS), F32),
                   jax.ShapeDtypeStruct((2, 1, SSD_HEADS), F32)],
        scratch_shapes=[pltpu.VMEM((D_INNER, SSD_STATE), F32)],
        compiler_params=_cparams("arbitrary", "arbitrary"),
    )(act, act, act, dt2, dt2t, a_row, a_col, dte, dy, states)


REP = ATTN_HEADS // ATTN_KV
SCALE = ATTN_DIM ** -0.5
GROUP_Q = REP * ATTN_DIM
K_BLK0 = D_MODEL // LANES
V_BLK0 = K_BLK0 + ATTN_KV


def _attn_specs(nb):
    q = pl.BlockSpec((BLOCK, GROUP_Q), lambda g, n: (n, g))

    def kv(blk0):
        return [pl.BlockSpec((BLOCK, LANES), lambda g, n: (jnp.maximum(n - 1, 0), blk0 + g)),
                pl.BlockSpec((BLOCK, LANES), lambda g, n: (n, blk0 + g)),
                pl.BlockSpec((BLOCK, LANES), lambda g, n: (jnp.minimum(n + 1, nb - 1), blk0 + g))]

    bias = pl.BlockSpec((None, REP, BLOCK, 3 * BLOCK), lambda g, n: (g, 0, 0, 0))
    sink = pl.BlockSpec((None, REP, 1, LANES), lambda g, n: (g, 0, 0, 0))
    return q, kv(K_BLK0), kv(V_BLK0), bias, sink


def _band_mask():
    ii = lax.broadcasted_iota(jnp.int32, (BLOCK, 3 * BLOCK), 0)
    jj = lax.broadcasted_iota(jnp.int32, (BLOCK, 3 * BLOCK), 1)
    return (jj >= ii) & (jj - 2 * BLOCK <= ii)


def _attn_valid(n, nb):
    jj = lax.broadcasted_iota(jnp.int32, (1, 3 * BLOCK), 1)
    return ((jj >= BLOCK) | (n > 0)) & ((jj < 2 * BLOCK) | (n < nb - 1))


def _attn_probs(q, kcat, bias, snk, valid):
    s = jnp.where(valid, _dot_nt(q, kcat) + bias, NEG)
    m = jnp.maximum(jnp.max(s, axis=1, keepdims=True), snk)
    p = jnp.exp(s - m)
    es = jnp.exp(snk - m)
    r = 1.0 / (jnp.sum(p, axis=1, keepdims=True) + es)
    return p * r, es * r


def _stack_heads(ref, lane):
    parts = []
    for pr in range(REP // 2):
        tile = ref[:, pr * LANES:(pr + 1) * LANES]
        parts += [jnp.where(lane, tile, 0.0), jnp.where(lane, 0.0, tile)]
    return jnp.concatenate(parts, axis=0)


def _unstack_heads(x, lane):
    return jnp.concatenate([jnp.where(lane, x[(2 * pr) * BLOCK:(2 * pr + 1) * BLOCK], x[(2 * pr + 1) * BLOCK:(2 * pr + 2) * BLOCK])
                            for pr in range(REP // 2)], axis=1)


def _stack_bias(b_ref, s_ref):
    bias = jnp.concatenate([b_ref[r] for r in range(REP)], axis=0)
    snk = jnp.concatenate([jnp.broadcast_to(s_ref[r][:, 0:1], (BLOCK, 1)) for r in range(REP)], axis=0)
    return bias, snk


def _attn_fwd(qkv, bias4, sink4):
    t = qkv.shape[0]
    nb = t // BLOCK
    qs, ks, vs, bs, ss = _attn_specs(nb)

    def body(q_ref, kp_ref, kc_ref, kn_ref, vp_ref, vc_ref, vn_ref, b_ref, s_ref, o_ref):
        n = pl.program_id(1)
        kcat = jnp.concatenate([kp_ref[...], kc_ref[...], kn_ref[...]], axis=0)
        vcat = jnp.concatenate([vp_ref[...], vc_ref[...], vn_ref[...]], axis=0)
        valid = _attn_valid(n, nb)
        lane = lax.broadcasted_iota(jnp.int32, (BLOCK, LANES), 1) < ATTN_DIM
        bias, snk = _stack_bias(b_ref, s_ref)
        pn, _ = _attn_probs(_stack_heads(q_ref, lane) * SCALE, kcat, bias, snk, valid)
        o_ref[...] = _unstack_heads(_dot(pn.astype(BF16), vcat), lane).astype(o_ref.dtype)

    return pl.pallas_call(
        body, name="attn_fwd", grid=(ATTN_KV, nb), in_specs=[qs] + ks + vs + [bs, ss],
        out_specs=qs, out_shape=jax.ShapeDtypeStruct((t, D_MODEL), BF16),
        compiler_params=_cparams("parallel", "arbitrary"),
    )(qkv, qkv, qkv, qkv, qkv, qkv, qkv, bias4, sink4)


def _attn_bwd(qkv, bias4, sink4, do):
    t = qkv.shape[0]
    nb = t // BLOCK
    qs, ks, vs, bs, ss = _attn_specs(nb)
    part = pl.BlockSpec((3, BLOCK, LANES), lambda g, n: (0, n, g))

    def body(q_ref, kp_ref, kc_ref, kn_ref, vp_ref, vc_ref, vn_ref, b_ref, s_ref, do_ref,
             dq_ref, dk_ref, dv_ref, db_ref, ds_ref):
        n = pl.program_id(1)

        @pl.when(n == 0)
        def _():
            db_ref[...] = jnp.zeros_like(db_ref)
            ds_ref[...] = jnp.zeros_like(ds_ref)

        kcat = jnp.concatenate([kp_ref[...], kc_ref[...], kn_ref[...]], axis=0)
        vcat = jnp.concatenate([vp_ref[...], vc_ref[...], vn_ref[...]], axis=0)
        valid = _attn_valid(n, nb)
        lane = lax.broadcasted_iota(jnp.int32, (BLOCK, LANES), 1) < ATTN_DIM
        bias, snk = _stack_bias(b_ref, s_ref)
        q = _stack_heads(q_ref, lane)
        do = _stack_heads(do_ref, lane)
        pn, psink = _attn_probs(q * SCALE, kcat, bias, snk, valid)
        dp = _dot_nt(do, vcat)
        delta = jnp.sum(pn * dp, axis=1, keepdims=True)
        dsc = pn * (dp - delta)
        dsink = psink * delta
        for r in range(REP):
            rows = slice(r * BLOCK, (r + 1) * BLOCK)
            db_ref[r] += dsc[rows]
            ds_ref[r] += jnp.broadcast_to(-jnp.sum(dsink[rows], axis=0, keepdims=True), (1, LANES))
        dsb = (dsc * SCALE).astype(BF16)
        dq_ref[...] = _unstack_heads(_dot(dsb, kcat), lane).astype(dq_ref.dtype)
        dk = _dot_tn(dsb, q)
        dv = _dot_tn(pn.astype(BF16), do)
        for j in range(3):
            dk_ref[j] = dk[j * BLOCK:(j + 1) * BLOCK]
            dv_ref[j] = dv[j * BLOCK:(j + 1) * BLOCK]

    kv_cols = ATTN_KV * LANES
    return pl.pallas_call(
        body, name="attn_bwd", grid=(ATTN_KV, nb), in_specs=[qs] + ks + vs + [bs, ss, qs],
        out_specs=[qs, part, part, bs, ss],
        out_shape=[jax.ShapeDtypeStruct((t, D_MODEL), BF16), jax.ShapeDtypeStruct((3, t, kv_cols), F32),
                   jax.ShapeDtypeStruct((3, t, kv_cols), F32), jax.ShapeDtypeStruct(bias4.shape, F32),
                   jax.ShapeDtypeStruct(sink4.shape, F32)],
        compiler_params=_cparams("parallel", "arbitrary"),
    )(qkv, qkv, qkv, qkv, qkv, qkv, qkv, bias4, sink4, do)


def _kv_combine(name, parts):
    _, t, cols = parts.shape

    def body(p_ref, o_ref):
        z = jnp.zeros((BLOCK, LANES), F32)
        from_next = jnp.concatenate([p_ref[0, BLOCK:, :], z], axis=0)
        from_prev = jnp.concatenate([z, p_ref[2, :t - BLOCK, :]], axis=0)
        o_ref[...] = (from_next + p_ref[1] + from_prev).astype(o_ref.dtype)

    return pl.pallas_call(
        body, name=name, grid=(cols // LANES,),
        in_specs=[pl.BlockSpec((3, t, LANES), lambda g: (0, 0, g))],
        out_specs=pl.BlockSpec((t, LANES), lambda g: (0, g)),
        out_shape=jax.ShapeDtypeStruct((t, cols), BF16),
        compiler_params=_cparams("parallel"),
    )(parts)


def _t5_bucket(rel):
    nb = N_BUCKETS // 2
    max_exact = nb // 2
    ret = jnp.where(rel > 0, nb, 0)
    n = jnp.abs(rel)
    nf = jnp.maximum(n, 1).astype(jnp.float32)
    large = max_exact + (jnp.log(nf / max_exact) / math.log(MAX_DISTANCE / max_exact) * (nb - max_exact)).astype(jnp.int32)
    large = jnp.minimum(large, nb - 1)
    return ret + jnp.where(n < max_exact, n, large)


def _bucket_map():
    i = jnp.arange(BLOCK)[:, None]
    j = jnp.arange(3 * BLOCK)[None, :]
    return _t5_bucket(j - BLOCK - i)


def _bias_from_table(table, onehot_t):
    def body(t_ref, o_ref, out_ref):
        out_ref[...] = _dot(t_ref[...], o_ref[...], HI)

    return pl.pallas_call(body, name="bias_from_table", out_shape=jax.ShapeDtypeStruct((ATTN_HEADS, onehot_t.shape[1]), F32),
                          compiler_params=pltpu.CompilerParams(vmem_limit_bytes=VMEM_LIMIT_BYTES))(table.T, onehot_t)


def _bias_table_grad(dbias, onehot_t):
    def body(d_ref, o_ref, out_ref):
        out_ref[...] = _dot_nt(d_ref[...], o_ref[...], HI)

    return pl.pallas_call(body, name="bias_table_grad", out_shape=jax.ShapeDtypeStruct((ATTN_HEADS, N_BUCKETS), F32),
                          compiler_params=pltpu.CompilerParams(vmem_limit_bytes=VMEM_LIMIT_BYTES))(dbias, onehot_t)


HBM_SPEC = pl.BlockSpec(memory_space=pl.ANY)


def _comm_call(name, body, xs, out_shapes, n_sems):
    n = len(xs)
    return pl.pallas_call(
        body, name=name, in_specs=[HBM_SPEC] * n, out_specs=[HBM_SPEC] * n, out_shape=out_shapes,
        scratch_shapes=[pltpu.SemaphoreType.DMA((n * n_sems,)), pltpu.SemaphoreType.DMA((n * n_sems,)),
                        pltpu.SemaphoreType.DMA((n,))],
    )(*xs)


def _allgather_chips(name, xs):
    n = len(xs)

    def body(*refs):
        x_refs, out_refs, (send_sems, recv_sems, local_sems) = refs[:n], refs[n:2 * n], refs[2 * n:]
        mx, my, mc = lax.axis_index("x"), lax.axis_index("y"), lax.axis_index("c")
        me = 2 * mx + my
        chips = [(1 - mx, my), (mx, 1 - my), (1 - mx, 1 - my)]
        sibling = (mx, my, 1 - mc)

        def part(i, slot, h):
            r2 = xs[i].shape[0] // 2
            return out_refs[i].at[slot, pl.ds(h * r2, r2)]

        def copy(i, k, src, dst, to):
            return pltpu.make_async_remote_copy(src_ref=src, dst_ref=dst, send_sem=send_sems.at[6 * i + k],
                                                recv_sem=recv_sems.at[6 * i + k], device_id=to, device_id_type=MESH)

        mine = [pltpu.make_async_copy(x_refs[i], out_refs[i].at[me], local_sems.at[i]) for i in range(n)]
        first = [copy(i, k, x_refs[i].at[pl.ds(mc * (xs[i].shape[0] // 2), xs[i].shape[0] // 2)], part(i, me, mc), (px, py, mc))
                 for i in range(n) for k, (px, py) in enumerate(chips)]
        for cp in mine + first:
            cp.start()
        passed = []
        for k, (px, py) in enumerate(chips):
            for i in range(n):
                landed = part(i, 2 * px + py, mc)
                copy(i, k, landed, landed, (px, py, mc)).wait_recv()
                passed.append(copy(i, 3 + k, landed, landed, sibling))
                passed[-1].start()
        for k, (px, py) in enumerate(chips):
            for i in range(n):
                theirs = part(i, 2 * px + py, 1 - mc)
                copy(i, 3 + k, theirs, theirs, sibling).wait_recv()
        for cp in first + passed:
            cp.wait_send()
        for cp in mine:
            cp.wait()

    return _comm_call(name, body, xs, [jax.ShapeDtypeStruct((4,) + x.shape, x.dtype) for x in xs], 6)


SEM_SPEC = pl.BlockSpec(memory_space=pltpu.SEMAPHORE)
DATAFLOW = pltpu.SideEffectType.DATAFLOW_SIDE_EFFECTING


def _gather_start(name, x, after):
    def body(x_ref, land_ref, after_ref, send_sems, recv_sems, x_thru, land_thru, token):
        mx, my, mc = lax.axis_index("x"), lax.axis_index("y"), lax.axis_index("c")
        chips = [(1 - mx, my), (mx, 1 - my), (1 - mx, 1 - my)]
        barrier = pltpu.get_barrier_semaphore()
        for px, py in chips:
            pl.semaphore_signal(barrier, inc=1, device_id=(px, py, mc), device_id_type=MESH)
        pl.semaphore_wait(barrier, len(chips))
        for k, (px, py) in enumerate(chips):
            pltpu.make_async_remote_copy(src_ref=x_ref, dst_ref=land_ref.at[2 * mx + my], send_sem=send_sems.at[k],
                                         recv_sem=recv_sems.at[k], device_id=(px, py, mc), device_id_type=MESH).start()
        token[...] = jnp.zeros_like(token)

    land = lax.empty((4,) + x.shape, x.dtype)
    return pl.pallas_call(
        body, name=name,
        out_shape=(pltpu.SemaphoreType.DMA((3,)), pltpu.SemaphoreType.DMA((3,)), pltpu.HBM(x.shape, x.dtype),
                   pltpu.HBM(land.shape, land.dtype), jax.ShapeDtypeStruct((SUBLANES, LANES), F32)),
        in_specs=(HBM_SPEC, HBM_SPEC, HBM_SPEC),
        out_specs=(SEM_SPEC, SEM_SPEC, HBM_SPEC, HBM_SPEC, pl.BlockSpec(memory_space=pltpu.VMEM)),
        input_output_aliases={0: 2, 1: 3},
        compiler_params=pltpu.CompilerParams(has_side_effects=DATAFLOW, collective_id=0),
    )(pltpu.with_memory_space_constraint(x, pltpu.HBM), pltpu.with_memory_space_constraint(land, pltpu.HBM), after)


def _gather_wait(name, send_sems, recv_sems, x_thru, land_thru, after):
    def body(x_ref, land_ref, send_sems, recv_sems, after_ref, x_dead, got_ref):
        mx, my, mc = lax.axis_index("x"), lax.axis_index("y"), lax.axis_index("c")
        chips = [(1 - mx, my), (mx, 1 - my), (1 - mx, 1 - my)]
        for k, (px, py) in enumerate(chips):
            copy = pltpu.make_async_remote_copy(src_ref=x_ref, dst_ref=land_ref.at[2 * px + py], send_sem=send_sems.at[k],
                                                recv_sem=recv_sems.at[k], device_id=(px, py, mc), device_id_type=MESH)
            copy.wait_send()
            copy.wait_recv()

    return pl.pallas_call(
        body, name=name, out_shape=(pltpu.HBM(x_thru.shape, x_thru.dtype), pltpu.HBM(land_thru.shape, land_thru.dtype)),
        in_specs=(HBM_SPEC, HBM_SPEC, SEM_SPEC, SEM_SPEC, HBM_SPEC), out_specs=(HBM_SPEC, HBM_SPEC),
        input_output_aliases={0: 0, 1: 1},
        compiler_params=pltpu.CompilerParams(has_side_effects=DATAFLOW),
    )(x_thru, land_thru, send_sems, recv_sems, after)[1]


def _swap_cores(name, xs):
    n = len(xs)

    def body(*refs):
        x_refs, out_refs, (send_sems, recv_sems, _) = refs[:n], refs[n:2 * n], refs[2 * n:]
        mx, my, mc = lax.axis_index("x"), lax.axis_index("y"), lax.axis_index("c")
        sends = [pltpu.make_async_remote_copy(src_ref=x_refs[i].at[1 - mc], dst_ref=out_refs[i], send_sem=send_sems.at[i],
                                              recv_sem=recv_sems.at[i], device_id=(mx, my, 1 - mc), device_id_type=MESH)
                 for i in range(n)]
        for cp in sends:
            cp.start()
        for cp in sends:
            cp.wait()

    return _comm_call(name, body, xs, [jax.ShapeDtypeStruct(x.shape[1:], x.dtype) for x in xs], 1)


def _scatter_chips(name, gs):
    n = len(gs)

    def body(*refs):
        g_refs, out_refs, (send_sems, recv_sems, local_sems) = refs[:n], refs[n:2 * n], refs[2 * n:]
        mx, my, mc = lax.axis_index("x"), lax.axis_index("y"), lax.axis_index("c")
        me = 2 * mx + my
        chips = [(1 - mx, my), (mx, 1 - my), (1 - mx, 1 - my)]

        def copy(i, k, src, dst, to):
            return pltpu.make_async_remote_copy(src_ref=src, dst_ref=dst, send_sem=send_sems.at[3 * i + k],
                                                recv_sem=recv_sems.at[3 * i + k], device_id=to, device_id_type=MESH)

        mine = [pltpu.make_async_copy(g_refs[i].at[me], out_refs[i].at[me], local_sems.at[i]) for i in range(n)]
        sends = [copy(i, k, g_refs[i].at[2 * px + py], out_refs[i].at[me], (px, py, mc))
                 for i in range(n) for k, (px, py) in enumerate(chips)]
        for cp in mine + sends:
            cp.start()
        for i in range(n):
            for k, (px, py) in enumerate(chips):
                copy(i, k, g_refs[i].at[0], out_refs[i].at[2 * px + py], (px, py, mc)).wait_recv()
        for cp in sends:
            cp.wait_send()
        for cp in mine:
            cp.wait()

    return _comm_call(name, body, gs, [jax.ShapeDtypeStruct(g.shape, g.dtype) for g in gs], 3)


def _allgather_all(name, x):
    def body(x_ref, out_ref, send_sems, recv_sems, local_sem):
        mx, my, mc = lax.axis_index("x"), lax.axis_index("y"), lax.axis_index("c")
        me = 4 * mx + 2 * my + mc
        flips = [(fx, fy, fc) for fx in (0, 1) for fy in (0, 1) for fc in (0, 1)][1:]
        peers = [(mx ^ fx, my ^ fy, mc ^ fc) for fx, fy, fc in flips]
        mine = pltpu.make_async_copy(x_ref, out_ref.at[me], local_sem)
        mine.start()
        sends = [pltpu.make_async_remote_copy(src_ref=x_ref, dst_ref=out_ref.at[me], send_sem=send_sems.at[k],
                                              recv_sem=recv_sems.at[k], device_id=peer, device_id_type=MESH)
                 for k, peer in enumerate(peers)]
        for cp in sends:
            cp.start()
        for k, (px, py, pc) in enumerate(peers):
            pltpu.make_async_remote_copy(src_ref=x_ref, dst_ref=out_ref.at[4 * px + 2 * py + pc], send_sem=send_sems.at[k],
                                         recv_sem=recv_sems.at[k], device_id=(px, py, pc), device_id_type=MESH).wait_recv()
        for cp in sends:
            cp.wait_send()
        mine.wait()

    return pl.pallas_call(
        body, name=name, in_specs=[HBM_SPEC], out_specs=HBM_SPEC,
        out_shape=jax.ShapeDtypeStruct((8,) + x.shape, x.dtype),
        scratch_shapes=[pltpu.SemaphoreType.DMA((7,)), pltpu.SemaphoreType.DMA((7,)), pltpu.SemaphoreType.DMA],
    )(x)


def _allgather_cores(name, xs):
    n = len(xs)

    def body(*refs):
        x_refs, out_refs, (send_sems, recv_sems, local_sems) = refs[:n], refs[n:2 * n], refs[2 * n:]
        mx, my, mc = lax.axis_index("x"), lax.axis_index("y"), lax.axis_index("c")

        def copy(i, slot):
            return pltpu.make_async_remote_copy(src_ref=x_refs[i], dst_ref=out_refs[i].at[slot], send_sem=send_sems.at[i],
                                                recv_sem=recv_sems.at[i], device_id=(mx, my, 1 - mc), device_id_type=MESH)

        mine = [pltpu.make_async_copy(x_refs[i], out_refs[i].at[mc], local_sems.at[i]) for i in range(n)]
        sends = [copy(i, mc) for i in range(n)]
        for cp in mine + sends:
            cp.start()
        for i in range(n):
            copy(i, 1 - mc).wait_recv()
        for cp in sends:
            cp.wait_send()
        for cp in mine:
            cp.wait()

    return _comm_call(name, body, xs, [jax.ShapeDtypeStruct((2,) + x.shape, x.dtype) for x in xs], 1)


def _sum_slots(name, st, tb=256):
    n, r, c = st.shape
    tb = _pick(r, (tb, 32))

    def body(s_ref, o_ref):
        acc = s_ref[0].astype(F32)
        for k in range(1, n):
            acc = acc + s_ref[k].astype(F32)
        o_ref[...] = acc

    return pl.pallas_call(
        body, name=name, grid=(r // tb,), in_specs=[pl.BlockSpec((n, tb, c), lambda i: (0, i, 0))],
        out_specs=pl.BlockSpec((tb, c), lambda i: (i, 0)), out_shape=jax.ShapeDtypeStruct((r, c), F32),
        compiler_params=_cparams("parallel"),
    )(st)


def _adamw(name, w, g, m, v):
    def fn(w, g, m, v):
        m2 = ADAM_B1 * m + (1.0 - ADAM_B1) * g
        v2 = ADAM_B2 * v + (1.0 - ADAM_B2) * jnp.square(g)
        m_hat = m2 / (1.0 - ADAM_B1 ** ADAM_STEP)
        v_hat = v2 / (1.0 - ADAM_B2 ** ADAM_STEP)
        delta = -ADAM_LR * (m_hat / (jnp.sqrt(v_hat) + ADAM_EPS) + ADAM_WD * w)
        return [delta, m2, v2], []

    tb = _pick(w.shape[0], (256, 32))
    return _rowwise(name, fn, [_row(w), _row(g), _row(m), _row(v)], [], [(w.shape[1], F32)] * 3, tb=tb)


BIG = ("w_ssd_out", "w_attn_out", "w_o", "w_mlp_in", "w_mlp_out", "conv_w")
SMALL = ("pre_mix_norm", "b_gate", "conv_b", "dt_bias", "a_log", "d_skip", "ssd_norm", "attn_sink", "rel_bias_table",
         "post_mix_norm", "pre_mlp_norm", "post_mlp_norm")
ALL_W = ("pre_mix_norm", "w_in", "b_gate", "conv_w", "conv_b", "dt_bias", "a_log", "d_skip", "ssd_norm", "w_ssd_out",
         "attn_sink", "rel_bias_table", "w_attn_out", "w_o", "post_mix_norm", "pre_mlp_norm", "w_mlp_in", "w_mlp_out",
         "post_mlp_norm")


def _pack_rows(parts, rows, dtype):
    flat = jnp.concatenate([p.reshape(-1, D_MODEL).astype(dtype) for p in parts], axis=0)
    return jnp.pad(flat, ((0, rows - flat.shape[0]), (0, 0)))


def _pack_big(shards, dtype):
    return _pack_rows([shards[n] for n in BIG], BIG_ROWS, dtype)


def _unpack_big(flat, like):
    out, r = {}, 0
    for n in BIG:
        shp = like[n].shape
        nr = math.prod(shp) // D_MODEL
        out[n] = flat[r:r + nr].reshape(shp)
        r += nr
    return out


def _pack_small(parts, extra=None):
    flat = jnp.concatenate([parts[n].reshape(-1).astype(F32) for n in SMALL] + ([extra.reshape(-1)] if extra is not None else []))
    return jnp.pad(flat, (0, SMALL_ROWS * D_MODEL - flat.shape[0])).reshape(SMALL_ROWS, D_MODEL)


def _unpack_small(flat2, like):
    flat = flat2.reshape(-1)
    out, r = {}, 0
    for n in SMALL:
        shp = like[n].shape
        k = math.prod(shp)
        out[n] = flat[r:r + k].reshape(shp)
        r += k
    return out, flat[r]


def _shard_of_full(name, full, s):
    if name in ("w_in", "w_mlp_in"):
        w = full.shape[2] // 4
        return full[:, :, s * w:(s + 1) * w]
    if name == "conv_w":
        w = full.shape[3] // 4
        return full[:, :, :, s * w:(s + 1) * w]
    w = full.shape[1] // 4
    return full[:, s * w:(s + 1) * w, :]


def _full_of_shards(name, shards):
    axis = {"w_in": 2, "w_mlp_in": 2, "conv_w": 3}.get(name, 1)
    return jnp.concatenate(shards, axis=axis)


def _to_proj_layout(w):
    z, xbc, dt, q, k, v, gates = (w[..., 0:2048], w[..., 2048:6144], w[..., 6144:6208], w[..., 6208:7232],
                                  w[..., 7232:7488], w[..., 7488:7744], w[..., 7744:9792])
    pad = jnp.zeros(w.shape[:-1] + (N_MAIN - OFF_DT - dt.shape[-1],), w.dtype)

    def doubled(a):
        h = a.reshape(a.shape[:-1] + (ATTN_KV, 1, ATTN_DIM))
        return jnp.broadcast_to(h, a.shape[:-1] + (ATTN_KV, 2, ATTN_DIM)).reshape(a.shape[:-1] + (2 * a.shape[-1],))

    return jnp.concatenate([z, gates, xbc, dt, pad, q, doubled(k), doubled(v)], axis=-1)


def _from_proj_layout(w):
    z, gates, xbc, dt, q, k2, v2 = (w[..., 0:2048], w[..., 2048:4096], w[..., 4096:8192], w[..., 8192:8256],
                                    w[..., 8320:9344], w[..., 9344:9856], w[..., 9856:10368])

    def folded(a):
        return a.reshape(a.shape[:-1] + (ATTN_KV, 2, ATTN_DIM)).sum(axis=-2).reshape(a.shape[:-1] + (a.shape[-1] // 2,))

    return jnp.concatenate([z, xbc, dt, q, folded(k2), folded(v2), gates], axis=-1)


def _layer_fwd(h1, x, W, P, l, bias4):
    t = x.shape[0]
    S = {"x": x, "h1": h1}
    proj = _matmul("proj", h1, W["w_in_main"][l], "nn")
    qkv = _matmul("proj_qkv", h1, W["w_in_qkv"][l], "nn", out_dtype=BF16)
    S["proj"] = proj
    pre, act = _conv_fwd(proj, W["conv_w"][l], P["conv_b"][l].reshape(1, CONV_DIM))
    S["pre"], S["act"] = pre, act

    dtb = jnp.pad(P["dt_bias"][l].reshape(1, 2 * SSD_HEADS), ((0, 0), (0, LANES - 2 * SSD_HEADS)))

    def dt_fn(raw, b):
        v = raw + b
        dt = jnp.maximum(v, 0.0) + jnp.log1p(jnp.exp(-jnp.abs(v)))
        expand = (jnp.right_shift(lax.broadcasted_iota(jnp.int32, (LANES, 2 * D_INNER), 1), 6)
                  == lax.broadcasted_iota(jnp.int32, (LANES, 2 * D_INNER), 0)).astype(BF16)
        hi = dt.astype(BF16)
        rest = dt - hi.astype(F32)
        mid = rest.astype(BF16)
        lo = (rest - mid.astype(F32)).astype(BF16)
        return [dt, _dot(hi, expand) + _dot(mid, expand) + _dot(lo, expand)], []

    dt, dte = _rowwise("dt_fwd", dt_fn, [_row(proj, LANES, OFF_DT // LANES)], [dtb], [(LANES, F32), (2 * D_INNER, F32)], tb=512)
    dt2 = jnp.stack([dt[:, 0:SSD_HEADS], dt[:, SSD_HEADS:2 * SSD_HEADS]])
    dt2t = dt2.transpose(0, 2, 1)
    a = -jnp.exp(P["a_log"][l])
    a_row, a_col = a.reshape(2, 1, SSD_HEADS), a.reshape(2, SSD_HEADS, 1)
    S["dt2"], S["dt2t"], S["a_row"], S["a_col"], S["dte"] = dt2, dt2t, a_row, a_col, dte
    y2, states = _ssd_fwd(act, dt2, dt2t, a_row, a_col, dte)
    S["states"] = states

    dsk = jnp.repeat(P["d_skip"][l], SSD_HEAD_DIM).reshape(1, D_INNER)
    nw = P["ssd_norm"][l].reshape(1, D_INNER)
    S["dsk"], S["nw"] = dsk, nw

    def gn_fn(yf, yb, xs, z, dsk, nw):
        y = yf + yb + xs * dsk
        return [y, _gated_norm_fwd(y, z, nw)], []

    y, yn = _rowwise("gated_norm_fwd", gn_fn,
                     [_row(y2, lead=0), _row(y2, lead=1), _row(act, D_INNER, 0), _row(proj, D_INNER, OFF_Z // D_INNER)],
                     [dsk, nw], [(D_INNER, F32), (D_INNER, BF16)])
    S["y"], S["yn"] = y, yn
    if "late" in W:
        W.update(W.pop("late")(yn))
    y_ssd = _matmul("ssd_out", yn, W["w_ssd_out"][l], "nn")
    S["y_ssd"] = y_ssd

    sink4 = jnp.broadcast_to(P["attn_sink"][l].reshape(ATTN_KV, REP, 1, 1), (ATTN_KV, REP, 1, LANES))
    S["qkv"], S["sink4"] = qkv, sink4
    o = _attn_fwd(qkv, bias4, sink4)
    S["o"] = o
    y_attn = _matmul("attn_out", o, W["w_attn_out"][l], "nn")
    S["y_attn"] = y_attn

    bg = P["b_gate"][l].reshape(1, 2 * D_MODEL)
    S["bg"] = bg

    def merge_fn(gates, ys, ya, b):
        g = jax.nn.sigmoid(gates + b)
        return [g[:, :D_MODEL] * ys + g[:, D_MODEL:] * ya], []

    (mix_in,) = _rowwise("merge_fwd", merge_fn, [_row(proj, 2 * D_MODEL, OFF_G // (2 * D_MODEL)), _row(y_ssd), _row(y_attn)],
                         [bg], [(D_MODEL, BF16)])
    S["mix_in"] = mix_in
    mixed = _matmul("w_o", mix_in, W["w_o"][l], "nn")
    S["mixed"] = mixed

    g_pm = P["post_mix_norm"][l].reshape(1, D_MODEL)
    g_pl = P["pre_mlp_norm"][l].reshape(1, D_MODEL)

    def postmix_fn(x, mixed, g1, g2):
        x2 = x + _rms_fwd(mixed, g1)
        return [x2, _rms_fwd(x2, g2)], []

    x2, h2 = _rowwise("post_mix_fwd", postmix_fn, [_row(x), _row(mixed)], [g_pm, g_pl], [(D_MODEL, F32), (D_MODEL, BF16)])
    S["x2"], S["h2"] = x2, h2
    a1 = _matmul("mlp_in", h2, W["w_mlp_in"][l], "nn", out_dtype=BF16,
                 epilogue=lambda acc: jnp.square(jnp.maximum(acc, 0.0)))
    S["a1"] = a1
    f2 = _matmul("mlp_out", a1, W["w_mlp_out"][l], "nn")
    S["f2"] = f2
    return S


def _layer_bwd(S, dx3, W, P, l, bias4, onehot_t):
    t = dx3.shape[0]
    G = {}
    g_pmlp = P["post_mlp_norm"][l].reshape(1, D_MODEL)

    def b1_fn(f2, dx3, g):
        df2, dg = _rms_bwd(f2, g, dx3)
        return [df2], [dg]

    df2, G["post_mlp_norm"] = _rowwise("post_mlp_bwd", b1_fn, [_row(S["f2"]), _row(dx3)], [g_pmlp], [(D_MODEL, BF16)], [(1, D_MODEL)])
    df1 = _matmul("d_f1", df2, W["w_mlp_out"][l], "nt", out_dtype=BF16,
                  epilogue=lambda acc, a1: acc * (2.0 * jnp.sqrt(a1.astype(F32))), extras=[S["a1"]])
    G["w_mlp_out"] = _matmul("dw_mlp_out", S["a1"], df2, "tn")
    dh2 = _matmul("d_h2", df1, W["w_mlp_in"][l], "nt")
    G["w_mlp_in"] = _matmul("dw_mlp_in", S["h2"], df1, "tn")

    g_pm = P["post_mix_norm"][l].reshape(1, D_MODEL)
    g_pl = P["pre_mlp_norm"][l].reshape(1, D_MODEL)

    def b3_fn(x2, dh2, dx3, mixed, g_pl, g_pm):
        d1, dgl = _rms_bwd(x2, g_pl, dh2)
        dx2 = dx3 + d1
        dmixed, dgm = _rms_bwd(mixed, g_pm, dx2)
        return [dx2, dmixed], [dgl, dgm]

    dx2, dmixed, G["pre_mlp_norm"], G["post_mix_norm"] = _rowwise(
        "post_mix_bwd", b3_fn, [_row(S["x2"]), _row(dh2), _row(dx3), _row(S["mixed"])], [g_pl, g_pm],
        [(D_MODEL, F32), (D_MODEL, BF16)], [(1, D_MODEL), (1, D_MODEL)])
    dmix_in = _matmul("d_mix_in", dmixed, W["w_o"][l], "nt")
    G["w_o"] = _matmul("dw_o", S["mix_in"], dmixed, "tn")

    proj = S["proj"]

    def b4_fn(gates, ys, ya, dmix, b):
        g = jax.nn.sigmoid(gates + b)
        gs, ga = g[:, :D_MODEL], g[:, D_MODEL:]
        dg = jnp.concatenate([ys * dmix, ya * dmix], axis=-1) * g * (1.0 - g)
        return [gs * dmix, ga * dmix, dg], [jnp.sum(dg, axis=0, keepdims=True)]

    dy_ssd, dy_attn, dgates, G["b_gate"] = _rowwise(
        "merge_bwd", b4_fn, [_row(proj, 2 * D_MODEL, OFF_G // (2 * D_MODEL)), _row(S["y_ssd"]), _row(S["y_attn"]), _row(dmix_in)],
        [S["bg"]], [(D_MODEL, BF16), (D_MODEL, BF16), (2 * D_MODEL, BF16)], [(1, 2 * D_MODEL)])

    dyn = _matmul("d_yn", dy_ssd, W["w_ssd_out"][l], "nt")
    G["w_ssd_out"] = _matmul("dw_ssd_out", S["yn"], dy_ssd, "tn")
    do = _matmul("d_o", dy_attn, W["w_attn_out"][l], "nt", out_dtype=BF16)
    G["w_attn_out"] = _matmul("dw_attn_out", S["o"], dy_attn, "tn")

    dq, dkp, dvp, dbias4, dsink4 = _attn_bwd(S["qkv"], bias4, S["sink4"], do)
    dk = _kv_combine("dk_combine", dkp)
    dv = _kv_combine("dv_combine", dvp)
    G["attn_sink"] = dsink4[:, :, 0, 0].reshape(ATTN_HEADS)
    G["rel_bias_table"] = _bias_table_grad(dbias4.reshape(ATTN_HEADS, BLOCK * 3 * BLOCK), onehot_t).T

    act = S["act"]

    def b5_fn(y, z, xs, dyn, nw, dsk):
        dy, dz, dnw = _gated_norm_bwd(y, z, nw, dyn)
        return [dy, dz], [dnw, jnp.sum(dy * xs, axis=0, keepdims=True)]

    dy, dz, G["ssd_norm"], dskip_cols = _rowwise(
        "gated_norm_bwd", b5_fn, [_row(S["y"]), _row(proj, D_INNER, OFF_Z // D_INNER), _row(act, D_INNER, 0), _row(dyn)],
        [S["nw"], S["dsk"]], [(D_INNER, F32), (D_INNER, BF16)], [(1, D_INNER), (1, D_INNER)])
    G["d_skip"] = dskip_cols.reshape(SSD_HEADS, SSD_HEAD_DIM).sum(axis=-1)

    dxs2, dbs2, dcs2, ddt2, da2 = _ssd_bwd(act, S["dt2"], S["dt2t"], S["a_row"], S["a_col"], S["dte"], dy, S["states"])
    G["a_log"] = da2.reshape(2, SSD_HEADS) * S["a_row"].reshape(2, SSD_HEADS)

    def b6_fn(dxf, dxb, dy, dbf, dbb, dcf, dcb, pre, dsk):
        dact = jnp.concatenate([dxf + dxb + dy * dsk, dbf + dbb, dcf + dcb], axis=-1)
        return [dact * _silu_grad(pre)], []

    (dpre,) = _rowwise("silu_bwd", b6_fn,
                       [_row(dxs2, lead=0), _row(dxs2, lead=1), _row(dy), _row(dbs2, lead=0), _row(dbs2, lead=1),
                        _row(dcs2, lead=0), _row(dcs2, lead=1), _row(S["pre"])], [S["dsk"]], [(CONV_DIM, F32)], tb=128)
    du, dconv_w, dconv_b = _conv_bwd(dpre, proj, W["conv_w"][l])
    G["conv_w"] = dconv_w.reshape(SSD_CONV, 1, CONV_DIM)
    G["conv_b"] = dconv_b.reshape(CONV_DIM)

    dtb = jnp.pad(P["dt_bias"][l].reshape(1, 2 * SSD_HEADS), ((0, 0), (0, LANES - 2 * SSD_HEADS)))
    ddt = jnp.pad(jnp.concatenate([ddt2[0], ddt2[1]], axis=-1), ((0, 0), (0, LANES - 2 * SSD_HEADS)))

    def b7_fn(raw, ddt, b):
        draw = ddt * jax.nn.sigmoid(raw + b)
        return [draw], [jnp.sum(draw, axis=0, keepdims=True)]

    draw, ddtb = _rowwise("dt_bwd", b7_fn, [_row(proj, LANES, OFF_DT // LANES), _row(ddt)], [dtb], [(LANES, BF16)], [(1, LANES)], tb=512)
    G["dt_bias"] = ddtb[0, :2 * SSD_HEADS].reshape(2, SSD_HEADS)

    dproj = jnp.concatenate([dz, dgates, du, draw, dq, dk, dv], axis=-1)
    G["w_in"] = _from_proj_layout(_matmul("dw_in", S["h1"], dproj, "tn"))
    dh1 = _matmul("d_h1", dproj, W["w_in"][l], "nt")

    g_pre = P["pre_mix_norm"][l].reshape(1, D_MODEL)

    def b8_fn(x, dh1, dx2, g):
        d1, dg = _rms_bwd(x, g, dh1)
        return [dx2 + d1], [dg]

    dx, G["pre_mix_norm"] = _rowwise("pre_mix_bwd", b8_fn, [_row(S["x"]), _row(dh1), _row(dx2)], [g_pre], [(D_MODEL, F32)], [(1, D_MODEL)])
    return dx, G


def _reduce_scatter(tag, pieces):
    theirs = _swap_cores("swap_" + tag, pieces)
    chip_sums = []
    for i, p in enumerate(pieces):
        _, _, r2, c = p.shape
        mine = lax.dynamic_index_in_dim(p, lax.axis_index("c"), axis=0, keepdims=False).reshape(4 * r2, c)
        (cs,) = _rowwise(f"presum_{tag}{i}", lambda a, b: ([a.astype(F32) + b.astype(F32)], []),
                         [_row(mine), _row(theirs[i].reshape(4 * r2, c))], [], [(c, BF16)])
        chip_sums.append(cs.reshape(4, r2, c))
    staged = _scatter_chips("scatter_" + tag, chip_sums)
    halves = [_sum_slots(f"sum_{tag}{i}", st) for i, st in enumerate(staged)]
    shared = _allgather_cores("share_" + tag, halves)
    return [sh.reshape(2 * sh.shape[1], sh.shape[2]) for sh in shared]


def _step(x, target, shard_w, shard_m, shard_v):
    depth, _, win_cols = shard_w["w_in"].shape
    win_rows = depth * D_MODEL

    def win2d(a):
        return a.reshape(win_rows, win_cols)

    conv_rows = shard_w["conv_w"].reshape(-1, D_MODEL)
    win_g, conv_g = _allgather_chips("gather_w_in", [
        win2d(shard_w["w_in"]).astype(BF16), jnp.pad(conv_rows, ((0, 16 - conv_rows.shape[0]), (0, 0)))])
    packed = _pack_big(shard_w, BF16)
    send_sems, recv_sems, packed_thru, land_thru, token = _gather_start("gather_rest_start", packed, conv_g)

    def late(after):
        landed = _gather_wait("gather_rest_wait", send_sems, recv_sems, packed_thru, land_thru, after)
        me = 2 * lax.axis_index("x") + lax.axis_index("y")
        per_chip = [_unpack_big(jnp.where(me == s, packed, landed[s]), shard_w) for s in range(4)]
        return {n: _full_of_shards(n, [per_chip[s][n] for s in range(4)]) for n in BIG if n != "conv_w"}

    W = {"late": late}
    W["w_in"] = _to_proj_layout(jnp.concatenate([win_g[s].reshape(depth, D_MODEL, win_cols) for s in range(4)], axis=2))
    W["conv_w"] = jnp.concatenate([conv_g[s][:conv_rows.shape[0]].reshape(shard_w["conv_w"].shape) for s in range(4)],
                                  axis=3).reshape(depth, SSD_CONV, CONV_DIM)
    P = {n: shard_w[n] for n in SMALL}
    P["pre_mix_norm"] = P["pre_mix_norm"] + token[0, 0]
    loss_part, grad_x, full = _local_step(x, target, W, P)

    assert depth == 2
    win_pieces = jnp.stack([jnp.stack([full["w_in"][h][:, s * win_cols:(s + 1) * win_cols] for s in range(4)])
                            for h in range(2)]).astype(BF16)
    packs = [_pack_big({n: _shard_of_full(n, full[n], s) for n in BIG}, BF16) for s in range(4)]
    r2 = BIG_ROWS // 2
    pieces = jnp.stack([jnp.stack([p[h * r2:(h + 1) * r2] for p in packs]) for h in range(2)])
    g_win, g_big = _reduce_scatter("grads", [win_pieces, pieces])
    d_win, m_win, v_win = _adamw("adamw_w_in", win2d(shard_w["w_in"]), g_win, win2d(shard_m["w_in"]), win2d(shard_v["w_in"]))
    d_big, m_big, v_big = _adamw("adamw_big", _pack_big(shard_w, F32), g_big, _pack_big(shard_m, F32), _pack_big(shard_v, F32))

    small = _allgather_all("gather_small", _pack_small(full, loss_part))
    g_small = _sum_slots("sum_small", small, tb=SMALL_ROWS)
    d_small, m_small, v_small = _adamw("adamw_small", _pack_small(shard_w, jnp.zeros((), F32)), g_small,
                                       _pack_small(shard_m, jnp.zeros((), F32)), _pack_small(shard_v, jnp.zeros((), F32)))

    outs = {}
    for tag, win, big, sm in (("grad", g_win, g_big, g_small), ("delta", d_win, d_big, d_small),
                              ("new_m", m_win, m_big, m_small), ("new_v", v_win, v_big, v_small)):
        ub = _unpack_big(big, shard_w)
        us, extra = _unpack_small(sm, shard_w)
        outs[tag] = {"w_in": win.reshape(shard_w["w_in"].shape), **ub, **us}
        if tag == "grad":
            loss = extra
    return loss, grad_x, outs


def _local_step(x, target, W, P):
    depth = W["w_in"].shape[0]
    W = dict(W, w_in_main=W["w_in"][:, :, :N_MAIN], w_in_qkv=W["w_in"][:, :, N_MAIN:])
    onehot_t = (_bucket_map().reshape(1, -1) == jnp.arange(N_BUCKETS)[:, None]).astype(F32)
    bias = _bias_from_table(P["rel_bias_table"], onehot_t).reshape(ATTN_HEADS, BLOCK, 3 * BLOCK)
    bias4 = jnp.where(_band_mask(), bias, NEG).reshape(ATTN_KV, REP, BLOCK, 3 * BLOCK)

    def pre_fn(x, g):
        return [_rms_fwd(x, g)], []

    (h1,) = _rowwise("pre_mix_fwd", pre_fn, [_row(x)], [P["pre_mix_norm"][0].reshape(1, D_MODEL)], [(D_MODEL, BF16)])
    saved = []
    loss_cols = dxl = None
    for l in range(depth):
        S = _layer_fwd(h1, x, W, P, l, bias4)
        saved.append(S)
        g_pmlp = P["post_mlp_norm"][l].reshape(1, D_MODEL)
        if l + 1 < depth:
            def post_fn(x2, f2, g1, g2):
                x3 = x2 + _rms_fwd(f2, g1)
                return [x3, _rms_fwd(x3, g2)], []

            x, h1 = _rowwise("post_mlp_fwd", post_fn, [_row(S["x2"]), _row(S["f2"])],
                             [g_pmlp, P["pre_mix_norm"][l + 1].reshape(1, D_MODEL)], [(D_MODEL, F32), (D_MODEL, BF16)])
        else:
            def loss_fn(x2, f2, tgt, g1):
                diff = x2 + _rms_fwd(f2, g1) - tgt
                return [diff * (1.0 / D_MODEL)], [jnp.sum(diff * diff, axis=0, keepdims=True)]

            dxl, loss_cols = _rowwise("loss", loss_fn, [_row(S["x2"]), _row(S["f2"]), _row(target)], [g_pmlp],
                                      [(D_MODEL, F32)], [(1, D_MODEL)])
    loss_part = 0.5 * jnp.sum(loss_cols) / D_MODEL

    grads = [None] * depth
    dx = dxl
    for l in reversed(range(depth)):
        dx, grads[l] = _layer_bwd(saved[l], dx, W, P, l, bias4, onehot_t)
    grad_x = dx

    full = {n: jnp.stack([grads[l][n] for l in range(depth)]) for n in ALL_W if n != "rel_bias_table"}
    full["rel_bias_table"] = sum(grads[l]["rel_bias_table"] for l in range(depth))
    return loss_part, grad_x, full


def kernel(x, pre_mix_norm, w_in, b_gate, conv_w, conv_b, dt_bias, a_log, d_skip, ssd_norm, w_ssd_out, attn_sink, rel_bias_table, w_attn_out, w_o, post_mix_norm, pre_mlp_norm, w_mlp_in, w_mlp_out, post_mlp_norm, loss_target, m_pre_mix_norm, m_w_in, m_b_gate, m_conv_w, m_conv_b, m_dt_bias, m_a_log, m_d_skip, m_ssd_norm, m_w_ssd_out, m_attn_sink, m_rel_bias_table, m_w_attn_out, m_w_o, m_post_mix_norm, m_pre_mlp_norm, m_w_mlp_in, m_w_mlp_out, m_post_mlp_norm, v_pre_mix_norm, v_w_in, v_b_gate, v_conv_w, v_conv_b, v_dt_bias, v_a_log, v_d_skip, v_ssd_norm, v_w_ssd_out, v_attn_sink, v_rel_bias_table, v_w_attn_out, v_w_o, v_post_mix_norm, v_pre_mlp_norm, v_w_mlp_in, v_w_mlp_out, v_post_mlp_norm):
    a = locals()
    shard_w = {n: a[n] for n in ALL_W}
    shard_m = {n: a["m_" + n] for n in ALL_W}
    shard_v = {n: a["v_" + n] for n in ALL_W}
    loss, grad_x, outs = _step(x[0], loss_target[0], shard_w, shard_m, shard_v)
    return (loss, grad_x[None], *[outs["grad"][n] for n in ALL_W], *[outs["delta"][n] for n in ALL_W],
            *[outs["new_m"][n] for n in ALL_W], *[outs["new_v"][n] for n in ALL_W])
```

```python
import math

import jax
import jax.numpy as jnp
from jax import lax
from jax.experimental import pallas as pl
from jax.experimental.pallas import tpu as pltpu

F32 = jnp.float32
BF16 = jnp.bfloat16
MESH = pl.DeviceIdType.MESH

VMEM_LIMIT_BYTES = 52 * 1024 * 1024
LANES = 128
SUBLANES = 8

EPS = 1e-6
D_MODEL = 1024
D_INNER = 2048
SSD_HEADS = 32
SSD_HEAD_DIM = 64
SSD_GROUPS = 8
SSD_STATE = 128
SSD_CONV = 5
CHUNK = 128
CONV_DIM = 4096
ATTN_HEADS = 16
ATTN_KV = 4
ATTN_DIM = 64
BLOCK = 128
N_BUCKETS = 32
MAX_DISTANCE = 128
D_FF = 4096
N_IN = 9792
NEG = -1e30

OFF_Z, OFF_G, OFF_XBC, OFF_DT, N_MAIN, N_PROJ = 0, 2048, 4096, 8192, 8320, 10368
N_QKV = N_PROJ - N_MAIN

ADAM_LR, ADAM_B1, ADAM_B2, ADAM_EPS, ADAM_WD, ADAM_STEP = 0.001, 0.9, 0.999, 1e-08, 0.01, 10

BIG_ROWS = 6656
SMALL_ROWS = 32


def _cparams(*sem):
    return pltpu.CompilerParams(dimension_semantics=sem, vmem_limit_bytes=VMEM_LIMIT_BYTES)


def _dot(a, b, precision=None):
    return lax.dot_general(a, b, (((1,), (0,)), ((), ())), preferred_element_type=F32, precision=precision)


def _dot_nt(a, b, precision=None):
    return lax.dot_general(a, b, (((1,), (1,)), ((), ())), preferred_element_type=F32, precision=precision)


def _dot_tn(a, b):
    return lax.dot_general(a, b, (((0,), (0,)), ((), ())), preferred_element_type=F32)


def _pick(n, prefs):
    for p in prefs:
        if n % p == 0:
            return p
    return n


def _matmul(name, a, b, mode, out_dtype=F32, epilogue=None, extras=()):
    if mode == "nn":
        (m, k), (_, n) = a.shape, b.shape
    elif mode == "nt":
        (m, k), (n, _) = a.shape, b.shape
    else:
        (k, m), (_, n) = a.shape, b.shape
    tm = _pick(m, (512, 256, 128)) if mode == "tn" else _pick(m, (1024, 512, 256, 128))
    tn = _pick(n, (1024, 1152, 640, 512, 256, 128))
    tk = _pick(k, (1024, 3456, 512, 256, 128)) if mode != "tn" else _pick(k, (2048, 512, 256, 128))
    if tk > 2048:
        tm = _pick(m, (512, 256, 128))
    nk = k // tk
    if mode == "nn":
        a_spec = pl.BlockSpec((tm, tk), lambda i, j, q: (i, q))
        b_spec = pl.BlockSpec((tk, tn), lambda i, j, q: (q, j))
        fn = _dot
    elif mode == "nt":
        a_spec = pl.BlockSpec((tm, tk), lambda i, j, q: (i, q))
        b_spec = pl.BlockSpec((tn, tk), lambda i, j, q: (j, q))
        fn = _dot_nt
    else:
        a_spec = pl.BlockSpec((tk, tm), lambda i, j, q: (q, i))
        b_spec = pl.BlockSpec((tk, tn), lambda i, j, q: (q, j))
        fn = _dot_tn

    tile = pl.BlockSpec((tm, tn), lambda i, j, q: (i, j))
    n_ex = len(extras)

    def body(a_ref, b_ref, *rest):
        ex_refs, o_ref = rest[:n_ex], rest[n_ex]

        def store(acc):
            v = acc if epilogue is None else epilogue(acc, *[r[...] for r in ex_refs])
            o_ref[...] = v.astype(o_ref.dtype)

        p = fn(a_ref[...].astype(BF16), b_ref[...].astype(BF16))
        if nk == 1:
            store(p)
        else:
            acc_ref = rest[n_ex + 1]
            q = pl.program_id(2)

            @pl.when(q == 0)
            def _():
                acc_ref[...] = p

            @pl.when((q > 0) & (q < nk - 1))
            def _():
                acc_ref[...] += p

            @pl.when(q == nk - 1)
            def _():
                store(acc_ref[...] + p)

    return pl.pallas_call(
        body, name=name, grid=(m // tm, n // tn, nk),
        in_specs=[a_spec, b_spec] + [tile] * n_ex, out_specs=tile,
        out_shape=jax.ShapeDtypeStruct((m, n), out_dtype),
        scratch_shapes=[pltpu.VMEM((tm, tn), F32)] if nk > 1 else [],
        compiler_params=_cparams("parallel", "parallel", "arbitrary"),
    )(a, b, *extras)


def _row(arr, width=None, cb=0, lead=None):
    return (arr, width, cb, lead)


def _rowwise(name, fn, rows, vecs, outs, accs=(), tb=256):
    t = rows[0][0].shape[-2]
    tb = min(tb, t)
    in_specs, args = [], []
    for arr, width, cb, lead in rows:
        w = arr.shape[-1] if width is None else width
        if lead is None:
            in_specs.append(pl.BlockSpec((tb, w), lambda i, cb=cb: (i, cb)))
        else:
            in_specs.append(pl.BlockSpec((None, tb, w), lambda i, cb=cb, lead=lead: (lead, i, cb)))
        args.append(arr)
    for v in vecs:
        in_specs.append(pl.BlockSpec(v.shape, lambda i, nd=v.ndim: (0,) * nd))
        args.append(v)
    out_shape = [jax.ShapeDtypeStruct((t, c), dt) for c, dt in outs] + [jax.ShapeDtypeStruct(s, F32) for s in accs]
    out_specs = [pl.BlockSpec((tb, c), lambda i: (i, 0)) for c, _ in outs] + [pl.BlockSpec(s, lambda i: (0, 0)) for s in accs]
    n_in, n_out = len(args), len(outs)

    def body(*refs):
        vals = [r[...] for r in refs[:n_in]]
        o_vals, a_vals = fn(*vals)
        for r, v in zip(refs[n_in:n_in + n_out], o_vals):
            r[...] = v.astype(r.dtype)
        first = pl.program_id(0) == 0
        for r, v in zip(refs[n_in + n_out:], a_vals):
            @pl.when(first)
            def _(r=r, v=v):
                r[...] = v

            @pl.when(jnp.logical_not(first))
            def _(r=r, v=v):
                r[...] += v

    res = pl.pallas_call(
        body, name=name, grid=(t // tb,), in_specs=in_specs, out_specs=out_specs, out_shape=out_shape,
        compiler_params=_cparams("arbitrary"),
    )(*args)
    return res


def _rms_fwd(x, g):
    r = lax.rsqrt(jnp.mean(x * x, axis=-1, keepdims=True) + EPS)
    return x * r * g


def _rms_bwd(x, g, dy):
    r = lax.rsqrt(jnp.mean(x * x, axis=-1, keepdims=True) + EPS)
    xh = x * r
    dxh = dy * g
    dx = r * (dxh - xh * jnp.mean(dxh * xh, axis=-1, keepdims=True))
    return dx, jnp.sum(dy * xh, axis=0, keepdims=True)


def _silu(x):
    return x * jax.nn.sigmoid(x)


def _silu_grad(x):
    s = jax.nn.sigmoid(x)
    return s * (1.0 + x * (1.0 - s))


GROUP_W = D_INNER // SSD_GROUPS


def _gated_norm_fwd(y, z, w):
    u = y * _silu(z)
    parts = []
    for j in range(SSD_GROUPS):
        ug = u[:, j * GROUP_W:(j + 1) * GROUP_W]
        parts.append(ug * lax.rsqrt(jnp.mean(ug * ug, axis=-1, keepdims=True) + EPS))
    return jnp.concatenate(parts, axis=-1) * w


def _gated_norm_bwd(y, z, w, dyn):
    sz = _silu(z)
    u = y * sz
    duh = dyn * w
    du_parts, uh_parts = [], []
    for j in range(SSD_GROUPS):
        sl = slice(j * GROUP_W, (j + 1) * GROUP_W)
        ug = u[:, sl]
        r = lax.rsqrt(jnp.mean(ug * ug, axis=-1, keepdims=True) + EPS)
        uh = ug * r
        dg = duh[:, sl]
        du_parts.append(r * (dg - uh * jnp.mean(dg * uh, axis=-1, keepdims=True)))
        uh_parts.append(uh)
    du = jnp.concatenate(du_parts, axis=-1)
    uh = jnp.concatenate(uh_parts, axis=-1)
    dw = jnp.sum(dyn * uh, axis=0, keepdims=True)
    return du * sz, du * y * _silu_grad(z), dw


HALO = SUBLANES


def _halo_specs(tb, cb, col0, t):
    nblk8 = t // HALO
    per = tb // HALO
    main = pl.BlockSpec((tb, cb), lambda j, i: (i, col0 + j))
    prev = pl.BlockSpec((HALO, cb), lambda j, i: (jnp.maximum(i * per - 1, 0), col0 + j))
    nxt = pl.BlockSpec((HALO, cb), lambda j, i: (jnp.minimum((i + 1) * per, nblk8 - 1), col0 + j))
    return main, prev, nxt


def _fill_ext(ext_ref, cur_ref, prev_ref, next_ref, tb, ni):
    i = pl.program_id(1)
    ext_ref[0:HALO, :] = jnp.where(i > 0, prev_ref[...], 0.0)
    ext_ref[HALO:HALO + tb, :] = cur_ref[...]
    ext_ref[HALO + tb:HALO + tb + HALO, :] = jnp.where(i < ni - 1, next_ref[...], 0.0)


def _conv_fwd(proj, w, b):
    t = proj.shape[0]
    tb, cb = min(512, t), 512
    ni, nj = t // tb, CONV_DIM // cb
    main, prev, nxt = _halo_specs(tb, cb, OFF_XBC // cb, t)
    pad = (SSD_CONV - 1) // 2

    def body(u_ref, up_ref, un_ref, w_ref, b_ref, pre_ref, act_ref, ext_ref):
        _fill_ext(ext_ref, u_ref, up_ref, un_ref, tb, ni)
        acc = jnp.broadcast_to(b_ref[...], (tb, cb))
        for k in range(SSD_CONV):
            acc = acc + w_ref[k:k + 1, :] * ext_ref[pl.ds(HALO + k - pad, tb), :]
        pre_ref[...] = acc
        act_ref[...] = _silu(acc)

    out = pl.BlockSpec((tb, cb), lambda j, i: (i, j))
    return pl.pallas_call(
        body, name="conv_fwd", grid=(nj, ni),
        in_specs=[main, prev, nxt, pl.BlockSpec((SSD_CONV, cb), lambda j, i: (0, j)), pl.BlockSpec((1, cb), lambda j, i: (0, j))],
        out_specs=[out, out],
        out_shape=[jax.ShapeDtypeStruct((t, CONV_DIM), F32)] * 2,
        scratch_shapes=[pltpu.VMEM((tb + 2 * HALO, cb), F32)],
        compiler_params=_cparams("parallel", "arbitrary"),
    )(proj, proj, proj, w, b)


def _conv_bwd(dpre, proj, w):
    t = proj.shape[0]
    tb, cb = min(512, t), 512
    ni, nj = t // tb, CONV_DIM // cb
    umain, uprev, unext = _halo_specs(tb, cb, OFF_XBC // cb, t)
    dmain, dprev, dnext = _halo_specs(tb, cb, 0, t)
    pad = (SSD_CONV - 1) // 2

    def body(d_ref, dp_ref, dn_ref, u_ref, up_ref, un_ref, w_ref, du_ref, dw_ref, db_ref, extd_ref, extu_ref):
        _fill_ext(extd_ref, d_ref, dp_ref, dn_ref, tb, ni)
        _fill_ext(extu_ref, u_ref, up_ref, un_ref, tb, ni)
        d = d_ref[...]
        du = jnp.zeros((tb, cb), F32)
        @pl.when(pl.program_id(1) == 0)
        def _():
            dw_ref[...] = jnp.zeros_like(dw_ref)
            db_ref[...] = jnp.zeros_like(db_ref)

        for k in range(SSD_CONV):
            du = du + w_ref[k:k + 1, :] * extd_ref[pl.ds(HALO - k + pad, tb), :]
            dw_ref[k:k + 1, :] += jnp.sum(d * extu_ref[pl.ds(HALO + k - pad, tb), :], axis=0, keepdims=True)
        du_ref[...] = du.astype(du_ref.dtype)
        db_ref[...] += jnp.sum(d, axis=0, keepdims=True)

    return pl.pallas_call(
        body, name="conv_bwd", grid=(nj, ni),
        in_specs=[dmain, dprev, dnext, umain, uprev, unext, pl.BlockSpec((SSD_CONV, cb), lambda j, i: (0, j))],
        out_specs=[pl.BlockSpec((tb, cb), lambda j, i: (i, j)), pl.BlockSpec((SSD_CONV, cb), lambda j, i: (0, j)),
                   pl.BlockSpec((1, cb), lambda j, i: (0, j))],
        out_shape=[jax.ShapeDtypeStruct((t, CONV_DIM), BF16), jax.ShapeDtypeStruct((SSD_CONV, CONV_DIM), F32),
                   jax.ShapeDtypeStruct((1, CONV_DIM), F32)],
        scratch_shapes=[pltpu.VMEM((tb + 2 * HALO, cb), F32)] * 2,
        compiler_params=_cparams("parallel", "arbitrary"),
    )(dpre, dpre, dpre, proj, proj, proj, w)


PAIR = 2 * SSD_HEAD_DIM
HI = lax.Precision.HIGHEST


def _ssd_prelude(d, dt_ref, dtt_ref, ar_ref, ac_ref):
    li = lax.broadcasted_iota(jnp.int32, (CHUNK, CHUNK), 0)
    si = lax.broadcasted_iota(jnp.int32, (CHUNK, CHUNK), 1)
    fwd = d == 0
    hi, lo = jnp.where(fwd, li, si), jnp.where(fwd, si, li)
    tri = hi >= lo
    trif = tri.astype(F32)
    trit = (hi <= lo).astype(F32)
    dt = dt_ref[...]
    adt = dt * ar_ref[...]
    adtt = dtt_ref[...] * ac_ref[...]
    p = _dot(trif, adt, HI)
    pt = _dot_nt(adtt, trif, HI)
    tot = jnp.sum(adt, axis=0, keepdims=True)
    return tri, trit, dt, p, pt, tot


def _ssd_specs(nc, rev):
    def cidx(d, c):
        up = (d == 1) if rev else (d == 0)
        return jnp.where(up, c, nc - 1 - c)

    specs = [
        pl.BlockSpec((CHUNK, D_INNER), lambda d, c: (cidx(d, c), 0)),
        pl.BlockSpec((CHUNK, 1024), lambda d, c: (cidx(d, c), 2)),
        pl.BlockSpec((CHUNK, 1024), lambda d, c: (cidx(d, c), 3)),
        pl.BlockSpec((None, CHUNK, SSD_HEADS), lambda d, c: (d, cidx(d, c), 0)),
        pl.BlockSpec((None, SSD_HEADS, CHUNK), lambda d, c: (d, 0, cidx(d, c))),
        pl.BlockSpec((None, 1, SSD_HEADS), lambda d, c: (d, 0, 0)),
        pl.BlockSpec((None, SSD_HEADS, 1), lambda d, c: (d, 0, 0)),
        pl.BlockSpec((CHUNK, D_INNER), lambda d, c: (cidx(d, c), d)),
    ]
    return cidx, specs


def _head_decay(tri, p, pt, tot, h):
    pb = jnp.broadcast_to(p[:, h:h + 1], (CHUNK, CHUNK))
    dec = jnp.exp(jnp.where(tri, pb - pt[h:h + 1, :], NEG))
    return dec, jnp.exp(tot[:, h:h + 1] - pb), jnp.exp(pb)


def _ssd_fwd(act, dt2, dt2t, a_row, a_col, dte):
    t = act.shape[0]
    nc = t // CHUNK
    cidx, specs = _ssd_specs(nc, rev=False)

    def body(xs_ref, bs_ref, cs_ref, dt_ref, dtt_ref, ar_ref, ac_ref, dte_ref, y_ref, st_ref, h_ref):
        d, c = pl.program_id(0), pl.program_id(1)

        @pl.when(c == 0)
        def _():
            h_ref[...] = jnp.zeros_like(h_ref)

        st_ref[...] = h_ref[...]
        tri, _, _, p, pt, tot = _ssd_prelude(d, dt_ref, dtt_ref, ar_ref, ac_ref)
        etot = jnp.exp(tot)
        lane = lax.broadcasted_iota(jnp.int32, (CHUNK, PAIR), 1) < SSD_HEAD_DIM
        rowh = lax.broadcasted_iota(jnp.int32, (PAIR, SSD_STATE), 0) < SSD_HEAD_DIM
        for g in range(SSD_GROUPS):
            gs = slice(g * SSD_STATE, (g + 1) * SSD_STATE)
            bg = bs_ref[:, gs]
            cb = cs_ref[:, gs].astype(BF16)
            cbm = _dot_nt(cb, bg.astype(BF16))
            for pr in range(2):
                h0 = g * 4 + pr * 2
                h1 = h0 + 1
                sl = slice(h0 * SSD_HEAD_DIM, h0 * SSD_HEAD_DIM + PAIR)
                xdt = (xs_ref[:, sl] * dte_ref[:, sl]).astype(BF16)
                yd, st, epb = [], [], []
                for h in (h0, h1):
                    dec, wb, eb = _head_decay(tri, p, pt, tot, h)
                    yd.append(_dot((cbm * dec).astype(BF16), xdt))
                    st.append(_dot_tn(xdt, (bg * wb).astype(BF16)))
                    epb.append(eb)
                hin = h_ref[sl, :]
                yo = _dot_nt(cb, hin.astype(BF16)) * jnp.where(lane, epb[0], epb[1])
                y_ref[:, sl] = jnp.where(lane, yd[0], yd[1]) + yo
                et = jnp.where(rowh, etot[:, h0:h0 + 1], etot[:, h1:h1 + 1])
                h_ref[sl, :] = hin * et + jnp.where(rowh, st[0], st[1])

    return pl.pallas_call(
        body, name="ssd_fwd", grid=(2, nc), in_specs=specs,
        out_specs=[pl.BlockSpec((None, CHUNK, D_INNER), lambda d, c: (d, cidx(d, c), 0)),
                   pl.BlockSpec((None, None, D_INNER, SSD_STATE), lambda d, c: (d, cidx(d, c), 0, 0))],
        out_shape=[jax.ShapeDtypeStruct((2, t, D_INNER), F32), jax.ShapeDtypeStruct((2, nc, D_INNER, SSD_STATE), F32)],
        scratch_shapes=[pltpu.VMEM((D_INNER, SSD_STATE), F32)],
        compiler_params=_cparams("arbitrary", "arbitrary"),
    )(act, act, act, dt2, dt2t, a_row, a_col, dte)


def _put_col(acc, col, h):
    lane = lax.broadcasted_iota(jnp.int32, acc.shape, 1)
    return jnp.where(lane == h, col, acc)


def _put_row(acc, row, h):
    sub = lax.broadcasted_iota(jnp.int32, acc.shape, 0)
    return jnp.where(sub == h, row, acc)


def _sum_all(x):
    return jnp.sum(jnp.sum(x, axis=0, keepdims=True), axis=1, keepdims=True)


def _ssd_bwd(act, dt2, dt2t, a_row, a_col, dte, dy, states):
    t = act.shape[0]
    nc = t // CHUNK
    cidx, specs = _ssd_specs(nc, rev=True)
    specs = specs + [
        pl.BlockSpec((CHUNK, D_INNER), lambda d, c: (cidx(d, c), 0)),
        pl.BlockSpec((None, None, D_INNER, SSD_STATE), lambda d, c: (d, cidx(d, c), 0, 0)),
    ]

    def body(xs_ref, bs_ref, cs_ref, dt_ref, dtt_ref, ar_ref, ac_ref, dte_ref, dy_ref, st_ref,
             dxs_ref, dbs_ref, dcs_ref, ddt_ref, da_ref, dh_ref):
        d, c = pl.program_id(0), pl.program_id(1)

        @pl.when(c == 0)
        def _():
            dh_ref[...] = jnp.zeros_like(dh_ref)
            da_ref[...] = jnp.zeros_like(da_ref)

        tri, trit, dt, p, pt, tot = _ssd_prelude(d, dt_ref, dtt_ref, ar_ref, ac_ref)
        etot = jnp.exp(tot)
        lane = lax.broadcasted_iota(jnp.int32, (CHUNK, PAIR), 1) < SSD_HEAD_DIM
        rowh = lax.broadcasted_iota(jnp.int32, (PAIR, SSD_STATE), 0) < SSD_HEAD_DIM
        first_head = lax.broadcasted_iota(jnp.int32, (PAIR, LANES), 0) < SSD_HEAD_DIM
        out_lane = lax.broadcasted_iota(jnp.int32, (PAIR, LANES), 1)
        ddtx = jnp.zeros((CHUNK, LANES), F32)
        lane32 = lax.broadcasted_iota(jnp.int32, (CHUNK, SSD_HEADS), 1)
        dp_col = jnp.zeros((CHUNK, SSD_HEADS), F32)
        dp_row = jnp.zeros((SSD_HEADS, CHUNK), F32)
        dtot = jnp.zeros((1, SSD_HEADS), F32)
        for g in range(SSD_GROUPS):
            gs = slice(g * SSD_STATE, (g + 1) * SSD_STATE)
            bg = bs_ref[:, gs]
            bb = bg.astype(BF16)
            cb = cs_ref[:, gs].astype(BF16)
            cbm = _dot_nt(cb, bb)
            dcb = jnp.zeros((CHUNK, CHUNK), F32)
            dc_acc = jnp.zeros((CHUNK, SSD_STATE), F32)
            db_acc = jnp.zeros((CHUNK, SSD_STATE), F32)
            for pr in range(2):
                h0 = g * 4 + pr * 2
                h1 = h0 + 1
                sl = slice(h0 * SSD_HEAD_DIM, h0 * SSD_HEAD_DIM + PAIR)
                xp = xs_ref[:, sl]
                dtp = dte_ref[:, sl]
                xdt_f = xp * dtp
                xdt = xdt_f.astype(BF16)
                dyp = dy_ref[:, sl]
                dyb = dyp.astype(BF16)
                hin = st_ref[sl, :]
                dh = dh_ref[sl, :]
                hb = hin.astype(BF16)
                dhb = dh.astype(BF16)
                heads = [_head_decay(tri, p, pt, tot, h) for h in (h0, h1)]
                dye = dyp * jnp.where(lane, heads[0][2], heads[1][2])
                dyeb = dye.astype(BF16)
                gy = _dot_nt(cb, hb) * dye
                dc_acc = dc_acc + _dot(dyeb, hb)
                dhin = _dot_tn(dyeb, cb)
                hh = dh * hin
                dxdt = jnp.zeros((CHUNK, PAIR), F32)
                for idx, h in enumerate((h0, h1)):
                    hm = lane if idx == 0 else jnp.logical_not(lane)
                    rm = rowh if idx == 0 else jnp.logical_not(rowh)
                    dec, wb, _ = heads[idx]
                    mf = cbm * dec
                    t1 = _dot_tn(mf.astype(BF16), dyb)
                    t2 = _dot_nt((bg * wb).astype(BF16), dhb)
                    dxdt = jnp.where(hm, t1 + t2, dxdt)
                    dm = _dot_nt(jnp.where(hm, dyp, 0.0).astype(BF16), xdt)
                    dcb = dcb + dm * dec
                    e = dm * mf
                    qw = _dot(jnp.where(hm, xdt_f, 0.0).astype(BF16), dhb) * wb
                    db_acc = db_acc + qw
                    qwb = qw * bg
                    col = jnp.sum(e + jnp.where(hm, gy, 0.0) - qwb, axis=1, keepdims=True)
                    dp_col = jnp.where(lane32 == h, col, dp_col)
                    dp_row = _put_row(dp_row, -jnp.sum(e, axis=0, keepdims=True), h)
                    dtot_h = _sum_all(qwb) + etot[:, h:h + 1] * _sum_all(jnp.where(rm, hh, 0.0))
                    dtot = _put_col(dtot, dtot_h, h)
                dxs_ref[:, sl] = dxdt * dtp
                ddx = dxdt * xp
                ddx_hi = ddx.astype(BF16)
                ddx_lo = (ddx - ddx_hi.astype(F32)).astype(BF16)
                route = (out_lane == jnp.where(first_head, h0, h1)).astype(BF16)
                ddtx = ddtx + _dot(ddx_hi, route) + _dot(ddx_lo, route)
                et = jnp.where(rowh, etot[:, h0:h0 + 1], etot[:, h1:h1 + 1])
                dh_ref[sl, :] = dh * et + dhin
            dcbb = dcb.astype(BF16)
            dcs_ref[:, gs] = _dot(dcbb, bb) + dc_acc
            dbs_ref[:, gs] = _dot_tn(dcbb, cb) + db_acc
        d_adt = _dot(trit, dp_col, HI) + _dot_nt(trit, dp_row, HI) + dtot
        ddt_ref[...] = ddtx[:, :SSD_HEADS] + ar_ref[...] * d_adt
        da_ref[...] += jnp.sum(dt * d_adt, axis=0, keepdims=True)

    return pl.pallas_call(
        body, name="ssd_bwd", grid=(2, nc), in_specs=specs,
        out_specs=[pl.BlockSpec((None, CHUNK, D_INNER), lambda d, c: (d, cidx(d, c), 0)),
                   pl.BlockSpec((None, CHUNK, 1024), lambda d, c: (d, cidx(d, c), 0)),
                   pl.BlockSpec((None, CHUNK, 1024), lambda d, c: (d, cidx(d, c), 0)),
                   pl.BlockSpec((None, CHUNK, SSD_HEADS), lambda d, c: (d, cidx(d, c), 0)),
                   pl.BlockSpec((None, 1, SSD_HEADS), lambda d, c: (d, 0, 0))],
        out_shape=[jax.ShapeDtypeStruct((2, t, D_INNER), F32), jax.ShapeDtypeStruct((2, t, 1024), F32),
                   jax.ShapeDtypeStruct((2, t, 1024), F32), jax.ShapeDtypeStruct((2, t, SSD_HEADS), F32),
                   jax.ShapeDtypeStruct((2, 1, SSD_HEADS), F32)],
        scratch_shapes=[pltpu.VMEM((D_INNER, SSD_STATE), F32)],
        compiler_params=_cparams("arbitrary", "arbitrary"),
    )(act, act, act, dt2, dt2t, a_row, a_col, dte, dy, states)


REP = ATTN_HEADS // ATTN_KV
SCALE = ATTN_DIM ** -0.5
GROUP_Q = REP * ATTN_DIM
K_BLK0 = D_MODEL // LANES
V_BLK0 = K_BLK0 + ATTN_KV


def _attn_specs(nb):
    q = pl.BlockSpec((BLOCK, GROUP_Q), lambda g, n: (n, g))

    def kv(blk0):
        return [pl.BlockSpec((BLOCK, LANES), lambda g, n: (jnp.maximum(n - 1, 0), blk0 + g)),
                pl.BlockSpec((BLOCK, LANES), lambda g, n: (n, blk0 + g)),
                pl.BlockSpec((BLOCK, LANES), lambda g, n: (jnp.minimum(n + 1, nb - 1), blk0 + g))]

    bias = pl.BlockSpec((None, REP, BLOCK, 3 * BLOCK), lambda g, n: (g, 0, 0, 0))
    sink = pl.BlockSpec((None, REP, 1, LANES), lambda g, n: (g, 0, 0, 0))
    return q, kv(K_BLK0), kv(V_BLK0), bias, sink


def _band_mask():
    ii = lax.broadcasted_iota(jnp.int32, (BLOCK, 3 * BLOCK), 0)
    jj = lax.broadcasted_iota(jnp.int32, (BLOCK, 3 * BLOCK), 1)
    return (jj >= ii) & (jj - 2 * BLOCK <= ii)


def _attn_valid(n, nb):
    jj = lax.broadcasted_iota(jnp.int32, (1, 3 * BLOCK), 1)
    return ((jj >= BLOCK) | (n > 0)) & ((jj < 2 * BLOCK) | (n < nb - 1))


def _attn_probs(q, kcat, bias, snk, valid):
    s = jnp.where(valid, _dot_nt(q, kcat) + bias, NEG)
    m = jnp.maximum(jnp.max(s, axis=1, keepdims=True), snk)
    p = jnp.exp(s - m)
    es = jnp.exp(snk - m)
    r = 1.0 / (jnp.sum(p, axis=1, keepdims=True) + es)
    return p * r, es * r


def _stack_heads(ref, lane):
    parts = []
    for pr in range(REP // 2):
        tile = ref[:, pr * LANES:(pr + 1) * LANES]
        parts += [jnp.where(lane, tile, 0.0), jnp.where(lane, 0.0, tile)]
    return jnp.concatenate(parts, axis=0)


def _unstack_heads(x, lane):
    return jnp.concatenate([jnp.where(lane, x[(2 * pr) * BLOCK:(2 * pr + 1) * BLOCK], x[(2 * pr + 1) * BLOCK:(2 * pr + 2) * BLOCK])
                            for pr in range(REP // 2)], axis=1)


def _stack_bias(b_ref, s_ref):
    bias = jnp.concatenate([b_ref[r] for r in range(REP)], axis=0)
    snk = jnp.concatenate([jnp.broadcast_to(s_ref[r][:, 0:1], (BLOCK, 1)) for r in range(REP)], axis=0)
    return bias, snk


def _attn_fwd(qkv, bias4, sink4):
    t = qkv.shape[0]
    nb = t // BLOCK
    qs, ks, vs, bs, ss = _attn_specs(nb)

    def body(q_ref, kp_ref, kc_ref, kn_ref, vp_ref, vc_ref, vn_ref, b_ref, s_ref, o_ref):
        n = pl.program_id(1)
        kcat = jnp.concatenate([kp_ref[...], kc_ref[...], kn_ref[...]], axis=0)
        vcat = jnp.concatenate([vp_ref[...], vc_ref[...], vn_ref[...]], axis=0)
        valid = _attn_valid(n, nb)
        lane = lax.broadcasted_iota(jnp.int32, (BLOCK, LANES), 1) < ATTN_DIM
        bias, snk = _stack_bias(b_ref, s_ref)
        pn, _ = _attn_probs(_stack_heads(q_ref, lane) * SCALE, kcat, bias, snk, valid)
        o_ref[...] = _unstack_heads(_dot(pn.astype(BF16), vcat), lane).astype(o_ref.dtype)

    return pl.pallas_call(
        body, name="attn_fwd", grid=(ATTN_KV, nb), in_specs=[qs] + ks + vs + [bs, ss],
        out_specs=qs, out_shape=jax.ShapeDtypeStruct((t, D_MODEL), BF16),
        compiler_params=_cparams("parallel", "arbitrary"),
    )(qkv, qkv, qkv, qkv, qkv, qkv, qkv, bias4, sink4)


def _attn_bwd(qkv, bias4, sink4, do):
    t = qkv.shape[0]
    nb = t // BLOCK
    qs, ks, vs, bs, ss = _attn_specs(nb)
    part = pl.BlockSpec((3, BLOCK, LANES), lambda g, n: (0, n, g))

    def body(q_ref, kp_ref, kc_ref, kn_ref, vp_ref, vc_ref, vn_ref, b_ref, s_ref, do_ref,
             dq_ref, dk_ref, dv_ref, db_ref, ds_ref):
        n = pl.program_id(1)

        @pl.when(n == 0)
        def _():
            db_ref[...] = jnp.zeros_like(db_ref)
            ds_ref[...] = jnp.zeros_like(ds_ref)

        kcat = jnp.concatenate([kp_ref[...], kc_ref[...], kn_ref[...]], axis=0)
        vcat = jnp.concatenate([vp_ref[...], vc_ref[...], vn_ref[...]], axis=0)
        valid = _attn_valid(n, nb)
        lane = lax.broadcasted_iota(jnp.int32, (BLOCK, LANES), 1) < ATTN_DIM
        bias, snk = _stack_bias(b_ref, s_ref)
        q = _stack_heads(q_ref, lane)
        do = _stack_heads(do_ref, lane)
        pn, psink = _attn_probs(q * SCALE, kcat, bias, snk, valid)
        dp = _dot_nt(do, vcat)
        delta = jnp.sum(pn * dp, axis=1, keepdims=True)
        dsc = pn * (dp - delta)
        dsink = psink * delta
        for r in range(REP):
            rows = slice(r * BLOCK, (r + 1) * BLOCK)
            db_ref[r] += dsc[rows]
            ds_ref[r] += jnp.broadcast_to(-jnp.sum(dsink[rows], axis=0, keepdims=True), (1, LANES))
        dsb = (dsc * SCALE).astype(BF16)
        dq_ref[...] = _unstack_heads(_dot(dsb, kcat), lane).astype(dq_ref.dtype)
        dk = _dot_tn(dsb, q)
        dv = _dot_tn(pn.astype(BF16), do)
        for j in range(3):
            dk_ref[j] = dk[j * BLOCK:(j + 1) * BLOCK]
            dv_ref[j] = dv[j * BLOCK:(j + 1) * BLOCK]

    kv_cols = ATTN_KV * LANES
    return pl.pallas_call(
        body, name="attn_bwd", grid=(ATTN_KV, nb), in_specs=[qs] + ks + vs + [bs, ss, qs],
        out_specs=[qs, part, part, bs, ss],
        out_shape=[jax.ShapeDtypeStruct((t, D_MODEL), BF16), jax.ShapeDtypeStruct((3, t, kv_cols), F32),
                   jax.ShapeDtypeStruct((3, t, kv_cols), F32), jax.ShapeDtypeStruct(bias4.shape, F32),
                   jax.ShapeDtypeStruct(sink4.shape, F32)],
        compiler_params=_cparams("parallel", "arbitrary"),
    )(qkv, qkv, qkv, qkv, qkv, qkv, qkv, bias4, sink4, do)


def _kv_combine(name, parts):
    _, t, cols = parts.shape

    def body(p_ref, o_ref):
        z = jnp.zeros((BLOCK, LANES), F32)
        from_next = jnp.concatenate([p_ref[0, BLOCK:, :], z], axis=0)
        from_prev = jnp.concatenate([z, p_ref[2, :t - BLOCK, :]], axis=0)
        o_ref[...] = (from_next + p_ref[1] + from_prev).astype(o_ref.dtype)

    return pl.pallas_call(
        body, name=name, grid=(cols // LANES,),
        in_specs=[pl.BlockSpec((3, t, LANES), lambda g: (0, 0, g))],
        out_specs=pl.BlockSpec((t, LANES), lambda g: (0, g)),
        out_shape=jax.ShapeDtypeStruct((t, cols), BF16),
        compiler_params=_cparams("parallel"),
    )(parts)


def _t5_bucket(rel):
    nb = N_BUCKETS // 2
    max_exact = nb // 2
    ret = jnp.where(rel > 0, nb, 0)
    n = jnp.abs(rel)
    nf = jnp.maximum(n, 1).astype(jnp.float32)
    large = max_exact + (jnp.log(nf / max_exact) / math.log(MAX_DISTANCE / max_exact) * (nb - max_exact)).astype(jnp.int32)
    large = jnp.minimum(large, nb - 1)
    return ret + jnp.where(n < max_exact, n, large)


def _bucket_map():
    i = jnp.arange(BLOCK)[:, None]
    j = jnp.arange(3 * BLOCK)[None, :]
    return _t5_bucket(j - BLOCK - i)


def _bias_from_table(table, onehot_t):
    def body(t_ref, o_ref, out_ref):
        out_ref[...] = _dot(t_ref[...], o_ref[...], HI)

    return pl.pallas_call(body, name="bias_from_table", out_shape=jax.ShapeDtypeStruct((ATTN_HEADS, onehot_t.shape[1]), F32),
                          compiler_params=pltpu.CompilerParams(vmem_limit_bytes=VMEM_LIMIT_BYTES))(table.T, onehot_t)


def _bias_table_grad(dbias, onehot_t):
    def body(d_ref, o_ref, out_ref):
        out_ref[...] = _dot_nt(d_ref[...], o_ref[...], HI)

    return pl.pallas_call(body, name="bias_table_grad", out_shape=jax.ShapeDtypeStruct((ATTN_HEADS, N_BUCKETS), F32),
                          compiler_params=pltpu.CompilerParams(vmem_limit_bytes=VMEM_LIMIT_BYTES))(dbias, onehot_t)


HBM_SPEC = pl.BlockSpec(memory_space=pl.ANY)


def _comm_call(name, body, xs, out_shapes, n_sems):
    n = len(xs)
    return pl.pallas_call(
        body, name=name, in_specs=[HBM_SPEC] * n, out_specs=[HBM_SPEC] * n, out_shape=out_shapes,
        scratch_shapes=[pltpu.SemaphoreType.DMA((n * n_sems,)), pltpu.SemaphoreType.DMA((n * n_sems,)),
                        pltpu.SemaphoreType.DMA((n,))],
    )(*xs)


def _allgather_chips(name, xs):
    n = len(xs)

    def body(*refs):
        x_refs, out_refs, (send_sems, recv_sems, local_sems) = refs[:n], refs[n:2 * n], refs[2 * n:]
        mx, my, mc = lax.axis_index("x"), lax.axis_index("y"), lax.axis_index("c")
        me = 2 * mx + my
        chips = [(1 - mx, my), (mx, 1 - my), (1 - mx, 1 - my)]
        sibling = (mx, my, 1 - mc)

        def part(i, slot, h):
            r2 = xs[i].shape[0] // 2
            return out_refs[i].at[slot, pl.ds(h * r2, r2)]

        def copy(i, k, src, dst, to):
            return pltpu.make_async_remote_copy(src_ref=src, dst_ref=dst, send_sem=send_sems.at[6 * i + k],
                                                recv_sem=recv_sems.at[6 * i + k], device_id=to, device_id_type=MESH)

        mine = [pltpu.make_async_copy(x_refs[i], out_refs[i].at[me], local_sems.at[i]) for i in range(n)]
        first = [copy(i, k, x_refs[i].at[pl.ds(mc * (xs[i].shape[0] // 2), xs[i].shape[0] // 2)], part(i, me, mc), (px, py, mc))
                 for i in range(n) for k, (px, py) in enumerate(chips)]
        for cp in mine + first:
            cp.start()
        passed = []
        for k, (px, py) in enumerate(chips):
            for i in range(n):
                landed = part(i, 2 * px + py, mc)
                copy(i, k, landed, landed, (px, py, mc)).wait_recv()
                passed.append(copy(i, 3 + k, landed, landed, sibling))
                passed[-1].start()
        for k, (px, py) in enumerate(chips):
            for i in range(n):
                theirs = part(i, 2 * px + py, 1 - mc)
                copy(i, 3 + k, theirs, theirs, sibling).wait_recv()
        for cp in first + passed:
            cp.wait_send()
        for cp in mine:
            cp.wait()

    return _comm_call(name, body, xs, [jax.ShapeDtypeStruct((4,) + x.shape, x.dtype) for x in xs], 6)


SEM_SPEC = pl.BlockSpec(memory_space=pltpu.SEMAPHORE)
DATAFLOW = pltpu.SideEffectType.DATAFLOW_SIDE_EFFECTING


def _send_start(name, xs, after, scatter, collective_id):
    n = len(xs)

    def body(*refs):
        x_refs, land_refs = refs[:n], refs[n:2 * n]
        send_sems, recv_sems, token = refs[2 * n + 1], refs[2 * n + 2], refs[-1]
        mx, my, mc = lax.axis_index("x"), lax.axis_index("y"), lax.axis_index("c")
        chips = [(1 - mx, my), (mx, 1 - my), (1 - mx, 1 - my)]
        barrier = pltpu.get_barrier_semaphore()
        for px, py in chips:
            pl.semaphore_signal(barrier, inc=1, device_id=(px, py, mc), device_id_type=MESH)
        pl.semaphore_wait(barrier, len(chips))
        for i in range(n):
            for k, (px, py) in enumerate(chips):
                src = x_refs[i].at[2 * px + py] if scatter else x_refs[i]
                pltpu.make_async_remote_copy(src_ref=src, dst_ref=land_refs[i].at[2 * mx + my], send_sem=send_sems.at[3 * i + k],
                                             recv_sem=recv_sems.at[3 * i + k], device_id=(px, py, mc), device_id_type=MESH).start()
        token[...] = jnp.zeros_like(token)

    lands = [lax.empty((4,) + (x.shape[1:] if scatter else x.shape), x.dtype) for x in xs]
    hbm = [pltpu.HBM(a.shape, a.dtype) for a in list(xs) + lands]
    out = pl.pallas_call(
        body, name=name,
        out_shape=(pltpu.SemaphoreType.DMA((3 * n,)), pltpu.SemaphoreType.DMA((3 * n,)), *hbm, jax.ShapeDtypeStruct((SUBLANES, LANES), F32)),
        in_specs=(HBM_SPEC,) * (2 * n + 1),
        out_specs=(SEM_SPEC, SEM_SPEC) + (HBM_SPEC,) * (2 * n) + (pl.BlockSpec(memory_space=pltpu.VMEM),),
        input_output_aliases={i: 2 + i for i in range(2 * n)},
        compiler_params=pltpu.CompilerParams(has_side_effects=DATAFLOW, collective_id=collective_id),
    )(*[pltpu.with_memory_space_constraint(a, pltpu.HBM) for a in list(xs) + lands], after)
    return out[0], out[1], list(out[2:2 + n]), list(out[2 + n:2 + 2 * n]), out[-1]


def _send_wait(name, send_sems, recv_sems, x_thrus, land_thrus, after, scatter):
    n = len(x_thrus)

    def body(*refs):
        x_refs, land_refs, send_sems, recv_sems = refs[:n], refs[n:2 * n], refs[2 * n], refs[2 * n + 1]
        mx, my, mc = lax.axis_index("x"), lax.axis_index("y"), lax.axis_index("c")
        chips = [(1 - mx, my), (mx, 1 - my), (1 - mx, 1 - my)]
        for i in range(n):
            for k, (px, py) in enumerate(chips):
                src = x_refs[i].at[0] if scatter else x_refs[i]
                copy = pltpu.make_async_remote_copy(src_ref=src, dst_ref=land_refs[i].at[2 * px + py], send_sem=send_sems.at[3 * i + k],
                                                    recv_sem=recv_sems.at[3 * i + k], device_id=(px, py, mc), device_id_type=MESH)
                copy.wait_send()
                copy.wait_recv()

    arrs = list(x_thrus) + list(land_thrus)
    out = pl.pallas_call(
        body, name=name, out_shape=tuple(pltpu.HBM(a.shape, a.dtype) for a in arrs),
        in_specs=(HBM_SPEC,) * (2 * n) + (SEM_SPEC, SEM_SPEC, HBM_SPEC), out_specs=(HBM_SPEC,) * (2 * n),
        input_output_aliases={i: i for i in range(2 * n)},
        compiler_params=pltpu.CompilerParams(has_side_effects=DATAFLOW),
    )(*arrs, send_sems, recv_sems, after)
    return list(out[n:])


def _swap_cores(name, xs):
    n = len(xs)

    def body(*refs):
        x_refs, out_refs, (send_sems, recv_sems, _) = refs[:n], refs[n:2 * n], refs[2 * n:]
        mx, my, mc = lax.axis_index("x"), lax.axis_index("y"), lax.axis_index("c")
        sends = [pltpu.make_async_remote_copy(src_ref=x_refs[i].at[1 - mc], dst_ref=out_refs[i], send_sem=send_sems.at[i],
                                              recv_sem=recv_sems.at[i], device_id=(mx, my, 1 - mc), device_id_type=MESH)
                 for i in range(n)]
        for cp in sends:
            cp.start()
        for cp in sends:
            cp.wait()

    return _comm_call(name, body, xs, [jax.ShapeDtypeStruct(x.shape[1:], x.dtype) for x in xs], 1)


def _scatter_chips(name, gs):
    n = len(gs)

    def body(*refs):
        g_refs, out_refs, (send_sems, recv_sems, local_sems) = refs[:n], refs[n:2 * n], refs[2 * n:]
        mx, my, mc = lax.axis_index("x"), lax.axis_index("y"), lax.axis_index("c")
        me = 2 * mx + my
        chips = [(1 - mx, my), (mx, 1 - my), (1 - mx, 1 - my)]

        def copy(i, k, src, dst, to):
            return pltpu.make_async_remote_copy(src_ref=src, dst_ref=dst, send_sem=send_sems.at[3 * i + k],
                                                recv_sem=recv_sems.at[3 * i + k], device_id=to, device_id_type=MESH)

        mine = [pltpu.make_async_copy(g_refs[i].at[me], out_refs[i].at[me], local_sems.at[i]) for i in range(n)]
        sends = [copy(i, k, g_refs[i].at[2 * px + py], out_refs[i].at[me], (px, py, mc))
                 for i in range(n) for k, (px, py) in enumerate(chips)]
        for cp in mine + sends:
            cp.start()
        for i in range(n):
            for k, (px, py) in enumerate(chips):
                copy(i, k, g_refs[i].at[0], out_refs[i].at[2 * px + py], (px, py, mc)).wait_recv()
        for cp in sends:
            cp.wait_send()
        for cp in mine:
            cp.wait()

    return _comm_call(name, body, gs, [jax.ShapeDtypeStruct(g.shape, g.dtype) for g in gs], 3)


def _allgather_all(name, x):
    def body(x_ref, out_ref, send_sems, recv_sems, local_sem):
        mx, my, mc = lax.axis_index("x"), lax.axis_index("y"), lax.axis_index("c")
        me = 4 * mx + 2 * my + mc
        flips = [(fx, fy, fc) for fx in (0, 1) for fy in (0, 1) for fc in (0, 1)][1:]
        peers = [(mx ^ fx, my ^ fy, mc ^ fc) for fx, fy, fc in flips]
        mine = pltpu.make_async_copy(x_ref, out_ref.at[me], local_sem)
        mine.start()
        sends = [pltpu.make_async_remote_copy(src_ref=x_ref, dst_ref=out_ref.at[me], send_sem=send_sems.at[k],
                                              recv_sem=recv_sems.at[k], device_id=peer, device_id_type=MESH)
                 for k, peer in enumerate(peers)]
        for cp in sends:
            cp.start()
        for k, (px, py, pc) in enumerate(peers):
            pltpu.make_async_remote_copy(src_ref=x_ref, dst_ref=out_ref.at[4 * px + 2 * py + pc], send_sem=send_sems.at[k],
                                         recv_sem=recv_sems.at[k], device_id=(px, py, pc), device_id_type=MESH).wait_recv()
        for cp in sends:
            cp.wait_send()
        mine.wait()

    return pl.pallas_call(
        body, name=name, in_specs=[HBM_SPEC], out_specs=HBM_SPEC,
        out_shape=jax.ShapeDtypeStruct((8,) + x.shape, x.dtype),
        scratch_shapes=[pltpu.SemaphoreType.DMA((7,)), pltpu.SemaphoreType.DMA((7,)), pltpu.SemaphoreType.DMA],
    )(x)


def _allgather_cores(name, xs):
    n = len(xs)

    def body(*refs):
        x_refs, out_refs, (send_sems, recv_sems, local_sems) = refs[:n], refs[n:2 * n], refs[2 * n:]
        mx, my, mc = lax.axis_index("x"), lax.axis_index("y"), lax.axis_index("c")

        def copy(i, slot):
            return pltpu.make_async_remote_copy(src_ref=x_refs[i], dst_ref=out_refs[i].at[slot], send_sem=send_sems.at[i],
                                                recv_sem=recv_sems.at[i], device_id=(mx, my, 1 - mc), device_id_type=MESH)

        mine = [pltpu.make_async_copy(x_refs[i], out_refs[i].at[mc], local_sems.at[i]) for i in range(n)]
        sends = [copy(i, mc) for i in range(n)]
        for cp in mine + sends:
            cp.start()
        for i in range(n):
            copy(i, 1 - mc).wait_recv()
        for cp in sends:
            cp.wait_send()
        for cp in mine:
            cp.wait()

    return _comm_call(name, body, xs, [jax.ShapeDtypeStruct((2,) + x.shape, x.dtype) for x in xs], 1)


def _sum_slots(name, st, tb=256):
    n, r, c = st.shape
    tb = _pick(r, (tb, 128, 32))

    def body(s_ref, o_ref):
        acc = s_ref[0].astype(F32)
        for k in range(1, n):
            acc = acc + s_ref[k].astype(F32)
        o_ref[...] = acc

    return pl.pallas_call(
        body, name=name, grid=(r // tb,), in_specs=[pl.BlockSpec((n, tb, c), lambda i: (0, i, 0))],
        out_specs=pl.BlockSpec((tb, c), lambda i: (i, 0)), out_shape=jax.ShapeDtypeStruct((r, c), F32),
        compiler_params=_cparams("parallel"),
    )(st)


def _adamw(name, w, g, m, v):
    def fn(w, g, m, v):
        m2 = ADAM_B1 * m + (1.0 - ADAM_B1) * g
        v2 = ADAM_B2 * v + (1.0 - ADAM_B2) * jnp.square(g)
        m_hat = m2 / (1.0 - ADAM_B1 ** ADAM_STEP)
        v_hat = v2 / (1.0 - ADAM_B2 ** ADAM_STEP)
        delta = -ADAM_LR * (m_hat / (jnp.sqrt(v_hat) + ADAM_EPS) + ADAM_WD * w)
        return [delta, m2, v2], []

    tb = _pick(w.shape[0], (256, 32))
    return _rowwise(name, fn, [_row(w), _row(g), _row(m), _row(v)], [], [(w.shape[1], F32)] * 3, tb=tb)


BIG = ("w_ssd_out", "w_attn_out", "w_o", "w_mlp_in", "w_mlp_out", "conv_w")
SMALL = ("pre_mix_norm", "b_gate", "conv_b", "dt_bias", "a_log", "d_skip", "ssd_norm", "attn_sink", "rel_bias_table",
         "post_mix_norm", "pre_mlp_norm", "post_mlp_norm")
ALL_W = ("pre_mix_norm", "w_in", "b_gate", "conv_w", "conv_b", "dt_bias", "a_log", "d_skip", "ssd_norm", "w_ssd_out",
         "attn_sink", "rel_bias_table", "w_attn_out", "w_o", "post_mix_norm", "pre_mlp_norm", "w_mlp_in", "w_mlp_out",
         "post_mlp_norm")


def _pack_rows(parts, rows, dtype):
    flat = jnp.concatenate([p.reshape(-1, D_MODEL).astype(dtype) for p in parts], axis=0)
    return jnp.pad(flat, ((0, rows - flat.shape[0]), (0, 0)))


def _pack_big(shards, dtype):
    return _pack_rows([shards[n] for n in BIG], BIG_ROWS, dtype)


def _unpack_big(flat, like):
    out, r = {}, 0
    for n in BIG:
        shp = like[n].shape
        nr = math.prod(shp) // D_MODEL
        out[n] = flat[r:r + nr].reshape(shp)
        r += nr
    return out


def _pack_small(parts, extra=None):
    flat = jnp.concatenate([parts[n].reshape(-1).astype(F32) for n in SMALL] + ([extra.reshape(-1)] if extra is not None else []))
    return jnp.pad(flat, (0, SMALL_ROWS * D_MODEL - flat.shape[0])).reshape(SMALL_ROWS, D_MODEL)


def _unpack_small(flat2, like):
    flat = flat2.reshape(-1)
    out, r = {}, 0
    for n in SMALL:
        shp = like[n].shape
        k = math.prod(shp)
        out[n] = flat[r:r + k].reshape(shp)
        r += k
    return out, flat[r]


def _shard_of_full(name, full, s):
    if name in ("w_in", "w_mlp_in"):
        w = full.shape[2] // 4
        return full[:, :, s * w:(s + 1) * w]
    if name == "conv_w":
        w = full.shape[3] // 4
        return full[:, :, :, s * w:(s + 1) * w]
    w = full.shape[1] // 4
    return full[:, s * w:(s + 1) * w, :]


def _full_of_shards(name, shards):
    axis = {"w_in": 2, "w_mlp_in": 2, "conv_w": 3}.get(name, 1)
    return jnp.concatenate(shards, axis=axis)


def _to_proj_layout(w):
    z, xbc, dt, q, k, v, gates = (w[..., 0:2048], w[..., 2048:6144], w[..., 6144:6208], w[..., 6208:7232],
                                  w[..., 7232:7488], w[..., 7488:7744], w[..., 7744:9792])
    pad = jnp.zeros(w.shape[:-1] + (N_MAIN - OFF_DT - dt.shape[-1],), w.dtype)

    def doubled(a):
        h = a.reshape(a.shape[:-1] + (ATTN_KV, 1, ATTN_DIM))
        return jnp.broadcast_to(h, a.shape[:-1] + (ATTN_KV, 2, ATTN_DIM)).reshape(a.shape[:-1] + (2 * a.shape[-1],))

    return jnp.concatenate([z, gates, xbc, dt, pad, q, doubled(k), doubled(v)], axis=-1)


def _from_proj_layout(w):
    z, gates, xbc, dt, q, k2, v2 = (w[..., 0:2048], w[..., 2048:4096], w[..., 4096:8192], w[..., 8192:8256],
                                    w[..., 8320:9344], w[..., 9344:9856], w[..., 9856:10368])

    def folded(a):
        return a.reshape(a.shape[:-1] + (ATTN_KV, 2, ATTN_DIM)).sum(axis=-2).reshape(a.shape[:-1] + (a.shape[-1] // 2,))

    return jnp.concatenate([z, xbc, dt, q, folded(k2), folded(v2), gates], axis=-1)


def _layer_fwd(h1, x, W, P, l, bias4):
    t = x.shape[0]
    S = {"x": x, "h1": h1}
    proj = _matmul("proj", h1, W["w_in_main"][l], "nn")
    qkv = _matmul("proj_qkv", h1, W["w_in_qkv"][l], "nn", out_dtype=BF16)
    S["proj"] = proj
    pre, act = _conv_fwd(proj, W["conv_w"][l], P["conv_b"][l].reshape(1, CONV_DIM))
    S["pre"], S["act"] = pre, act

    dtb = jnp.pad(P["dt_bias"][l].reshape(1, 2 * SSD_HEADS), ((0, 0), (0, LANES - 2 * SSD_HEADS)))

    def dt_fn(raw, b):
        v = raw + b
        dt = jnp.maximum(v, 0.0) + jnp.log1p(jnp.exp(-jnp.abs(v)))
        expand = (jnp.right_shift(lax.broadcasted_iota(jnp.int32, (LANES, 2 * D_INNER), 1), 6)
                  == lax.broadcasted_iota(jnp.int32, (LANES, 2 * D_INNER), 0)).astype(BF16)
        hi = dt.astype(BF16)
        rest = dt - hi.astype(F32)
        mid = rest.astype(BF16)
        lo = (rest - mid.astype(F32)).astype(BF16)
        return [dt, _dot(hi, expand) + _dot(mid, expand) + _dot(lo, expand)], []

    dt, dte = _rowwise("dt_fwd", dt_fn, [_row(proj, LANES, OFF_DT // LANES)], [dtb], [(LANES, F32), (2 * D_INNER, F32)], tb=512)
    dt2 = jnp.stack([dt[:, 0:SSD_HEADS], dt[:, SSD_HEADS:2 * SSD_HEADS]])
    dt2t = dt2.transpose(0, 2, 1)
    a = -jnp.exp(P["a_log"][l])
    a_row, a_col = a.reshape(2, 1, SSD_HEADS), a.reshape(2, SSD_HEADS, 1)
    S["dt2"], S["dt2t"], S["a_row"], S["a_col"], S["dte"] = dt2, dt2t, a_row, a_col, dte
    y2, states = _ssd_fwd(act, dt2, dt2t, a_row, a_col, dte)
    S["states"] = states

    dsk = jnp.repeat(P["d_skip"][l], SSD_HEAD_DIM).reshape(1, D_INNER)
    nw = P["ssd_norm"][l].reshape(1, D_INNER)
    S["dsk"], S["nw"] = dsk, nw

    def gn_fn(yf, yb, xs, z, dsk, nw):
        y = yf + yb + xs * dsk
        return [y, _gated_norm_fwd(y, z, nw)], []

    y, yn = _rowwise("gated_norm_fwd", gn_fn,
                     [_row(y2, lead=0), _row(y2, lead=1), _row(act, D_INNER, 0), _row(proj, D_INNER, OFF_Z // D_INNER)],
                     [dsk, nw], [(D_INNER, F32), (D_INNER, BF16)])
    S["y"], S["yn"] = y, yn
    if "late" in W:
        W.update(W.pop("late")(yn))
    y_ssd = _matmul("ssd_out", yn, W["w_ssd_out"][l], "nn")
    S["y_ssd"] = y_ssd

    sink4 = jnp.broadcast_to(P["attn_sink"][l].reshape(ATTN_KV, REP, 1, 1), (ATTN_KV, REP, 1, LANES))
    S["qkv"], S["sink4"] = qkv, sink4
    o = _attn_fwd(qkv, bias4, sink4)
    S["o"] = o
    y_attn = _matmul("attn_out", o, W["w_attn_out"][l], "nn")
    S["y_attn"] = y_attn

    bg = P["b_gate"][l].reshape(1, 2 * D_MODEL)
    S["bg"] = bg

    def merge_fn(gates, ys, ya, b):
        g = jax.nn.sigmoid(gates + b)
        return [g[:, :D_MODEL] * ys + g[:, D_MODEL:] * ya], []

    (mix_in,) = _rowwise("merge_fwd", merge_fn, [_row(proj, 2 * D_MODEL, OFF_G // (2 * D_MODEL)), _row(y_ssd), _row(y_attn)],
                         [bg], [(D_MODEL, BF16)])
    S["mix_in"] = mix_in
    mixed = _matmul("w_o", mix_in, W["w_o"][l], "nn")
    S["mixed"] = mixed

    g_pm = P["post_mix_norm"][l].reshape(1, D_MODEL)
    g_pl = P["pre_mlp_norm"][l].reshape(1, D_MODEL)

    def postmix_fn(x, mixed, g1, g2):
        x2 = x + _rms_fwd(mixed, g1)
        return [x2, _rms_fwd(x2, g2)], []

    x2, h2 = _rowwise("post_mix_fwd", postmix_fn, [_row(x), _row(mixed)], [g_pm, g_pl], [(D_MODEL, F32), (D_MODEL, BF16)])
    S["x2"], S["h2"] = x2, h2
    a1 = _matmul("mlp_in", h2, W["w_mlp_in"][l], "nn", out_dtype=BF16,
                 epilogue=lambda acc: jnp.square(jnp.maximum(acc, 0.0)))
    S["a1"] = a1
    f2 = _matmul("mlp_out", a1, W["w_mlp_out"][l], "nn")
    S["f2"] = f2
    return S


def _layer_bwd(S, dx3, W, P, l, bias4, onehot_t, zero=None):
    t = dx3.shape[0]
    G = {}
    g_pmlp = P["post_mlp_norm"][l].reshape(1, D_MODEL)
    if zero is not None:
        g_pmlp = g_pmlp + zero

    def b1_fn(f2, dx3, g):
        df2, dg = _rms_bwd(f2, g, dx3)
        return [df2], [dg]

    df2, G["post_mlp_norm"] = _rowwise("post_mlp_bwd", b1_fn, [_row(S["f2"]), _row(dx3)], [g_pmlp], [(D_MODEL, BF16)], [(1, D_MODEL)])
    df1 = _matmul("d_f1", df2, W["w_mlp_out"][l], "nt", out_dtype=BF16,
                  epilogue=lambda acc, a1: acc * (2.0 * jnp.sqrt(a1.astype(F32))), extras=[S["a1"]])
    G["w_mlp_out"] = _matmul("dw_mlp_out", S["a1"], df2, "tn")
    dh2 = _matmul("d_h2", df1, W["w_mlp_in"][l], "nt")
    G["w_mlp_in"] = _matmul("dw_mlp_in", S["h2"], df1, "tn")

    g_pm = P["post_mix_norm"][l].reshape(1, D_MODEL)
    g_pl = P["pre_mlp_norm"][l].reshape(1, D_MODEL)

    def b3_fn(x2, dh2, dx3, mixed, g_pl, g_pm):
        d1, dgl = _rms_bwd(x2, g_pl, dh2)
        dx2 = dx3 + d1
        dmixed, dgm = _rms_bwd(mixed, g_pm, dx2)
        return [dx2, dmixed], [dgl, dgm]

    dx2, dmixed, G["pre_mlp_norm"], G["post_mix_norm"] = _rowwise(
        "post_mix_bwd", b3_fn, [_row(S["x2"]), _row(dh2), _row(dx3), _row(S["mixed"])], [g_pl, g_pm],
        [(D_MODEL, F32), (D_MODEL, BF16)], [(1, D_MODEL), (1, D_MODEL)])
    dmix_in = _matmul("d_mix_in", dmixed, W["w_o"][l], "nt")
    G["w_o"] = _matmul("dw_o", S["mix_in"], dmixed, "tn")

    proj = S["proj"]

    def b4_fn(gates, ys, ya, dmix, b):
        g = jax.nn.sigmoid(gates + b)
        gs, ga = g[:, :D_MODEL], g[:, D_MODEL:]
        dg = jnp.concatenate([ys * dmix, ya * dmix], axis=-1) * g * (1.0 - g)
        return [gs * dmix, ga * dmix, dg], [jnp.sum(dg, axis=0, keepdims=True)]

    dy_ssd, dy_attn, dgates, G["b_gate"] = _rowwise(
        "merge_bwd", b4_fn, [_row(proj, 2 * D_MODEL, OFF_G // (2 * D_MODEL)), _row(S["y_ssd"]), _row(S["y_attn"]), _row(dmix_in)],
        [S["bg"]], [(D_MODEL, BF16), (D_MODEL, BF16), (2 * D_MODEL, BF16)], [(1, 2 * D_MODEL)])

    dyn = _matmul("d_yn", dy_ssd, W["w_ssd_out"][l], "nt")
    G["w_ssd_out"] = _matmul("dw_ssd_out", S["yn"], dy_ssd, "tn")
    do = _matmul("d_o", dy_attn, W["w_attn_out"][l], "nt", out_dtype=BF16)
    G["w_attn_out"] = _matmul("dw_attn_out", S["o"], dy_attn, "tn")

    dq, dkp, dvp, dbias4, dsink4 = _attn_bwd(S["qkv"], bias4, S["sink4"], do)
    dk = _kv_combine("dk_combine", dkp)
    dv = _kv_combine("dv_combine", dvp)
    G["attn_sink"] = dsink4[:, :, 0, 0].reshape(ATTN_HEADS)
    G["rel_bias_table"] = _bias_table_grad(dbias4.reshape(ATTN_HEADS, BLOCK * 3 * BLOCK), onehot_t).T

    act = S["act"]

    def b5_fn(y, z, xs, dyn, nw, dsk):
        dy, dz, dnw = _gated_norm_bwd(y, z, nw, dyn)
        return [dy, dz], [dnw, jnp.sum(dy * xs, axis=0, keepdims=True)]

    dy, dz, G["ssd_norm"], dskip_cols = _rowwise(
        "gated_norm_bwd", b5_fn, [_row(S["y"]), _row(proj, D_INNER, OFF_Z // D_INNER), _row(act, D_INNER, 0), _row(dyn)],
        [S["nw"], S["dsk"]], [(D_INNER, F32), (D_INNER, BF16)], [(1, D_INNER), (1, D_INNER)])
    G["d_skip"] = dskip_cols.reshape(SSD_HEADS, SSD_HEAD_DIM).sum(axis=-1)

    dxs2, dbs2, dcs2, ddt2, da2 = _ssd_bwd(act, S["dt2"], S["dt2t"], S["a_row"], S["a_col"], S["dte"], dy, S["states"])
    G["a_log"] = da2.reshape(2, SSD_HEADS) * S["a_row"].reshape(2, SSD_HEADS)

    def b6_fn(dxf, dxb, dy, dbf, dbb, dcf, dcb, pre, dsk):
        dact = jnp.concatenate([dxf + dxb + dy * dsk, dbf + dbb, dcf + dcb], axis=-1)
        return [dact * _silu_grad(pre)], []

    (dpre,) = _rowwise("silu_bwd", b6_fn,
                       [_row(dxs2, lead=0), _row(dxs2, lead=1), _row(dy), _row(dbs2, lead=0), _row(dbs2, lead=1),
                        _row(dcs2, lead=0), _row(dcs2, lead=1), _row(S["pre"])], [S["dsk"]], [(CONV_DIM, F32)], tb=128)
    du, dconv_w, dconv_b = _conv_bwd(dpre, proj, W["conv_w"][l])
    G["conv_w"] = dconv_w.reshape(SSD_CONV, 1, CONV_DIM)
    G["conv_b"] = dconv_b.reshape(CONV_DIM)

    dtb = jnp.pad(P["dt_bias"][l].reshape(1, 2 * SSD_HEADS), ((0, 0), (0, LANES - 2 * SSD_HEADS)))
    ddt = jnp.pad(jnp.concatenate([ddt2[0], ddt2[1]], axis=-1), ((0, 0), (0, LANES - 2 * SSD_HEADS)))

    def b7_fn(raw, ddt, b):
        draw = ddt * jax.nn.sigmoid(raw + b)
        return [draw], [jnp.sum(draw, axis=0, keepdims=True)]

    draw, ddtb = _rowwise("dt_bwd", b7_fn, [_row(proj, LANES, OFF_DT // LANES), _row(ddt)], [dtb], [(LANES, BF16)], [(1, LANES)], tb=512)
    G["dt_bias"] = ddtb[0, :2 * SSD_HEADS].reshape(2, SSD_HEADS)

    dproj = jnp.concatenate([dz, dgates, du, draw, dq, dk, dv], axis=-1)
    G["w_in"] = _from_proj_layout(_matmul("dw_in", S["h1"], dproj, "tn"))
    dh1 = _matmul("d_h1", dproj, W["w_in"][l], "nt")

    g_pre = P["pre_mix_norm"][l].reshape(1, D_MODEL)

    def b8_fn(x, dh1, dx2, g):
        d1, dg = _rms_bwd(x, g, dh1)
        return [dx2 + d1], [dg]

    dx, G["pre_mix_norm"] = _rowwise("pre_mix_bwd", b8_fn, [_row(S["x"]), _row(dh1), _row(dx2)], [g_pre], [(D_MODEL, F32)], [(1, D_MODEL)])
    return dx, G


def _chip_sums(tag, pieces):
    theirs = _swap_cores("swap_" + tag, pieces)
    out = []
    for i, p in enumerate(pieces):
        _, _, r2, c = p.shape
        mine = lax.dynamic_index_in_dim(p, lax.axis_index("c"), axis=0, keepdims=False).reshape(4 * r2, c)
        (cs,) = _rowwise(f"presum_{tag}{i}", lambda a, b: ([a.astype(F32) + b.astype(F32)], []),
                         [_row(mine), _row(theirs[i].reshape(4 * r2, c))], [], [(c, BF16)])
        out.append(cs.reshape(4, r2, c))
    return out


def _halves(blocks):
    r2 = blocks[0].shape[0] // 2
    return jnp.stack([jnp.stack([b[h * r2:(h + 1) * r2] for b in blocks]) for h in range(2)])


def _layer_pieces(g, like):
    cols = like["w_in"].shape[2]
    win = _halves([g["w_in"][:, s * cols:(s + 1) * cols].astype(BF16) for s in range(4)])
    packs = [_pack_rows([_shard_of_full(n, g[n][None], s)[0] for n in BIG], BIG_ROWS // 2, BF16) for s in range(4)]
    return [win, _halves(packs)]


def _unpack_layer(flat, like):
    out, r = {}, 0
    for n in BIG:
        shp = like[n].shape[1:]
        nr = math.prod(shp) // D_MODEL
        out[n] = flat[r:r + nr].reshape(shp)
        r += nr
    return out


def _step(x, target, shard_w, shard_m, shard_v):
    depth, _, win_cols = shard_w["w_in"].shape
    win_rows = depth * D_MODEL

    def win2d(a):
        return a.reshape(win_rows, win_cols)

    conv_rows = shard_w["conv_w"].reshape(-1, D_MODEL)
    win_g, conv_g = _allgather_chips("gather_w_in", [
        win2d(shard_w["w_in"]).astype(BF16), jnp.pad(conv_rows, ((0, 16 - conv_rows.shape[0]), (0, 0)))])
    packed = _pack_big(shard_w, BF16)
    send_sems, recv_sems, packed_thru, land_thru, token = _send_start("gather_rest_start", [packed], conv_g, False, 0)

    def late(after):
        (landed,) = _send_wait("gather_rest_wait", send_sems, recv_sems, packed_thru, land_thru, after, False)
        me = 2 * lax.axis_index("x") + lax.axis_index("y")
        per_chip = [_unpack_big(jnp.where(me == s, packed, landed[s]), shard_w) for s in range(4)]
        return {n: _full_of_shards(n, [per_chip[s][n] for s in range(4)]) for n in BIG if n != "conv_w"}

    W = {"late": late}
    W["w_in"] = _to_proj_layout(jnp.concatenate([win_g[s].reshape(depth, D_MODEL, win_cols) for s in range(4)], axis=2))
    W["conv_w"] = jnp.concatenate([conv_g[s][:conv_rows.shape[0]].reshape(shard_w["conv_w"].shape) for s in range(4)],
                                  axis=3).reshape(depth, SSD_CONV, CONV_DIM)
    P = {n: shard_w[n] for n in SMALL}
    P["pre_mix_norm"] = P["pre_mix_norm"] + token[0, 0]
    assert depth == 2
    me = 2 * lax.axis_index("x") + lax.axis_index("y")
    sent = {}

    def grads_ready(g):
        sent["sums"] = _chip_sums("l1_", _layer_pieces(g, shard_w))
        sent["sems"] = _send_start("scatter_l1_start", sent["sums"], sent["sums"][0], True, 1)
        return sent["sems"][4][0, 0]

    loss_part, grad_x, full = _local_step(x, target, W, P, grads_ready)
    send1, recv1, thru1, land1, _ = sent["sems"]
    landed1 = _send_wait("scatter_l1_wait", send1, recv1, thru1, land1, grad_x, True)
    staged1 = [jnp.stack([jnp.where(me == s, own[s], got[s]) for s in range(4)]) for own, got in zip(sent["sums"], landed1)]
    sums0 = _chip_sums("l0_", _layer_pieces({n: full[n][0] for n in BIG + ("w_in",)}, shard_w))
    staged0 = _scatter_chips("scatter_grads", sums0)
    halves = [_sum_slots(f"sum_grads{i}", st) for i, st in enumerate(list(staged0) + staged1)]
    win0, big0, win1, big1 = [sh.reshape(2 * sh.shape[1], sh.shape[2]) for sh in _allgather_cores("share_grads", halves)]
    g_win = jnp.concatenate([win0, win1], axis=0)
    per_layer = [_unpack_layer(big0, shard_w), _unpack_layer(big1, shard_w)]
    g_big = _pack_big({n: jnp.stack([per_layer[0][n], per_layer[1][n]]) for n in BIG}, F32)
    d_win, m_win, v_win = _adamw("adamw_w_in", win2d(shard_w["w_in"]), g_win, win2d(shard_m["w_in"]), win2d(shard_v["w_in"]))
    d_big, m_big, v_big = _adamw("adamw_big", _pack_big(shard_w, F32), g_big, _pack_big(shard_m, F32), _pack_big(shard_v, F32))

    small = _allgather_all("gather_small", _pack_small(full, loss_part))
    g_small = _sum_slots("sum_small", small, tb=SMALL_ROWS)
    d_small, m_small, v_small = _adamw("adamw_small", _pack_small(shard_w, jnp.zeros((), F32)), g_small,
                                       _pack_small(shard_m, jnp.zeros((), F32)), _pack_small(shard_v, jnp.zeros((), F32)))

    outs = {}
    for tag, win, big, sm in (("grad", g_win, g_big, g_small), ("delta", d_win, d_big, d_small),
                              ("new_m", m_win, m_big, m_small), ("new_v", v_win, v_big, v_small)):
        ub = _unpack_big(big, shard_w)
        us, extra = _unpack_small(sm, shard_w)
        outs[tag] = {"w_in": win.reshape(shard_w["w_in"].shape), **ub, **us}
        if tag == "grad":
            loss = extra
    return loss, grad_x, outs


def _local_step(x, target, W, P, grads_ready=None):
    depth = W["w_in"].shape[0]
    W = dict(W, w_in_main=W["w_in"][:, :, :N_MAIN], w_in_qkv=W["w_in"][:, :, N_MAIN:])
    onehot_t = (_bucket_map().reshape(1, -1) == jnp.arange(N_BUCKETS)[:, None]).astype(F32)
    bias = _bias_from_table(P["rel_bias_table"], onehot_t).reshape(ATTN_HEADS, BLOCK, 3 * BLOCK)
    bias4 = jnp.where(_band_mask(), bias, NEG).reshape(ATTN_KV, REP, BLOCK, 3 * BLOCK)

    def pre_fn(x, g):
        return [_rms_fwd(x, g)], []

    (h1,) = _rowwise("pre_mix_fwd", pre_fn, [_row(x)], [P["pre_mix_norm"][0].reshape(1, D_MODEL)], [(D_MODEL, BF16)])
    saved = []
    loss_cols = dxl = None
    for l in range(depth):
        S = _layer_fwd(h1, x, W, P, l, bias4)
        saved.append(S)
        g_pmlp = P["post_mlp_norm"][l].reshape(1, D_MODEL)
        if l + 1 < depth:
            def post_fn(x2, f2, g1, g2):
                x3 = x2 + _rms_fwd(f2, g1)
                return [x3, _rms_fwd(x3, g2)], []

            x, h1 = _rowwise("post_mlp_fwd", post_fn, [_row(S["x2"]), _row(S["f2"])],
                             [g_pmlp, P["pre_mix_norm"][l + 1].reshape(1, D_MODEL)], [(D_MODEL, F32), (D_MODEL, BF16)])
        else:
            def loss_fn(x2, f2, tgt, g1):
                diff = x2 + _rms_fwd(f2, g1) - tgt
                return [diff * (1.0 / D_MODEL)], [jnp.sum(diff * diff, axis=0, keepdims=True)]

            dxl, loss_cols = _rowwise("loss", loss_fn, [_row(S["x2"]), _row(S["f2"]), _row(target)], [g_pmlp],
                                      [(D_MODEL, F32)], [(1, D_MODEL)])
    loss_part = 0.5 * jnp.sum(loss_cols) / D_MODEL

    grads = [None] * depth
    dx = dxl
    zero = None
    for l in reversed(range(depth)):
        dx, grads[l] = _layer_bwd(saved[l], dx, W, P, l, bias4, onehot_t, zero)
        if grads_ready is not None and l == depth - 1 and depth > 1:
            zero = grads_ready(grads[l])
    grad_x = dx

    full = {n: jnp.stack([grads[l][n] for l in range(depth)]) for n in ALL_W if n != "rel_bias_table"}
    full["rel_bias_table"] = sum(grads[l]["rel_bias_table"] for l in range(depth))
    return loss_part, grad_x, full


def kernel(x, pre_mix_norm, w_in, b_gate, conv_w, conv_b, dt_bias, a_log, d_skip, ssd_norm, w_ssd_out, attn_sink, rel_bias_table, w_attn_out, w_o, post_mix_norm, pre_mlp_norm, w_mlp_in, w_mlp_out, post_mlp_norm, loss_target, m_pre_mix_norm, m_w_in, m_b_gate, m_conv_w, m_conv_b, m_dt_bias, m_a_log, m_d_skip, m_ssd_norm, m_w_ssd_out, m_attn_sink, m_rel_bias_table, m_w_attn_out, m_w_o, m_post_mix_norm, m_pre_mlp_norm, m_w_mlp_in, m_w_mlp_out, m_post_mlp_norm, v_pre_mix_norm, v_w_in, v_b_gate, v_conv_w, v_conv_b, v_dt_bias, v_a_log, v_d_skip, v_ssd_norm, v_w_ssd_out, v_attn_sink, v_rel_bias_table, v_w_attn_out, v_w_o, v_post_mix_norm, v_pre_mlp_norm, v_w_mlp_in, v_w_mlp_out, v_post_mlp_norm):
    a = locals()
    shard_w = {n: a[n] for n in ALL_W}
    shard_m = {n: a["m_" + n] for n in ALL_W}
    shard_v = {n: a["v_" + n] for n in ALL_W}
    loss, grad_x, outs = _step(x[0], loss_target[0], shard_w, shard_m, shard_v)
    return (loss, grad_x[None], *[outs["grad"][n] for n in ALL_W], *[outs["delta"][n] for n in ALL_W],
            *[outs["new_m"][n] for n in ALL_W], *[outs["new_v"][n] for n in ALL_W])
```

```python
import math

import jax
import jax.numpy as jnp
from jax import lax
from jax.experimental import pallas as pl
from jax.experimental.pallas import tpu as pltpu

F32 = jnp.float32
BF16 = jnp.bfloat16
MESH = pl.DeviceIdType.MESH

VMEM_LIMIT_BYTES = 52 * 1024 * 1024
LANES = 128
SUBLANES = 8

EPS = 1e-6
D_MODEL = 1024
D_INNER = 2048
SSD_HEADS = 32
SSD_HEAD_DIM = 64
SSD_GROUPS = 8
SSD_STATE = 128
SSD_CONV = 5
CHUNK = 128
CONV_DIM = 4096
ATTN_HEADS = 16
ATTN_KV = 4
ATTN_DIM = 64
BLOCK = 128
N_BUCKETS = 32
MAX_DISTANCE = 128
D_FF = 4096
N_IN = 9792
NEG = -1e30

OFF_Z, OFF_G, OFF_XBC, OFF_DT, N_MAIN, N_PROJ = 0, 2048, 4096, 8192, 8320, 10368
N_QKV = N_PROJ - N_MAIN

ADAM_LR, ADAM_B1, ADAM_B2, ADAM_EPS, ADAM_WD, ADAM_STEP = 0.001, 0.9, 0.999, 1e-08, 0.01, 10

BIG_ROWS = 6656
SMALL_ROWS = 32


def _cparams(*sem):
    return pltpu.CompilerParams(dimension_semantics=sem, vmem_limit_bytes=VMEM_LIMIT_BYTES)


def _dot(a, b, precision=None):
    return lax.dot_general(a, b, (((1,), (0,)), ((), ())), preferred_element_type=F32, precision=precision)


def _dot_nt(a, b, precision=None):
    return lax.dot_general(a, b, (((1,), (1,)), ((), ())), preferred_element_type=F32, precision=precision)


def _dot_tn(a, b):
    return lax.dot_general(a, b, (((0,), (0,)), ((), ())), preferred_element_type=F32)


def _pick(n, prefs):
    for p in prefs:
        if n % p == 0:
            return p
    return n


def _matmul(name, a, b, mode, out_dtype=F32, epilogue=None, extras=()):
    if mode == "nn":
        (m, k), (_, n) = a.shape, b.shape
    elif mode == "nt":
        (m, k), (n, _) = a.shape, b.shape
    else:
        (k, m), (_, n) = a.shape, b.shape
    tm = _pick(m, (512, 256, 128)) if mode == "tn" else _pick(m, (1024, 512, 256, 128))
    tn = _pick(n, (1024, 1152, 640, 512, 256, 128))
    tk = _pick(k, (1024, 3456, 512, 256, 128)) if mode != "tn" else _pick(k, (2048, 512, 256, 128))
    if tk > 2048:
        tm = _pick(m, (512, 256, 128))
    nk = k // tk
    if mode == "nn":
        a_spec = pl.BlockSpec((tm, tk), lambda i, j, q: (i, q))
        b_spec = pl.BlockSpec((tk, tn), lambda i, j, q: (q, j))
        fn = _dot
    elif mode == "nt":
        a_spec = pl.BlockSpec((tm, tk), lambda i, j, q: (i, q))
        b_spec = pl.BlockSpec((tn, tk), lambda i, j, q: (j, q))
        fn = _dot_nt
    else:
        a_spec = pl.BlockSpec((tk, tm), lambda i, j, q: (q, i))
        b_spec = pl.BlockSpec((tk, tn), lambda i, j, q: (q, j))
        fn = _dot_tn

    tile = pl.BlockSpec((tm, tn), lambda i, j, q: (i, j))
    n_ex = len(extras)

    def body(a_ref, b_ref, *rest):
        ex_refs, o_ref = rest[:n_ex], rest[n_ex]

        def store(acc):
            v = acc if epilogue is None else epilogue(acc, *[r[...] for r in ex_refs])
            o_ref[...] = v.astype(o_ref.dtype)

        p = fn(a_ref[...].astype(BF16), b_ref[...].astype(BF16))
        if nk == 1:
            store(p)
        else:
            acc_ref = rest[n_ex + 1]
            q = pl.program_id(2)

            @pl.when(q == 0)
            def _():
                acc_ref[...] = p

            @pl.when((q > 0) & (q < nk - 1))
            def _():
                acc_ref[...] += p

            @pl.when(q == nk - 1)
            def _():
                store(acc_ref[...] + p)

    return pl.pallas_call(
        body, name=name, grid=(m // tm, n // tn, nk),
        in_specs=[a_spec, b_spec] + [tile] * n_ex, out_specs=tile,
        out_shape=jax.ShapeDtypeStruct((m, n), out_dtype),
        scratch_shapes=[pltpu.VMEM((tm, tn), F32)] if nk > 1 else [],
        compiler_params=_cparams("parallel", "parallel", "arbitrary"),
    )(a, b, *extras)


def _row(arr, width=None, cb=0, lead=None):
    return (arr, width, cb, lead)


def _rowwise(name, fn, rows, vecs, outs, accs=(), tb=256):
    t = rows[0][0].shape[-2]
    tb = min(tb, t)
    in_specs, args = [], []
    for arr, width, cb, lead in rows:
        w = arr.shape[-1] if width is None else width
        if lead is None:
            in_specs.append(pl.BlockSpec((tb, w), lambda i, cb=cb: (i, cb)))
        else:
            in_specs.append(pl.BlockSpec((None, tb, w), lambda i, cb=cb, lead=lead: (lead, i, cb)))
        args.append(arr)
    for v in vecs:
        in_specs.append(pl.BlockSpec(v.shape, lambda i, nd=v.ndim: (0,) * nd))
        args.append(v)
    out_shape = [jax.ShapeDtypeStruct((t, c), dt) for c, dt in outs] + [jax.ShapeDtypeStruct(s, F32) for s in accs]
    out_specs = [pl.BlockSpec((tb, c), lambda i: (i, 0)) for c, _ in outs] + [pl.BlockSpec(s, lambda i: (0, 0)) for s in accs]
    n_in, n_out = len(args), len(outs)

    def body(*refs):
        vals = [r[...] for r in refs[:n_in]]
        o_vals, a_vals = fn(*vals)
        for r, v in zip(refs[n_in:n_in + n_out], o_vals):
            r[...] = v.astype(r.dtype)
        first = pl.program_id(0) == 0
        for r, v in zip(refs[n_in + n_out:], a_vals):
            @pl.when(first)
            def _(r=r, v=v):
                r[...] = v

            @pl.when(jnp.logical_not(first))
            def _(r=r, v=v):
                r[...] += v

    res = pl.pallas_call(
        body, name=name, grid=(t // tb,), in_specs=in_specs, out_specs=out_specs, out_shape=out_shape,
        compiler_params=_cparams("arbitrary"),
    )(*args)
    return res


def _rms_fwd(x, g):
    r = lax.rsqrt(jnp.mean(x * x, axis=-1, keepdims=True) + EPS)
    return x * r * g


def _rms_bwd(x, g, dy):
    r = lax.rsqrt(jnp.mean(x * x, axis=-1, keepdims=True) + EPS)
    xh = x * r
    dxh = dy * g
    dx = r * (dxh - xh * jnp.mean(dxh * xh, axis=-1, keepdims=True))
    return dx, jnp.sum(dy * xh, axis=0, keepdims=True)


def _silu(x):
    return x * jax.nn.sigmoid(x)


def _silu_grad(x):
    s = jax.nn.sigmoid(x)
    return s * (1.0 + x * (1.0 - s))


GROUP_W = D_INNER // SSD_GROUPS


def _gated_norm_fwd(y, z, w):
    u = y * _silu(z)
    parts = []
    for j in range(SSD_GROUPS):
        ug = u[:, j * GROUP_W:(j + 1) * GROUP_W]
        parts.append(ug * lax.rsqrt(jnp.mean(ug * ug, axis=-1, keepdims=True) + EPS))
    return jnp.concatenate(parts, axis=-1) * w


def _gated_norm_bwd(y, z, w, dyn):
    sz = _silu(z)
    u = y * sz
    duh = dyn * w
    du_parts, uh_parts = [], []
    for j in range(SSD_GROUPS):
        sl = slice(j * GROUP_W, (j + 1) * GROUP_W)
        ug = u[:, sl]
        r = lax.rsqrt(jnp.mean(ug * ug, axis=-1, keepdims=True) + EPS)
        uh = ug * r
        dg = duh[:, sl]
        du_parts.append(r * (dg - uh * jnp.mean(dg * uh, axis=-1, keepdims=True)))
        uh_parts.append(uh)
    du = jnp.concatenate(du_parts, axis=-1)
    uh = jnp.concatenate(uh_parts, axis=-1)
    dw = jnp.sum(dyn * uh, axis=0, keepdims=True)
    return du * sz, du * y * _silu_grad(z), dw


HALO = SUBLANES


def _halo_specs(tb, cb, col0, t):
    nblk8 = t // HALO
    per = tb // HALO
    main = pl.BlockSpec((tb, cb), lambda j, i: (i, col0 + j))
    prev = pl.BlockSpec((HALO, cb), lambda j, i: (jnp.maximum(i * per - 1, 0), col0 + j))
    nxt = pl.BlockSpec((HALO, cb), lambda j, i: (jnp.minimum((i + 1) * per, nblk8 - 1), col0 + j))
    return main, prev, nxt


def _fill_ext(ext_ref, cur_ref, prev_ref, next_ref, tb, ni):
    i = pl.program_id(1)
    ext_ref[0:HALO, :] = jnp.where(i > 0, prev_ref[...], 0.0)
    ext_ref[HALO:HALO + tb, :] = cur_ref[...]
    ext_ref[HALO + tb:HALO + tb + HALO, :] = jnp.where(i < ni - 1, next_ref[...], 0.0)


def _conv_fwd(proj, w, b):
    t = proj.shape[0]
    tb, cb = min(512, t), 512
    ni, nj = t // tb, CONV_DIM // cb
    main, prev, nxt = _halo_specs(tb, cb, OFF_XBC // cb, t)
    pad = (SSD_CONV - 1) // 2

    def body(u_ref, up_ref, un_ref, w_ref, b_ref, pre_ref, act_ref, ext_ref):
        _fill_ext(ext_ref, u_ref, up_ref, un_ref, tb, ni)
        acc = jnp.broadcast_to(b_ref[...], (tb, cb))
        for k in range(SSD_CONV):
            acc = acc + w_ref[k:k + 1, :] * ext_ref[pl.ds(HALO + k - pad, tb), :]
        pre_ref[...] = acc
        act_ref[...] = _silu(acc)

    out = pl.BlockSpec((tb, cb), lambda j, i: (i, j))
    return pl.pallas_call(
        body, name="conv_fwd", grid=(nj, ni),
        in_specs=[main, prev, nxt, pl.BlockSpec((SSD_CONV, cb), lambda j, i: (0, j)), pl.BlockSpec((1, cb), lambda j, i: (0, j))],
        out_specs=[out, out],
        out_shape=[jax.ShapeDtypeStruct((t, CONV_DIM), F32)] * 2,
        scratch_shapes=[pltpu.VMEM((tb + 2 * HALO, cb), F32)],
        compiler_params=_cparams("parallel", "arbitrary"),
    )(proj, proj, proj, w, b)


def _conv_bwd(dpre, proj, w):
    t = proj.shape[0]
    tb, cb = min(512, t), 512
    ni, nj = t // tb, CONV_DIM // cb
    umain, uprev, unext = _halo_specs(tb, cb, OFF_XBC // cb, t)
    dmain, dprev, dnext = _halo_specs(tb, cb, 0, t)
    pad = (SSD_CONV - 1) // 2

    def body(d_ref, dp_ref, dn_ref, u_ref, up_ref, un_ref, w_ref, du_ref, dw_ref, db_ref, extd_ref, extu_ref):
        _fill_ext(extd_ref, d_ref, dp_ref, dn_ref, tb, ni)
        _fill_ext(extu_ref, u_ref, up_ref, un_ref, tb, ni)
        d = d_ref[...]
        du = jnp.zeros((tb, cb), F32)
        @pl.when(pl.program_id(1) == 0)
        def _():
            dw_ref[...] = jnp.zeros_like(dw_ref)
            db_ref[...] = jnp.zeros_like(db_ref)

        for k in range(SSD_CONV):
            du = du + w_ref[k:k + 1, :] * extd_ref[pl.ds(HALO - k + pad, tb), :]
            dw_ref[k:k + 1, :] += jnp.sum(d * extu_ref[pl.ds(HALO + k - pad, tb), :], axis=0, keepdims=True)
        du_ref[...] = du.astype(du_ref.dtype)
        db_ref[...] += jnp.sum(d, axis=0, keepdims=True)

    return pl.pallas_call(
        body, name="conv_bwd", grid=(nj, ni),
        in_specs=[dmain, dprev, dnext, umain, uprev, unext, pl.BlockSpec((SSD_CONV, cb), lambda j, i: (0, j))],
        out_specs=[pl.BlockSpec((tb, cb), lambda j, i: (i, j)), pl.BlockSpec((SSD_CONV, cb), lambda j, i: (0, j)),
                   pl.BlockSpec((1, cb), lambda j, i: (0, j))],
        out_shape=[jax.ShapeDtypeStruct((t, CONV_DIM), BF16), jax.ShapeDtypeStruct((SSD_CONV, CONV_DIM), F32),
                   jax.ShapeDtypeStruct((1, CONV_DIM), F32)],
        scratch_shapes=[pltpu.VMEM((tb + 2 * HALO, cb), F32)] * 2,
        compiler_params=_cparams("parallel", "arbitrary"),
    )(dpre, dpre, dpre, proj, proj, proj, w)


PAIR = 2 * SSD_HEAD_DIM
HI = lax.Precision.HIGHEST


def _ssd_prelude(d, dt_ref, dtt_ref, ar_ref, ac_ref):
    li = lax.broadcasted_iota(jnp.int32, (CHUNK, CHUNK), 0)
    si = lax.broadcasted_iota(jnp.int32, (CHUNK, CHUNK), 1)
    fwd = d == 0
    hi, lo = jnp.where(fwd, li, si), jnp.where(fwd, si, li)
    tri = hi >= lo
    trif = tri.astype(F32)
    trit = (hi <= lo).astype(F32)
    dt = dt_ref[...]
    adt = dt * ar_ref[...]
    adtt = dtt_ref[...] * ac_ref[...]
    p = _dot(trif, adt, HI)
    pt = _dot_nt(adtt, trif, HI)
    tot = jnp.sum(adt, axis=0, keepdims=True)
    return tri, trit, dt, p, pt, tot


def _ssd_specs(nc, rev):
    def cidx(d, c):
        up = (d == 1) if rev else (d == 0)
        return jnp.where(up, c, nc - 1 - c)

    specs = [
        pl.BlockSpec((CHUNK, D_INNER), lambda d, c: (cidx(d, c), 0)),
        pl.BlockSpec((CHUNK, 1024), lambda d, c: (cidx(d, c), 2)),
        pl.BlockSpec((CHUNK, 1024), lambda d, c: (cidx(d, c), 3)),
        pl.BlockSpec((None, CHUNK, SSD_HEADS), lambda d, c: (d, cidx(d, c), 0)),
        pl.BlockSpec((None, SSD_HEADS, CHUNK), lambda d, c: (d, 0, cidx(d, c))),
        pl.BlockSpec((None, 1, SSD_HEADS), lambda d, c: (d, 0, 0)),
        pl.BlockSpec((None, SSD_HEADS, 1), lambda d, c: (d, 0, 0)),
        pl.BlockSpec((CHUNK, D_INNER), lambda d, c: (cidx(d, c), d)),
    ]
    return cidx, specs


def _head_decay(tri, p, pt, tot, h):
    pb = jnp.broadcast_to(p[:, h:h + 1], (CHUNK, CHUNK))
    dec = jnp.exp(jnp.where(tri, pb - pt[h:h + 1, :], NEG))
    return dec, jnp.exp(tot[:, h:h + 1] - pb), jnp.exp(pb)


def _ssd_fwd(act, dt2, dt2t, a_row, a_col, dte):
    t = act.shape[0]
    nc = t // CHUNK
    cidx, specs = _ssd_specs(nc, rev=False)

    def body(xs_ref, bs_ref, cs_ref, dt_ref, dtt_ref, ar_ref, ac_ref, dte_ref, y_ref, st_ref, h_ref):
        d, c = pl.program_id(0), pl.program_id(1)

        @pl.when(c == 0)
        def _():
            h_ref[...] = jnp.zeros_like(h_ref)

        st_ref[...] = h_ref[...]
        tri, _, _, p, pt, tot = _ssd_prelude(d, dt_ref, dtt_ref, ar_ref, ac_ref)
        etot = jnp.exp(tot)
        lane = lax.broadcasted_iota(jnp.int32, (CHUNK, PAIR), 1) < SSD_HEAD_DIM
        rowh = lax.broadcasted_iota(jnp.int32, (PAIR, SSD_STATE), 0) < SSD_HEAD_DIM
        for g in range(SSD_GROUPS):
            gs = slice(g * SSD_STATE, (g + 1) * SSD_STATE)
            bg = bs_ref[:, gs]
            cb = cs_ref[:, gs].astype(BF16)
            cbm = _dot_nt(cb, bg.astype(BF16))
            for pr in range(2):
                h0 = g * 4 + pr * 2
                h1 = h0 + 1
                sl = slice(h0 * SSD_HEAD_DIM, h0 * SSD_HEAD_DIM + PAIR)
                xdt = (xs_ref[:, sl] * dte_ref[:, sl]).astype(BF16)
                yd, st, epb = [], [], []
                for h in (h0, h1):
                    dec, wb, eb = _head_decay(tri, p, pt, tot, h)
                    yd.append(_dot((cbm * dec).astype(BF16), xdt))
                    st.append(_dot_tn(xdt, (bg * wb).astype(BF16)))
                    epb.append(eb)
                hin = h_ref[sl, :]
                yo = _dot_nt(cb, hin.astype(BF16)) * jnp.where(lane, epb[0], epb[1])
                y_ref[:, sl] = jnp.where(lane, yd[0], yd[1]) + yo
                et = jnp.where(rowh, etot[:, h0:h0 + 1], etot[:, h1:h1 + 1])
                h_ref[sl, :] = hin * et + jnp.where(rowh, st[0], st[1])

    return pl.pallas_call(
        body, name="ssd_fwd", grid=(2, nc), in_specs=specs,
        out_specs=[pl.BlockSpec((None, CHUNK, D_INNER), lambda d, c: (d, cidx(d, c), 0)),
                   pl.BlockSpec((None, None, D_INNER, SSD_STATE), lambda d, c: (d, cidx(d, c), 0, 0))],
        out_shape=[jax.ShapeDtypeStruct((2, t, D_INNER), F32), jax.ShapeDtypeStruct((2, nc, D_INNER, SSD_STATE), F32)],
        scratch_shapes=[pltpu.VMEM((D_INNER, SSD_STATE), F32)],
        compiler_params=_cparams("arbitrary", "arbitrary"),
    )(act, act, act, dt2, dt2t, a_row, a_col, dte)


def _put_col(acc, col, h):
    lane = lax.broadcasted_iota(jnp.int32, acc.shape, 1)
    return jnp.where(lane == h, col, acc)


def _put_row(acc, row, h):
    sub = lax.broadcasted_iota(jnp.int32, acc.shape, 0)
    return jnp.where(sub == h, row, acc)


def _sum_all(x):
    return jnp.sum(jnp.sum(x, axis=0, keepdims=True), axis=1, keepdims=True)


def _ssd_bwd(act, dt2, dt2t, a_row, a_col, dte, dy, states):
    t = act.shape[0]
    nc = t // CHUNK
    cidx, specs = _ssd_specs(nc, rev=True)
    specs = specs + [
        pl.BlockSpec((CHUNK, D_INNER), lambda d, c: (cidx(d, c), 0)),
        pl.BlockSpec((None, None, D_INNER, SSD_STATE), lambda d, c: (d, cidx(d, c), 0, 0)),
    ]

    def body(xs_ref, bs_ref, cs_ref, dt_ref, dtt_ref, ar_ref, ac_ref, dte_ref, dy_ref, st_ref,
             dxs_ref, dbs_ref, dcs_ref, ddt_ref, da_ref, dh_ref):
        d, c = pl.program_id(0), pl.program_id(1)

        @pl.when(c == 0)
        def _():
            dh_ref[...] = jnp.zeros_like(dh_ref)
            da_ref[...] = jnp.zeros_like(da_ref)

        tri, trit, dt, p, pt, tot = _ssd_prelude(d, dt_ref, dtt_ref, ar_ref, ac_ref)
        etot = jnp.exp(tot)
        lane = lax.broadcasted_iota(jnp.int32, (CHUNK, PAIR), 1) < SSD_HEAD_DIM
        rowh = lax.broadcasted_iota(jnp.int32, (PAIR, SSD_STATE), 0) < SSD_HEAD_DIM
        first_head = lax.broadcasted_iota(jnp.int32, (PAIR, LANES), 0) < SSD_HEAD_DIM
        out_lane = lax.broadcasted_iota(jnp.int32, (PAIR, LANES), 1)
        ddtx = jnp.zeros((CHUNK, LANES), F32)
        lane32 = lax.broadcasted_iota(jnp.int32, (CHUNK, SSD_HEADS), 1)
        dp_col = jnp.zeros((CHUNK, SSD_HEADS), F32)
        dp_row = jnp.zeros((SSD_HEADS, CHUNK), F32)
        dtot = jnp.zeros((1, SSD_HEADS), F32)
        for g in range(SSD_GROUPS):
            gs = slice(g * SSD_STATE, (g + 1) * SSD_STATE)
            bg = bs_ref[:, gs]
            bb = bg.astype(BF16)
            cb = cs_ref[:, gs].astype(BF16)
            cbm = _dot_nt(cb, bb)
            dcb = jnp.zeros((CHUNK, CHUNK), F32)
            dc_acc = jnp.zeros((CHUNK, SSD_STATE), F32)
            db_acc = jnp.zeros((CHUNK, SSD_STATE), F32)
            for pr in range(2):
                h0 = g * 4 + pr * 2
                h1 = h0 + 1
                sl = slice(h0 * SSD_HEAD_DIM, h0 * SSD_HEAD_DIM + PAIR)
                xp = xs_ref[:, sl]
                dtp = dte_ref[:, sl]
                xdt_f = xp * dtp
                xdt = xdt_f.astype(BF16)
                dyp = dy_ref[:, sl]
                dyb = dyp.astype(BF16)
                hin = st_ref[sl, :]
                dh = dh_ref[sl, :]
                hb = hin.astype(BF16)
                dhb = dh.astype(BF16)
                heads = [_head_decay(tri, p, pt, tot, h) for h in (h0, h1)]
                dye = dyp * jnp.where(lane, heads[0][2], heads[1][2])
                dyeb = dye.astype(BF16)
                gy = _dot_nt(cb, hb) * dye
                dc_acc = dc_acc + _dot(dyeb, hb)
                dhin = _dot_tn(dyeb, cb)
                hh = dh * hin
                dxdt = jnp.zeros((CHUNK, PAIR), F32)
                for idx, h in enumerate((h0, h1)):
                    hm = lane if idx == 0 else jnp.logical_not(lane)
                    rm = rowh if idx == 0 else jnp.logical_not(rowh)
                    dec, wb, _ = heads[idx]
                    mf = cbm * dec
                    t1 = _dot_tn(mf.astype(BF16), dyb)
                    t2 = _dot_nt((bg * wb).astype(BF16), dhb)
                    dxdt = jnp.where(hm, t1 + t2, dxdt)
                    dm = _dot_nt(jnp.where(hm, dyp, 0.0).astype(BF16), xdt)
                    dcb = dcb + dm * dec
                    e = dm * mf
                    qw = _dot(jnp.where(hm, xdt_f, 0.0).astype(BF16), dhb) * wb
                    db_acc = db_acc + qw
                    qwb = qw * bg
                    col = jnp.sum(e + jnp.where(hm, gy, 0.0) - qwb, axis=1, keepdims=True)
                    dp_col = jnp.where(lane32 == h, col, dp_col)
                    dp_row = _put_row(dp_row, -jnp.sum(e, axis=0, keepdims=True), h)
                    dtot_h = _sum_all(qwb) + etot[:, h:h + 1] * _sum_all(jnp.where(rm, hh, 0.0))
                    dtot = _put_col(dtot, dtot_h, h)
                dxs_ref[:, sl] = dxdt * dtp
                ddx = dxdt * xp
                ddx_hi = ddx.astype(BF16)
                ddx_lo = (ddx - ddx_hi.astype(F32)).astype(BF16)
                route = (out_lane == jnp.where(first_head, h0, h1)).astype(BF16)
                ddtx = ddtx + _dot(ddx_hi, route) + _dot(ddx_lo, route)
                et = jnp.where(rowh, etot[:, h0:h0 + 1], etot[:, h1:h1 + 1])
                dh_ref[sl, :] = dh * et + dhin
            dcbb = dcb.astype(BF16)
            dcs_ref[:, gs] = _dot(dcbb, bb) + dc_acc
            dbs_ref[:, gs] = _dot_tn(dcbb, cb) + db_acc
        d_adt = _dot(trit, dp_col, HI) + _dot_nt(trit, dp_row, HI) + dtot
        ddt_ref[...] = ddtx[:, :SSD_HEADS] + ar_ref[...] * d_adt
        da_ref[...] += jnp.sum(dt * d_adt, axis=0, keepdims=True)

    return pl.pallas_call(
        body, name="ssd_bwd", grid=(2, nc), in_specs=specs,
        out_specs=[pl.BlockSpec((None, CHUNK, D_INNER), lambda d, c: (d, cidx(d, c), 0)),
                   pl.BlockSpec((None, CHUNK, 1024), lambda d, c: (d, cidx(d, c), 0)),
                   pl.BlockSpec((None, CHUNK, 1024), lambda d, c: (d, cidx(d, c), 0)),
                   pl.BlockSpec((None, CHUNK, SSD_HEADS), lambda d, c: (d, cidx(d, c), 0)),
                   pl.BlockSpec((None, 1, SSD_HEADS), lambda d, c: (d, 0, 0))],
        out_shape=[jax.ShapeDtypeStruct((2, t, D_INNER), F32), jax.ShapeDtypeStruct((2, t, 1024), F32),
                   jax.ShapeDtypeStruct((2, t, 1024), F32), jax.ShapeDtypeStruct((2, t, SSD_HEADS), F32),
                   jax.ShapeDtypeStruct((2, 1, SSD_HEADS), F32)],
        scratch_shapes=[pltpu.VMEM((D_INNER, SSD_STATE), F32)],
        compiler_params=_cparams("arbitrary", "arbitrary"),
    )(act, act, act, dt2, dt2t, a_row, a_col, dte, dy, states)


REP = ATTN_HEADS // ATTN_KV
SCALE = ATTN_DIM ** -0.5
GROUP_Q = REP * ATTN_DIM
K_BLK0 = D_MODEL // LANES
V_BLK0 = K_BLK0 + ATTN_KV


def _attn_specs(nb):
    q = pl.BlockSpec((BLOCK, GROUP_Q), lambda g, n: (n, g))

    def kv(blk0):
        return [pl.BlockSpec((BLOCK, LANES), lambda g, n: (jnp.maximum(n - 1, 0), blk0 + g)),
                pl.BlockSpec((BLOCK, LANES), lambda g, n: (n, blk0 + g)),
                pl.BlockSpec((BLOCK, LANES), lambda g, n: (jnp.minimum(n + 1, nb - 1), blk0 + g))]

    bias = pl.BlockSpec((None, REP, BLOCK, 3 * BLOCK), lambda g, n: (g, 0, 0, 0))
    sink = pl.BlockSpec((None, REP, 1, LANES), lambda g, n: (g, 0, 0, 0))
    return q, kv(K_BLK0), kv(V_BLK0), bias, sink


def _band_mask():
    ii = lax.broadcasted_iota(jnp.int32, (BLOCK, 3 * BLOCK), 0)
    jj = lax.broadcasted_iota(jnp.int32, (BLOCK, 3 * BLOCK), 1)
    return (jj >= ii) & (jj - 2 * BLOCK <= ii)


def _attn_valid(n, nb):
    jj = lax.broadcasted_iota(jnp.int32, (1, 3 * BLOCK), 1)
    return ((jj >= BLOCK) | (n > 0)) & ((jj < 2 * BLOCK) | (n < nb - 1))


def _attn_probs(q, kcat, bias, snk, valid):
    s = jnp.where(valid, _dot_nt(q, kcat) + bias, NEG)
    m = jnp.maximum(jnp.max(s, axis=1, keepdims=True), snk)
    p = jnp.exp(s - m)
    es = jnp.exp(snk - m)
    r = 1.0 / (jnp.sum(p, axis=1, keepdims=True) + es)
    return p * r, es * r


def _stack_heads(ref, lane):
    parts = []
    for pr in range(REP // 2):
        tile = ref[:, pr * LANES:(pr + 1) * LANES]
        parts += [jnp.where(lane, tile, 0.0), jnp.where(lane, 0.0, tile)]
    return jnp.concatenate(parts, axis=0)


def _unstack_heads(x, lane):
    return jnp.concatenate([jnp.where(lane, x[(2 * pr) * BLOCK:(2 * pr + 1) * BLOCK], x[(2 * pr + 1) * BLOCK:(2 * pr + 2) * BLOCK])
                            for pr in range(REP // 2)], axis=1)


def _stack_bias(b_ref, s_ref):
    bias = jnp.concatenate([b_ref[r] for r in range(REP)], axis=0)
    snk = jnp.concatenate([jnp.broadcast_to(s_ref[r][:, 0:1], (BLOCK, 1)) for r in range(REP)], axis=0)
    return bias, snk


def _attn_fwd(qkv, bias4, sink4):
    t = qkv.shape[0]
    nb = t // BLOCK
    qs, ks, vs, bs, ss = _attn_specs(nb)

    def body(q_ref, kp_ref, kc_ref, kn_ref, vp_ref, vc_ref, vn_ref, b_ref, s_ref, o_ref):
        n = pl.program_id(1)
        kcat = jnp.concatenate([kp_ref[...], kc_ref[...], kn_ref[...]], axis=0)
        vcat = jnp.concatenate([vp_ref[...], vc_ref[...], vn_ref[...]], axis=0)
        valid = _attn_valid(n, nb)
        lane = lax.broadcasted_iota(jnp.int32, (BLOCK, LANES), 1) < ATTN_DIM
        bias, snk = _stack_bias(b_ref, s_ref)
        pn, _ = _attn_probs(_stack_heads(q_ref, lane) * SCALE, kcat, bias, snk, valid)
        o_ref[...] = _unstack_heads(_dot(pn.astype(BF16), vcat), lane).astype(o_ref.dtype)

    return pl.pallas_call(
        body, name="attn_fwd", grid=(ATTN_KV, nb), in_specs=[qs] + ks + vs + [bs, ss],
        out_specs=qs, out_shape=jax.ShapeDtypeStruct((t, D_MODEL), BF16),
        compiler_params=_cparams("parallel", "arbitrary"),
    )(qkv, qkv, qkv, qkv, qkv, qkv, qkv, bias4, sink4)


def _attn_bwd(qkv, bias4, sink4, do):
    t = qkv.shape[0]
    nb = t // BLOCK
    qs, ks, vs, bs, ss = _attn_specs(nb)
    part = pl.BlockSpec((3, BLOCK, LANES), lambda g, n: (0, n, g))

    def body(q_ref, kp_ref, kc_ref, kn_ref, vp_ref, vc_ref, vn_ref, b_ref, s_ref, do_ref,
             dq_ref, dk_ref, dv_ref, db_ref, ds_ref):
        n = pl.program_id(1)

        @pl.when(n == 0)
        def _():
            db_ref[...] = jnp.zeros_like(db_ref)
            ds_ref[...] = jnp.zeros_like(ds_ref)

        kcat = jnp.concatenate([kp_ref[...], kc_ref[...], kn_ref[...]], axis=0)
        vcat = jnp.concatenate([vp_ref[...], vc_ref[...], vn_ref[...]], axis=0)
        valid = _attn_valid(n, nb)
        lane = lax.broadcasted_iota(jnp.int32, (BLOCK, LANES), 1) < ATTN_DIM
        bias, snk = _stack_bias(b_ref, s_ref)
        q = _stack_heads(q_ref, lane)
        do = _stack_heads(do_ref, lane)
        pn, psink = _attn_probs(q * SCALE, kcat, bias, snk, valid)
        dp = _dot_nt(do, vcat)
        delta = jnp.sum(pn * dp, axis=1, keepdims=True)
        dsc = pn * (dp - delta)
        dsink = psink * delta
        for r in range(REP):
            rows = slice(r * BLOCK, (r + 1) * BLOCK)
            db_ref[r] += dsc[rows]
            ds_ref[r] += jnp.broadcast_to(-jnp.sum(dsink[rows], axis=0, keepdims=True), (1, LANES))
        dsb = (dsc * SCALE).astype(BF16)
        dq_ref[...] = _unstack_heads(_dot(dsb, kcat), lane).astype(dq_ref.dtype)
        dk = _dot_tn(dsb, q)
        dv = _dot_tn(pn.astype(BF16), do)
        for j in range(3):
            dk_ref[j] = dk[j * BLOCK:(j + 1) * BLOCK]
            dv_ref[j] = dv[j * BLOCK:(j + 1) * BLOCK]

    kv_cols = ATTN_KV * LANES
    return pl.pallas_call(
        body, name="attn_bwd", grid=(ATTN_KV, nb), in_specs=[qs] + ks + vs + [bs, ss, qs],
        out_specs=[qs, part, part, bs, ss],
        out_shape=[jax.ShapeDtypeStruct((t, D_MODEL), BF16), jax.ShapeDtypeStruct((3, t, kv_cols), F32),
                   jax.ShapeDtypeStruct((3, t, kv_cols), F32), jax.ShapeDtypeStruct(bias4.shape, F32),
                   jax.ShapeDtypeStruct(sink4.shape, F32)],
        compiler_params=_cparams("parallel", "arbitrary"),
    )(qkv, qkv, qkv, qkv, qkv, qkv, qkv, bias4, sink4, do)


def _kv_combine(name, parts):
    _, t, cols = parts.shape

    def body(p_ref, o_ref):
        z = jnp.zeros((BLOCK, LANES), F32)
        from_next = jnp.concatenate([p_ref[0, BLOCK:, :], z], axis=0)
        from_prev = jnp.concatenate([z, p_ref[2, :t - BLOCK, :]], axis=0)
        o_ref[...] = (from_next + p_ref[1] + from_prev).astype(o_ref.dtype)

    return pl.pallas_call(
        body, name=name, grid=(cols // LANES,),
        in_specs=[pl.BlockSpec((3, t, LANES), lambda g: (0, 0, g))],
        out_specs=pl.BlockSpec((t, LANES), lambda g: (0, g)),
        out_shape=jax.ShapeDtypeStruct((t, cols), BF16),
        compiler_params=_cparams("parallel"),
    )(parts)


def _t5_bucket(rel):
    nb = N_BUCKETS // 2
    max_exact = nb // 2
    ret = jnp.where(rel > 0, nb, 0)
    n = jnp.abs(rel)
    nf = jnp.maximum(n, 1).astype(jnp.float32)
    large = max_exact + (jnp.log(nf / max_exact) / math.log(MAX_DISTANCE / max_exact) * (nb - max_exact)).astype(jnp.int32)
    large = jnp.minimum(large, nb - 1)
    return ret + jnp.where(n < max_exact, n, large)


def _bucket_map():
    i = jnp.arange(BLOCK)[:, None]
    j = jnp.arange(3 * BLOCK)[None, :]
    return _t5_bucket(j - BLOCK - i)


def _bias_from_table(table, onehot_t):
    def body(t_ref, o_ref, out_ref):
        out_ref[...] = _dot(t_ref[...], o_ref[...], HI)

    return pl.pallas_call(body, name="bias_from_table", out_shape=jax.ShapeDtypeStruct((ATTN_HEADS, onehot_t.shape[1]), F32),
                          compiler_params=pltpu.CompilerParams(vmem_limit_bytes=VMEM_LIMIT_BYTES))(table.T, onehot_t)


def _bias_table_grad(dbias, onehot_t):
    def body(d_ref, o_ref, out_ref):
        out_ref[...] = _dot_nt(d_ref[...], o_ref[...], HI)

    return pl.pallas_call(body, name="bias_table_grad", out_shape=jax.ShapeDtypeStruct((ATTN_HEADS, N_BUCKETS), F32),
                          compiler_params=pltpu.CompilerParams(vmem_limit_bytes=VMEM_LIMIT_BYTES))(dbias, onehot_t)


HBM_SPEC = pl.BlockSpec(memory_space=pl.ANY)


def _comm_call(name, body, xs, out_shapes, n_sems):
    n = len(xs)
    return pl.pallas_call(
        body, name=name, in_specs=[HBM_SPEC] * n, out_specs=[HBM_SPEC] * n, out_shape=out_shapes,
        scratch_shapes=[pltpu.SemaphoreType.DMA((n * n_sems,)), pltpu.SemaphoreType.DMA((n * n_sems,))],
    )(*xs)


def _allgather_chips(name, xs):
    n = len(xs)

    def body(*refs):
        x_refs, out_refs, (send_sems, recv_sems) = refs[:n], refs[n:2 * n], refs[2 * n:]
        mx, my, mc = lax.axis_index("x"), lax.axis_index("y"), lax.axis_index("c")
        me = 2 * mx + my
        chips = [(1 - mx, my), (mx, 1 - my), (1 - mx, 1 - my)]
        sibling = (mx, my, 1 - mc)

        def part(i, slot, h):
            r2 = xs[i].shape[0] // 2
            return out_refs[i].at[slot, pl.ds(h * r2, r2)]

        def copy(i, k, src, dst, to):
            return pltpu.make_async_remote_copy(src_ref=src, dst_ref=dst, send_sem=send_sems.at[6 * i + k],
                                                recv_sem=recv_sems.at[6 * i + k], device_id=to, device_id_type=MESH)

        first = [copy(i, k, x_refs[i].at[pl.ds(mc * (xs[i].shape[0] // 2), xs[i].shape[0] // 2)], part(i, me, mc), (px, py, mc))
                 for i in range(n) for k, (px, py) in enumerate(chips)]
        for cp in first:
            cp.start()
        passed = []
        for k, (px, py) in enumerate(chips):
            for i in range(n):
                landed = part(i, 2 * px + py, mc)
                copy(i, k, landed, landed, (px, py, mc)).wait_recv()
                passed.append(copy(i, 3 + k, landed, landed, sibling))
                passed[-1].start()
        for k, (px, py) in enumerate(chips):
            for i in range(n):
                theirs = part(i, 2 * px + py, 1 - mc)
                copy(i, 3 + k, theirs, theirs, sibling).wait_recv()
        for cp in first + passed:
            cp.wait_send()

    return _comm_call(name, body, xs, [jax.ShapeDtypeStruct((4,) + x.shape, x.dtype) for x in xs], 6)


SEM_SPEC = pl.BlockSpec(memory_space=pltpu.SEMAPHORE)
DATAFLOW = pltpu.SideEffectType.DATAFLOW_SIDE_EFFECTING


def _send_start(name, xs, after, scatter, collective_id):
    n = len(xs)

    def body(*refs):
        x_refs, land_refs = refs[:n], refs[n:2 * n]
        send_sems, recv_sems, token = refs[2 * n + 1], refs[2 * n + 2], refs[-1]
        mx, my, mc = lax.axis_index("x"), lax.axis_index("y"), lax.axis_index("c")
        chips = [(1 - mx, my), (mx, 1 - my), (1 - mx, 1 - my)]
        barrier = pltpu.get_barrier_semaphore()
        for px, py in chips:
            pl.semaphore_signal(barrier, inc=1, device_id=(px, py, mc), device_id_type=MESH)
        pl.semaphore_wait(barrier, len(chips))
        for i in range(n):
            for k, (px, py) in enumerate(chips):
                src = x_refs[i].at[2 * px + py] if scatter else x_refs[i]
                pltpu.make_async_remote_copy(src_ref=src, dst_ref=land_refs[i].at[2 * mx + my], send_sem=send_sems.at[3 * i + k],
                                             recv_sem=recv_sems.at[3 * i + k], device_id=(px, py, mc), device_id_type=MESH).start()
        token[...] = jnp.zeros_like(token)

    lands = [lax.empty((4,) + (x.shape[1:] if scatter else x.shape), x.dtype) for x in xs]
    hbm = [pltpu.HBM(a.shape, a.dtype) for a in list(xs) + lands]
    out = pl.pallas_call(
        body, name=name,
        out_shape=(pltpu.SemaphoreType.DMA((3 * n,)), pltpu.SemaphoreType.DMA((3 * n,)), *hbm, jax.ShapeDtypeStruct((SUBLANES, LANES), F32)),
        in_specs=(HBM_SPEC,) * (2 * n + 1),
        out_specs=(SEM_SPEC, SEM_SPEC) + (HBM_SPEC,) * (2 * n) + (pl.BlockSpec(memory_space=pltpu.VMEM),),
        input_output_aliases={i: 2 + i for i in range(2 * n)},
        compiler_params=pltpu.CompilerParams(has_side_effects=DATAFLOW, collective_id=collective_id),
    )(*[pltpu.with_memory_space_constraint(a, pltpu.HBM) for a in list(xs) + lands], after)
    return out[0], out[1], list(out[2:2 + n]), list(out[2 + n:2 + 2 * n]), out[-1]


def _send_wait(name, send_sems, recv_sems, x_thrus, land_thrus, after, scatter):
    n = len(x_thrus)

    def body(*refs):
        x_refs, land_refs, send_sems, recv_sems = refs[:n], refs[n:2 * n], refs[2 * n], refs[2 * n + 1]
        mx, my, mc = lax.axis_index("x"), lax.axis_index("y"), lax.axis_index("c")
        chips = [(1 - mx, my), (mx, 1 - my), (1 - mx, 1 - my)]
        for i in range(n):
            for k, (px, py) in enumerate(chips):
                src = x_refs[i].at[0] if scatter else x_refs[i]
                copy = pltpu.make_async_remote_copy(src_ref=src, dst_ref=land_refs[i].at[2 * px + py], send_sem=send_sems.at[3 * i + k],
                                                    recv_sem=recv_sems.at[3 * i + k], device_id=(px, py, mc), device_id_type=MESH)
                copy.wait_send()
                copy.wait_recv()

    arrs = list(x_thrus) + list(land_thrus)
    out = pl.pallas_call(
        body, name=name, out_shape=tuple(pltpu.HBM(a.shape, a.dtype) for a in arrs),
        in_specs=(HBM_SPEC,) * (2 * n) + (SEM_SPEC, SEM_SPEC, HBM_SPEC), out_specs=(HBM_SPEC,) * (2 * n),
        input_output_aliases={i: i for i in range(2 * n)},
        compiler_params=pltpu.CompilerParams(has_side_effects=DATAFLOW),
    )(*arrs, send_sems, recv_sems, after)
    return list(out[n:])


def _swap_cores(name, xs, slab=True):
    n = len(xs)

    def body(*refs):
        x_refs, out_refs, (send_sems, recv_sems) = refs[:n], refs[n:2 * n], refs[2 * n:]
        mx, my, mc = lax.axis_index("x"), lax.axis_index("y"), lax.axis_index("c")
        sends = [pltpu.make_async_remote_copy(src_ref=x_refs[i].at[1 - mc] if slab else x_refs[i], dst_ref=out_refs[i],
                                              send_sem=send_sems.at[i], recv_sem=recv_sems.at[i],
                                              device_id=(mx, my, 1 - mc), device_id_type=MESH)
                 for i in range(n)]
        for cp in sends:
            cp.start()
        for cp in sends:
            cp.wait()

    return _comm_call(name, body, xs, [jax.ShapeDtypeStruct(x.shape[1:] if slab else x.shape, x.dtype) for x in xs], 1)


def _scatter_chips(name, gs):
    n = len(gs)

    def body(*refs):
        g_refs, out_refs, (send_sems, recv_sems) = refs[:n], refs[n:2 * n], refs[2 * n:]
        mx, my, mc = lax.axis_index("x"), lax.axis_index("y"), lax.axis_index("c")
        me = 2 * mx + my
        chips = [(1 - mx, my), (mx, 1 - my), (1 - mx, 1 - my)]

        def copy(i, k, src, dst, to):
            return pltpu.make_async_remote_copy(src_ref=src, dst_ref=dst, send_sem=send_sems.at[3 * i + k],
                                                recv_sem=recv_sems.at[3 * i + k], device_id=to, device_id_type=MESH)

        sends = [copy(i, k, g_refs[i].at[2 * px + py], out_refs[i].at[me], (px, py, mc))
                 for i in range(n) for k, (px, py) in enumerate(chips)]
        for cp in sends:
            cp.start()
        for i in range(n):
            for k, (px, py) in enumerate(chips):
                copy(i, k, g_refs[i].at[0], out_refs[i].at[2 * px + py], (px, py, mc)).wait_recv()
        for cp in sends:
            cp.wait_send()

    return _comm_call(name, body, gs, [jax.ShapeDtypeStruct(g.shape, g.dtype) for g in gs], 3)


def _allgather_all(name, x):
    def body(x_ref, out_ref, send_sems, recv_sems, local_sem):
        mx, my, mc = lax.axis_index("x"), lax.axis_index("y"), lax.axis_index("c")
        me = 4 * mx + 2 * my + mc
        flips = [(fx, fy, fc) for fx in (0, 1) for fy in (0, 1) for fc in (0, 1)][1:]
        peers = [(mx ^ fx, my ^ fy, mc ^ fc) for fx, fy, fc in flips]
        mine = pltpu.make_async_copy(x_ref, out_ref.at[me], local_sem)
        mine.start()
        sends = [pltpu.make_async_remote_copy(src_ref=x_ref, dst_ref=out_ref.at[me], send_sem=send_sems.at[k],
                                              recv_sem=recv_sems.at[k], device_id=peer, device_id_type=MESH)
                 for k, peer in enumerate(peers)]
        for cp in sends:
            cp.start()
        for k, (px, py, pc) in enumerate(peers):
            pltpu.make_async_remote_copy(src_ref=x_ref, dst_ref=out_ref.at[4 * px + 2 * py + pc], send_sem=send_sems.at[k],
                                         recv_sem=recv_sems.at[k], device_id=(px, py, pc), device_id_type=MESH).wait_recv()
        for cp in sends:
            cp.wait_send()
        mine.wait()

    return pl.pallas_call(
        body, name=name, in_specs=[HBM_SPEC], out_specs=HBM_SPEC,
        out_shape=jax.ShapeDtypeStruct((8,) + x.shape, x.dtype),
        scratch_shapes=[pltpu.SemaphoreType.DMA((7,)), pltpu.SemaphoreType.DMA((7,)), pltpu.SemaphoreType.DMA],
    )(x)


def _sum_slots(name, st, tb=256):
    n, r, c = st.shape
    tb = _pick(r, (tb, 128, 32))

    def body(s_ref, o_ref):
        acc = s_ref[0].astype(F32)
        for k in range(1, n):
            acc = acc + s_ref[k].astype(F32)
        o_ref[...] = acc

    return pl.pallas_call(
        body, name=name, grid=(r // tb,), in_specs=[pl.BlockSpec((n, tb, c), lambda i: (0, i, 0))],
        out_specs=pl.BlockSpec((tb, c), lambda i: (i, 0)), out_shape=jax.ShapeDtypeStruct((r, c), F32),
        compiler_params=_cparams("parallel"),
    )(st)


def _adamw(name, w, g, m, v):
    def fn(w, g, m, v):
        m2 = ADAM_B1 * m + (1.0 - ADAM_B1) * g
        v2 = ADAM_B2 * v + (1.0 - ADAM_B2) * jnp.square(g)
        m_hat = m2 / (1.0 - ADAM_B1 ** ADAM_STEP)
        v_hat = v2 / (1.0 - ADAM_B2 ** ADAM_STEP)
        delta = -ADAM_LR * (m_hat / (jnp.sqrt(v_hat) + ADAM_EPS) + ADAM_WD * w)
        return [delta, m2, v2], []

    tb = _pick(w.shape[0], (256, 32))
    return _rowwise(name, fn, [_row(w), _row(g), _row(m), _row(v)], [], [(w.shape[1], F32)] * 3, tb=tb)


BIG = ("w_ssd_out", "w_attn_out", "w_o", "w_mlp_in", "w_mlp_out", "conv_w")
SMALL = ("pre_mix_norm", "b_gate", "conv_b", "dt_bias", "a_log", "d_skip", "ssd_norm", "attn_sink", "rel_bias_table",
         "post_mix_norm", "pre_mlp_norm", "post_mlp_norm")
ALL_W = ("pre_mix_norm", "w_in", "b_gate", "conv_w", "conv_b", "dt_bias", "a_log", "d_skip", "ssd_norm", "w_ssd_out",
         "attn_sink", "rel_bias_table", "w_attn_out", "w_o", "post_mix_norm", "pre_mlp_norm", "w_mlp_in", "w_mlp_out",
         "post_mlp_norm")


def _pack_rows(parts, rows, dtype):
    flat = jnp.concatenate([p.reshape(-1, D_MODEL).astype(dtype) for p in parts], axis=0)
    return jnp.pad(flat, ((0, rows - flat.shape[0]), (0, 0)))


def _pack_big(shards, dtype):
    return _pack_rows([shards[n] for n in BIG], BIG_ROWS, dtype)


def _unpack_big(flat, like):
    out, r = {}, 0
    for n in BIG:
        shp = like[n].shape
        nr = math.prod(shp) // D_MODEL
        out[n] = flat[r:r + nr].reshape(shp)
        r += nr
    return out


def _pack_small(parts, extra=None):
    flat = jnp.concatenate([parts[n].reshape(-1).astype(F32) for n in SMALL] + ([extra.reshape(-1)] if extra is not None else []))
    return jnp.pad(flat, (0, SMALL_ROWS * D_MODEL - flat.shape[0])).reshape(SMALL_ROWS, D_MODEL)


def _unpack_small(flat2, like):
    flat = flat2.reshape(-1)
    out, r = {}, 0
    for n in SMALL:
        shp = like[n].shape
        k = math.prod(shp)
        out[n] = flat[r:r + k].reshape(shp)
        r += k
    return out, flat[r]


def _shard_of_full(name, full, s):
    if name in ("w_in", "w_mlp_in"):
        w = full.shape[2] // 4
        return full[:, :, s * w:(s + 1) * w]
    if name == "conv_w":
        w = full.shape[3] // 4
        return full[:, :, :, s * w:(s + 1) * w]
    w = full.shape[1] // 4
    return full[:, s * w:(s + 1) * w, :]


def _full_of_shards(name, shards):
    axis = {"w_in": 2, "w_mlp_in": 2, "conv_w": 3}.get(name, 1)
    return jnp.concatenate(shards, axis=axis)


def _to_proj_layout(w):
    z, xbc, dt, q, k, v, gates = (w[..., 0:2048], w[..., 2048:6144], w[..., 6144:6208], w[..., 6208:7232],
                                  w[..., 7232:7488], w[..., 7488:7744], w[..., 7744:9792])
    pad = jnp.zeros(w.shape[:-1] + (N_MAIN - OFF_DT - dt.shape[-1],), w.dtype)

    def doubled(a):
        h = a.reshape(a.shape[:-1] + (ATTN_KV, 1, ATTN_DIM))
        return jnp.broadcast_to(h, a.shape[:-1] + (ATTN_KV, 2, ATTN_DIM)).reshape(a.shape[:-1] + (2 * a.shape[-1],))

    return jnp.concatenate([z, gates, xbc, dt, pad, q, doubled(k), doubled(v)], axis=-1)


def _from_proj_layout(w):
    z, gates, xbc, dt, q, k2, v2 = (w[..., 0:2048], w[..., 2048:4096], w[..., 4096:8192], w[..., 8192:8256],
                                    w[..., 8320:9344], w[..., 9344:9856], w[..., 9856:10368])

    def folded(a):
        return a.reshape(a.shape[:-1] + (ATTN_KV, 2, ATTN_DIM)).sum(axis=-2).reshape(a.shape[:-1] + (a.shape[-1] // 2,))

    return jnp.concatenate([z, xbc, dt, q, folded(k2), folded(v2), gates], axis=-1)


def _layer_fwd(h1, x, W, P, l, bias4):
    t = x.shape[0]
    S = {"x": x, "h1": h1}
    proj = _matmul("proj", h1, W["w_in_main"][l], "nn")
    qkv = _matmul("proj_qkv", h1, W["w_in_qkv"][l], "nn", out_dtype=BF16)
    S["proj"] = proj
    pre, act = _conv_fwd(proj, W["conv_w"][l], P["conv_b"][l].reshape(1, CONV_DIM))
    S["pre"], S["act"] = pre, act

    dtb = jnp.pad(P["dt_bias"][l].reshape(1, 2 * SSD_HEADS), ((0, 0), (0, LANES - 2 * SSD_HEADS)))

    def dt_fn(raw, b):
        v = raw + b
        dt = jnp.maximum(v, 0.0) + jnp.log1p(jnp.exp(-jnp.abs(v)))
        expand = (jnp.right_shift(lax.broadcasted_iota(jnp.int32, (LANES, 2 * D_INNER), 1), 6)
                  == lax.broadcasted_iota(jnp.int32, (LANES, 2 * D_INNER), 0)).astype(BF16)
        hi = dt.astype(BF16)
        rest = dt - hi.astype(F32)
        mid = rest.astype(BF16)
        lo = (rest - mid.astype(F32)).astype(BF16)
        return [dt, _dot(hi, expand) + _dot(mid, expand) + _dot(lo, expand)], []

    dt, dte = _rowwise("dt_fwd", dt_fn, [_row(proj, LANES, OFF_DT // LANES)], [dtb], [(LANES, F32), (2 * D_INNER, F32)], tb=512)
    dt2 = jnp.stack([dt[:, 0:SSD_HEADS], dt[:, SSD_HEADS:2 * SSD_HEADS]])
    dt2t = dt2.transpose(0, 2, 1)
    a = -jnp.exp(P["a_log"][l])
    a_row, a_col = a.reshape(2, 1, SSD_HEADS), a.reshape(2, SSD_HEADS, 1)
    S["dt2"], S["dt2t"], S["a_row"], S["a_col"], S["dte"] = dt2, dt2t, a_row, a_col, dte
    y2, states = _ssd_fwd(act, dt2, dt2t, a_row, a_col, dte)
    S["states"] = states

    dsk = jnp.repeat(P["d_skip"][l], SSD_HEAD_DIM).reshape(1, D_INNER)
    nw = P["ssd_norm"][l].reshape(1, D_INNER)
    S["dsk"], S["nw"] = dsk, nw

    def gn_fn(yf, yb, xs, z, dsk, nw):
        y = yf + yb + xs * dsk
        return [y, _gated_norm_fwd(y, z, nw)], []

    y, yn = _rowwise("gated_norm_fwd", gn_fn,
                     [_row(y2, lead=0), _row(y2, lead=1), _row(act, D_INNER, 0), _row(proj, D_INNER, OFF_Z // D_INNER)],
                     [dsk, nw], [(D_INNER, F32), (D_INNER, BF16)])
    S["y"], S["yn"] = y, yn
    if "late" in W:
        W.update(W.pop("late")(yn))
    y_ssd = _matmul("ssd_out", yn, W["w_ssd_out"][l], "nn")
    S["y_ssd"] = y_ssd

    sink4 = jnp.broadcast_to(P["attn_sink"][l].reshape(ATTN_KV, REP, 1, 1), (ATTN_KV, REP, 1, LANES))
    S["qkv"], S["sink4"] = qkv, sink4
    o = _attn_fwd(qkv, bias4, sink4)
    S["o"] = o
    y_attn = _matmul("attn_out", o, W["w_attn_out"][l], "nn")
    S["y_attn"] = y_attn

    bg = P["b_gate"][l].reshape(1, 2 * D_MODEL)
    S["bg"] = bg

    def merge_fn(gates, ys, ya, b):
        g = jax.nn.sigmoid(gates + b)
        return [g[:, :D_MODEL] * ys + g[:, D_MODEL:] * ya], []

    (mix_in,) = _rowwise("merge_fwd", merge_fn, [_row(proj, 2 * D_MODEL, OFF_G // (2 * D_MODEL)), _row(y_ssd), _row(y_attn)],
                         [bg], [(D_MODEL, BF16)])
    S["mix_in"] = mix_in
    mixed = _matmul("w_o", mix_in, W["w_o"][l], "nn")
    S["mixed"] = mixed

    g_pm = P["post_mix_norm"][l].reshape(1, D_MODEL)
    g_pl = P["pre_mlp_norm"][l].reshape(1, D_MODEL)

    def postmix_fn(x, mixed, g1, g2):
        x2 = x + _rms_fwd(mixed, g1)
        return [x2, _rms_fwd(x2, g2)], []

    x2, h2 = _rowwise("post_mix_fwd", postmix_fn, [_row(x), _row(mixed)], [g_pm, g_pl], [(D_MODEL, F32), (D_MODEL, BF16)])
    S["x2"], S["h2"] = x2, h2
    a1 = _matmul("mlp_in", h2, W["w_mlp_in"][l], "nn", out_dtype=BF16,
                 epilogue=lambda acc: jnp.square(jnp.maximum(acc, 0.0)))
    S["a1"] = a1
    f2 = _matmul("mlp_out", a1, W["w_mlp_out"][l], "nn")
    S["f2"] = f2
    return S


def _layer_bwd(S, dx3, W, P, l, bias4, onehot_t, zero=None):
    t = dx3.shape[0]
    G = {}
    g_pmlp = P["post_mlp_norm"][l].reshape(1, D_MODEL)
    if zero is not None:
        g_pmlp = g_pmlp + zero

    def b1_fn(f2, dx3, g):
        df2, dg = _rms_bwd(f2, g, dx3)
        return [df2], [dg]

    df2, G["post_mlp_norm"] = _rowwise("post_mlp_bwd", b1_fn, [_row(S["f2"]), _row(dx3)], [g_pmlp], [(D_MODEL, BF16)], [(1, D_MODEL)])
    df1 = _matmul("d_f1", df2, W["w_mlp_out"][l], "nt", out_dtype=BF16,
                  epilogue=lambda acc, a1: acc * (2.0 * jnp.sqrt(a1.astype(F32))), extras=[S["a1"]])
    G["w_mlp_out"] = _matmul("dw_mlp_out", S["a1"], df2, "tn")
    dh2 = _matmul("d_h2", df1, W["w_mlp_in"][l], "nt")
    G["w_mlp_in"] = _matmul("dw_mlp_in", S["h2"], df1, "tn")

    g_pm = P["post_mix_norm"][l].reshape(1, D_MODEL)
    g_pl = P["pre_mlp_norm"][l].reshape(1, D_MODEL)

    def b3_fn(x2, dh2, dx3, mixed, g_pl, g_pm):
        d1, dgl = _rms_bwd(x2, g_pl, dh2)
        dx2 = dx3 + d1
        dmixed, dgm = _rms_bwd(mixed, g_pm, dx2)
        return [dx2, dmixed], [dgl, dgm]

    dx2, dmixed, G["pre_mlp_norm"], G["post_mix_norm"] = _rowwise(
        "post_mix_bwd", b3_fn, [_row(S["x2"]), _row(dh2), _row(dx3), _row(S["mixed"])], [g_pl, g_pm],
        [(D_MODEL, F32), (D_MODEL, BF16)], [(1, D_MODEL), (1, D_MODEL)])
    dmix_in = _matmul("d_mix_in", dmixed, W["w_o"][l], "nt")
    G["w_o"] = _matmul("dw_o", S["mix_in"], dmixed, "tn")

    proj = S["proj"]

    def b4_fn(gates, ys, ya, dmix, b):
        g = jax.nn.sigmoid(gates + b)
        gs, ga = g[:, :D_MODEL], g[:, D_MODEL:]
        dg = jnp.concatenate([ys * dmix, ya * dmix], axis=-1) * g * (1.0 - g)
        return [gs * dmix, ga * dmix, dg], [jnp.sum(dg, axis=0, keepdims=True)]

    dy_ssd, dy_attn, dgates, G["b_gate"] = _rowwise(
        "merge_bwd", b4_fn, [_row(proj, 2 * D_MODEL, OFF_G // (2 * D_MODEL)), _row(S["y_ssd"]), _row(S["y_attn"]), _row(dmix_in)],
        [S["bg"]], [(D_MODEL, BF16), (D_MODEL, BF16), (2 * D_MODEL, BF16)], [(1, 2 * D_MODEL)])

    dyn = _matmul("d_yn", dy_ssd, W["w_ssd_out"][l], "nt")
    G["w_ssd_out"] = _matmul("dw_ssd_out", S["yn"], dy_ssd, "tn")
    do = _matmul("d_o", dy_attn, W["w_attn_out"][l], "nt", out_dtype=BF16)
    G["w_attn_out"] = _matmul("dw_attn_out", S["o"], dy_attn, "tn")

    dq, dkp, dvp, dbias4, dsink4 = _attn_bwd(S["qkv"], bias4, S["sink4"], do)
    dk = _kv_combine("dk_combine", dkp)
    dv = _kv_combine("dv_combine", dvp)
    G["attn_sink"] = dsink4[:, :, 0, 0].reshape(ATTN_HEADS)
    G["rel_bias_table"] = _bias_table_grad(dbias4.reshape(ATTN_HEADS, BLOCK * 3 * BLOCK), onehot_t).T

    act = S["act"]

    def b5_fn(y, z, xs, dyn, nw, dsk):
        dy, dz, dnw = _gated_norm_bwd(y, z, nw, dyn)
        return [dy, dz], [dnw, jnp.sum(dy * xs, axis=0, keepdims=True)]

    dy, dz, G["ssd_norm"], dskip_cols = _rowwise(
        "gated_norm_bwd", b5_fn, [_row(S["y"]), _row(proj, D_INNER, OFF_Z // D_INNER), _row(act, D_INNER, 0), _row(dyn)],
        [S["nw"], S["dsk"]], [(D_INNER, F32), (D_INNER, BF16)], [(1, D_INNER), (1, D_INNER)])
    G["d_skip"] = dskip_cols.reshape(SSD_HEADS, SSD_HEAD_DIM).sum(axis=-1)

    dxs2, dbs2, dcs2, ddt2, da2 = _ssd_bwd(act, S["dt2"], S["dt2t"], S["a_row"], S["a_col"], S["dte"], dy, S["states"])
    G["a_log"] = da2.reshape(2, SSD_HEADS) * S["a_row"].reshape(2, SSD_HEADS)

    def b6_fn(dxf, dxb, dy, dbf, dbb, dcf, dcb, pre, dsk):
        dact = jnp.concatenate([dxf + dxb + dy * dsk, dbf + dbb, dcf + dcb], axis=-1)
        return [dact * _silu_grad(pre)], []

    (dpre,) = _rowwise("silu_bwd", b6_fn,
                       [_row(dxs2, lead=0), _row(dxs2, lead=1), _row(dy), _row(dbs2, lead=0), _row(dbs2, lead=1),
                        _row(dcs2, lead=0), _row(dcs2, lead=1), _row(S["pre"])], [S["dsk"]], [(CONV_DIM, F32)], tb=128)
    du, dconv_w, dconv_b = _conv_bwd(dpre, proj, W["conv_w"][l])
    G["conv_w"] = dconv_w.reshape(SSD_CONV, 1, CONV_DIM)
    G["conv_b"] = dconv_b.reshape(CONV_DIM)

    dtb = jnp.pad(P["dt_bias"][l].reshape(1, 2 * SSD_HEADS), ((0, 0), (0, LANES - 2 * SSD_HEADS)))
    ddt = jnp.pad(jnp.concatenate([ddt2[0], ddt2[1]], axis=-1), ((0, 0), (0, LANES - 2 * SSD_HEADS)))

    def b7_fn(raw, ddt, b):
        draw = ddt * jax.nn.sigmoid(raw + b)
        return [draw], [jnp.sum(draw, axis=0, keepdims=True)]

    draw, ddtb = _rowwise("dt_bwd", b7_fn, [_row(proj, LANES, OFF_DT // LANES), _row(ddt)], [dtb], [(LANES, BF16)], [(1, LANES)], tb=512)
    G["dt_bias"] = ddtb[0, :2 * SSD_HEADS].reshape(2, SSD_HEADS)

    dproj = jnp.concatenate([dz, dgates, du, draw, dq, dk, dv], axis=-1)
    G["w_in"] = _from_proj_layout(_matmul("dw_in", S["h1"], dproj, "tn"))
    dh1 = _matmul("d_h1", dproj, W["w_in"][l], "nt")

    g_pre = P["pre_mix_norm"][l].reshape(1, D_MODEL)

    def b8_fn(x, dh1, dx2, g):
        d1, dg = _rms_bwd(x, g, dh1)
        return [dx2 + d1], [dg]

    dx, G["pre_mix_norm"] = _rowwise("pre_mix_bwd", b8_fn, [_row(S["x"]), _row(dh1), _row(dx2)], [g_pre], [(D_MODEL, F32)], [(1, D_MODEL)])
    return dx, G


def _chip_sums(tag, pieces):
    theirs = _swap_cores("swap_" + tag, pieces)
    out = []
    for i, p in enumerate(pieces):
        _, _, r2, c = p.shape
        mine = lax.dynamic_index_in_dim(p, lax.axis_index("c"), axis=0, keepdims=False).reshape(4 * r2, c)
        (cs,) = _rowwise(f"presum_{tag}{i}", lambda a, b: ([a.astype(F32) + b.astype(F32)], []),
                         [_row(mine), _row(theirs[i].reshape(4 * r2, c))], [], [(c, BF16)])
        out.append(cs.reshape(4, r2, c))
    return out


def _halves(blocks):
    r2 = blocks[0].shape[0] // 2
    return jnp.stack([jnp.stack([b[h * r2:(h + 1) * r2] for b in blocks]) for h in range(2)])


def _layer_pieces(g, like):
    cols = like["w_in"].shape[2]
    win = _halves([g["w_in"][:, s * cols:(s + 1) * cols].astype(BF16) for s in range(4)])
    packs = [_pack_rows([_shard_of_full(n, g[n][None], s)[0] for n in BIG], BIG_ROWS // 2, BF16) for s in range(4)]
    return [win, _halves(packs)]


def _unpack_layer(flat, like):
    out, r = {}, 0
    for n in BIG:
        shp = like[n].shape[1:]
        nr = math.prod(shp) // D_MODEL
        out[n] = flat[r:r + nr].reshape(shp)
        r += nr
    return out


def _step(x, target, shard_w, shard_m, shard_v):
    depth, _, win_cols = shard_w["w_in"].shape
    win_rows = depth * D_MODEL

    def win2d(a):
        return a.reshape(win_rows, win_cols)

    conv_rows = shard_w["conv_w"].reshape(-1, D_MODEL)
    win_own = win2d(shard_w["w_in"]).astype(BF16)
    conv_own = jnp.pad(conv_rows, ((0, 16 - conv_rows.shape[0]), (0, 0)))
    win_g, conv_g = _allgather_chips("gather_w_in", [win_own, conv_own])
    packed = _pack_big(shard_w, BF16)
    send_sems, recv_sems, packed_thru, land_thru, token = _send_start("gather_rest_start", [packed], conv_g, False, 0)

    def late(after):
        (landed,) = _send_wait("gather_rest_wait", send_sems, recv_sems, packed_thru, land_thru, after, False)
        per_chip = [_unpack_big(jnp.where(me == s, packed, landed[s]), shard_w) for s in range(4)]
        return {n: _full_of_shards(n, [per_chip[s][n] for s in range(4)]) for n in BIG if n != "conv_w"}

    W = {"late": late}
    me = 2 * lax.axis_index("x") + lax.axis_index("y")
    win_g = [jnp.where(me == s, win_own, win_g[s]) for s in range(4)]
    conv_g = [jnp.where(me == s, conv_own, conv_g[s]) for s in range(4)]
    W["w_in"] = _to_proj_layout(jnp.concatenate([win_g[s].reshape(depth, D_MODEL, win_cols) for s in range(4)], axis=2))
    W["conv_w"] = jnp.concatenate([conv_g[s][:conv_rows.shape[0]].reshape(shard_w["conv_w"].shape) for s in range(4)],
                                  axis=3).reshape(depth, SSD_CONV, CONV_DIM)
    P = {n: shard_w[n] for n in SMALL}
    P["pre_mix_norm"] = P["pre_mix_norm"] + token[0, 0]
    assert depth == 2
    sent = {}

    def grads_ready(g):
        sent["sums"] = _chip_sums("l1_", _layer_pieces(g, shard_w))
        sent["sems"] = _send_start("scatter_l1_start", sent["sums"], sent["sums"][0], True, 1)
        return sent["sems"][4][0, 0]

    loss_part, grad_x, full = _local_step(x, target, W, P, grads_ready)
    send1, recv1, thru1, land1, _ = sent["sems"]
    landed1 = _send_wait("scatter_l1_wait", send1, recv1, thru1, land1, grad_x, True)
    sums0 = _chip_sums("l0_", _layer_pieces({n: full[n][0] for n in BIG + ("w_in",)}, shard_w))
    landed0 = _scatter_chips("scatter_grads", sums0)
    staged = [jnp.stack([jnp.where(me == s, own[s], got[s]) for s in range(4)])
              for own, got in zip(sums0 + sent["sums"], list(landed0) + landed1)]
    halves = [_sum_slots(f"sum_grads{i}", st) for i, st in enumerate(staged)]
    core = lax.axis_index("c")
    win0, big0, win1, big1 = [jnp.where(core == 0, jnp.concatenate([mine, other]), jnp.concatenate([other, mine]))
                              for mine, other in zip(halves, _swap_cores("share_grads", halves, slab=False))]
    g_win = jnp.concatenate([win0, win1], axis=0)
    per_layer = [_unpack_layer(big0, shard_w), _unpack_layer(big1, shard_w)]
    g_big = _pack_big({n: jnp.stack([per_layer[0][n], per_layer[1][n]]) for n in BIG}, F32)
    d_win, m_win, v_win = _adamw("adamw_w_in", win2d(shard_w["w_in"]), g_win, win2d(shard_m["w_in"]), win2d(shard_v["w_in"]))
    d_big, m_big, v_big = _adamw("adamw_big", _pack_big(shard_w, F32), g_big, _pack_big(shard_m, F32), _pack_big(shard_v, F32))

    small = _allgather_all("gather_small", _pack_small(full, loss_part))
    g_small = _sum_slots("sum_small", small, tb=SMALL_ROWS)
    d_small, m_small, v_small = _adamw("adamw_small", _pack_small(shard_w, jnp.zeros((), F32)), g_small,
                                       _pack_small(shard_m, jnp.zeros((), F32)), _pack_small(shard_v, jnp.zeros((), F32)))

    outs = {}
    for tag, win, big, sm in (("grad", g_win, g_big, g_small), ("delta", d_win, d_big, d_small),
                              ("new_m", m_win, m_big, m_small), ("new_v", v_win, v_big, v_small)):
        ub = _unpack_big(big, shard_w)
        us, extra = _unpack_small(sm, shard_w)
        outs[tag] = {"w_in": win.reshape(shard_w["w_in"].shape), **ub, **us}
        if tag == "grad":
            loss = extra
    return loss, grad_x, outs


def _local_step(x, target, W, P, grads_ready=None):
    depth = W["w_in"].shape[0]
    W = dict(W, w_in_main=W["w_in"][:, :, :N_MAIN], w_in_qkv=W["w_in"][:, :, N_MAIN:])
    onehot_t = (_bucket_map().reshape(1, -1) == jnp.arange(N_BUCKETS)[:, None]).astype(F32)
    bias = _bias_from_table(P["rel_bias_table"], onehot_t).reshape(ATTN_HEADS, BLOCK, 3 * BLOCK)
    bias4 = jnp.where(_band_mask(), bias, NEG).reshape(ATTN_KV, REP, BLOCK, 3 * BLOCK)

    def pre_fn(x, g):
        return [_rms_fwd(x, g)], []

    (h1,) = _rowwise("pre_mix_fwd", pre_fn, [_row(x)], [P["pre_mix_norm"][0].reshape(1, D_MODEL)], [(D_MODEL, BF16)])
    saved = []
    loss_cols = dxl = None
    for l in range(depth):
        S = _layer_fwd(h1, x, W, P, l, bias4)
        saved.append(S)
        g_pmlp = P["post_mlp_norm"][l].reshape(1, D_MODEL)
        if l + 1 < depth:
            def post_fn(x2, f2, g1, g2):
                x3 = x2 + _rms_fwd(f2, g1)
                return [x3, _rms_fwd(x3, g2)], []

            x, h1 = _rowwise("post_mlp_fwd", post_fn, [_row(S["x2"]), _row(S["f2"])],
                             [g_pmlp, P["pre_mix_norm"][l + 1].reshape(1, D_MODEL)], [(D_MODEL, F32), (D_MODEL, BF16)])
        else:
            def loss_fn(x2, f2, tgt, g1):
                diff = x2 + _rms_fwd(f2, g1) - tgt
                return [diff * (1.0 / D_MODEL)], [jnp.sum(diff * diff, axis=0, keepdims=True)]

            dxl, loss_cols = _rowwise("loss", loss_fn, [_row(S["x2"]), _row(S["f2"]), _row(target)], [g_pmlp],
                                      [(D_MODEL, F32)], [(1, D_MODEL)])
    loss_part = 0.5 * jnp.sum(loss_cols) / D_MODEL

    grads = [None] * depth
    dx = dxl
    zero = None
    for l in reversed(range(depth)):
        dx, grads[l] = _layer_bwd(saved[l], dx, W, P, l, bias4, onehot_t, zero)
        if grads_ready is not None and l == depth - 1 and depth > 1:
            zero = grads_ready(grads[l])
    grad_x = dx

    full = {n: jnp.stack([grads[l][n] for l in range(depth)]) for n in ALL_W if n != "rel_bias_table"}
    full["rel_bias_table"] = sum(grads[l]["rel_bias_table"] for l in range(depth))
    return loss_part, grad_x, full


def kernel(x, pre_mix_norm, w_in, b_gate, conv_w, conv_b, dt_bias, a_log, d_skip, ssd_norm, w_ssd_out, attn_sink, rel_bias_table, w_attn_out, w_o, post_mix_norm, pre_mlp_norm, w_mlp_in, w_mlp_out, post_mlp_norm, loss_target, m_pre_mix_norm, m_w_in, m_b_gate, m_conv_w, m_conv_b, m_dt_bias, m_a_log, m_d_skip, m_ssd_norm, m_w_ssd_out, m_attn_sink, m_rel_bias_table, m_w_attn_out, m_w_o, m_post_mix_norm, m_pre_mlp_norm, m_w_mlp_in, m_w_mlp_out, m_post_mlp_norm, v_pre_mix_norm, v_w_in, v_b_gate, v_conv_w, v_conv_b, v_dt_bias, v_a_log, v_d_skip, v_ssd_norm, v_w_ssd_out, v_attn_sink, v_rel_bias_table, v_w_attn_out, v_w_o, v_post_mix_norm, v_pre_mlp_norm, v_w_mlp_in, v_w_mlp_out, v_post_mlp_norm):
    a = locals()
    shard_w = {n: a[n] for n in ALL_W}
    shard_m = {n: a["m_" + n] for n in ALL_W}
    shard_v = {n: a["v_" + n] for n in ALL_W}
    loss, grad_x, outs = _step(x[0], loss_target[0], shard_w, shard_m, shard_v)
    return (loss, grad_x[None], *[outs["grad"][n] for n in ALL_W], *[outs["delta"][n] for n in ALL_W],
            *[outs["new_m"][n] for n in ALL_W], *[outs["new_v"][n] for n in ALL_W])
```

```python
import math

import jax
import jax.numpy as jnp
from jax import lax
from jax.experimental import pallas as pl
from jax.experimental.pallas import tpu as pltpu

F32 = jnp.float32
BF16 = jnp.bfloat16
MESH = pl.DeviceIdType.MESH

VMEM_LIMIT_BYTES = 52 * 1024 * 1024
LANES = 128
SUBLANES = 8

EPS = 1e-6
D_MODEL = 1024
D_INNER = 2048
SSD_HEADS = 32
SSD_HEAD_DIM = 64
SSD_GROUPS = 8
SSD_STATE = 128
SSD_CONV = 5
CHUNK = 128
CONV_DIM = 4096
ATTN_HEADS = 16
ATTN_KV = 4
ATTN_DIM = 64
BLOCK = 128
N_BUCKETS = 32
MAX_DISTANCE = 128
D_FF = 4096
N_IN = 9792
NEG = -1e30

OFF_Z, OFF_G, OFF_XBC, OFF_DT, N_MAIN, N_PROJ = 0, 2048, 4096, 8192, 8320, 10368
N_QKV = N_PROJ - N_MAIN

ADAM_LR, ADAM_B1, ADAM_B2, ADAM_EPS, ADAM_WD, ADAM_STEP = 0.001, 0.9, 0.999, 1e-08, 0.01, 10

BIG_ROWS = 6656
SMALL_ROWS = 32


def _cparams(*sem):
    return pltpu.CompilerParams(dimension_semantics=sem, vmem_limit_bytes=VMEM_LIMIT_BYTES)


def _dot(a, b, precision=None):
    return lax.dot_general(a, b, (((1,), (0,)), ((), ())), preferred_element_type=F32, precision=precision)


def _dot_nt(a, b, precision=None):
    return lax.dot_general(a, b, (((1,), (1,)), ((), ())), preferred_element_type=F32, precision=precision)


def _dot_tn(a, b):
    return lax.dot_general(a, b, (((0,), (0,)), ((), ())), preferred_element_type=F32)


def _pick(n, prefs):
    for p in prefs:
        if n % p == 0:
            return p
    return n


def _matmul(name, a, b, mode, out_dtype=F32, epilogue=None, extras=()):
    if mode == "nn":
        (m, k), (_, n) = a.shape, b.shape
    elif mode == "nt":
        (m, k), (n, _) = a.shape, b.shape
    else:
        (k, m), (_, n) = a.shape, b.shape
    tm = _pick(m, (512, 256, 128)) if mode == "tn" else _pick(m, (1024, 512, 256, 128))
    tn = _pick(n, (1024, 1152, 640, 512, 256, 128))
    tk = _pick(k, (4096, 2048, 1024, 3456, 512, 256, 128)) if mode != "tn" else _pick(k, (2048, 512, 256, 128))
    if tk > 2048:
        tm = _pick(m, (512, 256, 128))
    nk = k // tk
    if mode == "nn":
        a_spec = pl.BlockSpec((tm, tk), lambda i, j, q: (i, q))
        b_spec = pl.BlockSpec((tk, tn), lambda i, j, q: (q, j))
        fn = _dot
    elif mode == "nt":
        a_spec = pl.BlockSpec((tm, tk), lambda i, j, q: (i, q))
        b_spec = pl.BlockSpec((tn, tk), lambda i, j, q: (j, q))
        fn = _dot_nt
    else:
        a_spec = pl.BlockSpec((tk, tm), lambda i, j, q: (q, i))
        b_spec = pl.BlockSpec((tk, tn), lambda i, j, q: (q, j))
        fn = _dot_tn

    tile = pl.BlockSpec((tm, tn), lambda i, j, q: (i, j))
    n_ex = len(extras)

    def body(a_ref, b_ref, *rest):
        ex_refs, o_ref = rest[:n_ex], rest[n_ex]

        def store(acc):
            v = acc if epilogue is None else epilogue(acc, *[r[...] for r in ex_refs])
            o_ref[...] = v.astype(o_ref.dtype)

        p = fn(a_ref[...].astype(BF16), b_ref[...].astype(BF16))
        if nk == 1:
            store(p)
        else:
            acc_ref = rest[n_ex + 1]
            q = pl.program_id(2)

            @pl.when(q == 0)
            def _():
                acc_ref[...] = p

            @pl.when((q > 0) & (q < nk - 1))
            def _():
                acc_ref[...] += p

            @pl.when(q == nk - 1)
            def _():
                store(acc_ref[...] + p)

    return pl.pallas_call(
        body, name=name, grid=(m // tm, n // tn, nk),
        in_specs=[a_spec, b_spec] + [tile] * n_ex, out_specs=tile,
        out_shape=jax.ShapeDtypeStruct((m, n), out_dtype),
        scratch_shapes=[pltpu.VMEM((tm, tn), F32)] if nk > 1 else [],
        compiler_params=_cparams("parallel", "parallel", "arbitrary"),
    )(a, b, *extras)


def _row(arr, width=None, cb=0, lead=None):
    return (arr, width, cb, lead)


def _rowwise(name, fn, rows, vecs, outs, accs=(), tb=256):
    t = rows[0][0].shape[-2]
    tb = min(tb, t)
    in_specs, args = [], []
    for arr, width, cb, lead in rows:
        w = arr.shape[-1] if width is None else width
        if lead is None:
            in_specs.append(pl.BlockSpec((tb, w), lambda i, cb=cb: (i, cb)))
        else:
            in_specs.append(pl.BlockSpec((None, tb, w), lambda i, cb=cb, lead=lead: (lead, i, cb)))
        args.append(arr)
    for v in vecs:
        in_specs.append(pl.BlockSpec(v.shape, lambda i, nd=v.ndim: (0,) * nd))
        args.append(v)
    out_shape = [jax.ShapeDtypeStruct((t, c), dt) for c, dt in outs] + [jax.ShapeDtypeStruct(s, F32) for s in accs]
    out_specs = [pl.BlockSpec((tb, c), lambda i: (i, 0)) for c, _ in outs] + [pl.BlockSpec(s, lambda i: (0, 0)) for s in accs]
    n_in, n_out = len(args), len(outs)

    def body(*refs):
        vals = [r[...] for r in refs[:n_in]]
        o_vals, a_vals = fn(*vals)
        for r, v in zip(refs[n_in:n_in + n_out], o_vals):
            r[...] = v.astype(r.dtype)
        first = pl.program_id(0) == 0
        for r, v in zip(refs[n_in + n_out:], a_vals):
            @pl.when(first)
            def _(r=r, v=v):
                r[...] = v

            @pl.when(jnp.logical_not(first))
            def _(r=r, v=v):
                r[...] += v

    res = pl.pallas_call(
        body, name=name, grid=(t // tb,), in_specs=in_specs, out_specs=out_specs, out_shape=out_shape,
        compiler_params=_cparams("arbitrary"),
    )(*args)
    return res


def _rms_fwd(x, g):
    r = lax.rsqrt(jnp.mean(x * x, axis=-1, keepdims=True) + EPS)
    return x * r * g


def _rms_bwd(x, g, dy):
    r = lax.rsqrt(jnp.mean(x * x, axis=-1, keepdims=True) + EPS)
    xh = x * r
    dxh = dy * g
    dx = r * (dxh - xh * jnp.mean(dxh * xh, axis=-1, keepdims=True))
    return dx, jnp.sum(dy * xh, axis=0, keepdims=True)


def _silu(x):
    return x * jax.nn.sigmoid(x)


def _silu_grad(x):
    s = jax.nn.sigmoid(x)
    return s * (1.0 + x * (1.0 - s))


GROUP_W = D_INNER // SSD_GROUPS


def _gated_norm_fwd(y, z, w):
    u = y * _silu(z)
    parts = []
    for j in range(SSD_GROUPS):
        ug = u[:, j * GROUP_W:(j + 1) * GROUP_W]
        parts.append(ug * lax.rsqrt(jnp.mean(ug * ug, axis=-1, keepdims=True) + EPS))
    return jnp.concatenate(parts, axis=-1) * w


def _gated_norm_bwd(y, z, w, dyn):
    sz = _silu(z)
    u = y * sz
    duh = dyn * w
    du_parts, uh_parts = [], []
    for j in range(SSD_GROUPS):
        sl = slice(j * GROUP_W, (j + 1) * GROUP_W)
        ug = u[:, sl]
        r = lax.rsqrt(jnp.mean(ug * ug, axis=-1, keepdims=True) + EPS)
        uh = ug * r
        dg = duh[:, sl]
        du_parts.append(r * (dg - uh * jnp.mean(dg * uh, axis=-1, keepdims=True)))
        uh_parts.append(uh)
    du = jnp.concatenate(du_parts, axis=-1)
    uh = jnp.concatenate(uh_parts, axis=-1)
    dw = jnp.sum(dyn * uh, axis=0, keepdims=True)
    return du * sz, du * y * _silu_grad(z), dw


HALO = SUBLANES


def _halo_specs(tb, cb, col0, t):
    nblk8 = t // HALO
    per = tb // HALO
    main = pl.BlockSpec((tb, cb), lambda j, i: (i, col0 + j))
    prev = pl.BlockSpec((HALO, cb), lambda j, i: (jnp.maximum(i * per - 1, 0), col0 + j))
    nxt = pl.BlockSpec((HALO, cb), lambda j, i: (jnp.minimum((i + 1) * per, nblk8 - 1), col0 + j))
    return main, prev, nxt


def _fill_ext(ext_ref, cur_ref, prev_ref, next_ref, tb, ni):
    i = pl.program_id(1)
    ext_ref[0:HALO, :] = jnp.where(i > 0, prev_ref[...], 0.0)
    ext_ref[HALO:HALO + tb, :] = cur_ref[...]
    ext_ref[HALO + tb:HALO + tb + HALO, :] = jnp.where(i < ni - 1, next_ref[...], 0.0)


def _conv_fwd(proj, w, b):
    t = proj.shape[0]
    tb, cb = min(512, t), 512
    ni, nj = t // tb, CONV_DIM // cb
    main, prev, nxt = _halo_specs(tb, cb, OFF_XBC // cb, t)
    pad = (SSD_CONV - 1) // 2

    def body(u_ref, up_ref, un_ref, w_ref, b_ref, pre_ref, act_ref, ext_ref):
        _fill_ext(ext_ref, u_ref, up_ref, un_ref, tb, ni)
        acc = jnp.broadcast_to(b_ref[...], (tb, cb))
        for k in range(SSD_CONV):
            acc = acc + w_ref[k:k + 1, :] * ext_ref[pl.ds(HALO + k - pad, tb), :]
        pre_ref[...] = acc
        act_ref[...] = _silu(acc)

    out = pl.BlockSpec((tb, cb), lambda j, i: (i, j))
    return pl.pallas_call(
        body, name="conv_fwd", grid=(nj, ni),
        in_specs=[main, prev, nxt, pl.BlockSpec((SSD_CONV, cb), lambda j, i: (0, j)), pl.BlockSpec((1, cb), lambda j, i: (0, j))],
        out_specs=[out, out],
        out_shape=[jax.ShapeDtypeStruct((t, CONV_DIM), F32)] * 2,
        scratch_shapes=[pltpu.VMEM((tb + 2 * HALO, cb), F32)],
        compiler_params=_cparams("parallel", "arbitrary"),
    )(proj, proj, proj, w, b)


def _conv_bwd(dpre, proj, w):
    t = proj.shape[0]
    tb, cb = min(512, t), 512
    ni, nj = t // tb, CONV_DIM // cb
    umain, uprev, unext = _halo_specs(tb, cb, OFF_XBC // cb, t)
    dmain, dprev, dnext = _halo_specs(tb, cb, 0, t)
    pad = (SSD_CONV - 1) // 2

    def body(d_ref, dp_ref, dn_ref, u_ref, up_ref, un_ref, w_ref, du_ref, dw_ref, db_ref, extd_ref, extu_ref):
        _fill_ext(extd_ref, d_ref, dp_ref, dn_ref, tb, ni)
        _fill_ext(extu_ref, u_ref, up_ref, un_ref, tb, ni)
        d = d_ref[...]
        du = jnp.zeros((tb, cb), F32)
        @pl.when(pl.program_id(1) == 0)
        def _():
            dw_ref[...] = jnp.zeros_like(dw_ref)
            db_ref[...] = jnp.zeros_like(db_ref)

        for k in range(SSD_CONV):
            du = du + w_ref[k:k + 1, :] * extd_ref[pl.ds(HALO - k + pad, tb), :]
            dw_ref[k:k + 1, :] += jnp.sum(d * extu_ref[pl.ds(HALO + k - pad, tb), :], axis=0, keepdims=True)
        du_ref[...] = du.astype(du_ref.dtype)
        db_ref[...] += jnp.sum(d, axis=0, keepdims=True)

    return pl.pallas_call(
        body, name="conv_bwd", grid=(nj, ni),
        in_specs=[dmain, dprev, dnext, umain, uprev, unext, pl.BlockSpec((SSD_CONV, cb), lambda j, i: (0, j))],
        out_specs=[pl.BlockSpec((tb, cb), lambda j, i: (i, j)), pl.BlockSpec((SSD_CONV, cb), lambda j, i: (0, j)),
                   pl.BlockSpec((1, cb), lambda j, i: (0, j))],
        out_shape=[jax.ShapeDtypeStruct((t, CONV_DIM), BF16), jax.ShapeDtypeStruct((SSD_CONV, CONV_DIM), F32),
                   jax.ShapeDtypeStruct((1, CONV_DIM), F32)],
        scratch_shapes=[pltpu.VMEM((tb + 2 * HALO, cb), F32)] * 2,
        compiler_params=_cparams("parallel", "arbitrary"),
    )(dpre, dpre, dpre, proj, proj, proj, w)


PAIR = 2 * SSD_HEAD_DIM
HI = lax.Precision.HIGHEST


def _ssd_prelude(d, dt_ref, dtt_ref, ar_ref, ac_ref):
    li = lax.broadcasted_iota(jnp.int32, (CHUNK, CHUNK), 0)
    si = lax.broadcasted_iota(jnp.int32, (CHUNK, CHUNK), 1)
    fwd = d == 0
    hi, lo = jnp.where(fwd, li, si), jnp.where(fwd, si, li)
    tri = hi >= lo
    trif = tri.astype(F32)
    trit = (hi <= lo).astype(F32)
    dt = dt_ref[...]
    adt = dt * ar_ref[...]
    adtt = dtt_ref[...] * ac_ref[...]
    p = _dot(trif, adt, HI)
    pt = _dot_nt(adtt, trif, HI)
    tot = jnp.sum(adt, axis=0, keepdims=True)
    return tri, trit, dt, p, pt, tot


def _ssd_specs(nc, rev):
    def cidx(d, c):
        up = (d == 1) if rev else (d == 0)
        return jnp.where(up, c, nc - 1 - c)

    specs = [
        pl.BlockSpec((CHUNK, D_INNER), lambda d, c: (cidx(d, c), 0)),
        pl.BlockSpec((CHUNK, 1024), lambda d, c: (cidx(d, c), 2)),
        pl.BlockSpec((CHUNK, 1024), lambda d, c: (cidx(d, c), 3)),
        pl.BlockSpec((None, CHUNK, SSD_HEADS), lambda d, c: (d, cidx(d, c), 0)),
        pl.BlockSpec((None, SSD_HEADS, CHUNK), lambda d, c: (d, 0, cidx(d, c))),
        pl.BlockSpec((None, 1, SSD_HEADS), lambda d, c: (d, 0, 0)),
        pl.BlockSpec((None, SSD_HEADS, 1), lambda d, c: (d, 0, 0)),
        pl.BlockSpec((CHUNK, D_INNER), lambda d, c: (cidx(d, c), d)),
    ]
    return cidx, specs


def _head_decay(tri, p, pt, tot, h):
    pb = jnp.broadcast_to(p[:, h:h + 1], (CHUNK, CHUNK))
    dec = jnp.exp(jnp.where(tri, pb - pt[h:h + 1, :], NEG))
    return dec, jnp.exp(tot[:, h:h + 1] - pb), jnp.exp(pb)


def _ssd_fwd(act, dt2, dt2t, a_row, a_col, dte):
    t = act.shape[0]
    nc = t // CHUNK
    cidx, specs = _ssd_specs(nc, rev=False)

    def body(xs_ref, bs_ref, cs_ref, dt_ref, dtt_ref, ar_ref, ac_ref, dte_ref, y_ref, st_ref, h_ref):
        d, c = pl.program_id(0), pl.program_id(1)

        @pl.when(c == 0)
        def _():
            h_ref[...] = jnp.zeros_like(h_ref)

        st_ref[...] = h_ref[...]
        tri, _, _, p, pt, tot = _ssd_prelude(d, dt_ref, dtt_ref, ar_ref, ac_ref)
        etot = jnp.exp(tot)
        lane = lax.broadcasted_iota(jnp.int32, (CHUNK, PAIR), 1) < SSD_HEAD_DIM
        rowh = lax.broadcasted_iota(jnp.int32, (PAIR, SSD_STATE), 0) < SSD_HEAD_DIM
        for g in range(SSD_GROUPS):
            gs = slice(g * SSD_STATE, (g + 1) * SSD_STATE)
            bg = bs_ref[:, gs]
            cb = cs_ref[:, gs].astype(BF16)
            cbm = _dot_nt(cb, bg.astype(BF16))
            for pr in range(2):
                h0 = g * 4 + pr * 2
                h1 = h0 + 1
                sl = slice(h0 * SSD_HEAD_DIM, h0 * SSD_HEAD_DIM + PAIR)
                xdt = (xs_ref[:, sl] * dte_ref[:, sl]).astype(BF16)
                yd, st, epb = [], [], []
                for h in (h0, h1):
                    dec, wb, eb = _head_decay(tri, p, pt, tot, h)
                    yd.append(_dot((cbm * dec).astype(BF16), xdt))
                    st.append(_dot_tn(xdt, (bg * wb).astype(BF16)))
                    epb.append(eb)
                hin = h_ref[sl, :]
                yo = _dot_nt(cb, hin.astype(BF16)) * jnp.where(lane, epb[0], epb[1])
                y_ref[:, sl] = jnp.where(lane, yd[0], yd[1]) + yo
                et = jnp.where(rowh, etot[:, h0:h0 + 1], etot[:, h1:h1 + 1])
                h_ref[sl, :] = hin * et + jnp.where(rowh, st[0], st[1])

    return pl.pallas_call(
        body, name="ssd_fwd", grid=(2, nc), in_specs=specs,
        out_specs=[pl.BlockSpec((None, CHUNK, D_INNER), lambda d, c: (d, cidx(d, c), 0)),
                   pl.BlockSpec((None, None, D_INNER, SSD_STATE), lambda d, c: (d, cidx(d, c), 0, 0))],
        out_shape=[jax.ShapeDtypeStruct((2, t, D_INNER), F32), jax.ShapeDtypeStruct((2, nc, D_INNER, SSD_STATE), F32)],
        scratch_shapes=[pltpu.VMEM((D_INNER, SSD_STATE), F32)],
        compiler_params=_cparams("arbitrary", "arbitrary"),
    )(act, act, act, dt2, dt2t, a_row, a_col, dte)


def _put_col(acc, col, h):
    lane = lax.broadcasted_iota(jnp.int32, acc.shape, 1)
    return jnp.where(lane == h, col, acc)


def _put_row(acc, row, h):
    sub = lax.broadcasted_iota(jnp.int32, acc.shape, 0)
    return jnp.where(sub == h, row, acc)


def _sum_all(x):
    return jnp.sum(jnp.sum(x, axis=0, keepdims=True), axis=1, keepdims=True)


def _ssd_bwd(act, dt2, dt2t, a_row, a_col, dte, dy, states):
    t = act.shape[0]
    nc = t // CHUNK
    cidx, specs = _ssd_specs(nc, rev=True)
    specs = specs + [
        pl.BlockSpec((CHUNK, D_INNER), lambda d, c: (cidx(d, c), 0)),
        pl.BlockSpec((None, None, D_INNER, SSD_STATE), lambda d, c: (d, cidx(d, c), 0, 0)),
    ]

    def body(xs_ref, bs_ref, cs_ref, dt_ref, dtt_ref, ar_ref, ac_ref, dte_ref, dy_ref, st_ref,
             dxs_ref, dbs_ref, dcs_ref, ddt_ref, da_ref, dh_ref):
        d, c = pl.program_id(0), pl.program_id(1)

        @pl.when(c == 0)
        def _():
            dh_ref[...] = jnp.zeros_like(dh_ref)
            da_ref[...] = jnp.zeros_like(da_ref)

        tri, trit, dt, p, pt, tot = _ssd_prelude(d, dt_ref, dtt_ref, ar_ref, ac_ref)
        etot = jnp.exp(tot)
        lane = lax.broadcasted_iota(jnp.int32, (CHUNK, PAIR), 1) < SSD_HEAD_DIM
        rowh = lax.broadcasted_iota(jnp.int32, (PAIR, SSD_STATE), 0) < SSD_HEAD_DIM
        first_head = lax.broadcasted_iota(jnp.int32, (PAIR, LANES), 0) < SSD_HEAD_DIM
        out_lane = lax.broadcasted_iota(jnp.int32, (PAIR, LANES), 1)
        ddtx = jnp.zeros((CHUNK, LANES), F32)
        lane32 = lax.broadcasted_iota(jnp.int32, (CHUNK, SSD_HEADS), 1)
        dp_col = jnp.zeros((CHUNK, SSD_HEADS), F32)
        dp_row = jnp.zeros((SSD_HEADS, CHUNK), F32)
        dtot = jnp.zeros((1, SSD_HEADS), F32)
        for g in range(SSD_GROUPS):
            gs = slice(g * SSD_STATE, (g + 1) * SSD_STATE)
            bg = bs_ref[:, gs]
            bb = bg.astype(BF16)
            cb = cs_ref[:, gs].astype(BF16)
            cbm = _dot_nt(cb, bb)
            dcb = jnp.zeros((CHUNK, CHUNK), F32)
            dc_acc = jnp.zeros((CHUNK, SSD_STATE), F32)
            db_acc = jnp.zeros((CHUNK, SSD_STATE), F32)
            for pr in range(2):
                h0 = g * 4 + pr * 2
                h1 = h0 + 1
                sl = slice(h0 * SSD_HEAD_DIM, h0 * SSD_HEAD_DIM + PAIR)
                xp = xs_ref[:, sl]
                dtp = dte_ref[:, sl]
                xdt_f = xp * dtp
                xdt = xdt_f.astype(BF16)
                dyp = dy_ref[:, sl]
                dyb = dyp.astype(BF16)
                hin = st_ref[sl, :]
                dh = dh_ref[sl, :]
                hb = hin.astype(BF16)
                dhb = dh.astype(BF16)
                heads = [_head_decay(tri, p, pt, tot, h) for h in (h0, h1)]
                dye = dyp * jnp.where(lane, heads[0][2], heads[1][2])
                dyeb = dye.astype(BF16)
                gy = _dot_nt(cb, hb) * dye
                dc_acc = dc_acc + _dot(dyeb, hb)
                dhin = _dot_tn(dyeb, cb)
                hh = dh * hin
                dxdt = jnp.zeros((CHUNK, PAIR), F32)
                for idx, h in enumerate((h0, h1)):
                    hm = lane if idx == 0 else jnp.logical_not(lane)
                    rm = rowh if idx == 0 else jnp.logical_not(rowh)
                    dec, wb, _ = heads[idx]
                    mf = cbm * dec
                    t1 = _dot_tn(mf.astype(BF16), dyb)
                    t2 = _dot_nt((bg * wb).astype(BF16), dhb)
                    dxdt = jnp.where(hm, t1 + t2, dxdt)
                    dm = _dot_nt(jnp.where(hm, dyp, 0.0).astype(BF16), xdt)
                    dcb = dcb + dm * dec
                    e = dm * mf
                    qw = _dot(jnp.where(hm, xdt_f, 0.0).astype(BF16), dhb) * wb
                    db_acc = db_acc + qw
                    qwb = qw * bg
                    col = jnp.sum(e + jnp.where(hm, gy, 0.0) - qwb, axis=1, keepdims=True)
                    dp_col = jnp.where(lane32 == h, col, dp_col)
                    dp_row = _put_row(dp_row, -jnp.sum(e, axis=0, keepdims=True), h)
                    dtot_h = _sum_all(qwb) + etot[:, h:h + 1] * _sum_all(jnp.where(rm, hh, 0.0))
                    dtot = _put_col(dtot, dtot_h, h)
                dxs_ref[:, sl] = dxdt * dtp
                ddx = dxdt * xp
                ddx_hi = ddx.astype(BF16)
                ddx_lo = (ddx - ddx_hi.astype(F32)).astype(BF16)
                route = (out_lane == jnp.where(first_head, h0, h1)).astype(BF16)
                ddtx = ddtx + _dot(ddx_hi, route) + _dot(ddx_lo, route)
                et = jnp.where(rowh, etot[:, h0:h0 + 1], etot[:, h1:h1 + 1])
                dh_ref[sl, :] = dh * et + dhin
            dcbb = dcb.astype(BF16)
            dcs_ref[:, gs] = _dot(dcbb, bb) + dc_acc
            dbs_ref[:, gs] = _dot_tn(dcbb, cb) + db_acc
        d_adt = _dot(trit, dp_col, HI) + _dot_nt(trit, dp_row, HI) + dtot
        ddt_ref[...] = ddtx[:, :SSD_HEADS] + ar_ref[...] * d_adt
        da_ref[...] += jnp.sum(dt * d_adt, axis=0, keepdims=True)

    return pl.pallas_call(
        body, name="ssd_bwd", grid=(2, nc), in_specs=specs,
        out_specs=[pl.BlockSpec((None, CHUNK, D_INNER), lambda d, c: (d, cidx(d, c), 0)),
                   pl.BlockSpec((None, CHUNK, 1024), lambda d, c: (d, cidx(d, c), 0)),
                   pl.BlockSpec((None, CHUNK, 1024), lambda d, c: (d, cidx(d, c), 0)),
                   pl.BlockSpec((None, CHUNK, SSD_HEADS), lambda d, c: (d, cidx(d, c), 0)),
                   pl.BlockSpec((None, 1, SSD_HEADS), lambda d, c: (d, 0, 0))],
        out_shape=[jax.ShapeDtypeStruct((2, t, D_INNER), F32), jax.ShapeDtypeStruct((2, t, 1024), F32),
                   jax.ShapeDtypeStruct((2, t, 1024), F32), jax.ShapeDtypeStruct((2, t, SSD_HEADS), F32),
                   jax.ShapeDtypeStruct((2, 1, SSD_HEADS), F32)],
        scratch_shapes=[pltpu.VMEM((D_INNER, SSD_STATE), F32)],
        compiler_params=_cparams("arbitrary", "arbitrary"),
    )(act, act, act, dt2, dt2t, a_row, a_col, dte, dy, states)


REP = ATTN_HEADS // ATTN_KV
SCALE = ATTN_DIM ** -0.5
GROUP_Q = REP * ATTN_DIM
K_BLK0 = D_MODEL // LANES
V_BLK0 = K_BLK0 + ATTN_KV


def _attn_specs(nb):
    q = pl.BlockSpec((BLOCK, GROUP_Q), lambda g, n: (n, g))

    def kv(blk0):
        return [pl.BlockSpec((BLOCK, LANES), lambda g, n: (jnp.maximum(n - 1, 0), blk0 + g)),
                pl.BlockSpec((BLOCK, LANES), lambda g, n: (n, blk0 + g)),
                pl.BlockSpec((BLOCK, LANES), lambda g, n: (jnp.minimum(n + 1, nb - 1), blk0 + g))]

    bias = pl.BlockSpec((None, REP, BLOCK, 3 * BLOCK), lambda g, n: (g, 0, 0, 0))
    sink = pl.BlockSpec((None, REP, 1, LANES), lambda g, n: (g, 0, 0, 0))
    return q, kv(K_BLK0), kv(V_BLK0), bias, sink


def _band_mask():
    ii = lax.broadcasted_iota(jnp.int32, (BLOCK, 3 * BLOCK), 0)
    jj = lax.broadcasted_iota(jnp.int32, (BLOCK, 3 * BLOCK), 1)
    return (jj >= ii) & (jj - 2 * BLOCK <= ii)


def _attn_valid(n, nb):
    jj = lax.broadcasted_iota(jnp.int32, (1, 3 * BLOCK), 1)
    return ((jj >= BLOCK) | (n > 0)) & ((jj < 2 * BLOCK) | (n < nb - 1))


def _attn_probs(q, kcat, bias, snk, valid):
    s = jnp.where(valid, _dot_nt(q, kcat) + bias, NEG)
    m = jnp.maximum(jnp.max(s, axis=1, keepdims=True), snk)
    p = jnp.exp(s - m)
    es = jnp.exp(snk - m)
    r = 1.0 / (jnp.sum(p, axis=1, keepdims=True) + es)
    return p * r, es * r


def _stack_heads(ref, lane):
    parts = []
    for pr in range(REP // 2):
        tile = ref[:, pr * LANES:(pr + 1) * LANES]
        parts += [jnp.where(lane, tile, 0.0), jnp.where(lane, 0.0, tile)]
    return jnp.concatenate(parts, axis=0)


def _unstack_heads(x, lane):
    return jnp.concatenate([jnp.where(lane, x[(2 * pr) * BLOCK:(2 * pr + 1) * BLOCK], x[(2 * pr + 1) * BLOCK:(2 * pr + 2) * BLOCK])
                            for pr in range(REP // 2)], axis=1)


def _stack_bias(b_ref, s_ref):
    bias = jnp.concatenate([b_ref[r] for r in range(REP)], axis=0)
    snk = jnp.concatenate([jnp.broadcast_to(s_ref[r][:, 0:1], (BLOCK, 1)) for r in range(REP)], axis=0)
    return bias, snk


def _attn_fwd(qkv, bias4, sink4):
    t = qkv.shape[0]
    nb = t // BLOCK
    qs, ks, vs, bs, ss = _attn_specs(nb)

    def body(q_ref, kp_ref, kc_ref, kn_ref, vp_ref, vc_ref, vn_ref, b_ref, s_ref, o_ref):
        n = pl.program_id(1)
        kcat = jnp.concatenate([kp_ref[...], kc_ref[...], kn_ref[...]], axis=0)
        vcat = jnp.concatenate([vp_ref[...], vc_ref[...], vn_ref[...]], axis=0)
        valid = _attn_valid(n, nb)
        lane = lax.broadcasted_iota(jnp.int32, (BLOCK, LANES), 1) < ATTN_DIM
        bias, snk = _stack_bias(b_ref, s_ref)
        pn, _ = _attn_probs(_stack_heads(q_ref, lane) * SCALE, kcat, bias, snk, valid)
        o_ref[...] = _unstack_heads(_dot(pn.astype(BF16), vcat), lane).astype(o_ref.dtype)

    return pl.pallas_call(
        body, name="attn_fwd", grid=(ATTN_KV, nb), in_specs=[qs] + ks + vs + [bs, ss],
        out_specs=qs, out_shape=jax.ShapeDtypeStruct((t, D_MODEL), BF16),
        compiler_params=_cparams("parallel", "arbitrary"),
    )(qkv, qkv, qkv, qkv, qkv, qkv, qkv, bias4, sink4)


def _attn_bwd(qkv, bias4, sink4, do):
    t = qkv.shape[0]
    nb = t // BLOCK
    qs, ks, vs, bs, ss = _attn_specs(nb)
    part = pl.BlockSpec((3, BLOCK, LANES), lambda g, n: (0, n, g))

    def body(q_ref, kp_ref, kc_ref, kn_ref, vp_ref, vc_ref, vn_ref, b_ref, s_ref, do_ref,
             dq_ref, dk_ref, dv_ref, db_ref, ds_ref):
        n = pl.program_id(1)

        @pl.when(n == 0)
        def _():
            db_ref[...] = jnp.zeros_like(db_ref)
            ds_ref[...] = jnp.zeros_like(ds_ref)

        kcat = jnp.concatenate([kp_ref[...], kc_ref[...], kn_ref[...]], axis=0)
        vcat = jnp.concatenate([vp_ref[...], vc_ref[...], vn_ref[...]], axis=0)
        valid = _attn_valid(n, nb)
        lane = lax.broadcasted_iota(jnp.int32, (BLOCK, LANES), 1) < ATTN_DIM
        bias, snk = _stack_bias(b_ref, s_ref)
        q = _stack_heads(q_ref, lane)
        do = _stack_heads(do_ref, lane)
        pn, psink = _attn_probs(q * SCALE, kcat, bias, snk, valid)
        dp = _dot_nt(do, vcat)
        delta = jnp.sum(pn * dp, axis=1, keepdims=True)
        dsc = pn * (dp - delta)
        dsink = psink * delta
        for r in range(REP):
            rows = slice(r * BLOCK, (r + 1) * BLOCK)
            db_ref[r] += dsc[rows]
            ds_ref[r] += jnp.broadcast_to(-jnp.sum(dsink[rows], axis=0, keepdims=True), (1, LANES))
        dsb = (dsc * SCALE).astype(BF16)
        dq_ref[...] = _unstack_heads(_dot(dsb, kcat), lane).astype(dq_ref.dtype)
        dk = _dot_tn(dsb, q)
        dv = _dot_tn(pn.astype(BF16), do)
        for j in range(3):
            dk_ref[j] = dk[j * BLOCK:(j + 1) * BLOCK]
            dv_ref[j] = dv[j * BLOCK:(j + 1) * BLOCK]

    kv_cols = ATTN_KV * LANES
    return pl.pallas_call(
        body, name="attn_bwd", grid=(ATTN_KV, nb), in_specs=[qs] + ks + vs + [bs, ss, qs],
        out_specs=[qs, part, part, bs, ss],
        out_shape=[jax.ShapeDtypeStruct((t, D_MODEL), BF16), jax.ShapeDtypeStruct((3, t, kv_cols), F32),
                   jax.ShapeDtypeStruct((3, t, kv_cols), F32), jax.ShapeDtypeStruct(bias4.shape, F32),
                   jax.ShapeDtypeStruct(sink4.shape, F32)],
        compiler_params=_cparams("parallel", "arbitrary"),
    )(qkv, qkv, qkv, qkv, qkv, qkv, qkv, bias4, sink4, do)


def _kv_combine(name, parts):
    _, t, cols = parts.shape

    def body(p_ref, o_ref):
        z = jnp.zeros((BLOCK, LANES), F32)
        from_next = jnp.concatenate([p_ref[0, BLOCK:, :], z], axis=0)
        from_prev = jnp.concatenate([z, p_ref[2, :t - BLOCK, :]], axis=0)
        o_ref[...] = (from_next + p_ref[1] + from_prev).astype(o_ref.dtype)

    return pl.pallas_call(
        body, name=name, grid=(cols // LANES,),
        in_specs=[pl.BlockSpec((3, t, LANES), lambda g: (0, 0, g))],
        out_specs=pl.BlockSpec((t, LANES), lambda g: (0, g)),
        out_shape=jax.ShapeDtypeStruct((t, cols), BF16),
        compiler_params=_cparams("parallel"),
    )(parts)


def _t5_bucket(rel):
    nb = N_BUCKETS // 2
    max_exact = nb // 2
    ret = jnp.where(rel > 0, nb, 0)
    n = jnp.abs(rel)
    nf = jnp.maximum(n, 1).astype(jnp.float32)
    large = max_exact + (jnp.log(nf / max_exact) / math.log(MAX_DISTANCE / max_exact) * (nb - max_exact)).astype(jnp.int32)
    large = jnp.minimum(large, nb - 1)
    return ret + jnp.where(n < max_exact, n, large)


def _bucket_map():
    i = jnp.arange(BLOCK)[:, None]
    j = jnp.arange(3 * BLOCK)[None, :]
    return _t5_bucket(j - BLOCK - i)


def _bias_from_table(table, onehot_t):
    def body(t_ref, o_ref, out_ref):
        out_ref[...] = _dot(t_ref[...], o_ref[...], HI)

    return pl.pallas_call(body, name="bias_from_table", out_shape=jax.ShapeDtypeStruct((ATTN_HEADS, onehot_t.shape[1]), F32),
                          compiler_params=pltpu.CompilerParams(vmem_limit_bytes=VMEM_LIMIT_BYTES))(table.T, onehot_t)


def _bias_table_grad(dbias, onehot_t):
    def body(d_ref, o_ref, out_ref):
        out_ref[...] = _dot_nt(d_ref[...], o_ref[...], HI)

    return pl.pallas_call(body, name="bias_table_grad", out_shape=jax.ShapeDtypeStruct((ATTN_HEADS, N_BUCKETS), F32),
                          compiler_params=pltpu.CompilerParams(vmem_limit_bytes=VMEM_LIMIT_BYTES))(dbias, onehot_t)


HBM_SPEC = pl.BlockSpec(memory_space=pl.ANY)


def _comm_call(name, body, xs, out_shapes, n_sems):
    n = len(xs)
    return pl.pallas_call(
        body, name=name, in_specs=[HBM_SPEC] * n, out_specs=[HBM_SPEC] * n, out_shape=out_shapes,
        scratch_shapes=[pltpu.SemaphoreType.DMA((n * n_sems,)), pltpu.SemaphoreType.DMA((n * n_sems,))],
    )(*xs)


def _allgather_chips(name, xs):
    n = len(xs)

    def body(*refs):
        x_refs, out_refs, (send_sems, recv_sems) = refs[:n], refs[n:2 * n], refs[2 * n:]
        mx, my, mc = lax.axis_index("x"), lax.axis_index("y"), lax.axis_index("c")
        me = 2 * mx + my
        chips = [(1 - mx, my), (mx, 1 - my), (1 - mx, 1 - my)]
        sibling = (mx, my, 1 - mc)

        def part(i, slot, h):
            r2 = xs[i].shape[0] // 2
            return out_refs[i].at[slot, pl.ds(h * r2, r2)]

        def copy(i, k, src, dst, to):
            return pltpu.make_async_remote_copy(src_ref=src, dst_ref=dst, send_sem=send_sems.at[6 * i + k],
                                                recv_sem=recv_sems.at[6 * i + k], device_id=to, device_id_type=MESH)

        first = [copy(i, k, x_refs[i].at[pl.ds(mc * (xs[i].shape[0] // 2), xs[i].shape[0] // 2)], part(i, me, mc), (px, py, mc))
                 for i in range(n) for k, (px, py) in enumerate(chips)]
        for cp in first:
            cp.start()
        passed = []
        for k, (px, py) in enumerate(chips):
            for i in range(n):
                landed = part(i, 2 * px + py, mc)
                copy(i, k, landed, landed, (px, py, mc)).wait_recv()
                passed.append(copy(i, 3 + k, landed, landed, sibling))
                passed[-1].start()
        for k, (px, py) in enumerate(chips):
            for i in range(n):
                theirs = part(i, 2 * px + py, 1 - mc)
                copy(i, 3 + k, theirs, theirs, sibling).wait_recv()
        for cp in first + passed:
            cp.wait_send()

    return _comm_call(name, body, xs, [jax.ShapeDtypeStruct((4,) + x.shape, x.dtype) for x in xs], 6)


SEM_SPEC = pl.BlockSpec(memory_space=pltpu.SEMAPHORE)
DATAFLOW = pltpu.SideEffectType.DATAFLOW_SIDE_EFFECTING


def _send_start(name, xs, after, scatter, collective_id):
    n = len(xs)

    def body(*refs):
        x_refs, land_refs = refs[:n], refs[n:2 * n]
        send_sems, recv_sems, token = refs[2 * n + 1], refs[2 * n + 2], refs[-1]
        mx, my, mc = lax.axis_index("x"), lax.axis_index("y"), lax.axis_index("c")
        chips = [(1 - mx, my), (mx, 1 - my), (1 - mx, 1 - my)]
        barrier = pltpu.get_barrier_semaphore()
        for px, py in chips:
            pl.semaphore_signal(barrier, inc=1, device_id=(px, py, mc), device_id_type=MESH)
        pl.semaphore_wait(barrier, len(chips))
        for i in range(n):
            for k, (px, py) in enumerate(chips):
                src = x_refs[i].at[2 * px + py] if scatter else x_refs[i]
                pltpu.make_async_remote_copy(src_ref=src, dst_ref=land_refs[i].at[2 * mx + my], send_sem=send_sems.at[3 * i + k],
                                             recv_sem=recv_sems.at[3 * i + k], device_id=(px, py, mc), device_id_type=MESH).start()
        token[...] = jnp.zeros_like(token)

    lands = [lax.empty((4,) + (x.shape[1:] if scatter else x.shape), x.dtype) for x in xs]
    hbm = [pltpu.HBM(a.shape, a.dtype) for a in list(xs) + lands]
    out = pl.pallas_call(
        body, name=name,
        out_shape=(pltpu.SemaphoreType.DMA((3 * n,)), pltpu.SemaphoreType.DMA((3 * n,)), *hbm, jax.ShapeDtypeStruct((SUBLANES, LANES), F32)),
        in_specs=(HBM_SPEC,) * (2 * n + 1),
        out_specs=(SEM_SPEC, SEM_SPEC) + (HBM_SPEC,) * (2 * n) + (pl.BlockSpec(memory_space=pltpu.VMEM),),
        input_output_aliases={i: 2 + i for i in range(2 * n)},
        compiler_params=pltpu.CompilerParams(has_side_effects=DATAFLOW, collective_id=collective_id),
    )(*[pltpu.with_memory_space_constraint(a, pltpu.HBM) for a in list(xs) + lands], after)
    return out[0], out[1], list(out[2:2 + n]), list(out[2 + n:2 + 2 * n]), out[-1]


def _send_wait(name, send_sems, recv_sems, x_thrus, land_thrus, after, scatter):
    n = len(x_thrus)

    def body(*refs):
        x_refs, land_refs, send_sems, recv_sems = refs[:n], refs[n:2 * n], refs[2 * n], refs[2 * n + 1]
        mx, my, mc = lax.axis_index("x"), lax.axis_index("y"), lax.axis_index("c")
        chips = [(1 - mx, my), (mx, 1 - my), (1 - mx, 1 - my)]
        for i in range(n):
            for k, (px, py) in enumerate(chips):
                src = x_refs[i].at[0] if scatter else x_refs[i]
                copy = pltpu.make_async_remote_copy(src_ref=src, dst_ref=land_refs[i].at[2 * px + py], send_sem=send_sems.at[3 * i + k],
                                                    recv_sem=recv_sems.at[3 * i + k], device_id=(px, py, mc), device_id_type=MESH)
                copy.wait_send()
                copy.wait_recv()

    arrs = list(x_thrus) + list(land_thrus)
    out = pl.pallas_call(
        body, name=name, out_shape=tuple(pltpu.HBM(a.shape, a.dtype) for a in arrs),
        in_specs=(HBM_SPEC,) * (2 * n) + (SEM_SPEC, SEM_SPEC, HBM_SPEC), out_specs=(HBM_SPEC,) * (2 * n),
        input_output_aliases={i: i for i in range(2 * n)},
        compiler_params=pltpu.CompilerParams(has_side_effects=DATAFLOW),
    )(*arrs, send_sems, recv_sems, after)
    return list(out[n:])


def _swap_cores(name, xs, slab=True):
    n = len(xs)

    def body(*refs):
        x_refs, out_refs, (send_sems, recv_sems) = refs[:n], refs[n:2 * n], refs[2 * n:]
        mx, my, mc = lax.axis_index("x"), lax.axis_index("y"), lax.axis_index("c")
        sends = [pltpu.make_async_remote_copy(src_ref=x_refs[i].at[1 - mc] if slab else x_refs[i], dst_ref=out_refs[i],
                                              send_sem=send_sems.at[i], recv_sem=recv_sems.at[i],
                                              device_id=(mx, my, 1 - mc), device_id_type=MESH)
                 for i in range(n)]
        for cp in sends:
            cp.start()
        for cp in sends:
            cp.wait()

    return _comm_call(name, body, xs, [jax.ShapeDtypeStruct(x.shape[1:] if slab else x.shape, x.dtype) for x in xs], 1)


def _scatter_chips(name, gs):
    n = len(gs)

    def body(*refs):
        g_refs, out_refs, (send_sems, recv_sems) = refs[:n], refs[n:2 * n], refs[2 * n:]
        mx, my, mc = lax.axis_index("x"), lax.axis_index("y"), lax.axis_index("c")
        me = 2 * mx + my
        chips = [(1 - mx, my), (mx, 1 - my), (1 - mx, 1 - my)]

        def copy(i, k, src, dst, to):
            return pltpu.make_async_remote_copy(src_ref=src, dst_ref=dst, send_sem=send_sems.at[3 * i + k],
                                                recv_sem=recv_sems.at[3 * i + k], device_id=to, device_id_type=MESH)

        sends = [copy(i, k, g_refs[i].at[2 * px + py], out_refs[i].at[me], (px, py, mc))
                 for i in range(n) for k, (px, py) in enumerate(chips)]
        for cp in sends:
            cp.start()
        for i in range(n):
            for k, (px, py) in enumerate(chips):
                copy(i, k, g_refs[i].at[0], out_refs[i].at[2 * px + py], (px, py, mc)).wait_recv()
        for cp in sends:
            cp.wait_send()

    return _comm_call(name, body, gs, [jax.ShapeDtypeStruct(g.shape, g.dtype) for g in gs], 3)


def _allgather_all(name, x):
    def body(x_ref, out_ref, send_sems, recv_sems, local_sem):
        mx, my, mc = lax.axis_index("x"), lax.axis_index("y"), lax.axis_index("c")
        me = 4 * mx + 2 * my + mc
        flips = [(fx, fy, fc) for fx in (0, 1) for fy in (0, 1) for fc in (0, 1)][1:]
        peers = [(mx ^ fx, my ^ fy, mc ^ fc) for fx, fy, fc in flips]
        mine = pltpu.make_async_copy(x_ref, out_ref.at[me], local_sem)
        mine.start()
        sends = [pltpu.make_async_remote_copy(src_ref=x_ref, dst_ref=out_ref.at[me], send_sem=send_sems.at[k],
                                              recv_sem=recv_sems.at[k], device_id=peer, device_id_type=MESH)
                 for k, peer in enumerate(peers)]
        for cp in sends:
            cp.start()
        for k, (px, py, pc) in enumerate(peers):
            pltpu.make_async_remote_copy(src_ref=x_ref, dst_ref=out_ref.at[4 * px + 2 * py + pc], send_sem=send_sems.at[k],
                                         recv_sem=recv_sems.at[k], device_id=(px, py, pc), device_id_type=MESH).wait_recv()
        for cp in sends:
            cp.wait_send()
        mine.wait()

    return pl.pallas_call(
        body, name=name, in_specs=[HBM_SPEC], out_specs=HBM_SPEC,
        out_shape=jax.ShapeDtypeStruct((8,) + x.shape, x.dtype),
        scratch_shapes=[pltpu.SemaphoreType.DMA((7,)), pltpu.SemaphoreType.DMA((7,)), pltpu.SemaphoreType.DMA],
    )(x)


def _sum_slots(name, st, tb=256):
    n, r, c = st.shape
    tb = _pick(r, (tb, 128, 32))

    def body(s_ref, o_ref):
        acc = s_ref[0].astype(F32)
        for k in range(1, n):
            acc = acc + s_ref[k].astype(F32)
        o_ref[...] = acc

    return pl.pallas_call(
        body, name=name, grid=(r // tb,), in_specs=[pl.BlockSpec((n, tb, c), lambda i: (0, i, 0))],
        out_specs=pl.BlockSpec((tb, c), lambda i: (i, 0)), out_shape=jax.ShapeDtypeStruct((r, c), F32),
        compiler_params=_cparams("parallel"),
    )(st)


def _adamw(name, w, g, m, v):
    def fn(w, g, m, v):
        m2 = ADAM_B1 * m + (1.0 - ADAM_B1) * g
        v2 = ADAM_B2 * v + (1.0 - ADAM_B2) * jnp.square(g)
        m_hat = m2 / (1.0 - ADAM_B1 ** ADAM_STEP)
        v_hat = v2 / (1.0 - ADAM_B2 ** ADAM_STEP)
        delta = -ADAM_LR * (m_hat / (jnp.sqrt(v_hat) + ADAM_EPS) + ADAM_WD * w)
        return [delta, m2, v2], []

    tb = _pick(w.shape[0], (256, 32))
    return _rowwise(name, fn, [_row(w), _row(g), _row(m), _row(v)], [], [(w.shape[1], F32)] * 3, tb=tb)


BIG = ("w_ssd_out", "w_attn_out", "w_o", "w_mlp_in", "w_mlp_out", "conv_w")
SMALL = ("pre_mix_norm", "b_gate", "conv_b", "dt_bias", "a_log", "d_skip", "ssd_norm", "attn_sink", "rel_bias_table",
         "post_mix_norm", "pre_mlp_norm", "post_mlp_norm")
ALL_W = ("pre_mix_norm", "w_in", "b_gate", "conv_w", "conv_b", "dt_bias", "a_log", "d_skip", "ssd_norm", "w_ssd_out",
         "attn_sink", "rel_bias_table", "w_attn_out", "w_o", "post_mix_norm", "pre_mlp_norm", "w_mlp_in", "w_mlp_out",
         "post_mlp_norm")


def _pack_rows(parts, rows, dtype):
    flat = jnp.concatenate([p.reshape(-1, D_MODEL).astype(dtype) for p in parts], axis=0)
    return jnp.pad(flat, ((0, rows - flat.shape[0]), (0, 0)))


def _pack_big(shards, dtype):
    return _pack_rows([shards[n] for n in BIG], BIG_ROWS, dtype)


def _unpack_big(flat, like):
    out, r = {}, 0
    for n in BIG:
        shp = like[n].shape
        nr = math.prod(shp) // D_MODEL
        out[n] = flat[r:r + nr].reshape(shp)
        r += nr
    return out


def _pack_small(parts, extra=None):
    flat = jnp.concatenate([parts[n].reshape(-1).astype(F32) for n in SMALL] + ([extra.reshape(-1)] if extra is not None else []))
    return jnp.pad(flat, (0, SMALL_ROWS * D_MODEL - flat.shape[0])).reshape(SMALL_ROWS, D_MODEL)


def _unpack_small(flat2, like):
    flat = flat2.reshape(-1)
    out, r = {}, 0
    for n in SMALL:
        shp = like[n].shape
        k = math.prod(shp)
        out[n] = flat[r:r + k].reshape(shp)
        r += k
    return out, flat[r]


def _shard_of_full(name, full, s):
    if name in ("w_in", "w_mlp_in"):
        w = full.shape[2] // 4
        return full[:, :, s * w:(s + 1) * w]
    if name == "conv_w":
        w = full.shape[3] // 4
        return full[:, :, :, s * w:(s + 1) * w]
    w = full.shape[1] // 4
    return full[:, s * w:(s + 1) * w, :]


def _full_of_shards(name, shards):
    axis = {"w_in": 2, "w_mlp_in": 2, "conv_w": 3}.get(name, 1)
    return jnp.concatenate(shards, axis=axis)


def _to_proj_layout(w):
    z, xbc, dt, q, k, v, gates = (w[..., 0:2048], w[..., 2048:6144], w[..., 6144:6208], w[..., 6208:7232],
                                  w[..., 7232:7488], w[..., 7488:7744], w[..., 7744:9792])
    pad = jnp.zeros(w.shape[:-1] + (N_MAIN - OFF_DT - dt.shape[-1],), w.dtype)

    def doubled(a):
        h = a.reshape(a.shape[:-1] + (ATTN_KV, 1, ATTN_DIM))
        return jnp.broadcast_to(h, a.shape[:-1] + (ATTN_KV, 2, ATTN_DIM)).reshape(a.shape[:-1] + (2 * a.shape[-1],))

    return jnp.concatenate([z, gates, xbc, dt, pad, q, doubled(k), doubled(v)], axis=-1)


def _from_proj_layout(w):
    z, gates, xbc, dt, q, k2, v2 = (w[..., 0:2048], w[..., 2048:4096], w[..., 4096:8192], w[..., 8192:8256],
                                    w[..., 8320:9344], w[..., 9344:9856], w[..., 9856:10368])

    def folded(a):
        return a.reshape(a.shape[:-1] + (ATTN_KV, 2, ATTN_DIM)).sum(axis=-2).reshape(a.shape[:-1] + (a.shape[-1] // 2,))

    return jnp.concatenate([z, xbc, dt, q, folded(k2), folded(v2), gates], axis=-1)


def _layer_fwd(h1, x, W, P, l, bias4):
    t = x.shape[0]
    S = {"x": x, "h1": h1}
    proj = _matmul("proj", h1, W["w_in_main"][l], "nn")
    qkv = _matmul("proj_qkv", h1, W["w_in_qkv"][l], "nn", out_dtype=BF16)
    S["proj"] = proj
    pre, act = _conv_fwd(proj, W["conv_w"][l], P["conv_b"][l].reshape(1, CONV_DIM))
    S["pre"], S["act"] = pre, act

    dtb = jnp.pad(P["dt_bias"][l].reshape(1, 2 * SSD_HEADS), ((0, 0), (0, LANES - 2 * SSD_HEADS)))

    def dt_fn(raw, b):
        v = raw + b
        dt = jnp.maximum(v, 0.0) + jnp.log1p(jnp.exp(-jnp.abs(v)))
        expand = (jnp.right_shift(lax.broadcasted_iota(jnp.int32, (LANES, 2 * D_INNER), 1), 6)
                  == lax.broadcasted_iota(jnp.int32, (LANES, 2 * D_INNER), 0)).astype(BF16)
        hi = dt.astype(BF16)
        rest = dt - hi.astype(F32)
        mid = rest.astype(BF16)
        lo = (rest - mid.astype(F32)).astype(BF16)
        return [dt, _dot(hi, expand) + _dot(mid, expand) + _dot(lo, expand)], []

    dt, dte = _rowwise("dt_fwd", dt_fn, [_row(proj, LANES, OFF_DT // LANES)], [dtb], [(LANES, F32), (2 * D_INNER, F32)], tb=512)
    dt2 = jnp.stack([dt[:, 0:SSD_HEADS], dt[:, SSD_HEADS:2 * SSD_HEADS]])
    dt2t = dt2.transpose(0, 2, 1)
    a = -jnp.exp(P["a_log"][l])
    a_row, a_col = a.reshape(2, 1, SSD_HEADS), a.reshape(2, SSD_HEADS, 1)
    S["dt2"], S["dt2t"], S["a_row"], S["a_col"], S["dte"] = dt2, dt2t, a_row, a_col, dte
    y2, states = _ssd_fwd(act, dt2, dt2t, a_row, a_col, dte)
    S["states"] = states

    dsk = jnp.repeat(P["d_skip"][l], SSD_HEAD_DIM).reshape(1, D_INNER)
    nw = P["ssd_norm"][l].reshape(1, D_INNER)
    S["dsk"], S["nw"] = dsk, nw

    def gn_fn(yf, yb, xs, z, dsk, nw):
        y = yf + yb + xs * dsk
        return [y, _gated_norm_fwd(y, z, nw)], []

    y, yn = _rowwise("gated_norm_fwd", gn_fn,
                     [_row(y2, lead=0), _row(y2, lead=1), _row(act, D_INNER, 0), _row(proj, D_INNER, OFF_Z // D_INNER)],
                     [dsk, nw], [(D_INNER, F32), (D_INNER, BF16)])
    S["y"], S["yn"] = y, yn
    if "late" in W:
        W.update(W.pop("late")(yn))
    y_ssd = _matmul("ssd_out", yn, W["w_ssd_out"][l], "nn")
    S["y_ssd"] = y_ssd

    sink4 = jnp.broadcast_to(P["attn_sink"][l].reshape(ATTN_KV, REP, 1, 1), (ATTN_KV, REP, 1, LANES))
    S["qkv"], S["sink4"] = qkv, sink4
    o = _attn_fwd(qkv, bias4, sink4)
    S["o"] = o
    y_attn = _matmul("attn_out", o, W["w_attn_out"][l], "nn")
    S["y_attn"] = y_attn

    bg = P["b_gate"][l].reshape(1, 2 * D_MODEL)
    S["bg"] = bg

    def merge_fn(gates, ys, ya, b):
        g = jax.nn.sigmoid(gates + b)
        return [g[:, :D_MODEL] * ys + g[:, D_MODEL:] * ya], []

    (mix_in,) = _rowwise("merge_fwd", merge_fn, [_row(proj, 2 * D_MODEL, OFF_G // (2 * D_MODEL)), _row(y_ssd), _row(y_attn)],
                         [bg], [(D_MODEL, BF16)])
    S["mix_in"] = mix_in
    mixed = _matmul("w_o", mix_in, W["w_o"][l], "nn")
    S["mixed"] = mixed

    g_pm = P["post_mix_norm"][l].reshape(1, D_MODEL)
    g_pl = P["pre_mlp_norm"][l].reshape(1, D_MODEL)

    def postmix_fn(x, mixed, g1, g2):
        x2 = x + _rms_fwd(mixed, g1)
        return [x2, _rms_fwd(x2, g2)], []

    x2, h2 = _rowwise("post_mix_fwd", postmix_fn, [_row(x), _row(mixed)], [g_pm, g_pl], [(D_MODEL, F32), (D_MODEL, BF16)])
    S["x2"], S["h2"] = x2, h2
    a1 = _matmul("mlp_in", h2, W["w_mlp_in"][l], "nn", out_dtype=BF16,
                 epilogue=lambda acc: jnp.square(jnp.maximum(acc, 0.0)))
    S["a1"] = a1
    f2 = _matmul("mlp_out", a1, W["w_mlp_out"][l], "nn")
    S["f2"] = f2
    return S


def _layer_bwd(S, dx3, W, P, l, bias4, onehot_t, zero=None):
    t = dx3.shape[0]
    G = {}
    g_pmlp = P["post_mlp_norm"][l].reshape(1, D_MODEL)
    if zero is not None:
        g_pmlp = g_pmlp + zero

    def b1_fn(f2, dx3, g):
        df2, dg = _rms_bwd(f2, g, dx3)
        return [df2], [dg]

    df2, G["post_mlp_norm"] = _rowwise("post_mlp_bwd", b1_fn, [_row(S["f2"]), _row(dx3)], [g_pmlp], [(D_MODEL, BF16)], [(1, D_MODEL)])
    df1 = _matmul("d_f1", df2, W["w_mlp_out"][l], "nt", out_dtype=BF16,
                  epilogue=lambda acc, a1: acc * (2.0 * jnp.sqrt(a1.astype(F32))), extras=[S["a1"]])
    G["w_mlp_out"] = _matmul("dw_mlp_out", S["a1"], df2, "tn")
    dh2 = _matmul("d_h2", df1, W["w_mlp_in"][l], "nt")
    G["w_mlp_in"] = _matmul("dw_mlp_in", S["h2"], df1, "tn")

    g_pm = P["post_mix_norm"][l].reshape(1, D_MODEL)
    g_pl = P["pre_mlp_norm"][l].reshape(1, D_MODEL)

    def b3_fn(x2, dh2, dx3, mixed, g_pl, g_pm):
        d1, dgl = _rms_bwd(x2, g_pl, dh2)
        dx2 = dx3 + d1
        dmixed, dgm = _rms_bwd(mixed, g_pm, dx2)
        return [dx2, dmixed], [dgl, dgm]

    dx2, dmixed, G["pre_mlp_norm"], G["post_mix_norm"] = _rowwise(
        "post_mix_bwd", b3_fn, [_row(S["x2"]), _row(dh2), _row(dx3), _row(S["mixed"])], [g_pl, g_pm],
        [(D_MODEL, F32), (D_MODEL, BF16)], [(1, D_MODEL), (1, D_MODEL)])
    dmix_in = _matmul("d_mix_in", dmixed, W["w_o"][l], "nt")
    G["w_o"] = _matmul("dw_o", S["mix_in"], dmixed, "tn")

    proj = S["proj"]

    def b4_fn(gates, ys, ya, dmix, b):
        g = jax.nn.sigmoid(gates + b)
        gs, ga = g[:, :D_MODEL], g[:, D_MODEL:]
        dg = jnp.concatenate([ys * dmix, ya * dmix], axis=-1) * g * (1.0 - g)
        return [gs * dmix, ga * dmix, dg], [jnp.sum(dg, axis=0, keepdims=True)]

    dy_ssd, dy_attn, dgates, G["b_gate"] = _rowwise(
        "merge_bwd", b4_fn, [_row(proj, 2 * D_MODEL, OFF_G // (2 * D_MODEL)), _row(S["y_ssd"]), _row(S["y_attn"]), _row(dmix_in)],
        [S["bg"]], [(D_MODEL, BF16), (D_MODEL, BF16), (2 * D_MODEL, BF16)], [(1, 2 * D_MODEL)])

    dyn = _matmul("d_yn", dy_ssd, W["w_ssd_out"][l], "nt")
    G["w_ssd_out"] = _matmul("dw_ssd_out", S["yn"], dy_ssd, "tn")
    do = _matmul("d_o", dy_attn, W["w_attn_out"][l], "nt", out_dtype=BF16)
    G["w_attn_out"] = _matmul("dw_attn_out", S["o"], dy_attn, "tn")

    dq, dkp, dvp, dbias4, dsink4 = _attn_bwd(S["qkv"], bias4, S["sink4"], do)
    dk = _kv_combine("dk_combine", dkp)
    dv = _kv_combine("dv_combine", dvp)
    G["attn_sink"] = dsink4[:, :, 0, 0].reshape(ATTN_HEADS)
    G["rel_bias_table"] = _bias_table_grad(dbias4.reshape(ATTN_HEADS, BLOCK * 3 * BLOCK), onehot_t).T

    act = S["act"]

    def b5_fn(y, z, xs, dyn, nw, dsk):
        dy, dz, dnw = _gated_norm_bwd(y, z, nw, dyn)
        return [dy, dz], [dnw, jnp.sum(dy * xs, axis=0, keepdims=True)]

    dy, dz, G["ssd_norm"], dskip_cols = _rowwise(
        "gated_norm_bwd", b5_fn, [_row(S["y"]), _row(proj, D_INNER, OFF_Z // D_INNER), _row(act, D_INNER, 0), _row(dyn)],
        [S["nw"], S["dsk"]], [(D_INNER, F32), (D_INNER, BF16)], [(1, D_INNER), (1, D_INNER)])
    G["d_skip"] = dskip_cols.reshape(SSD_HEADS, SSD_HEAD_DIM).sum(axis=-1)

    dxs2, dbs2, dcs2, ddt2, da2 = _ssd_bwd(act, S["dt2"], S["dt2t"], S["a_row"], S["a_col"], S["dte"], dy, S["states"])
    G["a_log"] = da2.reshape(2, SSD_HEADS) * S["a_row"].reshape(2, SSD_HEADS)

    def b6_fn(dxf, dxb, dy, dbf, dbb, dcf, dcb, pre, dsk):
        dact = jnp.concatenate([dxf + dxb + dy * dsk, dbf + dbb, dcf + dcb], axis=-1)
        return [dact * _silu_grad(pre)], []

    (dpre,) = _rowwise("silu_bwd", b6_fn,
                       [_row(dxs2, lead=0), _row(dxs2, lead=1), _row(dy), _row(dbs2, lead=0), _row(dbs2, lead=1),
                        _row(dcs2, lead=0), _row(dcs2, lead=1), _row(S["pre"])], [S["dsk"]], [(CONV_DIM, F32)], tb=128)
    du, dconv_w, dconv_b = _conv_bwd(dpre, proj, W["conv_w"][l])
    G["conv_w"] = dconv_w.reshape(SSD_CONV, 1, CONV_DIM)
    G["conv_b"] = dconv_b.reshape(CONV_DIM)

    dtb = jnp.pad(P["dt_bias"][l].reshape(1, 2 * SSD_HEADS), ((0, 0), (0, LANES - 2 * SSD_HEADS)))
    ddt = jnp.pad(jnp.concatenate([ddt2[0], ddt2[1]], axis=-1), ((0, 0), (0, LANES - 2 * SSD_HEADS)))

    def b7_fn(raw, ddt, b):
        draw = ddt * jax.nn.sigmoid(raw + b)
        return [draw], [jnp.sum(draw, axis=0, keepdims=True)]

    draw, ddtb = _rowwise("dt_bwd", b7_fn, [_row(proj, LANES, OFF_DT // LANES), _row(ddt)], [dtb], [(LANES, BF16)], [(1, LANES)], tb=512)
    G["dt_bias"] = ddtb[0, :2 * SSD_HEADS].reshape(2, SSD_HEADS)

    dproj = jnp.concatenate([dz, dgates, du, draw, dq, dk, dv], axis=-1)
    G["w_in"] = _from_proj_layout(_matmul("dw_in", S["h1"], dproj, "tn"))
    dh1 = _matmul("d_h1", dproj, W["w_in"][l], "nt")

    g_pre = P["pre_mix_norm"][l].reshape(1, D_MODEL)

    def b8_fn(x, dh1, dx2, g):
        d1, dg = _rms_bwd(x, g, dh1)
        return [dx2 + d1], [dg]

    dx, G["pre_mix_norm"] = _rowwise("pre_mix_bwd", b8_fn, [_row(S["x"]), _row(dh1), _row(dx2)], [g_pre], [(D_MODEL, F32)], [(1, D_MODEL)])
    return dx, G


def _chip_sums(tag, pieces):
    theirs = _swap_cores("swap_" + tag, pieces)
    out = []
    for i, p in enumerate(pieces):
        _, _, r2, c = p.shape
        mine = lax.dynamic_index_in_dim(p, lax.axis_index("c"), axis=0, keepdims=False).reshape(4 * r2, c)
        (cs,) = _rowwise(f"presum_{tag}{i}", lambda a, b: ([a.astype(F32) + b.astype(F32)], []),
                         [_row(mine), _row(theirs[i].reshape(4 * r2, c))], [], [(c, BF16)])
        out.append(cs.reshape(4, r2, c))
    return out


def _halves(blocks):
    r2 = blocks[0].shape[0] // 2
    return jnp.stack([jnp.stack([b[h * r2:(h + 1) * r2] for b in blocks]) for h in range(2)])


def _layer_pieces(g, like):
    cols = like["w_in"].shape[2]
    win = _halves([g["w_in"][:, s * cols:(s + 1) * cols].astype(BF16) for s in range(4)])
    packs = [_pack_rows([_shard_of_full(n, g[n][None], s)[0] for n in BIG], BIG_ROWS // 2, BF16) for s in range(4)]
    return [win, _halves(packs)]


def _unpack_layer(flat, like):
    out, r = {}, 0
    for n in BIG:
        shp = like[n].shape[1:]
        nr = math.prod(shp) // D_MODEL
        out[n] = flat[r:r + nr].reshape(shp)
        r += nr
    return out


def _step(x, target, shard_w, shard_m, shard_v):
    depth, _, win_cols = shard_w["w_in"].shape
    win_rows = depth * D_MODEL

    def win2d(a):
        return a.reshape(win_rows, win_cols)

    conv_rows = shard_w["conv_w"].reshape(-1, D_MODEL)
    win_own = win2d(shard_w["w_in"]).astype(BF16)
    conv_own = jnp.pad(conv_rows, ((0, 16 - conv_rows.shape[0]), (0, 0)))
    win_g, conv_g = _allgather_chips("gather_w_in", [win_own, conv_own])
    packed = _pack_big(shard_w, BF16)
    send_sems, recv_sems, packed_thru, land_thru, token = _send_start("gather_rest_start", [packed], conv_g, False, 0)

    def late(after):
        (landed,) = _send_wait("gather_rest_wait", send_sems, recv_sems, packed_thru, land_thru, after, False)
        per_chip = [_unpack_big(jnp.where(me == s, packed, landed[s]), shard_w) for s in range(4)]
        return {n: _full_of_shards(n, [per_chip[s][n] for s in range(4)]) for n in BIG if n != "conv_w"}

    W = {"late": late}
    me = 2 * lax.axis_index("x") + lax.axis_index("y")
    win_g = [jnp.where(me == s, win_own, win_g[s]) for s in range(4)]
    conv_g = [jnp.where(me == s, conv_own, conv_g[s]) for s in range(4)]
    W["w_in"] = _to_proj_layout(jnp.concatenate([win_g[s].reshape(depth, D_MODEL, win_cols) for s in range(4)], axis=2))
    W["conv_w"] = jnp.concatenate([conv_g[s][:conv_rows.shape[0]].reshape(shard_w["conv_w"].shape) for s in range(4)],
                                  axis=3).reshape(depth, SSD_CONV, CONV_DIM)
    P = {n: shard_w[n] for n in SMALL}
    P["pre_mix_norm"] = P["pre_mix_norm"] + token[0, 0]
    assert depth == 2
    sent = {}

    def grads_ready(g):
        sent["sums"] = _chip_sums("l1_", _layer_pieces(g, shard_w))
        sent["sems"] = _send_start("scatter_l1_start", sent["sums"], sent["sums"][0], True, 1)
        return sent["sems"][4][0, 0]

    loss_part, grad_x, full = _local_step(x, target, W, P, grads_ready)
    send1, recv1, thru1, land1, _ = sent["sems"]
    landed1 = _send_wait("scatter_l1_wait", send1, recv1, thru1, land1, grad_x, True)
    sums0 = _chip_sums("l0_", _layer_pieces({n: full[n][0] for n in BIG + ("w_in",)}, shard_w))
    landed0 = _scatter_chips("scatter_grads", sums0)
    staged = [jnp.stack([jnp.where(me == s, own[s], got[s]) for s in range(4)])
              for own, got in zip(sums0 + sent["sums"], list(landed0) + landed1)]
    halves = [_sum_slots(f"sum_grads{i}", st) for i, st in enumerate(staged)]
    core = lax.axis_index("c")
    win0, big0, win1, big1 = [jnp.where(core == 0, jnp.concatenate([mine, other]), jnp.concatenate([other, mine]))
                              for mine, other in zip(halves, _swap_cores("share_grads", halves, slab=False))]
    g_win = jnp.concatenate([win0, win1], axis=0)
    per_layer = [_unpack_layer(big0, shard_w), _unpack_layer(big1, shard_w)]
    g_big = _pack_big({n: jnp.stack([per_layer[0][n], per_layer[1][n]]) for n in BIG}, F32)
    d_win, m_win, v_win = _adamw("adamw_w_in", win2d(shard_w["w_in"]), g_win, win2d(shard_m["w_in"]), win2d(shard_v["w_in"]))
    d_big, m_big, v_big = _adamw("adamw_big", _pack_big(shard_w, F32), g_big, _pack_big(shard_m, F32), _pack_big(shard_v, F32))

    small = _allgather_all("gather_small", _pack_small(full, loss_part))
    g_small = _sum_slots("sum_small", small, tb=SMALL_ROWS)
    d_small, m_small, v_small = _adamw("adamw_small", _pack_small(shard_w, jnp.zeros((), F32)), g_small,
                                       _pack_small(shard_m, jnp.zeros((), F32)), _pack_small(shard_v, jnp.zeros((), F32)))

    outs = {}
    for tag, win, big, sm in (("grad", g_win, g_big, g_small), ("delta", d_win, d_big, d_small),
                              ("new_m", m_win, m_big, m_small), ("new_v", v_win, v_big, v_small)):
        ub = _unpack_big(big, shard_w)
        us, extra = _unpack_small(sm, shard_w)
        outs[tag] = {"w_in": win.reshape(shard_w["w_in"].shape), **ub, **us}
        if tag == "grad":
            loss = extra
    return loss, grad_x, outs


def _local_step(x, target, W, P, grads_ready=None):
    depth = W["w_in"].shape[0]
    W = dict(W, w_in_main=W["w_in"][:, :, :N_MAIN], w_in_qkv=W["w_in"][:, :, N_MAIN:])
    onehot_t = (_bucket_map().reshape(1, -1) == jnp.arange(N_BUCKETS)[:, None]).astype(F32)
    bias = _bias_from_table(P["rel_bias_table"], onehot_t).reshape(ATTN_HEADS, BLOCK, 3 * BLOCK)
    bias4 = jnp.where(_band_mask(), bias, NEG).reshape(ATTN_KV, REP, BLOCK, 3 * BLOCK)

    def pre_fn(x, g):
        return [_rms_fwd(x, g)], []

    (h1,) = _rowwise("pre_mix_fwd", pre_fn, [_row(x)], [P["pre_mix_norm"][0].reshape(1, D_MODEL)], [(D_MODEL, BF16)])
    saved = []
    loss_cols = dxl = None
    for l in range(depth):
        S = _layer_fwd(h1, x, W, P, l, bias4)
        saved.append(S)
        g_pmlp = P["post_mlp_norm"][l].reshape(1, D_MODEL)
        if l + 1 < depth:
            def post_fn(x2, f2, g1, g2):
                x3 = x2 + _rms_fwd(f2, g1)
                return [x3, _rms_fwd(x3, g2)], []

            x, h1 = _rowwise("post_mlp_fwd", post_fn, [_row(S["x2"]), _row(S["f2"])],
                             [g_pmlp, P["pre_mix_norm"][l + 1].reshape(1, D_MODEL)], [(D_MODEL, F32), (D_MODEL, BF16)])
        else:
            def loss_fn(x2, f2, tgt, g1):
                diff = x2 + _rms_fwd(f2, g1) - tgt
                return [diff * (1.0 / D_MODEL)], [jnp.sum(diff * diff, axis=0, keepdims=True)]

            dxl, loss_cols = _rowwise("loss", loss_fn, [_row(S["x2"]), _row(S["f2"]), _row(target)], [g_pmlp],
                                      [(D_MODEL, F32)], [(1, D_MODEL)])
    loss_part = 0.5 * jnp.sum(loss_cols) / D_MODEL

    grads = [None] * depth
    dx = dxl
    zero = None
    for l in reversed(range(depth)):
        dx, grads[l] = _layer_bwd(saved[l], dx, W, P, l, bias4, onehot_t, zero)
        if grads_ready is not None and l == depth - 1 and depth > 1:
            zero = grads_ready(grads[l])
    grad_x = dx

    full = {n: jnp.stack([grads[l][n] for l in range(depth)]) for n in ALL_W if n != "rel_bias_table"}
    full["rel_bias_table"] = sum(grads[l]["rel_bias_table"] for l in range(depth))
    return loss_part, grad_x, full


def kernel(x, pre_mix_norm, w_in, b_gate, conv_w, conv_b, dt_bias, a_log, d_skip, ssd_norm, w_ssd_out, attn_sink, rel_bias_table, w_attn_out, w_o, post_mix_norm, pre_mlp_norm, w_mlp_in, w_mlp_out, post_mlp_norm, loss_target, m_pre_mix_norm, m_w_in, m_b_gate, m_conv_w, m_conv_b, m_dt_bias, m_a_log, m_d_skip, m_ssd_norm, m_w_ssd_out, m_attn_sink, m_rel_bias_table, m_w_attn_out, m_w_o, m_post_mix_norm, m_pre_mlp_norm, m_w_mlp_in, m_w_mlp_out, m_post_mlp_norm, v_pre_mix_norm, v_w_in, v_b_gate, v_conv_w, v_conv_b, v_dt_bias, v_a_log, v_d_skip, v_ssd_norm, v_w_ssd_out, v_attn_sink, v_rel_bias_table, v_w_attn_out, v_w_o, v_post_mix_norm, v_pre_mlp_norm, v_w_mlp_in, v_w_mlp_out, v_post_mlp_norm):
    a = locals()
    shard_w = {n: a[n] for n in ALL_W}
    shard_m = {n: a["m_" + n] for n in ALL_W}
    shard_v = {n: a["v_" + n] for n in ALL_W}
    loss, grad_x, outs = _step(x[0], loss_target[0], shard_w, shard_m, shard_v)
    return (loss, grad_x[None], *[outs["grad"][n] for n in ALL_W], *[outs["delta"][n] for n in ALL_W],
            *[outs["new_m"][n] for n in ALL_W], *[outs["new_v"][n] for n in ALL_W])
```

```python
import math

import jax
import jax.numpy as jnp
from jax import lax
from jax.experimental import pallas as pl
from jax.experimental.pallas import tpu as pltpu

F32 = jnp.float32
BF16 = jnp.bfloat16
MESH = pl.DeviceIdType.MESH

VMEM_LIMIT_BYTES = 52 * 1024 * 1024
LANES = 128
SUBLANES = 8

EPS = 1e-6
D_MODEL = 1024
D_INNER = 2048
SSD_HEADS = 32
SSD_HEAD_DIM = 64
SSD_GROUPS = 8
SSD_STATE = 128
SSD_CONV = 5
CHUNK = 128
CONV_DIM = 4096
ATTN_HEADS = 16
ATTN_KV = 4
ATTN_DIM = 64
BLOCK = 128
N_BUCKETS = 32
MAX_DISTANCE = 128
D_FF = 4096
N_IN = 9792
NEG = -1e30

OFF_Z, OFF_G, OFF_XBC, OFF_DT, N_MAIN, N_PROJ = 0, 2048, 4096, 8192, 8320, 10368
N_QKV = N_PROJ - N_MAIN

ADAM_LR, ADAM_B1, ADAM_B2, ADAM_EPS, ADAM_WD, ADAM_STEP = 0.001, 0.9, 0.999, 1e-08, 0.01, 10

BIG_ROWS = 6656
SMALL_ROWS = 32


def _cparams(*sem):
    return pltpu.CompilerParams(dimension_semantics=sem, vmem_limit_bytes=VMEM_LIMIT_BYTES)


def _dot(a, b, precision=None):
    return lax.dot_general(a, b, (((1,), (0,)), ((), ())), preferred_element_type=F32, precision=precision)


def _dot_nt(a, b, precision=None):
    return lax.dot_general(a, b, (((1,), (1,)), ((), ())), preferred_element_type=F32, precision=precision)


def _dot_tn(a, b):
    return lax.dot_general(a, b, (((0,), (0,)), ((), ())), preferred_element_type=F32)


def _pick(n, prefs):
    for p in prefs:
        if n % p == 0:
            return p
    return n


def _matmul(name, a, b, mode, out_dtype=F32, epilogue=None, extras=()):
    if mode == "nn":
        (m, k), (_, n) = a.shape, b.shape
    elif mode == "nt":
        (m, k), (n, _) = a.shape, b.shape
    else:
        (k, m), (_, n) = a.shape, b.shape
    tm = _pick(m, (512, 256, 128)) if mode == "tn" else _pick(m, (1024, 512, 256, 128))
    tn = _pick(n, (1024, 1152, 1664, 512, 256, 128))
    tk = _pick(k, (4096, 2048, 1024, 3456, 512, 256, 128)) if mode != "tn" else _pick(k, (4096, 512, 256, 128))
    if tk > 2048:
        tm = _pick(m, (512, 256, 128))
    nk = k // tk
    if mode == "nn":
        a_spec = pl.BlockSpec((tm, tk), lambda i, j, q: (i, q))
        b_spec = pl.BlockSpec((tk, tn), lambda i, j, q: (q, j))
        fn = _dot
    elif mode == "nt":
        a_spec = pl.BlockSpec((tm, tk), lambda i, j, q: (i, q))
        b_spec = pl.BlockSpec((tn, tk), lambda i, j, q: (j, q))
        fn = _dot_nt
    else:
        a_spec = pl.BlockSpec((tk, tm), lambda i, j, q: (q, i))
        b_spec = pl.BlockSpec((tk, tn), lambda i, j, q: (q, j))
        fn = _dot_tn

    tile = pl.BlockSpec((tm, tn), lambda i, j, q: (i, j))
    n_ex = len(extras)

    def body(a_ref, b_ref, *rest):
        ex_refs, o_ref = rest[:n_ex], rest[n_ex]

        def store(acc):
            v = acc if epilogue is None else epilogue(acc, *[r[...] for r in ex_refs])
            o_ref[...] = v.astype(o_ref.dtype)

        p = fn(a_ref[...].astype(BF16), b_ref[...].astype(BF16))
        if nk == 1:
            store(p)
        else:
            acc_ref = rest[n_ex + 1]
            q = pl.program_id(2)

            @pl.when(q == 0)
            def _():
                acc_ref[...] = p

            @pl.when((q > 0) & (q < nk - 1))
            def _():
                acc_ref[...] += p

            @pl.when(q == nk - 1)
            def _():
                store(acc_ref[...] + p)

    return pl.pallas_call(
        body, name=name, grid=(m // tm, n // tn, nk),
        in_specs=[a_spec, b_spec] + [tile] * n_ex, out_specs=tile,
        out_shape=jax.ShapeDtypeStruct((m, n), out_dtype),
        scratch_shapes=[pltpu.VMEM((tm, tn), F32)] if nk > 1 else [],
        compiler_params=_cparams("parallel", "parallel", "arbitrary"),
    )(a, b, *extras)


def _row(arr, width=None, cb=0, lead=None):
    return (arr, width, cb, lead)


def _rowwise(name, fn, rows, vecs, outs, accs=(), tb=256):
    t = rows[0][0].shape[-2]
    tb = min(tb, t)
    in_specs, args = [], []
    for arr, width, cb, lead in rows:
        w = arr.shape[-1] if width is None else width
        if lead is None:
            in_specs.append(pl.BlockSpec((tb, w), lambda i, cb=cb: (i, cb)))
        else:
            in_specs.append(pl.BlockSpec((None, tb, w), lambda i, cb=cb, lead=lead: (lead, i, cb)))
        args.append(arr)
    for v in vecs:
        in_specs.append(pl.BlockSpec(v.shape, lambda i, nd=v.ndim: (0,) * nd))
        args.append(v)
    out_shape = [jax.ShapeDtypeStruct((t, c), dt) for c, dt in outs] + [jax.ShapeDtypeStruct(s, F32) for s in accs]
    out_specs = [pl.BlockSpec((tb, c), lambda i: (i, 0)) for c, _ in outs] + [pl.BlockSpec(s, lambda i: (0, 0)) for s in accs]
    n_in, n_out = len(args), len(outs)

    def body(*refs):
        vals = [r[...] for r in refs[:n_in]]
        o_vals, a_vals = fn(*vals)
        for r, v in zip(refs[n_in:n_in + n_out], o_vals):
            r[...] = v.astype(r.dtype)
        first = pl.program_id(0) == 0
        for r, v in zip(refs[n_in + n_out:], a_vals):
            @pl.when(first)
            def _(r=r, v=v):
                r[...] = v

            @pl.when(jnp.logical_not(first))
            def _(r=r, v=v):
                r[...] += v

    res = pl.pallas_call(
        body, name=name, grid=(t // tb,), in_specs=in_specs, out_specs=out_specs, out_shape=out_shape,
        compiler_params=_cparams("arbitrary"),
    )(*args)
    return res


def _rms_fwd(x, g):
    r = lax.rsqrt(jnp.mean(x * x, axis=-1, keepdims=True) + EPS)
    return x * r * g


def _rms_bwd(x, g, dy):
    r = lax.rsqrt(jnp.mean(x * x, axis=-1, keepdims=True) + EPS)
    xh = x * r
    dxh = dy * g
    dx = r * (dxh - xh * jnp.mean(dxh * xh, axis=-1, keepdims=True))
    return dx, jnp.sum(dy * xh, axis=0, keepdims=True)


def _silu(x):
    return x * jax.nn.sigmoid(x)


def _silu_grad(x):
    s = jax.nn.sigmoid(x)
    return s * (1.0 + x * (1.0 - s))


GROUP_W = D_INNER // SSD_GROUPS


def _gated_norm_fwd(y, z, w):
    u = y * _silu(z)
    parts = []
    for j in range(SSD_GROUPS):
        ug = u[:, j * GROUP_W:(j + 1) * GROUP_W]
        parts.append(ug * lax.rsqrt(jnp.mean(ug * ug, axis=-1, keepdims=True) + EPS))
    return jnp.concatenate(parts, axis=-1) * w


def _gated_norm_bwd(y, z, w, dyn):
    sz = _silu(z)
    u = y * sz
    duh = dyn * w
    du_parts, uh_parts = [], []
    for j in range(SSD_GROUPS):
        sl = slice(j * GROUP_W, (j + 1) * GROUP_W)
        ug = u[:, sl]
        r = lax.rsqrt(jnp.mean(ug * ug, axis=-1, keepdims=True) + EPS)
        uh = ug * r
        dg = duh[:, sl]
        du_parts.append(r * (dg - uh * jnp.mean(dg * uh, axis=-1, keepdims=True)))
        uh_parts.append(uh)
    du = jnp.concatenate(du_parts, axis=-1)
    uh = jnp.concatenate(uh_parts, axis=-1)
    dw = jnp.sum(dyn * uh, axis=0, keepdims=True)
    return du * sz, du * y * _silu_grad(z), dw


HALO = SUBLANES


def _halo_specs(tb, cb, col0, t):
    nblk8 = t // HALO
    per = tb // HALO
    main = pl.BlockSpec((tb, cb), lambda j, i: (i, col0 + j))
    prev = pl.BlockSpec((HALO, cb), lambda j, i: (jnp.maximum(i * per - 1, 0), col0 + j))
    nxt = pl.BlockSpec((HALO, cb), lambda j, i: (jnp.minimum((i + 1) * per, nblk8 - 1), col0 + j))
    return main, prev, nxt


def _fill_ext(ext_ref, cur_ref, prev_ref, next_ref, tb, ni):
    i = pl.program_id(1)
    ext_ref[0:HALO, :] = jnp.where(i > 0, prev_ref[...], 0.0)
    ext_ref[HALO:HALO + tb, :] = cur_ref[...]
    ext_ref[HALO + tb:HALO + tb + HALO, :] = jnp.where(i < ni - 1, next_ref[...], 0.0)


def _conv_fwd(proj, w, b):
    t = proj.shape[0]
    tb, cb = min(512, t), 512
    ni, nj = t // tb, CONV_DIM // cb
    main, prev, nxt = _halo_specs(tb, cb, OFF_XBC // cb, t)
    pad = (SSD_CONV - 1) // 2

    def body(u_ref, up_ref, un_ref, w_ref, b_ref, pre_ref, act_ref, ext_ref):
        _fill_ext(ext_ref, u_ref, up_ref, un_ref, tb, ni)
        acc = jnp.broadcast_to(b_ref[...], (tb, cb))
        for k in range(SSD_CONV):
            acc = acc + w_ref[k:k + 1, :] * ext_ref[pl.ds(HALO + k - pad, tb), :]
        pre_ref[...] = acc
        act_ref[...] = _silu(acc)

    out = pl.BlockSpec((tb, cb), lambda j, i: (i, j))
    return pl.pallas_call(
        body, name="conv_fwd", grid=(nj, ni),
        in_specs=[main, prev, nxt, pl.BlockSpec((SSD_CONV, cb), lambda j, i: (0, j)), pl.BlockSpec((1, cb), lambda j, i: (0, j))],
        out_specs=[out, out],
        out_shape=[jax.ShapeDtypeStruct((t, CONV_DIM), F32)] * 2,
        scratch_shapes=[pltpu.VMEM((tb + 2 * HALO, cb), F32)],
        compiler_params=_cparams("parallel", "arbitrary"),
    )(proj, proj, proj, w, b)


def _conv_bwd(dpre, proj, w):
    t = proj.shape[0]
    tb, cb = min(512, t), 512
    ni, nj = t // tb, CONV_DIM // cb
    umain, uprev, unext = _halo_specs(tb, cb, OFF_XBC // cb, t)
    dmain, dprev, dnext = _halo_specs(tb, cb, 0, t)
    pad = (SSD_CONV - 1) // 2

    def body(d_ref, dp_ref, dn_ref, u_ref, up_ref, un_ref, w_ref, du_ref, dw_ref, db_ref, extd_ref, extu_ref):
        _fill_ext(extd_ref, d_ref, dp_ref, dn_ref, tb, ni)
        _fill_ext(extu_ref, u_ref, up_ref, un_ref, tb, ni)
        d = d_ref[...]
        du = jnp.zeros((tb, cb), F32)
        @pl.when(pl.program_id(1) == 0)
        def _():
            dw_ref[...] = jnp.zeros_like(dw_ref)
            db_ref[...] = jnp.zeros_like(db_ref)

        for k in range(SSD_CONV):
            du = du + w_ref[k:k + 1, :] * extd_ref[pl.ds(HALO - k + pad, tb), :]
            dw_ref[k:k + 1, :] += jnp.sum(d * extu_ref[pl.ds(HALO + k - pad, tb), :], axis=0, keepdims=True)
        du_ref[...] = du.astype(du_ref.dtype)
        db_ref[...] += jnp.sum(d, axis=0, keepdims=True)

    return pl.pallas_call(
        body, name="conv_bwd", grid=(nj, ni),
        in_specs=[dmain, dprev, dnext, umain, uprev, unext, pl.BlockSpec((SSD_CONV, cb), lambda j, i: (0, j))],
        out_specs=[pl.BlockSpec((tb, cb), lambda j, i: (i, j)), pl.BlockSpec((SSD_CONV, cb), lambda j, i: (0, j)),
                   pl.BlockSpec((1, cb), lambda j, i: (0, j))],
        out_shape=[jax.ShapeDtypeStruct((t, CONV_DIM), BF16), jax.ShapeDtypeStruct((SSD_CONV, CONV_DIM), F32),
                   jax.ShapeDtypeStruct((1, CONV_DIM), F32)],
        scratch_shapes=[pltpu.VMEM((tb + 2 * HALO, cb), F32)] * 2,
        compiler_params=_cparams("parallel", "arbitrary"),
    )(dpre, dpre, dpre, proj, proj, proj, w)


PAIR = 2 * SSD_HEAD_DIM
HI = lax.Precision.HIGHEST


def _ssd_prelude(d, dt_ref, dtt_ref, ar_ref, ac_ref):
    li = lax.broadcasted_iota(jnp.int32, (CHUNK, CHUNK), 0)
    si = lax.broadcasted_iota(jnp.int32, (CHUNK, CHUNK), 1)
    fwd = d == 0
    hi, lo = jnp.where(fwd, li, si), jnp.where(fwd, si, li)
    tri = hi >= lo
    trif = tri.astype(F32)
    trit = (hi <= lo).astype(F32)
    dt = dt_ref[...]
    adt = dt * ar_ref[...]
    adtt = dtt_ref[...] * ac_ref[...]
    p = _dot(trif, adt, HI)
    pt = _dot_nt(adtt, trif, HI)
    tot = jnp.sum(adt, axis=0, keepdims=True)
    return tri, trit, dt, p, pt, tot


def _ssd_specs(nc, rev):
    def cidx(d, c):
        up = (d == 1) if rev else (d == 0)
        return jnp.where(up, c, nc - 1 - c)

    specs = [
        pl.BlockSpec((CHUNK, D_INNER), lambda d, c: (cidx(d, c), 0)),
        pl.BlockSpec((CHUNK, 1024), lambda d, c: (cidx(d, c), 2)),
        pl.BlockSpec((CHUNK, 1024), lambda d, c: (cidx(d, c), 3)),
        pl.BlockSpec((None, CHUNK, SSD_HEADS), lambda d, c: (d, cidx(d, c), 0)),
        pl.BlockSpec((None, SSD_HEADS, CHUNK), lambda d, c: (d, 0, cidx(d, c))),
        pl.BlockSpec((None, 1, SSD_HEADS), lambda d, c: (d, 0, 0)),
        pl.BlockSpec((None, SSD_HEADS, 1), lambda d, c: (d, 0, 0)),
        pl.BlockSpec((CHUNK, D_INNER), lambda d, c: (cidx(d, c), d)),
    ]
    return cidx, specs


def _head_decay(tri, p, pt, tot, h):
    pb = jnp.broadcast_to(p[:, h:h + 1], (CHUNK, CHUNK))
    dec = jnp.exp(jnp.where(tri, pb - pt[h:h + 1, :], NEG))
    return dec, jnp.exp(tot[:, h:h + 1] - pb), jnp.exp(pb)


def _ssd_fwd(act, dt2, dt2t, a_row, a_col, dte):
    t = act.shape[0]
    nc = t // CHUNK
    cidx, specs = _ssd_specs(nc, rev=False)

    def body(xs_ref, bs_ref, cs_ref, dt_ref, dtt_ref, ar_ref, ac_ref, dte_ref, y_ref, st_ref, h_ref):
        d, c = pl.program_id(0), pl.program_id(1)

        @pl.when(c == 0)
        def _():
            h_ref[...] = jnp.zeros_like(h_ref)

        st_ref[...] = h_ref[...]
        tri, _, _, p, pt, tot = _ssd_prelude(d, dt_ref, dtt_ref, ar_ref, ac_ref)
        etot = jnp.exp(tot)
        lane = lax.broadcasted_iota(jnp.int32, (CHUNK, PAIR), 1) < SSD_HEAD_DIM
        rowh = lax.broadcasted_iota(jnp.int32, (PAIR, SSD_STATE), 0) < SSD_HEAD_DIM
        for g in range(SSD_GROUPS):
            gs = slice(g * SSD_STATE, (g + 1) * SSD_STATE)
            bg = bs_ref[:, gs]
            cb = cs_ref[:, gs].astype(BF16)
            cbm = _dot_nt(cb, bg.astype(BF16))
            for pr in range(2):
                h0 = g * 4 + pr * 2
                h1 = h0 + 1
                sl = slice(h0 * SSD_HEAD_DIM, h0 * SSD_HEAD_DIM + PAIR)
                xdt = (xs_ref[:, sl] * dte_ref[:, sl]).astype(BF16)
                yd, st, epb = [], [], []
                for h in (h0, h1):
                    dec, wb, eb = _head_decay(tri, p, pt, tot, h)
                    yd.append(_dot((cbm * dec).astype(BF16), xdt))
                    st.append(_dot_tn(xdt, (bg * wb).astype(BF16)))
                    epb.append(eb)
                hin = h_ref[sl, :]
                yo = _dot_nt(cb, hin.astype(BF16)) * jnp.where(lane, epb[0], epb[1])
                y_ref[:, sl] = jnp.where(lane, yd[0], yd[1]) + yo
                et = jnp.where(rowh, etot[:, h0:h0 + 1], etot[:, h1:h1 + 1])
                h_ref[sl, :] = hin * et + jnp.where(rowh, st[0], st[1])

    return pl.pallas_call(
        body, name="ssd_fwd", grid=(2, nc), in_specs=specs,
        out_specs=[pl.BlockSpec((None, CHUNK, D_INNER), lambda d, c: (d, cidx(d, c), 0)),
                   pl.BlockSpec((None, None, D_INNER, SSD_STATE), lambda d, c: (d, cidx(d, c), 0, 0))],
        out_shape=[jax.ShapeDtypeStruct((2, t, D_INNER), F32), jax.ShapeDtypeStruct((2, nc, D_INNER, SSD_STATE), F32)],
        scratch_shapes=[pltpu.VMEM((D_INNER, SSD_STATE), F32)],
        compiler_params=_cparams("arbitrary", "arbitrary"),
    )(act, act, act, dt2, dt2t, a_row, a_col, dte)


def _put_col(acc, col, h):
    lane = lax.broadcasted_iota(jnp.int32, acc.shape, 1)
    return jnp.where(lane == h, col, acc)


def _put_row(acc, row, h):
    sub = lax.broadcasted_iota(jnp.int32, acc.shape, 0)
    return jnp.where(sub == h, row, acc)


def _sum_all(x):
    return jnp.sum(jnp.sum(x, axis=0, keepdims=True), axis=1, keepdims=True)


def _ssd_bwd(act, dt2, dt2t, a_row, a_col, dte, dy, states):
    t = act.shape[0]
    nc = t // CHUNK
    cidx, specs = _ssd_specs(nc, rev=True)
    specs = specs + [
        pl.BlockSpec((CHUNK, D_INNER), lambda d, c: (cidx(d, c), 0)),
        pl.BlockSpec((None, None, D_INNER, SSD_STATE), lambda d, c: (d, cidx(d, c), 0, 0)),
    ]

    def body(xs_ref, bs_ref, cs_ref, dt_ref, dtt_ref, ar_ref, ac_ref, dte_ref, dy_ref, st_ref,
             dxs_ref, dbs_ref, dcs_ref, ddt_ref, da_ref, dh_ref):
        d, c = pl.program_id(0), pl.program_id(1)

        @pl.when(c == 0)
        def _():
            dh_ref[...] = jnp.zeros_like(dh_ref)
            da_ref[...] = jnp.zeros_like(da_ref)

        tri, trit, dt, p, pt, tot = _ssd_prelude(d, dt_ref, dtt_ref, ar_ref, ac_ref)
        etot = jnp.exp(tot)
        lane = lax.broadcasted_iota(jnp.int32, (CHUNK, PAIR), 1) < SSD_HEAD_DIM
        rowh = lax.broadcasted_iota(jnp.int32, (PAIR, SSD_STATE), 0) < SSD_HEAD_DIM
        first_head = lax.broadcasted_iota(jnp.int32, (PAIR, LANES), 0) < SSD_HEAD_DIM
        out_lane = lax.broadcasted_iota(jnp.int32, (PAIR, LANES), 1)
        ddtx = jnp.zeros((CHUNK, LANES), F32)
        lane32 = lax.broadcasted_iota(jnp.int32, (CHUNK, SSD_HEADS), 1)
        dp_col = jnp.zeros((CHUNK, SSD_HEADS), F32)
        dp_row = jnp.zeros((SSD_HEADS, CHUNK), F32)
        dtot = jnp.zeros((1, SSD_HEADS), F32)
        for g in range(SSD_GROUPS):
            gs = slice(g * SSD_STATE, (g + 1) * SSD_STATE)
            bg = bs_ref[:, gs]
            bb = bg.astype(BF16)
            cb = cs_ref[:, gs].astype(BF16)
            cbm = _dot_nt(cb, bb)
            dcb = jnp.zeros((CHUNK, CHUNK), F32)
            dc_acc = jnp.zeros((CHUNK, SSD_STATE), F32)
            db_acc = jnp.zeros((CHUNK, SSD_STATE), F32)
            for pr in range(2):
                h0 = g * 4 + pr * 2
                h1 = h0 + 1
                sl = slice(h0 * SSD_HEAD_DIM, h0 * SSD_HEAD_DIM + PAIR)
                xp = xs_ref[:, sl]
                dtp = dte_ref[:, sl]
                xdt_f = xp * dtp
                xdt = xdt_f.astype(BF16)
                dyp = dy_ref[:, sl]
                dyb = dyp.astype(BF16)
                hin = st_ref[sl, :]
                dh = dh_ref[sl, :]
                hb = hin.astype(BF16)
                dhb = dh.astype(BF16)
                heads = [_head_decay(tri, p, pt, tot, h) for h in (h0, h1)]
                dye = dyp * jnp.where(lane, heads[0][2], heads[1][2])
                dyeb = dye.astype(BF16)
                gy = _dot_nt(cb, hb) * dye
                dc_acc = dc_acc + _dot(dyeb, hb)
                dhin = _dot_tn(dyeb, cb)
                hh = dh * hin
                dxdt = jnp.zeros((CHUNK, PAIR), F32)
                for idx, h in enumerate((h0, h1)):
                    hm = lane if idx == 0 else jnp.logical_not(lane)
                    rm = rowh if idx == 0 else jnp.logical_not(rowh)
                    dec, wb, _ = heads[idx]
                    mf = cbm * dec
                    t1 = _dot_tn(mf.astype(BF16), dyb)
                    t2 = _dot_nt((bg * wb).astype(BF16), dhb)
                    dxdt = jnp.where(hm, t1 + t2, dxdt)
                    dm = _dot_nt(jnp.where(hm, dyp, 0.0).astype(BF16), xdt)
                    dcb = dcb + dm * dec
                    e = dm * mf
                    qw = _dot(jnp.where(hm, xdt_f, 0.0).astype(BF16), dhb) * wb
                    db_acc = db_acc + qw
                    qwb = qw * bg
                    col = jnp.sum(e + jnp.where(hm, gy, 0.0) - qwb, axis=1, keepdims=True)
                    dp_col = jnp.where(lane32 == h, col, dp_col)
                    dp_row = _put_row(dp_row, -jnp.sum(e, axis=0, keepdims=True), h)
                    dtot_h = _sum_all(qwb) + etot[:, h:h + 1] * _sum_all(jnp.where(rm, hh, 0.0))
                    dtot = _put_col(dtot, dtot_h, h)
                dxs_ref[:, sl] = dxdt * dtp
                ddx = dxdt * xp
                ddx_hi = ddx.astype(BF16)
                ddx_lo = (ddx - ddx_hi.astype(F32)).astype(BF16)
                route = (out_lane == jnp.where(first_head, h0, h1)).astype(BF16)
                ddtx = ddtx + _dot(ddx_hi, route) + _dot(ddx_lo, route)
                et = jnp.where(rowh, etot[:, h0:h0 + 1], etot[:, h1:h1 + 1])
                dh_ref[sl, :] = dh * et + dhin
            dcbb = dcb.astype(BF16)
            dcs_ref[:, gs] = _dot(dcbb, bb) + dc_acc
            dbs_ref[:, gs] = _dot_tn(dcbb, cb) + db_acc
        d_adt = _dot(trit, dp_col, HI) + _dot_nt(trit, dp_row, HI) + dtot
        ddt_ref[...] = ddtx[:, :SSD_HEADS] + ar_ref[...] * d_adt
        da_ref[...] += jnp.sum(dt * d_adt, axis=0, keepdims=True)

    return pl.pallas_call(
        body, name="ssd_bwd", grid=(2, nc), in_specs=specs,
        out_specs=[pl.BlockSpec((None, CHUNK, D_INNER), lambda d, c: (d, cidx(d, c), 0)),
                   pl.BlockSpec((None, CHUNK, 1024), lambda d, c: (d, cidx(d, c), 0)),
                   pl.BlockSpec((None, CHUNK, 1024), lambda d, c: (d, cidx(d, c), 0)),
                   pl.BlockSpec((None, CHUNK, SSD_HEADS), lambda d, c: (d, cidx(d, c), 0)),
                   pl.BlockSpec((None, 1, SSD_HEADS), lambda d, c: (d, 0, 0))],
        out_shape=[jax.ShapeDtypeStruct((2, t, D_INNER), F32), jax.ShapeDtypeStruct((2, t, 1024), F32),
                   jax.ShapeDtypeStruct((2, t, 1024), F32), jax.ShapeDtypeStruct((2, t, SSD_HEADS), F32),
                   jax.ShapeDtypeStruct((2, 1, SSD_HEADS), F32)],
        scratch_shapes=[pltpu.VMEM((D_INNER, SSD_STATE), F32)],
        compiler_params=_cparams("arbitrary", "arbitrary"),
    )(act, act, act, dt2, dt2t, a_row, a_col, dte, dy, states)


REP = ATTN_HEADS // ATTN_KV
SCALE = ATTN_DIM ** -0.5
GROUP_Q = REP * ATTN_DIM
K_BLK0 = D_MODEL // LANES
V_BLK0 = K_BLK0 + ATTN_KV


def _attn_specs(nb):
    q = pl.BlockSpec((BLOCK, GROUP_Q), lambda g, n: (n, g))

    def kv(blk0):
        return [pl.BlockSpec((BLOCK, LANES), lambda g, n: (jnp.maximum(n - 1, 0), blk0 + g)),
                pl.BlockSpec((BLOCK, LANES), lambda g, n: (n, blk0 + g)),
                pl.BlockSpec((BLOCK, LANES), lambda g, n: (jnp.minimum(n + 1, nb - 1), blk0 + g))]

    bias = pl.BlockSpec((None, REP, BLOCK, 3 * BLOCK), lambda g, n: (g, 0, 0, 0))
    sink = pl.BlockSpec((None, REP, 1, LANES), lambda g, n: (g, 0, 0, 0))
    return q, kv(K_BLK0), kv(V_BLK0), bias, sink


def _band_mask():
    ii = lax.broadcasted_iota(jnp.int32, (BLOCK, 3 * BLOCK), 0)
    jj = lax.broadcasted_iota(jnp.int32, (BLOCK, 3 * BLOCK), 1)
    return (jj >= ii) & (jj - 2 * BLOCK <= ii)


def _attn_valid(n, nb):
    jj = lax.broadcasted_iota(jnp.int32, (1, 3 * BLOCK), 1)
    return ((jj >= BLOCK) | (n > 0)) & ((jj < 2 * BLOCK) | (n < nb - 1))


def _attn_probs(q, kcat, bias, snk, valid):
    s = jnp.where(valid, _dot_nt(q, kcat) + bias, NEG)
    m = jnp.maximum(jnp.max(s, axis=1, keepdims=True), snk)
    p = jnp.exp(s - m)
    es = jnp.exp(snk - m)
    r = 1.0 / (jnp.sum(p, axis=1, keepdims=True) + es)
    return p * r, es * r


def _stack_heads(ref, lane):
    parts = []
    for pr in range(REP // 2):
        tile = ref[:, pr * LANES:(pr + 1) * LANES]
        parts += [jnp.where(lane, tile, 0.0), jnp.where(lane, 0.0, tile)]
    return jnp.concatenate(parts, axis=0)


def _unstack_heads(x, lane):
    return jnp.concatenate([jnp.where(lane, x[(2 * pr) * BLOCK:(2 * pr + 1) * BLOCK], x[(2 * pr + 1) * BLOCK:(2 * pr + 2) * BLOCK])
                            for pr in range(REP // 2)], axis=1)


def _stack_bias(b_ref, s_ref):
    bias = jnp.concatenate([b_ref[r] for r in range(REP)], axis=0)
    snk = jnp.concatenate([jnp.broadcast_to(s_ref[r][:, 0:1], (BLOCK, 1)) for r in range(REP)], axis=0)
    return bias, snk


def _attn_fwd(qkv, bias4, sink4):
    t = qkv.shape[0]
    nb = t // BLOCK
    qs, ks, vs, bs, ss = _attn_specs(nb)

    def body(q_ref, kp_ref, kc_ref, kn_ref, vp_ref, vc_ref, vn_ref, b_ref, s_ref, o_ref):
        n = pl.program_id(1)
        kcat = jnp.concatenate([kp_ref[...], kc_ref[...], kn_ref[...]], axis=0)
        vcat = jnp.concatenate([vp_ref[...], vc_ref[...], vn_ref[...]], axis=0)
        valid = _attn_valid(n, nb)
        lane = lax.broadcasted_iota(jnp.int32, (BLOCK, LANES), 1) < ATTN_DIM
        bias, snk = _stack_bias(b_ref, s_ref)
        pn, _ = _attn_probs(_stack_heads(q_ref, lane) * SCALE, kcat, bias, snk, valid)
        o_ref[...] = _unstack_heads(_dot(pn.astype(BF16), vcat), lane).astype(o_ref.dtype)

    return pl.pallas_call(
        body, name="attn_fwd", grid=(ATTN_KV, nb), in_specs=[qs] + ks + vs + [bs, ss],
        out_specs=qs, out_shape=jax.ShapeDtypeStruct((t, D_MODEL), BF16),
        compiler_params=_cparams("parallel", "arbitrary"),
    )(qkv, qkv, qkv, qkv, qkv, qkv, qkv, bias4, sink4)


def _attn_bwd(qkv, bias4, sink4, do):
    t = qkv.shape[0]
    nb = t // BLOCK
    qs, ks, vs, bs, ss = _attn_specs(nb)
    part = pl.BlockSpec((3, BLOCK, LANES), lambda g, n: (0, n, g))

    def body(q_ref, kp_ref, kc_ref, kn_ref, vp_ref, vc_ref, vn_ref, b_ref, s_ref, do_ref,
             dq_ref, dk_ref, dv_ref, db_ref, ds_ref):
        n = pl.program_id(1)

        @pl.when(n == 0)
        def _():
            db_ref[...] = jnp.zeros_like(db_ref)
            ds_ref[...] = jnp.zeros_like(ds_ref)

        kcat = jnp.concatenate([kp_ref[...], kc_ref[...], kn_ref[...]], axis=0)
        vcat = jnp.concatenate([vp_ref[...], vc_ref[...], vn_ref[...]], axis=0)
        valid = _attn_valid(n, nb)
        lane = lax.broadcasted_iota(jnp.int32, (BLOCK, LANES), 1) < ATTN_DIM
        bias, snk = _stack_bias(b_ref, s_ref)
        q = _stack_heads(q_ref, lane)
        do = _stack_heads(do_ref, lane)
        pn, psink = _attn_probs(q * SCALE, kcat, bias, snk, valid)
        dp = _dot_nt(do, vcat)
        delta = jnp.sum(pn * dp, axis=1, keepdims=True)
        dsc = pn * (dp - delta)
        dsink = psink * delta
        for r in range(REP):
            rows = slice(r * BLOCK, (r + 1) * BLOCK)
            db_ref[r] += dsc[rows]
            ds_ref[r] += jnp.broadcast_to(-jnp.sum(dsink[rows], axis=0, keepdims=True), (1, LANES))
        dsb = (dsc * SCALE).astype(BF16)
        dq_ref[...] = _unstack_heads(_dot(dsb, kcat), lane).astype(dq_ref.dtype)
        dk = _dot_tn(dsb, q)
        dv = _dot_tn(pn.astype(BF16), do)
        for j in range(3):
            dk_ref[j] = dk[j * BLOCK:(j + 1) * BLOCK]
            dv_ref[j] = dv[j * BLOCK:(j + 1) * BLOCK]

    kv_cols = ATTN_KV * LANES
    return pl.pallas_call(
        body, name="attn_bwd", grid=(ATTN_KV, nb), in_specs=[qs] + ks + vs + [bs, ss, qs],
        out_specs=[qs, part, part, bs, ss],
        out_shape=[jax.ShapeDtypeStruct((t, D_MODEL), BF16), jax.ShapeDtypeStruct((3, t, kv_cols), F32),
                   jax.ShapeDtypeStruct((3, t, kv_cols), F32), jax.ShapeDtypeStruct(bias4.shape, F32),
                   jax.ShapeDtypeStruct(sink4.shape, F32)],
        compiler_params=_cparams("parallel", "arbitrary"),
    )(qkv, qkv, qkv, qkv, qkv, qkv, qkv, bias4, sink4, do)


def _kv_combine(name, parts):
    _, t, cols = parts.shape

    def body(p_ref, o_ref):
        z = jnp.zeros((BLOCK, LANES), F32)
        from_next = jnp.concatenate([p_ref[0, BLOCK:, :], z], axis=0)
        from_prev = jnp.concatenate([z, p_ref[2, :t - BLOCK, :]], axis=0)
        o_ref[...] = (from_next + p_ref[1] + from_prev).astype(o_ref.dtype)

    return pl.pallas_call(
        body, name=name, grid=(cols // LANES,),
        in_specs=[pl.BlockSpec((3, t, LANES), lambda g: (0, 0, g))],
        out_specs=pl.BlockSpec((t, LANES), lambda g: (0, g)),
        out_shape=jax.ShapeDtypeStruct((t, cols), BF16),
        compiler_params=_cparams("parallel"),
    )(parts)


def _t5_bucket(rel):
    nb = N_BUCKETS // 2
    max_exact = nb // 2
    ret = jnp.where(rel > 0, nb, 0)
    n = jnp.abs(rel)
    nf = jnp.maximum(n, 1).astype(jnp.float32)
    large = max_exact + (jnp.log(nf / max_exact) / math.log(MAX_DISTANCE / max_exact) * (nb - max_exact)).astype(jnp.int32)
    large = jnp.minimum(large, nb - 1)
    return ret + jnp.where(n < max_exact, n, large)


def _bucket_map():
    i = jnp.arange(BLOCK)[:, None]
    j = jnp.arange(3 * BLOCK)[None, :]
    return _t5_bucket(j - BLOCK - i)


def _bias_from_table(table, onehot_t):
    def body(t_ref, o_ref, out_ref):
        out_ref[...] = _dot(t_ref[...], o_ref[...], HI)

    return pl.pallas_call(body, name="bias_from_table", out_shape=jax.ShapeDtypeStruct((ATTN_HEADS, onehot_t.shape[1]), F32),
                          compiler_params=pltpu.CompilerParams(vmem_limit_bytes=VMEM_LIMIT_BYTES))(table.T, onehot_t)


def _bias_table_grad(dbias, onehot_t):
    def body(d_ref, o_ref, out_ref):
        out_ref[...] = _dot_nt(d_ref[...], o_ref[...], HI)

    return pl.pallas_call(body, name="bias_table_grad", out_shape=jax.ShapeDtypeStruct((ATTN_HEADS, N_BUCKETS), F32),
                          compiler_params=pltpu.CompilerParams(vmem_limit_bytes=VMEM_LIMIT_BYTES))(dbias, onehot_t)


HBM_SPEC = pl.BlockSpec(memory_space=pl.ANY)


def _comm_call(name, body, xs, out_shapes, n_sems):
    n = len(xs)
    return pl.pallas_call(
        body, name=name, in_specs=[HBM_SPEC] * n, out_specs=[HBM_SPEC] * n, out_shape=out_shapes,
        scratch_shapes=[pltpu.SemaphoreType.DMA((n * n_sems,)), pltpu.SemaphoreType.DMA((n * n_sems,))],
    )(*xs)


def _allgather_chips(name, xs):
    n = len(xs)

    def body(*refs):
        x_refs, out_refs, (send_sems, recv_sems) = refs[:n], refs[n:2 * n], refs[2 * n:]
        mx, my, mc = lax.axis_index("x"), lax.axis_index("y"), lax.axis_index("c")
        me = 2 * mx + my
        chips = [(1 - mx, my), (mx, 1 - my), (1 - mx, 1 - my)]
        sibling = (mx, my, 1 - mc)

        def part(i, slot, h):
            r2 = xs[i].shape[0] // 2
            return out_refs[i].at[slot, pl.ds(h * r2, r2)]

        def copy(i, k, src, dst, to):
            return pltpu.make_async_remote_copy(src_ref=src, dst_ref=dst, send_sem=send_sems.at[6 * i + k],
                                                recv_sem=recv_sems.at[6 * i + k], device_id=to, device_id_type=MESH)

        first = [copy(i, k, x_refs[i].at[pl.ds(mc * (xs[i].shape[0] // 2), xs[i].shape[0] // 2)], part(i, me, mc), (px, py, mc))
                 for i in range(n) for k, (px, py) in enumerate(chips)]
        for cp in first:
            cp.start()
        passed = []
        for k, (px, py) in enumerate(chips):
            for i in range(n):
                landed = part(i, 2 * px + py, mc)
                copy(i, k, landed, landed, (px, py, mc)).wait_recv()
                passed.append(copy(i, 3 + k, landed, landed, sibling))
                passed[-1].start()
        for k, (px, py) in enumerate(chips):
            for i in range(n):
                theirs = part(i, 2 * px + py, 1 - mc)
                copy(i, 3 + k, theirs, theirs, sibling).wait_recv()
        for cp in first + passed:
            cp.wait_send()

    return _comm_call(name, body, xs, [jax.ShapeDtypeStruct((4,) + x.shape, x.dtype) for x in xs], 6)


SEM_SPEC = pl.BlockSpec(memory_space=pltpu.SEMAPHORE)
DATAFLOW = pltpu.SideEffectType.DATAFLOW_SIDE_EFFECTING


def _send_start(name, xs, after, scatter, collective_id):
    n = len(xs)

    def body(*refs):
        x_refs, land_refs = refs[:n], refs[n:2 * n]
        send_sems, recv_sems, token = refs[2 * n + 1], refs[2 * n + 2], refs[-1]
        mx, my, mc = lax.axis_index("x"), lax.axis_index("y"), lax.axis_index("c")
        chips = [(1 - mx, my), (mx, 1 - my), (1 - mx, 1 - my)]
        barrier = pltpu.get_barrier_semaphore()
        for px, py in chips:
            pl.semaphore_signal(barrier, inc=1, device_id=(px, py, mc), device_id_type=MESH)
        pl.semaphore_wait(barrier, len(chips))
        for i in range(n):
            for k, (px, py) in enumerate(chips):
                src = x_refs[i].at[2 * px + py] if scatter else x_refs[i]
                pltpu.make_async_remote_copy(src_ref=src, dst_ref=land_refs[i].at[2 * mx + my], send_sem=send_sems.at[3 * i + k],
                                             recv_sem=recv_sems.at[3 * i + k], device_id=(px, py, mc), device_id_type=MESH).start()
        token[...] = jnp.zeros_like(token)

    lands = [lax.empty((4,) + (x.shape[1:] if scatter else x.shape), x.dtype) for x in xs]
    hbm = [pltpu.HBM(a.shape, a.dtype) for a in list(xs) + lands]
    out = pl.pallas_call(
        body, name=name,
        out_shape=(pltpu.SemaphoreType.DMA((3 * n,)), pltpu.SemaphoreType.DMA((3 * n,)), *hbm, jax.ShapeDtypeStruct((SUBLANES, LANES), F32)),
        in_specs=(HBM_SPEC,) * (2 * n + 1),
        out_specs=(SEM_SPEC, SEM_SPEC) + (HBM_SPEC,) * (2 * n) + (pl.BlockSpec(memory_space=pltpu.VMEM),),
        input_output_aliases={i: 2 + i for i in range(2 * n)},
        compiler_params=pltpu.CompilerParams(has_side_effects=DATAFLOW, collective_id=collective_id),
    )(*[pltpu.with_memory_space_constraint(a, pltpu.HBM) for a in list(xs) + lands], after)
    return out[0], out[1], list(out[2:2 + n]), list(out[2 + n:2 + 2 * n]), out[-1]


def _send_wait(name, send_sems, recv_sems, x_thrus, land_thrus, after, scatter):
    n = len(x_thrus)

    def body(*refs):
        x_refs, land_refs, send_sems, recv_sems = refs[:n], refs[n:2 * n], refs[2 * n], refs[2 * n + 1]
        mx, my, mc = lax.axis_index("x"), lax.axis_index("y"), lax.axis_index("c")
        chips = [(1 - mx, my), (mx, 1 - my), (1 - mx, 1 - my)]
        for i in range(n):
            for k, (px, py) in enumerate(chips):
                src = x_refs[i].at[0] if scatter else x_refs[i]
                copy = pltpu.make_async_remote_copy(src_ref=src, dst_ref=land_refs[i].at[2 * px + py], send_sem=send_sems.at[3 * i + k],
                                                    recv_sem=recv_sems.at[3 * i + k], device_id=(px, py, mc), device_id_type=MESH)
                copy.wait_send()
                copy.wait_recv()

    arrs = list(x_thrus) + list(land_thrus)
    out = pl.pallas_call(
        body, name=name, out_shape=tuple(pltpu.HBM(a.shape, a.dtype) for a in arrs),
        in_specs=(HBM_SPEC,) * (2 * n) + (SEM_SPEC, SEM_SPEC, HBM_SPEC), out_specs=(HBM_SPEC,) * (2 * n),
        input_output_aliases={i: i for i in range(2 * n)},
        compiler_params=pltpu.CompilerParams(has_side_effects=DATAFLOW),
    )(*arrs, send_sems, recv_sems, after)
    return list(out[n:])


def _swap_cores(name, xs, slab=True):
    n = len(xs)

    def body(*refs):
        x_refs, out_refs, (send_sems, recv_sems) = refs[:n], refs[n:2 * n], refs[2 * n:]
        mx, my, mc = lax.axis_index("x"), lax.axis_index("y"), lax.axis_index("c")
        sends = [pltpu.make_async_remote_copy(src_ref=x_refs[i].at[1 - mc] if slab else x_refs[i], dst_ref=out_refs[i],
                                              send_sem=send_sems.at[i], recv_sem=recv_sems.at[i],
                                              device_id=(mx, my, 1 - mc), device_id_type=MESH)
                 for i in range(n)]
        for cp in sends:
            cp.start()
        for cp in sends:
            cp.wait()

    return _comm_call(name, body, xs, [jax.ShapeDtypeStruct(x.shape[1:] if slab else x.shape, x.dtype) for x in xs], 1)


def _scatter_chips(name, gs):
    n = len(gs)

    def body(*refs):
        g_refs, out_refs, (send_sems, recv_sems) = refs[:n], refs[n:2 * n], refs[2 * n:]
        mx, my, mc = lax.axis_index("x"), lax.axis_index("y"), lax.axis_index("c")
        me = 2 * mx + my
        chips = [(1 - mx, my), (mx, 1 - my), (1 - mx, 1 - my)]

        def copy(i, k, src, dst, to):
            return pltpu.make_async_remote_copy(src_ref=src, dst_ref=dst, send_sem=send_sems.at[3 * i + k],
                                                recv_sem=recv_sems.at[3 * i + k], device_id=to, device_id_type=MESH)

        sends = [copy(i, k, g_refs[i].at[2 * px + py], out_refs[i].at[me], (px, py, mc))
                 for i in range(n) for k, (px, py) in enumerate(chips)]
        for cp in sends:
            cp.start()
        for i in range(n):
            for k, (px, py) in enumerate(chips):
                copy(i, k, g_refs[i].at[0], out_refs[i].at[2 * px + py], (px, py, mc)).wait_recv()
        for cp in sends:
            cp.wait_send()

    return _comm_call(name, body, gs, [jax.ShapeDtypeStruct(g.shape, g.dtype) for g in gs], 3)


def _allgather_all(name, x):
    def body(x_ref, out_ref, send_sems, recv_sems, local_sem):
        mx, my, mc = lax.axis_index("x"), lax.axis_index("y"), lax.axis_index("c")
        me = 4 * mx + 2 * my + mc
        flips = [(fx, fy, fc) for fx in (0, 1) for fy in (0, 1) for fc in (0, 1)][1:]
        peers = [(mx ^ fx, my ^ fy, mc ^ fc) for fx, fy, fc in flips]
        mine = pltpu.make_async_copy(x_ref, out_ref.at[me], local_sem)
        mine.start()
        sends = [pltpu.make_async_remote_copy(src_ref=x_ref, dst_ref=out_ref.at[me], send_sem=send_sems.at[k],
                                              recv_sem=recv_sems.at[k], device_id=peer, device_id_type=MESH)
                 for k, peer in enumerate(peers)]
        for cp in sends:
            cp.start()
        for k, (px, py, pc) in enumerate(peers):
            pltpu.make_async_remote_copy(src_ref=x_ref, dst_ref=out_ref.at[4 * px + 2 * py + pc], send_sem=send_sems.at[k],
                                         recv_sem=recv_sems.at[k], device_id=(px, py, pc), device_id_type=MESH).wait_recv()
        for cp in sends:
            cp.wait_send()
        mine.wait()

    return pl.pallas_call(
        body, name=name, in_specs=[HBM_SPEC], out_specs=HBM_SPEC,
        out_shape=jax.ShapeDtypeStruct((8,) + x.shape, x.dtype),
        scratch_shapes=[pltpu.SemaphoreType.DMA((7,)), pltpu.SemaphoreType.DMA((7,)), pltpu.SemaphoreType.DMA],
    )(x)


def _sum_slots(name, st, tb=256):
    n, r, c = st.shape
    tb = _pick(r, (tb, 128, 32))

    def body(s_ref, o_ref):
        acc = s_ref[0].astype(F32)
        for k in range(1, n):
            acc = acc + s_ref[k].astype(F32)
        o_ref[...] = acc

    return pl.pallas_call(
        body, name=name, grid=(r // tb,), in_specs=[pl.BlockSpec((n, tb, c), lambda i: (0, i, 0))],
        out_specs=pl.BlockSpec((tb, c), lambda i: (i, 0)), out_shape=jax.ShapeDtypeStruct((r, c), F32),
        compiler_params=_cparams("parallel"),
    )(st)


def _adamw(name, w, g, m, v):
    def fn(w, g, m, v):
        m2 = ADAM_B1 * m + (1.0 - ADAM_B1) * g
        v2 = ADAM_B2 * v + (1.0 - ADAM_B2) * jnp.square(g)
        m_hat = m2 / (1.0 - ADAM_B1 ** ADAM_STEP)
        v_hat = v2 / (1.0 - ADAM_B2 ** ADAM_STEP)
        delta = -ADAM_LR * (m_hat / (jnp.sqrt(v_hat) + ADAM_EPS) + ADAM_WD * w)
        return [delta, m2, v2], []

    tb = _pick(w.shape[0], (256, 32))
    return _rowwise(name, fn, [_row(w), _row(g), _row(m), _row(v)], [], [(w.shape[1], F32)] * 3, tb=tb)


BIG = ("w_ssd_out", "w_attn_out", "w_o", "w_mlp_in", "w_mlp_out", "conv_w")
SMALL = ("pre_mix_norm", "b_gate", "conv_b", "dt_bias", "a_log", "d_skip", "ssd_norm", "attn_sink", "rel_bias_table",
         "post_mix_norm", "pre_mlp_norm", "post_mlp_norm")
ALL_W = ("pre_mix_norm", "w_in", "b_gate", "conv_w", "conv_b", "dt_bias", "a_log", "d_skip", "ssd_norm", "w_ssd_out",
         "attn_sink", "rel_bias_table", "w_attn_out", "w_o", "post_mix_norm", "pre_mlp_norm", "w_mlp_in", "w_mlp_out",
         "post_mlp_norm")


def _pack_rows(parts, rows, dtype):
    flat = jnp.concatenate([p.reshape(-1, D_MODEL).astype(dtype) for p in parts], axis=0)
    return jnp.pad(flat, ((0, rows - flat.shape[0]), (0, 0)))


def _pack_big(shards, dtype):
    return _pack_rows([shards[n] for n in BIG], BIG_ROWS, dtype)


def _unpack_big(flat, like):
    out, r = {}, 0
    for n in BIG:
        shp = like[n].shape
        nr = math.prod(shp) // D_MODEL
        out[n] = flat[r:r + nr].reshape(shp)
        r += nr
    return out


def _pack_small(parts, extra=None):
    flat = jnp.concatenate([parts[n].reshape(-1).astype(F32) for n in SMALL] + ([extra.reshape(-1)] if extra is not None else []))
    return jnp.pad(flat, (0, SMALL_ROWS * D_MODEL - flat.shape[0])).reshape(SMALL_ROWS, D_MODEL)


def _unpack_small(flat2, like):
    flat = flat2.reshape(-1)
    out, r = {}, 0
    for n in SMALL:
        shp = like[n].shape
        k = math.prod(shp)
        out[n] = flat[r:r + k].reshape(shp)
        r += k
    return out, flat[r]


def _shard_of_full(name, full, s):
    if name in ("w_in", "w_mlp_in"):
        w = full.shape[2] // 4
        return full[:, :, s * w:(s + 1) * w]
    if name == "conv_w":
        w = full.shape[3] // 4
        return full[:, :, :, s * w:(s + 1) * w]
    w = full.shape[1] // 4
    return full[:, s * w:(s + 1) * w, :]


def _full_of_shards(name, shards):
    axis = {"w_in": 2, "w_mlp_in": 2, "conv_w": 3}.get(name, 1)
    return jnp.concatenate(shards, axis=axis)


def _to_proj_layout(w):
    z, xbc, dt, q, k, v, gates = (w[..., 0:2048], w[..., 2048:6144], w[..., 6144:6208], w[..., 6208:7232],
                                  w[..., 7232:7488], w[..., 7488:7744], w[..., 7744:9792])
    pad = jnp.zeros(w.shape[:-1] + (N_MAIN - OFF_DT - dt.shape[-1],), w.dtype)

    def doubled(a):
        h = a.reshape(a.shape[:-1] + (ATTN_KV, 1, ATTN_DIM))
        return jnp.broadcast_to(h, a.shape[:-1] + (ATTN_KV, 2, ATTN_DIM)).reshape(a.shape[:-1] + (2 * a.shape[-1],))

    return jnp.concatenate([z, gates, xbc, dt, pad, q, doubled(k), doubled(v)], axis=-1)


def _from_proj_layout(w):
    z, gates, xbc, dt, q, k2, v2 = (w[..., 0:2048], w[..., 2048:4096], w[..., 4096:8192], w[..., 8192:8256],
                                    w[..., 8320:9344], w[..., 9344:9856], w[..., 9856:10368])

    def folded(a):
        return a.reshape(a.shape[:-1] + (ATTN_KV, 2, ATTN_DIM)).sum(axis=-2).reshape(a.shape[:-1] + (a.shape[-1] // 2,))

    return jnp.concatenate([z, xbc, dt, q, folded(k2), folded(v2), gates], axis=-1)


def _layer_fwd(h1, x, W, P, l, bias4):
    t = x.shape[0]
    S = {"x": x, "h1": h1}
    proj = _matmul("proj", h1, W["w_in_main"][l], "nn")
    qkv = _matmul("proj_qkv", h1, W["w_in_qkv"][l], "nn", out_dtype=BF16)
    S["proj"] = proj
    pre, act = _conv_fwd(proj, W["conv_w"][l], P["conv_b"][l].reshape(1, CONV_DIM))
    S["pre"], S["act"] = pre, act

    dtb = jnp.pad(P["dt_bias"][l].reshape(1, 2 * SSD_HEADS), ((0, 0), (0, LANES - 2 * SSD_HEADS)))

    def dt_fn(raw, b):
        v = raw + b
        dt = jnp.maximum(v, 0.0) + jnp.log1p(jnp.exp(-jnp.abs(v)))
        expand = (jnp.right_shift(lax.broadcasted_iota(jnp.int32, (LANES, 2 * D_INNER), 1), 6)
                  == lax.broadcasted_iota(jnp.int32, (LANES, 2 * D_INNER), 0)).astype(BF16)
        hi = dt.astype(BF16)
        rest = dt - hi.astype(F32)
        mid = rest.astype(BF16)
        lo = (rest - mid.astype(F32)).astype(BF16)
        return [dt, _dot(hi, expand) + _dot(mid, expand) + _dot(lo, expand)], []

    dt, dte = _rowwise("dt_fwd", dt_fn, [_row(proj, LANES, OFF_DT // LANES)], [dtb], [(LANES, F32), (2 * D_INNER, F32)], tb=512)
    dt2 = jnp.stack([dt[:, 0:SSD_HEADS], dt[:, SSD_HEADS:2 * SSD_HEADS]])
    dt2t = dt2.transpose(0, 2, 1)
    a = -jnp.exp(P["a_log"][l])
    a_row, a_col = a.reshape(2, 1, SSD_HEADS), a.reshape(2, SSD_HEADS, 1)
    S["dt2"], S["dt2t"], S["a_row"], S["a_col"], S["dte"] = dt2, dt2t, a_row, a_col, dte
    y2, states = _ssd_fwd(act, dt2, dt2t, a_row, a_col, dte)
    S["states"] = states

    dsk = jnp.repeat(P["d_skip"][l], SSD_HEAD_DIM).reshape(1, D_INNER)
    nw = P["ssd_norm"][l].reshape(1, D_INNER)
    S["dsk"], S["nw"] = dsk, nw

    def gn_fn(yf, yb, xs, z, dsk, nw):
        y = yf + yb + xs * dsk
        return [y, _gated_norm_fwd(y, z, nw)], []

    y, yn = _rowwise("gated_norm_fwd", gn_fn,
                     [_row(y2, lead=0), _row(y2, lead=1), _row(act, D_INNER, 0), _row(proj, D_INNER, OFF_Z // D_INNER)],
                     [dsk, nw], [(D_INNER, F32), (D_INNER, BF16)])
    S["y"], S["yn"] = y, yn
    if "late" in W:
        W.update(W.pop("late")(yn))
    y_ssd = _matmul("ssd_out", yn, W["w_ssd_out"][l], "nn")
    S["y_ssd"] = y_ssd

    sink4 = jnp.broadcast_to(P["attn_sink"][l].reshape(ATTN_KV, REP, 1, 1), (ATTN_KV, REP, 1, LANES))
    S["qkv"], S["sink4"] = qkv, sink4
    o = _attn_fwd(qkv, bias4, sink4)
    S["o"] = o
    y_attn = _matmul("attn_out", o, W["w_attn_out"][l], "nn")
    S["y_attn"] = y_attn

    bg = P["b_gate"][l].reshape(1, 2 * D_MODEL)
    S["bg"] = bg

    def merge_fn(gates, ys, ya, b):
        g = jax.nn.sigmoid(gates + b)
        return [g[:, :D_MODEL] * ys + g[:, D_MODEL:] * ya], []

    (mix_in,) = _rowwise("merge_fwd", merge_fn, [_row(proj, 2 * D_MODEL, OFF_G // (2 * D_MODEL)), _row(y_ssd), _row(y_attn)],
                         [bg], [(D_MODEL, BF16)])
    S["mix_in"] = mix_in
    mixed = _matmul("w_o", mix_in, W["w_o"][l], "nn")
    S["mixed"] = mixed

    g_pm = P["post_mix_norm"][l].reshape(1, D_MODEL)
    g_pl = P["pre_mlp_norm"][l].reshape(1, D_MODEL)

    def postmix_fn(x, mixed, g1, g2):
        x2 = x + _rms_fwd(mixed, g1)
        return [x2, _rms_fwd(x2, g2)], []

    x2, h2 = _rowwise("post_mix_fwd", postmix_fn, [_row(x), _row(mixed)], [g_pm, g_pl], [(D_MODEL, F32), (D_MODEL, BF16)])
    S["x2"], S["h2"] = x2, h2
    a1 = _matmul("mlp_in", h2, W["w_mlp_in"][l], "nn", out_dtype=BF16,
                 epilogue=lambda acc: jnp.square(jnp.maximum(acc, 0.0)))
    S["a1"] = a1
    f2 = _matmul("mlp_out", a1, W["w_mlp_out"][l], "nn")
    S["f2"] = f2
    return S


def _layer_bwd(S, dx3, W, P, l, bias4, onehot_t, zero=None):
    t = dx3.shape[0]
    G = {}
    g_pmlp = P["post_mlp_norm"][l].reshape(1, D_MODEL)
    if zero is not None:
        g_pmlp = g_pmlp + zero

    def b1_fn(f2, dx3, g):
        df2, dg = _rms_bwd(f2, g, dx3)
        return [df2], [dg]

    df2, G["post_mlp_norm"] = _rowwise("post_mlp_bwd", b1_fn, [_row(S["f2"]), _row(dx3)], [g_pmlp], [(D_MODEL, BF16)], [(1, D_MODEL)])
    df1 = _matmul("d_f1", df2, W["w_mlp_out"][l], "nt", out_dtype=BF16,
                  epilogue=lambda acc, a1: acc * (2.0 * jnp.sqrt(a1.astype(F32))), extras=[S["a1"]])
    G["w_mlp_out"] = _matmul("dw_mlp_out", S["a1"], df2, "tn")
    dh2 = _matmul("d_h2", df1, W["w_mlp_in"][l], "nt")
    G["w_mlp_in"] = _matmul("dw_mlp_in", S["h2"], df1, "tn")

    g_pm = P["post_mix_norm"][l].reshape(1, D_MODEL)
    g_pl = P["pre_mlp_norm"][l].reshape(1, D_MODEL)

    def b3_fn(x2, dh2, dx3, mixed, g_pl, g_pm):
        d1, dgl = _rms_bwd(x2, g_pl, dh2)
        dx2 = dx3 + d1
        dmixed, dgm = _rms_bwd(mixed, g_pm, dx2)
        return [dx2, dmixed], [dgl, dgm]

    dx2, dmixed, G["pre_mlp_norm"], G["post_mix_norm"] = _rowwise(
        "post_mix_bwd", b3_fn, [_row(S["x2"]), _row(dh2), _row(dx3), _row(S["mixed"])], [g_pl, g_pm],
        [(D_MODEL, F32), (D_MODEL, BF16)], [(1, D_MODEL), (1, D_MODEL)])
    dmix_in = _matmul("d_mix_in", dmixed, W["w_o"][l], "nt")
    G["w_o"] = _matmul("dw_o", S["mix_in"], dmixed, "tn")

    proj = S["proj"]

    def b4_fn(gates, ys, ya, dmix, b):
        g = jax.nn.sigmoid(gates + b)
        gs, ga = g[:, :D_MODEL], g[:, D_MODEL:]
        dg = jnp.concatenate([ys * dmix, ya * dmix], axis=-1) * g * (1.0 - g)
        return [gs * dmix, ga * dmix, dg], [jnp.sum(dg, axis=0, keepdims=True)]

    dy_ssd, dy_attn, dgates, G["b_gate"] = _rowwise(
        "merge_bwd", b4_fn, [_row(proj, 2 * D_MODEL, OFF_G // (2 * D_MODEL)), _row(S["y_ssd"]), _row(S["y_attn"]), _row(dmix_in)],
        [S["bg"]], [(D_MODEL, BF16), (D_MODEL, BF16), (2 * D_MODEL, BF16)], [(1, 2 * D_MODEL)])

    dyn = _matmul("d_yn", dy_ssd, W["w_ssd_out"][l], "nt")
    G["w_ssd_out"] = _matmul("dw_ssd_out", S["yn"], dy_ssd, "tn")
    do = _matmul("d_o", dy_attn, W["w_attn_out"][l], "nt", out_dtype=BF16)
    G["w_attn_out"] = _matmul("dw_attn_out", S["o"], dy_attn, "tn")

    dq, dkp, dvp, dbias4, dsink4 = _attn_bwd(S["qkv"], bias4, S["sink4"], do)
    dk = _kv_combine("dk_combine", dkp)
    dv = _kv_combine("dv_combine", dvp)
    G["attn_sink"] = dsink4[:, :, 0, 0].reshape(ATTN_HEADS)
    G["rel_bias_table"] = _bias_table_grad(dbias4.reshape(ATTN_HEADS, BLOCK * 3 * BLOCK), onehot_t).T

    act = S["act"]

    def b5_fn(y, z, xs, dyn, nw, dsk):
        dy, dz, dnw = _gated_norm_bwd(y, z, nw, dyn)
        return [dy, dz], [dnw, jnp.sum(dy * xs, axis=0, keepdims=True)]

    dy, dz, G["ssd_norm"], dskip_cols = _rowwise(
        "gated_norm_bwd", b5_fn, [_row(S["y"]), _row(proj, D_INNER, OFF_Z // D_INNER), _row(act, D_INNER, 0), _row(dyn)],
        [S["nw"], S["dsk"]], [(D_INNER, F32), (D_INNER, BF16)], [(1, D_INNER), (1, D_INNER)])
    G["d_skip"] = dskip_cols.reshape(SSD_HEADS, SSD_HEAD_DIM).sum(axis=-1)

    dxs2, dbs2, dcs2, ddt2, da2 = _ssd_bwd(act, S["dt2"], S["dt2t"], S["a_row"], S["a_col"], S["dte"], dy, S["states"])
    G["a_log"] = da2.reshape(2, SSD_HEADS) * S["a_row"].reshape(2, SSD_HEADS)

    def b6_fn(dxf, dxb, dy, dbf, dbb, dcf, dcb, pre, dsk):
        dact = jnp.concatenate([dxf + dxb + dy * dsk, dbf + dbb, dcf + dcb], axis=-1)
        return [dact * _silu_grad(pre)], []

    (dpre,) = _rowwise("silu_bwd", b6_fn,
                       [_row(dxs2, lead=0), _row(dxs2, lead=1), _row(dy), _row(dbs2, lead=0), _row(dbs2, lead=1),
                        _row(dcs2, lead=0), _row(dcs2, lead=1), _row(S["pre"])], [S["dsk"]], [(CONV_DIM, F32)], tb=128)
    du, dconv_w, dconv_b = _conv_bwd(dpre, proj, W["conv_w"][l])
    G["conv_w"] = dconv_w.reshape(SSD_CONV, 1, CONV_DIM)
    G["conv_b"] = dconv_b.reshape(CONV_DIM)

    dtb = jnp.pad(P["dt_bias"][l].reshape(1, 2 * SSD_HEADS), ((0, 0), (0, LANES - 2 * SSD_HEADS)))
    ddt = jnp.pad(jnp.concatenate([ddt2[0], ddt2[1]], axis=-1), ((0, 0), (0, LANES - 2 * SSD_HEADS)))

    def b7_fn(raw, ddt, b):
        draw = ddt * jax.nn.sigmoid(raw + b)
        return [draw], [jnp.sum(draw, axis=0, keepdims=True)]

    draw, ddtb = _rowwise("dt_bwd", b7_fn, [_row(proj, LANES, OFF_DT // LANES), _row(ddt)], [dtb], [(LANES, BF16)], [(1, LANES)], tb=512)
    G["dt_bias"] = ddtb[0, :2 * SSD_HEADS].reshape(2, SSD_HEADS)

    dproj = jnp.concatenate([dz, dgates, du, draw, dq, dk, dv], axis=-1)
    G["w_in"] = _from_proj_layout(_matmul("dw_in", S["h1"], dproj, "tn"))
    dh1 = _matmul("d_h1", dproj, W["w_in"][l], "nt")

    g_pre = P["pre_mix_norm"][l].reshape(1, D_MODEL)

    def b8_fn(x, dh1, dx2, g):
        d1, dg = _rms_bwd(x, g, dh1)
        return [dx2 + d1], [dg]

    dx, G["pre_mix_norm"] = _rowwise("pre_mix_bwd", b8_fn, [_row(S["x"]), _row(dh1), _row(dx2)], [g_pre], [(D_MODEL, F32)], [(1, D_MODEL)])
    return dx, G


def _chip_sums(tag, pieces):
    theirs = _swap_cores("swap_" + tag, pieces)
    out = []
    for i, p in enumerate(pieces):
        _, _, r2, c = p.shape
        mine = lax.dynamic_index_in_dim(p, lax.axis_index("c"), axis=0, keepdims=False).reshape(4 * r2, c)
        (cs,) = _rowwise(f"presum_{tag}{i}", lambda a, b: ([a.astype(F32) + b.astype(F32)], []),
                         [_row(mine), _row(theirs[i].reshape(4 * r2, c))], [], [(c, BF16)])
        out.append(cs.reshape(4, r2, c))
    return out


def _halves(blocks):
    r2 = blocks[0].shape[0] // 2
    return jnp.stack([jnp.stack([b[h * r2:(h + 1) * r2] for b in blocks]) for h in range(2)])


def _layer_pieces(g, like):
    cols = like["w_in"].shape[2]
    win = _halves([g["w_in"][:, s * cols:(s + 1) * cols].astype(BF16) for s in range(4)])
    packs = [_pack_rows([_shard_of_full(n, g[n][None], s)[0] for n in BIG], BIG_ROWS // 2, BF16) for s in range(4)]
    return [win, _halves(packs)]


def _unpack_layer(flat, like):
    out, r = {}, 0
    for n in BIG:
        shp = like[n].shape[1:]
        nr = math.prod(shp) // D_MODEL
        out[n] = flat[r:r + nr].reshape(shp)
        r += nr
    return out


def _step(x, target, shard_w, shard_m, shard_v):
    depth, _, win_cols = shard_w["w_in"].shape
    win_rows = depth * D_MODEL

    def win2d(a):
        return a.reshape(win_rows, win_cols)

    conv_rows = shard_w["conv_w"].reshape(-1, D_MODEL)
    win_own = win2d(shard_w["w_in"]).astype(BF16)
    conv_own = jnp.pad(conv_rows, ((0, 16 - conv_rows.shape[0]), (0, 0)))
    win_g, conv_g = _allgather_chips("gather_w_in", [win_own, conv_own])
    packed = _pack_big(shard_w, BF16)
    send_sems, recv_sems, packed_thru, land_thru, token = _send_start("gather_rest_start", [packed], conv_g, False, 0)

    def late(after):
        (landed,) = _send_wait("gather_rest_wait", send_sems, recv_sems, packed_thru, land_thru, after, False)
        per_chip = [_unpack_big(jnp.where(me == s, packed, landed[s]), shard_w) for s in range(4)]
        return {n: _full_of_shards(n, [per_chip[s][n] for s in range(4)]) for n in BIG if n != "conv_w"}

    W = {"late": late}
    me = 2 * lax.axis_index("x") + lax.axis_index("y")
    win_g = [jnp.where(me == s, win_own, win_g[s]) for s in range(4)]
    conv_g = [jnp.where(me == s, conv_own, conv_g[s]) for s in range(4)]
    W["w_in"] = _to_proj_layout(jnp.concatenate([win_g[s].reshape(depth, D_MODEL, win_cols) for s in range(4)], axis=2))
    W["conv_w"] = jnp.concatenate([conv_g[s][:conv_rows.shape[0]].reshape(shard_w["conv_w"].shape) for s in range(4)],
                                  axis=3).reshape(depth, SSD_CONV, CONV_DIM)
    P = {n: shard_w[n] for n in SMALL}
    P["pre_mix_norm"] = P["pre_mix_norm"] + token[0, 0]
    assert depth == 2
    sent = {}

    def grads_ready(g):
        sent["sums"] = _chip_sums("l1_", _layer_pieces(g, shard_w))
        sent["sems"] = _send_start("scatter_l1_start", sent["sums"], sent["sums"][0], True, 1)
        return sent["sems"][4][0, 0]

    loss_part, grad_x, full = _local_step(x, target, W, P, grads_ready)
    send1, recv1, thru1, land1, _ = sent["sems"]
    landed1 = _send_wait("scatter_l1_wait", send1, recv1, thru1, land1, grad_x, True)
    sums0 = _chip_sums("l0_", _layer_pieces({n: full[n][0] for n in BIG + ("w_in",)}, shard_w))
    landed0 = _scatter_chips("scatter_grads", sums0)
    staged = [jnp.stack([jnp.where(me == s, own[s], got[s]) for s in range(4)])
              for own, got in zip(sums0 + sent["sums"], list(landed0) + landed1)]
    halves = [_sum_slots(f"sum_grads{i}", st) for i, st in enumerate(staged)]
    core = lax.axis_index("c")
    win0, big0, win1, big1 = [jnp.where(core == 0, jnp.concatenate([mine, other]), jnp.concatenate([other, mine]))
                              for mine, other in zip(halves, _swap_cores("share_grads", halves, slab=False))]
    g_win = jnp.concatenate([win0, win1], axis=0)
    per_layer = [_unpack_layer(big0, shard_w), _unpack_layer(big1, shard_w)]
    g_big = _pack_big({n: jnp.stack([per_layer[0][n], per_layer[1][n]]) for n in BIG}, F32)
    d_win, m_win, v_win = _adamw("adamw_w_in", win2d(shard_w["w_in"]), g_win, win2d(shard_m["w_in"]), win2d(shard_v["w_in"]))
    d_big, m_big, v_big = _adamw("adamw_big", _pack_big(shard_w, F32), g_big, _pack_big(shard_m, F32), _pack_big(shard_v, F32))

    small = _allgather_all("gather_small", _pack_small(full, loss_part))
    g_small = _sum_slots("sum_small", small, tb=SMALL_ROWS)
    d_small, m_small, v_small = _adamw("adamw_small", _pack_small(shard_w, jnp.zeros((), F32)), g_small,
                                       _pack_small(shard_m, jnp.zeros((), F32)), _pack_small(shard_v, jnp.zeros((), F32)))

    outs = {}
    for tag, win, big, sm in (("grad", g_win, g_big, g_small), ("delta", d_win, d_big, d_small),
                              ("new_m", m_win, m_big, m_small), ("new_v", v_win, v_big, v_small)):
        ub = _unpack_big(big, shard_w)
        us, extra = _unpack_small(sm, shard_w)
        outs[tag] = {"w_in": win.reshape(shard_w["w_in"].shape), **ub, **us}
        if tag == "grad":
            loss = extra
    return loss, grad_x, outs


def _local_step(x, target, W, P, grads_ready=None):
    depth = W["w_in"].shape[0]
    W = dict(W, w_in_main=W["w_in"][:, :, :N_MAIN], w_in_qkv=W["w_in"][:, :, N_MAIN:])
    onehot_t = (_bucket_map().reshape(1, -1) == jnp.arange(N_BUCKETS)[:, None]).astype(F32)
    bias = _bias_from_table(P["rel_bias_table"], onehot_t).reshape(ATTN_HEADS, BLOCK, 3 * BLOCK)
    bias4 = jnp.where(_band_mask(), bias, NEG).reshape(ATTN_KV, REP, BLOCK, 3 * BLOCK)

    def pre_fn(x, g):
        return [_rms_fwd(x, g)], []

    (h1,) = _rowwise("pre_mix_fwd", pre_fn, [_row(x)], [P["pre_mix_norm"][0].reshape(1, D_MODEL)], [(D_MODEL, BF16)])
    saved = []
    loss_cols = dxl = None
    for l in range(depth):
        S = _layer_fwd(h1, x, W, P, l, bias4)
        saved.append(S)
        g_pmlp = P["post_mlp_norm"][l].reshape(1, D_MODEL)
        if l + 1 < depth:
            def post_fn(x2, f2, g1, g2):
                x3 = x2 + _rms_fwd(f2, g1)
                return [x3, _rms_fwd(x3, g2)], []

            x, h1 = _rowwise("post_mlp_fwd", post_fn, [_row(S["x2"]), _row(S["f2"])],
                             [g_pmlp, P["pre_mix_norm"][l + 1].reshape(1, D_MODEL)], [(D_MODEL, F32), (D_MODEL, BF16)])
        else:
            def loss_fn(x2, f2, tgt, g1):
                diff = x2 + _rms_fwd(f2, g1) - tgt
                return [diff * (1.0 / D_MODEL)], [jnp.sum(diff * diff, axis=0, keepdims=True)]

            dxl, loss_cols = _rowwise("loss", loss_fn, [_row(S["x2"]), _row(S["f2"]), _row(target)], [g_pmlp],
                                      [(D_MODEL, F32)], [(1, D_MODEL)])
    loss_part = 0.5 * jnp.sum(loss_cols) / D_MODEL

    grads = [None] * depth
    dx = dxl
    zero = None
    for l in reversed(range(depth)):
        dx, grads[l] = _layer_bwd(saved[l], dx, W, P, l, bias4, onehot_t, zero)
        if grads_ready is not None and l == depth - 1 and depth > 1:
            zero = grads_ready(grads[l])
    grad_x = dx

    full = {n: jnp.stack([grads[l][n] for l in range(depth)]) for n in ALL_W if n != "rel_bias_table"}
    full["rel_bias_table"] = sum(grads[l]["rel_bias_table"] for l in range(depth))
    return loss_part, grad_x, full


def kernel(x, pre_mix_norm, w_in, b_gate, conv_w, conv_b, dt_bias, a_log, d_skip, ssd_norm, w_ssd_out, attn_sink, rel_bias_table, w_attn_out, w_o, post_mix_norm, pre_mlp_norm, w_mlp_in, w_mlp_out, post_mlp_norm, loss_target, m_pre_mix_norm, m_w_in, m_b_gate, m_conv_w, m_conv_b, m_dt_bias, m_a_log, m_d_skip, m_ssd_norm, m_w_ssd_out, m_attn_sink, m_rel_bias_table, m_w_attn_out, m_w_o, m_post_mix_norm, m_pre_mlp_norm, m_w_mlp_in, m_w_mlp_out, m_post_mlp_norm, v_pre_mix_norm, v_w_in, v_b_gate, v_conv_w, v_conv_b, v_dt_bias, v_a_log, v_d_skip, v_ssd_norm, v_w_ssd_out, v_attn_sink, v_rel_bias_table, v_w_attn_out, v_w_o, v_post_mix_norm, v_pre_mlp_norm, v_w_mlp_in, v_w_mlp_out, v_post_mlp_norm):
    a = locals()
    shard_w = {n: a[n] for n in ALL_W}
    shard_m = {n: a["m_" + n] for n in ALL_W}
    shard_v = {n: a["v_" + n] for n in ALL_W}
    loss, grad_x, outs = _step(x[0], loss_target[0], shard_w, shard_m, shard_v)
    return (loss, grad_x[None], *[outs["grad"][n] for n in ALL_W], *[outs["delta"][n] for n in ALL_W],
            *[outs["new_m"][n] for n in ALL_W], *[outs["new_v"][n] for n in ALL_W])
```

```python
import math

import jax
import jax.numpy as jnp
from jax import lax
from jax.experimental import pallas as pl
from jax.experimental.pallas import tpu as pltpu

F32 = jnp.float32
BF16 = jnp.bfloat16
MESH = pl.DeviceIdType.MESH

VMEM_LIMIT_BYTES = 52 * 1024 * 1024
LANES = 128
SUBLANES = 8

EPS = 1e-6
D_MODEL = 1024
D_INNER = 2048
SSD_HEADS = 32
SSD_HEAD_DIM = 64
SSD_GROUPS = 8
SSD_STATE = 128
SSD_CONV = 5
CHUNK = 128
CONV_DIM = 4096
ATTN_HEADS = 16
ATTN_KV = 4
ATTN_DIM = 64
BLOCK = 128
N_BUCKETS = 32
MAX_DISTANCE = 128
D_FF = 4096
N_IN = 9792
NEG = -1e30

OFF_Z, OFF_G, OFF_XBC, OFF_DT, N_MAIN, N_PROJ = 0, 2048, 4096, 8192, 8320, 10368
N_QKV = N_PROJ - N_MAIN

ADAM_LR, ADAM_B1, ADAM_B2, ADAM_EPS, ADAM_WD, ADAM_STEP = 0.001, 0.9, 0.999, 1e-08, 0.01, 10

BIG_ROWS = 6656
SMALL_ROWS = 32


def _cparams(*sem):
    return pltpu.CompilerParams(dimension_semantics=sem, vmem_limit_bytes=VMEM_LIMIT_BYTES)


def _dot(a, b, precision=None):
    return lax.dot_general(a, b, (((1,), (0,)), ((), ())), preferred_element_type=F32, precision=precision)


def _dot_nt(a, b, precision=None):
    return lax.dot_general(a, b, (((1,), (1,)), ((), ())), preferred_element_type=F32, precision=precision)


def _dot_tn(a, b):
    return lax.dot_general(a, b, (((0,), (0,)), ((), ())), preferred_element_type=F32)


def _pick(n, prefs):
    for p in prefs:
        if n % p == 0:
            return p
    return n


def _matmul(name, a, b, mode, out_dtype=F32, epilogue=None, extras=()):
    if mode == "nn":
        (m, k), (_, n) = a.shape, b.shape
    elif mode == "nt":
        (m, k), (n, _) = a.shape, b.shape
    else:
        (k, m), (_, n) = a.shape, b.shape
    tm = _pick(m, (512, 256, 128)) if mode == "tn" else _pick(m, (1024, 512, 256, 128))
    tn = _pick(n, (1024, 1152, 1664, 512, 256, 128))
    tk = _pick(k, (4096, 2048, 1024, 3456, 512, 256, 128)) if mode != "tn" else _pick(k, (4096, 512, 256, 128))
    if tk > 3456:
        tm = _pick(m, (512, 256, 128))
    nk = k // tk
    if mode == "nn":
        a_spec = pl.BlockSpec((tm, tk), lambda i, j, q: (i, q))
        b_spec = pl.BlockSpec((tk, tn), lambda i, j, q: (q, j))
        fn = _dot
    elif mode == "nt":
        a_spec = pl.BlockSpec((tm, tk), lambda i, j, q: (i, q))
        b_spec = pl.BlockSpec((tn, tk), lambda i, j, q: (j, q))
        fn = _dot_nt
    else:
        a_spec = pl.BlockSpec((tk, tm), lambda i, j, q: (q, i))
        b_spec = pl.BlockSpec((tk, tn), lambda i, j, q: (q, j))
        fn = _dot_tn

    tile = pl.BlockSpec((tm, tn), lambda i, j, q: (i, j))
    n_ex = len(extras)

    def body(a_ref, b_ref, *rest):
        ex_refs, o_ref = rest[:n_ex], rest[n_ex]

        def store(acc):
            v = acc if epilogue is None else epilogue(acc, *[r[...] for r in ex_refs])
            o_ref[...] = v.astype(o_ref.dtype)

        p = fn(a_ref[...].astype(BF16), b_ref[...].astype(BF16))
        if nk == 1:
            store(p)
        else:
            acc_ref = rest[n_ex + 1]
            q = pl.program_id(2)

            @pl.when(q == 0)
            def _():
                acc_ref[...] = p

            @pl.when((q > 0) & (q < nk - 1))
            def _():
                acc_ref[...] += p

            @pl.when(q == nk - 1)
            def _():
                store(acc_ref[...] + p)

    return pl.pallas_call(
        body, name=name, grid=(m // tm, n // tn, nk),
        in_specs=[a_spec, b_spec] + [tile] * n_ex, out_specs=tile,
        out_shape=jax.ShapeDtypeStruct((m, n), out_dtype),
        scratch_shapes=[pltpu.VMEM((tm, tn), F32)] if nk > 1 else [],
        compiler_params=_cparams("parallel", "parallel", "arbitrary"),
    )(a, b, *extras)


def _row(arr, width=None, cb=0, lead=None):
    return (arr, width, cb, lead)


def _rowwise(name, fn, rows, vecs, outs, accs=(), tb=256):
    t = rows[0][0].shape[-2]
    tb = min(tb, t)
    in_specs, args = [], []
    for arr, width, cb, lead in rows:
        w = arr.shape[-1] if width is None else width
        if lead is None:
            in_specs.append(pl.BlockSpec((tb, w), lambda i, cb=cb: (i, cb)))
        else:
            in_specs.append(pl.BlockSpec((None, tb, w), lambda i, cb=cb, lead=lead: (lead, i, cb)))
        args.append(arr)
    for v in vecs:
        in_specs.append(pl.BlockSpec(v.shape, lambda i, nd=v.ndim: (0,) * nd))
        args.append(v)
    out_shape = [jax.ShapeDtypeStruct((t, c), dt) for c, dt in outs] + [jax.ShapeDtypeStruct(s, F32) for s in accs]
    out_specs = [pl.BlockSpec((tb, c), lambda i: (i, 0)) for c, _ in outs] + [pl.BlockSpec(s, lambda i: (0, 0)) for s in accs]
    n_in, n_out = len(args), len(outs)

    def body(*refs):
        vals = [r[...] for r in refs[:n_in]]
        o_vals, a_vals = fn(*vals)
        for r, v in zip(refs[n_in:n_in + n_out], o_vals):
            r[...] = v.astype(r.dtype)
        first = pl.program_id(0) == 0
        for r, v in zip(refs[n_in + n_out:], a_vals):
            @pl.when(first)
            def _(r=r, v=v):
                r[...] = v

            @pl.when(jnp.logical_not(first))
            def _(r=r, v=v):
                r[...] += v

    res = pl.pallas_call(
        body, name=name, grid=(t // tb,), in_specs=in_specs, out_specs=out_specs, out_shape=out_shape,
        compiler_params=_cparams("arbitrary"),
    )(*args)
    return res


def _rms_fwd(x, g):
    r = lax.rsqrt(jnp.mean(x * x, axis=-1, keepdims=True) + EPS)
    return x * r * g


def _rms_bwd(x, g, dy):
    r = lax.rsqrt(jnp.mean(x * x, axis=-1, keepdims=True) + EPS)
    xh = x * r
    dxh = dy * g
    dx = r * (dxh - xh * jnp.mean(dxh * xh, axis=-1, keepdims=True))
    return dx, jnp.sum(dy * xh, axis=0, keepdims=True)


def _silu(x):
    return x * jax.nn.sigmoid(x)


def _silu_grad(x):
    s = jax.nn.sigmoid(x)
    return s * (1.0 + x * (1.0 - s))


GROUP_W = D_INNER // SSD_GROUPS


def _gated_norm_fwd(y, z, w):
    u = y * _silu(z)
    parts = []
    for j in range(SSD_GROUPS):
        ug = u[:, j * GROUP_W:(j + 1) * GROUP_W]
        parts.append(ug * lax.rsqrt(jnp.mean(ug * ug, axis=-1, keepdims=True) + EPS))
    return jnp.concatenate(parts, axis=-1) * w


def _gated_norm_bwd(y, z, w, dyn):
    sz = _silu(z)
    u = y * sz
    duh = dyn * w
    du_parts, uh_parts = [], []
    for j in range(SSD_GROUPS):
        sl = slice(j * GROUP_W, (j + 1) * GROUP_W)
        ug = u[:, sl]
        r = lax.rsqrt(jnp.mean(ug * ug, axis=-1, keepdims=True) + EPS)
        uh = ug * r
        dg = duh[:, sl]
        du_parts.append(r * (dg - uh * jnp.mean(dg * uh, axis=-1, keepdims=True)))
        uh_parts.append(uh)
    du = jnp.concatenate(du_parts, axis=-1)
    uh = jnp.concatenate(uh_parts, axis=-1)
    dw = jnp.sum(dyn * uh, axis=0, keepdims=True)
    return du * sz, du * y * _silu_grad(z), dw


HALO = SUBLANES


def _halo_specs(tb, cb, col0, t):
    nblk8 = t // HALO
    per = tb // HALO
    main = pl.BlockSpec((tb, cb), lambda j, i: (i, col0 + j))
    prev = pl.BlockSpec((HALO, cb), lambda j, i: (jnp.maximum(i * per - 1, 0), col0 + j))
    nxt = pl.BlockSpec((HALO, cb), lambda j, i: (jnp.minimum((i + 1) * per, nblk8 - 1), col0 + j))
    return main, prev, nxt


def _fill_ext(ext_ref, cur_ref, prev_ref, next_ref, tb, ni):
    i = pl.program_id(1)
    ext_ref[0:HALO, :] = jnp.where(i > 0, prev_ref[...], 0.0)
    ext_ref[HALO:HALO + tb, :] = cur_ref[...]
    ext_ref[HALO + tb:HALO + tb + HALO, :] = jnp.where(i < ni - 1, next_ref[...], 0.0)


def _conv_fwd(proj, w, b):
    t = proj.shape[0]
    tb, cb = min(1024, t), 512
    ni, nj = t // tb, CONV_DIM // cb
    main, prev, nxt = _halo_specs(tb, cb, OFF_XBC // cb, t)
    pad = (SSD_CONV - 1) // 2

    def body(u_ref, up_ref, un_ref, w_ref, b_ref, pre_ref, act_ref, ext_ref):
        _fill_ext(ext_ref, u_ref, up_ref, un_ref, tb, ni)
        acc = jnp.broadcast_to(b_ref[...], (tb, cb))
        for k in range(SSD_CONV):
            acc = acc + w_ref[k:k + 1, :] * ext_ref[pl.ds(HALO + k - pad, tb), :]
        pre_ref[...] = acc
        act_ref[...] = _silu(acc)

    out = pl.BlockSpec((tb, cb), lambda j, i: (i, j))
    return pl.pallas_call(
        body, name="conv_fwd", grid=(nj, ni),
        in_specs=[main, prev, nxt, pl.BlockSpec((SSD_CONV, cb), lambda j, i: (0, j)), pl.BlockSpec((1, cb), lambda j, i: (0, j))],
        out_specs=[out, out],
        out_shape=[jax.ShapeDtypeStruct((t, CONV_DIM), F32)] * 2,
        scratch_shapes=[pltpu.VMEM((tb + 2 * HALO, cb), F32)],
        compiler_params=_cparams("parallel", "arbitrary"),
    )(proj, proj, proj, w, b)


def _conv_bwd(dpre, proj, w):
    t = proj.shape[0]
    tb, cb = min(1024, t), 512
    ni, nj = t // tb, CONV_DIM // cb
    umain, uprev, unext = _halo_specs(tb, cb, OFF_XBC // cb, t)
    dmain, dprev, dnext = _halo_specs(tb, cb, 0, t)
    pad = (SSD_CONV - 1) // 2

    def body(d_ref, dp_ref, dn_ref, u_ref, up_ref, un_ref, w_ref, du_ref, dw_ref, db_ref, extd_ref, extu_ref):
        _fill_ext(extd_ref, d_ref, dp_ref, dn_ref, tb, ni)
        _fill_ext(extu_ref, u_ref, up_ref, un_ref, tb, ni)
        d = d_ref[...]
        du = jnp.zeros((tb, cb), F32)
        @pl.when(pl.program_id(1) == 0)
        def _():
            dw_ref[...] = jnp.zeros_like(dw_ref)
            db_ref[...] = jnp.zeros_like(db_ref)

        for k in range(SSD_CONV):
            du = du + w_ref[k:k + 1, :] * extd_ref[pl.ds(HALO - k + pad, tb), :]
            dw_ref[k:k + 1, :] += jnp.sum(d * extu_ref[pl.ds(HALO + k - pad, tb), :], axis=0, keepdims=True)
        du_ref[...] = du.astype(du_ref.dtype)
        db_ref[...] += jnp.sum(d, axis=0, keepdims=True)

    return pl.pallas_call(
        body, name="conv_bwd", grid=(nj, ni),
        in_specs=[dmain, dprev, dnext, umain, uprev, unext, pl.BlockSpec((SSD_CONV, cb), lambda j, i: (0, j))],
        out_specs=[pl.BlockSpec((tb, cb), lambda j, i: (i, j)), pl.BlockSpec((SSD_CONV, cb), lambda j, i: (0, j)),
                   pl.BlockSpec((1, cb), lambda j, i: (0, j))],
        out_shape=[jax.ShapeDtypeStruct((t, CONV_DIM), BF16), jax.ShapeDtypeStruct((SSD_CONV, CONV_DIM), F32),
                   jax.ShapeDtypeStruct((1, CONV_DIM), F32)],
        scratch_shapes=[pltpu.VMEM((tb + 2 * HALO, cb), F32)] * 2,
        compiler_params=_cparams("parallel", "arbitrary"),
    )(dpre, dpre, dpre, proj, proj, proj, w)


PAIR = 2 * SSD_HEAD_DIM
HI = lax.Precision.HIGHEST


def _ssd_prelude(d, dt_ref, dtt_ref, ar_ref, ac_ref):
    li = lax.broadcasted_iota(jnp.int32, (CHUNK, CHUNK), 0)
    si = lax.broadcasted_iota(jnp.int32, (CHUNK, CHUNK), 1)
    fwd = d == 0
    hi, lo = jnp.where(fwd, li, si), jnp.where(fwd, si, li)
    tri = hi >= lo
    trif = tri.astype(F32)
    trit = (hi <= lo).astype(F32)
    dt = dt_ref[...]
    adt = dt * ar_ref[...]
    adtt = dtt_ref[...] * ac_ref[...]
    p = _dot(trif, adt, HI)
    pt = _dot_nt(adtt, trif, HI)
    tot = jnp.sum(adt, axis=0, keepdims=True)
    return tri, trit, dt, p, pt, tot


def _ssd_specs(nc, rev):
    def cidx(d, c):
        up = (d == 1) if rev else (d == 0)
        return jnp.where(up, c, nc - 1 - c)

    specs = [
        pl.BlockSpec((CHUNK, D_INNER), lambda d, c: (cidx(d, c), 0)),
        pl.BlockSpec((CHUNK, 1024), lambda d, c: (cidx(d, c), 2)),
        pl.BlockSpec((CHUNK, 1024), lambda d, c: (cidx(d, c), 3)),
        pl.BlockSpec((None, CHUNK, SSD_HEADS), lambda d, c: (d, cidx(d, c), 0)),
        pl.BlockSpec((None, SSD_HEADS, CHUNK), lambda d, c: (d, 0, cidx(d, c))),
        pl.BlockSpec((None, 1, SSD_HEADS), lambda d, c: (d, 0, 0)),
        pl.BlockSpec((None, SSD_HEADS, 1), lambda d, c: (d, 0, 0)),
        pl.BlockSpec((CHUNK, D_INNER), lambda d, c: (cidx(d, c), d)),
    ]
    return cidx, specs


def _head_decay(tri, p, pt, tot, h):
    pb = jnp.broadcast_to(p[:, h:h + 1], (CHUNK, CHUNK))
    dec = jnp.exp(jnp.where(tri, pb - pt[h:h + 1, :], NEG))
    return dec, jnp.exp(tot[:, h:h + 1] - pb), jnp.exp(pb)


def _ssd_fwd(act, dt2, dt2t, a_row, a_col, dte):
    t = act.shape[0]
    nc = t // CHUNK
    cidx, specs = _ssd_specs(nc, rev=False)

    def body(xs_ref, bs_ref, cs_ref, dt_ref, dtt_ref, ar_ref, ac_ref, dte_ref, y_ref, st_ref, h_ref):
        d, c = pl.program_id(0), pl.program_id(1)

        @pl.when(c == 0)
        def _():
            h_ref[...] = jnp.zeros_like(h_ref)

        st_ref[...] = h_ref[...]
        tri, _, _, p, pt, tot = _ssd_prelude(d, dt_ref, dtt_ref, ar_ref, ac_ref)
        etot = jnp.exp(tot)
        lane = lax.broadcasted_iota(jnp.int32, (CHUNK, PAIR), 1) < SSD_HEAD_DIM
        rowh = lax.broadcasted_iota(jnp.int32, (PAIR, SSD_STATE), 0) < SSD_HEAD_DIM
        for g in range(SSD_GROUPS):
            gs = slice(g * SSD_STATE, (g + 1) * SSD_STATE)
            bg = bs_ref[:, gs]
            cb = cs_ref[:, gs].astype(BF16)
            cbm = _dot_nt(cb, bg.astype(BF16))
            for pr in range(2):
                h0 = g * 4 + pr * 2
                h1 = h0 + 1
                sl = slice(h0 * SSD_HEAD_DIM, h0 * SSD_HEAD_DIM + PAIR)
                xdt = (xs_ref[:, sl] * dte_ref[:, sl]).astype(BF16)
                yd, st, epb = [], [], []
                for h in (h0, h1):
                    dec, wb, eb = _head_decay(tri, p, pt, tot, h)
                    yd.append(_dot((cbm * dec).astype(BF16), xdt))
                    st.append(_dot_tn(xdt, (bg * wb).astype(BF16)))
                    epb.append(eb)
                hin = h_ref[sl, :]
                yo = _dot_nt(cb, hin.astype(BF16)) * jnp.where(lane, epb[0], epb[1])
                y_ref[:, sl] = jnp.where(lane, yd[0], yd[1]) + yo
                et = jnp.where(rowh, etot[:, h0:h0 + 1], etot[:, h1:h1 + 1])
                h_ref[sl, :] = hin * et + jnp.where(rowh, st[0], st[1])

    return pl.pallas_call(
        body, name="ssd_fwd", grid=(2, nc), in_specs=specs,
        out_specs=[pl.BlockSpec((None, CHUNK, D_INNER), lambda d, c: (d, cidx(d, c), 0)),
                   pl.BlockSpec((None, None, D_INNER, SSD_STATE), lambda d, c: (d, cidx(d, c), 0, 0))],
        out_shape=[jax.ShapeDtypeStruct((2, t, D_INNER), F32), jax.ShapeDtypeStruct((2, nc, D_INNER, SSD_STATE), F32)],
        scratch_shapes=[pltpu.VMEM((D_INNER, SSD_STATE), F32)],
        compiler_params=_cparams("arbitrary", "arbitrary"),
    )(act, act, act, dt2, dt2t, a_row, a_col, dte)


def _put_col(acc, col, h):
    lane = lax.broadcasted_iota(jnp.int32, acc.shape, 1)
    return jnp.where(lane == h, col, acc)


def _put_row(acc, row, h):
    sub = lax.broadcasted_iota(jnp.int32, acc.shape, 0)
    return jnp.where(sub == h, row, acc)


def _sum_all(x):
    return jnp.sum(jnp.sum(x, axis=0, keepdims=True), axis=1, keepdims=True)


def _ssd_bwd(act, dt2, dt2t, a_row, a_col, dte, dy, states):
    t = act.shape[0]
    nc = t // CHUNK
    cidx, specs = _ssd_specs(nc, rev=True)
    specs = specs + [
        pl.BlockSpec((CHUNK, D_INNER), lambda d, c: (cidx(d, c), 0)),
        pl.BlockSpec((None, None, D_INNER, SSD_STATE), lambda d, c: (d, cidx(d, c), 0, 0)),
    ]

    def body(xs_ref, bs_ref, cs_ref, dt_ref, dtt_ref, ar_ref, ac_ref, dte_ref, dy_ref, st_ref,
             dxs_ref, dbs_ref, dcs_ref, ddt_ref, da_ref, dh_ref):
        d, c = pl.program_id(0), pl.program_id(1)

        @pl.when(c == 0)
        def _():
            dh_ref[...] = jnp.zeros_like(dh_ref)
            da_ref[...] = jnp.zeros_like(da_ref)

        tri, trit, dt, p, pt, tot = _ssd_prelude(d, dt_ref, dtt_ref, ar_ref, ac_ref)
        etot = jnp.exp(tot)
        lane = lax.broadcasted_iota(jnp.int32, (CHUNK, PAIR), 1) < SSD_HEAD_DIM
        rowh = lax.broadcasted_iota(jnp.int32, (PAIR, SSD_STATE), 0) < SSD_HEAD_DIM
        first_head = lax.broadcasted_iota(jnp.int32, (PAIR, LANES), 0) < SSD_HEAD_DIM
        out_lane = lax.broadcasted_iota(jnp.int32, (PAIR, LANES), 1)
        ddtx = jnp.zeros((CHUNK, LANES), F32)
        lane32 = lax.broadcasted_iota(jnp.int32, (CHUNK, SSD_HEADS), 1)
        dp_col = jnp.zeros((CHUNK, SSD_HEADS), F32)
        dp_row = jnp.zeros((SSD_HEADS, CHUNK), F32)
        dtot = jnp.zeros((1, SSD_HEADS), F32)
        for g in range(SSD_GROUPS):
            gs = slice(g * SSD_STATE, (g + 1) * SSD_STATE)
            bg = bs_ref[:, gs]
            bb = bg.astype(BF16)
            cb = cs_ref[:, gs].astype(BF16)
            cbm = _dot_nt(cb, bb)
            dcb = jnp.zeros((CHUNK, CHUNK), F32)
            dc_acc = jnp.zeros((CHUNK, SSD_STATE), F32)
            db_acc = jnp.zeros((CHUNK, SSD_STATE), F32)
            for pr in range(2):
                h0 = g * 4 + pr * 2
                h1 = h0 + 1
                sl = slice(h0 * SSD_HEAD_DIM, h0 * SSD_HEAD_DIM + PAIR)
                xp = xs_ref[:, sl]
                dtp = dte_ref[:, sl]
                xdt_f = xp * dtp
                xdt = xdt_f.astype(BF16)
                dyp = dy_ref[:, sl]
                dyb = dyp.astype(BF16)
                hin = st_ref[sl, :]
                dh = dh_ref[sl, :]
                hb = hin.astype(BF16)
                dhb = dh.astype(BF16)
                heads = [_head_decay(tri, p, pt, tot, h) for h in (h0, h1)]
                dye = dyp * jnp.where(lane, heads[0][2], heads[1][2])
                dyeb = dye.astype(BF16)
                gy = _dot_nt(cb, hb) * dye
                dc_acc = dc_acc + _dot(dyeb, hb)
                dhin = _dot_tn(dyeb, cb)
                hh = dh * hin
                dxdt = jnp.zeros((CHUNK, PAIR), F32)
                for idx, h in enumerate((h0, h1)):
                    hm = lane if idx == 0 else jnp.logical_not(lane)
                    rm = rowh if idx == 0 else jnp.logical_not(rowh)
                    dec, wb, _ = heads[idx]
                    mf = cbm * dec
                    t1 = _dot_tn(mf.astype(BF16), dyb)
                    t2 = _dot_nt((bg * wb).astype(BF16), dhb)
                    dxdt = jnp.where(hm, t1 + t2, dxdt)
                    dm = _dot_nt(jnp.where(hm, dyp, 0.0).astype(BF16), xdt)
                    dcb = dcb + dm * dec
                    e = dm * mf
                    qw = _dot(jnp.where(hm, xdt_f, 0.0).astype(BF16), dhb) * wb
                    db_acc = db_acc + qw
                    qwb = qw * bg
                    col = jnp.sum(e + jnp.where(hm, gy, 0.0) - qwb, axis=1, keepdims=True)
                    dp_col = jnp.where(lane32 == h, col, dp_col)
                    dp_row = _put_row(dp_row, -jnp.sum(e, axis=0, keepdims=True), h)
                    dtot_h = _sum_all(qwb) + etot[:, h:h + 1] * _sum_all(jnp.where(rm, hh, 0.0))
                    dtot = _put_col(dtot, dtot_h, h)
                dxs_ref[:, sl] = dxdt * dtp
                ddx = dxdt * xp
                ddx_hi = ddx.astype(BF16)
                ddx_lo = (ddx - ddx_hi.astype(F32)).astype(BF16)
                route = (out_lane == jnp.where(first_head, h0, h1)).astype(BF16)
                ddtx = ddtx + _dot(ddx_hi, route) + _dot(ddx_lo, route)
                et = jnp.where(rowh, etot[:, h0:h0 + 1], etot[:, h1:h1 + 1])
                dh_ref[sl, :] = dh * et + dhin
            dcbb = dcb.astype(BF16)
            dcs_ref[:, gs] = _dot(dcbb, bb) + dc_acc
            dbs_ref[:, gs] = _dot_tn(dcbb, cb) + db_acc
        d_adt = _dot(trit, dp_col, HI) + _dot_nt(trit, dp_row, HI) + dtot
        ddt_ref[...] = ddtx[:, :SSD_HEADS] + ar_ref[...] * d_adt
        da_ref[...] += jnp.sum(dt * d_adt, axis=0, keepdims=True)

    return pl.pallas_call(
        body, name="ssd_bwd", grid=(2, nc), in_specs=specs,
        out_specs=[pl.BlockSpec((None, CHUNK, D_INNER), lambda d, c: (d, cidx(d, c), 0)),
                   pl.BlockSpec((None, CHUNK, 1024), lambda d, c: (d, cidx(d, c), 0)),
                   pl.BlockSpec((None, CHUNK, 1024), lambda d, c: (d, cidx(d, c), 0)),
                   pl.BlockSpec((None, CHUNK, SSD_HEADS), lambda d, c: (d, cidx(d, c), 0)),
                   pl.BlockSpec((None, 1, SSD_HEADS), lambda d, c: (d, 0, 0))],
        out_shape=[jax.ShapeDtypeStruct((2, t, D_INNER), F32), jax.ShapeDtypeStruct((2, t, 1024), F32),
                   jax.ShapeDtypeStruct((2, t, 1024), F32), jax.ShapeDtypeStruct((2, t, SSD_HEADS), F32),
                   jax.ShapeDtypeStruct((2, 1, SSD_HEADS), F32)],
        scratch_shapes=[pltpu.VMEM((D_INNER, SSD_STATE), F32)],
        compiler_params=_cparams("arbitrary", "arbitrary"),
    )(act, act, act, dt2, dt2t, a_row, a_col, dte, dy, states)


REP = ATTN_HEADS // ATTN_KV
SCALE = ATTN_DIM ** -0.5
GROUP_Q = REP * ATTN_DIM
K_BLK0 = D_MODEL // LANES
V_BLK0 = K_BLK0 + ATTN_KV


def _attn_specs(nb):
    q = pl.BlockSpec((BLOCK, GROUP_Q), lambda g, n: (n, g))

    def kv(blk0):
        return [pl.BlockSpec((BLOCK, LANES), lambda g, n: (jnp.maximum(n - 1, 0), blk0 + g)),
                pl.BlockSpec((BLOCK, LANES), lambda g, n: (n, blk0 + g)),
                pl.BlockSpec((BLOCK, LANES), lambda g, n: (jnp.minimum(n + 1, nb - 1), blk0 + g))]

    bias = pl.BlockSpec((None, REP, BLOCK, 3 * BLOCK), lambda g, n: (g, 0, 0, 0))
    sink = pl.BlockSpec((None, REP, 1, LANES), lambda g, n: (g, 0, 0, 0))
    return q, kv(K_BLK0), kv(V_BLK0), bias, sink


def _band_mask():
    ii = lax.broadcasted_iota(jnp.int32, (BLOCK, 3 * BLOCK), 0)
    jj = lax.broadcasted_iota(jnp.int32, (BLOCK, 3 * BLOCK), 1)
    return (jj >= ii) & (jj - 2 * BLOCK <= ii)


def _attn_valid(n, nb):
    jj = lax.broadcasted_iota(jnp.int32, (1, 3 * BLOCK), 1)
    return ((jj >= BLOCK) | (n > 0)) & ((jj < 2 * BLOCK) | (n < nb - 1))


def _attn_probs(q, kcat, bias, snk, valid):
    s = jnp.where(valid, _dot_nt(q, kcat) + bias, NEG)
    m = jnp.maximum(jnp.max(s, axis=1, keepdims=True), snk)
    p = jnp.exp(s - m)
    es = jnp.exp(snk - m)
    r = 1.0 / (jnp.sum(p, axis=1, keepdims=True) + es)
    return p * r, es * r


def _stack_heads(ref, lane):
    parts = []
    for pr in range(REP // 2):
        tile = ref[:, pr * LANES:(pr + 1) * LANES]
        parts += [jnp.where(lane, tile, 0.0), jnp.where(lane, 0.0, tile)]
    return jnp.concatenate(parts, axis=0)


def _unstack_heads(x, lane):
    return jnp.concatenate([jnp.where(lane, x[(2 * pr) * BLOCK:(2 * pr + 1) * BLOCK], x[(2 * pr + 1) * BLOCK:(2 * pr + 2) * BLOCK])
                            for pr in range(REP // 2)], axis=1)


def _stack_bias(b_ref, s_ref):
    bias = jnp.concatenate([b_ref[r] for r in range(REP)], axis=0)
    snk = jnp.concatenate([jnp.broadcast_to(s_ref[r][:, 0:1], (BLOCK, 1)) for r in range(REP)], axis=0)
    return bias, snk


def _attn_fwd(qkv, bias4, sink4):
    t = qkv.shape[0]
    nb = t // BLOCK
    qs, ks, vs, bs, ss = _attn_specs(nb)

    def body(q_ref, kp_ref, kc_ref, kn_ref, vp_ref, vc_ref, vn_ref, b_ref, s_ref, o_ref):
        n = pl.program_id(1)
        kcat = jnp.concatenate([kp_ref[...], kc_ref[...], kn_ref[...]], axis=0)
        vcat = jnp.concatenate([vp_ref[...], vc_ref[...], vn_ref[...]], axis=0)
        valid = _attn_valid(n, nb)
        lane = lax.broadcasted_iota(jnp.int32, (BLOCK, LANES), 1) < ATTN_DIM
        bias, snk = _stack_bias(b_ref, s_ref)
        pn, _ = _attn_probs(_stack_heads(q_ref, lane) * SCALE, kcat, bias, snk, valid)
        o_ref[...] = _unstack_heads(_dot(pn.astype(BF16), vcat), lane).astype(o_ref.dtype)

    return pl.pallas_call(
        body, name="attn_fwd", grid=(ATTN_KV, nb), in_specs=[qs] + ks + vs + [bs, ss],
        out_specs=qs, out_shape=jax.ShapeDtypeStruct((t, D_MODEL), BF16),
        compiler_params=_cparams("parallel", "arbitrary"),
    )(qkv, qkv, qkv, qkv, qkv, qkv, qkv, bias4, sink4)


def _attn_bwd(qkv, bias4, sink4, do):
    t = qkv.shape[0]
    nb = t // BLOCK
    qs, ks, vs, bs, ss = _attn_specs(nb)
    part = pl.BlockSpec((3, BLOCK, LANES), lambda g, n: (0, n, g))

    def body(q_ref, kp_ref, kc_ref, kn_ref, vp_ref, vc_ref, vn_ref, b_ref, s_ref, do_ref,
             dq_ref, dk_ref, dv_ref, db_ref, ds_ref):
        n = pl.program_id(1)

        @pl.when(n == 0)
        def _():
            db_ref[...] = jnp.zeros_like(db_ref)
            ds_ref[...] = jnp.zeros_like(ds_ref)

        kcat = jnp.concatenate([kp_ref[...], kc_ref[...], kn_ref[...]], axis=0)
        vcat = jnp.concatenate([vp_ref[...], vc_ref[...], vn_ref[...]], axis=0)
        valid = _attn_valid(n, nb)
        lane = lax.broadcasted_iota(jnp.int32, (BLOCK, LANES), 1) < ATTN_DIM
        bias, snk = _stack_bias(b_ref, s_ref)
        q = _stack_heads(q_ref, lane)
        do = _stack_heads(do_ref, lane)
        pn, psink = _attn_probs(q * SCALE, kcat, bias, snk, valid)
        dp = _dot_nt(do, vcat)
        delta = jnp.sum(pn * dp, axis=1, keepdims=True)
        dsc = pn * (dp - delta)
        dsink = psink * delta
        for r in range(REP):
            rows = slice(r * BLOCK, (r + 1) * BLOCK)
            db_ref[r] += dsc[rows]
            ds_ref[r] += jnp.broadcast_to(-jnp.sum(dsink[rows], axis=0, keepdims=True), (1, LANES))
        dsb = (dsc * SCALE).astype(BF16)
        dq_ref[...] = _unstack_heads(_dot(dsb, kcat), lane).astype(dq_ref.dtype)
        dk = _dot_tn(dsb, q)
        dv = _dot_tn(pn.astype(BF16), do)
        for j in range(3):
            dk_ref[j] = dk[j * BLOCK:(j + 1) * BLOCK]
            dv_ref[j] = dv[j * BLOCK:(j + 1) * BLOCK]

    kv_cols = ATTN_KV * LANES
    return pl.pallas_call(
        body, name="attn_bwd", grid=(ATTN_KV, nb), in_specs=[qs] + ks + vs + [bs, ss, qs],
        out_specs=[qs, part, part, bs, ss],
        out_shape=[jax.ShapeDtypeStruct((t, D_MODEL), BF16), jax.ShapeDtypeStruct((3, t, kv_cols), F32),
                   jax.ShapeDtypeStruct((3, t, kv_cols), F32), jax.ShapeDtypeStruct(bias4.shape, F32),
                   jax.ShapeDtypeStruct(sink4.shape, F32)],
        compiler_params=_cparams("parallel", "arbitrary"),
    )(qkv, qkv, qkv, qkv, qkv, qkv, qkv, bias4, sink4, do)


def _kv_combine(name, parts):
    _, t, cols = parts.shape

    def body(p_ref, o_ref):
        z = jnp.zeros((BLOCK, LANES), F32)
        from_next = jnp.concatenate([p_ref[0, BLOCK:, :], z], axis=0)
        from_prev = jnp.concatenate([z, p_ref[2, :t - BLOCK, :]], axis=0)
        o_ref[...] = (from_next + p_ref[1] + from_prev).astype(o_ref.dtype)

    return pl.pallas_call(
        body, name=name, grid=(cols // LANES,),
        in_specs=[pl.BlockSpec((3, t, LANES), lambda g: (0, 0, g))],
        out_specs=pl.BlockSpec((t, LANES), lambda g: (0, g)),
        out_shape=jax.ShapeDtypeStruct((t, cols), BF16),
        compiler_params=_cparams("parallel"),
    )(parts)


def _t5_bucket(rel):
    nb = N_BUCKETS // 2
    max_exact = nb // 2
    ret = jnp.where(rel > 0, nb, 0)
    n = jnp.abs(rel)
    nf = jnp.maximum(n, 1).astype(jnp.float32)
    large = max_exact + (jnp.log(nf / max_exact) / math.log(MAX_DISTANCE / max_exact) * (nb - max_exact)).astype(jnp.int32)
    large = jnp.minimum(large, nb - 1)
    return ret + jnp.where(n < max_exact, n, large)


def _bucket_map():
    i = jnp.arange(BLOCK)[:, None]
    j = jnp.arange(3 * BLOCK)[None, :]
    return _t5_bucket(j - BLOCK - i)


def _bias_from_table(table, onehot_t):
    def body(t_ref, o_ref, out_ref):
        out_ref[...] = _dot(t_ref[...], o_ref[...], HI)

    return pl.pallas_call(body, name="bias_from_table", out_shape=jax.ShapeDtypeStruct((ATTN_HEADS, onehot_t.shape[1]), F32),
                          compiler_params=pltpu.CompilerParams(vmem_limit_bytes=VMEM_LIMIT_BYTES))(table.T, onehot_t)


def _bias_table_grad(dbias, onehot_t):
    def body(d_ref, o_ref, out_ref):
        out_ref[...] = _dot_nt(d_ref[...], o_ref[...], HI)

    return pl.pallas_call(body, name="bias_table_grad", out_shape=jax.ShapeDtypeStruct((ATTN_HEADS, N_BUCKETS), F32),
                          compiler_params=pltpu.CompilerParams(vmem_limit_bytes=VMEM_LIMIT_BYTES))(dbias, onehot_t)


HBM_SPEC = pl.BlockSpec(memory_space=pl.ANY)


def _comm_call(name, body, xs, out_shapes, n_sems):
    n = len(xs)
    return pl.pallas_call(
        body, name=name, in_specs=[HBM_SPEC] * n, out_specs=[HBM_SPEC] * n, out_shape=out_shapes,
        scratch_shapes=[pltpu.SemaphoreType.DMA((n * n_sems,)), pltpu.SemaphoreType.DMA((n * n_sems,))],
    )(*xs)


def _allgather_chips(name, xs):
    n = len(xs)

    def body(*refs):
        x_refs, out_refs, (send_sems, recv_sems) = refs[:n], refs[n:2 * n], refs[2 * n:]
        mx, my, mc = lax.axis_index("x"), lax.axis_index("y"), lax.axis_index("c")
        me = 2 * mx + my
        chips = [(1 - mx, my), (mx, 1 - my), (1 - mx, 1 - my)]
        sibling = (mx, my, 1 - mc)

        def part(i, slot, h):
            r2 = xs[i].shape[0] // 2
            return out_refs[i].at[slot, pl.ds(h * r2, r2)]

        def copy(i, k, src, dst, to):
            return pltpu.make_async_remote_copy(src_ref=src, dst_ref=dst, send_sem=send_sems.at[6 * i + k],
                                                recv_sem=recv_sems.at[6 * i + k], device_id=to, device_id_type=MESH)

        first = [copy(i, k, x_refs[i].at[pl.ds(mc * (xs[i].shape[0] // 2), xs[i].shape[0] // 2)], part(i, me, mc), (px, py, mc))
                 for i in range(n) for k, (px, py) in enumerate(chips)]
        for cp in first:
            cp.start()
        passed = []
        for k, (px, py) in enumerate(chips):
            for i in range(n):
                landed = part(i, 2 * px + py, mc)
                copy(i, k, landed, landed, (px, py, mc)).wait_recv()
                passed.append(copy(i, 3 + k, landed, landed, sibling))
                passed[-1].start()
        for k, (px, py) in enumerate(chips):
            for i in range(n):
                theirs = part(i, 2 * px + py, 1 - mc)
                copy(i, 3 + k, theirs, theirs, sibling).wait_recv()
        for cp in first + passed:
            cp.wait_send()

    return _comm_call(name, body, xs, [jax.ShapeDtypeStruct((4,) + x.shape, x.dtype) for x in xs], 6)


SEM_SPEC = pl.BlockSpec(memory_space=pltpu.SEMAPHORE)
DATAFLOW = pltpu.SideEffectType.DATAFLOW_SIDE_EFFECTING


def _send_start(name, xs, after, scatter, collective_id):
    n = len(xs)

    def body(*refs):
        x_refs, land_refs = refs[:n], refs[n:2 * n]
        send_sems, recv_sems, token = refs[2 * n + 1], refs[2 * n + 2], refs[-1]
        mx, my, mc = lax.axis_index("x"), lax.axis_index("y"), lax.axis_index("c")
        chips = [(1 - mx, my), (mx, 1 - my), (1 - mx, 1 - my)]
        barrier = pltpu.get_barrier_semaphore()
        for px, py in chips:
            pl.semaphore_signal(barrier, inc=1, device_id=(px, py, mc), device_id_type=MESH)
        pl.semaphore_wait(barrier, len(chips))
        for i in range(n):
            for k, (px, py) in enumerate(chips):
                src = x_refs[i].at[2 * px + py] if scatter else x_refs[i]
                pltpu.make_async_remote_copy(src_ref=src, dst_ref=land_refs[i].at[2 * mx + my], send_sem=send_sems.at[3 * i + k],
                                             recv_sem=recv_sems.at[3 * i + k], device_id=(px, py, mc), device_id_type=MESH).start()
        token[...] = jnp.zeros_like(token)

    lands = [lax.empty((4,) + (x.shape[1:] if scatter else x.shape), x.dtype) for x in xs]
    hbm = [pltpu.HBM(a.shape, a.dtype) for a in list(xs) + lands]
    out = pl.pallas_call(
        body, name=name,
        out_shape=(pltpu.SemaphoreType.DMA((3 * n,)), pltpu.SemaphoreType.DMA((3 * n,)), *hbm, jax.ShapeDtypeStruct((SUBLANES, LANES), F32)),
        in_specs=(HBM_SPEC,) * (2 * n + 1),
        out_specs=(SEM_SPEC, SEM_SPEC) + (HBM_SPEC,) * (2 * n) + (pl.BlockSpec(memory_space=pltpu.VMEM),),
        input_output_aliases={i: 2 + i for i in range(2 * n)},
        compiler_params=pltpu.CompilerParams(has_side_effects=DATAFLOW, collective_id=collective_id),
    )(*[pltpu.with_memory_space_constraint(a, pltpu.HBM) for a in list(xs) + lands], after)
    return out[0], out[1], list(out[2:2 + n]), list(out[2 + n:2 + 2 * n]), out[-1]


def _send_wait(name, send_sems, recv_sems, x_thrus, land_thrus, after, scatter):
    n = len(x_thrus)

    def body(*refs):
        x_refs, land_refs, send_sems, recv_sems = refs[:n], refs[n:2 * n], refs[2 * n], refs[2 * n + 1]
        mx, my, mc = lax.axis_index("x"), lax.axis_index("y"), lax.axis_index("c")
        chips = [(1 - mx, my), (mx, 1 - my), (1 - mx, 1 - my)]
        for i in range(n):
            for k, (px, py) in enumerate(chips):
                src = x_refs[i].at[0] if scatter else x_refs[i]
                copy = pltpu.make_async_remote_copy(src_ref=src, dst_ref=land_refs[i].at[2 * px + py], send_sem=send_sems.at[3 * i + k],
                                                    recv_sem=recv_sems.at[3 * i + k], device_id=(px, py, mc), device_id_type=MESH)
                copy.wait_send()
                copy.wait_recv()

    arrs = list(x_thrus) + list(land_thrus)
    out = pl.pallas_call(
        body, name=name, out_shape=tuple(pltpu.HBM(a.shape, a.dtype) for a in arrs),
        in_specs=(HBM_SPEC,) * (2 * n) + (SEM_SPEC, SEM_SPEC, HBM_SPEC), out_specs=(HBM_SPEC,) * (2 * n),
        input_output_aliases={i: i for i in range(2 * n)},
        compiler_params=pltpu.CompilerParams(has_side_effects=DATAFLOW),
    )(*arrs, send_sems, recv_sems, after)
    return list(out[n:])


def _swap_cores(name, xs, slab=True):
    n = len(xs)

    def body(*refs):
        x_refs, out_refs, (send_sems, recv_sems) = refs[:n], refs[n:2 * n], refs[2 * n:]
        mx, my, mc = lax.axis_index("x"), lax.axis_index("y"), lax.axis_index("c")
        sends = [pltpu.make_async_remote_copy(src_ref=x_refs[i].at[1 - mc] if slab else x_refs[i], dst_ref=out_refs[i],
                                              send_sem=send_sems.at[i], recv_sem=recv_sems.at[i],
                                              device_id=(mx, my, 1 - mc), device_id_type=MESH)
                 for i in range(n)]
        for cp in sends:
            cp.start()
        for cp in sends:
            cp.wait()

    return _comm_call(name, body, xs, [jax.ShapeDtypeStruct(x.shape[1:] if slab else x.shape, x.dtype) for x in xs], 1)


def _scatter_chips(name, gs):
    n = len(gs)

    def body(*refs):
        g_refs, out_refs, (send_sems, recv_sems) = refs[:n], refs[n:2 * n], refs[2 * n:]
        mx, my, mc = lax.axis_index("x"), lax.axis_index("y"), lax.axis_index("c")
        me = 2 * mx + my
        chips = [(1 - mx, my), (mx, 1 - my), (1 - mx, 1 - my)]

        def copy(i, k, src, dst, to):
            return pltpu.make_async_remote_copy(src_ref=src, dst_ref=dst, send_sem=send_sems.at[3 * i + k],
                                                recv_sem=recv_sems.at[3 * i + k], device_id=to, device_id_type=MESH)

        sends = [copy(i, k, g_refs[i].at[2 * px + py], out_refs[i].at[me], (px, py, mc))
                 for i in range(n) for k, (px, py) in enumerate(chips)]
        for cp in sends:
            cp.start()
        for i in range(n):
            for k, (px, py) in enumerate(chips):
                copy(i, k, g_refs[i].at[0], out_refs[i].at[2 * px + py], (px, py, mc)).wait_recv()
        for cp in sends:
            cp.wait_send()

    return _comm_call(name, body, gs, [jax.ShapeDtypeStruct(g.shape, g.dtype) for g in gs], 3)


def _allgather_all(name, x):
    def body(x_ref, out_ref, send_sems, recv_sems, local_sem):
        mx, my, mc = lax.axis_index("x"), lax.axis_index("y"), lax.axis_index("c")
        me = 4 * mx + 2 * my + mc
        flips = [(fx, fy, fc) for fx in (0, 1) for fy in (0, 1) for fc in (0, 1)][1:]
        peers = [(mx ^ fx, my ^ fy, mc ^ fc) for fx, fy, fc in flips]
        mine = pltpu.make_async_copy(x_ref, out_ref.at[me], local_sem)
        mine.start()
        sends = [pltpu.make_async_remote_copy(src_ref=x_ref, dst_ref=out_ref.at[me], send_sem=send_sems.at[k],
                                              recv_sem=recv_sems.at[k], device_id=peer, device_id_type=MESH)
                 for k, peer in enumerate(peers)]
        for cp in sends:
            cp.start()
        for k, (px, py, pc) in enumerate(peers):
            pltpu.make_async_remote_copy(src_ref=x_ref, dst_ref=out_ref.at[4 * px + 2 * py + pc], send_sem=send_sems.at[k],
                                         recv_sem=recv_sems.at[k], device_id=(px, py, pc), device_id_type=MESH).wait_recv()
        for cp in sends:
            cp.wait_send()
        mine.wait()

    return pl.pallas_call(
        body, name=name, in_specs=[HBM_SPEC], out_specs=HBM_SPEC,
        out_shape=jax.ShapeDtypeStruct((8,) + x.shape, x.dtype),
        scratch_shapes=[pltpu.SemaphoreType.DMA((7,)), pltpu.SemaphoreType.DMA((7,)), pltpu.SemaphoreType.DMA],
    )(x)


def _sum_slots(name, st, tb=256):
    n, r, c = st.shape
    tb = _pick(r, (tb, 128, 32))

    def body(s_ref, o_ref):
        acc = s_ref[0].astype(F32)
        for k in range(1, n):
            acc = acc + s_ref[k].astype(F32)
        o_ref[...] = acc

    return pl.pallas_call(
        body, name=name, grid=(r // tb,), in_specs=[pl.BlockSpec((n, tb, c), lambda i: (0, i, 0))],
        out_specs=pl.BlockSpec((tb, c), lambda i: (i, 0)), out_shape=jax.ShapeDtypeStruct((r, c), F32),
        compiler_params=_cparams("parallel"),
    )(st)


def _adamw(name, w, g, m, v):
    def fn(w, g, m, v):
        m2 = ADAM_B1 * m + (1.0 - ADAM_B1) * g
        v2 = ADAM_B2 * v + (1.0 - ADAM_B2) * jnp.square(g)
        m_hat = m2 / (1.0 - ADAM_B1 ** ADAM_STEP)
        v_hat = v2 / (1.0 - ADAM_B2 ** ADAM_STEP)
        delta = -ADAM_LR * (m_hat / (jnp.sqrt(v_hat) + ADAM_EPS) + ADAM_WD * w)
        return [delta, m2, v2], []

    tb = _pick(w.shape[0], (256, 32))
    return _rowwise(name, fn, [_row(w), _row(g), _row(m), _row(v)], [], [(w.shape[1], F32)] * 3, tb=tb)


BIG = ("w_ssd_out", "w_attn_out", "w_o", "w_mlp_in", "w_mlp_out", "conv_w")
SMALL = ("pre_mix_norm", "b_gate", "conv_b", "dt_bias", "a_log", "d_skip", "ssd_norm", "attn_sink", "rel_bias_table",
         "post_mix_norm", "pre_mlp_norm", "post_mlp_norm")
ALL_W = ("pre_mix_norm", "w_in", "b_gate", "conv_w", "conv_b", "dt_bias", "a_log", "d_skip", "ssd_norm", "w_ssd_out",
         "attn_sink", "rel_bias_table", "w_attn_out", "w_o", "post_mix_norm", "pre_mlp_norm", "w_mlp_in", "w_mlp_out",
         "post_mlp_norm")


def _pack_rows(parts, rows, dtype):
    flat = jnp.concatenate([p.reshape(-1, D_MODEL).astype(dtype) for p in parts], axis=0)
    return jnp.pad(flat, ((0, rows - flat.shape[0]), (0, 0)))


def _pack_big(shards, dtype):
    return _pack_rows([shards[n] for n in BIG], BIG_ROWS, dtype)


def _unpack_big(flat, like):
    out, r = {}, 0
    for n in BIG:
        shp = like[n].shape
        nr = math.prod(shp) // D_MODEL
        out[n] = flat[r:r + nr].reshape(shp)
        r += nr
    return out


def _pack_small(parts, extra=None):
    flat = jnp.concatenate([parts[n].reshape(-1).astype(F32) for n in SMALL] + ([extra.reshape(-1)] if extra is not None else []))
    return jnp.pad(flat, (0, SMALL_ROWS * D_MODEL - flat.shape[0])).reshape(SMALL_ROWS, D_MODEL)


def _unpack_small(flat2, like):
    flat = flat2.reshape(-1)
    out, r = {}, 0
    for n in SMALL:
        shp = like[n].shape
        k = math.prod(shp)
        out[n] = flat[r:r + k].reshape(shp)
        r += k
    return out, flat[r]


def _shard_of_full(name, full, s):
    if name in ("w_in", "w_mlp_in"):
        w = full.shape[2] // 4
        return full[:, :, s * w:(s + 1) * w]
    if name == "conv_w":
        w = full.shape[3] // 4
        return full[:, :, :, s * w:(s + 1) * w]
    w = full.shape[1] // 4
    return full[:, s * w:(s + 1) * w, :]


def _full_of_shards(name, shards):
    axis = {"w_in": 2, "w_mlp_in": 2, "conv_w": 3}.get(name, 1)
    return jnp.concatenate(shards, axis=axis)


def _to_proj_layout(w):
    z, xbc, dt, q, k, v, gates = (w[..., 0:2048], w[..., 2048:6144], w[..., 6144:6208], w[..., 6208:7232],
                                  w[..., 7232:7488], w[..., 7488:7744], w[..., 7744:9792])
    pad = jnp.zeros(w.shape[:-1] + (N_MAIN - OFF_DT - dt.shape[-1],), w.dtype)

    def doubled(a):
        h = a.reshape(a.shape[:-1] + (ATTN_KV, 1, ATTN_DIM))
        return jnp.broadcast_to(h, a.shape[:-1] + (ATTN_KV, 2, ATTN_DIM)).reshape(a.shape[:-1] + (2 * a.shape[-1],))

    return jnp.concatenate([z, gates, xbc, dt, pad, q, doubled(k), doubled(v)], axis=-1)


def _from_proj_layout(w):
    z, gates, xbc, dt, q, k2, v2 = (w[..., 0:2048], w[..., 2048:4096], w[..., 4096:8192], w[..., 8192:8256],
                                    w[..., 8320:9344], w[..., 9344:9856], w[..., 9856:10368])

    def folded(a):
        return a.reshape(a.shape[:-1] + (ATTN_KV, 2, ATTN_DIM)).sum(axis=-2).reshape(a.shape[:-1] + (a.shape[-1] // 2,))

    return jnp.concatenate([z, xbc, dt, q, folded(k2), folded(v2), gates], axis=-1)


def _layer_fwd(h1, x, W, P, l, bias4):
    t = x.shape[0]
    S = {"x": x, "h1": h1}
    proj = _matmul("proj", h1, W["w_in_main"][l], "nn")
    qkv = _matmul("proj_qkv", h1, W["w_in_qkv"][l], "nn", out_dtype=BF16)
    S["proj"] = proj
    pre, act = _conv_fwd(proj, W["conv_w"][l], P["conv_b"][l].reshape(1, CONV_DIM))
    S["pre"], S["act"] = pre, act

    dtb = jnp.pad(P["dt_bias"][l].reshape(1, 2 * SSD_HEADS), ((0, 0), (0, LANES - 2 * SSD_HEADS)))

    def dt_fn(raw, b):
        v = raw + b
        dt = jnp.maximum(v, 0.0) + jnp.log1p(jnp.exp(-jnp.abs(v)))
        expand = (jnp.right_shift(lax.broadcasted_iota(jnp.int32, (LANES, 2 * D_INNER), 1), 6)
                  == lax.broadcasted_iota(jnp.int32, (LANES, 2 * D_INNER), 0)).astype(BF16)
        hi = dt.astype(BF16)
        rest = dt - hi.astype(F32)
        mid = rest.astype(BF16)
        lo = (rest - mid.astype(F32)).astype(BF16)
        return [dt, _dot(hi, expand) + _dot(mid, expand) + _dot(lo, expand)], []

    dt, dte = _rowwise("dt_fwd", dt_fn, [_row(proj, LANES, OFF_DT // LANES)], [dtb], [(LANES, F32), (2 * D_INNER, F32)], tb=512)
    dt2 = jnp.stack([dt[:, 0:SSD_HEADS], dt[:, SSD_HEADS:2 * SSD_HEADS]])
    dt2t = dt2.transpose(0, 2, 1)
    a = -jnp.exp(P["a_log"][l])
    a_row, a_col = a.reshape(2, 1, SSD_HEADS), a.reshape(2, SSD_HEADS, 1)
    S["dt2"], S["dt2t"], S["a_row"], S["a_col"], S["dte"] = dt2, dt2t, a_row, a_col, dte
    y2, states = _ssd_fwd(act, dt2, dt2t, a_row, a_col, dte)
    S["states"] = states

    dsk = jnp.repeat(P["d_skip"][l], SSD_HEAD_DIM).reshape(1, D_INNER)
    nw = P["ssd_norm"][l].reshape(1, D_INNER)
    S["dsk"], S["nw"] = dsk, nw

    def gn_fn(yf, yb, xs, z, dsk, nw):
        y = yf + yb + xs * dsk
        return [y, _gated_norm_fwd(y, z, nw)], []

    y, yn = _rowwise("gated_norm_fwd", gn_fn,
                     [_row(y2, lead=0), _row(y2, lead=1), _row(act, D_INNER, 0), _row(proj, D_INNER, OFF_Z // D_INNER)],
                     [dsk, nw], [(D_INNER, F32), (D_INNER, BF16)])
    S["y"], S["yn"] = y, yn
    if "late" in W:
        W.update(W.pop("late")(yn))
    y_ssd = _matmul("ssd_out", yn, W["w_ssd_out"][l], "nn")
    S["y_ssd"] = y_ssd

    sink4 = jnp.broadcast_to(P["attn_sink"][l].reshape(ATTN_KV, REP, 1, 1), (ATTN_KV, REP, 1, LANES))
    S["qkv"], S["sink4"] = qkv, sink4
    o = _attn_fwd(qkv, bias4, sink4)
    S["o"] = o
    y_attn = _matmul("attn_out", o, W["w_attn_out"][l], "nn")
    S["y_attn"] = y_attn

    bg = P["b_gate"][l].reshape(1, 2 * D_MODEL)
    S["bg"] = bg

    def merge_fn(gates, ys, ya, b):
        g = jax.nn.sigmoid(gates + b)
        return [g[:, :D_MODEL] * ys + g[:, D_MODEL:] * ya], []

    (mix_in,) = _rowwise("merge_fwd", merge_fn, [_row(proj, 2 * D_MODEL, OFF_G // (2 * D_MODEL)), _row(y_ssd), _row(y_attn)],
                         [bg], [(D_MODEL, BF16)])
    S["mix_in"] = mix_in
    mixed = _matmul("w_o", mix_in, W["w_o"][l], "nn")
    S["mixed"] = mixed

    g_pm = P["post_mix_norm"][l].reshape(1, D_MODEL)
    g_pl = P["pre_mlp_norm"][l].reshape(1, D_MODEL)

    def postmix_fn(x, mixed, g1, g2):
        x2 = x + _rms_fwd(mixed, g1)
        return [x2, _rms_fwd(x2, g2)], []

    x2, h2 = _rowwise("post_mix_fwd", postmix_fn, [_row(x), _row(mixed)], [g_pm, g_pl], [(D_MODEL, F32), (D_MODEL, BF16)])
    S["x2"], S["h2"] = x2, h2
    a1 = _matmul("mlp_in", h2, W["w_mlp_in"][l], "nn", out_dtype=BF16,
                 epilogue=lambda acc: jnp.square(jnp.maximum(acc, 0.0)))
    S["a1"] = a1
    f2 = _matmul("mlp_out", a1, W["w_mlp_out"][l], "nn")
    S["f2"] = f2
    return S


def _layer_bwd(S, dx3, W, P, l, bias4, onehot_t, zero=None):
    t = dx3.shape[0]
    G = {}
    g_pmlp = P["post_mlp_norm"][l].reshape(1, D_MODEL)
    if zero is not None:
        g_pmlp = g_pmlp + zero

    def b1_fn(f2, dx3, g):
        df2, dg = _rms_bwd(f2, g, dx3)
        return [df2], [dg]

    df2, G["post_mlp_norm"] = _rowwise("post_mlp_bwd", b1_fn, [_row(S["f2"]), _row(dx3)], [g_pmlp], [(D_MODEL, BF16)], [(1, D_MODEL)])
    df1 = _matmul("d_f1", df2, W["w_mlp_out"][l], "nt", out_dtype=BF16,
                  epilogue=lambda acc, a1: acc * (2.0 * jnp.sqrt(a1.astype(F32))), extras=[S["a1"]])
    G["w_mlp_out"] = _matmul("dw_mlp_out", S["a1"], df2, "tn")
    dh2 = _matmul("d_h2", df1, W["w_mlp_in"][l], "nt")
    G["w_mlp_in"] = _matmul("dw_mlp_in", S["h2"], df1, "tn")

    g_pm = P["post_mix_norm"][l].reshape(1, D_MODEL)
    g_pl = P["pre_mlp_norm"][l].reshape(1, D_MODEL)

    def b3_fn(x2, dh2, dx3, mixed, g_pl, g_pm):
        d1, dgl = _rms_bwd(x2, g_pl, dh2)
        dx2 = dx3 + d1
        dmixed, dgm = _rms_bwd(mixed, g_pm, dx2)
        return [dx2, dmixed], [dgl, dgm]

    dx2, dmixed, G["pre_mlp_norm"], G["post_mix_norm"] = _rowwise(
        "post_mix_bwd", b3_fn, [_row(S["x2"]), _row(dh2), _row(dx3), _row(S["mixed"])], [g_pl, g_pm],
        [(D_MODEL, F32), (D_MODEL, BF16)], [(1, D_MODEL), (1, D_MODEL)])
    dmix_in = _matmul("d_mix_in", dmixed, W["w_o"][l], "nt")
    G["w_o"] = _matmul("dw_o", S["mix_in"], dmixed, "tn")

    proj = S["proj"]

    def b4_fn(gates, ys, ya, dmix, b):
        g = jax.nn.sigmoid(gates + b)
        gs, ga = g[:, :D_MODEL], g[:, D_MODEL:]
        dg = jnp.concatenate([ys * dmix, ya * dmix], axis=-1) * g * (1.0 - g)
        return [gs * dmix, ga * dmix, dg], [jnp.sum(dg, axis=0, keepdims=True)]

    dy_ssd, dy_attn, dgates, G["b_gate"] = _rowwise(
        "merge_bwd", b4_fn, [_row(proj, 2 * D_MODEL, OFF_G // (2 * D_MODEL)), _row(S["y_ssd"]), _row(S["y_attn"]), _row(dmix_in)],
        [S["bg"]], [(D_MODEL, BF16), (D_MODEL, BF16), (2 * D_MODEL, BF16)], [(1, 2 * D_MODEL)])

    dyn = _matmul("d_yn", dy_ssd, W["w_ssd_out"][l], "nt")
    G["w_ssd_out"] = _matmul("dw_ssd_out", S["yn"], dy_ssd, "tn")
    do = _matmul("d_o", dy_attn, W["w_attn_out"][l], "nt", out_dtype=BF16)
    G["w_attn_out"] = _matmul("dw_attn_out", S["o"], dy_attn, "tn")

    dq, dkp, dvp, dbias4, dsink4 = _attn_bwd(S["qkv"], bias4, S["sink4"], do)
    dk = _kv_combine("dk_combine", dkp)
    dv = _kv_combine("dv_combine", dvp)
    G["attn_sink"] = dsink4[:, :, 0, 0].reshape(ATTN_HEADS)
    G["rel_bias_table"] = _bias_table_grad(dbias4.reshape(ATTN_HEADS, BLOCK * 3 * BLOCK), onehot_t).T

    act = S["act"]

    def b5_fn(y, z, xs, dyn, nw, dsk):
        dy, dz, dnw = _gated_norm_bwd(y, z, nw, dyn)
        return [dy, dz], [dnw, jnp.sum(dy * xs, axis=0, keepdims=True)]

    dy, dz, G["ssd_norm"], dskip_cols = _rowwise(
        "gated_norm_bwd", b5_fn, [_row(S["y"]), _row(proj, D_INNER, OFF_Z // D_INNER), _row(act, D_INNER, 0), _row(dyn)],
        [S["nw"], S["dsk"]], [(D_INNER, F32), (D_INNER, BF16)], [(1, D_INNER), (1, D_INNER)])
    G["d_skip"] = dskip_cols.reshape(SSD_HEADS, SSD_HEAD_DIM).sum(axis=-1)

    dxs2, dbs2, dcs2, ddt2, da2 = _ssd_bwd(act, S["dt2"], S["dt2t"], S["a_row"], S["a_col"], S["dte"], dy, S["states"])
    G["a_log"] = da2.reshape(2, SSD_HEADS) * S["a_row"].reshape(2, SSD_HEADS)

    def b6_fn(dxf, dxb, dy, dbf, dbb, dcf, dcb, pre, dsk):
        dact = jnp.concatenate([dxf + dxb + dy * dsk, dbf + dbb, dcf + dcb], axis=-1)
        return [dact * _silu_grad(pre)], []

    (dpre,) = _rowwise("silu_bwd", b6_fn,
                       [_row(dxs2, lead=0), _row(dxs2, lead=1), _row(dy), _row(dbs2, lead=0), _row(dbs2, lead=1),
                        _row(dcs2, lead=0), _row(dcs2, lead=1), _row(S["pre"])], [S["dsk"]], [(CONV_DIM, F32)], tb=128)
    du, dconv_w, dconv_b = _conv_bwd(dpre, proj, W["conv_w"][l])
    G["conv_w"] = dconv_w.reshape(SSD_CONV, 1, CONV_DIM)
    G["conv_b"] = dconv_b.reshape(CONV_DIM)

    dtb = jnp.pad(P["dt_bias"][l].reshape(1, 2 * SSD_HEADS), ((0, 0), (0, LANES - 2 * SSD_HEADS)))
    ddt = jnp.pad(jnp.concatenate([ddt2[0], ddt2[1]], axis=-1), ((0, 0), (0, LANES - 2 * SSD_HEADS)))

    def b7_fn(raw, ddt, b):
        draw = ddt * jax.nn.sigmoid(raw + b)
        return [draw], [jnp.sum(draw, axis=0, keepdims=True)]

    draw, ddtb = _rowwise("dt_bwd", b7_fn, [_row(proj, LANES, OFF_DT // LANES), _row(ddt)], [dtb], [(LANES, BF16)], [(1, LANES)], tb=512)
    G["dt_bias"] = ddtb[0, :2 * SSD_HEADS].reshape(2, SSD_HEADS)

    dproj = jnp.concatenate([dz, dgates, du, draw, dq, dk, dv], axis=-1)
    G["w_in"] = _from_proj_layout(_matmul("dw_in", S["h1"], dproj, "tn"))
    dh1 = _matmul("d_h1", dproj, W["w_in"][l], "nt")

    g_pre = P["pre_mix_norm"][l].reshape(1, D_MODEL)

    def b8_fn(x, dh1, dx2, g):
        d1, dg = _rms_bwd(x, g, dh1)
        return [dx2 + d1], [dg]

    dx, G["pre_mix_norm"] = _rowwise("pre_mix_bwd", b8_fn, [_row(S["x"]), _row(dh1), _row(dx2)], [g_pre], [(D_MODEL, F32)], [(1, D_MODEL)])
    return dx, G


def _chip_sums(tag, pieces):
    theirs = _swap_cores("swap_" + tag, pieces)
    out = []
    for i, p in enumerate(pieces):
        _, _, r2, c = p.shape
        mine = lax.dynamic_index_in_dim(p, lax.axis_index("c"), axis=0, keepdims=False).reshape(4 * r2, c)
        (cs,) = _rowwise(f"presum_{tag}{i}", lambda a, b: ([a.astype(F32) + b.astype(F32)], []),
                         [_row(mine), _row(theirs[i].reshape(4 * r2, c))], [], [(c, BF16)])
        out.append(cs.reshape(4, r2, c))
    return out


def _halves(blocks):
    r2 = blocks[0].shape[0] // 2
    return jnp.stack([jnp.stack([b[h * r2:(h + 1) * r2] for b in blocks]) for h in range(2)])


def _layer_pieces(g, like):
    cols = like["w_in"].shape[2]
    win = _halves([g["w_in"][:, s * cols:(s + 1) * cols].astype(BF16) for s in range(4)])
    packs = [_pack_rows([_shard_of_full(n, g[n][None], s)[0] for n in BIG], BIG_ROWS // 2, BF16) for s in range(4)]
    return [win, _halves(packs)]


def _unpack_layer(flat, like):
    out, r = {}, 0
    for n in BIG:
        shp = like[n].shape[1:]
        nr = math.prod(shp) // D_MODEL
        out[n] = flat[r:r + nr].reshape(shp)
        r += nr
    return out


def _step(x, target, shard_w, shard_m, shard_v):
    depth, _, win_cols = shard_w["w_in"].shape
    win_rows = depth * D_MODEL

    def win2d(a):
        return a.reshape(win_rows, win_cols)

    conv_rows = shard_w["conv_w"].reshape(-1, D_MODEL)
    win_own = win2d(shard_w["w_in"]).astype(BF16)
    conv_own = jnp.pad(conv_rows, ((0, 16 - conv_rows.shape[0]), (0, 0)))
    win_g, conv_g = _allgather_chips("gather_w_in", [win_own, conv_own])
    packed = _pack_big(shard_w, BF16)
    send_sems, recv_sems, packed_thru, land_thru, token = _send_start("gather_rest_start", [packed], conv_g, False, 0)

    def late(after):
        (landed,) = _send_wait("gather_rest_wait", send_sems, recv_sems, packed_thru, land_thru, after, False)
        per_chip = [_unpack_big(jnp.where(me == s, packed, landed[s]), shard_w) for s in range(4)]
        return {n: _full_of_shards(n, [per_chip[s][n] for s in range(4)]) for n in BIG if n != "conv_w"}

    W = {"late": late}
    me = 2 * lax.axis_index("x") + lax.axis_index("y")
    win_g = [jnp.where(me == s, win_own, win_g[s]) for s in range(4)]
    conv_g = [jnp.where(me == s, conv_own, conv_g[s]) for s in range(4)]
    W["w_in"] = _to_proj_layout(jnp.concatenate([win_g[s].reshape(depth, D_MODEL, win_cols) for s in range(4)], axis=2))
    W["conv_w"] = jnp.concatenate([conv_g[s][:conv_rows.shape[0]].reshape(shard_w["conv_w"].shape) for s in range(4)],
                                  axis=3).reshape(depth, SSD_CONV, CONV_DIM)
    P = {n: shard_w[n] for n in SMALL}
    P["pre_mix_norm"] = P["pre_mix_norm"] + token[0, 0]
    assert depth == 2
    sent = {}

    def grads_ready(g):
        sent["sums"] = _chip_sums("l1_", _layer_pieces(g, shard_w))
        sent["sems"] = _send_start("scatter_l1_start", sent["sums"], sent["sums"][0], True, 1)
        return sent["sems"][4][0, 0]

    loss_part, grad_x, full = _local_step(x, target, W, P, grads_ready)
    send1, recv1, thru1, land1, _ = sent["sems"]
    landed1 = _send_wait("scatter_l1_wait", send1, recv1, thru1, land1, grad_x, True)
    sums0 = _chip_sums("l0_", _layer_pieces({n: full[n][0] for n in BIG + ("w_in",)}, shard_w))
    landed0 = _scatter_chips("scatter_grads", sums0)
    staged = [jnp.stack([jnp.where(me == s, own[s], got[s]) for s in range(4)])
              for own, got in zip(sums0 + sent["sums"], list(landed0) + landed1)]
    halves = [_sum_slots(f"sum_grads{i}", st) for i, st in enumerate(staged)]
    core = lax.axis_index("c")
    win0, big0, win1, big1 = [jnp.where(core == 0, jnp.concatenate([mine, other]), jnp.concatenate([other, mine]))
                              for mine, other in zip(halves, _swap_cores("share_grads", halves, slab=False))]
    g_win = jnp.concatenate([win0, win1], axis=0)
    per_layer = [_unpack_layer(big0, shard_w), _unpack_layer(big1, shard_w)]
    g_big = _pack_big({n: jnp.stack([per_layer[0][n], per_layer[1][n]]) for n in BIG}, F32)
    d_win, m_win, v_win = _adamw("adamw_w_in", win2d(shard_w["w_in"]), g_win, win2d(shard_m["w_in"]), win2d(shard_v["w_in"]))
    d_big, m_big, v_big = _adamw("adamw_big", _pack_big(shard_w, F32), g_big, _pack_big(shard_m, F32), _pack_big(shard_v, F32))

    small = _allgather_all("gather_small", _pack_small(full, loss_part))
    g_small = _sum_slots("sum_small", small, tb=SMALL_ROWS)
    d_small, m_small, v_small = _adamw("adamw_small", _pack_small(shard_w, jnp.zeros((), F32)), g_small,
                                       _pack_small(shard_m, jnp.zeros((), F32)), _pack_small(shard_v, jnp.zeros((), F32)))

    outs = {}
    for tag, win, big, sm in (("grad", g_win, g_big, g_small), ("delta", d_win, d_big, d_small),
                              ("new_m", m_win, m_big, m_small), ("new_v", v_win, v_big, v_small)):
        ub = _unpack_big(big, shard_w)
        us, extra = _unpack_small(sm, shard_w)
        outs[tag] = {"w_in": win.reshape(shard_w["w_in"].shape), **ub, **us}
        if tag == "grad":
            loss = extra
    return loss, grad_x, outs


def _local_step(x, target, W, P, grads_ready=None):
    depth = W["w_in"].shape[0]
    W = dict(W, w_in_main=W["w_in"][:, :, :N_MAIN], w_in_qkv=W["w_in"][:, :, N_MAIN:])
    onehot_t = (_bucket_map().reshape(1, -1) == jnp.arange(N_BUCKETS)[:, None]).astype(F32)
    bias = _bias_from_table(P["rel_bias_table"], onehot_t).reshape(ATTN_HEADS, BLOCK, 3 * BLOCK)
    bias4 = jnp.where(_band_mask(), bias, NEG).reshape(ATTN_KV, REP, BLOCK, 3 * BLOCK)

    def pre_fn(x, g):
        return [_rms_fwd(x, g)], []

    (h1,) = _rowwise("pre_mix_fwd", pre_fn, [_row(x)], [P["pre_mix_norm"][0].reshape(1, D_MODEL)], [(D_MODEL, BF16)])
    saved = []
    loss_cols = dxl = None
    for l in range(depth):
        S = _layer_fwd(h1, x, W, P, l, bias4)
        saved.append(S)
        g_pmlp = P["post_mlp_norm"][l].reshape(1, D_MODEL)
        if l + 1 < depth:
            def post_fn(x2, f2, g1, g2):
                x3 = x2 + _rms_fwd(f2, g1)
                return [x3, _rms_fwd(x3, g2)], []

            x, h1 = _rowwise("post_mlp_fwd", post_fn, [_row(S["x2"]), _row(S["f2"])],
                             [g_pmlp, P["pre_mix_norm"][l + 1].reshape(1, D_MODEL)], [(D_MODEL, F32), (D_MODEL, BF16)])
        else:
            def loss_fn(x2, f2, tgt, g1):
                diff = x2 + _rms_fwd(f2, g1) - tgt
                return [diff * (1.0 / D_MODEL)], [jnp.sum(diff * diff, axis=0, keepdims=True)]

            dxl, loss_cols = _rowwise("loss", loss_fn, [_row(S["x2"]), _row(S["f2"]), _row(target)], [g_pmlp],
                                      [(D_MODEL, F32)], [(1, D_MODEL)])
    loss_part = 0.5 * jnp.sum(loss_cols) / D_MODEL

    grads = [None] * depth
    dx = dxl
    zero = None
    for l in reversed(range(depth)):
        dx, grads[l] = _layer_bwd(saved[l], dx, W, P, l, bias4, onehot_t, zero)
        if grads_ready is not None and l == depth - 1 and depth > 1:
            zero = grads_ready(grads[l])
    grad_x = dx

    full = {n: jnp.stack([grads[l][n] for l in range(depth)]) for n in ALL_W if n != "rel_bias_table"}
    full["rel_bias_table"] = sum(grads[l]["rel_bias_table"] for l in range(depth))
    return loss_part, grad_x, full


def kernel(x, pre_mix_norm, w_in, b_gate, conv_w, conv_b, dt_bias, a_log, d_skip, ssd_norm, w_ssd_out, attn_sink, rel_bias_table, w_attn_out, w_o, post_mix_norm, pre_mlp_norm, w_mlp_in, w_mlp_out, post_mlp_norm, loss_target, m_pre_mix_norm, m_w_in, m_b_gate, m_conv_w, m_conv_b, m_dt_bias, m_a_log, m_d_skip, m_ssd_norm, m_w_ssd_out, m_attn_sink, m_rel_bias_table, m_w_attn_out, m_w_o, m_post_mix_norm, m_pre_mlp_norm, m_w_mlp_in, m_w_mlp_out, m_post_mlp_norm, v_pre_mix_norm, v_w_in, v_b_gate, v_conv_w, v_conv_b, v_dt_bias, v_a_log, v_d_skip, v_ssd_norm, v_w_ssd_out, v_attn_sink, v_rel_bias_table, v_w_attn_out, v_w_o, v_post_mix_norm, v_pre_mlp_norm, v_w_mlp_in, v_w_mlp_out, v_post_mlp_norm):
    a = locals()
    shard_w = {n: a[n] for n in ALL_W}
    shard_m = {n: a["m_" + n] for n in ALL_W}
    shard_v = {n: a["v_" + n] for n in ALL_W}
    loss, grad_x, outs = _step(x[0], loss_target[0], shard_w, shard_m, shard_v)
    return (loss, grad_x[None], *[outs["grad"][n] for n in ALL_W], *[outs["delta"][n] for n in ALL_W],
            *[outs["new_m"][n] for n in ALL_W], *[outs["new_v"][n] for n in ALL_W])
```

```python
import math

import jax
import jax.numpy as jnp
from jax import lax
from jax.experimental import pallas as pl
from jax.experimental.pallas import tpu as pltpu

F32 = jnp.float32
BF16 = jnp.bfloat16
MESH = pl.DeviceIdType.MESH

VMEM_LIMIT_BYTES = 52 * 1024 * 1024
LANES = 128
SUBLANES = 8

EPS = 1e-6
D_MODEL = 1024
D_INNER = 2048
SSD_HEADS = 32
SSD_HEAD_DIM = 64
SSD_GROUPS = 8
SSD_STATE = 128
SSD_CONV = 5
CHUNK = 128
CONV_DIM = 4096
ATTN_HEADS = 16
ATTN_KV = 4
ATTN_DIM = 64
BLOCK = 128
N_BUCKETS = 32
MAX_DISTANCE = 128
D_FF = 4096
N_IN = 9792
NEG = -1e30

OFF_Z, OFF_G, OFF_XBC, OFF_DT, N_MAIN, N_PROJ = 0, 2048, 4096, 8192, 8320, 10368
N_QKV = N_PROJ - N_MAIN

ADAM_LR, ADAM_B1, ADAM_B2, ADAM_EPS, ADAM_WD, ADAM_STEP = 0.001, 0.9, 0.999, 1e-08, 0.01, 10

BIG_ROWS = 6656
SMALL_ROWS = 32


def _cparams(*sem):
    return pltpu.CompilerParams(dimension_semantics=sem, vmem_limit_bytes=VMEM_LIMIT_BYTES)


def _dot(a, b, precision=None):
    return lax.dot_general(a, b, (((1,), (0,)), ((), ())), preferred_element_type=F32, precision=precision)


def _dot_nt(a, b, precision=None):
    return lax.dot_general(a, b, (((1,), (1,)), ((), ())), preferred_element_type=F32, precision=precision)


def _dot_tn(a, b):
    return lax.dot_general(a, b, (((0,), (0,)), ((), ())), preferred_element_type=F32)


def _pick(n, prefs):
    for p in prefs:
        if n % p == 0:
            return p
    return n


def _matmul(name, a, b, mode, out_dtype=F32, epilogue=None, extras=()):
    if mode == "nn":
        (m, k), (_, n) = a.shape, b.shape
    elif mode == "nt":
        (m, k), (n, _) = a.shape, b.shape
    else:
        (k, m), (_, n) = a.shape, b.shape
    tm = _pick(m, (512, 256, 128)) if mode == "tn" else _pick(m, (1024, 512, 256, 128))
    tn = _pick(n, (1024, 1152, 1664, 512, 256, 128))
    tk = _pick(k, (4096, 2048, 1024, 3456, 512, 256, 128)) if mode != "tn" else _pick(k, (4096, 512, 256, 128))
    if tk > 3456:
        tm = _pick(m, (512, 256, 128))
    nk = k // tk
    if mode == "nn":
        a_spec = pl.BlockSpec((tm, tk), lambda i, j, q: (i, q))
        b_spec = pl.BlockSpec((tk, tn), lambda i, j, q: (q, j))
        fn = _dot
    elif mode == "nt":
        a_spec = pl.BlockSpec((tm, tk), lambda i, j, q: (i, q))
        b_spec = pl.BlockSpec((tn, tk), lambda i, j, q: (j, q))
        fn = _dot_nt
    else:
        a_spec = pl.BlockSpec((tk, tm), lambda i, j, q: (q, i))
        b_spec = pl.BlockSpec((tk, tn), lambda i, j, q: (q, j))
        fn = _dot_tn

    tile = pl.BlockSpec((tm, tn), lambda i, j, q: (i, j))
    n_ex = len(extras)

    def body(a_ref, b_ref, *rest):
        ex_refs, o_ref = rest[:n_ex], rest[n_ex]

        def store(acc):
            v = acc if epilogue is None else epilogue(acc, *[r[...] for r in ex_refs])
            o_ref[...] = v.astype(o_ref.dtype)

        p = fn(a_ref[...].astype(BF16), b_ref[...].astype(BF16))
        if nk == 1:
            store(p)
        else:
            acc_ref = rest[n_ex + 1]
            q = pl.program_id(2)

            @pl.when(q == 0)
            def _():
                acc_ref[...] = p

            @pl.when((q > 0) & (q < nk - 1))
            def _():
                acc_ref[...] += p

            @pl.when(q == nk - 1)
            def _():
                store(acc_ref[...] + p)

    return pl.pallas_call(
        body, name=name, grid=(m // tm, n // tn, nk),
        in_specs=[a_spec, b_spec] + [tile] * n_ex, out_specs=tile,
        out_shape=jax.ShapeDtypeStruct((m, n), out_dtype),
        scratch_shapes=[pltpu.VMEM((tm, tn), F32)] if nk > 1 else [],
        compiler_params=_cparams("parallel", "parallel", "arbitrary"),
    )(a, b, *extras)


def _row(arr, width=None, cb=0, lead=None):
    return (arr, width, cb, lead)


def _rowwise(name, fn, rows, vecs, outs, accs=(), tb=256):
    t = rows[0][0].shape[-2]
    widest = max([arr.shape[-1] if width is None else width for arr, width, _, _ in rows] + [c for c, _ in outs])
    if widest <= D_MODEL and t % (2 * tb) == 0:
        tb = 2 * tb
    tb = min(tb, t)
    in_specs, args = [], []
    for arr, width, cb, lead in rows:
        w = arr.shape[-1] if width is None else width
        if lead is None:
            in_specs.append(pl.BlockSpec((tb, w), lambda i, cb=cb: (i, cb)))
        else:
            in_specs.append(pl.BlockSpec((None, tb, w), lambda i, cb=cb, lead=lead: (lead, i, cb)))
        args.append(arr)
    for v in vecs:
        in_specs.append(pl.BlockSpec(v.shape, lambda i, nd=v.ndim: (0,) * nd))
        args.append(v)
    out_shape = [jax.ShapeDtypeStruct((t, c), dt) for c, dt in outs] + [jax.ShapeDtypeStruct(s, F32) for s in accs]
    out_specs = [pl.BlockSpec((tb, c), lambda i: (i, 0)) for c, _ in outs] + [pl.BlockSpec(s, lambda i: (0, 0)) for s in accs]
    n_in, n_out = len(args), len(outs)

    def body(*refs):
        vals = [r[...] for r in refs[:n_in]]
        o_vals, a_vals = fn(*vals)
        for r, v in zip(refs[n_in:n_in + n_out], o_vals):
            r[...] = v.astype(r.dtype)
        first = pl.program_id(0) == 0
        for r, v in zip(refs[n_in + n_out:], a_vals):
            @pl.when(first)
            def _(r=r, v=v):
                r[...] = v

            @pl.when(jnp.logical_not(first))
            def _(r=r, v=v):
                r[...] += v

    res = pl.pallas_call(
        body, name=name, grid=(t // tb,), in_specs=in_specs, out_specs=out_specs, out_shape=out_shape,
        compiler_params=_cparams("arbitrary"),
    )(*args)
    return res


def _rms_fwd(x, g):
    r = lax.rsqrt(jnp.mean(x * x, axis=-1, keepdims=True) + EPS)
    return x * r * g


def _rms_bwd(x, g, dy):
    r = lax.rsqrt(jnp.mean(x * x, axis=-1, keepdims=True) + EPS)
    xh = x * r
    dxh = dy * g
    dx = r * (dxh - xh * jnp.mean(dxh * xh, axis=-1, keepdims=True))
    return dx, jnp.sum(dy * xh, axis=0, keepdims=True)


def _silu(x):
    return x * jax.nn.sigmoid(x)


def _silu_grad(x):
    s = jax.nn.sigmoid(x)
    return s * (1.0 + x * (1.0 - s))


GROUP_W = D_INNER // SSD_GROUPS


def _gated_norm_fwd(y, z, w):
    u = y * _silu(z)
    parts = []
    for j in range(SSD_GROUPS):
        ug = u[:, j * GROUP_W:(j + 1) * GROUP_W]
        parts.append(ug * lax.rsqrt(jnp.mean(ug * ug, axis=-1, keepdims=True) + EPS))
    return jnp.concatenate(parts, axis=-1) * w


def _gated_norm_bwd(y, z, w, dyn):
    sz = _silu(z)
    u = y * sz
    duh = dyn * w
    du_parts, uh_parts = [], []
    for j in range(SSD_GROUPS):
        sl = slice(j * GROUP_W, (j + 1) * GROUP_W)
        ug = u[:, sl]
        r = lax.rsqrt(jnp.mean(ug * ug, axis=-1, keepdims=True) + EPS)
        uh = ug * r
        dg = duh[:, sl]
        du_parts.append(r * (dg - uh * jnp.mean(dg * uh, axis=-1, keepdims=True)))
        uh_parts.append(uh)
    du = jnp.concatenate(du_parts, axis=-1)
    uh = jnp.concatenate(uh_parts, axis=-1)
    dw = jnp.sum(dyn * uh, axis=0, keepdims=True)
    return du * sz, du * y * _silu_grad(z), dw


HALO = SUBLANES


def _halo_specs(tb, cb, col0, t):
    nblk8 = t // HALO
    per = tb // HALO
    main = pl.BlockSpec((tb, cb), lambda j, i: (i, col0 + j))
    prev = pl.BlockSpec((HALO, cb), lambda j, i: (jnp.maximum(i * per - 1, 0), col0 + j))
    nxt = pl.BlockSpec((HALO, cb), lambda j, i: (jnp.minimum((i + 1) * per, nblk8 - 1), col0 + j))
    return main, prev, nxt


def _fill_ext(ext_ref, cur_ref, prev_ref, next_ref, tb, ni):
    i = pl.program_id(1)
    ext_ref[0:HALO, :] = jnp.where(i > 0, prev_ref[...], 0.0)
    ext_ref[HALO:HALO + tb, :] = cur_ref[...]
    ext_ref[HALO + tb:HALO + tb + HALO, :] = jnp.where(i < ni - 1, next_ref[...], 0.0)


def _conv_fwd(proj, w, b):
    t = proj.shape[0]
    tb, cb = min(1024, t), 512
    ni, nj = t // tb, CONV_DIM // cb
    main, prev, nxt = _halo_specs(tb, cb, OFF_XBC // cb, t)
    pad = (SSD_CONV - 1) // 2

    def body(u_ref, up_ref, un_ref, w_ref, b_ref, pre_ref, act_ref, ext_ref):
        _fill_ext(ext_ref, u_ref, up_ref, un_ref, tb, ni)
        acc = jnp.broadcast_to(b_ref[...], (tb, cb))
        for k in range(SSD_CONV):
            acc = acc + w_ref[k:k + 1, :] * ext_ref[pl.ds(HALO + k - pad, tb), :]
        pre_ref[...] = acc
        act_ref[...] = _silu(acc)

    out = pl.BlockSpec((tb, cb), lambda j, i: (i, j))
    return pl.pallas_call(
        body, name="conv_fwd", grid=(nj, ni),
        in_specs=[main, prev, nxt, pl.BlockSpec((SSD_CONV, cb), lambda j, i: (0, j)), pl.BlockSpec((1, cb), lambda j, i: (0, j))],
        out_specs=[out, out],
        out_shape=[jax.ShapeDtypeStruct((t, CONV_DIM), F32)] * 2,
        scratch_shapes=[pltpu.VMEM((tb + 2 * HALO, cb), F32)],
        compiler_params=_cparams("parallel", "arbitrary"),
    )(proj, proj, proj, w, b)


def _conv_bwd(dpre, proj, w):
    t = proj.shape[0]
    tb, cb = min(1024, t), 512
    ni, nj = t // tb, CONV_DIM // cb
    umain, uprev, unext = _halo_specs(tb, cb, OFF_XBC // cb, t)
    dmain, dprev, dnext = _halo_specs(tb, cb, 0, t)
    pad = (SSD_CONV - 1) // 2

    def body(d_ref, dp_ref, dn_ref, u_ref, up_ref, un_ref, w_ref, du_ref, dw_ref, db_ref, extd_ref, extu_ref):
        _fill_ext(extd_ref, d_ref, dp_ref, dn_ref, tb, ni)
        _fill_ext(extu_ref, u_ref, up_ref, un_ref, tb, ni)
        d = d_ref[...]
        du = jnp.zeros((tb, cb), F32)
        @pl.when(pl.program_id(1) == 0)
        def _():
            dw_ref[...] = jnp.zeros_like(dw_ref)
            db_ref[...] = jnp.zeros_like(db_ref)

        for k in range(SSD_CONV):
            du = du + w_ref[k:k + 1, :] * extd_ref[pl.ds(HALO - k + pad, tb), :]
            dw_ref[k:k + 1, :] += jnp.sum(d * extu_ref[pl.ds(HALO + k - pad, tb), :], axis=0, keepdims=True)
        du_ref[...] = du.astype(du_ref.dtype)
        db_ref[...] += jnp.sum(d, axis=0, keepdims=True)

    return pl.pallas_call(
        body, name="conv_bwd", grid=(nj, ni),
        in_specs=[dmain, dprev, dnext, umain, uprev, unext, pl.BlockSpec((SSD_CONV, cb), lambda j, i: (0, j))],
        out_specs=[pl.BlockSpec((tb, cb), lambda j, i: (i, j)), pl.BlockSpec((SSD_CONV, cb), lambda j, i: (0, j)),
                   pl.BlockSpec((1, cb), lambda j, i: (0, j))],
        out_shape=[jax.ShapeDtypeStruct((t, CONV_DIM), BF16), jax.ShapeDtypeStruct((SSD_CONV, CONV_DIM), F32),
                   jax.ShapeDtypeStruct((1, CONV_DIM), F32)],
        scratch_shapes=[pltpu.VMEM((tb + 2 * HALO, cb), F32)] * 2,
        compiler_params=_cparams("parallel", "arbitrary"),
    )(dpre, dpre, dpre, proj, proj, proj, w)


PAIR = 2 * SSD_HEAD_DIM
HI = lax.Precision.HIGHEST


def _ssd_prelude(d, dt_ref, dtt_ref, ar_ref, ac_ref):
    li = lax.broadcasted_iota(jnp.int32, (CHUNK, CHUNK), 0)
    si = lax.broadcasted_iota(jnp.int32, (CHUNK, CHUNK), 1)
    fwd = d == 0
    hi, lo = jnp.where(fwd, li, si), jnp.where(fwd, si, li)
    tri = hi >= lo
    trif = tri.astype(F32)
    trit = (hi <= lo).astype(F32)
    dt = dt_ref[...]
    adt = dt * ar_ref[...]
    adtt = dtt_ref[...] * ac_ref[...]
    p = _dot(trif, adt, HI)
    pt = _dot_nt(adtt, trif, HI)
    tot = jnp.sum(adt, axis=0, keepdims=True)
    return tri, trit, dt, p, pt, tot


def _ssd_specs(nc, rev):
    def cidx(d, c):
        up = (d == 1) if rev else (d == 0)
        return jnp.where(up, c, nc - 1 - c)

    specs = [
        pl.BlockSpec((CHUNK, D_INNER), lambda d, c: (cidx(d, c), 0)),
        pl.BlockSpec((CHUNK, 1024), lambda d, c: (cidx(d, c), 2)),
        pl.BlockSpec((CHUNK, 1024), lambda d, c: (cidx(d, c), 3)),
        pl.BlockSpec((None, CHUNK, SSD_HEADS), lambda d, c: (d, cidx(d, c), 0)),
        pl.BlockSpec((None, SSD_HEADS, CHUNK), lambda d, c: (d, 0, cidx(d, c))),
        pl.BlockSpec((None, 1, SSD_HEADS), lambda d, c: (d, 0, 0)),
        pl.BlockSpec((None, SSD_HEADS, 1), lambda d, c: (d, 0, 0)),
        pl.BlockSpec((CHUNK, D_INNER), lambda d, c: (cidx(d, c), d)),
    ]
    return cidx, specs


def _head_decay(tri, p, pt, tot, h):
    pb = jnp.broadcast_to(p[:, h:h + 1], (CHUNK, CHUNK))
    dec = jnp.exp(jnp.where(tri, pb - pt[h:h + 1, :], NEG))
    return dec, jnp.exp(tot[:, h:h + 1] - pb), jnp.exp(pb)


def _ssd_fwd(act, dt2, dt2t, a_row, a_col, dte):
    t = act.shape[0]
    nc = t // CHUNK
    cidx, specs = _ssd_specs(nc, rev=False)

    def body(xs_ref, bs_ref, cs_ref, dt_ref, dtt_ref, ar_ref, ac_ref, dte_ref, y_ref, st_ref, h_ref):
        d, c = pl.program_id(0), pl.program_id(1)

        @pl.when(c == 0)
        def _():
            h_ref[...] = jnp.zeros_like(h_ref)

        st_ref[...] = h_ref[...]
        tri, _, _, p, pt, tot = _ssd_prelude(d, dt_ref, dtt_ref, ar_ref, ac_ref)
        etot = jnp.exp(tot)
        lane = lax.broadcasted_iota(jnp.int32, (CHUNK, PAIR), 1) < SSD_HEAD_DIM
        rowh = lax.broadcasted_iota(jnp.int32, (PAIR, SSD_STATE), 0) < SSD_HEAD_DIM
        for g in range(SSD_GROUPS):
            gs = slice(g * SSD_STATE, (g + 1) * SSD_STATE)
            bg = bs_ref[:, gs]
            cb = cs_ref[:, gs].astype(BF16)
            cbm = _dot_nt(cb, bg.astype(BF16))
            for pr in range(2):
                h0 = g * 4 + pr * 2
                h1 = h0 + 1
                sl = slice(h0 * SSD_HEAD_DIM, h0 * SSD_HEAD_DIM + PAIR)
                xdt = (xs_ref[:, sl] * dte_ref[:, sl]).astype(BF16)
                yd, st, epb = [], [], []
                for h in (h0, h1):
                    dec, wb, eb = _head_decay(tri, p, pt, tot, h)
                    yd.append(_dot((cbm * dec).astype(BF16), xdt))
                    st.append(_dot_tn(xdt, (bg * wb).astype(BF16)))
                    epb.append(eb)
                hin = h_ref[sl, :]
                yo = _dot_nt(cb, hin.astype(BF16)) * jnp.where(lane, epb[0], epb[1])
                y_ref[:, sl] = jnp.where(lane, yd[0], yd[1]) + yo
                et = jnp.where(rowh, etot[:, h0:h0 + 1], etot[:, h1:h1 + 1])
                h_ref[sl, :] = hin * et + jnp.where(rowh, st[0], st[1])

    return pl.pallas_call(
        body, name="ssd_fwd", grid=(2, nc), in_specs=specs,
        out_specs=[pl.BlockSpec((None, CHUNK, D_INNER), lambda d, c: (d, cidx(d, c), 0)),
                   pl.BlockSpec((None, None, D_INNER, SSD_STATE), lambda d, c: (d, cidx(d, c), 0, 0))],
        out_shape=[jax.ShapeDtypeStruct((2, t, D_INNER), F32), jax.ShapeDtypeStruct((2, nc, D_INNER, SSD_STATE), F32)],
        scratch_shapes=[pltpu.VMEM((D_INNER, SSD_STATE), F32)],
        compiler_params=_cparams("arbitrary", "arbitrary"),
    )(act, act, act, dt2, dt2t, a_row, a_col, dte)


def _put_col(acc, col, h):
    lane = lax.broadcasted_iota(jnp.int32, acc.shape, 1)
    return jnp.where(lane == h, col, acc)


def _put_row(acc, row, h):
    sub = lax.broadcasted_iota(jnp.int32, acc.shape, 0)
    return jnp.where(sub == h, row, acc)


def _sum_all(x):
    return jnp.sum(jnp.sum(x, axis=0, keepdims=True), axis=1, keepdims=True)


def _ssd_bwd(act, dt2, dt2t, a_row, a_col, dte, dy, states):
    t = act.shape[0]
    nc = t // CHUNK
    cidx, specs = _ssd_specs(nc, rev=True)
    specs = specs + [
        pl.BlockSpec((CHUNK, D_INNER), lambda d, c: (cidx(d, c), 0)),
        pl.BlockSpec((None, None, D_INNER, SSD_STATE), lambda d, c: (d, cidx(d, c), 0, 0)),
    ]

    def body(xs_ref, bs_ref, cs_ref, dt_ref, dtt_ref, ar_ref, ac_ref, dte_ref, dy_ref, st_ref,
             dxs_ref, dbs_ref, dcs_ref, ddt_ref, da_ref, dh_ref):
        d, c = pl.program_id(0), pl.program_id(1)

        @pl.when(c == 0)
        def _():
            dh_ref[...] = jnp.zeros_like(dh_ref)
            da_ref[...] = jnp.zeros_like(da_ref)

        tri, trit, dt, p, pt, tot = _ssd_prelude(d, dt_ref, dtt_ref, ar_ref, ac_ref)
        etot = jnp.exp(tot)
        lane = lax.broadcasted_iota(jnp.int32, (CHUNK, PAIR), 1) < SSD_HEAD_DIM
        rowh = lax.broadcasted_iota(jnp.int32, (PAIR, SSD_STATE), 0) < SSD_HEAD_DIM
        first_head = lax.broadcasted_iota(jnp.int32, (PAIR, LANES), 0) < SSD_HEAD_DIM
        out_lane = lax.broadcasted_iota(jnp.int32, (PAIR, LANES), 1)
        ddtx = jnp.zeros((CHUNK, LANES), F32)
        lane32 = lax.broadcasted_iota(jnp.int32, (CHUNK, SSD_HEADS), 1)
        dp_col = jnp.zeros((CHUNK, SSD_HEADS), F32)
        dp_row = jnp.zeros((SSD_HEADS, CHUNK), F32)
        dtot = jnp.zeros((1, SSD_HEADS), F32)
        for g in range(SSD_GROUPS):
            gs = slice(g * SSD_STATE, (g + 1) * SSD_STATE)
            bg = bs_ref[:, gs]
            bb = bg.astype(BF16)
            cb = cs_ref[:, gs].astype(BF16)
            cbm = _dot_nt(cb, bb)
            dcb = jnp.zeros((CHUNK, CHUNK), F32)
            dc_acc = jnp.zeros((CHUNK, SSD_STATE), F32)
            db_acc = jnp.zeros((CHUNK, SSD_STATE), F32)
            for pr in range(2):
                h0 = g * 4 + pr * 2
                h1 = h0 + 1
                sl = slice(h0 * SSD_HEAD_DIM, h0 * SSD_HEAD_DIM + PAIR)
                xp = xs_ref[:, sl]
                dtp = dte_ref[:, sl]
                xdt_f = xp * dtp
                xdt = xdt_f.astype(BF16)
                dyp = dy_ref[:, sl]
                dyb = dyp.astype(BF16)
                hin = st_ref[sl, :]
                dh = dh_ref[sl, :]
                hb = hin.astype(BF16)
                dhb = dh.astype(BF16)
                heads = [_head_decay(tri, p, pt, tot, h) for h in (h0, h1)]
                dye = dyp * jnp.where(lane, heads[0][2], heads[1][2])
                dyeb = dye.astype(BF16)
                gy = _dot_nt(cb, hb) * dye
                dc_acc = dc_acc + _dot(dyeb, hb)
                dhin = _dot_tn(dyeb, cb)
                hh = dh * hin
                dxdt = jnp.zeros((CHUNK, PAIR), F32)
                for idx, h in enumerate((h0, h1)):
                    hm = lane if idx == 0 else jnp.logical_not(lane)
                    rm = rowh if idx == 0 else jnp.logical_not(rowh)
                    dec, wb, _ = heads[idx]
                    mf = cbm * dec
                    t1 = _dot_tn(mf.astype(BF16), dyb)
                    t2 = _dot_nt((bg * wb).astype(BF16), dhb)
                    dxdt = jnp.where(hm, t1 + t2, dxdt)
                    dm = _dot_nt(jnp.where(hm, dyp, 0.0).astype(BF16), xdt)
                    dcb = dcb + dm * dec
                    e = dm * mf
                    qw = _dot(jnp.where(hm, xdt_f, 0.0).astype(BF16), dhb) * wb
                    db_acc = db_acc + qw
                    qwb = qw * bg
                    col = jnp.sum(e + jnp.where(hm, gy, 0.0) - qwb, axis=1, keepdims=True)
                    dp_col = jnp.where(lane32 == h, col, dp_col)
                    dp_row = _put_row(dp_row, -jnp.sum(e, axis=0, keepdims=True), h)
                    dtot_h = _sum_all(qwb) + etot[:, h:h + 1] * _sum_all(jnp.where(rm, hh, 0.0))
                    dtot = _put_col(dtot, dtot_h, h)
                dxs_ref[:, sl] = dxdt * dtp
                ddx = dxdt * xp
                ddx_hi = ddx.astype(BF16)
                ddx_lo = (ddx - ddx_hi.astype(F32)).astype(BF16)
                route = (out_lane == jnp.where(first_head, h0, h1)).astype(BF16)
                ddtx = ddtx + _dot(ddx_hi, route) + _dot(ddx_lo, route)
                et = jnp.where(rowh, etot[:, h0:h0 + 1], etot[:, h1:h1 + 1])
                dh_ref[sl, :] = dh * et + dhin
            dcbb = dcb.astype(BF16)
            dcs_ref[:, gs] = _dot(dcbb, bb) + dc_acc
            dbs_ref[:, gs] = _dot_tn(dcbb, cb) + db_acc
        d_adt = _dot(trit, dp_col, HI) + _dot_nt(trit, dp_row, HI) + dtot
        ddt_ref[...] = ddtx[:, :SSD_HEADS] + ar_ref[...] * d_adt
        da_ref[...] += jnp.sum(dt * d_adt, axis=0, keepdims=True)

    return pl.pallas_call(
        body, name="ssd_bwd", grid=(2, nc), in_specs=specs,
        out_specs=[pl.BlockSpec((None, CHUNK, D_INNER), lambda d, c: (d, cidx(d, c), 0)),
                   pl.BlockSpec((None, CHUNK, 1024), lambda d, c: (d, cidx(d, c), 0)),
                   pl.BlockSpec((None, CHUNK, 1024), lambda d, c: (d, cidx(d, c), 0)),
                   pl.BlockSpec((None, CHUNK, SSD_HEADS), lambda d, c: (d, cidx(d, c), 0)),
                   pl.BlockSpec((None, 1, SSD_HEADS), lambda d, c: (d, 0, 0))],
        out_shape=[jax.ShapeDtypeStruct((2, t, D_INNER), F32), jax.ShapeDtypeStruct((2, t, 1024), F32),
                   jax.ShapeDtypeStruct((2, t, 1024), F32), jax.ShapeDtypeStruct((2, t, SSD_HEADS), F32),
                   jax.ShapeDtypeStruct((2, 1, SSD_HEADS), F32)],
        scratch_shapes=[pltpu.VMEM((D_INNER, SSD_STATE), F32)],
        compiler_params=_cparams("arbitrary", "arbitrary"),
    )(act, act, act, dt2, dt2t, a_row, a_col, dte, dy, states)


REP = ATTN_HEADS // ATTN_KV
SCALE = ATTN_DIM ** -0.5
GROUP_Q = REP * ATTN_DIM
K_BLK0 = D_MODEL // LANES
V_BLK0 = K_BLK0 + ATTN_KV


def _attn_specs(nb):
    q = pl.BlockSpec((BLOCK, GROUP_Q), lambda g, n: (n, g))

    def kv(blk0):
        return [pl.BlockSpec((BLOCK, LANES), lambda g, n: (jnp.maximum(n - 1, 0), blk0 + g)),
                pl.BlockSpec((BLOCK, LANES), lambda g, n: (n, blk0 + g)),
                pl.BlockSpec((BLOCK, LANES), lambda g, n: (jnp.minimum(n + 1, nb - 1), blk0 + g))]

    bias = pl.BlockSpec((None, REP, BLOCK, 3 * BLOCK), lambda g, n: (g, 0, 0, 0))
    sink = pl.BlockSpec((None, REP, 1, LANES), lambda g, n: (g, 0, 0, 0))
    return q, kv(K_BLK0), kv(V_BLK0), bias, sink


def _band_mask():
    ii = lax.broadcasted_iota(jnp.int32, (BLOCK, 3 * BLOCK), 0)
    jj = lax.broadcasted_iota(jnp.int32, (BLOCK, 3 * BLOCK), 1)
    return (jj >= ii) & (jj - 2 * BLOCK <= ii)


def _attn_valid(n, nb):
    jj = lax.broadcasted_iota(jnp.int32, (1, 3 * BLOCK), 1)
    return ((jj >= BLOCK) | (n > 0)) & ((jj < 2 * BLOCK) | (n < nb - 1))


def _attn_probs(q, kcat, bias, snk, valid):
    s = jnp.where(valid, _dot_nt(q, kcat) + bias, NEG)
    m = jnp.maximum(jnp.max(s, axis=1, keepdims=True), snk)
    p = jnp.exp(s - m)
    es = jnp.exp(snk - m)
    r = 1.0 / (jnp.sum(p, axis=1, keepdims=True) + es)
    return p * r, es * r


def _stack_heads(ref, lane):
    parts = []
    for pr in range(REP // 2):
        tile = ref[:, pr * LANES:(pr + 1) * LANES]
        parts += [jnp.where(lane, tile, 0.0), jnp.where(lane, 0.0, tile)]
    return jnp.concatenate(parts, axis=0)


def _unstack_heads(x, lane):
    return jnp.concatenate([jnp.where(lane, x[(2 * pr) * BLOCK:(2 * pr + 1) * BLOCK], x[(2 * pr + 1) * BLOCK:(2 * pr + 2) * BLOCK])
                            for pr in range(REP // 2)], axis=1)


def _stack_bias(b_ref, s_ref):
    bias = jnp.concatenate([b_ref[r] for r in range(REP)], axis=0)
    snk = jnp.concatenate([jnp.broadcast_to(s_ref[r][:, 0:1], (BLOCK, 1)) for r in range(REP)], axis=0)
    return bias, snk


def _attn_fwd(qkv, bias4, sink4):
    t = qkv.shape[0]
    nb = t // BLOCK
    qs, ks, vs, bs, ss = _attn_specs(nb)

    def body(q_ref, kp_ref, kc_ref, kn_ref, vp_ref, vc_ref, vn_ref, b_ref, s_ref, o_ref):
        n = pl.program_id(1)
        kcat = jnp.concatenate([kp_ref[...], kc_ref[...], kn_ref[...]], axis=0)
        vcat = jnp.concatenate([vp_ref[...], vc_ref[...], vn_ref[...]], axis=0)
        valid = _attn_valid(n, nb)
        lane = lax.broadcasted_iota(jnp.int32, (BLOCK, LANES), 1) < ATTN_DIM
        bias, snk = _stack_bias(b_ref, s_ref)
        pn, _ = _attn_probs(_stack_heads(q_ref, lane) * SCALE, kcat, bias, snk, valid)
        o_ref[...] = _unstack_heads(_dot(pn.astype(BF16), vcat), lane).astype(o_ref.dtype)

    return pl.pallas_call(
        body, name="attn_fwd", grid=(ATTN_KV, nb), in_specs=[qs] + ks + vs + [bs, ss],
        out_specs=qs, out_shape=jax.ShapeDtypeStruct((t, D_MODEL), BF16),
        compiler_params=_cparams("parallel", "arbitrary"),
    )(qkv, qkv, qkv, qkv, qkv, qkv, qkv, bias4, sink4)


def _attn_bwd(qkv, bias4, sink4, do):
    t = qkv.shape[0]
    nb = t // BLOCK
    qs, ks, vs, bs, ss = _attn_specs(nb)
    part = pl.BlockSpec((3, BLOCK, LANES), lambda g, n: (0, n, g))

    def body(q_ref, kp_ref, kc_ref, kn_ref, vp_ref, vc_ref, vn_ref, b_ref, s_ref, do_ref,
             dq_ref, dk_ref, dv_ref, db_ref, ds_ref):
        n = pl.program_id(1)

        @pl.when(n == 0)
        def _():
            db_ref[...] = jnp.zeros_like(db_ref)
            ds_ref[...] = jnp.zeros_like(ds_ref)

        kcat = jnp.concatenate([kp_ref[...], kc_ref[...], kn_ref[...]], axis=0)
        vcat = jnp.concatenate([vp_ref[...], vc_ref[...], vn_ref[...]], axis=0)
        valid = _attn_valid(n, nb)
        lane = lax.broadcasted_iota(jnp.int32, (BLOCK, LANES), 1) < ATTN_DIM
        bias, snk = _stack_bias(b_ref, s_ref)
        q = _stack_heads(q_ref, lane)
        do = _stack_heads(do_ref, lane)
        pn, psink = _attn_probs(q * SCALE, kcat, bias, snk, valid)
        dp = _dot_nt(do, vcat)
        delta = jnp.sum(pn * dp, axis=1, keepdims=True)
        dsc = pn * (dp - delta)
        dsink = psink * delta
        for r in range(REP):
            rows = slice(r * BLOCK, (r + 1) * BLOCK)
            db_ref[r] += dsc[rows]
            ds_ref[r] += jnp.broadcast_to(-jnp.sum(dsink[rows], axis=0, keepdims=True), (1, LANES))
        dsb = (dsc * SCALE).astype(BF16)
        dq_ref[...] = _unstack_heads(_dot(dsb, kcat), lane).astype(dq_ref.dtype)
        dk = _dot_tn(dsb, q)
        dv = _dot_tn(pn.astype(BF16), do)
        for j in range(3):
            dk_ref[j] = dk[j * BLOCK:(j + 1) * BLOCK]
            dv_ref[j] = dv[j * BLOCK:(j + 1) * BLOCK]

    kv_cols = ATTN_KV * LANES
    return pl.pallas_call(
        body, name="attn_bwd", grid=(ATTN_KV, nb), in_specs=[qs] + ks + vs + [bs, ss, qs],
        out_specs=[qs, part, part, bs, ss],
        out_shape=[jax.ShapeDtypeStruct((t, D_MODEL), BF16), jax.ShapeDtypeStruct((3, t, kv_cols), F32),
                   jax.ShapeDtypeStruct((3, t, kv_cols), F32), jax.ShapeDtypeStruct(bias4.shape, F32),
                   jax.ShapeDtypeStruct(sink4.shape, F32)],
        compiler_params=_cparams("parallel", "arbitrary"),
    )(qkv, qkv, qkv, qkv, qkv, qkv, qkv, bias4, sink4, do)


def _kv_combine(name, parts):
    _, t, cols = parts.shape

    def body(p_ref, o_ref):
        z = jnp.zeros((BLOCK, LANES), F32)
        from_next = jnp.concatenate([p_ref[0, BLOCK:, :], z], axis=0)
        from_prev = jnp.concatenate([z, p_ref[2, :t - BLOCK, :]], axis=0)
        o_ref[...] = (from_next + p_ref[1] + from_prev).astype(o_ref.dtype)

    return pl.pallas_call(
        body, name=name, grid=(cols // LANES,),
        in_specs=[pl.BlockSpec((3, t, LANES), lambda g: (0, 0, g))],
        out_specs=pl.BlockSpec((t, LANES), lambda g: (0, g)),
        out_shape=jax.ShapeDtypeStruct((t, cols), BF16),
        compiler_params=_cparams("parallel"),
    )(parts)


def _t5_bucket(rel):
    nb = N_BUCKETS // 2
    max_exact = nb // 2
    ret = jnp.where(rel > 0, nb, 0)
    n = jnp.abs(rel)
    nf = jnp.maximum(n, 1).astype(jnp.float32)
    large = max_exact + (jnp.log(nf / max_exact) / math.log(MAX_DISTANCE / max_exact) * (nb - max_exact)).astype(jnp.int32)
    large = jnp.minimum(large, nb - 1)
    return ret + jnp.where(n < max_exact, n, large)


def _bucket_map():
    i = jnp.arange(BLOCK)[:, None]
    j = jnp.arange(3 * BLOCK)[None, :]
    return _t5_bucket(j - BLOCK - i)


def _bias_from_table(table, onehot_t):
    def body(t_ref, o_ref, out_ref):
        out_ref[...] = _dot(t_ref[...], o_ref[...], HI)

    return pl.pallas_call(body, name="bias_from_table", out_shape=jax.ShapeDtypeStruct((ATTN_HEADS, onehot_t.shape[1]), F32),
                          compiler_params=pltpu.CompilerParams(vmem_limit_bytes=VMEM_LIMIT_BYTES))(table.T, onehot_t)


def _bias_table_grad(dbias, onehot_t):
    def body(d_ref, o_ref, out_ref):
        out_ref[...] = _dot_nt(d_ref[...], o_ref[...], HI)

    return pl.pallas_call(body, name="bias_table_grad", out_shape=jax.ShapeDtypeStruct((ATTN_HEADS, N_BUCKETS), F32),
                          compiler_params=pltpu.CompilerParams(vmem_limit_bytes=VMEM_LIMIT_BYTES))(dbias, onehot_t)


HBM_SPEC = pl.BlockSpec(memory_space=pl.ANY)


def _comm_call(name, body, xs, out_shapes, n_sems):
    n = len(xs)
    return pl.pallas_call(
        body, name=name, in_specs=[HBM_SPEC] * n, out_specs=[HBM_SPEC] * n, out_shape=out_shapes,
        scratch_shapes=[pltpu.SemaphoreType.DMA((n * n_sems,)), pltpu.SemaphoreType.DMA((n * n_sems,))],
    )(*xs)


def _allgather_chips(name, xs):
    n = len(xs)

    def body(*refs):
        x_refs, out_refs, (send_sems, recv_sems) = refs[:n], refs[n:2 * n], refs[2 * n:]
        mx, my, mc = lax.axis_index("x"), lax.axis_index("y"), lax.axis_index("c")
        me = 2 * mx + my
        chips = [(1 - mx, my), (mx, 1 - my), (1 - mx, 1 - my)]
        sibling = (mx, my, 1 - mc)

        def part(i, slot, h):
            r2 = xs[i].shape[0] // 2
            return out_refs[i].at[slot, pl.ds(h * r2, r2)]

        def copy(i, k, src, dst, to):
            return pltpu.make_async_remote_copy(src_ref=src, dst_ref=dst, send_sem=send_sems.at[6 * i + k],
                                                recv_sem=recv_sems.at[6 * i + k], device_id=to, device_id_type=MESH)

        first = [copy(i, k, x_refs[i].at[pl.ds(mc * (xs[i].shape[0] // 2), xs[i].shape[0] // 2)], part(i, me, mc), (px, py, mc))
                 for i in range(n) for k, (px, py) in enumerate(chips)]
        for cp in first:
            cp.start()
        passed = []
        for k, (px, py) in enumerate(chips):
            for i in range(n):
                landed = part(i, 2 * px + py, mc)
                copy(i, k, landed, landed, (px, py, mc)).wait_recv()
                passed.append(copy(i, 3 + k, landed, landed, sibling))
                passed[-1].start()
        for k, (px, py) in enumerate(chips):
            for i in range(n):
                theirs = part(i, 2 * px + py, 1 - mc)
                copy(i, 3 + k, theirs, theirs, sibling).wait_recv()
        for cp in first + passed:
            cp.wait_send()

    return _comm_call(name, body, xs, [jax.ShapeDtypeStruct((4,) + x.shape, x.dtype) for x in xs], 6)


SEM_SPEC = pl.BlockSpec(memory_space=pltpu.SEMAPHORE)
DATAFLOW = pltpu.SideEffectType.DATAFLOW_SIDE_EFFECTING


def _send_start(name, xs, after, scatter, collective_id):
    n = len(xs)

    def body(*refs):
        x_refs, land_refs = refs[:n], refs[n:2 * n]
        send_sems, recv_sems, token = refs[2 * n + 1], refs[2 * n + 2], refs[-1]
        mx, my, mc = lax.axis_index("x"), lax.axis_index("y"), lax.axis_index("c")
        chips = [(1 - mx, my), (mx, 1 - my), (1 - mx, 1 - my)]
        barrier = pltpu.get_barrier_semaphore()
        for px, py in chips:
            pl.semaphore_signal(barrier, inc=1, device_id=(px, py, mc), device_id_type=MESH)
        pl.semaphore_wait(barrier, len(chips))
        for i in range(n):
            for k, (px, py) in enumerate(chips):
                src = x_refs[i].at[2 * px + py] if scatter else x_refs[i]
                pltpu.make_async_remote_copy(src_ref=src, dst_ref=land_refs[i].at[2 * mx + my], send_sem=send_sems.at[3 * i + k],
                                             recv_sem=recv_sems.at[3 * i + k], device_id=(px, py, mc), device_id_type=MESH).start()
        token[...] = jnp.zeros_like(token)

    lands = [lax.empty((4,) + (x.shape[1:] if scatter else x.shape), x.dtype) for x in xs]
    hbm = [pltpu.HBM(a.shape, a.dtype) for a in list(xs) + lands]
    out = pl.pallas_call(
        body, name=name,
        out_shape=(pltpu.SemaphoreType.DMA((3 * n,)), pltpu.SemaphoreType.DMA((3 * n,)), *hbm, jax.ShapeDtypeStruct((SUBLANES, LANES), F32)),
        in_specs=(HBM_SPEC,) * (2 * n + 1),
        out_specs=(SEM_SPEC, SEM_SPEC) + (HBM_SPEC,) * (2 * n) + (pl.BlockSpec(memory_space=pltpu.VMEM),),
        input_output_aliases={i: 2 + i for i in range(2 * n)},
        compiler_params=pltpu.CompilerParams(has_side_effects=DATAFLOW, collective_id=collective_id),
    )(*[pltpu.with_memory_space_constraint(a, pltpu.HBM) for a in list(xs) + lands], after)
    return out[0], out[1], list(out[2:2 + n]), list(out[2 + n:2 + 2 * n]), out[-1]


def _send_wait(name, send_sems, recv_sems, x_thrus, land_thrus, after, scatter):
    n = len(x_thrus)

    def body(*refs):
        x_refs, land_refs, send_sems, recv_sems = refs[:n], refs[n:2 * n], refs[2 * n], refs[2 * n + 1]
        mx, my, mc = lax.axis_index("x"), lax.axis_index("y"), lax.axis_index("c")
        chips = [(1 - mx, my), (mx, 1 - my), (1 - mx, 1 - my)]
        for i in range(n):
            for k, (px, py) in enumerate(chips):
                src = x_refs[i].at[0] if scatter else x_refs[i]
                copy = pltpu.make_async_remote_copy(src_ref=src, dst_ref=land_refs[i].at[2 * px + py], send_sem=send_sems.at[3 * i + k],
                                                    recv_sem=recv_sems.at[3 * i + k], device_id=(px, py, mc), device_id_type=MESH)
                copy.wait_send()
                copy.wait_recv()

    arrs = list(x_thrus) + list(land_thrus)
    out = pl.pallas_call(
        body, name=name, out_shape=tuple(pltpu.HBM(a.shape, a.dtype) for a in arrs),
        in_specs=(HBM_SPEC,) * (2 * n) + (SEM_SPEC, SEM_SPEC, HBM_SPEC), out_specs=(HBM_SPEC,) * (2 * n),
        input_output_aliases={i: i for i in range(2 * n)},
        compiler_params=pltpu.CompilerParams(has_side_effects=DATAFLOW),
    )(*arrs, send_sems, recv_sems, after)
    return list(out[n:])


def _swap_cores(name, xs, slab=True):
    n = len(xs)

    def body(*refs):
        x_refs, out_refs, (send_sems, recv_sems) = refs[:n], refs[n:2 * n], refs[2 * n:]
        mx, my, mc = lax.axis_index("x"), lax.axis_index("y"), lax.axis_index("c")
        sends = [pltpu.make_async_remote_copy(src_ref=x_refs[i].at[1 - mc] if slab else x_refs[i], dst_ref=out_refs[i],
                                              send_sem=send_sems.at[i], recv_sem=recv_sems.at[i],
                                              device_id=(mx, my, 1 - mc), device_id_type=MESH)
                 for i in range(n)]
        for cp in sends:
            cp.start()
        for cp in sends:
            cp.wait()

    return _comm_call(name, body, xs, [jax.ShapeDtypeStruct(x.shape[1:] if slab else x.shape, x.dtype) for x in xs], 1)


def _scatter_chips(name, gs):
    n = len(gs)

    def body(*refs):
        g_refs, out_refs, (send_sems, recv_sems) = refs[:n], refs[n:2 * n], refs[2 * n:]
        mx, my, mc = lax.axis_index("x"), lax.axis_index("y"), lax.axis_index("c")
        me = 2 * mx + my
        chips = [(1 - mx, my), (mx, 1 - my), (1 - mx, 1 - my)]

        def copy(i, k, src, dst, to):
            return pltpu.make_async_remote_copy(src_ref=src, dst_ref=dst, send_sem=send_sems.at[3 * i + k],
                                                recv_sem=recv_sems.at[3 * i + k], device_id=to, device_id_type=MESH)

        sends = [copy(i, k, g_refs[i].at[2 * px + py], out_refs[i].at[me], (px, py, mc))
                 for i in range(n) for k, (px, py) in enumerate(chips)]
        for cp in sends:
            cp.start()
        for i in range(n):
            for k, (px, py) in enumerate(chips):
                copy(i, k, g_refs[i].at[0], out_refs[i].at[2 * px + py], (px, py, mc)).wait_recv()
        for cp in sends:
            cp.wait_send()

    return _comm_call(name, body, gs, [jax.ShapeDtypeStruct(g.shape, g.dtype) for g in gs], 3)


def _allgather_all(name, x):
    def body(x_ref, out_ref, send_sems, recv_sems, local_sem):
        mx, my, mc = lax.axis_index("x"), lax.axis_index("y"), lax.axis_index("c")
        me = 4 * mx + 2 * my + mc
        flips = [(fx, fy, fc) for fx in (0, 1) for fy in (0, 1) for fc in (0, 1)][1:]
        peers = [(mx ^ fx, my ^ fy, mc ^ fc) for fx, fy, fc in flips]
        mine = pltpu.make_async_copy(x_ref, out_ref.at[me], local_sem)
        mine.start()
        sends = [pltpu.make_async_remote_copy(src_ref=x_ref, dst_ref=out_ref.at[me], send_sem=send_sems.at[k],
                                              recv_sem=recv_sems.at[k], device_id=peer, device_id_type=MESH)
                 for k, peer in enumerate(peers)]
        for cp in sends:
            cp.start()
        for k, (px, py, pc) in enumerate(peers):
            pltpu.make_async_remote_copy(src_ref=x_ref, dst_ref=out_ref.at[4 * px + 2 * py + pc], send_sem=send_sems.at[k],
                                         recv_sem=recv_sems.at[k], device_id=(px, py, pc), device_id_type=MESH).wait_recv()
        for cp in sends:
            cp.wait_send()
        mine.wait()

    return pl.pallas_call(
        body, name=name, in_specs=[HBM_SPEC], out_specs=HBM_SPEC,
        out_shape=jax.ShapeDtypeStruct((8,) + x.shape, x.dtype),
        scratch_shapes=[pltpu.SemaphoreType.DMA((7,)), pltpu.SemaphoreType.DMA((7,)), pltpu.SemaphoreType.DMA],
    )(x)


def _sum_slots(name, st, tb=256):
    n, r, c = st.shape
    tb = _pick(r, (tb, 128, 32))

    def body(s_ref, o_ref):
        acc = s_ref[0].astype(F32)
        for k in range(1, n):
            acc = acc + s_ref[k].astype(F32)
        o_ref[...] = acc

    return pl.pallas_call(
        body, name=name, grid=(r // tb,), in_specs=[pl.BlockSpec((n, tb, c), lambda i: (0, i, 0))],
        out_specs=pl.BlockSpec((tb, c), lambda i: (i, 0)), out_shape=jax.ShapeDtypeStruct((r, c), F32),
        compiler_params=_cparams("parallel"),
    )(st)


def _adamw(name, w, g, m, v):
    def fn(w, g, m, v):
        m2 = ADAM_B1 * m + (1.0 - ADAM_B1) * g
        v2 = ADAM_B2 * v + (1.0 - ADAM_B2) * jnp.square(g)
        m_hat = m2 / (1.0 - ADAM_B1 ** ADAM_STEP)
        v_hat = v2 / (1.0 - ADAM_B2 ** ADAM_STEP)
        delta = -ADAM_LR * (m_hat / (jnp.sqrt(v_hat) + ADAM_EPS) + ADAM_WD * w)
        return [delta, m2, v2], []

    tb = _pick(w.shape[0], (256, 32))
    return _rowwise(name, fn, [_row(w), _row(g), _row(m), _row(v)], [], [(w.shape[1], F32)] * 3, tb=tb)


BIG = ("w_ssd_out", "w_attn_out", "w_o", "w_mlp_in", "w_mlp_out", "conv_w")
SMALL = ("pre_mix_norm", "b_gate", "conv_b", "dt_bias", "a_log", "d_skip", "ssd_norm", "attn_sink", "rel_bias_table",
         "post_mix_norm", "pre_mlp_norm", "post_mlp_norm")
ALL_W = ("pre_mix_norm", "w_in", "b_gate", "conv_w", "conv_b", "dt_bias", "a_log", "d_skip", "ssd_norm", "w_ssd_out",
         "attn_sink", "rel_bias_table", "w_attn_out", "w_o", "post_mix_norm", "pre_mlp_norm", "w_mlp_in", "w_mlp_out",
         "post_mlp_norm")


def _pack_rows(parts, rows, dtype):
    flat = jnp.concatenate([p.reshape(-1, D_MODEL).astype(dtype) for p in parts], axis=0)
    return jnp.pad(flat, ((0, rows - flat.shape[0]), (0, 0)))


def _pack_big(shards, dtype):
    return _pack_rows([shards[n] for n in BIG], BIG_ROWS, dtype)


def _unpack_big(flat, like):
    out, r = {}, 0
    for n in BIG:
        shp = like[n].shape
        nr = math.prod(shp) // D_MODEL
        out[n] = flat[r:r + nr].reshape(shp)
        r += nr
    return out


def _pack_small(parts, extra=None):
    flat = jnp.concatenate([parts[n].reshape(-1).astype(F32) for n in SMALL] + ([extra.reshape(-1)] if extra is not None else []))
    return jnp.pad(flat, (0, SMALL_ROWS * D_MODEL - flat.shape[0])).reshape(SMALL_ROWS, D_MODEL)


def _unpack_small(flat2, like):
    flat = flat2.reshape(-1)
    out, r = {}, 0
    for n in SMALL:
        shp = like[n].shape
        k = math.prod(shp)
        out[n] = flat[r:r + k].reshape(shp)
        r += k
    return out, flat[r]


def _shard_of_full(name, full, s):
    if name in ("w_in", "w_mlp_in"):
        w = full.shape[2] // 4
        return full[:, :, s * w:(s + 1) * w]
    if name == "conv_w":
        w = full.shape[3] // 4
        return full[:, :, :, s * w:(s + 1) * w]
    w = full.shape[1] // 4
    return full[:, s * w:(s + 1) * w, :]


def _full_of_shards(name, shards):
    axis = {"w_in": 2, "w_mlp_in": 2, "conv_w": 3}.get(name, 1)
    return jnp.concatenate(shards, axis=axis)


def _to_proj_layout(w):
    z, xbc, dt, q, k, v, gates = (w[..., 0:2048], w[..., 2048:6144], w[..., 6144:6208], w[..., 6208:7232],
                                  w[..., 7232:7488], w[..., 7488:7744], w[..., 7744:9792])
    pad = jnp.zeros(w.shape[:-1] + (N_MAIN - OFF_DT - dt.shape[-1],), w.dtype)

    def doubled(a):
        h = a.reshape(a.shape[:-1] + (ATTN_KV, 1, ATTN_DIM))
        return jnp.broadcast_to(h, a.shape[:-1] + (ATTN_KV, 2, ATTN_DIM)).reshape(a.shape[:-1] + (2 * a.shape[-1],))

    return jnp.concatenate([z, gates, xbc, dt, pad, q, doubled(k), doubled(v)], axis=-1)


def _from_proj_layout(w):
    z, gates, xbc, dt, q, k2, v2 = (w[..., 0:2048], w[..., 2048:4096], w[..., 4096:8192], w[..., 8192:8256],
                                    w[..., 8320:9344], w[..., 9344:9856], w[..., 9856:10368])

    def folded(a):
        return a.reshape(a.shape[:-1] + (ATTN_KV, 2, ATTN_DIM)).sum(axis=-2).reshape(a.shape[:-1] + (a.shape[-1] // 2,))

    return jnp.concatenate([z, xbc, dt, q, folded(k2), folded(v2), gates], axis=-1)


def _layer_fwd(h1, x, W, P, l, bias4):
    t = x.shape[0]
    S = {"x": x, "h1": h1}
    proj = _matmul("proj", h1, W["w_in_main"][l], "nn")
    qkv = _matmul("proj_qkv", h1, W["w_in_qkv"][l], "nn", out_dtype=BF16)
    S["proj"] = proj
    pre, act = _conv_fwd(proj, W["conv_w"][l], P["conv_b"][l].reshape(1, CONV_DIM))
    S["pre"], S["act"] = pre, act

    dtb = jnp.pad(P["dt_bias"][l].reshape(1, 2 * SSD_HEADS), ((0, 0), (0, LANES - 2 * SSD_HEADS)))

    def dt_fn(raw, b):
        v = raw + b
        dt = jnp.maximum(v, 0.0) + jnp.log1p(jnp.exp(-jnp.abs(v)))
        expand = (jnp.right_shift(lax.broadcasted_iota(jnp.int32, (LANES, 2 * D_INNER), 1), 6)
                  == lax.broadcasted_iota(jnp.int32, (LANES, 2 * D_INNER), 0)).astype(BF16)
        hi = dt.astype(BF16)
        rest = dt - hi.astype(F32)
        mid = rest.astype(BF16)
        lo = (rest - mid.astype(F32)).astype(BF16)
        return [dt, _dot(hi, expand) + _dot(mid, expand) + _dot(lo, expand)], []

    dt, dte = _rowwise("dt_fwd", dt_fn, [_row(proj, LANES, OFF_DT // LANES)], [dtb], [(LANES, F32), (2 * D_INNER, F32)], tb=512)
    dt2 = jnp.stack([dt[:, 0:SSD_HEADS], dt[:, SSD_HEADS:2 * SSD_HEADS]])
    dt2t = dt2.transpose(0, 2, 1)
    a = -jnp.exp(P["a_log"][l])
    a_row, a_col = a.reshape(2, 1, SSD_HEADS), a.reshape(2, SSD_HEADS, 1)
    S["dt2"], S["dt2t"], S["a_row"], S["a_col"], S["dte"] = dt2, dt2t, a_row, a_col, dte
    y2, states = _ssd_fwd(act, dt2, dt2t, a_row, a_col, dte)
    S["states"] = states

    dsk = jnp.repeat(P["d_skip"][l], SSD_HEAD_DIM).reshape(1, D_INNER)
    nw = P["ssd_norm"][l].reshape(1, D_INNER)
    S["dsk"], S["nw"] = dsk, nw

    def gn_fn(yf, yb, xs, z, dsk, nw):
        y = yf + yb + xs * dsk
        return [y, _gated_norm_fwd(y, z, nw)], []

    y, yn = _rowwise("gated_norm_fwd", gn_fn,
                     [_row(y2, lead=0), _row(y2, lead=1), _row(act, D_INNER, 0), _row(proj, D_INNER, OFF_Z // D_INNER)],
                     [dsk, nw], [(D_INNER, F32), (D_INNER, BF16)])
    S["y"], S["yn"] = y, yn
    if "late" in W:
        W.update(W.pop("late")(yn))
    y_ssd = _matmul("ssd_out", yn, W["w_ssd_out"][l], "nn")
    S["y_ssd"] = y_ssd

    sink4 = jnp.broadcast_to(P["attn_sink"][l].reshape(ATTN_KV, REP, 1, 1), (ATTN_KV, REP, 1, LANES))
    S["qkv"], S["sink4"] = qkv, sink4
    o = _attn_fwd(qkv, bias4, sink4)
    S["o"] = o
    y_attn = _matmul("attn_out", o, W["w_attn_out"][l], "nn")
    S["y_attn"] = y_attn

    bg = P["b_gate"][l].reshape(1, 2 * D_MODEL)
    S["bg"] = bg

    def merge_fn(gates, ys, ya, b):
        g = jax.nn.sigmoid(gates + b)
        return [g[:, :D_MODEL] * ys + g[:, D_MODEL:] * ya], []

    (mix_in,) = _rowwise("merge_fwd", merge_fn, [_row(proj, 2 * D_MODEL, OFF_G // (2 * D_MODEL)), _row(y_ssd), _row(y_attn)],
                         [bg], [(D_MODEL, BF16)])
    S["mix_in"] = mix_in
    mixed = _matmul("w_o", mix_in, W["w_o"][l], "nn")
    S["mixed"] = mixed

    g_pm = P["post_mix_norm"][l].reshape(1, D_MODEL)
    g_pl = P["pre_mlp_norm"][l].reshape(1, D_MODEL)

    def postmix_fn(x, mixed, g1, g2):
        x2 = x + _rms_fwd(mixed, g1)
        return [x2, _rms_fwd(x2, g2)], []

    x2, h2 = _rowwise("post_mix_fwd", postmix_fn, [_row(x), _row(mixed)], [g_pm, g_pl], [(D_MODEL, F32), (D_MODEL, BF16)])
    S["x2"], S["h2"] = x2, h2
    a1 = _matmul("mlp_in", h2, W["w_mlp_in"][l], "nn", out_dtype=BF16,
                 epilogue=lambda acc: jnp.square(jnp.maximum(acc, 0.0)))
    S["a1"] = a1
    f2 = _matmul("mlp_out", a1, W["w_mlp_out"][l], "nn")
    S["f2"] = f2
    return S


def _layer_bwd(S, dx3, W, P, l, bias4, onehot_t, zero=None):
    t = dx3.shape[0]
    G = {}
    g_pmlp = P["post_mlp_norm"][l].reshape(1, D_MODEL)
    if zero is not None:
        g_pmlp = g_pmlp + zero

    def b1_fn(f2, dx3, g):
        df2, dg = _rms_bwd(f2, g, dx3)
        return [df2], [dg]

    df2, G["post_mlp_norm"] = _rowwise("post_mlp_bwd", b1_fn, [_row(S["f2"]), _row(dx3)], [g_pmlp], [(D_MODEL, BF16)], [(1, D_MODEL)])
    df1 = _matmul("d_f1", df2, W["w_mlp_out"][l], "nt", out_dtype=BF16,
                  epilogue=lambda acc, a1: acc * (2.0 * jnp.sqrt(a1.astype(F32))), extras=[S["a1"]])
    G["w_mlp_out"] = _matmul("dw_mlp_out", S["a1"], df2, "tn")
    dh2 = _matmul("d_h2", df1, W["w_mlp_in"][l], "nt")
    G["w_mlp_in"] = _matmul("dw_mlp_in", S["h2"], df1, "tn")

    g_pm = P["post_mix_norm"][l].reshape(1, D_MODEL)
    g_pl = P["pre_mlp_norm"][l].reshape(1, D_MODEL)

    def b3_fn(x2, dh2, dx3, mixed, g_pl, g_pm):
        d1, dgl = _rms_bwd(x2, g_pl, dh2)
        dx2 = dx3 + d1
        dmixed, dgm = _rms_bwd(mixed, g_pm, dx2)
        return [dx2, dmixed], [dgl, dgm]

    dx2, dmixed, G["pre_mlp_norm"], G["post_mix_norm"] = _rowwise(
        "post_mix_bwd", b3_fn, [_row(S["x2"]), _row(dh2), _row(dx3), _row(S["mixed"])], [g_pl, g_pm],
        [(D_MODEL, F32), (D_MODEL, BF16)], [(1, D_MODEL), (1, D_MODEL)])
    dmix_in = _matmul("d_mix_in", dmixed, W["w_o"][l], "nt")
    G["w_o"] = _matmul("dw_o", S["mix_in"], dmixed, "tn")

    proj = S["proj"]

    def b4_fn(gates, ys, ya, dmix, b):
        g = jax.nn.sigmoid(gates + b)
        gs, ga = g[:, :D_MODEL], g[:, D_MODEL:]
        dg = jnp.concatenate([ys * dmix, ya * dmix], axis=-1) * g * (1.0 - g)
        return [gs * dmix, ga * dmix, dg], [jnp.sum(dg, axis=0, keepdims=True)]

    dy_ssd, dy_attn, dgates, G["b_gate"] = _rowwise(
        "merge_bwd", b4_fn, [_row(proj, 2 * D_MODEL, OFF_G // (2 * D_MODEL)), _row(S["y_ssd"]), _row(S["y_attn"]), _row(dmix_in)],
        [S["bg"]], [(D_MODEL, BF16), (D_MODEL, BF16), (2 * D_MODEL, BF16)], [(1, 2 * D_MODEL)])

    dyn = _matmul("d_yn", dy_ssd, W["w_ssd_out"][l], "nt")
    G["w_ssd_out"] = _matmul("dw_ssd_out", S["yn"], dy_ssd, "tn")
    do = _matmul("d_o", dy_attn, W["w_attn_out"][l], "nt", out_dtype=BF16)
    G["w_attn_out"] = _matmul("dw_attn_out", S["o"], dy_attn, "tn")

    dq, dkp, dvp, dbias4, dsink4 = _attn_bwd(S["qkv"], bias4, S["sink4"], do)
    dk = _kv_combine("dk_combine", dkp)
    dv = _kv_combine("dv_combine", dvp)
    G["attn_sink"] = dsink4[:, :, 0, 0].reshape(ATTN_HEADS)
    G["rel_bias_table"] = _bias_table_grad(dbias4.reshape(ATTN_HEADS, BLOCK * 3 * BLOCK), onehot_t).T

    act = S["act"]

    def b5_fn(y, z, xs, dyn, nw, dsk):
        dy, dz, dnw = _gated_norm_bwd(y, z, nw, dyn)
        return [dy, dz], [dnw, jnp.sum(dy * xs, axis=0, keepdims=True)]

    dy, dz, G["ssd_norm"], dskip_cols = _rowwise(
        "gated_norm_bwd", b5_fn, [_row(S["y"]), _row(proj, D_INNER, OFF_Z // D_INNER), _row(act, D_INNER, 0), _row(dyn)],
        [S["nw"], S["dsk"]], [(D_INNER, F32), (D_INNER, BF16)], [(1, D_INNER), (1, D_INNER)])
    G["d_skip"] = dskip_cols.reshape(SSD_HEADS, SSD_HEAD_DIM).sum(axis=-1)

    dxs2, dbs2, dcs2, ddt2, da2 = _ssd_bwd(act, S["dt2"], S["dt2t"], S["a_row"], S["a_col"], S["dte"], dy, S["states"])
    G["a_log"] = da2.reshape(2, SSD_HEADS) * S["a_row"].reshape(2, SSD_HEADS)

    def b6_fn(dxf, dxb, dy, dbf, dbb, dcf, dcb, pre, dsk):
        dact = jnp.concatenate([dxf + dxb + dy * dsk, dbf + dbb, dcf + dcb], axis=-1)
        return [dact * _silu_grad(pre)], []

    (dpre,) = _rowwise("silu_bwd", b6_fn,
                       [_row(dxs2, lead=0), _row(dxs2, lead=1), _row(dy), _row(dbs2, lead=0), _row(dbs2, lead=1),
                        _row(dcs2, lead=0), _row(dcs2, lead=1), _row(S["pre"])], [S["dsk"]], [(CONV_DIM, F32)], tb=128)
    du, dconv_w, dconv_b = _conv_bwd(dpre, proj, W["conv_w"][l])
    G["conv_w"] = dconv_w.reshape(SSD_CONV, 1, CONV_DIM)
    G["conv_b"] = dconv_b.reshape(CONV_DIM)

    dtb = jnp.pad(P["dt_bias"][l].reshape(1, 2 * SSD_HEADS), ((0, 0), (0, LANES - 2 * SSD_HEADS)))
    ddt = jnp.pad(jnp.concatenate([ddt2[0], ddt2[1]], axis=-1), ((0, 0), (0, LANES - 2 * SSD_HEADS)))

    def b7_fn(raw, ddt, b):
        draw = ddt * jax.nn.sigmoid(raw + b)
        return [draw], [jnp.sum(draw, axis=0, keepdims=True)]

    draw, ddtb = _rowwise("dt_bwd", b7_fn, [_row(proj, LANES, OFF_DT // LANES), _row(ddt)], [dtb], [(LANES, BF16)], [(1, LANES)], tb=512)
    G["dt_bias"] = ddtb[0, :2 * SSD_HEADS].reshape(2, SSD_HEADS)

    dproj = jnp.concatenate([dz, dgates, du, draw, dq, dk, dv], axis=-1)
    G["w_in"] = _from_proj_layout(_matmul("dw_in", S["h1"], dproj, "tn"))
    dh1 = _matmul("d_h1", dproj, W["w_in"][l], "nt")

    g_pre = P["pre_mix_norm"][l].reshape(1, D_MODEL)

    def b8_fn(x, dh1, dx2, g):
        d1, dg = _rms_bwd(x, g, dh1)
        return [dx2 + d1], [dg]

    dx, G["pre_mix_norm"] = _rowwise("pre_mix_bwd", b8_fn, [_row(S["x"]), _row(dh1), _row(dx2)], [g_pre], [(D_MODEL, F32)], [(1, D_MODEL)])
    return dx, G


def _chip_sums(tag, pieces):
    theirs = _swap_cores("swap_" + tag, pieces)
    out = []
    for i, p in enumerate(pieces):
        _, _, r2, c = p.shape
        mine = lax.dynamic_index_in_dim(p, lax.axis_index("c"), axis=0, keepdims=False).reshape(4 * r2, c)
        (cs,) = _rowwise(f"presum_{tag}{i}", lambda a, b: ([a.astype(F32) + b.astype(F32)], []),
                         [_row(mine), _row(theirs[i].reshape(4 * r2, c))], [], [(c, BF16)])
        out.append(cs.reshape(4, r2, c))
    return out


def _halves(blocks):
    r2 = blocks[0].shape[0] // 2
    return jnp.stack([jnp.stack([b[h * r2:(h + 1) * r2] for b in blocks]) for h in range(2)])


def _layer_pieces(g, like):
    cols = like["w_in"].shape[2]
    win = _halves([g["w_in"][:, s * cols:(s + 1) * cols].astype(BF16) for s in range(4)])
    packs = [_pack_rows([_shard_of_full(n, g[n][None], s)[0] for n in BIG], BIG_ROWS // 2, BF16) for s in range(4)]
    return [win, _halves(packs)]


def _unpack_layer(flat, like):
    out, r = {}, 0
    for n in BIG:
        shp = like[n].shape[1:]
        nr = math.prod(shp) // D_MODEL
        out[n] = flat[r:r + nr].reshape(shp)
        r += nr
    return out


def _step(x, target, shard_w, shard_m, shard_v):
    depth, _, win_cols = shard_w["w_in"].shape
    win_rows = depth * D_MODEL

    def win2d(a):
        return a.reshape(win_rows, win_cols)

    conv_rows = shard_w["conv_w"].reshape(-1, D_MODEL)
    win_own = win2d(shard_w["w_in"]).astype(BF16)
    conv_own = jnp.pad(conv_rows, ((0, 16 - conv_rows.shape[0]), (0, 0)))
    win_g, conv_g = _allgather_chips("gather_w_in", [win_own, conv_own])
    packed = _pack_big(shard_w, BF16)
    send_sems, recv_sems, packed_thru, land_thru, token = _send_start("gather_rest_start", [packed], conv_g, False, 0)

    def late(after):
        (landed,) = _send_wait("gather_rest_wait", send_sems, recv_sems, packed_thru, land_thru, after, False)
        per_chip = [_unpack_big(jnp.where(me == s, packed, landed[s]), shard_w) for s in range(4)]
        return {n: _full_of_shards(n, [per_chip[s][n] for s in range(4)]) for n in BIG if n != "conv_w"}

    W = {"late": late}
    me = 2 * lax.axis_index("x") + lax.axis_index("y")
    win_g = [jnp.where(me == s, win_own, win_g[s]) for s in range(4)]
    conv_g = [jnp.where(me == s, conv_own, conv_g[s]) for s in range(4)]
    W["w_in"] = _to_proj_layout(jnp.concatenate([win_g[s].reshape(depth, D_MODEL, win_cols) for s in range(4)], axis=2))
    W["conv_w"] = jnp.concatenate([conv_g[s][:conv_rows.shape[0]].reshape(shard_w["conv_w"].shape) for s in range(4)],
                                  axis=3).reshape(depth, SSD_CONV, CONV_DIM)
    P = {n: shard_w[n] for n in SMALL}
    P["pre_mix_norm"] = P["pre_mix_norm"] + token[0, 0]
    assert depth == 2
    sent = {}

    def grads_ready(g):
        sent["sums"] = _chip_sums("l1_", _layer_pieces(g, shard_w))
        sent["sems"] = _send_start("scatter_l1_start", sent["sums"], sent["sums"][0], True, 1)
        return sent["sems"][4][0, 0]

    loss_part, grad_x, full = _local_step(x, target, W, P, grads_ready)
    send1, recv1, thru1, land1, _ = sent["sems"]
    landed1 = _send_wait("scatter_l1_wait", send1, recv1, thru1, land1, grad_x, True)
    sums0 = _chip_sums("l0_", _layer_pieces({n: full[n][0] for n in BIG + ("w_in",)}, shard_w))
    landed0 = _scatter_chips("scatter_grads", sums0)
    staged = [jnp.stack([jnp.where(me == s, own[s], got[s]) for s in range(4)])
              for own, got in zip(sums0 + sent["sums"], list(landed0) + landed1)]
    halves = [_sum_slots(f"sum_grads{i}", st) for i, st in enumerate(staged)]
    core = lax.axis_index("c")
    win0, big0, win1, big1 = [jnp.where(core == 0, jnp.concatenate([mine, other]), jnp.concatenate([other, mine]))
                              for mine, other in zip(halves, _swap_cores("share_grads", halves, slab=False))]
    g_win = jnp.concatenate([win0, win1], axis=0)
    per_layer = [_unpack_layer(big0, shard_w), _unpack_layer(big1, shard_w)]
    g_big = _pack_big({n: jnp.stack([per_layer[0][n], per_layer[1][n]]) for n in BIG}, F32)
    d_win, m_win, v_win = _adamw("adamw_w_in", win2d(shard_w["w_in"]), g_win, win2d(shard_m["w_in"]), win2d(shard_v["w_in"]))
    d_big, m_big, v_big = _adamw("adamw_big", _pack_big(shard_w, F32), g_big, _pack_big(shard_m, F32), _pack_big(shard_v, F32))

    small = _allgather_all("gather_small", _pack_small(full, loss_part))
    g_small = _sum_slots("sum_small", small, tb=SMALL_ROWS)
    d_small, m_small, v_small = _adamw("adamw_small", _pack_small(shard_w, jnp.zeros((), F32)), g_small,
                                       _pack_small(shard_m, jnp.zeros((), F32)), _pack_small(shard_v, jnp.zeros((), F32)))

    outs = {}
    for tag, win, big, sm in (("grad", g_win, g_big, g_small), ("delta", d_win, d_big, d_small),
                              ("new_m", m_win, m_big, m_small), ("new_v", v_win, v_big, v_small)):
        ub = _unpack_big(big, shard_w)
        us, extra = _unpack_small(sm, shard_w)
        outs[tag] = {"w_in": win.reshape(shard_w["w_in"].shape), **ub, **us}
        if tag == "grad":
            loss = extra
    return loss, grad_x, outs


def _local_step(x, target, W, P, grads_ready=None):
    depth = W["w_in"].shape[0]
    W = dict(W, w_in_main=W["w_in"][:, :, :N_MAIN], w_in_qkv=W["w_in"][:, :, N_MAIN:])
    onehot_t = (_bucket_map().reshape(1, -1) == jnp.arange(N_BUCKETS)[:, None]).astype(F32)
    bias = _bias_from_table(P["rel_bias_table"], onehot_t).reshape(ATTN_HEADS, BLOCK, 3 * BLOCK)
    bias4 = jnp.where(_band_mask(), bias, NEG).reshape(ATTN_KV, REP, BLOCK, 3 * BLOCK)

    def pre_fn(x, g):
        return [_rms_fwd(x, g)], []

    (h1,) = _rowwise("pre_mix_fwd", pre_fn, [_row(x)], [P["pre_mix_norm"][0].reshape(1, D_MODEL)], [(D_MODEL, BF16)])
    saved = []
    loss_cols = dxl = None
    for l in range(depth):
        S = _layer_fwd(h1, x, W, P, l, bias4)
        saved.append(S)
        g_pmlp = P["post_mlp_norm"][l].reshape(1, D_MODEL)
        if l + 1 < depth:
            def post_fn(x2, f2, g1, g2):
                x3 = x2 + _rms_fwd(f2, g1)
                return [x3, _rms_fwd(x3, g2)], []

            x, h1 = _rowwise("post_mlp_fwd", post_fn, [_row(S["x2"]), _row(S["f2"])],
                             [g_pmlp, P["pre_mix_norm"][l + 1].reshape(1, D_MODEL)], [(D_MODEL, F32), (D_MODEL, BF16)])
        else:
            def loss_fn(x2, f2, tgt, g1):
                diff = x2 + _rms_fwd(f2, g1) - tgt
                return [diff * (1.0 / D_MODEL)], [jnp.sum(diff * diff, axis=0, keepdims=True)]

            dxl, loss_cols = _rowwise("loss", loss_fn, [_row(S["x2"]), _row(S["f2"]), _row(target)], [g_pmlp],
                                      [(D_MODEL, F32)], [(1, D_MODEL)])
    loss_part = 0.5 * jnp.sum(loss_cols) / D_MODEL

    grads = [None] * depth
    dx = dxl
    zero = None
    for l in reversed(range(depth)):
        dx, grads[l] = _layer_bwd(saved[l], dx, W, P, l, bias4, onehot_t, zero)
        if grads_ready is not None and l == depth - 1 and depth > 1:
            zero = grads_ready(grads[l])
    grad_x = dx

    full = {n: jnp.stack([grads[l][n] for l in range(depth)]) for n in ALL_W if n != "rel_bias_table"}
    full["rel_bias_table"] = sum(grads[l]["rel_bias_table"] for l in range(depth))
    return loss_part, grad_x, full


def kernel(x, pre_mix_norm, w_in, b_gate, conv_w, conv_b, dt_bias, a_log, d_skip, ssd_norm, w_ssd_out, attn_sink, rel_bias_table, w_attn_out, w_o, post_mix_norm, pre_mlp_norm, w_mlp_in, w_mlp_out, post_mlp_norm, loss_target, m_pre_mix_norm, m_w_in, m_b_gate, m_conv_w, m_conv_b, m_dt_bias, m_a_log, m_d_skip, m_ssd_norm, m_w_ssd_out, m_attn_sink, m_rel_bias_table, m_w_attn_out, m_w_o, m_post_mix_norm, m_pre_mlp_norm, m_w_mlp_in, m_w_mlp_out, m_post_mlp_norm, v_pre_mix_norm, v_w_in, v_b_gate, v_conv_w, v_conv_b, v_dt_bias, v_a_log, v_d_skip, v_ssd_norm, v_w_ssd_out, v_attn_sink, v_rel_bias_table, v_w_attn_out, v_w_o, v_post_mix_norm, v_pre_mlp_norm, v_w_mlp_in, v_w_mlp_out, v_post_mlp_norm):
    a = locals()
    shard_w = {n: a[n] for n in ALL_W}
    shard_m = {n: a["m_" + n] for n in ALL_W}
    shard_v = {n: a["v_" + n] for n in ALL_W}
    loss, grad_x, outs = _step(x[0], loss_target[0], shard_w, shard_m, shard_v)
    return (loss, grad_x[None], *[outs["grad"][n] for n in ALL_W], *[outs["delta"][n] for n in ALL_W],
            *[outs["new_m"][n] for n in ALL_W], *[outs["new_v"][n] for n in ALL_W])
```
